```python
import math
import numpy as np
import jax
import jax.numpy as jnp
from jax import lax

D_MODEL = 1024
BATCH = 32
SEQ = 2048
DEPTH = 2

MIX_W = D_MODEL // 4
N_BRANCH = 4
POOL_GROUPS = 4
POOL_WINDOWS = (2, 4, 8, 16)
POOL_GW = MIX_W // POOL_GROUPS
GLA_HEADS = 4
GLA_DK = MIX_W // (2 * GLA_HEADS)
GLA_DV = MIX_W // GLA_HEADS
GLA_LOWRANK = 16
GLA_GATE_NORM = 16.0
GLA_CHUNK = 64
GDN_HEADS = 4
GDN_DH = MIX_W // GDN_HEADS
GDN_CONV = 4
GDN_CHUNK = 64
NSA_HEADS = 4
NSA_GROUPS = 2
NSA_HPG = NSA_HEADS // NSA_GROUPS
NSA_DH = MIX_W // NSA_HEADS
NSA_KV = NSA_GROUPS * NSA_DH
NSA_CMP_LEN = 32
NSA_CMP_STRIDE = 16
NSA_SEL_LEN = 64
NSA_N_SEL = 16
NSA_WINDOW = 512
NSA_SEL_QBLOCK = 16
NSA_WIN_QBLOCK = 128
MEM_LEN = 256
X_HEADS = 4
X_DH = 128
D_FF = 2816
FFN_CONV = 3
EPS = 1e-6
IN_SPLITS = (MIX_W,
             GLA_HEADS * GLA_DK, GLA_HEADS * GLA_DK, GLA_HEADS * GLA_DV, GLA_HEADS * GLA_DV, GLA_LOWRANK,
             MIX_W, MIX_W, MIX_W, GDN_HEADS, GDN_HEADS, MIX_W,
             NSA_HEADS * NSA_DH, NSA_KV, NSA_KV, NSA_KV, NSA_KV, NSA_KV, NSA_KV, 3 * NSA_HEADS)
N_IN = sum(IN_SPLITS)

kernel_name = 'hybrid_pool_gla_gdn_nsa_block'


def rmsnorm(x, g):
    xf = x.astype(jnp.float32)
    y = xf * lax.rsqrt(jnp.mean(xf * xf, axis=-1, keepdims=True) + EPS)
    return (y * g.astype(jnp.float32)).astype(x.dtype)


def causal_dwconv(x, w):
    width = w.shape[0]
    s = x.shape[1]
    xp = jnp.pad(x, ((0, 0), (width - 1, 0), (0, 0)))
    return sum(xp[:, i:i + s] * w[i] for i in range(width))


def masked_softmax(s, mask):
    s = jnp.where(mask, s.astype(jnp.float32), -jnp.inf)
    m = jnp.max(s, axis=-1, keepdims=True)
    m = jnp.where(jnp.isfinite(m), m, 0.0)
    e = jnp.exp(s - m)
    den = jnp.sum(e, axis=-1, keepdims=True)
    return e / jnp.where(den > 0, den, 1.0)


def _chunk(t, c):
    b, s, h = t.shape[:3]
    t = t.reshape((b, s // c, c, h) + t.shape[3:])
    return jnp.moveaxis(t, 3, 1)


def _unchunk(t):
    b, h, n, c, d = t.shape
    return jnp.moveaxis(t, 1, 3).reshape(b, n * c, h, d)


def pool_mixer(u, w_grp, scale):
    b, s, _ = u.shape
    uf = u.astype(jnp.float32).reshape(b, s, POOL_GROUPS, POOL_GW)
    cs = jnp.pad(jnp.cumsum(uf, axis=1), ((0, 0), (1, 0), (0, 0), (0, 0)))
    t = jnp.arange(s)
    pooled = []
    for gi, w in enumerate(POOL_WINDOWS):
        lo = jnp.maximum(t + 1 - w, 0)
        cnt = (t + 1 - lo).astype(jnp.float32)
        csg = cs[:, :, gi]
        pooled.append((csg[:, t + 1] - csg[:, lo]) / cnt[None, :, None])
    pooled = jnp.stack(pooled, axis=2)
    mixed = jnp.einsum('bsgc,gcd->bsgd', pooled - uf, w_grp.astype(jnp.float32))
    return (mixed.reshape(b, s, MIX_W) * scale.astype(jnp.float32)).astype(u.dtype)


def gla_mixer(q, k, v, r, lr, w_lr, b_lr, g_norm):
    b, s, _ = q.shape
    h, dk, dv, c = GLA_HEADS, GLA_DK, GLA_DV, GLA_CHUNK
    f32 = jnp.float32
    q = q.astype(f32).reshape(b, s, h, dk) * dk ** -0.5
    k = k.astype(f32).reshape(b, s, h, dk)
    v = v.astype(f32).reshape(b, s, h, dv)
    gk = jax.nn.log_sigmoid(lr.astype(f32) @ w_lr.astype(f32) + b_lr.astype(f32)) / GLA_GATE_NORM
    gk = gk.reshape(b, s, h, dk)
    qc, kc, vc, gc = (_chunk(t, c) for t in (q, k, v, gk))
    bcum = jnp.cumsum(gc, axis=3)
    blast = bcum[:, :, :, -1:, :]
    q_e = qc * jnp.exp(bcum)
    k_e = kc * jnp.exp(-bcum)
    causal = jnp.tril(jnp.ones((c, c), bool))
    att = jnp.where(causal, jnp.einsum('bhnid,bhnjd->bhnij', q_e, k_e), 0.0)
    o_intra = jnp.einsum('bhnij,bhnjv->bhniv', att, vc)
    k_upd = kc * jnp.exp(blast - bcum)
    dec = jnp.exp(blast[:, :, :, 0, :])

    def step(st, inp):
        q_n, k_n, v_n, d_n = inp
        o_n = jnp.einsum('bhcd,bhdv->bhcv', q_n, st)
        st = st * d_n[..., None] + jnp.einsum('bhcd,bhcv->bhdv', k_n, v_n)
        return st, o_n

    st0 = jnp.zeros((b, h, dk, dv), f32)
    _, o_inter = lax.scan(step, st0, tuple(jnp.moveaxis(t, 2, 0) for t in (q_e, k_upd, vc, dec)))
    o = _unchunk(o_intra + jnp.moveaxis(o_inter, 0, 2))
    o = rmsnorm(o, g_norm) * jax.nn.silu(r.astype(f32).reshape(b, s, h, dv))
    return o.reshape(b, s, MIX_W).astype(r.dtype)


def gdn_mixer(q, k, v, beta_raw, a_raw, gate, conv_w, a_log, dt_bias, g_norm):
    b, s, _ = q.shape
    h, dh, c = GDN_HEADS, GDN_DH, GDN_CHUNK
    f32 = jnp.float32
    qkv = jnp.concatenate([q, k, v], axis=-1).astype(f32)
    qkv = jax.nn.silu(causal_dwconv(qkv, conv_w.astype(f32)))
    q, k, v = (t.reshape(b, s, h, dh) for t in jnp.split(qkv, 3, axis=-1))
    q = q * lax.rsqrt(jnp.sum(q * q, axis=-1, keepdims=True) + EPS) * dh ** -0.5
    k = k * lax.rsqrt(jnp.sum(k * k, axis=-1, keepdims=True) + EPS)
    beta = jax.nn.sigmoid(beta_raw.astype(f32))
    g = -jnp.exp(a_log.astype(f32)) * jax.nn.softplus(a_raw.astype(f32) + dt_bias.astype(f32))
    qc, kc, vc, bc, gc = (_chunk(t, c) for t in (q, k, v, beta, g))
    gcum = jnp.cumsum(gc, axis=-1)
    incl = jnp.tril(jnp.ones((c, c), bool))
    strict = jnp.tril(jnp.ones((c, c), bool), -1)
    decay = jnp.exp(jnp.where(incl, gcum[..., :, None] - gcum[..., None, :], -jnp.inf))
    kb = kc * bc[..., None]
    vb = vc * bc[..., None]
    a_mat = jnp.where(strict, jnp.einsum('bhnid,bhnjd->bhnij', kb, kc) * decay, 0.0) + jnp.eye(c, dtype=f32)
    u = lax.linalg.triangular_solve(a_mat, vb, left_side=True, lower=True, unit_diagonal=True)
    w = lax.linalg.triangular_solve(a_mat, kb * jnp.exp(gcum)[..., None], left_side=True, lower=True,
                                    unit_diagonal=True)
    a_qk = jnp.einsum('bhnid,bhnjd->bhnij', qc, kc) * decay
    q_dec = qc * jnp.exp(gcum)[..., None]
    k_dec = kc * jnp.exp(gcum[..., -1:] - gcum)[..., None]
    dec = jnp.exp(gcum[..., -1])

    def step(st, inp):
        u_n, w_n, qd_n, aqk_n, kd_n, d_n = inp
        v_new = u_n - jnp.einsum('bhcd,bhdv->bhcv', w_n, st)
        o_n = jnp.einsum('bhcd,bhdv->bhcv', qd_n, st) + jnp.einsum('bhij,bhjv->bhiv', aqk_n, v_new)
        st = st * d_n[..., None, None] + jnp.einsum('bhcd,bhcv->bhdv', kd_n, v_new)
        return st, o_n

    st0 = jnp.zeros((b, h, dh, dh), f32)
    _, o = lax.scan(step, st0, tuple(jnp.moveaxis(t, 2, 0) for t in (u, w, q_dec, a_qk, k_dec, dec)))
    o = _unchunk(jnp.moveaxis(o, 0, 2))
    o = rmsnorm(o, g_norm) * jax.nn.silu(gate.astype(f32).reshape(b, s, h, dh))
    return o.reshape(b, s, MIX_W).astype(gate.dtype)


def _compress(t, pe, w1, w2):
    b, s, g, dh = t.shape
    n_sub = NSA_CMP_LEN // NSA_CMP_STRIDE
    n_chunks = s // NSA_CMP_STRIDE
    n_cmp = n_chunks - n_sub + 1
    tc = t.reshape(b, n_chunks, NSA_CMP_STRIDE, g, dh)
    blk = jnp.concatenate([tc[:, i:i + n_cmp] for i in range(n_sub)], axis=2)
    blk = blk + pe[None, None, :, None, :]
    flat = jnp.moveaxis(blk, 3, 2).reshape(b, n_cmp, g, NSA_CMP_LEN * dh)
    return jax.nn.gelu(flat @ w1) @ w2


def nsa_mixer(q, kc, vc, ks, vs, kw, vw, gate_raw, pe, cmp_w1, cmp_w2):
    b, s, _ = q.shape
    g, j, dh = NSA_GROUPS, NSA_HPG, NSA_DH
    f32 = jnp.float32
    scale = dh ** -0.5
    q = q.astype(f32).reshape(b, s, g, j, dh)
    kc, vc, ks, vs, kw, vw = (t.astype(f32).reshape(b, s, g, dh) for t in (kc, vc, ks, vs, kw, vw))
    t_pos = jnp.arange(s)

    ck = _compress(kc, pe[0], cmp_w1[0], cmp_w2[0])
    cv = _compress(vc, pe[1], cmp_w1[1], cmp_w2[1])
    n_cmp = ck.shape[1]
    cmp_end = jnp.arange(n_cmp) * NSA_CMP_STRIDE + NSA_CMP_LEN - 1
    p_cmp = masked_softmax(jnp.einsum('bsgjd,bngd->bgjsn', q, ck) * scale,
                           cmp_end[None, :] <= t_pos[:, None])
    o_cmp = jnp.einsum('bgjsn,bngd->bsgjd', p_cmp, cv)

    n_slc = s // NSA_SEL_LEN
    c_start = np.arange(n_cmp) * NSA_CMP_STRIDE
    s_start = np.arange(n_slc) * NSA_SEL_LEN
    cover = ((c_start[:, None] <= s_start[None, :] + NSA_SEL_LEN - 1)
             & (c_start[:, None] + NSA_CMP_LEN - 1 >= s_start[None, :])).astype(np.float32)
    imp = jnp.einsum('bgjsn,nm->bgsm', p_cmp, jnp.asarray(cover))
    blk = jnp.arange(n_slc)[None, :]
    cur = (t_pos // NSA_SEL_LEN)[:, None]
    forced = (blk == 0) | (blk == cur) | (blk == cur - 1)
    imp = jnp.where(forced, jnp.inf, jnp.where(blk > cur, -jnp.inf, imp))
    n_top = min(NSA_N_SEL, n_slc)
    _, idx = lax.top_k(imp, n_top)

    qb_len = NSA_SEL_QBLOCK
    nqb = s // qb_len
    ks_b = jnp.moveaxis(ks.reshape(b, n_slc, NSA_SEL_LEN, g, dh), 3, 1)
    vs_b = jnp.moveaxis(vs.reshape(b, n_slc, NSA_SEL_LEN, g, dh), 3, 1)
    gather = jax.vmap(jax.vmap(lambda tab, ii: tab[ii]))
    q_blocks = jnp.moveaxis(q.reshape(b, nqb, qb_len, g, j, dh), 1, 0)
    idx_blocks = jnp.moveaxis(idx.reshape(b, g, nqb, qb_len, n_top), 2, 0)
    pos_blocks = t_pos.reshape(nqb, qb_len)
    within = jnp.arange(NSA_SEL_LEN)

    def sel_block(args):
        qb, ib, pb = args
        kg = gather(ks_b, ib)
        vg = gather(vs_b, ib)
        kpos = ib[..., None] * NSA_SEL_LEN + within
        valid = (kpos <= pb[None, None, :, None, None]).reshape(b, g, 1, qb_len, n_top * NSA_SEL_LEN)
        sc = jnp.einsum('bqgjd,bgqkld->bgjqkl', qb, kg) * scale
        p = masked_softmax(sc.reshape(b, g, j, qb_len, n_top * NSA_SEL_LEN), valid).reshape(sc.shape)
        return jnp.einsum('bgjqkl,bgqkld->bqgjd', p, vg)

    o_slc = lax.map(sel_block, (q_blocks, idx_blocks, pos_blocks))
    o_slc = jnp.moveaxis(o_slc, 0, 1).reshape(b, s, g, j, dh)

    qw_len = NSA_WIN_QBLOCK
    span = NSA_WINDOW + qw_len
    nqw = s // qw_len
    kw_p = jnp.pad(kw, ((0, 0), (NSA_WINDOW, 0), (0, 0), (0, 0)))
    vw_p = jnp.pad(vw, ((0, 0), (NSA_WINDOW, 0), (0, 0), (0, 0)))
    qw_blocks = jnp.moveaxis(q.reshape(b, nqw, qw_len, g, j, dh), 1, 0)
    starts = jnp.arange(nqw) * qw_len

    def win_block(args):
        qb, st = args
        kb = lax.dynamic_slice_in_dim(kw_p, st, span, axis=1)
        vb = lax.dynamic_slice_in_dim(vw_p, st, span, axis=1)
        qp = st + jnp.arange(qw_len)
        kp = st - NSA_WINDOW + jnp.arange(span)
        valid = (kp[None, :] <= qp[:, None]) & (kp[None, :] > qp[:, None] - NSA_WINDOW) & (kp[None, :] >= 0)
        p = masked_softmax(jnp.einsum('bqgjd,bkgd->bgjqk', qb, kb) * scale, valid)
        return jnp.einsum('bgjqk,bkgd->bqgjd', p, vb)

    o_win = lax.map(win_block, (qw_blocks, starts))
    o_win = jnp.moveaxis(o_win, 0, 1).reshape(b, s, g, j, dh)

    gt = jax.nn.sigmoid(gate_raw.astype(f32)).reshape(b, s, g, j, 3)
    o = gt[..., 0:1] * o_cmp + gt[..., 1:2] * o_slc + gt[..., 2:3] * o_win
    return o.reshape(b, s, MIX_W).astype(gate_raw.dtype)


def mem_cross_attn(h, mem_n, w_q, w_kv, w_o):
    b, s, _ = h.shape
    m = mem_n.shape[1]
    q = (h @ w_q).reshape(b, s, X_HEADS, X_DH)
    kv = (mem_n @ w_kv).reshape(b, m, 2, X_HEADS, X_DH)
    sc = jnp.einsum('bshd,bmhd->bhsm', q, kv[:, :, 0]).astype(jnp.float32) * X_DH ** -0.5
    p = jax.nn.softmax(sc, axis=-1).astype(kv.dtype)
    o = jnp.einsum('bhsm,bmhd->bshd', p, kv[:, :, 1]).reshape(b, s, X_HEADS * X_DH)
    return o @ w_o


def conv_ffn(h, w_up, conv_w, conv_b, w_down):
    u, v = jnp.split(h @ w_up, 2, axis=-1)
    u = causal_dwconv(u, conv_w) + conv_b
    return (jax.nn.gelu(u) * v) @ w_down


def setup_inputs(seed: int = 0) -> dict:
    key = jax.random.key(seed)
    k = jax.random.split(key, 32)
    f32 = jnp.float32
    L = DEPTH

    def nrm(i, shape, fan_in):
        return jax.random.normal(k[i], shape, f32) * fan_in ** -0.5

    def gain(i, shape):
        return 1.0 + 0.02 * jax.random.normal(k[i], shape, f32)

    def small(i, shape):
        return 0.01 * jax.random.normal(k[i], shape, f32)

    dt = jnp.exp(jax.random.uniform(k[11], (L, GDN_HEADS), f32, math.log(1e-3), math.log(1e-1)))
    return {
        'x': jax.random.normal(k[0], (BATCH, SEQ, D_MODEL), f32),
        'mem': jax.random.normal(k[1], (BATCH, MEM_LEN, D_MODEL), f32),
        'g_mix': gain(2, (L, D_MODEL)),
        'w_in': nrm(3, (L, D_MODEL, N_IN), D_MODEL),
        'pool_w': nrm(4, (L, POOL_GROUPS, POOL_GW, POOL_GW), POOL_GW),
        'pool_scale': gain(5, (L, MIX_W)),
        'gla_w_lr': nrm(6, (L, GLA_LOWRANK, GLA_HEADS * GLA_DK), GLA_LOWRANK),
        'gla_b_lr': small(7, (L, GLA_HEADS * GLA_DK)),
        'gla_g_norm': gain(8, (L, GLA_DV)),
        'gdn_conv': nrm(9, (L, GDN_CONV, 3 * MIX_W), GDN_CONV),
        'gdn_a_log': jnp.log(jax.random.uniform(k[10], (L, GDN_HEADS), f32, 1.0, 16.0)),
        'gdn_dt_bias': dt + jnp.log(-jnp.expm1(-dt)),
        'gdn_g_norm': gain(12, (L, GDN_DH)),
        'nsa_pe': 0.1 * jax.random.normal(k[13], (L, 2, NSA_CMP_LEN, NSA_DH), f32),
        'nsa_cmp_w1': nrm(14, (L, 2, NSA_CMP_LEN * NSA_DH, NSA_DH), NSA_CMP_LEN * NSA_DH),
        'nsa_cmp_w2': nrm(15, (L, 2, NSA_DH, NSA_DH), NSA_DH),
        'w_branch': nrm(16, (L, N_BRANCH, MIX_W, D_MODEL), MIX_W),
        'w_gate': nrm(17, (L, N_BRANCH, D_MODEL, D_MODEL), D_MODEL),
        'b_gate': small(18, (L, N_BRANCH, D_MODEL)),
        'w_out': nrm(19, (L, D_MODEL, D_MODEL), D_MODEL),
        'g_cross': gain(20, (L, D_MODEL)),
        'g_mem': gain(21, (L, D_MODEL)),
        'w_xq': nrm(22, (L, D_MODEL, X_HEADS * X_DH), D_MODEL),
        'w_mem_kv': nrm(23, (L, D_MODEL, 2 * X_HEADS * X_DH), D_MODEL),
        'w_xo': nrm(24, (L, X_HEADS * X_DH, D_MODEL), X_HEADS * X_DH),
        'g_ffn': gain(25, (L, D_MODEL)),
        'w_up': nrm(26, (L, D_MODEL, 2 * D_FF), D_MODEL),
        'ffn_conv': nrm(27, (L, FFN_CONV, D_FF), FFN_CONV),
        'ffn_conv_b': small(28, (L, D_FF)),
        'w_down': nrm(29, (L, D_FF, D_MODEL), D_FF),
        'g_final': gain(30, (D_MODEL,)),
    }


def reference(x, mem, g_mix, w_in, pool_w, pool_scale, gla_w_lr, gla_b_lr, gla_g_norm, gdn_conv,
              gdn_a_log, gdn_dt_bias, gdn_g_norm, nsa_pe, nsa_cmp_w1, nsa_cmp_w2, w_branch, w_gate,
              b_gate, w_out, g_cross, g_mem, w_xq, w_mem_kv, w_xo, g_ffn, w_up, ffn_conv, ffn_conv_b,
              w_down, g_final):
    bounds = [int(v) for v in np.cumsum(IN_SPLITS)[:-1]]
    for l in range(DEPTH):
        h = rmsnorm(x, g_mix[l])
        z = h @ w_in[l]
        (p_in, a_q, a_k, a_v, a_r, a_lr, d_q, d_k, d_v, d_b, d_a, d_g,
         n_q, n_kc, n_vc, n_ks, n_vs, n_kw, n_vw, n_g) = jnp.split(z, bounds, axis=-1)
        branches = [
            pool_mixer(p_in, pool_w[l], pool_scale[l]),
            gla_mixer(a_q, a_k, a_v, a_r, a_lr, gla_w_lr[l], gla_b_lr[l], gla_g_norm[l]),
            gdn_mixer(d_q, d_k, d_v, d_b, d_a, d_g, gdn_conv[l], gdn_a_log[l], gdn_dt_bias[l], gdn_g_norm[l]),
            nsa_mixer(n_q, n_kc, n_vc, n_ks, n_vs, n_kw, n_vw, n_g, nsa_pe[l], nsa_cmp_w1[l], nsa_cmp_w2[l]),
        ]
        y = 0.0
        for i in range(N_BRANCH):
            gate = jax.nn.sigmoid(h @ w_gate[l, i] + b_gate[l, i])
            y = y + gate * (branches[i] @ w_branch[l, i])
        x = x + y @ w_out[l]
        x = x + mem_cross_attn(rmsnorm(x, g_cross[l]), rmsnorm(mem, g_mem[l]), w_xq[l], w_mem_kv[l], w_xo[l])
        x = x + conv_ffn(rmsnorm(x, g_ffn[l]), w_up[l], ffn_conv[l], ffn_conv_b[l], w_down[l])
    return rmsnorm(x, g_final)
```

```python
import functools

import numpy as np
import jax
import jax.numpy as jnp
from jax import lax
from jax.experimental import pallas as pl
from jax.experimental.pallas import tpu as pltpu

F32 = jnp.float32
BF16 = jnp.bfloat16
HIGHEST = lax.Precision.HIGHEST

D_MODEL = 1024
MIX_W = 256
POOL_WINDOWS = (2, 4, 8, 16)
POOL_GW = 64
GLA_HEADS = 4
GLA_DK = 32
GLA_DV = 64
GLA_LOWRANK = 16
GLA_GATE_NORM = 16.0
CHUNK = 64
GDN_HEADS = 4
GDN_DH = 64
GDN_CONV = 4
NSA_HEADS = 4
NSA_GROUPS = 2
NSA_HPG = 2
NSA_DH = 64
NSA_KV = 128
NSA_CMP_LEN = 32
NSA_CMP_STRIDE = 16
NSA_SEL_LEN = 64
NSA_N_SEL = 16
NSA_WINDOW = 512
X_HEADS = 4
X_DH = 128
D_FF = 2816
FFN_CONV = 3
EPS = 1e-6
N_BRANCH = 4

IN_SPLITS = (MIX_W,
             128, 128, 256, 256, GLA_LOWRANK,
             MIX_W, MIX_W, MIX_W, GDN_HEADS, GDN_HEADS, MIX_W,
             256, NSA_KV, NSA_KV, NSA_KV, NSA_KV, NSA_KV, NSA_KV, 3 * NSA_HEADS)
N_IN = sum(IN_SPLITS)

MISC_LR = 0
MISC_B = 16
MISC_A = 20
MISC_G = 24
LANE = 128
SUBLANE = 8

ZP_W, ZG_W, ZD_W, ZN_W, ZM_W = 256, 768, 1024, 1024, 128
Z_W = ZP_W + ZG_W + ZD_W + ZN_W + ZM_W

TM_PROJ = 256
TM_COMB = 256
TM_CROSS = 512
TS_FFN = 512
FF_CHUNK = 256
TQ_NSA = 256
N_SLC_PAD = 128
N_CMP_PAD = 128

VMEM_LIMIT = 56 * 1024 * 1024


def _cparams(sem):
    return pltpu.CompilerParams(dimension_semantics=sem, vmem_limit_bytes=VMEM_LIMIT)


def _rms(x, g):
    return x * lax.rsqrt(jnp.mean(x * x, axis=-1, keepdims=True) + EPS) * g


def _dot(a, b):
    return jnp.dot(a.astype(BF16), b.astype(BF16), preferred_element_type=F32)


def _dot_nt(a, b):
    return lax.dot_general(a.astype(BF16), b.astype(BF16), (((1,), (1,)), ((), ())),
                           preferred_element_type=F32)


def _dot_tn(a, b):
    return lax.dot_general(a.astype(BF16), b.astype(BF16), (((0,), (0,)), ((), ())),
                           preferred_element_type=F32)


def _dot_hi(a, b):
    return jnp.dot(a, b, precision=HIGHEST, preferred_element_type=F32)


def _dot_nt_hi(a, b):
    return lax.dot_general(a, b, (((1,), (1,)), ((), ())), precision=HIGHEST,
                           preferred_element_type=F32)


def _iota(shape, axis):
    return lax.broadcasted_iota(jnp.int32, shape, axis)


def _block_mask(rows, cols, rb, cb):
    return (_iota((rows, cols), 0) // rb) == (_iota((rows, cols), 1) // cb)


def _shift_rows(x, k):
    t = _iota(x.shape, 0)
    return jnp.where(t >= k, pltpu.roll(x, k, 0), 0.0)


def _cumsum_rows(x):
    k = 1
    while k < x.shape[0]:
        x = x + _shift_rows(x, k)
        k *= 2
    return x


def _softplus(x):
    return jnp.maximum(x, 0.0) + jnp.log1p(jnp.exp(-jnp.abs(x)))


def _log_sigmoid(x):
    return -_softplus(-x)


def _silu(x):
    return x * jax.nn.sigmoid(x)


def _masked_softmax(s, mask):
    s = jnp.where(mask, s, -jnp.inf)
    m = jnp.max(s, axis=-1, keepdims=True)
    m = jnp.where(jnp.isfinite(m), m, 0.0)
    e = jnp.exp(s - m)
    den = jnp.sum(e, axis=-1, keepdims=True)
    return e / jnp.where(den > 0, den, 1.0)


def _tile4(x):
    return jnp.concatenate([x, x, x, x], axis=0)


def _inproj_kernel(x_ref, g_ref, w_ref, zp_ref, zg_ref, zd_ref, zn_ref, zm_ref):
    h = _rms(x_ref[...], g_ref[...]).astype(BF16)
    off = 0
    for ref in (zp_ref, zg_ref, zd_ref, zn_ref, zm_ref):
        n = ref.shape[-1]
        ref[...] = jnp.dot(h, w_ref[:, off:off + n], preferred_element_type=F32)
        off += n


def _inproj(x2, g, w):
    t = x2.shape[0]
    widths = (ZP_W, ZG_W, ZD_W, ZN_W, ZM_W)
    return pl.pallas_call(
        _inproj_kernel,
        grid=(t // TM_PROJ,),
        in_specs=[pl.BlockSpec((TM_PROJ, D_MODEL), lambda i: (i, 0)),
                  pl.BlockSpec((1, D_MODEL), lambda i: (0, 0)),
                  pl.BlockSpec((D_MODEL, Z_W), lambda i: (0, 0))],
        out_specs=[pl.BlockSpec((TM_PROJ, n), lambda i: (i, 0)) for n in widths],
        out_shape=[jax.ShapeDtypeStruct((t, n), F32) for n in widths],
        compiler_params=_cparams(("parallel",)),
        name="inproj",
    )(x2, g, w)


def _pool_kernel(u_ref, w_ref, sc_ref, o_ref):
    u = u_ref[0]
    s2 = u + _shift_rows(u, 1)
    s4 = s2 + _shift_rows(s2, 2)
    s8 = s4 + _shift_rows(s4, 4)
    s16 = s8 + _shift_rows(s8, 8)
    grp = _iota(u.shape, 1) // POOL_GW
    win = jnp.where(grp == 0, s2, jnp.where(grp == 1, s4, jnp.where(grp == 2, s8, s16)))
    width = jnp.where(grp == 0, POOL_WINDOWS[0],
                      jnp.where(grp == 1, POOL_WINDOWS[1],
                                jnp.where(grp == 2, POOL_WINDOWS[2], POOL_WINDOWS[3])))
    cnt = jnp.minimum(_iota(u.shape, 0) + 1, width).astype(F32)
    diff = win / cnt - u
    o_ref[0] = (_dot(diff, w_ref[...]) * sc_ref[...]).astype(o_ref.dtype)


def _pool(zp, w_bd, scale):
    b, s, _ = zp.shape
    return pl.pallas_call(
        _pool_kernel,
        grid=(b,),
        in_specs=[pl.BlockSpec((1, s, MIX_W), lambda i: (i, 0, 0)),
                  pl.BlockSpec((MIX_W, MIX_W), lambda i: (0, 0)),
                  pl.BlockSpec((1, MIX_W), lambda i: (0, 0))],
        out_specs=pl.BlockSpec((1, s, MIX_W), lambda i: (i, 0, 0)),
        out_shape=jax.ShapeDtypeStruct((b, s, MIX_W), BF16),
        compiler_params=_cparams(("parallel",)),
        name="pool",
    )(zp, w_bd, scale)


def _gla_kernel(zg_ref, zm_ref, wlr_ref, blr_ref, gn_ref, ones_ref, o_ref, st_ref):
    st_ref[...] = jnp.zeros_like(st_ref)
    n_chunks = zg_ref.shape[1] // CHUNK
    mask_k = _block_mask(4 * CHUNK, 128, CHUNK, GLA_DK)
    mask_v = _block_mask(4 * CHUNK, 256, CHUNK, GLA_DV)
    mask_st = _block_mask(256, 128, GLA_DV, GLA_DK)
    causal = (_iota((CHUNK, 256), 1) % CHUNK) <= _iota((CHUNK, 256), 0)

    def body(c, carry):
        r0 = pl.multiple_of(c * CHUNK, CHUNK)
        zg = zg_ref[0, pl.ds(r0, CHUNK), :]
        zm = zm_ref[0, pl.ds(r0, CHUNK), :]
        q = zg[:, 0:128] * GLA_DK ** -0.5
        k = zg[:, 128:256]
        v = zg[:, 256:512]
        r = zg[:, 512:768]
        gk = _log_sigmoid(_dot_hi(zm, wlr_ref[...]) + blr_ref[...]) / GLA_GATE_NORM
        bc = _cumsum_rows(gk)
        bl = bc[CHUNK - 1:CHUNK, :]
        q_e = q * jnp.exp(bc)
        k_e = k * jnp.exp(-bc)
        k_u = k * jnp.exp(bl - bc)
        dec = jnp.exp(bl)
        att = jnp.where(causal, _dot_nt(q_e, jnp.where(mask_k, _tile4(k_e), 0.0)), 0.0)
        o = _dot(att, jnp.where(mask_v, _tile4(v), 0.0)) + _dot_nt(q_e, st_ref[...])
        st_ref[...] = st_ref[...] * dec + jnp.where(mask_st, _dot_tn(v, k_u), 0.0)
        ms = _dot_hi(o * o, ones_ref[...])
        o = o * lax.rsqrt(ms + EPS) * gn_ref[...] * _silu(r)
        o_ref[0, pl.ds(r0, CHUNK), :] = o.astype(o_ref.dtype)
        return carry

    lax.fori_loop(0, n_chunks, body, 0)


def _gla(zg, zm, wlr, blr, gn, ones_bd):
    b, s, _ = zg.shape
    return pl.pallas_call(
        _gla_kernel,
        grid=(b,),
        in_specs=[pl.BlockSpec((1, s, ZG_W), lambda i: (i, 0, 0)),
                  pl.BlockSpec((1, s, ZM_W), lambda i: (i, 0, 0)),
                  pl.BlockSpec((ZM_W, 128), lambda i: (0, 0)),
                  pl.BlockSpec((1, 128), lambda i: (0, 0)),
                  pl.BlockSpec((1, MIX_W), lambda i: (0, 0)),
                  pl.BlockSpec((MIX_W, MIX_W), lambda i: (0, 0))],
        out_specs=pl.BlockSpec((1, s, MIX_W), lambda i: (i, 0, 0)),
        out_shape=jax.ShapeDtypeStruct((b, s, MIX_W), BF16),
        scratch_shapes=[pltpu.VMEM((256, 128), F32)],
        compiler_params=_cparams(("parallel",)),
        name="gla",
    )(zg, zm, wlr, blr, gn, ones_bd)


def _gdn_kernel(zd_ref, zm_ref, cw_ref, eb_ref, ea_ref, alog_ref, dtb_ref, gn_ref, ones_ref,
                o_ref, st_ref):
    st_ref[...] = jnp.zeros_like(st_ref)
    n_chunks = zd_ref.shape[1] // CHUNK
    c4 = 4 * CHUNK
    mask_bd = _block_mask(c4, c4, CHUNK, CHUNK)
    eye = _iota((c4, c4), 0) == _iota((c4, c4), 1)
    col = _iota((CHUNK, c4), 1) % CHUNK
    row = _iota((CHUNK, c4), 0)
    incl = col <= row
    strict = col < row
    tile_eye = (col == row).astype(F32)
    ones_rows = jnp.ones((CHUNK, c4), F32)
    cw = cw_ref[...]

    def body(c, carry):
        r0 = pl.multiple_of(c * CHUNK, CHUNK)
        zd = zd_ref[0, pl.ds(r0, CHUNK), :]
        zm = zm_ref[0, pl.ds(r0, CHUNK), :]
        h0 = pl.multiple_of(jnp.maximum(r0 - SUBLANE, 0), SUBLANE)
        halo = jnp.where(c > 0, zd_ref[0, pl.ds(h0, SUBLANE), 0:768], 0.0)
        ext = jnp.concatenate([halo, zd[:, 0:768]], axis=0)
        conv = (cw[3:4] * ext + cw[2:3] * pltpu.roll(ext, 1, 0)
                + cw[1:2] * pltpu.roll(ext, 2, 0) + cw[0:1] * pltpu.roll(ext, 3, 0))
        qkv = _silu(conv[SUBLANE:, :])
        q = qkv[:, 0:256]
        k = qkv[:, 256:512]
        v = qkv[:, 512:768]
        gate = zd[:, 768:1024]
        q = q * lax.rsqrt(_dot_hi(q * q, ones_ref[...]) + EPS) * GDN_DH ** -0.5
        k = k * lax.rsqrt(_dot_hi(k * k, ones_ref[...]) + EPS)
        beta = jax.nn.sigmoid(_dot_hi(zm, eb_ref[...]))
        g = -jnp.exp(alog_ref[...]) * _softplus(_dot_hi(zm, ea_ref[...]) + dtb_ref[...])
        gc = _cumsum_rows(g)
        gl = gc[CHUNK - 1:CHUNK, :]
        g_row = _dot_hi(ones_rows, jnp.where(eye, _tile4(gc), 0.0))
        decay = jnp.exp(jnp.where(incl, gc - g_row, -jnp.inf))
        kb = k * beta
        vb = v * beta
        k_bd = jnp.where(mask_bd, _tile4(k), 0.0)
        n_mat = jnp.where(strict, _dot_nt(kb, k_bd) * decay, 0.0)
        a_qk = _dot_nt(q, k_bd) * decay
        m = -jnp.where(mask_bd, _tile4(n_mat), 0.0)
        x = tile_eye - n_mat
        p = 2
        while p < CHUNK:
            m = _dot(m, m)
            x = x + _dot(x, m)
            p *= 2
        rhs = jnp.concatenate([jnp.where(mask_bd, _tile4(vb), 0.0),
                               jnp.where(mask_bd, _tile4(kb * jnp.exp(gc)), 0.0)], axis=1)
        uw = _dot(x, rhs)
        u = uw[:, 0:256]
        w = uw[:, 256:512]
        q_dec = q * jnp.exp(gc)
        k_dec = k * jnp.exp(gl - gc)
        dec = jnp.exp(gl)
        st = st_ref[...]
        v_new = u - _dot(w, st)
        o = _dot(q_dec, st) + _dot(a_qk, jnp.where(mask_bd, _tile4(v_new), 0.0))
        st_ref[...] = st * dec + jnp.where(mask_bd, _dot_tn(k_dec, v_new), 0.0)
        ms = _dot_hi(o * o, ones_ref[...]) * (1.0 / GDN_DH)
        o = o * lax.rsqrt(ms + EPS) * gn_ref[...] * _silu(gate)
        o_ref[0, pl.ds(r0, CHUNK), :] = o.astype(o_ref.dtype)
        return carry

    lax.fori_loop(0, n_chunks, body, 0)


def _gdn(zd, zm, cw, eb, ea, alog, dtb, gn, ones_bd):
    b, s, _ = zd.shape
    return pl.pallas_call(
        _gdn_kernel,
        grid=(b,),
        in_specs=[pl.BlockSpec((1, s, ZD_W), lambda i: (i, 0, 0)),
                  pl.BlockSpec((1, s, ZM_W), lambda i: (i, 0, 0)),
                  pl.BlockSpec((GDN_CONV, 768), lambda i: (0, 0)),
                  pl.BlockSpec((ZM_W, MIX_W), lambda i: (0, 0)),
                  pl.BlockSpec((ZM_W, MIX_W), lambda i: (0, 0)),
                  pl.BlockSpec((1, MIX_W), lambda i: (0, 0)),
                  pl.BlockSpec((1, MIX_W), lambda i: (0, 0)),
                  pl.BlockSpec((1, MIX_W), lambda i: (0, 0)),
                  pl.BlockSpec((MIX_W, MIX_W), lambda i: (0, 0))],
        out_specs=pl.BlockSpec((1, s, MIX_W), lambda i: (i, 0, 0)),
        out_shape=jax.ShapeDtypeStruct((b, s, MIX_W), BF16),
        scratch_shapes=[pltpu.VMEM((256, 256), F32)],
        compiler_params=_cparams(("parallel",)),
        name="gdn",
    )(zd, zm, cw, eb, ea, alog, dtb, gn, ones_bd)


def _cmp_kernel(kc_ref, vc_ref, pe_ref, w1_ref, w2_ref, ck_ref, cv_ref):
    n_sub = NSA_CMP_LEN // NSA_CMP_STRIDE
    n_chunks = kc_ref.shape[1] // NSA_CMP_STRIDE
    for which, (src_ref, out_ref) in enumerate(((kc_ref, ck_ref), (vc_ref, cv_ref))):
        parts = [jnp.zeros((n_chunks, LANE), F32) for _ in range(n_sub)]
        for i in range(NSA_CMP_STRIDE):
            slab = src_ref[0, pl.ds(i, n_chunks, stride=NSA_CMP_STRIDE), :]
            for sub in range(n_sub):
                p = sub * NSA_CMP_STRIDE + i
                parts[sub] = parts[sub] + _dot(slab + pe_ref[which, p:p + 1, :], w1_ref[which, p])
        pre = parts[0] + pltpu.roll(parts[1], n_chunks - 1, 0)
        out_ref[0] = _dot(jax.nn.gelu(pre), w2_ref[which])


def _nsa_compress(zn, pe_x, w1_bd, w2_bd):
    b, s, _ = zn.shape
    n_chunks = s // NSA_CMP_STRIDE
    return pl.pallas_call(
        _cmp_kernel,
        grid=(b,),
        in_specs=[pl.BlockSpec((1, s, LANE), lambda i: (i, 0, 2)),
                  pl.BlockSpec((1, s, LANE), lambda i: (i, 0, 3)),
                  pl.BlockSpec((2, NSA_CMP_LEN, LANE), lambda i: (0, 0, 0)),
                  pl.BlockSpec((2, NSA_CMP_LEN, LANE, LANE), lambda i: (0, 0, 0, 0)),
                  pl.BlockSpec((2, LANE, LANE), lambda i: (0, 0, 0))],
        out_specs=[pl.BlockSpec((1, n_chunks, LANE), lambda i: (i, 0, 0))] * 2,
        out_shape=[jax.ShapeDtypeStruct((b, n_chunks, LANE), F32)] * 2,
        compiler_params=_cparams(("parallel",)),
        name="nsa_compress",
    )(zn, zn, pe_x, w1_bd, w2_bd)


def _nsa_kernel(q_ref, ksv_ref, kwv_ref, ck_ref, cv_ref, zm_ref, cover_ref, eexp_ref, gexp_ref, o_ref):
    tq = q_ref.shape[1]
    s_len = ksv_ref.shape[1]
    span = NSA_WINDOW + tq
    s0 = pl.program_id(1) * tq
    t_pos = s0 + _iota((tq, 1), 0)
    q = q_ref[0] * NSA_DH ** -0.5
    lane_grp = _iota((1, LANE), 1) // NSA_DH
    ck = ck_ref[0]
    cv = cv_ref[0]
    k_slc = ksv_ref[0, :, 0:LANE]
    v_slc = ksv_ref[0, :, LANE:2 * LANE]
    w0 = pl.multiple_of(jnp.maximum(s0 - NSA_WINDOW, 0), tq)
    k_win = kwv_ref[0, pl.ds(w0, span), 0:LANE]
    v_win = kwv_ref[0, pl.ds(w0, span), LANE:2 * LANE]

    cmp_end = _iota((1, N_CMP_PAD), 1) * NSA_CMP_STRIDE + (NSA_CMP_LEN - 1)
    cmp_valid = cmp_end <= t_pos
    k_pos = _iota((1, s_len), 1)
    causal = k_pos <= t_pos
    kw_pos = w0 + _iota((1, span), 1)
    win_valid = (kw_pos <= t_pos) & (kw_pos > t_pos - NSA_WINDOW)
    blk = _iota((tq, N_SLC_PAD), 1)
    cur = t_pos // NSA_SEL_LEN
    forced = (blk == 0) | (blk == cur) | (blk == cur - 1)
    future = blk > cur

    o_cmp = [jnp.zeros((tq, LANE), F32) for _ in range(NSA_HPG)]
    o_slc = [jnp.zeros((tq, LANE), F32) for _ in range(NSA_HPG)]
    o_win = [jnp.zeros((tq, LANE), F32) for _ in range(NSA_HPG)]
    for g in range(NSA_GROUPS):
        in_grp = lane_grp == g
        qs = [jnp.where(in_grp, q[:, j * LANE:(j + 1) * LANE], 0.0) for j in range(NSA_HPG)]
        imp_c = jnp.zeros((tq, N_CMP_PAD), F32)
        for j in range(NSA_HPG):
            p_c = _masked_softmax(_dot_nt_hi(qs[j], ck), cmp_valid)
            o_cmp[j] = o_cmp[j] + jnp.where(in_grp, _dot(p_c, cv), 0.0)
            imp_c = imp_c + p_c
        imp = _dot_hi(imp_c, cover_ref[...])
        imp = jnp.where(forced, jnp.inf, jnp.where(future, -jnp.inf, imp))
        rank = jnp.zeros((tq, N_SLC_PAD), jnp.int32)
        for m in range(s_len // NSA_SEL_LEN):
            other = imp[:, m:m + 1]
            ahead = (other > imp) | ((other == imp) & (m < blk))
            rank = rank + ahead.astype(jnp.int32)
        sel = (rank < NSA_N_SEL).astype(BF16)
        sel_keys = jnp.dot(sel, eexp_ref[...], preferred_element_type=F32) > 0.5
        slc_valid = sel_keys & causal
        for j in range(NSA_HPG):
            p_s = _masked_softmax(_dot_nt(qs[j], k_slc), slc_valid)
            o_slc[j] = o_slc[j] + jnp.where(in_grp, _dot(p_s, v_slc), 0.0)
            p_w = _masked_softmax(_dot_nt(qs[j], k_win), win_valid)
            o_win[j] = o_win[j] + jnp.where(in_grp, _dot(p_w, v_win), 0.0)

    zm = zm_ref[0]
    gates = [jax.nn.sigmoid(_dot_hi(zm, gexp_ref[c])) for c in range(3)]
    o = (gates[0] * jnp.concatenate(o_cmp, axis=1) + gates[1] * jnp.concatenate(o_slc, axis=1)
         + gates[2] * jnp.concatenate(o_win, axis=1))
    o_ref[0] = o.astype(o_ref.dtype)


def _nsa(zn, zm, ck, cv, cover, eexp, gexp):
    b, s, _ = zn.shape
    tq = TQ_NSA
    return pl.pallas_call(
        _nsa_kernel,
        grid=(b, s // tq),
        in_specs=[pl.BlockSpec((1, tq, 256), lambda i, j: (i, j, 0)),
                  pl.BlockSpec((1, s, 256), lambda i, j: (i, 0, 2)),
                  pl.BlockSpec((1, s, 256), lambda i, j: (i, 0, 3)),
                  pl.BlockSpec((1, N_CMP_PAD, LANE), lambda i, j: (i, 0, 0)),
                  pl.BlockSpec((1, N_CMP_PAD, LANE), lambda i, j: (i, 0, 0)),
                  pl.BlockSpec((1, tq, ZM_W), lambda i, j: (i, j, 0)),
                  pl.BlockSpec((N_CMP_PAD, N_SLC_PAD), lambda i, j: (0, 0)),
                  pl.BlockSpec((N_SLC_PAD, s), lambda i, j: (0, 0)),
                  pl.BlockSpec((3, ZM_W, MIX_W), lambda i, j: (0, 0, 0))],
        out_specs=pl.BlockSpec((1, tq, MIX_W), lambda i, j: (i, j, 0)),
        out_shape=jax.ShapeDtypeStruct((b, s, MIX_W), BF16),
        compiler_params=_cparams(("parallel", "arbitrary")),
        name="nsa_attn",
    )(zn, zn, zn, ck, cv, zm, cover, eexp, gexp)


def _combine_kernel(x_ref, g_ref, op_ref, oa_ref, od_ref, on_ref, wg_ref, bg_ref, wb_ref, wo_ref, o_ref):
    x = x_ref[...]
    h = _rms(x, g_ref[...]).astype(BF16)
    y = jnp.zeros(x.shape, F32)
    for i, br_ref in enumerate((op_ref, oa_ref, od_ref, on_ref)):
        gate = jax.nn.sigmoid(jnp.dot(h, wg_ref[i], preferred_element_type=F32) + bg_ref[i])
        y = y + gate * jnp.dot(br_ref[...], wb_ref[i], preferred_element_type=F32)
    o_ref[...] = x + _dot(y, wo_ref[...])


def _combine(x2, g, branches, wg, bg, wb, wo):
    t = x2.shape[0]
    tm = TM_COMB
    return pl.pallas_call(
        _combine_kernel,
        grid=(t // tm,),
        in_specs=[pl.BlockSpec((tm, D_MODEL), lambda i: (i, 0)),
                  pl.BlockSpec((1, D_MODEL), lambda i: (0, 0))]
                 + [pl.BlockSpec((tm, MIX_W), lambda i: (i, 0))] * N_BRANCH
                 + [pl.BlockSpec((N_BRANCH, D_MODEL, D_MODEL), lambda i: (0, 0, 0)),
                    pl.BlockSpec((N_BRANCH, 1, D_MODEL), lambda i: (0, 0, 0)),
                    pl.BlockSpec((N_BRANCH, MIX_W, D_MODEL), lambda i: (0, 0, 0)),
                    pl.BlockSpec((D_MODEL, D_MODEL), lambda i: (0, 0))],
        out_specs=pl.BlockSpec((tm, D_MODEL), lambda i: (i, 0)),
        out_shape=jax.ShapeDtypeStruct((t, D_MODEL), F32),
        compiler_params=_cparams(("parallel",)),
        name="combine",
    )(x2, g, *branches, wg, bg, wb, wo)


def _memkv_kernel(m_ref, g_ref, w_ref, o_ref):
    o_ref[0] = _dot(_rms(m_ref[0], g_ref[...]), w_ref[...]).astype(o_ref.dtype)


def _memkv(mem, g, w):
    b, m, _ = mem.shape
    n = 2 * X_HEADS * X_DH
    return pl.pallas_call(
        _memkv_kernel,
        grid=(b,),
        in_specs=[pl.BlockSpec((1, m, D_MODEL), lambda i: (i, 0, 0)),
                  pl.BlockSpec((1, D_MODEL), lambda i: (0, 0)),
                  pl.BlockSpec((D_MODEL, n), lambda i: (0, 0))],
        out_specs=pl.BlockSpec((1, m, n), lambda i: (i, 0, 0)),
        out_shape=jax.ShapeDtypeStruct((b, m, n), BF16),
        compiler_params=_cparams(("parallel",)),
        name="mem_kv",
    )(mem, g, w)


def _cross_kernel(x_ref, g_ref, wq_ref, kv_ref, wo_ref, o_ref):
    x = x_ref[0]
    hn = _rms(x, g_ref[...])
    q = _dot(hn, wq_ref[...])
    n_k = X_HEADS * X_DH
    outs = []
    for h in range(X_HEADS):
        k_h = kv_ref[0, :, h * X_DH:(h + 1) * X_DH]
        v_h = kv_ref[0, :, n_k + h * X_DH:n_k + (h + 1) * X_DH]
        sc = _dot_nt(q[:, h * X_DH:(h + 1) * X_DH], k_h) * X_DH ** -0.5
        e = jnp.exp(sc - jnp.max(sc, axis=-1, keepdims=True))
        p = e / jnp.sum(e, axis=-1, keepdims=True)
        outs.append(_dot(p, v_h))
    o_ref[0] = x + _dot(jnp.concatenate(outs, axis=1), wo_ref[...])


def _cross(x3, g, wq, kv, wo):
    b, s, _ = x3.shape
    tm = TM_CROSS
    m = kv.shape[1]
    n_k = X_HEADS * X_DH
    return pl.pallas_call(
        _cross_kernel,
        grid=(b, s // tm),
        in_specs=[pl.BlockSpec((1, tm, D_MODEL), lambda i, j: (i, j, 0)),
                  pl.BlockSpec((1, D_MODEL), lambda i, j: (0, 0)),
                  pl.BlockSpec((D_MODEL, n_k), lambda i, j: (0, 0)),
                  pl.BlockSpec((1, m, 2 * n_k), lambda i, j: (i, 0, 0)),
                  pl.BlockSpec((n_k, D_MODEL), lambda i, j: (0, 0))],
        out_specs=pl.BlockSpec((1, tm, D_MODEL), lambda i, j: (i, j, 0)),
        out_shape=jax.ShapeDtypeStruct((b, s, D_MODEL), F32),
        compiler_params=_cparams(("parallel", "parallel")),
        name="cross_attn",
    )(x3, g, wq, kv, wo)


def _ffn_kernel(x_ref, g_ref, wup_ref, cw_ref, cb_ref, wd_ref, gf_ref, o_ref, tail_ref, acc_ref, *, final):
    @pl.when(pl.program_id(1) == 0)
    def _():
        tail_ref[...] = jnp.zeros_like(tail_ref)

    ts = x_ref.shape[1]
    x = x_ref[0]
    hn = _rms(x, g_ref[...]).astype(BF16)
    acc_ref[...] = x
    for c in range(D_FF // FF_CHUNK):
        cols = slice(c * FF_CHUNK, (c + 1) * FF_CHUNK)
        gcols = slice(D_FF + c * FF_CHUNK, D_FF + (c + 1) * FF_CHUNK)
        u = jnp.dot(hn, wup_ref[:, cols], preferred_element_type=F32)
        v = jnp.dot(hn, wup_ref[:, gcols], preferred_element_type=F32)
        ext = jnp.concatenate([tail_ref[:, cols], u], axis=0)
        tail_ref[:, cols] = u[ts - SUBLANE:, :]
        cw = cw_ref[:, cols]
        y = (cw[2:3] * u + cw[1:2] * pltpu.roll(ext, 1, 0)[SUBLANE:]
             + cw[0:1] * pltpu.roll(ext, 2, 0)[SUBLANE:] + cb_ref[:, cols])
        acc_ref[...] += _dot(jax.nn.gelu(y) * v, wd_ref[cols, :])
    out = acc_ref[...]
    if final:
        out = _rms(out, gf_ref[...])
    o_ref[0] = out


def _ffn(x3, g, wup, cw, cb, wd, gf, final):
    b, s, _ = x3.shape
    ts = TS_FFN
    return pl.pallas_call(
        functools.partial(_ffn_kernel, final=final),
        grid=(b, s // ts),
        in_specs=[pl.BlockSpec((1, ts, D_MODEL), lambda i, j: (i, j, 0)),
                  pl.BlockSpec((1, D_MODEL), lambda i, j: (0, 0)),
                  pl.BlockSpec((D_MODEL, 2 * D_FF), lambda i, j: (0, 0), pipeline_mode=pl.Buffered(1)),
                  pl.BlockSpec((FFN_CONV, D_FF), lambda i, j: (0, 0)),
                  pl.BlockSpec((1, D_FF), lambda i, j: (0, 0)),
                  pl.BlockSpec((D_FF, D_MODEL), lambda i, j: (0, 0), pipeline_mode=pl.Buffered(1)),
                  pl.BlockSpec((1, D_MODEL), lambda i, j: (0, 0))],
        out_specs=pl.BlockSpec((1, ts, D_MODEL), lambda i, j: (i, j, 0)),
        out_shape=jax.ShapeDtypeStruct((b, s, D_MODEL), F32),
        scratch_shapes=[pltpu.VMEM((SUBLANE, D_FF), F32), pltpu.VMEM((ts, D_MODEL), F32)],
        compiler_params=_cparams(("parallel", "arbitrary")),
        name="conv_ffn",
    )(x3, g, wup, cw, cb, wd, gf)


def _inproj_columns():
    starts = np.concatenate([[0], np.cumsum(IN_SPLITS)])
    (p_in, a_q, a_k, a_v, a_r, a_lr, d_q, d_k, d_v, d_b, d_a, d_g,
     n_q, n_kc, n_vc, n_ks, n_vs, n_kw, n_vw, n_g) = [np.arange(starts[i], starts[i + 1])
                                                      for i in range(len(IN_SPLITS))]
    n_q = n_q.reshape(NSA_GROUPS, NSA_HPG, NSA_DH).transpose(1, 0, 2).reshape(-1)
    misc = np.full((ZM_W,), N_IN)
    misc[MISC_LR:MISC_LR + GLA_LOWRANK] = a_lr
    misc[MISC_B:MISC_B + GDN_HEADS] = d_b
    misc[MISC_A:MISC_A + GDN_HEADS] = d_a
    misc[MISC_G:MISC_G + 3 * NSA_HEADS] = n_g
    cols = np.concatenate([p_in, a_q, a_k, a_v, a_r, d_q, d_k, d_v, d_g,
                           n_q, n_kc, n_vc, n_ks, n_vs, n_kw, n_vw, misc])
    assert cols.shape[0] == Z_W
    return cols


def _head_expand(offset, n_heads, width):
    e = np.zeros((ZM_W, n_heads * width), np.float32)
    for h in range(n_heads):
        e[offset + h, h * width:(h + 1) * width] = 1.0
    return e


def _nsa_constants(s):
    n_cmp = s // NSA_CMP_STRIDE - NSA_CMP_LEN // NSA_CMP_STRIDE + 1
    n_slc = s // NSA_SEL_LEN
    c_start = np.arange(n_cmp) * NSA_CMP_STRIDE
    s_start = np.arange(n_slc) * NSA_SEL_LEN
    cover = np.zeros((N_CMP_PAD, N_SLC_PAD), np.float32)
    cover[:n_cmp, :n_slc] = ((c_start[:, None] <= s_start[None, :] + NSA_SEL_LEN - 1)
                             & (c_start[:, None] + NSA_CMP_LEN - 1 >= s_start[None, :]))
    eexp = (np.arange(N_SLC_PAD)[:, None] == (np.arange(s) // NSA_SEL_LEN)[None, :]).astype(np.float32)
    gexp = np.zeros((3, ZM_W, MIX_W), np.float32)
    for g in range(NSA_GROUPS):
        for j in range(NSA_HPG):
            slot = j * NSA_GROUPS + g
            for c in range(3):
                gexp[c, MISC_G + (g * NSA_HPG + j) * 3 + c, slot * NSA_DH:(slot + 1) * NSA_DH] = 1.0
    return jnp.asarray(cover), jnp.asarray(eexp, dtype=BF16), jnp.asarray(gexp)


def _block_diag(blocks):
    n, a, b = blocks.shape
    return jnp.einsum('gh,gab->gahb', jnp.eye(n, dtype=blocks.dtype), blocks).reshape(n * a, n * b)


def kernel(x, mem, g_mix, w_in, pool_w, pool_scale, gla_w_lr, gla_b_lr, gla_g_norm, gdn_conv, gdn_a_log,
           gdn_dt_bias, gdn_g_norm, nsa_pe, nsa_cmp_w1, nsa_cmp_w2, w_branch, w_gate, b_gate, w_out, g_cross,
           g_mem, w_xq, w_mem_kv, w_xo, g_ffn, w_up, ffn_conv, ffn_conv_b, w_down, g_final):
    b, s, d = x.shape
    depth = w_in.shape[0]
    t = b * s
    cols = _inproj_columns()
    cover, eexp, gexp = _nsa_constants(s)
    eb = jnp.asarray(_head_expand(MISC_B, GDN_HEADS, GDN_DH))
    ea = jnp.asarray(_head_expand(MISC_A, GDN_HEADS, GDN_DH))
    ones_gla = _block_diag(jnp.full((GLA_HEADS, GLA_DV, GLA_DV), 1.0 / GLA_DV, F32))
    ones_gdn = _block_diag(jnp.ones((GDN_HEADS, GDN_DH, GDN_DH), F32))
    nsa_rows = np.arange(MIX_W).reshape(NSA_GROUPS, NSA_HPG, NSA_DH).transpose(1, 0, 2).reshape(-1)
    row = lambda v: v.reshape(1, -1).astype(F32)

    x2 = x.reshape(t, d)
    for l in range(depth):
        w_in_r = jnp.concatenate([w_in[l], jnp.zeros((d, 1), F32)], axis=1)[:, cols].astype(BF16)
        zp, zg, zd, zn, zm = _inproj(x2, row(g_mix[l]), w_in_r)
        zp, zg, zd, zn, zm = (z.reshape(b, s, -1) for z in (zp, zg, zd, zn, zm))

        o_pool = _pool(zp, _block_diag(pool_w[l]).astype(BF16), row(pool_scale[l]))

        wlr = jnp.zeros((ZM_W, GLA_HEADS * GLA_DK), F32).at[MISC_LR:MISC_LR + GLA_LOWRANK].set(gla_w_lr[l])
        o_gla = _gla(zg, zm, wlr, row(gla_b_lr[l]), row(jnp.tile(gla_g_norm[l], GLA_HEADS)), ones_gla)

        o_gdn = _gdn(zd, zm, gdn_conv[l], eb, ea, row(jnp.repeat(gdn_a_log[l], GDN_DH)),
                     row(jnp.repeat(gdn_dt_bias[l], GDN_DH)), row(jnp.tile(gdn_g_norm[l], GDN_HEADS)),
                     ones_gdn)

        pe_x = jnp.tile(nsa_pe[l], (1, 1, NSA_GROUPS))
        w1 = nsa_cmp_w1[l].reshape(2, NSA_CMP_LEN, NSA_DH, NSA_DH)
        eye_g = jnp.eye(NSA_GROUPS, dtype=F32)
        w1_bd = jnp.einsum('gh,kpde->kpgdhe', eye_g, w1).reshape(2, NSA_CMP_LEN, LANE, LANE).astype(BF16)
        w2_bd = jnp.einsum('gh,kde->kgdhe', eye_g, nsa_cmp_w2[l]).reshape(2, LANE, LANE).astype(BF16)
        ck, cv = _nsa_compress(zn, pe_x, w1_bd, w2_bd)
        o_nsa = _nsa(zn, zm, ck, cv, cover, eexp, gexp)

        wb = jnp.concatenate([w_branch[l, :3], w_branch[l, 3][nsa_rows][None]], axis=0).astype(BF16)
        branches = [o.reshape(t, MIX_W) for o in (o_pool, o_gla, o_gdn, o_nsa)]
        x2 = _combine(x2, row(g_mix[l]), branches, w_gate[l].astype(BF16),
                      b_gate[l].reshape(N_BRANCH, 1, d), wb, w_out[l].astype(BF16))

        kv = _memkv(mem, row(g_mem[l]), w_mem_kv[l].astype(BF16))
        x3 = _cross(x2.reshape(b, s, d), row(g_cross[l]), w_xq[l].astype(BF16), kv, w_xo[l].astype(BF16))

        x3 = _ffn(x3, row(g_ffn[l]), w_up[l].astype(BF16), ffn_conv[l], row(ffn_conv_b[l]),
                  w_down[l].astype(BF16), row(g_final), final=(l == depth - 1))
        x2 = x3.reshape(t, d)
    return x2.reshape(b, s, d)
```

```python
import functools

import numpy as np
import jax
import jax.numpy as jnp
from jax import lax
from jax.experimental import pallas as pl
from jax.experimental.pallas import tpu as pltpu

F32 = jnp.float32
BF16 = jnp.bfloat16
HIGHEST = lax.Precision.HIGHEST

D_MODEL = 1024
MIX_W = 256
POOL_WINDOWS = (2, 4, 8, 16)
POOL_GW = 64
GLA_HEADS = 4
GLA_DK = 32
GLA_DV = 64
GLA_LOWRANK = 16
GLA_GATE_NORM = 16.0
CHUNK = 64
GDN_HEADS = 4
GDN_DH = 64
GDN_CONV = 4
NSA_HEADS = 4
NSA_GROUPS = 2
NSA_HPG = 2
NSA_DH = 64
NSA_KV = 128
NSA_CMP_LEN = 32
NSA_CMP_STRIDE = 16
NSA_SEL_LEN = 64
NSA_N_SEL = 16
NSA_WINDOW = 512
X_HEADS = 4
X_DH = 128
D_FF = 2816
FFN_CONV = 3
EPS = 1e-6
N_BRANCH = 4

IN_SPLITS = (MIX_W,
             128, 128, 256, 256, GLA_LOWRANK,
             MIX_W, MIX_W, MIX_W, GDN_HEADS, GDN_HEADS, MIX_W,
             256, NSA_KV, NSA_KV, NSA_KV, NSA_KV, NSA_KV, NSA_KV, 3 * NSA_HEADS)
N_IN = sum(IN_SPLITS)

MISC_LR = 0
MISC_B = 16
MISC_A = 20
MISC_G = 24
LANE = 128
SUBLANE = 8

Z_WIDTHS = (256, 768, 1024, 768, 256, 128)
Z_DTYPES = (F32, F32, F32, BF16, F32, F32)
ZP_W, ZG_W, ZD_W, ZN_W, ZC_W, ZM_W = Z_WIDTHS
Z_W = sum(Z_WIDTHS)
NEG_BIG = -1e30

TM_PROJ = 256
TM_COMB = 256
TM_CROSS = 512
TS_FFN = 512
FF_CHUNK = 256
TQ_NSA = 256
N_SLC_PAD = 128
N_CMP_PAD = 128

VMEM_LIMIT = 56 * 1024 * 1024


def _cparams(sem):
    return pltpu.CompilerParams(dimension_semantics=sem, vmem_limit_bytes=VMEM_LIMIT)


def _rms(x, g):
    return x * lax.rsqrt(jnp.mean(x * x, axis=-1, keepdims=True) + EPS) * g


def _dot(a, b):
    return jnp.dot(a.astype(BF16), b.astype(BF16), preferred_element_type=F32)


def _dot_nt(a, b):
    return lax.dot_general(a.astype(BF16), b.astype(BF16), (((1,), (1,)), ((), ())),
                           preferred_element_type=F32)


def _dot_tn(a, b):
    return lax.dot_general(a.astype(BF16), b.astype(BF16), (((0,), (0,)), ((), ())),
                           preferred_element_type=F32)


def _dot_hi(a, b):
    return jnp.dot(a, b, precision=HIGHEST, preferred_element_type=F32)


def _dot_nt_hi(a, b):
    return lax.dot_general(a, b, (((1,), (1,)), ((), ())), precision=HIGHEST,
                           preferred_element_type=F32)


def _split2_dot(a, b):
    hi = a.astype(BF16)
    lo = (a - hi.astype(F32)).astype(BF16)
    return jnp.dot(hi, b, preferred_element_type=F32) + jnp.dot(lo, b, preferred_element_type=F32)


def _iota(shape, axis):
    return lax.broadcasted_iota(jnp.int32, shape, axis)


def _block_mask(rows, cols, rb, cb):
    return (_iota((rows, cols), 0) // rb) == (_iota((rows, cols), 1) // cb)


def _shift_rows(x, k):
    t = _iota(x.shape, 0)
    return jnp.where(t >= k, pltpu.roll(x, k, 0), 0.0)


def _cumsum_rows(x):
    k = 1
    while k < x.shape[0]:
        x = x + _shift_rows(x, k)
        k *= 2
    return x


def _softplus(x):
    return jnp.maximum(x, 0.0) + jnp.log1p(jnp.exp(-jnp.abs(x)))


def _log_sigmoid(x):
    return -_softplus(-x)


def _silu(x):
    return x * jax.nn.sigmoid(x)


def _masked_softmax(s, mask):
    s = jnp.where(mask, s, -jnp.inf)
    m = jnp.max(s, axis=-1, keepdims=True)
    m = jnp.where(jnp.isfinite(m), m, 0.0)
    e = jnp.exp(s - m)
    den = jnp.sum(e, axis=-1, keepdims=True)
    return e / jnp.where(den > 0, den, 1.0)


def _tile4(x):
    return jnp.concatenate([x, x, x, x], axis=0)


def _inproj_kernel(x_ref, g_ref, w_ref, *z_refs):
    h = _rms(x_ref[...], g_ref[...]).astype(BF16)
    off = 0
    for ref in z_refs:
        n = ref.shape[-1]
        ref[...] = jnp.dot(h, w_ref[:, off:off + n], preferred_element_type=F32).astype(ref.dtype)
        off += n


def _inproj(x2, g, w):
    t = x2.shape[0]
    return pl.pallas_call(
        _inproj_kernel,
        grid=(t // TM_PROJ,),
        in_specs=[pl.BlockSpec((TM_PROJ, D_MODEL), lambda i: (i, 0)),
                  pl.BlockSpec((1, D_MODEL), lambda i: (0, 0)),
                  pl.BlockSpec((D_MODEL, Z_W), lambda i: (0, 0))],
        out_specs=[pl.BlockSpec((TM_PROJ, n), lambda i: (i, 0)) for n in Z_WIDTHS],
        out_shape=[jax.ShapeDtypeStruct((t, n), dt) for n, dt in zip(Z_WIDTHS, Z_DTYPES)],
        compiler_params=_cparams(("parallel",)),
        name="inproj",
    )(x2, g, w)


def _pool_kernel(u_ref, w_ref, sc_ref, o_ref):
    u = u_ref[0]
    s2 = u + _shift_rows(u, 1)
    s4 = s2 + _shift_rows(s2, 2)
    s8 = s4 + _shift_rows(s4, 4)
    s16 = s8 + _shift_rows(s8, 8)
    grp = _iota(u.shape, 1) // POOL_GW
    win = jnp.where(grp == 0, s2, jnp.where(grp == 1, s4, jnp.where(grp == 2, s8, s16)))
    width = jnp.where(grp == 0, POOL_WINDOWS[0],
                      jnp.where(grp == 1, POOL_WINDOWS[1],
                                jnp.where(grp == 2, POOL_WINDOWS[2], POOL_WINDOWS[3])))
    cnt = jnp.minimum(_iota(u.shape, 0) + 1, width).astype(F32)
    diff = win / cnt - u
    o_ref[0] = (_dot(diff, w_ref[...]) * sc_ref[...]).astype(o_ref.dtype)


def _pool(zp, w_bd, scale):
    b, s, _ = zp.shape
    return pl.pallas_call(
        _pool_kernel,
        grid=(b,),
        in_specs=[pl.BlockSpec((1, s, MIX_W), lambda i: (i, 0, 0)),
                  pl.BlockSpec((MIX_W, MIX_W), lambda i: (0, 0)),
                  pl.BlockSpec((1, MIX_W), lambda i: (0, 0))],
        out_specs=pl.BlockSpec((1, s, MIX_W), lambda i: (i, 0, 0)),
        out_shape=jax.ShapeDtypeStruct((b, s, MIX_W), BF16),
        compiler_params=_cparams(("parallel",)),
        name="pool",
    )(zp, w_bd, scale)


def _gla_kernel(zg_ref, zm_ref, wlr_ref, blr_ref, gn_ref, ones_ref, o_ref, st_ref):
    st_ref[...] = jnp.zeros_like(st_ref)
    n_chunks = zg_ref.shape[1] // CHUNK
    mask_k = _block_mask(4 * CHUNK, 128, CHUNK, GLA_DK)
    mask_v = _block_mask(4 * CHUNK, 256, CHUNK, GLA_DV)
    mask_st = _block_mask(256, 128, GLA_DV, GLA_DK)
    causal = (_iota((CHUNK, 256), 1) % CHUNK) <= _iota((CHUNK, 256), 0)

    def body(c, carry):
        r0 = pl.multiple_of(c * CHUNK, CHUNK)
        zg = zg_ref[0, pl.ds(r0, CHUNK), :]
        zm = zm_ref[0, pl.ds(r0, CHUNK), :]
        q = zg[:, 0:128] * GLA_DK ** -0.5
        k = zg[:, 128:256]
        v = zg[:, 256:512]
        r = zg[:, 512:768]
        gk = _log_sigmoid(_dot_hi(zm, wlr_ref[...]) + blr_ref[...]) / GLA_GATE_NORM
        bc = _cumsum_rows(gk)
        bl = bc[CHUNK - 1:CHUNK, :]
        q_e = q * jnp.exp(bc)
        k_e = k * jnp.exp(-bc)
        k_u = k * jnp.exp(bl - bc)
        dec = jnp.exp(bl)
        att = jnp.where(causal, _dot_nt(q_e, jnp.where(mask_k, _tile4(k_e), 0.0)), 0.0)
        o = _dot(att, jnp.where(mask_v, _tile4(v), 0.0)) + _dot_nt(q_e, st_ref[...])
        st_ref[...] = st_ref[...] * dec + jnp.where(mask_st, _dot_tn(v, k_u), 0.0)
        ms = _dot_hi(o * o, ones_ref[...])
        o = o * lax.rsqrt(ms + EPS) * gn_ref[...] * _silu(r)
        o_ref[0, pl.ds(r0, CHUNK), :] = o.astype(o_ref.dtype)
        return carry

    lax.fori_loop(0, n_chunks, body, 0)


def _gla(zg, zm, wlr, blr, gn, ones_bd):
    b, s, _ = zg.shape
    return pl.pallas_call(
        _gla_kernel,
        grid=(b,),
        in_specs=[pl.BlockSpec((1, s, ZG_W), lambda i: (i, 0, 0)),
                  pl.BlockSpec((1, s, ZM_W), lambda i: (i, 0, 0)),
                  pl.BlockSpec((ZM_W, 128), lambda i: (0, 0)),
                  pl.BlockSpec((1, 128), lambda i: (0, 0)),
                  pl.BlockSpec((1, MIX_W), lambda i: (0, 0)),
                  pl.BlockSpec((MIX_W, MIX_W), lambda i: (0, 0))],
        out_specs=pl.BlockSpec((1, s, MIX_W), lambda i: (i, 0, 0)),
        out_shape=jax.ShapeDtypeStruct((b, s, MIX_W), BF16),
        scratch_shapes=[pltpu.VMEM((256, 128), F32)],
        compiler_params=_cparams(("parallel",)),
        name="gla",
    )(zg, zm, wlr, blr, gn, ones_bd)


def _gdn_kernel(zd_ref, zm_ref, cw_ref, eb_ref, ea_ref, alog_ref, dtb_ref, gn_ref, ones_ref,
                o_ref, st_ref):
    st_ref[...] = jnp.zeros_like(st_ref)
    n_chunks = zd_ref.shape[1] // CHUNK
    c4 = 4 * CHUNK
    mask_bd = _block_mask(c4, c4, CHUNK, CHUNK)
    eye = _iota((c4, c4), 0) == _iota((c4, c4), 1)
    col = _iota((CHUNK, c4), 1) % CHUNK
    row = _iota((CHUNK, c4), 0)
    incl = col <= row
    strict = col < row
    tile_eye = (col == row).astype(F32)
    ones_rows = jnp.ones((CHUNK, c4), F32)
    cw = cw_ref[...]

    def body(c, carry):
        r0 = pl.multiple_of(c * CHUNK, CHUNK)
        zd = zd_ref[0, pl.ds(r0, CHUNK), :]
        zm = zm_ref[0, pl.ds(r0, CHUNK), :]
        h0 = pl.multiple_of(jnp.maximum(r0 - SUBLANE, 0), SUBLANE)
        halo = jnp.where(c > 0, zd_ref[0, pl.ds(h0, SUBLANE), 0:768], 0.0)
        ext = jnp.concatenate([halo, zd[:, 0:768]], axis=0)
        conv = (cw[3:4] * ext + cw[2:3] * pltpu.roll(ext, 1, 0)
                + cw[1:2] * pltpu.roll(ext, 2, 0) + cw[0:1] * pltpu.roll(ext, 3, 0))
        qkv = _silu(conv[SUBLANE:, :])
        q = qkv[:, 0:256]
        k = qkv[:, 256:512]
        v = qkv[:, 512:768]
        gate = zd[:, 768:1024]
        q = q * lax.rsqrt(_dot_hi(q * q, ones_ref[...]) + EPS) * GDN_DH ** -0.5
        k = k * lax.rsqrt(_dot_hi(k * k, ones_ref[...]) + EPS)
        beta = jax.nn.sigmoid(_dot_hi(zm, eb_ref[...]))
        g = -jnp.exp(alog_ref[...]) * _softplus(_dot_hi(zm, ea_ref[...]) + dtb_ref[...])
        gc = _cumsum_rows(g)
        gl = gc[CHUNK - 1:CHUNK, :]
        g_row = _dot_hi(ones_rows, jnp.where(eye, _tile4(gc), 0.0))
        decay = jnp.exp(jnp.where(incl, gc - g_row, -jnp.inf))
        kb = k * beta
        vb = v * beta
        k_bd = jnp.where(mask_bd, _tile4(k), 0.0)
        n_mat = jnp.where(strict, _dot_nt(kb, k_bd) * decay, 0.0)
        a_qk = _dot_nt(q, k_bd) * decay
        m = -jnp.where(mask_bd, _tile4(n_mat), 0.0)
        x = tile_eye - n_mat
        p = 2
        while p < CHUNK:
            m = _dot(m, m)
            x = x + _dot(x, m)
            p *= 2
        rhs = jnp.concatenate([jnp.where(mask_bd, _tile4(vb), 0.0),
                               jnp.where(mask_bd, _tile4(kb * jnp.exp(gc)), 0.0)], axis=1)
        uw = _dot(x, rhs)
        u = uw[:, 0:256]
        w = uw[:, 256:512]
        q_dec = q * jnp.exp(gc)
        k_dec = k * jnp.exp(gl - gc)
        dec = jnp.exp(gl)
        st = st_ref[...]
        v_new = u - _dot(w, st)
        o = _dot(q_dec, st) + _dot(a_qk, jnp.where(mask_bd, _tile4(v_new), 0.0))
        st_ref[...] = st * dec + jnp.where(mask_bd, _dot_tn(k_dec, v_new), 0.0)
        ms = _dot_hi(o * o, ones_ref[...]) * (1.0 / GDN_DH)
        o = o * lax.rsqrt(ms + EPS) * gn_ref[...] * _silu(gate)
        o_ref[0, pl.ds(r0, CHUNK), :] = o.astype(o_ref.dtype)
        return carry

    lax.fori_loop(0, n_chunks, body, 0)


def _gdn(zd, zm, cw, eb, ea, alog, dtb, gn, ones_bd):
    b, s, _ = zd.shape
    return pl.pallas_call(
        _gdn_kernel,
        grid=(b,),
        in_specs=[pl.BlockSpec((1, s, ZD_W), lambda i: (i, 0, 0)),
                  pl.BlockSpec((1, s, ZM_W), lambda i: (i, 0, 0)),
                  pl.BlockSpec((GDN_CONV, 768), lambda i: (0, 0)),
                  pl.BlockSpec((ZM_W, MIX_W), lambda i: (0, 0)),
                  pl.BlockSpec((ZM_W, MIX_W), lambda i: (0, 0)),
                  pl.BlockSpec((1, MIX_W), lambda i: (0, 0)),
                  pl.BlockSpec((1, MIX_W), lambda i: (0, 0)),
                  pl.BlockSpec((1, MIX_W), lambda i: (0, 0)),
                  pl.BlockSpec((MIX_W, MIX_W), lambda i: (0, 0))],
        out_specs=pl.BlockSpec((1, s, MIX_W), lambda i: (i, 0, 0)),
        out_shape=jax.ShapeDtypeStruct((b, s, MIX_W), BF16),
        scratch_shapes=[pltpu.VMEM((256, 256), F32)],
        compiler_params=_cparams(("parallel",)),
        name="gdn",
    )(zd, zm, cw, eb, ea, alog, dtb, gn, ones_bd)


def _cmp_kernel(kc_ref, vc_ref, pe_ref, w1_ref, w2_ref, ck_ref, cv_ref):
    n_sub = NSA_CMP_LEN // NSA_CMP_STRIDE
    n_chunks = kc_ref.shape[1] // NSA_CMP_STRIDE
    for which, (src_ref, out_ref) in enumerate(((kc_ref, ck_ref), (vc_ref, cv_ref))):
        parts = [jnp.zeros((n_chunks, LANE), F32) for _ in range(n_sub)]
        for i in range(NSA_CMP_STRIDE):
            slab = src_ref[0, pl.ds(i, n_chunks, stride=NSA_CMP_STRIDE), :]
            for sub in range(n_sub):
                p = sub * NSA_CMP_STRIDE + i
                parts[sub] = parts[sub] + _dot(slab + pe_ref[which, p:p + 1, :], w1_ref[which, p])
        pre = parts[0] + pltpu.roll(parts[1], n_chunks - 1, 0)
        out_ref[0] = _dot(jax.nn.gelu(pre), w2_ref[which])


def _nsa_compress(zc, pe_x, w1_bd, w2_bd):
    b, s, _ = zc.shape
    n_chunks = s // NSA_CMP_STRIDE
    return pl.pallas_call(
        _cmp_kernel,
        grid=(b,),
        in_specs=[pl.BlockSpec((1, s, LANE), lambda i: (i, 0, 0)),
                  pl.BlockSpec((1, s, LANE), lambda i: (i, 0, 1)),
                  pl.BlockSpec((2, NSA_CMP_LEN, LANE), lambda i: (0, 0, 0)),
                  pl.BlockSpec((2, NSA_CMP_LEN, LANE, LANE), lambda i: (0, 0, 0, 0)),
                  pl.BlockSpec((2, LANE, LANE), lambda i: (0, 0, 0))],
        out_specs=[pl.BlockSpec((1, n_chunks, LANE), lambda i: (i, 0, 0))] * 2,
        out_shape=[jax.ShapeDtypeStruct((b, n_chunks, LANE), F32)] * 2,
        compiler_params=_cparams(("parallel",)),
        name="nsa_compress",
    )(zc, zc, pe_x, w1_bd, w2_bd)


def _masked_attend(s, mask, v):
    s = jnp.where(mask, s, -jnp.inf)
    m = jnp.max(s, axis=-1, keepdims=True)
    m = jnp.where(jnp.isfinite(m), m, 0.0)
    e = jnp.exp(s - m)
    den = jnp.sum(e, axis=-1, keepdims=True)
    return _dot(e, v) / jnp.where(den > 0, den, 1.0)


def _nsa_kernel(q_ref, ksv_ref, kwv_ref, ck_ref, cv_ref, zm_ref, covt_ref, gexp_ref, o_ref,
                kaug_ref, s_ref, mx_ref, l_ref, acc_ref):
    tq = q_ref.shape[1]
    s_len = ksv_ref.shape[1]
    n_slc = s_len // NSA_SEL_LEN
    span = NSA_WINDOW + tq
    qi = pl.program_id(1)
    s0 = pl.multiple_of(qi * tq, tq)
    lane_grp = _iota((1, LANE), 1) // NSA_DH

    @pl.when(qi == 0)
    def _():
        lane = _iota((s_len, LANE), 1)
        key_blk = _iota((s_len, LANE), 0) // NSA_SEL_LEN
        k = ksv_ref[0, :, 0:LANE].astype(F32)
        for g in range(NSA_GROUPS):
            onehot = jnp.where(lane - (1 - g) * NSA_DH == key_blk, 1.0, 0.0)
            kaug_ref[g] = jnp.where(lane // NSA_DH == g, k, onehot).astype(BF16)

    t_col = s0 + _iota((tq, 1), 0)
    t_col2 = jnp.concatenate([t_col, t_col], axis=0)
    t_row = s0 + _iota((1, tq), 1)
    q = q_ref[0].astype(F32) * NSA_DH ** -0.5
    ck = ck_ref[0]
    cv = cv_ref[0]
    cmp_end = _iota((1, N_CMP_PAD), 1) * NSA_CMP_STRIDE + (NSA_CMP_LEN - 1)
    cmp_valid = cmp_end <= t_col2
    w0 = pl.multiple_of(jnp.maximum(s0 - NSA_WINDOW, 0), tq)
    k_win = kwv_ref[0, pl.ds(w0, span), 0:LANE]
    v_win = kwv_ref[0, pl.ds(w0, span), LANE:2 * LANE]
    kw_pos = w0 + _iota((1, span), 1)
    win_valid = (kw_pos <= t_col2) & (kw_pos > t_col2 - NSA_WINDOW)
    blk_t = _iota((n_slc, tq), 0)
    cur_t = t_row // NSA_SEL_LEN
    forced_t = (blk_t == 0) | (blk_t == cur_t) | (blk_t == cur_t - 1)
    future_t = blk_t > cur_t
    row_in_tile = jnp.concatenate([_iota((tq, tq), 0)] * NSA_HPG, axis=0)
    diag_ok = _iota((NSA_HPG * tq, tq), 1) <= row_in_tile

    o_cmp = [jnp.zeros((tq, LANE), F32) for _ in range(NSA_HPG)]
    o_slc = [jnp.zeros((tq, LANE), F32) for _ in range(NSA_HPG)]
    o_win = [jnp.zeros((tq, LANE), F32) for _ in range(NSA_HPG)]
    for g in range(NSA_GROUPS):
        in_grp = lane_grp == g
        q2 = jnp.concatenate([jnp.where(in_grp, q[:, j * LANE:(j + 1) * LANE], 0.0)
                              for j in range(NSA_HPG)], axis=0)
        p_c = _masked_softmax(_dot_nt_hi(q2, ck), cmp_valid)
        oc = jnp.where(in_grp, _dot(p_c, cv), 0.0)
        imp_c = p_c[0:tq] + p_c[tq:2 * tq]
        imp_t = _dot_nt_hi(covt_ref[...], imp_c)[0:n_slc]
        imp_t = jnp.where(forced_t, jnp.inf, jnp.where(future_t, -jnp.inf, imp_t))
        rank = jnp.zeros((n_slc, tq), jnp.int32)
        for m in range(n_slc):
            other = imp_t[m:m + 1, :]
            ahead = (other > imp_t) | ((other == imp_t) & (blk_t > m))
            rank = rank + ahead.astype(jnp.int32)
        bias_t = jnp.where(rank < NSA_N_SEL, 0.0, NEG_BIG)
        lo = (1 - g) * NSA_DH
        rows = ([jnp.zeros((lo, tq), F32)] if lo else []) + [bias_t, jnp.zeros((LANE - lo - n_slc, tq), F32)]
        bias = jnp.concatenate(rows, axis=0).T
        qa = jnp.where(in_grp, q2, jnp.concatenate([bias] * NSA_HPG, axis=0)).astype(BF16)

        mx_ref[...] = jnp.full(mx_ref.shape, NEG_BIG, F32)

        def scores(kt, carry):
            off = pl.multiple_of(kt * tq, tq)
            s_t = _dot_nt(qa, kaug_ref[g, pl.ds(off, tq), :])
            s_ref[kt] = s_t
            mx_ref[...] = jnp.maximum(mx_ref[...], s_t)
            return carry

        lax.fori_loop(0, qi, scores, 0)
        s_d = jnp.where(diag_ok, _dot_nt(qa, kaug_ref[g, pl.ds(s0, tq), :]), NEG_BIG)
        s_ref[qi] = s_d
        m_s = jnp.max(jnp.maximum(mx_ref[...], s_d), axis=-1, keepdims=True)
        m_s = jnp.where(jnp.isfinite(m_s), m_s, 0.0)
        l_ref[...] = jnp.zeros(l_ref.shape, F32)
        acc_ref[...] = jnp.zeros(acc_ref.shape, F32)

        def attend(kt, carry):
            off = pl.multiple_of(kt * tq, tq)
            e = jnp.exp(s_ref[kt] - m_s)
            l_ref[...] += e
            acc_ref[...] += _dot(e, ksv_ref[0, pl.ds(off, tq), LANE:2 * LANE])
            return carry

        lax.fori_loop(0, qi + 1, attend, 0)
        den = jnp.sum(l_ref[...], axis=-1, keepdims=True)
        os_ = jnp.where(in_grp, acc_ref[...] / jnp.where(den > 0, den, 1.0), 0.0)
        ow = jnp.where(in_grp, _masked_attend(_dot_nt(q2, k_win), win_valid, v_win), 0.0)
        for j in range(NSA_HPG):
            rows_j = slice(j * tq, (j + 1) * tq)
            o_cmp[j] = o_cmp[j] + oc[rows_j]
            o_slc[j] = o_slc[j] + os_[rows_j]
            o_win[j] = o_win[j] + ow[rows_j]

    gates = jax.nn.sigmoid(_split2_dot(zm_ref[0], gexp_ref[...]))
    o = (gates[:, 0:MIX_W] * jnp.concatenate(o_cmp, axis=1)
         + gates[:, MIX_W:2 * MIX_W] * jnp.concatenate(o_slc, axis=1)
         + gates[:, 2 * MIX_W:3 * MIX_W] * jnp.concatenate(o_win, axis=1))
    o_ref[0] = o.astype(o_ref.dtype)


def _nsa(zn, zm, ck, cv, covt, gexp):
    b, s, _ = zn.shape
    tq = TQ_NSA
    return pl.pallas_call(
        _nsa_kernel,
        grid=(b, s // tq),
        in_specs=[pl.BlockSpec((1, tq, 256), lambda i, j: (i, j, 0)),
                  pl.BlockSpec((1, s, 256), lambda i, j: (i, 0, 1)),
                  pl.BlockSpec((1, s, 256), lambda i, j: (i, 0, 2)),
                  pl.BlockSpec((1, N_CMP_PAD, LANE), lambda i, j: (i, 0, 0)),
                  pl.BlockSpec((1, N_CMP_PAD, LANE), lambda i, j: (i, 0, 0)),
                  pl.BlockSpec((1, tq, ZM_W), lambda i, j: (i, j, 0)),
                  pl.BlockSpec((N_SLC_PAD, N_CMP_PAD), lambda i, j: (0, 0)),
                  pl.BlockSpec((ZM_W, 3 * MIX_W), lambda i, j: (0, 0))],
        out_specs=pl.BlockSpec((1, tq, MIX_W), lambda i, j: (i, j, 0)),
        out_shape=jax.ShapeDtypeStruct((b, s, MIX_W), BF16),
        scratch_shapes=[pltpu.VMEM((NSA_GROUPS, s, LANE), BF16),
                        pltpu.VMEM((s // tq, NSA_HPG * tq, tq), F32),
                        pltpu.VMEM((NSA_HPG * tq, tq), F32),
                        pltpu.VMEM((NSA_HPG * tq, tq), F32),
                        pltpu.VMEM((NSA_HPG * tq, LANE), F32)],
        compiler_params=_cparams(("parallel", "arbitrary")),
        name="nsa_attn",
    )(zn, zn, zn, ck, cv, zm, covt, gexp)


def _combine_kernel(x_ref, g_ref, op_ref, oa_ref, od_ref, on_ref, wg_ref, bg_ref, wb_ref, wo_ref, o_ref):
    x = x_ref[...]
    h = _rms(x, g_ref[...]).astype(BF16)
    y = jnp.zeros(x.shape, F32)
    for i, br_ref in enumerate((op_ref, oa_ref, od_ref, on_ref)):
        gate = jax.nn.sigmoid(jnp.dot(h, wg_ref[i], preferred_element_type=F32) + bg_ref[i])
        y = y + gate * jnp.dot(br_ref[...], wb_ref[i], preferred_element_type=F32)
    o_ref[...] = x + _dot(y, wo_ref[...])


def _combine(x2, g, branches, wg, bg, wb, wo):
    t = x2.shape[0]
    tm = TM_COMB
    return pl.pallas_call(
        _combine_kernel,
        grid=(t // tm,),
        in_specs=[pl.BlockSpec((tm, D_MODEL), lambda i: (i, 0)),
                  pl.BlockSpec((1, D_MODEL), lambda i: (0, 0))]
                 + [pl.BlockSpec((tm, MIX_W), lambda i: (i, 0))] * N_BRANCH
                 + [pl.BlockSpec((N_BRANCH, D_MODEL, D_MODEL), lambda i: (0, 0, 0)),
                    pl.BlockSpec((N_BRANCH, 1, D_MODEL), lambda i: (0, 0, 0)),
                    pl.BlockSpec((N_BRANCH, MIX_W, D_MODEL), lambda i: (0, 0, 0)),
                    pl.BlockSpec((D_MODEL, D_MODEL), lambda i: (0, 0))],
        out_specs=pl.BlockSpec((tm, D_MODEL), lambda i: (i, 0)),
        out_shape=jax.ShapeDtypeStruct((t, D_MODEL), F32),
        compiler_params=_cparams(("parallel",)),
        name="combine",
    )(x2, g, *branches, wg, bg, wb, wo)


def _memkv_kernel(m_ref, g_ref, w_ref, o_ref):
    o_ref[0] = _dot(_rms(m_ref[0], g_ref[...]), w_ref[...]).astype(o_ref.dtype)


def _memkv(mem, g, w):
    b, m, _ = mem.shape
    n = 2 * X_HEADS * X_DH
    return pl.pallas_call(
        _memkv_kernel,
        grid=(b,),
        in_specs=[pl.BlockSpec((1, m, D_MODEL), lambda i: (i, 0, 0)),
                  pl.BlockSpec((1, D_MODEL), lambda i: (0, 0)),
                  pl.BlockSpec((D_MODEL, n), lambda i: (0, 0))],
        out_specs=pl.BlockSpec((1, m, n), lambda i: (i, 0, 0)),
        out_shape=jax.ShapeDtypeStruct((b, m, n), BF16),
        compiler_params=_cparams(("parallel",)),
        name="mem_kv",
    )(mem, g, w)


def _cross_kernel(x_ref, g_ref, wq_ref, kv_ref, wo_ref, o_ref):
    x = x_ref[0]
    hn = _rms(x, g_ref[...])
    q = _dot(hn, wq_ref[...])
    n_k = X_HEADS * X_DH
    outs = []
    for h in range(X_HEADS):
        k_h = kv_ref[0, :, h * X_DH:(h + 1) * X_DH]
        v_h = kv_ref[0, :, n_k + h * X_DH:n_k + (h + 1) * X_DH]
        sc = _dot_nt(q[:, h * X_DH:(h + 1) * X_DH], k_h) * X_DH ** -0.5
        e = jnp.exp(sc - jnp.max(sc, axis=-1, keepdims=True))
        p = e / jnp.sum(e, axis=-1, keepdims=True)
        outs.append(_dot(p, v_h))
    o_ref[0] = x + _dot(jnp.concatenate(outs, axis=1), wo_ref[...])


def _cross(x3, g, wq, kv, wo):
    b, s, _ = x3.shape
    tm = TM_CROSS
    m = kv.shape[1]
    n_k = X_HEADS * X_DH
    return pl.pallas_call(
        _cross_kernel,
        grid=(b, s // tm),
        in_specs=[pl.BlockSpec((1, tm, D_MODEL), lambda i, j: (i, j, 0)),
                  pl.BlockSpec((1, D_MODEL), lambda i, j: (0, 0)),
                  pl.BlockSpec((D_MODEL, n_k), lambda i, j: (0, 0)),
                  pl.BlockSpec((1, m, 2 * n_k), lambda i, j: (i, 0, 0)),
                  pl.BlockSpec((n_k, D_MODEL), lambda i, j: (0, 0))],
        out_specs=pl.BlockSpec((1, tm, D_MODEL), lambda i, j: (i, j, 0)),
        out_shape=jax.ShapeDtypeStruct((b, s, D_MODEL), F32),
        compiler_params=_cparams(("parallel", "parallel")),
        name="cross_attn",
    )(x3, g, wq, kv, wo)


def _ffn_kernel(x_ref, g_ref, wup_ref, cw_ref, cb_ref, wd_ref, gf_ref, o_ref, tail_ref, acc_ref, *, final):
    @pl.when(pl.program_id(1) == 0)
    def _():
        tail_ref[...] = jnp.zeros_like(tail_ref)

    ts = x_ref.shape[1]
    x = x_ref[0]
    hn = _rms(x, g_ref[...]).astype(BF16)
    acc_ref[...] = x
    for c in range(D_FF // FF_CHUNK):
        cols = slice(c * FF_CHUNK, (c + 1) * FF_CHUNK)
        gcols = slice(D_FF + c * FF_CHUNK, D_FF + (c + 1) * FF_CHUNK)
        u = jnp.dot(hn, wup_ref[:, cols], preferred_element_type=F32)
        v = jnp.dot(hn, wup_ref[:, gcols], preferred_element_type=F32)
        ext = jnp.concatenate([tail_ref[:, cols], u], axis=0)
        tail_ref[:, cols] = u[ts - SUBLANE:, :]
        cw = cw_ref[:, cols]
        y = (cw[2:3] * u + cw[1:2] * pltpu.roll(ext, 1, 0)[SUBLANE:]
             + cw[0:1] * pltpu.roll(ext, 2, 0)[SUBLANE:] + cb_ref[:, cols])
        acc_ref[...] += _dot(jax.nn.gelu(y) * v, wd_ref[cols, :])
    out = acc_ref[...]
    if final:
        out = _rms(out, gf_ref[...])
    o_ref[0] = out


def _ffn(x3, g, wup, cw, cb, wd, gf, final):
    b, s, _ = x3.shape
    ts = TS_FFN
    return pl.pallas_call(
        functools.partial(_ffn_kernel, final=final),
        grid=(b, s // ts),
        in_specs=[pl.BlockSpec((1, ts, D_MODEL), lambda i, j: (i, j, 0)),
                  pl.BlockSpec((1, D_MODEL), lambda i, j: (0, 0)),
                  pl.BlockSpec((D_MODEL, 2 * D_FF), lambda i, j: (0, 0), pipeline_mode=pl.Buffered(1)),
                  pl.BlockSpec((FFN_CONV, D_FF), lambda i, j: (0, 0)),
                  pl.BlockSpec((1, D_FF), lambda i, j: (0, 0)),
                  pl.BlockSpec((D_FF, D_MODEL), lambda i, j: (0, 0), pipeline_mode=pl.Buffered(1)),
                  pl.BlockSpec((1, D_MODEL), lambda i, j: (0, 0))],
        out_specs=pl.BlockSpec((1, ts, D_MODEL), lambda i, j: (i, j, 0)),
        out_shape=jax.ShapeDtypeStruct((b, s, D_MODEL), F32),
        scratch_shapes=[pltpu.VMEM((SUBLANE, D_FF), F32), pltpu.VMEM((ts, D_MODEL), F32)],
        compiler_params=_cparams(("parallel", "arbitrary")),
        name="conv_ffn",
    )(x3, g, wup, cw, cb, wd, gf)


def _inproj_columns():
    starts = np.concatenate([[0], np.cumsum(IN_SPLITS)])
    (p_in, a_q, a_k, a_v, a_r, a_lr, d_q, d_k, d_v, d_b, d_a, d_g,
     n_q, n_kc, n_vc, n_ks, n_vs, n_kw, n_vw, n_g) = [np.arange(starts[i], starts[i + 1])
                                                      for i in range(len(IN_SPLITS))]
    n_q = n_q.reshape(NSA_GROUPS, NSA_HPG, NSA_DH).transpose(1, 0, 2).reshape(-1)
    misc = np.full((ZM_W,), N_IN)
    misc[MISC_LR:MISC_LR + GLA_LOWRANK] = a_lr
    misc[MISC_B:MISC_B + GDN_HEADS] = d_b
    misc[MISC_A:MISC_A + GDN_HEADS] = d_a
    misc[MISC_G:MISC_G + 3 * NSA_HEADS] = n_g
    cols = np.concatenate([p_in, a_q, a_k, a_v, a_r, d_q, d_k, d_v, d_g,
                           n_q, n_ks, n_vs, n_kw, n_vw, n_kc, n_vc, misc])
    assert cols.shape[0] == Z_W
    return cols


def _head_expand(offset, n_heads, width):
    e = np.zeros((ZM_W, n_heads * width), np.float32)
    for h in range(n_heads):
        e[offset + h, h * width:(h + 1) * width] = 1.0
    return e


def _nsa_constants(s):
    n_cmp = s // NSA_CMP_STRIDE - NSA_CMP_LEN // NSA_CMP_STRIDE + 1
    n_slc = s // NSA_SEL_LEN
    c_start = np.arange(n_cmp) * NSA_CMP_STRIDE
    s_start = np.arange(n_slc) * NSA_SEL_LEN
    cover = np.zeros((N_CMP_PAD, N_SLC_PAD), np.float32)
    cover[:n_cmp, :n_slc] = ((c_start[:, None] <= s_start[None, :] + NSA_SEL_LEN - 1)
                             & (c_start[:, None] + NSA_CMP_LEN - 1 >= s_start[None, :]))
    gexp = np.zeros((ZM_W, 3, MIX_W), np.float32)
    for g in range(NSA_GROUPS):
        for j in range(NSA_HPG):
            slot = j * NSA_GROUPS + g
            for c in range(3):
                gexp[MISC_G + (g * NSA_HPG + j) * 3 + c, c, slot * NSA_DH:(slot + 1) * NSA_DH] = 1.0
    return jnp.asarray(cover.T), jnp.asarray(gexp.reshape(ZM_W, 3 * MIX_W), dtype=BF16)


def _block_diag(blocks):
    n, a, b = blocks.shape
    return jnp.einsum('gh,gab->gahb', jnp.eye(n, dtype=blocks.dtype), blocks).reshape(n * a, n * b)


def kernel(x, mem, g_mix, w_in, pool_w, pool_scale, gla_w_lr, gla_b_lr, gla_g_norm, gdn_conv, gdn_a_log,
           gdn_dt_bias, gdn_g_norm, nsa_pe, nsa_cmp_w1, nsa_cmp_w2, w_branch, w_gate, b_gate, w_out, g_cross,
           g_mem, w_xq, w_mem_kv, w_xo, g_ffn, w_up, ffn_conv, ffn_conv_b, w_down, g_final):
    b, s, d = x.shape
    depth = w_in.shape[0]
    t = b * s
    cols = _inproj_columns()
    covt, gexp = _nsa_constants(s)
    eb = jnp.asarray(_head_expand(MISC_B, GDN_HEADS, GDN_DH))
    ea = jnp.asarray(_head_expand(MISC_A, GDN_HEADS, GDN_DH))
    ones_gla = _block_diag(jnp.full((GLA_HEADS, GLA_DV, GLA_DV), 1.0 / GLA_DV, F32))
    ones_gdn = _block_diag(jnp.ones((GDN_HEADS, GDN_DH, GDN_DH), F32))
    nsa_rows = np.arange(MIX_W).reshape(NSA_GROUPS, NSA_HPG, NSA_DH).transpose(1, 0, 2).reshape(-1)
    row = lambda v: v.reshape(1, -1).astype(F32)

    x2 = x.reshape(t, d)
    for l in range(depth):
        w_in_r = jnp.concatenate([w_in[l], jnp.zeros((d, 1), F32)], axis=1)[:, cols].astype(BF16)
        zp, zg, zd, zn, zc, zm = (z.reshape(b, s, -1) for z in _inproj(x2, row(g_mix[l]), w_in_r))

        o_pool = _pool(zp, _block_diag(pool_w[l]).astype(BF16), row(pool_scale[l]))

        wlr = jnp.zeros((ZM_W, GLA_HEADS * GLA_DK), F32).at[MISC_LR:MISC_LR + GLA_LOWRANK].set(gla_w_lr[l])
        o_gla = _gla(zg, zm, wlr, row(gla_b_lr[l]), row(jnp.tile(gla_g_norm[l], GLA_HEADS)), ones_gla)

        o_gdn = _gdn(zd, zm, gdn_conv[l], eb, ea, row(jnp.repeat(gdn_a_log[l], GDN_DH)),
                     row(jnp.repeat(gdn_dt_bias[l], GDN_DH)), row(jnp.tile(gdn_g_norm[l], GDN_HEADS)),
                     ones_gdn)

        pe_x = jnp.tile(nsa_pe[l], (1, 1, NSA_GROUPS))
        w1 = nsa_cmp_w1[l].reshape(2, NSA_CMP_LEN, NSA_DH, NSA_DH)
        eye_g = jnp.eye(NSA_GROUPS, dtype=F32)
        w1_bd = jnp.einsum('gh,kpde->kpgdhe', eye_g, w1).reshape(2, NSA_CMP_LEN, LANE, LANE).astype(BF16)
        w2_bd = jnp.einsum('gh,kde->kgdhe', eye_g, nsa_cmp_w2[l]).reshape(2, LANE, LANE).astype(BF16)
        ck, cv = _nsa_compress(zc, pe_x, w1_bd, w2_bd)
        o_nsa = _nsa(zn, zm, ck, cv, covt, gexp)

        wb = jnp.concatenate([w_branch[l, :3], w_branch[l, 3][nsa_rows][None]], axis=0).astype(BF16)
        branches = [o.reshape(t, MIX_W) for o in (o_pool, o_gla, o_gdn, o_nsa)]
        x2 = _combine(x2, row(g_mix[l]), branches, w_gate[l].astype(BF16),
                      b_gate[l].reshape(N_BRANCH, 1, d), wb, w_out[l].astype(BF16))

        kv = _memkv(mem, row(g_mem[l]), w_mem_kv[l].astype(BF16))
        x3 = _cross(x2.reshape(b, s, d), row(g_cross[l]), w_xq[l].astype(BF16), kv, w_xo[l].astype(BF16))

        x3 = _ffn(x3, row(g_ffn[l]), w_up[l].astype(BF16), ffn_conv[l], row(ffn_conv_b[l]),
                  w_down[l].astype(BF16), row(g_final), final=(l == depth - 1))
        x2 = x3.reshape(t, d)
    return x2.reshape(b, s, d)
```

```python
import functools

import numpy as np
import jax
import jax.numpy as jnp
from jax import lax
from jax.experimental import pallas as pl
from jax.experimental.pallas import tpu as pltpu

F32 = jnp.float32
BF16 = jnp.bfloat16
HIGHEST = lax.Precision.HIGHEST

D_MODEL = 1024
MIX_W = 256
POOL_WINDOWS = (2, 4, 8, 16)
POOL_GW = 64
GLA_HEADS = 4
GLA_DK = 32
GLA_DV = 64
GLA_LOWRANK = 16
GLA_GATE_NORM = 16.0
CHUNK = 64
GDN_HEADS = 4
GDN_DH = 64
GDN_CONV = 4
NSA_HEADS = 4
NSA_GROUPS = 2
NSA_HPG = 2
NSA_DH = 64
NSA_KV = 128
NSA_CMP_LEN = 32
NSA_CMP_STRIDE = 16
NSA_SEL_LEN = 64
NSA_N_SEL = 16
NSA_WINDOW = 512
X_HEADS = 4
X_DH = 128
D_FF = 2816
FFN_CONV = 3
EPS = 1e-6
N_BRANCH = 4

IN_SPLITS = (MIX_W,
             128, 128, 256, 256, GLA_LOWRANK,
             MIX_W, MIX_W, MIX_W, GDN_HEADS, GDN_HEADS, MIX_W,
             256, NSA_KV, NSA_KV, NSA_KV, NSA_KV, NSA_KV, NSA_KV, 3 * NSA_HEADS)
N_IN = sum(IN_SPLITS)

MISC_LR = 0
MISC_B = 16
MISC_A = 20
MISC_G = 24
LANE = 128
SUBLANE = 8

Z_WIDTHS = (256, 768, 1024, 768, 256, 128)
Z_DTYPES = (F32, BF16, BF16, BF16, F32, F32)
ZP_W, ZG_W, ZD_W, ZN_W, ZC_W, ZM_W = Z_WIDTHS
Z_W = sum(Z_WIDTHS)
NEG_BIG = -1e30

TM_PROJ = 256
TM_COMB = 256
TM_CROSS = 512
TS_FFN = 512
FF_CHUNK = 256
TQ_NSA = 256
REC_SEQS = 4
N_SLC_PAD = 128
N_CMP_PAD = 128

VMEM_LIMIT = 56 * 1024 * 1024


def _cparams(sem):
    return pltpu.CompilerParams(dimension_semantics=sem, vmem_limit_bytes=VMEM_LIMIT)


def _rms(x, g):
    return x * lax.rsqrt(jnp.mean(x * x, axis=-1, keepdims=True) + EPS) * g


def _dot(a, b):
    return jnp.dot(a.astype(BF16), b.astype(BF16), preferred_element_type=F32)


def _dot_nt(a, b):
    return lax.dot_general(a.astype(BF16), b.astype(BF16), (((1,), (1,)), ((), ())),
                           preferred_element_type=F32)


def _dot_tn(a, b):
    return lax.dot_general(a.astype(BF16), b.astype(BF16), (((0,), (0,)), ((), ())),
                           preferred_element_type=F32)


def _dot_hi(a, b):
    return jnp.dot(a, b, precision=HIGHEST, preferred_element_type=F32)


def _dot_nt_hi(a, b):
    return lax.dot_general(a, b, (((1,), (1,)), ((), ())), precision=HIGHEST,
                           preferred_element_type=F32)


def _split2_dot(a, b):
    hi = a.astype(BF16)
    lo = (a - hi.astype(F32)).astype(BF16)
    return jnp.dot(hi, b, preferred_element_type=F32) + jnp.dot(lo, b, preferred_element_type=F32)


def _split3_rhs_dot(a, b):
    hi = b.astype(BF16)
    r1 = b - hi.astype(F32)
    mid = r1.astype(BF16)
    lo = (r1 - mid.astype(F32)).astype(BF16)
    return ((jnp.dot(a, hi, preferred_element_type=F32) + jnp.dot(a, mid, preferred_element_type=F32))
            + jnp.dot(a, lo, preferred_element_type=F32))


def _dot3(a, w_hi, w_lo):
    a_hi = a.astype(BF16)
    a_lo = (a - a_hi.astype(F32)).astype(BF16)
    return (jnp.dot(a_hi, w_hi, preferred_element_type=F32)
            + (jnp.dot(a_lo, w_hi, preferred_element_type=F32) + jnp.dot(a_hi, w_lo, preferred_element_type=F32)))


def _iota(shape, axis):
    return lax.broadcasted_iota(jnp.int32, shape, axis)


def _block_mask(rows, cols, rb, cb):
    return (_iota((rows, cols), 0) // rb) == (_iota((rows, cols), 1) // cb)


def _shift_rows(x, k):
    t = _iota(x.shape, 0)
    return jnp.where(t >= k, pltpu.roll(x, k, 0), 0.0)


def _cumsum_rows(x):
    k = 1
    while k < x.shape[0]:
        x = x + _shift_rows(x, k)
        k *= 2
    return x


def _softplus(x):
    return jnp.maximum(x, 0.0) + jnp.log1p(jnp.exp(-jnp.abs(x)))


def _log_sigmoid(x):
    return -_softplus(-x)


def _silu(x):
    return x * jax.nn.sigmoid(x)


def _masked_softmax(s, mask):
    s = jnp.where(mask, s, -jnp.inf)
    m = jnp.max(s, axis=-1, keepdims=True)
    m = jnp.where(jnp.isfinite(m), m, 0.0)
    e = jnp.exp(s - m)
    den = jnp.sum(e, axis=-1, keepdims=True)
    return e / jnp.where(den > 0, den, 1.0)


def _tile4(x):
    return jnp.concatenate([x, x, x, x], axis=0)


def _inproj_kernel(x_ref, g_ref, w_ref, *z_refs):
    h = _rms(x_ref[...], g_ref[...]).astype(BF16)
    off = 0
    for ref in z_refs:
        n = ref.shape[-1]
        ref[...] = jnp.dot(h, w_ref[:, off:off + n], preferred_element_type=F32).astype(ref.dtype)
        off += n


def _inproj(x2, g, w):
    t = x2.shape[0]
    return pl.pallas_call(
        _inproj_kernel,
        grid=(t // TM_PROJ,),
        in_specs=[pl.BlockSpec((TM_PROJ, D_MODEL), lambda i: (i, 0)),
                  pl.BlockSpec((1, D_MODEL), lambda i: (0, 0)),
                  pl.BlockSpec((D_MODEL, Z_W), lambda i: (0, 0))],
        out_specs=[pl.BlockSpec((TM_PROJ, n), lambda i: (i, 0)) for n in Z_WIDTHS],
        out_shape=[jax.ShapeDtypeStruct((t, n), dt) for n, dt in zip(Z_WIDTHS, Z_DTYPES)],
        compiler_params=_cparams(("parallel",)),
        name="inproj",
    )(x2, g, w)


def _pool_kernel(u_ref, w_ref, sc_ref, o_ref):
    u = u_ref[0]
    s2 = u + _shift_rows(u, 1)
    s4 = s2 + _shift_rows(s2, 2)
    s8 = s4 + _shift_rows(s4, 4)
    s16 = s8 + _shift_rows(s8, 8)
    grp = _iota(u.shape, 1) // POOL_GW
    win = jnp.where(grp == 0, s2, jnp.where(grp == 1, s4, jnp.where(grp == 2, s8, s16)))
    width = jnp.where(grp == 0, POOL_WINDOWS[0],
                      jnp.where(grp == 1, POOL_WINDOWS[1],
                                jnp.where(grp == 2, POOL_WINDOWS[2], POOL_WINDOWS[3])))
    cnt = jnp.minimum(_iota(u.shape, 0) + 1, width).astype(F32)
    diff = win / cnt - u
    o_ref[0] = (_dot(diff, w_ref[...]) * sc_ref[...]).astype(o_ref.dtype)


def _pool(zp, w_bd, scale):
    b, s, _ = zp.shape
    return pl.pallas_call(
        _pool_kernel,
        grid=(b,),
        in_specs=[pl.BlockSpec((1, s, MIX_W), lambda i: (i, 0, 0)),
                  pl.BlockSpec((MIX_W, MIX_W), lambda i: (0, 0)),
                  pl.BlockSpec((1, MIX_W), lambda i: (0, 0))],
        out_specs=pl.BlockSpec((1, s, MIX_W), lambda i: (i, 0, 0)),
        out_shape=jax.ShapeDtypeStruct((b, s, MIX_W), BF16),
        compiler_params=_cparams(("parallel",)),
        name="pool",
    )(zp, w_bd, scale)


def _gla_masks():
    return (_block_mask(4 * CHUNK, 128, CHUNK, GLA_DK),
            _block_mask(4 * CHUNK, 256, CHUNK, GLA_DV),
            _block_mask(256, 128, GLA_DV, GLA_DK),
            (_iota((CHUNK, 256), 1) % CHUNK) <= _iota((CHUNK, 256), 0))


def _seqs(f, *lists):
    return [f(*args) for args in zip(*lists)]


def _gla_chunk(masks, zg, zm, wlr_ref, blr_ref, gn_ref, ones_ref, st_refs):
    mask_k, mask_v, mask_st, causal = masks
    q = [z[:, 0:128] * GLA_DK ** -0.5 for z in zg]
    k = [z[:, 128:256] for z in zg]
    v = [z[:, 256:512] for z in zg]
    r = [z[:, 512:768] for z in zg]
    pre = _seqs(lambda a: _dot3(a, wlr_ref[0], wlr_ref[1]), zm)
    bc = _seqs(lambda a: _cumsum_rows(_log_sigmoid(a + blr_ref[...]) / GLA_GATE_NORM), pre)
    bl = [a[CHUNK - 1:CHUNK, :] for a in bc]
    q_e = _seqs(lambda a, c: a * jnp.exp(c), q, bc)
    k_e = _seqs(lambda a, c: a * jnp.exp(-c), k, bc)
    k_u = _seqs(lambda a, c, l: a * jnp.exp(l - c), k, bc, bl)
    st = [ref[...] for ref in st_refs]
    att = _seqs(lambda a, b: jnp.where(causal, _dot_nt(a, jnp.where(mask_k, _tile4(b), 0.0)), 0.0), q_e, k_e)
    inter = _seqs(_dot_nt, q_e, st)
    kv = _seqs(_dot_tn, v, k_u)
    o = _seqs(lambda a, b, c: _dot(a, jnp.where(mask_v, _tile4(b), 0.0)) + c, att, v, inter)
    for ref, s_old, l, upd in zip(st_refs, st, bl, kv):
        ref[...] = s_old * jnp.exp(l) + jnp.where(mask_st, upd, 0.0)
    ms = _seqs(lambda a: _split2_dot(a * a, ones_ref[...]) * (1.0 / GLA_DV), o)
    return _seqs(lambda a, m, g: a * lax.rsqrt(m + EPS) * gn_ref[...] * _silu(g), o, ms, r)


GDN_HALO = 16


def _gdn_masks():
    c4 = 4 * CHUNK
    col = _iota((CHUNK, c4), 1) % CHUNK
    row = _iota((CHUNK, c4), 0)
    return (_block_mask(c4, c4, CHUNK, CHUNK), col <= row, col < row, col == row)


def _gdn_chunk(masks, halo, zd, zm, cw_ref, eba_ref, alog_ref, dtb_ref, gn_ref, ones_ref, st_refs):
    mask_bd, incl, strict, diag = masks
    cw = cw_ref[...]
    bd = lambda a: jnp.where(mask_bd, _tile4(a), 0.0)

    def conv_silu(h, z):
        ext = jnp.concatenate([h, z[:, 0:768]], axis=0)
        conv = (cw[3:4] * ext + cw[2:3] * pltpu.roll(ext, 1, 0)
                + cw[1:2] * pltpu.roll(ext, 2, 0) + cw[0:1] * pltpu.roll(ext, 3, 0))
        return _silu(conv[GDN_HALO:, :])

    qkv = _seqs(conv_silu, halo, zd)
    v = [a[:, 512:768] for a in qkv]
    gate = [z[:, 768:1024] for z in zd]
    ssq = _seqs(lambda a: _split2_dot(jnp.concatenate([a[:, 0:256] * a[:, 0:256], a[:, 256:512] * a[:, 256:512]],
                                                      axis=0), ones_ref[...]), qkv)
    q = _seqs(lambda a, s: a[:, 0:256] * lax.rsqrt(s[0:CHUNK] + EPS) * GDN_DH ** -0.5, qkv, ssq)
    k = _seqs(lambda a, s: a[:, 256:512] * lax.rsqrt(s[CHUNK:2 * CHUNK] + EPS), qkv, ssq)
    ba = _seqs(lambda a: _split2_dot(a, eba_ref[...]), zm)
    beta = [jax.nn.sigmoid(a[:, 0:256]) for a in ba]
    gc = _seqs(lambda a: _cumsum_rows(-jnp.exp(alog_ref[...]) * _softplus(a[:, 256:512] + dtb_ref[...])), ba)
    gl = [a[CHUNK - 1:CHUNK, :] for a in gc]
    ones_cc = jnp.ones((CHUNK, CHUNK), BF16)
    g_row = _seqs(lambda a: _split3_rhs_dot(ones_cc, jnp.where(diag, a, 0.0)), gc)
    decay = _seqs(lambda a, b: jnp.exp(jnp.where(incl, a - b, -jnp.inf)), gc, g_row)
    kb = _seqs(lambda a, b: a * b, k, beta)
    vb = _seqs(lambda a, b: a * b, v, beta)
    k_bd = _seqs(bd, k)
    n_mat = _seqs(lambda a, b, d: jnp.where(strict, _dot_nt(a, b) * d, 0.0), kb, k_bd, decay)
    a_qk = _seqs(lambda a, b, d: _dot_nt(a, b) * d, q, k_bd, decay)
    m = _seqs(lambda a: -bd(a), n_mat)
    x = _seqs(lambda a: jnp.where(diag, 1.0, 0.0) - a, n_mat)
    p = 2
    while p < CHUNK:
        m = _seqs(lambda a: _dot(a, a), m)
        x = _seqs(lambda a, b: a + _dot(a, b), x, m)
        p *= 2
    uw = _seqs(lambda a, b, c, g: _dot(a, jnp.concatenate([bd(b), bd(c * jnp.exp(g))], axis=1)),
               x, vb, kb, gc)
    st = [ref[...] for ref in st_refs]
    v_new = _seqs(lambda a, s: a[:, 0:256] - _dot(a[:, 256:512], s), uw, st)
    o = _seqs(lambda a, g, s, qk, vn: _dot(a * jnp.exp(g), s) + _dot(qk, bd(vn)), q, gc, st, a_qk, v_new)
    upd = _seqs(lambda a, g, l, vn: _dot_tn(a * jnp.exp(l - g), vn), k, gc, gl, v_new)
    for ref, s_old, l, u in zip(st_refs, st, gl, upd):
        ref[...] = s_old * jnp.exp(l) + jnp.where(mask_bd, u, 0.0)
    ms = _seqs(lambda a: _split2_dot(a * a, ones_ref[...]) * (1.0 / GDN_DH), o)
    return _seqs(lambda a, m_, g: a * lax.rsqrt(m_ + EPS) * gn_ref[...] * _silu(g), o, ms, gate)


def _rec_kernel(zg_ref, zd_ref, zm_ref, wlr_ref, blr_ref, gng_ref, cw_ref, eba_ref, alog_ref, dtb_ref,
                gnd_ref, ones_ref, og_ref, od_ref, halo_ref, *st_refs):
    nb = zg_ref.shape[0]

    @pl.when(pl.program_id(1) == 0)
    def _():
        halo_ref[...] = jnp.zeros_like(halo_ref)
        for st_ref in st_refs:
            st_ref[...] = jnp.zeros_like(st_ref)

    zm = [zm_ref[i] for i in range(nb)]
    zg = [zg_ref[i].astype(F32) for i in range(nb)]
    zd = [zd_ref[i].astype(F32) for i in range(nb)]
    halo = [halo_ref[i] for i in range(nb)]
    for i in range(nb):
        halo_ref[i] = zd[i][CHUNK - GDN_HALO:, 0:768]
    og = _gla_chunk(_gla_masks(), zg, zm, wlr_ref, blr_ref, gng_ref, ones_ref, st_refs[:nb])
    od = _gdn_chunk(_gdn_masks(), halo, zd, zm, cw_ref, eba_ref, alog_ref, dtb_ref, gnd_ref, ones_ref,
                    st_refs[nb:])
    for i in range(nb):
        og_ref[i] = og[i].astype(og_ref.dtype)
        od_ref[i] = od[i].astype(od_ref.dtype)


def _recurrent(zg, zd, zm, wlr2, blr, gn_gla, cw, eba, alog, dtb, gn_gdn, ones_bd):
    b, s, _ = zg.shape
    nb = REC_SEQS if b % REC_SEQS == 0 else 1
    const = lambda shape: pl.BlockSpec(shape, lambda i, c: (0,) * len(shape))
    return pl.pallas_call(
        _rec_kernel,
        grid=(b // nb, s // CHUNK),
        in_specs=[pl.BlockSpec((nb, CHUNK, ZG_W), lambda i, c: (i, c, 0)),
                  pl.BlockSpec((nb, CHUNK, ZD_W), lambda i, c: (i, c, 0)),
                  pl.BlockSpec((nb, CHUNK, ZM_W), lambda i, c: (i, c, 0)),
                  const((2, ZM_W, 128)), const((1, 128)), const((1, MIX_W)),
                  const((GDN_CONV, 768)), const((ZM_W, 2 * MIX_W)), const((1, MIX_W)), const((1, MIX_W)),
                  const((1, MIX_W)), const((MIX_W, MIX_W))],
        out_specs=[pl.BlockSpec((nb, CHUNK, MIX_W), lambda i, c: (i, c, 0))] * 2,
        out_shape=[jax.ShapeDtypeStruct((b, s, MIX_W), BF16)] * 2,
        scratch_shapes=([pltpu.VMEM((nb, GDN_HALO, 768), F32)] + [pltpu.VMEM((256, 128), F32)] * nb
                        + [pltpu.VMEM((256, 256), F32)] * nb),
        compiler_params=_cparams(("parallel", "arbitrary")),
        name="gla_gdn",
    )(zg, zd, zm, wlr2, blr, gn_gla, cw, eba, alog, dtb, gn_gdn, ones_bd)


def _cmp_kernel(kc_ref, vc_ref, pe_ref, w1_ref, w2_ref, ck_ref, cv_ref):
    n_sub = NSA_CMP_LEN // NSA_CMP_STRIDE
    n_chunks = kc_ref.shape[1] // NSA_CMP_STRIDE
    for which, (src_ref, out_ref) in enumerate(((kc_ref, ck_ref), (vc_ref, cv_ref))):
        parts = [jnp.zeros((n_chunks, LANE), F32) for _ in range(n_sub)]
        for i in range(NSA_CMP_STRIDE):
            slab = src_ref[0, pl.ds(i, n_chunks, stride=NSA_CMP_STRIDE), :]
            for sub in range(n_sub):
                p = sub * NSA_CMP_STRIDE + i
                parts[sub] = parts[sub] + _dot(slab + pe_ref[which, p:p + 1, :], w1_ref[which, p])
        pre = parts[0] + pltpu.roll(parts[1], n_chunks - 1, 0)
        out_ref[0] = _dot(jax.nn.gelu(pre), w2_ref[which])


def _nsa_compress(zc, pe_x, w1_bd, w2_bd):
    b, s, _ = zc.shape
    n_chunks = s // NSA_CMP_STRIDE
    return pl.pallas_call(
        _cmp_kernel,
        grid=(b,),
        in_specs=[pl.BlockSpec((1, s, LANE), lambda i: (i, 0, 0)),
                  pl.BlockSpec((1, s, LANE), lambda i: (i, 0, 1)),
                  pl.BlockSpec((2, NSA_CMP_LEN, LANE), lambda i: (0, 0, 0)),
                  pl.BlockSpec((2, NSA_CMP_LEN, LANE, LANE), lambda i: (0, 0, 0, 0)),
                  pl.BlockSpec((2, LANE, LANE), lambda i: (0, 0, 0))],
        out_specs=[pl.BlockSpec((1, n_chunks, LANE), lambda i: (i, 0, 0))] * 2,
        out_shape=[jax.ShapeDtypeStruct((b, n_chunks, LANE), F32)] * 2,
        compiler_params=_cparams(("parallel",)),
        name="nsa_compress",
    )(zc, zc, pe_x, w1_bd, w2_bd)


def _masked_attend(s, mask, v):
    s = jnp.where(mask, s, -jnp.inf)
    m = jnp.max(s, axis=-1, keepdims=True)
    m = jnp.where(jnp.isfinite(m), m, 0.0)
    e = jnp.exp(s - m)
    den = jnp.sum(e, axis=-1, keepdims=True)
    return _dot(e, v) / jnp.where(den > 0, den, 1.0)


def _nsa_kernel(q_ref, ksv_ref, kwv_ref, ck_ref, cv_ref, zm_ref, covt_ref, gexp_ref, o_ref,
                kaug_ref, s_ref, mx_ref, l_ref, acc_ref):
    tq = q_ref.shape[1]
    s_len = ksv_ref.shape[1]
    n_slc = s_len // NSA_SEL_LEN
    span = NSA_WINDOW + tq
    qi = pl.program_id(1)
    s0 = pl.multiple_of(qi * tq, tq)
    lane_grp = _iota((1, LANE), 1) // NSA_DH

    @pl.when(qi == 0)
    def _():
        lane = _iota((s_len, LANE), 1)
        key_blk = _iota((s_len, LANE), 0) // NSA_SEL_LEN
        k = ksv_ref[0, :, 0:LANE].astype(F32)
        for g in range(NSA_GROUPS):
            onehot = jnp.where(lane - (1 - g) * NSA_DH == key_blk, 1.0, 0.0)
            kaug_ref[g] = jnp.where(lane // NSA_DH == g, k, onehot).astype(BF16)

    t_col = s0 + _iota((tq, 1), 0)
    t_col2 = jnp.concatenate([t_col, t_col], axis=0)
    t_row = s0 + _iota((1, tq), 1)
    q = q_ref[0].astype(F32) * NSA_DH ** -0.5
    ck = ck_ref[0]
    cv = cv_ref[0]
    cmp_end = _iota((1, N_CMP_PAD), 1) * NSA_CMP_STRIDE + (NSA_CMP_LEN - 1)
    cmp_valid = cmp_end <= t_col2
    w0 = pl.multiple_of(jnp.maximum(s0 - NSA_WINDOW, 0), tq)
    k_win = kwv_ref[0, pl.ds(w0, span), 0:LANE]
    v_win = kwv_ref[0, pl.ds(w0, span), LANE:2 * LANE]
    kw_pos = w0 + _iota((1, span), 1)
    win_valid = (kw_pos <= t_col2) & (kw_pos > t_col2 - NSA_WINDOW)
    blk_t = _iota((n_slc, tq), 0)
    cur_t = t_row // NSA_SEL_LEN
    forced_t = (blk_t == 0) | (blk_t == cur_t) | (blk_t == cur_t - 1)
    future_t = blk_t > cur_t
    row_in_tile = jnp.concatenate([_iota((tq, tq), 0)] * NSA_HPG, axis=0)
    diag_ok = _iota((NSA_HPG * tq, tq), 1) <= row_in_tile

    o_cmp = [jnp.zeros((tq, LANE), F32) for _ in range(NSA_HPG)]
    o_slc = [jnp.zeros((tq, LANE), F32) for _ in range(NSA_HPG)]
    o_win = [jnp.zeros((tq, LANE), F32) for _ in range(NSA_HPG)]
    for g in range(NSA_GROUPS):
        in_grp = lane_grp == g
        q2 = jnp.concatenate([jnp.where(in_grp, q[:, j * LANE:(j + 1) * LANE], 0.0)
                              for j in range(NSA_HPG)], axis=0)
        p_c = _masked_softmax(_dot_nt_hi(q2, ck), cmp_valid)
        oc = jnp.where(in_grp, _dot(p_c, cv), 0.0)
        imp_c = p_c[0:tq] + p_c[tq:2 * tq]
        imp_t = _dot_nt_hi(covt_ref[...], imp_c)[0:n_slc]
        imp_t = jnp.where(forced_t, jnp.inf, jnp.where(future_t, -jnp.inf, imp_t))
        rank = jnp.zeros((n_slc, tq), jnp.int32)
        for m in range(n_slc):
            other = imp_t[m:m + 1, :]
            ahead = (other > imp_t) | ((other == imp_t) & (blk_t > m))
            rank = rank + ahead.astype(jnp.int32)
        bias_t = jnp.where(rank < NSA_N_SEL, 0.0, NEG_BIG)
        lo = (1 - g) * NSA_DH
        rows = ([jnp.zeros((lo, tq), F32)] if lo else []) + [bias_t, jnp.zeros((LANE - lo - n_slc, tq), F32)]
        bias = jnp.concatenate(rows, axis=0).T
        qa = jnp.where(in_grp, q2, jnp.concatenate([bias] * NSA_HPG, axis=0)).astype(BF16)

        mx_ref[...] = jnp.full(mx_ref.shape, NEG_BIG, F32)

        def scores(kt, carry):
            off = pl.multiple_of(kt * tq, tq)
            s_t = _dot_nt(qa, kaug_ref[g, pl.ds(off, tq), :])
            s_ref[kt] = s_t
            mx_ref[...] = jnp.maximum(mx_ref[...], s_t)
            return carry

        lax.fori_loop(0, qi, scores, 0)
        s_d = jnp.where(diag_ok, _dot_nt(qa, kaug_ref[g, pl.ds(s0, tq), :]), NEG_BIG)
        s_ref[qi] = s_d
        m_s = jnp.max(jnp.maximum(mx_ref[...], s_d), axis=-1, keepdims=True)
        m_s = jnp.where(jnp.isfinite(m_s), m_s, 0.0)
        l_ref[...] = jnp.zeros(l_ref.shape, F32)
        acc_ref[...] = jnp.zeros(acc_ref.shape, F32)

        def attend(kt, carry):
            off = pl.multiple_of(kt * tq, tq)
            e = jnp.exp(s_ref[kt] - m_s)
            l_ref[...] += e
            acc_ref[...] += _dot(e, ksv_ref[0, pl.ds(off, tq), LANE:2 * LANE])
            return carry

        lax.fori_loop(0, qi + 1, attend, 0)
        den = jnp.sum(l_ref[...], axis=-1, keepdims=True)
        os_ = jnp.where(in_grp, acc_ref[...] / jnp.where(den > 0, den, 1.0), 0.0)
        ow = jnp.where(in_grp, _masked_attend(_dot_nt(q2, k_win), win_valid, v_win), 0.0)
        for j in range(NSA_HPG):
            rows_j = slice(j * tq, (j + 1) * tq)
            o_cmp[j] = o_cmp[j] + oc[rows_j]
            o_slc[j] = o_slc[j] + os_[rows_j]
            o_win[j] = o_win[j] + ow[rows_j]

    gates = jax.nn.sigmoid(_split2_dot(zm_ref[0], gexp_ref[...]))
    o = (gates[:, 0:MIX_W] * jnp.concatenate(o_cmp, axis=1)
         + gates[:, MIX_W:2 * MIX_W] * jnp.concatenate(o_slc, axis=1)
         + gates[:, 2 * MIX_W:3 * MIX_W] * jnp.concatenate(o_win, axis=1))
    o_ref[0] = o.astype(o_ref.dtype)


def _nsa(zn, zm, ck, cv, covt, gexp):
    b, s, _ = zn.shape
    tq = TQ_NSA
    return pl.pallas_call(
        _nsa_kernel,
        grid=(b, s // tq),
        in_specs=[pl.BlockSpec((1, tq, 256), lambda i, j: (i, j, 0)),
                  pl.BlockSpec((1, s, 256), lambda i, j: (i, 0, 1)),
                  pl.BlockSpec((1, s, 256), lambda i, j: (i, 0, 2)),
                  pl.BlockSpec((1, N_CMP_PAD, LANE), lambda i, j: (i, 0, 0)),
                  pl.BlockSpec((1, N_CMP_PAD, LANE), lambda i, j: (i, 0, 0)),
                  pl.BlockSpec((1, tq, ZM_W), lambda i, j: (i, j, 0)),
                  pl.BlockSpec((N_SLC_PAD, N_CMP_PAD), lambda i, j: (0, 0)),
                  pl.BlockSpec((ZM_W, 3 * MIX_W), lambda i, j: (0, 0))],
        out_specs=pl.BlockSpec((1, tq, MIX_W), lambda i, j: (i, j, 0)),
        out_shape=jax.ShapeDtypeStruct((b, s, MIX_W), BF16),
        scratch_shapes=[pltpu.VMEM((NSA_GROUPS, s, LANE), BF16),
                        pltpu.VMEM((s // tq, NSA_HPG * tq, tq), F32),
                        pltpu.VMEM((NSA_HPG * tq, tq), F32),
                        pltpu.VMEM((NSA_HPG * tq, tq), F32),
                        pltpu.VMEM((NSA_HPG * tq, LANE), F32)],
        compiler_params=_cparams(("parallel", "arbitrary")),
        name="nsa_attn",
    )(zn, zn, zn, ck, cv, zm, covt, gexp)


def _combine_kernel(x_ref, g_ref, op_ref, oa_ref, od_ref, on_ref, wg_ref, bg_ref, wb_ref, wo_ref, o_ref):
    x = x_ref[...]
    h = _rms(x, g_ref[...]).astype(BF16)
    y = jnp.zeros(x.shape, F32)
    for i, br_ref in enumerate((op_ref, oa_ref, od_ref, on_ref)):
        gate = jax.nn.sigmoid(jnp.dot(h, wg_ref[i], preferred_element_type=F32) + bg_ref[i])
        y = y + gate * jnp.dot(br_ref[...], wb_ref[i], preferred_element_type=F32)
    o_ref[...] = x + _dot(y, wo_ref[...])


def _combine(x2, g, branches, wg, bg, wb, wo):
    t = x2.shape[0]
    tm = TM_COMB
    return pl.pallas_call(
        _combine_kernel,
        grid=(t // tm,),
        in_specs=[pl.BlockSpec((tm, D_MODEL), lambda i: (i, 0)),
                  pl.BlockSpec((1, D_MODEL), lambda i: (0, 0))]
                 + [pl.BlockSpec((tm, MIX_W), lambda i: (i, 0))] * N_BRANCH
                 + [pl.BlockSpec((N_BRANCH, D_MODEL, D_MODEL), lambda i: (0, 0, 0)),
                    pl.BlockSpec((N_BRANCH, 1, D_MODEL), lambda i: (0, 0, 0)),
                    pl.BlockSpec((N_BRANCH, MIX_W, D_MODEL), lambda i: (0, 0, 0)),
                    pl.BlockSpec((D_MODEL, D_MODEL), lambda i: (0, 0))],
        out_specs=pl.BlockSpec((tm, D_MODEL), lambda i: (i, 0)),
        out_shape=jax.ShapeDtypeStruct((t, D_MODEL), F32),
        compiler_params=_cparams(("parallel",)),
        name="combine",
    )(x2, g, *branches, wg, bg, wb, wo)


def _memkv_kernel(m_ref, g_ref, w_ref, o_ref):
    o_ref[0] = _dot(_rms(m_ref[0], g_ref[...]), w_ref[...]).astype(o_ref.dtype)


def _memkv(mem, g, w):
    b, m, _ = mem.shape
    n = 2 * X_HEADS * X_DH
    return pl.pallas_call(
        _memkv_kernel,
        grid=(b,),
        in_specs=[pl.BlockSpec((1, m, D_MODEL), lambda i: (i, 0, 0)),
                  pl.BlockSpec((1, D_MODEL), lambda i: (0, 0)),
                  pl.BlockSpec((D_MODEL, n), lambda i: (0, 0))],
        out_specs=pl.BlockSpec((1, m, n), lambda i: (i, 0, 0)),
        out_shape=jax.ShapeDtypeStruct((b, m, n), BF16),
        compiler_params=_cparams(("parallel",)),
        name="mem_kv",
    )(mem, g, w)


def _cross_kernel(x_ref, g_ref, wq_ref, kv_ref, wo_ref, o_ref):
    x = x_ref[0]
    hn = _rms(x, g_ref[...])
    q = _dot(hn, wq_ref[...])
    n_k = X_HEADS * X_DH
    outs = []
    for h in range(X_HEADS):
        k_h = kv_ref[0, :, h * X_DH:(h + 1) * X_DH]
        v_h = kv_ref[0, :, n_k + h * X_DH:n_k + (h + 1) * X_DH]
        sc = _dot_nt(q[:, h * X_DH:(h + 1) * X_DH], k_h) * X_DH ** -0.5
        e = jnp.exp(sc - jnp.max(sc, axis=-1, keepdims=True))
        p = e / jnp.sum(e, axis=-1, keepdims=True)
        outs.append(_dot(p, v_h))
    o_ref[0] = x + _dot(jnp.concatenate(outs, axis=1), wo_ref[...])


def _cross(x3, g, wq, kv, wo):
    b, s, _ = x3.shape
    tm = TM_CROSS
    m = kv.shape[1]
    n_k = X_HEADS * X_DH
    return pl.pallas_call(
        _cross_kernel,
        grid=(b, s // tm),
        in_specs=[pl.BlockSpec((1, tm, D_MODEL), lambda i, j: (i, j, 0)),
                  pl.BlockSpec((1, D_MODEL), lambda i, j: (0, 0)),
                  pl.BlockSpec((D_MODEL, n_k), lambda i, j: (0, 0)),
                  pl.BlockSpec((1, m, 2 * n_k), lambda i, j: (i, 0, 0)),
                  pl.BlockSpec((n_k, D_MODEL), lambda i, j: (0, 0))],
        out_specs=pl.BlockSpec((1, tm, D_MODEL), lambda i, j: (i, j, 0)),
        out_shape=jax.ShapeDtypeStruct((b, s, D_MODEL), F32),
        compiler_params=_cparams(("parallel", "parallel")),
        name="cross_attn",
    )(x3, g, wq, kv, wo)


def _ffn_kernel(x_ref, g_ref, wup_ref, cw_ref, cb_ref, wd_ref, gf_ref, o_ref, tail_ref, acc_ref, *, final):
    @pl.when(pl.program_id(1) == 0)
    def _():
        tail_ref[...] = jnp.zeros_like(tail_ref)

    ts = x_ref.shape[1]
    x = x_ref[0]
    hn = _rms(x, g_ref[...]).astype(BF16)
    acc_ref[...] = x
    for c in range(D_FF // FF_CHUNK):
        cols = slice(c * FF_CHUNK, (c + 1) * FF_CHUNK)
        gcols = slice(D_FF + c * FF_CHUNK, D_FF + (c + 1) * FF_CHUNK)
        u = jnp.dot(hn, wup_ref[:, cols], preferred_element_type=F32)
        v = jnp.dot(hn, wup_ref[:, gcols], preferred_element_type=F32)
        ext = jnp.concatenate([tail_ref[:, cols], u], axis=0)
        tail_ref[:, cols] = u[ts - SUBLANE:, :]
        cw = cw_ref[:, cols]
        y = (cw[2:3] * u + cw[1:2] * pltpu.roll(ext, 1, 0)[SUBLANE:]
             + cw[0:1] * pltpu.roll(ext, 2, 0)[SUBLANE:] + cb_ref[:, cols])
        acc_ref[...] += _dot(jax.nn.gelu(y) * v, wd_ref[cols, :])
    out = acc_ref[...]
    if final:
        out = _rms(out, gf_ref[...])
    o_ref[0] = out


def _ffn(x3, g, wup, cw, cb, wd, gf, final):
    b, s, _ = x3.shape
    ts = TS_FFN
    return pl.pallas_call(
        functools.partial(_ffn_kernel, final=final),
        grid=(b, s // ts),
        in_specs=[pl.BlockSpec((1, ts, D_MODEL), lambda i, j: (i, j, 0)),
                  pl.BlockSpec((1, D_MODEL), lambda i, j: (0, 0)),
                  pl.BlockSpec((D_MODEL, 2 * D_FF), lambda i, j: (0, 0), pipeline_mode=pl.Buffered(1)),
                  pl.BlockSpec((FFN_CONV, D_FF), lambda i, j: (0, 0)),
                  pl.BlockSpec((1, D_FF), lambda i, j: (0, 0)),
                  pl.BlockSpec((D_FF, D_MODEL), lambda i, j: (0, 0), pipeline_mode=pl.Buffered(1)),
                  pl.BlockSpec((1, D_MODEL), lambda i, j: (0, 0))],
        out_specs=pl.BlockSpec((1, ts, D_MODEL), lambda i, j: (i, j, 0)),
        out_shape=jax.ShapeDtypeStruct((b, s, D_MODEL), F32),
        scratch_shapes=[pltpu.VMEM((SUBLANE, D_FF), F32), pltpu.VMEM((ts, D_MODEL), F32)],
        compiler_params=_cparams(("parallel", "arbitrary")),
        name="conv_ffn",
    )(x3, g, wup, cw, cb, wd, gf)


def _inproj_columns():
    starts = np.concatenate([[0], np.cumsum(IN_SPLITS)])
    (p_in, a_q, a_k, a_v, a_r, a_lr, d_q, d_k, d_v, d_b, d_a, d_g,
     n_q, n_kc, n_vc, n_ks, n_vs, n_kw, n_vw, n_g) = [np.arange(starts[i], starts[i + 1])
                                                      for i in range(len(IN_SPLITS))]
    n_q = n_q.reshape(NSA_GROUPS, NSA_HPG, NSA_DH).transpose(1, 0, 2).reshape(-1)
    misc = np.full((ZM_W,), N_IN)
    misc[MISC_LR:MISC_LR + GLA_LOWRANK] = a_lr
    misc[MISC_B:MISC_B + GDN_HEADS] = d_b
    misc[MISC_A:MISC_A + GDN_HEADS] = d_a
    misc[MISC_G:MISC_G + 3 * NSA_HEADS] = n_g
    cols = np.concatenate([p_in, a_q, a_k, a_v, a_r, d_q, d_k, d_v, d_g,
                           n_q, n_ks, n_vs, n_kw, n_vw, n_kc, n_vc, misc])
    assert cols.shape[0] == Z_W
    return cols


def _head_expand(offset, n_heads, width):
    e = np.zeros((ZM_W, n_heads * width), np.float32)
    for h in range(n_heads):
        e[offset + h, h * width:(h + 1) * width] = 1.0
    return e


def _nsa_constants(s):
    n_cmp = s // NSA_CMP_STRIDE - NSA_CMP_LEN // NSA_CMP_STRIDE + 1
    n_slc = s // NSA_SEL_LEN
    c_start = np.arange(n_cmp) * NSA_CMP_STRIDE
    s_start = np.arange(n_slc) * NSA_SEL_LEN
    cover = np.zeros((N_CMP_PAD, N_SLC_PAD), np.float32)
    cover[:n_cmp, :n_slc] = ((c_start[:, None] <= s_start[None, :] + NSA_SEL_LEN - 1)
                             & (c_start[:, None] + NSA_CMP_LEN - 1 >= s_start[None, :]))
    gexp = np.zeros((ZM_W, 3, MIX_W), np.float32)
    for g in range(NSA_GROUPS):
        for j in range(NSA_HPG):
            slot = j * NSA_GROUPS + g
            for c in range(3):
                gexp[MISC_G + (g * NSA_HPG + j) * 3 + c, c, slot * NSA_DH:(slot + 1) * NSA_DH] = 1.0
    return jnp.asarray(cover.T), jnp.asarray(gexp.reshape(ZM_W, 3 * MIX_W), dtype=BF16)


def _block_diag(blocks):
    n, a, b = blocks.shape
    return jnp.einsum('gh,gab->gahb', jnp.eye(n, dtype=blocks.dtype), blocks).reshape(n * a, n * b)


def kernel(x, mem, g_mix, w_in, pool_w, pool_scale, gla_w_lr, gla_b_lr, gla_g_norm, gdn_conv, gdn_a_log,
           gdn_dt_bias, gdn_g_norm, nsa_pe, nsa_cmp_w1, nsa_cmp_w2, w_branch, w_gate, b_gate, w_out, g_cross,
           g_mem, w_xq, w_mem_kv, w_xo, g_ffn, w_up, ffn_conv, ffn_conv_b, w_down, g_final):
    b, s, d = x.shape
    depth = w_in.shape[0]
    t = b * s
    cols = _inproj_columns()
    covt, gexp = _nsa_constants(s)
    eba = jnp.asarray(np.concatenate([_head_expand(MISC_B, GDN_HEADS, GDN_DH),
                                      _head_expand(MISC_A, GDN_HEADS, GDN_DH)], axis=1), dtype=BF16)
    ones_bd = _block_diag(jnp.ones((GDN_HEADS, GDN_DH, GDN_DH), BF16))
    nsa_rows = np.arange(MIX_W).reshape(NSA_GROUPS, NSA_HPG, NSA_DH).transpose(1, 0, 2).reshape(-1)
    row = lambda v: v.reshape(1, -1).astype(F32)

    x2 = x.reshape(t, d)
    for l in range(depth):
        w_in_r = jnp.concatenate([w_in[l], jnp.zeros((d, 1), F32)], axis=1)[:, cols].astype(BF16)
        zp, zg, zd, zn, zc, zm = (z.reshape(b, s, -1) for z in _inproj(x2, row(g_mix[l]), w_in_r))

        o_pool = _pool(zp, _block_diag(pool_w[l]).astype(BF16), row(pool_scale[l]))

        wlr = jnp.zeros((ZM_W, GLA_HEADS * GLA_DK), F32).at[MISC_LR:MISC_LR + GLA_LOWRANK].set(gla_w_lr[l])
        wlr_hi = wlr.astype(BF16)
        wlr2 = jnp.stack([wlr_hi, (wlr - wlr_hi.astype(F32)).astype(BF16)])
        o_gla, o_gdn = _recurrent(zg, zd, zm, wlr2, row(gla_b_lr[l]), row(jnp.tile(gla_g_norm[l], GLA_HEADS)),
                                  gdn_conv[l], eba, row(jnp.repeat(gdn_a_log[l], GDN_DH)),
                                  row(jnp.repeat(gdn_dt_bias[l], GDN_DH)),
                                  row(jnp.tile(gdn_g_norm[l], GDN_HEADS)), ones_bd)

        pe_x = jnp.tile(nsa_pe[l], (1, 1, NSA_GROUPS))
        w1 = nsa_cmp_w1[l].reshape(2, NSA_CMP_LEN, NSA_DH, NSA_DH)
        eye_g = jnp.eye(NSA_GROUPS, dtype=F32)
        w1_bd = jnp.einsum('gh,kpde->kpgdhe', eye_g, w1).reshape(2, NSA_CMP_LEN, LANE, LANE).astype(BF16)
        w2_bd = jnp.einsum('gh,kde->kgdhe', eye_g, nsa_cmp_w2[l]).reshape(2, LANE, LANE).astype(BF16)
        ck, cv = _nsa_compress(zc, pe_x, w1_bd, w2_bd)
        o_nsa = _nsa(zn, zm, ck, cv, covt, gexp)

        wb = jnp.concatenate([w_branch[l, :3], w_branch[l, 3][nsa_rows][None]], axis=0).astype(BF16)
        branches = [o.reshape(t, MIX_W) for o in (o_pool, o_gla, o_gdn, o_nsa)]
        x2 = _combine(x2, row(g_mix[l]), branches, w_gate[l].astype(BF16),
                      b_gate[l].reshape(N_BRANCH, 1, d), wb, w_out[l].astype(BF16))

        kv = _memkv(mem, row(g_mem[l]), w_mem_kv[l].astype(BF16))
        x3 = _cross(x2.reshape(b, s, d), row(g_cross[l]), w_xq[l].astype(BF16), kv, w_xo[l].astype(BF16))

        x3 = _ffn(x3, row(g_ffn[l]), w_up[l].astype(BF16), ffn_conv[l], row(ffn_conv_b[l]),
                  w_down[l].astype(BF16), row(g_final), final=(l == depth - 1))
        x2 = x3.reshape(t, d)
    return x2.reshape(b, s, d)
```

```python
import functools

import numpy as np
import jax
import jax.numpy as jnp
from jax import lax
from jax.experimental import pallas as pl
from jax.experimental.pallas import tpu as pltpu

F32 = jnp.float32
BF16 = jnp.bfloat16
HIGHEST = lax.Precision.HIGHEST

D_MODEL = 1024
MIX_W = 256
POOL_WINDOWS = (2, 4, 8, 16)
POOL_GW = 64
GLA_HEADS = 4
GLA_DK = 32
GLA_DV = 64
GLA_LOWRANK = 16
GLA_GATE_NORM = 16.0
CHUNK = 64
GDN_HEADS = 4
GDN_DH = 64
GDN_CONV = 4
NSA_HEADS = 4
NSA_GROUPS = 2
NSA_HPG = 2
NSA_DH = 64
NSA_KV = 128
NSA_CMP_LEN = 32
NSA_CMP_STRIDE = 16
NSA_SEL_LEN = 64
NSA_N_SEL = 16
NSA_WINDOW = 512
X_HEADS = 4
X_DH = 128
D_FF = 2816
FFN_CONV = 3
EPS = 1e-6
N_BRANCH = 4

IN_SPLITS = (MIX_W,
             128, 128, 256, 256, GLA_LOWRANK,
             MIX_W, MIX_W, MIX_W, GDN_HEADS, GDN_HEADS, MIX_W,
             256, NSA_KV, NSA_KV, NSA_KV, NSA_KV, NSA_KV, NSA_KV, 3 * NSA_HEADS)
N_IN = sum(IN_SPLITS)

MISC_LR = 0
MISC_B = 16
MISC_A = 20
MISC_G = 24
LANE = 128
SUBLANE = 8

Z_WIDTHS = (256, 768, 1024, 768, 256, 128)
Z_DTYPES = (F32, BF16, BF16, BF16, F32, F32)
ZP_W, ZG_W, ZD_W, ZN_W, ZC_W, ZM_W = Z_WIDTHS
Z_W = sum(Z_WIDTHS)
NEG_BIG = -1e30

TM_PROJ = 256
TM_COMB = 256
TM_CROSS = 512
TS_FFN = 512
FF_CHUNK = 256
TQ_NSA = 256
REC_SEQS = 4
N_SLC_PAD = 128
N_CMP_PAD = 128

VMEM_LIMIT = 56 * 1024 * 1024


def _cparams(sem):
    return pltpu.CompilerParams(dimension_semantics=sem, vmem_limit_bytes=VMEM_LIMIT)


def _rms(x, g):
    return x * lax.rsqrt(jnp.mean(x * x, axis=-1, keepdims=True) + EPS) * g


def _dot(a, b):
    return jnp.dot(a.astype(BF16), b.astype(BF16), preferred_element_type=F32)


def _dot_nt(a, b):
    return lax.dot_general(a.astype(BF16), b.astype(BF16), (((1,), (1,)), ((), ())),
                           preferred_element_type=F32)


def _dot_tn(a, b):
    return lax.dot_general(a.astype(BF16), b.astype(BF16), (((0,), (0,)), ((), ())),
                           preferred_element_type=F32)


def _dot_hi(a, b):
    return jnp.dot(a, b, precision=HIGHEST, preferred_element_type=F32)


def _dot_nt_hi(a, b):
    return lax.dot_general(a, b, (((1,), (1,)), ((), ())), precision=HIGHEST,
                           preferred_element_type=F32)


def _split2_dot(a, b):
    hi = a.astype(BF16)
    lo = (a - hi.astype(F32)).astype(BF16)
    return jnp.dot(hi, b, preferred_element_type=F32) + jnp.dot(lo, b, preferred_element_type=F32)


def _split3_rhs_dot(a, b):
    hi = b.astype(BF16)
    r1 = b - hi.astype(F32)
    mid = r1.astype(BF16)
    lo = (r1 - mid.astype(F32)).astype(BF16)
    return ((jnp.dot(a, hi, preferred_element_type=F32) + jnp.dot(a, mid, preferred_element_type=F32))
            + jnp.dot(a, lo, preferred_element_type=F32))


def _dot3(a, w_hi, w_lo):
    a_hi = a.astype(BF16)
    a_lo = (a - a_hi.astype(F32)).astype(BF16)
    return (jnp.dot(a_hi, w_hi, preferred_element_type=F32)
            + (jnp.dot(a_lo, w_hi, preferred_element_type=F32) + jnp.dot(a_hi, w_lo, preferred_element_type=F32)))


def _iota(shape, axis):
    return lax.broadcasted_iota(jnp.int32, shape, axis)


def _block_mask(rows, cols, rb, cb):
    return (_iota((rows, cols), 0) // rb) == (_iota((rows, cols), 1) // cb)


def _shift_rows(x, k):
    t = _iota(x.shape, 0)
    return jnp.where(t >= k, pltpu.roll(x, k, 0), 0.0)


def _cumsum_rows(x):
    k = 1
    while k < x.shape[0]:
        x = x + _shift_rows(x, k)
        k *= 2
    return x


def _softplus(x):
    return jnp.maximum(x, 0.0) + jnp.log1p(jnp.exp(-jnp.abs(x)))


def _log_sigmoid(x):
    return -_softplus(-x)


def _silu(x):
    return x * jax.nn.sigmoid(x)


def _masked_softmax(s, mask):
    s = jnp.where(mask, s, -jnp.inf)
    m = jnp.max(s, axis=-1, keepdims=True)
    m = jnp.where(jnp.isfinite(m), m, 0.0)
    e = jnp.exp(s - m)
    den = jnp.sum(e, axis=-1, keepdims=True)
    return e / jnp.where(den > 0, den, 1.0)


def _tile4(x):
    return jnp.concatenate([x, x, x, x], axis=0)


def _inproj_kernel(x_ref, g_ref, w_ref, *z_refs):
    h = _rms(x_ref[...], g_ref[...]).astype(BF16)
    off = 0
    for ref in z_refs:
        n = ref.shape[-1]
        ref[...] = jnp.dot(h, w_ref[:, off:off + n], preferred_element_type=F32).astype(ref.dtype)
        off += n


def _inproj(x2, g, w):
    t = x2.shape[0]
    return pl.pallas_call(
        _inproj_kernel,
        grid=(t // TM_PROJ,),
        in_specs=[pl.BlockSpec((TM_PROJ, D_MODEL), lambda i: (i, 0)),
                  pl.BlockSpec((1, D_MODEL), lambda i: (0, 0)),
                  pl.BlockSpec((D_MODEL, Z_W), lambda i: (0, 0))],
        out_specs=[pl.BlockSpec((TM_PROJ, n), lambda i: (i, 0)) for n in Z_WIDTHS],
        out_shape=[jax.ShapeDtypeStruct((t, n), dt) for n, dt in zip(Z_WIDTHS, Z_DTYPES)],
        compiler_params=_cparams(("parallel",)),
        name="inproj",
    )(x2, g, w)


def _pool_kernel(u_ref, w_ref, sc_ref, o_ref):
    u = u_ref[0]
    s2 = u + _shift_rows(u, 1)
    s4 = s2 + _shift_rows(s2, 2)
    s8 = s4 + _shift_rows(s4, 4)
    s16 = s8 + _shift_rows(s8, 8)
    grp = _iota(u.shape, 1) // POOL_GW
    win = jnp.where(grp == 0, s2, jnp.where(grp == 1, s4, jnp.where(grp == 2, s8, s16)))
    width = jnp.where(grp == 0, POOL_WINDOWS[0],
                      jnp.where(grp == 1, POOL_WINDOWS[1],
                                jnp.where(grp == 2, POOL_WINDOWS[2], POOL_WINDOWS[3])))
    cnt = jnp.minimum(_iota(u.shape, 0) + 1, width).astype(F32)
    diff = win / cnt - u
    o_ref[0] = (_dot(diff, w_ref[...]) * sc_ref[...]).astype(o_ref.dtype)


def _pool(zp, w_bd, scale):
    b, s, _ = zp.shape
    return pl.pallas_call(
        _pool_kernel,
        grid=(b,),
        in_specs=[pl.BlockSpec((1, s, MIX_W), lambda i: (i, 0, 0)),
                  pl.BlockSpec((MIX_W, MIX_W), lambda i: (0, 0)),
                  pl.BlockSpec((1, MIX_W), lambda i: (0, 0))],
        out_specs=pl.BlockSpec((1, s, MIX_W), lambda i: (i, 0, 0)),
        out_shape=jax.ShapeDtypeStruct((b, s, MIX_W), BF16),
        compiler_params=_cparams(("parallel",)),
        name="pool",
    )(zp, w_bd, scale)


def _gla_masks():
    return (_block_mask(4 * CHUNK, 128, CHUNK, GLA_DK),
            _block_mask(4 * CHUNK, 256, CHUNK, GLA_DV),
            _block_mask(256, 128, GLA_DV, GLA_DK),
            (_iota((CHUNK, 256), 1) % CHUNK) <= _iota((CHUNK, 256), 0))


def _seqs(f, *lists):
    return [f(*args) for args in zip(*lists)]


def _gla_chunk(masks, zg, zm, wlr_ref, blr_ref, gn_ref, ones_ref, st_refs):
    mask_k, mask_v, mask_st, causal = masks
    q = [z[:, 0:128] * GLA_DK ** -0.5 for z in zg]
    k = [z[:, 128:256] for z in zg]
    v = [z[:, 256:512] for z in zg]
    r = [z[:, 512:768] for z in zg]
    pre = _seqs(lambda a: _dot3(a, wlr_ref[0], wlr_ref[1]), zm)
    bc = _seqs(lambda a: _cumsum_rows(_log_sigmoid(a + blr_ref[...]) / GLA_GATE_NORM), pre)
    bl = [a[CHUNK - 1:CHUNK, :] for a in bc]
    q_e = _seqs(lambda a, c: a * jnp.exp(c), q, bc)
    k_e = _seqs(lambda a, c: a * jnp.exp(-c), k, bc)
    k_u = _seqs(lambda a, c, l: a * jnp.exp(l - c), k, bc, bl)
    st = [ref[...] for ref in st_refs]
    att = _seqs(lambda a, b: jnp.where(causal, _dot_nt(a, jnp.where(mask_k, _tile4(b), 0.0)), 0.0), q_e, k_e)
    inter = _seqs(_dot_nt, q_e, st)
    kv = _seqs(_dot_tn, v, k_u)
    o = _seqs(lambda a, b, c: _dot(a, jnp.where(mask_v, _tile4(b), 0.0)) + c, att, v, inter)
    for ref, s_old, l, upd in zip(st_refs, st, bl, kv):
        ref[...] = s_old * jnp.exp(l) + jnp.where(mask_st, upd, 0.0)
    ms = _seqs(lambda a: _split2_dot(a * a, ones_ref[...]) * (1.0 / GLA_DV), o)
    return _seqs(lambda a, m, g: a * lax.rsqrt(m + EPS) * gn_ref[...] * _silu(g), o, ms, r)


GDN_HALO = 16


def _gdn_masks():
    c4 = 4 * CHUNK
    col = _iota((CHUNK, c4), 1) % CHUNK
    row = _iota((CHUNK, c4), 0)
    return (_block_mask(c4, c4, CHUNK, CHUNK), col <= row, col < row, col == row)


def _gdn_chunk(masks, halo, zd, zm, cw_ref, eba_ref, alog_ref, dtb_ref, gn_ref, ones_ref, st_refs):
    mask_bd, incl, strict, diag = masks
    cw = cw_ref[...]
    bd = lambda a: jnp.where(mask_bd, _tile4(a), 0.0)

    def conv_silu(h, z):
        ext = jnp.concatenate([h, z[:, 0:768]], axis=0)
        conv = (cw[3:4] * ext + cw[2:3] * pltpu.roll(ext, 1, 0)
                + cw[1:2] * pltpu.roll(ext, 2, 0) + cw[0:1] * pltpu.roll(ext, 3, 0))
        return _silu(conv[GDN_HALO:, :])

    qkv = _seqs(conv_silu, halo, zd)
    v = [a[:, 512:768] for a in qkv]
    gate = [z[:, 768:1024] for z in zd]
    ssq = _seqs(lambda a: _split2_dot(jnp.concatenate([a[:, 0:256] * a[:, 0:256], a[:, 256:512] * a[:, 256:512]],
                                                      axis=0), ones_ref[...]), qkv)
    q = _seqs(lambda a, s: a[:, 0:256] * lax.rsqrt(s[0:CHUNK] + EPS) * GDN_DH ** -0.5, qkv, ssq)
    k = _seqs(lambda a, s: a[:, 256:512] * lax.rsqrt(s[CHUNK:2 * CHUNK] + EPS), qkv, ssq)
    ba = _seqs(lambda a: _split2_dot(a, eba_ref[...]), zm)
    beta = [jax.nn.sigmoid(a[:, 0:256]) for a in ba]
    gc = _seqs(lambda a: _cumsum_rows(-jnp.exp(alog_ref[...]) * _softplus(a[:, 256:512] + dtb_ref[...])), ba)
    gl = [a[CHUNK - 1:CHUNK, :] for a in gc]
    ones_cc = jnp.ones((CHUNK, CHUNK), BF16)
    g_row = _seqs(lambda a: _split3_rhs_dot(ones_cc, jnp.where(diag, a, 0.0)), gc)
    decay = _seqs(lambda a, b: jnp.exp(jnp.where(incl, a - b, -jnp.inf)), gc, g_row)
    kb = _seqs(lambda a, b: a * b, k, beta)
    vb = _seqs(lambda a, b: a * b, v, beta)
    k_bd = _seqs(bd, k)
    n_mat = _seqs(lambda a, b, d: jnp.where(strict, _dot_nt(a, b) * d, 0.0), kb, k_bd, decay)
    a_qk = _seqs(lambda a, b, d: _dot_nt(a, b) * d, q, k_bd, decay)
    m = _seqs(lambda a: -bd(a), n_mat)
    x = _seqs(lambda a: jnp.where(diag, 1.0, 0.0) - a, n_mat)
    p = 2
    while p < CHUNK:
        m = _seqs(lambda a: _dot(a, a), m)
        x = _seqs(lambda a, b: a + _dot(a, b), x, m)
        p *= 2
    uw = _seqs(lambda a, b, c, g: _dot(a, jnp.concatenate([bd(b), bd(c * jnp.exp(g))], axis=1)),
               x, vb, kb, gc)
    st = [ref[...] for ref in st_refs]
    v_new = _seqs(lambda a, s: a[:, 0:256] - _dot(a[:, 256:512], s), uw, st)
    o = _seqs(lambda a, g, s, qk, vn: _dot(a * jnp.exp(g), s) + _dot(qk, bd(vn)), q, gc, st, a_qk, v_new)
    upd = _seqs(lambda a, g, l, vn: _dot_tn(a * jnp.exp(l - g), vn), k, gc, gl, v_new)
    for ref, s_old, l, u in zip(st_refs, st, gl, upd):
        ref[...] = s_old * jnp.exp(l) + jnp.where(mask_bd, u, 0.0)
    ms = _seqs(lambda a: _split2_dot(a * a, ones_ref[...]) * (1.0 / GDN_DH), o)
    return _seqs(lambda a, m_, g: a * lax.rsqrt(m_ + EPS) * gn_ref[...] * _silu(g), o, ms, gate)


def _rec_kernel(zg_ref, zd_ref, zm_ref, wlr_ref, blr_ref, gng_ref, cw_ref, eba_ref, alog_ref, dtb_ref,
                gnd_ref, ones_ref, og_ref, od_ref, halo_ref, *st_refs):
    nb = zg_ref.shape[0]

    @pl.when(pl.program_id(1) == 0)
    def _():
        halo_ref[...] = jnp.zeros_like(halo_ref)
        for st_ref in st_refs:
            st_ref[...] = jnp.zeros_like(st_ref)

    zm = [zm_ref[i] for i in range(nb)]
    zg = [zg_ref[i].astype(F32) for i in range(nb)]
    zd = [zd_ref[i].astype(F32) for i in range(nb)]
    halo = [halo_ref[i] for i in range(nb)]
    for i in range(nb):
        halo_ref[i] = zd[i][CHUNK - GDN_HALO:, 0:768]
    og = _gla_chunk(_gla_masks(), zg, zm, wlr_ref, blr_ref, gng_ref, ones_ref, st_refs[:nb])
    od = _gdn_chunk(_gdn_masks(), halo, zd, zm, cw_ref, eba_ref, alog_ref, dtb_ref, gnd_ref, ones_ref,
                    st_refs[nb:])
    for i in range(nb):
        og_ref[i] = og[i].astype(og_ref.dtype)
        od_ref[i] = od[i].astype(od_ref.dtype)


def _recurrent(zg, zd, zm, wlr2, blr, gn_gla, cw, eba, alog, dtb, gn_gdn, ones_bd):
    b, s, _ = zg.shape
    nb = REC_SEQS if b % REC_SEQS == 0 else 1
    const = lambda shape: pl.BlockSpec(shape, lambda i, c: (0,) * len(shape))
    return pl.pallas_call(
        _rec_kernel,
        grid=(b // nb, s // CHUNK),
        in_specs=[pl.BlockSpec((nb, CHUNK, ZG_W), lambda i, c: (i, c, 0)),
                  pl.BlockSpec((nb, CHUNK, ZD_W), lambda i, c: (i, c, 0)),
                  pl.BlockSpec((nb, CHUNK, ZM_W), lambda i, c: (i, c, 0)),
                  const((2, ZM_W, 128)), const((1, 128)), const((1, MIX_W)),
                  const((GDN_CONV, 768)), const((ZM_W, 2 * MIX_W)), const((1, MIX_W)), const((1, MIX_W)),
                  const((1, MIX_W)), const((MIX_W, MIX_W))],
        out_specs=[pl.BlockSpec((nb, CHUNK, MIX_W), lambda i, c: (i, c, 0))] * 2,
        out_shape=[jax.ShapeDtypeStruct((b, s, MIX_W), BF16)] * 2,
        scratch_shapes=([pltpu.VMEM((nb, GDN_HALO, 768), F32)] + [pltpu.VMEM((256, 128), F32)] * nb
                        + [pltpu.VMEM((256, 256), F32)] * nb),
        compiler_params=_cparams(("parallel", "arbitrary")),
        name="gla_gdn",
    )(zg, zd, zm, wlr2, blr, gn_gla, cw, eba, alog, dtb, gn_gdn, ones_bd)


def _cmp_kernel(kc_ref, vc_ref, pe_ref, w1_ref, w2_ref, ck_ref, cv_ref):
    n_sub = NSA_CMP_LEN // NSA_CMP_STRIDE
    n_chunks = kc_ref.shape[1] // NSA_CMP_STRIDE
    for which, (src_ref, out_ref) in enumerate(((kc_ref, ck_ref), (vc_ref, cv_ref))):
        parts = [jnp.zeros((n_chunks, LANE), F32) for _ in range(n_sub)]
        for i in range(NSA_CMP_STRIDE):
            slab = src_ref[0, pl.ds(i, n_chunks, stride=NSA_CMP_STRIDE), :]
            for sub in range(n_sub):
                p = sub * NSA_CMP_STRIDE + i
                parts[sub] = parts[sub] + _dot(slab + pe_ref[which, p:p + 1, :], w1_ref[which, p])
        pre = parts[0] + pltpu.roll(parts[1], n_chunks - 1, 0)
        out_ref[0] = _dot(jax.nn.gelu(pre), w2_ref[which])


def _nsa_compress(zc, pe_x, w1_bd, w2_bd):
    b, s, _ = zc.shape
    n_chunks = s // NSA_CMP_STRIDE
    return pl.pallas_call(
        _cmp_kernel,
        grid=(b,),
        in_specs=[pl.BlockSpec((1, s, LANE), lambda i: (i, 0, 0)),
                  pl.BlockSpec((1, s, LANE), lambda i: (i, 0, 1)),
                  pl.BlockSpec((2, NSA_CMP_LEN, LANE), lambda i: (0, 0, 0)),
                  pl.BlockSpec((2, NSA_CMP_LEN, LANE, LANE), lambda i: (0, 0, 0, 0)),
                  pl.BlockSpec((2, LANE, LANE), lambda i: (0, 0, 0))],
        out_specs=[pl.BlockSpec((1, n_chunks, LANE), lambda i: (i, 0, 0))] * 2,
        out_shape=[jax.ShapeDtypeStruct((b, n_chunks, LANE), F32)] * 2,
        compiler_params=_cparams(("parallel",)),
        name="nsa_compress",
    )(zc, zc, pe_x, w1_bd, w2_bd)


def _exp_weights(s, m):
    return jnp.exp((s - m).astype(BF16))


def _normalise_aug(o_aug, in_grp):
    den = pltpu.roll(o_aug, NSA_DH, 1)
    return jnp.where(in_grp, o_aug / jnp.where(den > 0, den, 1.0), 0.0)


def _nsa_kernel(q_ref, ksv_ref, kwv_ref, ck_ref, cv_ref, zm_ref, covt_ref, gexp_ref, o_ref,
                kaug_ref, vaug_ref, vwaug_ref, s_ref, mx_ref, acc_ref):
    tq = q_ref.shape[1]
    s_len = ksv_ref.shape[1]
    n_slc = s_len // NSA_SEL_LEN
    qi = pl.program_id(1)
    s0 = pl.multiple_of(qi * tq, tq)
    lane_grp = _iota((1, LANE), 1) // NSA_DH

    @pl.when(qi == 0)
    def _():
        lane = _iota((s_len, LANE), 1)
        key_blk = _iota((s_len, LANE), 0) // NSA_SEL_LEN
        k = ksv_ref[0, :, 0:LANE].astype(F32)
        v = ksv_ref[0, :, LANE:2 * LANE].astype(F32)
        vw = kwv_ref[0, :, LANE:2 * LANE].astype(F32)
        for g in range(NSA_GROUPS):
            own = lane // NSA_DH == g
            onehot = jnp.where(lane - (1 - g) * NSA_DH == key_blk, 1.0, 0.0)
            kaug_ref[g] = jnp.where(own, k, onehot).astype(BF16)
            vaug_ref[g] = jnp.where(own, v, 1.0).astype(BF16)
            vwaug_ref[g] = jnp.where(own, vw, 1.0).astype(BF16)

    t_col = s0 + _iota((tq, 1), 0)
    t_col2 = jnp.concatenate([t_col, t_col], axis=0)
    t_row = s0 + _iota((1, tq), 1)
    q = q_ref[0].astype(F32) * NSA_DH ** -0.5
    ck = ck_ref[0]
    ck_hi = ck.astype(BF16)
    ck_lo = (ck - ck_hi.astype(F32)).astype(BF16)
    cv = cv_ref[0]
    cmp_end = _iota((1, N_CMP_PAD), 1) * NSA_CMP_STRIDE + (NSA_CMP_LEN - 1)
    cmp_valid = cmp_end <= t_col2
    blk_t = _iota((n_slc, tq), 0)
    cur_t = t_row // NSA_SEL_LEN
    forced_t = (blk_t == 0) | (blk_t == cur_t) | (blk_t == cur_t - 1)
    future_t = blk_t > cur_t
    row_in_tile = jnp.concatenate([_iota((tq, tq), 0)] * NSA_HPG, axis=0)
    col_in_tile = _iota((NSA_HPG * tq, tq), 1)
    diag_ok = col_in_tile <= row_in_tile
    n_wt = NSA_WINDOW // tq + 1
    never = 2 * tq
    win_off = [pl.multiple_of(jnp.maximum(qi - (n_wt - 1 - w), 0) * tq, tq) for w in range(n_wt)]
    win_ok = [col_in_tile > row_in_tile + jnp.where(qi >= n_wt - 1, 0, never)]
    win_ok += [col_in_tile >= jnp.where(qi >= n_wt - 1 - w, 0, never) for w in range(1, n_wt - 1)]
    win_ok += [diag_ok]

    o_cmp = [jnp.zeros((tq, LANE), F32) for _ in range(NSA_HPG)]
    o_slc = [jnp.zeros((tq, LANE), F32) for _ in range(NSA_HPG)]
    o_win = [jnp.zeros((tq, LANE), F32) for _ in range(NSA_HPG)]
    qa = []
    for g in range(NSA_GROUPS):
        in_grp = lane_grp == g
        q2 = jnp.concatenate([jnp.where(in_grp, q[:, j * LANE:(j + 1) * LANE], 0.0)
                              for j in range(NSA_HPG)], axis=0)
        qb = q2.astype(BF16)
        p_c = _masked_softmax(_dot_nt(qb, ck_hi) + _dot_nt(qb, ck_lo), cmp_valid)
        oc = jnp.where(in_grp, _dot(p_c, cv), 0.0)
        imp_c = p_c[0:tq] + p_c[tq:2 * tq]
        imp_hi = imp_c.astype(BF16)
        imp_lo = (imp_c - imp_hi.astype(F32)).astype(BF16)
        imp_t = (_dot_nt(covt_ref[...], imp_hi) + _dot_nt(covt_ref[...], imp_lo))[0:n_slc]
        imp_t = jnp.where(forced_t, jnp.inf, jnp.where(future_t, -jnp.inf, imp_t))
        rank = jnp.zeros((n_slc, tq), jnp.int32)
        for m in range(n_slc):
            other = imp_t[m:m + 1, :]
            ahead = (other > imp_t) | ((other == imp_t) & (blk_t > m))
            rank = rank + ahead.astype(jnp.int32)
        bias_t = jnp.where(rank < NSA_N_SEL, 0.0, NEG_BIG)
        lo = (1 - g) * NSA_DH
        rows = ([jnp.zeros((lo, tq), F32)] if lo else []) + [bias_t, jnp.zeros((LANE - lo - n_slc, tq), F32)]
        bias = jnp.concatenate(rows, axis=0).T
        qa.append(jnp.where(in_grp, q2, jnp.concatenate([bias] * NSA_HPG, axis=0)).astype(BF16))
        s_w = [jnp.where(ok, _dot_nt(qb, kwv_ref[0, pl.ds(off, tq), 0:LANE]), -jnp.inf)
               for ok, off in zip(win_ok, win_off)]
        m_w = s_w[0]
        for s_t in s_w[1:]:
            m_w = jnp.maximum(m_w, s_t)
        m_w = jnp.max(m_w, axis=-1, keepdims=True)
        m_w = jnp.where(jnp.isfinite(m_w), m_w, 0.0)
        ow = jnp.zeros((NSA_HPG * tq, LANE), F32)
        for s_t, off in zip(s_w, win_off):
            ow = ow + jnp.dot(_exp_weights(s_t, m_w), vwaug_ref[g, pl.ds(off, tq), :],
                              preferred_element_type=F32)
        ow = _normalise_aug(ow, in_grp)
        for j in range(NSA_HPG):
            rows_j = slice(j * tq, (j + 1) * tq)
            o_cmp[j] = o_cmp[j] + oc[rows_j]
            o_win[j] = o_win[j] + ow[rows_j]

    groups = range(NSA_GROUPS)
    half_max = lambda s_t: jnp.maximum(s_t[:, 0:LANE], s_t[:, LANE:2 * LANE])
    mx_ref[...] = jnp.full(mx_ref.shape, NEG_BIG, F32)

    def score_tiles(tiles):
        offs = [pl.multiple_of(kt * tq, tq) for kt in tiles]
        s_new = [[_dot_nt(qa[g], kaug_ref[g, pl.ds(off, tq), :]) for g in groups] for off in offs]
        for kt, s_kt in zip(tiles, s_new):
            for g in groups:
                s_ref[g, kt] = s_kt[g]
        for g in groups:
            m_new = half_max(s_new[0][g])
            for s_kt in s_new[1:]:
                m_new = jnp.maximum(m_new, half_max(s_kt[g]))
            mx_ref[g] = jnp.maximum(mx_ref[g], m_new)

    def score_pair(p, carry):
        score_tiles([2 * p, 2 * p + 1])
        return carry

    lax.fori_loop(0, qi // 2, score_pair, 0)

    @pl.when(qi % 2 == 1)
    def _():
        score_tiles([qi - 1])

    m_s = []
    for g in groups:
        s_d = jnp.where(diag_ok, _dot_nt(qa[g], kaug_ref[g, pl.ds(s0, tq), :]), NEG_BIG)
        s_ref[g, qi] = s_d
        m_g = jnp.max(jnp.maximum(mx_ref[g], half_max(s_d)), axis=-1, keepdims=True)
        m_s.append(jnp.where(jnp.isfinite(m_g), m_g, 0.0))
    acc_ref[...] = jnp.zeros(acc_ref.shape, F32)

    def attend_tiles(tiles):
        offs = [pl.multiple_of(kt * tq, tq) for kt in tiles]
        pv = [[jnp.dot(_exp_weights(s_ref[g, kt], m_s[g]), vaug_ref[g, pl.ds(off, tq), :],
                       preferred_element_type=F32) for g in groups] for kt, off in zip(tiles, offs)]
        for g in groups:
            upd = pv[0][g]
            for pv_kt in pv[1:]:
                upd = upd + pv_kt[g]
            acc_ref[g] += upd

    def attend_pair(p, carry):
        attend_tiles([2 * p, 2 * p + 1])
        return carry

    lax.fori_loop(0, (qi + 1) // 2, attend_pair, 0)

    @pl.when(qi % 2 == 0)
    def _():
        attend_tiles([qi])

    for g in groups:
        os_ = _normalise_aug(acc_ref[g], lane_grp == g)
        for j in range(NSA_HPG):
            o_slc[j] = o_slc[j] + os_[j * tq:(j + 1) * tq]

    gates = _split2_dot(jax.nn.sigmoid(zm_ref[0]), gexp_ref[...])
    o = (gates[:, 0:MIX_W] * jnp.concatenate(o_cmp, axis=1)
         + gates[:, MIX_W:2 * MIX_W] * jnp.concatenate(o_slc, axis=1)
         + gates[:, 2 * MIX_W:3 * MIX_W] * jnp.concatenate(o_win, axis=1))
    o_ref[0] = o.astype(o_ref.dtype)


def _nsa(zn, zm, ck, cv, covt, gexp):
    b, s, _ = zn.shape
    tq = TQ_NSA
    return pl.pallas_call(
        _nsa_kernel,
        grid=(b, s // tq),
        in_specs=[pl.BlockSpec((1, tq, 256), lambda i, j: (i, j, 0)),
                  pl.BlockSpec((1, s, 256), lambda i, j: (i, 0, 1)),
                  pl.BlockSpec((1, s, 256), lambda i, j: (i, 0, 2)),
                  pl.BlockSpec((1, N_CMP_PAD, LANE), lambda i, j: (i, 0, 0)),
                  pl.BlockSpec((1, N_CMP_PAD, LANE), lambda i, j: (i, 0, 0)),
                  pl.BlockSpec((1, tq, ZM_W), lambda i, j: (i, j, 0)),
                  pl.BlockSpec((N_SLC_PAD, N_CMP_PAD), lambda i, j: (0, 0)),
                  pl.BlockSpec((ZM_W, 3 * MIX_W), lambda i, j: (0, 0))],
        out_specs=pl.BlockSpec((1, tq, MIX_W), lambda i, j: (i, j, 0)),
        out_shape=jax.ShapeDtypeStruct((b, s, MIX_W), BF16),
        scratch_shapes=[pltpu.VMEM((NSA_GROUPS, s, LANE), BF16)] * 3
                       + [pltpu.VMEM((NSA_GROUPS, s // tq, NSA_HPG * tq, tq), F32),
                          pltpu.VMEM((NSA_GROUPS, NSA_HPG * tq, LANE), F32),
                          pltpu.VMEM((NSA_GROUPS, NSA_HPG * tq, LANE), F32)],
        compiler_params=_cparams(("parallel", "arbitrary")),
        name="nsa_attn",
    )(zn, zn, zn, ck, cv, zm, covt, gexp)


def _combine_kernel(x_ref, g_ref, op_ref, oa_ref, od_ref, on_ref, wg_ref, bg_ref, wb_ref, wo_ref, o_ref):
    x = x_ref[...]
    h = _rms(x, g_ref[...]).astype(BF16)
    y = jnp.zeros(x.shape, F32)
    for i, br_ref in enumerate((op_ref, oa_ref, od_ref, on_ref)):
        gate = jax.nn.sigmoid(jnp.dot(h, wg_ref[i], preferred_element_type=F32) + bg_ref[i])
        y = y + gate * jnp.dot(br_ref[...], wb_ref[i], preferred_element_type=F32)
    o_ref[...] = x + _dot(y, wo_ref[...])


def _combine(x2, g, branches, wg, bg, wb, wo):
    t = x2.shape[0]
    tm = TM_COMB
    return pl.pallas_call(
        _combine_kernel,
        grid=(t // tm,),
        in_specs=[pl.BlockSpec((tm, D_MODEL), lambda i: (i, 0)),
                  pl.BlockSpec((1, D_MODEL), lambda i: (0, 0))]
                 + [pl.BlockSpec((tm, MIX_W), lambda i: (i, 0))] * N_BRANCH
                 + [pl.BlockSpec((N_BRANCH, D_MODEL, D_MODEL), lambda i: (0, 0, 0)),
                    pl.BlockSpec((N_BRANCH, 1, D_MODEL), lambda i: (0, 0, 0)),
                    pl.BlockSpec((N_BRANCH, MIX_W, D_MODEL), lambda i: (0, 0, 0)),
                    pl.BlockSpec((D_MODEL, D_MODEL), lambda i: (0, 0))],
        out_specs=pl.BlockSpec((tm, D_MODEL), lambda i: (i, 0)),
        out_shape=jax.ShapeDtypeStruct((t, D_MODEL), F32),
        compiler_params=_cparams(("parallel",)),
        name="combine",
    )(x2, g, *branches, wg, bg, wb, wo)


def _memkv_kernel(m_ref, g_ref, w_ref, o_ref):
    o_ref[0] = _dot(_rms(m_ref[0], g_ref[...]), w_ref[...]).astype(o_ref.dtype)


def _memkv(mem, g, w):
    b, m, _ = mem.shape
    n = 2 * X_HEADS * X_DH
    return pl.pallas_call(
        _memkv_kernel,
        grid=(b,),
        in_specs=[pl.BlockSpec((1, m, D_MODEL), lambda i: (i, 0, 0)),
                  pl.BlockSpec((1, D_MODEL), lambda i: (0, 0)),
                  pl.BlockSpec((D_MODEL, n), lambda i: (0, 0))],
        out_specs=pl.BlockSpec((1, m, n), lambda i: (i, 0, 0)),
        out_shape=jax.ShapeDtypeStruct((b, m, n), BF16),
        compiler_params=_cparams(("parallel",)),
        name="mem_kv",
    )(mem, g, w)


def _cross_kernel(x_ref, g_ref, wq_ref, kv_ref, wo_ref, o_ref):
    x = x_ref[0]
    hn = _rms(x, g_ref[...])
    q = _dot(hn, wq_ref[...])
    n_k = X_HEADS * X_DH
    outs = []
    for h in range(X_HEADS):
        k_h = kv_ref[0, :, h * X_DH:(h + 1) * X_DH]
        v_h = kv_ref[0, :, n_k + h * X_DH:n_k + (h + 1) * X_DH]
        sc = _dot_nt(q[:, h * X_DH:(h + 1) * X_DH], k_h) * X_DH ** -0.5
        e = jnp.exp(sc - jnp.max(sc, axis=-1, keepdims=True))
        p = e / jnp.sum(e, axis=-1, keepdims=True)
        outs.append(_dot(p, v_h))
    o_ref[0] = x + _dot(jnp.concatenate(outs, axis=1), wo_ref[...])


def _cross(x3, g, wq, kv, wo):
    b, s, _ = x3.shape
    tm = TM_CROSS
    m = kv.shape[1]
    n_k = X_HEADS * X_DH
    return pl.pallas_call(
        _cross_kernel,
        grid=(b, s // tm),
        in_specs=[pl.BlockSpec((1, tm, D_MODEL), lambda i, j: (i, j, 0)),
                  pl.BlockSpec((1, D_MODEL), lambda i, j: (0, 0)),
                  pl.BlockSpec((D_MODEL, n_k), lambda i, j: (0, 0)),
                  pl.BlockSpec((1, m, 2 * n_k), lambda i, j: (i, 0, 0)),
                  pl.BlockSpec((n_k, D_MODEL), lambda i, j: (0, 0))],
        out_specs=pl.BlockSpec((1, tm, D_MODEL), lambda i, j: (i, j, 0)),
        out_shape=jax.ShapeDtypeStruct((b, s, D_MODEL), F32),
        compiler_params=_cparams(("parallel", "parallel")),
        name="cross_attn",
    )(x3, g, wq, kv, wo)


def _ffn_kernel(x_ref, g_ref, wup_ref, cw_ref, cb_ref, wd_ref, gf_ref, o_ref, tail_ref, acc_ref, *, final):
    @pl.when(pl.program_id(1) == 0)
    def _():
        tail_ref[...] = jnp.zeros_like(tail_ref)

    ts = x_ref.shape[1]
    x = x_ref[0]
    hn = _rms(x, g_ref[...]).astype(BF16)
    acc_ref[...] = x
    for c in range(D_FF // FF_CHUNK):
        cols = slice(c * FF_CHUNK, (c + 1) * FF_CHUNK)
        gcols = slice(D_FF + c * FF_CHUNK, D_FF + (c + 1) * FF_CHUNK)
        u = jnp.dot(hn, wup_ref[:, cols], preferred_element_type=F32)
        v = jnp.dot(hn, wup_ref[:, gcols], preferred_element_type=F32)
        ext = jnp.concatenate([tail_ref[:, cols], u], axis=0)
        tail_ref[:, cols] = u[ts - SUBLANE:, :]
        cw = cw_ref[:, cols]
        y = (cw[2:3] * u + cw[1:2] * pltpu.roll(ext, 1, 0)[SUBLANE:]
             + cw[0:1] * pltpu.roll(ext, 2, 0)[SUBLANE:] + cb_ref[:, cols])
        acc_ref[...] += _dot(jax.nn.gelu(y) * v, wd_ref[cols, :])
    out = acc_ref[...]
    if final:
        out = _rms(out, gf_ref[...])
    o_ref[0] = out


def _ffn(x3, g, wup, cw, cb, wd, gf, final):
    b, s, _ = x3.shape
    ts = TS_FFN
    return pl.pallas_call(
        functools.partial(_ffn_kernel, final=final),
        grid=(b, s // ts),
        in_specs=[pl.BlockSpec((1, ts, D_MODEL), lambda i, j: (i, j, 0)),
                  pl.BlockSpec((1, D_MODEL), lambda i, j: (0, 0)),
                  pl.BlockSpec((D_MODEL, 2 * D_FF), lambda i, j: (0, 0), pipeline_mode=pl.Buffered(1)),
                  pl.BlockSpec((FFN_CONV, D_FF), lambda i, j: (0, 0)),
                  pl.BlockSpec((1, D_FF), lambda i, j: (0, 0)),
                  pl.BlockSpec((D_FF, D_MODEL), lambda i, j: (0, 0), pipeline_mode=pl.Buffered(1)),
                  pl.BlockSpec((1, D_MODEL), lambda i, j: (0, 0))],
        out_specs=pl.BlockSpec((1, ts, D_MODEL), lambda i, j: (i, j, 0)),
        out_shape=jax.ShapeDtypeStruct((b, s, D_MODEL), F32),
        scratch_shapes=[pltpu.VMEM((SUBLANE, D_FF), F32), pltpu.VMEM((ts, D_MODEL), F32)],
        compiler_params=_cparams(("parallel", "arbitrary")),
        name="conv_ffn",
    )(x3, g, wup, cw, cb, wd, gf)


def _inproj_columns():
    starts = np.concatenate([[0], np.cumsum(IN_SPLITS)])
    (p_in, a_q, a_k, a_v, a_r, a_lr, d_q, d_k, d_v, d_b, d_a, d_g,
     n_q, n_kc, n_vc, n_ks, n_vs, n_kw, n_vw, n_g) = [np.arange(starts[i], starts[i + 1])
                                                      for i in range(len(IN_SPLITS))]
    n_q = n_q.reshape(NSA_GROUPS, NSA_HPG, NSA_DH).transpose(1, 0, 2).reshape(-1)
    misc = np.full((ZM_W,), N_IN)
    misc[MISC_LR:MISC_LR + GLA_LOWRANK] = a_lr
    misc[MISC_B:MISC_B + GDN_HEADS] = d_b
    misc[MISC_A:MISC_A + GDN_HEADS] = d_a
    misc[MISC_G:MISC_G + 3 * NSA_HEADS] = n_g
    cols = np.concatenate([p_in, a_q, a_k, a_v, a_r, d_q, d_k, d_v, d_g,
                           n_q, n_ks, n_vs, n_kw, n_vw, n_kc, n_vc, misc])
    assert cols.shape[0] == Z_W
    return cols


def _head_expand(offset, n_heads, width):
    e = np.zeros((ZM_W, n_heads * width), np.float32)
    for h in range(n_heads):
        e[offset + h, h * width:(h + 1) * width] = 1.0
    return e


def _nsa_constants(s):
    n_cmp = s // NSA_CMP_STRIDE - NSA_CMP_LEN // NSA_CMP_STRIDE + 1
    n_slc = s // NSA_SEL_LEN
    c_start = np.arange(n_cmp) * NSA_CMP_STRIDE
    s_start = np.arange(n_slc) * NSA_SEL_LEN
    cover = np.zeros((N_CMP_PAD, N_SLC_PAD), np.float32)
    cover[:n_cmp, :n_slc] = ((c_start[:, None] <= s_start[None, :] + NSA_SEL_LEN - 1)
                             & (c_start[:, None] + NSA_CMP_LEN - 1 >= s_start[None, :]))
    gexp = np.zeros((ZM_W, 3, MIX_W), np.float32)
    for g in range(NSA_GROUPS):
        for j in range(NSA_HPG):
            slot = j * NSA_GROUPS + g
            for c in range(3):
                gexp[MISC_G + (g * NSA_HPG + j) * 3 + c, c, slot * NSA_DH:(slot + 1) * NSA_DH] = 1.0
    return jnp.asarray(cover.T, dtype=BF16), jnp.asarray(gexp.reshape(ZM_W, 3 * MIX_W), dtype=BF16)


def _block_diag(blocks):
    n, a, b = blocks.shape
    return jnp.einsum('gh,gab->gahb', jnp.eye(n, dtype=blocks.dtype), blocks).reshape(n * a, n * b)


def kernel(x, mem, g_mix, w_in, pool_w, pool_scale, gla_w_lr, gla_b_lr, gla_g_norm, gdn_conv, gdn_a_log,
           gdn_dt_bias, gdn_g_norm, nsa_pe, nsa_cmp_w1, nsa_cmp_w2, w_branch, w_gate, b_gate, w_out, g_cross,
           g_mem, w_xq, w_mem_kv, w_xo, g_ffn, w_up, ffn_conv, ffn_conv_b, w_down, g_final):
    b, s, d = x.shape
    depth = w_in.shape[0]
    t = b * s
    cols = _inproj_columns()
    covt, gexp = _nsa_constants(s)
    eba = jnp.asarray(np.concatenate([_head_expand(MISC_B, GDN_HEADS, GDN_DH),
                                      _head_expand(MISC_A, GDN_HEADS, GDN_DH)], axis=1), dtype=BF16)
    ones_bd = _block_diag(jnp.ones((GDN_HEADS, GDN_DH, GDN_DH), BF16))
    nsa_rows = np.arange(MIX_W).reshape(NSA_GROUPS, NSA_HPG, NSA_DH).transpose(1, 0, 2).reshape(-1)
    row = lambda v: v.reshape(1, -1).astype(F32)

    x2 = x.reshape(t, d)
    for l in range(depth):
        w_in_r = jnp.concatenate([w_in[l], jnp.zeros((d, 1), F32)], axis=1)[:, cols].astype(BF16)
        zp, zg, zd, zn, zc, zm = (z.reshape(b, s, -1) for z in _inproj(x2, row(g_mix[l]), w_in_r))

        o_pool = _pool(zp, _block_diag(pool_w[l]).astype(BF16), row(pool_scale[l]))

        wlr = jnp.zeros((ZM_W, GLA_HEADS * GLA_DK), F32).at[MISC_LR:MISC_LR + GLA_LOWRANK].set(gla_w_lr[l])
        wlr_hi = wlr.astype(BF16)
        wlr2 = jnp.stack([wlr_hi, (wlr - wlr_hi.astype(F32)).astype(BF16)])
        o_gla, o_gdn = _recurrent(zg, zd, zm, wlr2, row(gla_b_lr[l]), row(jnp.tile(gla_g_norm[l], GLA_HEADS)),
                                  gdn_conv[l], eba, row(jnp.repeat(gdn_a_log[l], GDN_DH)),
                                  row(jnp.repeat(gdn_dt_bias[l], GDN_DH)),
                                  row(jnp.tile(gdn_g_norm[l], GDN_HEADS)), ones_bd)

        pe_x = jnp.tile(nsa_pe[l], (1, 1, NSA_GROUPS))
        w1 = nsa_cmp_w1[l].reshape(2, NSA_CMP_LEN, NSA_DH, NSA_DH)
        eye_g = jnp.eye(NSA_GROUPS, dtype=F32)
        w1_bd = jnp.einsum('gh,kpde->kpgdhe', eye_g, w1).reshape(2, NSA_CMP_LEN, LANE, LANE).astype(BF16)
        w2_bd = jnp.einsum('gh,kde->kgdhe', eye_g, nsa_cmp_w2[l]).reshape(2, LANE, LANE).astype(BF16)
        ck, cv = _nsa_compress(zc, pe_x, w1_bd, w2_bd)
        o_nsa = _nsa(zn, zm, ck, cv, covt, gexp)

        wb = jnp.concatenate([w_branch[l, :3], w_branch[l, 3][nsa_rows][None]], axis=0).astype(BF16)
        branches = [o.reshape(t, MIX_W) for o in (o_pool, o_gla, o_gdn, o_nsa)]
        x2 = _combine(x2, row(g_mix[l]), branches, w_gate[l].astype(BF16),
                      b_gate[l].reshape(N_BRANCH, 1, d), wb, w_out[l].astype(BF16))

        kv = _memkv(mem, row(g_mem[l]), w_mem_kv[l].astype(BF16))
        x3 = _cross(x2.reshape(b, s, d), row(g_cross[l]), w_xq[l].astype(BF16), kv, w_xo[l].astype(BF16))

        x3 = _ffn(x3, row(g_ffn[l]), w_up[l].astype(BF16), ffn_conv[l], row(ffn_conv_b[l]),
                  w_down[l].astype(BF16), row(g_final), final=(l == depth - 1))
        x2 = x3.reshape(t, d)
    return x2.reshape(b, s, d)
```

```python
import functools

import numpy as np
import jax
import jax.numpy as jnp
from jax import lax
from jax.experimental import pallas as pl
from jax.experimental.pallas import tpu as pltpu

F32 = jnp.float32
BF16 = jnp.bfloat16
HIGHEST = lax.Precision.HIGHEST

D_MODEL = 1024
MIX_W = 256
POOL_WINDOWS = (2, 4, 8, 16)
POOL_GW = 64
GLA_HEADS = 4
GLA_DK = 32
GLA_DV = 64
GLA_LOWRANK = 16
GLA_GATE_NORM = 16.0
CHUNK = 64
GDN_HEADS = 4
GDN_DH = 64
GDN_CONV = 4
NSA_HEADS = 4
NSA_GROUPS = 2
NSA_HPG = 2
NSA_DH = 64
NSA_KV = 128
NSA_CMP_LEN = 32
NSA_CMP_STRIDE = 16
NSA_SEL_LEN = 64
NSA_N_SEL = 16
NSA_WINDOW = 512
X_HEADS = 4
X_DH = 128
D_FF = 2816
FFN_CONV = 3
EPS = 1e-6
N_BRANCH = 4

IN_SPLITS = (MIX_W,
             128, 128, 256, 256, GLA_LOWRANK,
             MIX_W, MIX_W, MIX_W, GDN_HEADS, GDN_HEADS, MIX_W,
             256, NSA_KV, NSA_KV, NSA_KV, NSA_KV, NSA_KV, NSA_KV, 3 * NSA_HEADS)
N_IN = sum(IN_SPLITS)

MISC_LR = 0
MISC_B = 16
MISC_A = 20
MISC_G = 24
LANE = 128
SUBLANE = 8

Z_WIDTHS = (256, 768, 1024, 768, 256, 128)
Z_DTYPES = (F32, BF16, BF16, BF16, F32, F32)
ZP_W, ZG_W, ZD_W, ZN_W, ZC_W, ZM_W = Z_WIDTHS
Z_W = sum(Z_WIDTHS)
NEG_BIG = -1e30

TM_PROJ = 256
TM_COMB = 256
TM_CROSS = 512
TS_FFN = 512
FF_CHUNK = 256
TQ_NSA = 256
REC_SEQS = 8
N_SLC_PAD = 128
N_CMP_PAD = 128

VMEM_LIMIT = 56 * 1024 * 1024


def _cparams(sem):
    return pltpu.CompilerParams(dimension_semantics=sem, vmem_limit_bytes=VMEM_LIMIT)


def _rms(x, g):
    return x * lax.rsqrt(jnp.mean(x * x, axis=-1, keepdims=True) + EPS) * g


def _dot(a, b):
    return jnp.dot(a.astype(BF16), b.astype(BF16), preferred_element_type=F32)


def _dot_nt(a, b):
    return lax.dot_general(a.astype(BF16), b.astype(BF16), (((1,), (1,)), ((), ())),
                           preferred_element_type=F32)


def _dot_tn(a, b):
    return lax.dot_general(a.astype(BF16), b.astype(BF16), (((0,), (0,)), ((), ())),
                           preferred_element_type=F32)


def _dot_hi(a, b):
    return jnp.dot(a, b, precision=HIGHEST, preferred_element_type=F32)


def _dot_nt_hi(a, b):
    return lax.dot_general(a, b, (((1,), (1,)), ((), ())), precision=HIGHEST,
                           preferred_element_type=F32)


def _split2_dot(a, b):
    hi = a.astype(BF16)
    lo = (a - hi.astype(F32)).astype(BF16)
    return jnp.dot(hi, b, preferred_element_type=F32) + jnp.dot(lo, b, preferred_element_type=F32)


def _split3_rhs_dot(a, b):
    hi = b.astype(BF16)
    r1 = b - hi.astype(F32)
    mid = r1.astype(BF16)
    lo = (r1 - mid.astype(F32)).astype(BF16)
    return ((jnp.dot(a, hi, preferred_element_type=F32) + jnp.dot(a, mid, preferred_element_type=F32))
            + jnp.dot(a, lo, preferred_element_type=F32))


def _split3_lhs_dot(a, b):
    hi = a.astype(BF16)
    r1 = a - hi.astype(F32)
    mid = r1.astype(BF16)
    lo = (r1 - mid.astype(F32)).astype(BF16)
    return ((jnp.dot(hi, b, preferred_element_type=F32) + jnp.dot(mid, b, preferred_element_type=F32))
            + jnp.dot(lo, b, preferred_element_type=F32))


def _dot3(a, w_hi, w_lo):
    a_hi = a.astype(BF16)
    a_lo = (a - a_hi.astype(F32)).astype(BF16)
    return (jnp.dot(a_hi, w_hi, preferred_element_type=F32)
            + (jnp.dot(a_lo, w_hi, preferred_element_type=F32) + jnp.dot(a_hi, w_lo, preferred_element_type=F32)))


def _iota(shape, axis):
    return lax.broadcasted_iota(jnp.int32, shape, axis)


def _block_mask(rows, cols, rb, cb):
    return (_iota((rows, cols), 0) // rb) == (_iota((rows, cols), 1) // cb)


def _shift_rows(x, k):
    t = _iota(x.shape, 0)
    return jnp.where(t >= k, pltpu.roll(x, k, 0), 0.0)


def _cumsum_rows(x):
    k = 1
    while k < x.shape[0]:
        x = x + _shift_rows(x, k)
        k *= 2
    return x


def _softplus(x):
    return jnp.maximum(x, 0.0) + jnp.log1p(jnp.exp(-jnp.abs(x)))


def _log_sigmoid(x):
    return -_softplus(-x)


def _silu(x):
    return x * jax.nn.sigmoid(x)


def _masked_softmax(s, mask):
    s = jnp.where(mask, s, -jnp.inf)
    m = jnp.max(s, axis=-1, keepdims=True)
    m = jnp.where(jnp.isfinite(m), m, 0.0)
    e = jnp.exp(s - m)
    den = jnp.sum(e, axis=-1, keepdims=True)
    return e / jnp.where(den > 0, den, 1.0)


def _tile4(x):
    return jnp.concatenate([x, x, x, x], axis=0)


def _inproj_kernel(x_ref, g_ref, w_ref, *z_refs):
    h = _rms(x_ref[...], g_ref[...]).astype(BF16)
    off = 0
    for ref in z_refs:
        n = ref.shape[-1]
        ref[...] = jnp.dot(h, w_ref[:, off:off + n], preferred_element_type=F32).astype(ref.dtype)
        off += n


def _inproj(x2, g, w):
    t = x2.shape[0]
    return pl.pallas_call(
        _inproj_kernel,
        grid=(t // TM_PROJ,),
        in_specs=[pl.BlockSpec((TM_PROJ, D_MODEL), lambda i: (i, 0)),
                  pl.BlockSpec((1, D_MODEL), lambda i: (0, 0)),
                  pl.BlockSpec((D_MODEL, Z_W), lambda i: (0, 0))],
        out_specs=[pl.BlockSpec((TM_PROJ, n), lambda i: (i, 0)) for n in Z_WIDTHS],
        out_shape=[jax.ShapeDtypeStruct((t, n), dt) for n, dt in zip(Z_WIDTHS, Z_DTYPES)],
        compiler_params=_cparams(("parallel",)),
        name="inproj",
    )(x2, g, w)


def _pool_kernel(u_ref, w_ref, sc_ref, o_ref):
    u = u_ref[0]
    s2 = u + _shift_rows(u, 1)
    s4 = s2 + _shift_rows(s2, 2)
    s8 = s4 + _shift_rows(s4, 4)
    s16 = s8 + _shift_rows(s8, 8)
    grp = _iota(u.shape, 1) // POOL_GW
    win = jnp.where(grp == 0, s2, jnp.where(grp == 1, s4, jnp.where(grp == 2, s8, s16)))
    width = jnp.where(grp == 0, POOL_WINDOWS[0],
                      jnp.where(grp == 1, POOL_WINDOWS[1],
                                jnp.where(grp == 2, POOL_WINDOWS[2], POOL_WINDOWS[3])))
    cnt = jnp.minimum(_iota(u.shape, 0) + 1, width).astype(F32)
    diff = win / cnt - u
    o_ref[0] = (_dot(diff, w_ref[...]) * sc_ref[...]).astype(o_ref.dtype)


def _pool(zp, w_bd, scale):
    b, s, _ = zp.shape
    return pl.pallas_call(
        _pool_kernel,
        grid=(b,),
        in_specs=[pl.BlockSpec((1, s, MIX_W), lambda i: (i, 0, 0)),
                  pl.BlockSpec((MIX_W, MIX_W), lambda i: (0, 0)),
                  pl.BlockSpec((1, MIX_W), lambda i: (0, 0))],
        out_specs=pl.BlockSpec((1, s, MIX_W), lambda i: (i, 0, 0)),
        out_shape=jax.ShapeDtypeStruct((b, s, MIX_W), BF16),
        compiler_params=_cparams(("parallel",)),
        name="pool",
    )(zp, w_bd, scale)


def _gla_masks():
    return (_block_mask(4 * CHUNK, 128, CHUNK, GLA_DK),
            _block_mask(4 * CHUNK, 256, CHUNK, GLA_DV),
            _block_mask(256, 128, GLA_DV, GLA_DK),
            (_iota((CHUNK, 256), 1) % CHUNK) <= _iota((CHUNK, 256), 0))


def _seqs(f, *lists):
    return [f(*args) for args in zip(*lists)]


def _gla_chunk(masks, zg, zm, wlr_ref, blr_ref, gn_ref, ones_ref, st_refs):
    mask_k, mask_v, mask_st, causal = masks
    q = [z[:, 0:128] * GLA_DK ** -0.5 for z in zg]
    k = [z[:, 128:256] for z in zg]
    v = [z[:, 256:512] for z in zg]
    r = [z[:, 512:768] for z in zg]
    pre = _seqs(lambda a: _dot3(a, wlr_ref[0], wlr_ref[1]), zm)
    bc = _seqs(lambda a: _cumsum_rows(_log_sigmoid(a + blr_ref[...]) / GLA_GATE_NORM), pre)
    bl = [a[CHUNK - 1:CHUNK, :] for a in bc]
    q_e = _seqs(lambda a, c: a * jnp.exp(c), q, bc)
    k_e = _seqs(lambda a, c: a * jnp.exp(-c), k, bc)
    k_u = _seqs(lambda a, c, l: a * jnp.exp(l - c), k, bc, bl)
    st = [ref[...] for ref in st_refs]
    att = _seqs(lambda a, b: jnp.where(causal, _dot_nt(a, jnp.where(mask_k, _tile4(b), 0.0)), 0.0), q_e, k_e)
    inter = _seqs(_dot_nt, q_e, st)
    kv = _seqs(_dot_tn, v, k_u)
    o = _seqs(lambda a, b, c: _dot(a, _tile4(b.astype(BF16)) * ones_ref[...]) + c, att, v, inter)
    for ref, s_old, l, upd in zip(st_refs, st, bl, kv):
        ref[...] = s_old * jnp.exp(l) + jnp.where(mask_st, upd, 0.0)
    ms = _seqs(lambda a: _split2_dot(a * a, ones_ref[...]) * (1.0 / GLA_DV), o)
    return _seqs(lambda a, m, g: a * lax.rsqrt(m + EPS) * gn_ref[...] * _silu(g), o, ms, r)


GDN_HALO = 16


def _gdn_masks():
    c4 = 4 * CHUNK
    col = _iota((CHUNK, c4), 1) % CHUNK
    row = _iota((CHUNK, c4), 0)
    return (_block_mask(c4, c4, CHUNK, CHUNK), col <= row, col < row, col == row)


def _gdn_chunk(masks, halo, zd, zm, cw_ref, eba_ref, alog_ref, dtb_ref, gn_ref, ones_ref, st_refs):
    mask_bd, incl, strict, diag = masks
    cw = cw_ref[...]
    bd = lambda a: _tile4(a.astype(BF16)) * ones_ref[...]

    def conv_silu(h, z):
        ext = jnp.concatenate([h, z[:, 0:768]], axis=0)
        conv = (cw[3:4] * ext + cw[2:3] * pltpu.roll(ext, 1, 0)
                + cw[1:2] * pltpu.roll(ext, 2, 0) + cw[0:1] * pltpu.roll(ext, 3, 0))
        return _silu(conv[GDN_HALO:, :])

    qkv = _seqs(conv_silu, halo, zd)
    v = [a[:, 512:768] for a in qkv]
    gate = [z[:, 768:1024] for z in zd]
    ssq = _seqs(lambda a: _split2_dot(jnp.concatenate([a[:, 0:256] * a[:, 0:256], a[:, 256:512] * a[:, 256:512]],
                                                      axis=0), ones_ref[...]), qkv)
    q = _seqs(lambda a, s: a[:, 0:256] * lax.rsqrt(s[0:CHUNK] + EPS) * GDN_DH ** -0.5, qkv, ssq)
    k = _seqs(lambda a, s: a[:, 256:512] * lax.rsqrt(s[CHUNK:2 * CHUNK] + EPS), qkv, ssq)
    beta = _seqs(lambda a: _split2_dot(jax.nn.sigmoid(a), eba_ref[:, 0:MIX_W]), zm)
    gc = _seqs(lambda a: _split3_lhs_dot(_cumsum_rows(-jnp.exp(alog_ref[...]) * _softplus(a + dtb_ref[...])),
                                         eba_ref[:, MIX_W:2 * MIX_W]), zm)
    gl = [a[CHUNK - 1:CHUNK, :] for a in gc]
    ones_cc = jnp.ones((CHUNK, CHUNK), BF16)
    g_row = _seqs(lambda a: _split3_rhs_dot(ones_cc, jnp.where(diag, a, 0.0)), gc)
    decay = _seqs(lambda a, b: jnp.exp(jnp.where(incl, a - b, -jnp.inf)), gc, g_row)
    kb = _seqs(lambda a, b: a * b, k, beta)
    vb = _seqs(lambda a, b: a * b, v, beta)
    kq = _seqs(lambda a, b, c: _dot_nt(jnp.concatenate([a, b], axis=0), bd(c)), kb, q, k)
    n_mat = _seqs(lambda s, d: jnp.where(strict, s[0:CHUNK] * d, 0.0), kq, decay)
    a_qk = _seqs(lambda s, d: s[CHUNK:2 * CHUNK] * d, kq, decay)
    m = _seqs(lambda a: -a, n_mat)
    x = _seqs(lambda a: jnp.where(diag, 1.0, 0.0) + a, m)
    m = _seqs(lambda a: _dot(a, bd(a)), m)
    p = 2
    while p < CHUNK // 2:
        xm = _seqs(lambda a, b: _dot(jnp.concatenate([a, b], axis=0), bd(b)), x, m)
        x = _seqs(lambda a, r: a + r[0:CHUNK], x, xm)
        m = [r[CHUNK:2 * CHUNK] for r in xm]
        p *= 2
    x = _seqs(lambda a, b: a + _dot(a, bd(b)), x, m)
    uw = _seqs(lambda a, b, c, g: _dot(a, jnp.concatenate([bd(b), bd(c * jnp.exp(g))], axis=1)),
               x, vb, kb, gc)
    st = [ref[...] for ref in st_refs]
    ws = _seqs(lambda a, b, g, s: _dot(jnp.concatenate([a[:, 256:512], b * jnp.exp(g)], axis=0), s),
               uw, q, gc, st)
    v_new = _seqs(lambda a, r: a[:, 0:256] - r[0:CHUNK], uw, ws)
    o = _seqs(lambda r, qk, vn: r[CHUNK:2 * CHUNK] + _dot(qk, bd(vn)), ws, a_qk, v_new)
    upd = _seqs(lambda a, g, l, vn: _dot_tn(a * jnp.exp(l - g), vn), k, gc, gl, v_new)
    for ref, s_old, l, u in zip(st_refs, st, gl, upd):
        ref[...] = s_old * jnp.exp(l) + jnp.where(mask_bd, u, 0.0)
    ms = _seqs(lambda a: _split2_dot(a * a, ones_ref[...]) * (1.0 / GDN_DH), o)
    return _seqs(lambda a, m_, g: a * lax.rsqrt(m_ + EPS) * gn_ref[...] * _silu(g), o, ms, gate)


def _rec_kernel(zg_ref, zd_ref, zm_ref, wlr_ref, blr_ref, gng_ref, cw_ref, eba_ref, alog_ref, dtb_ref,
                gnd_ref, ones_ref, og_ref, od_ref, halo_ref, *st_refs):
    nb = zg_ref.shape[0]

    @pl.when(pl.program_id(1) == 0)
    def _():
        halo_ref[...] = jnp.zeros_like(halo_ref)
        for st_ref in st_refs:
            st_ref[...] = jnp.zeros_like(st_ref)

    zm = [zm_ref[i] for i in range(nb)]
    zg = [zg_ref[i].astype(F32) for i in range(nb)]
    zd = [zd_ref[i].astype(F32) for i in range(nb)]
    halo = [halo_ref[i] for i in range(nb)]
    for i in range(nb):
        halo_ref[i] = zd[i][CHUNK - GDN_HALO:, 0:768]
    og = _gla_chunk(_gla_masks(), zg, zm, wlr_ref, blr_ref, gng_ref, ones_ref, st_refs[:nb])
    od = _gdn_chunk(_gdn_masks(), halo, zd, zm, cw_ref, eba_ref, alog_ref, dtb_ref, gnd_ref, ones_ref,
                    st_refs[nb:])
    for i in range(nb):
        og_ref[i] = og[i].astype(og_ref.dtype)
        od_ref[i] = od[i].astype(od_ref.dtype)


def _recurrent(zg, zd, zm, wlr2, blr, gn_gla, cw, eba, alog, dtb, gn_gdn, ones_bd):
    b, s, _ = zg.shape
    nb = REC_SEQS if b % REC_SEQS == 0 else 1
    const = lambda shape: pl.BlockSpec(shape, lambda i, c: (0,) * len(shape))
    return pl.pallas_call(
        _rec_kernel,
        grid=(b // nb, s // CHUNK),
        in_specs=[pl.BlockSpec((nb, CHUNK, ZG_W), lambda i, c: (i, c, 0)),
                  pl.BlockSpec((nb, CHUNK, ZD_W), lambda i, c: (i, c, 0)),
                  pl.BlockSpec((nb, CHUNK, ZM_W), lambda i, c: (i, c, 0)),
                  const((2, ZM_W, 128)), const((1, 128)), const((1, MIX_W)),
                  const((GDN_CONV, 768)), const((ZM_W, 2 * MIX_W)), const((1, ZM_W)), const((1, ZM_W)),
                  const((1, MIX_W)), const((MIX_W, MIX_W))],
        out_specs=[pl.BlockSpec((nb, CHUNK, MIX_W), lambda i, c: (i, c, 0))] * 2,
        out_shape=[jax.ShapeDtypeStruct((b, s, MIX_W), BF16)] * 2,
        scratch_shapes=([pltpu.VMEM((nb, GDN_HALO, 768), F32)] + [pltpu.VMEM((256, 128), F32)] * nb
                        + [pltpu.VMEM((256, 256), F32)] * nb),
        compiler_params=_cparams(("parallel", "arbitrary")),
        name="gla_gdn",
    )(zg, zd, zm, wlr2, blr, gn_gla, cw, eba, alog, dtb, gn_gdn, ones_bd)


def _cmp_kernel(kc_ref, vc_ref, pe_ref, w1_ref, w2_ref, ck_ref, cv_ref):
    n_sub = NSA_CMP_LEN // NSA_CMP_STRIDE
    n_chunks = kc_ref.shape[1] // NSA_CMP_STRIDE
    for which, (src_ref, out_ref) in enumerate(((kc_ref, ck_ref), (vc_ref, cv_ref))):
        parts = [jnp.zeros((n_chunks, LANE), F32) for _ in range(n_sub)]
        for i in range(NSA_CMP_STRIDE):
            slab = src_ref[0, pl.ds(i, n_chunks, stride=NSA_CMP_STRIDE), :]
            for sub in range(n_sub):
                p = sub * NSA_CMP_STRIDE + i
                parts[sub] = parts[sub] + _dot(slab + pe_ref[which, p:p + 1, :], w1_ref[which, p])
        pre = parts[0] + pltpu.roll(parts[1], n_chunks - 1, 0)
        out_ref[0] = _dot(jax.nn.gelu(pre), w2_ref[which])


def _nsa_compress(zc, pe_x, w1_bd, w2_bd):
    b, s, _ = zc.shape
    n_chunks = s // NSA_CMP_STRIDE
    return pl.pallas_call(
        _cmp_kernel,
        grid=(b,),
        in_specs=[pl.BlockSpec((1, s, LANE), lambda i: (i, 0, 0)),
                  pl.BlockSpec((1, s, LANE), lambda i: (i, 0, 1)),
                  pl.BlockSpec((2, NSA_CMP_LEN, LANE), lambda i: (0, 0, 0)),
                  pl.BlockSpec((2, NSA_CMP_LEN, LANE, LANE), lambda i: (0, 0, 0, 0)),
                  pl.BlockSpec((2, LANE, LANE), lambda i: (0, 0, 0))],
        out_specs=[pl.BlockSpec((1, n_chunks, LANE), lambda i: (i, 0, 0))] * 2,
        out_shape=[jax.ShapeDtypeStruct((b, n_chunks, LANE), F32)] * 2,
        compiler_params=_cparams(("parallel",)),
        name="nsa_compress",
    )(zc, zc, pe_x, w1_bd, w2_bd)


def _exp_weights(s, m):
    return jnp.exp((s - m).astype(BF16))


def _normalise_aug(o_aug, in_grp):
    den = pltpu.roll(o_aug, NSA_DH, 1)
    return jnp.where(in_grp, o_aug / jnp.where(den > 0, den, 1.0), 0.0)


def _nsa_kernel(q_ref, ksv_ref, kwv_ref, ck_ref, cv_ref, zm_ref, covt_ref, gexp_ref, o_ref,
                kaug_ref, vaug_ref, vwaug_ref, s_ref, mx_ref, acc_ref):
    tq = q_ref.shape[1]
    s_len = ksv_ref.shape[1]
    n_slc = s_len // NSA_SEL_LEN
    qi = pl.program_id(1)
    s0 = pl.multiple_of(qi * tq, tq)
    lane_grp = _iota((1, LANE), 1) // NSA_DH

    @pl.when(qi == 0)
    def _():
        lane = _iota((s_len, LANE), 1)
        key_blk = _iota((s_len, LANE), 0) // NSA_SEL_LEN
        k = ksv_ref[0, :, 0:LANE].astype(F32)
        v = ksv_ref[0, :, LANE:2 * LANE].astype(F32)
        vw = kwv_ref[0, :, LANE:2 * LANE].astype(F32)
        for g in range(NSA_GROUPS):
            own = lane // NSA_DH == g
            onehot = jnp.where(lane - (1 - g) * NSA_DH == key_blk, 1.0, 0.0)
            kaug_ref[g] = jnp.where(own, k, onehot).astype(BF16)
            vaug_ref[g] = jnp.where(own, v, 1.0).astype(BF16)
            vwaug_ref[g] = jnp.where(own, vw, 1.0).astype(BF16)

    t_col = s0 + _iota((tq, 1), 0)
    t_col2 = jnp.concatenate([t_col, t_col], axis=0)
    t_row = s0 + _iota((1, tq), 1)
    q = q_ref[0].astype(F32) * NSA_DH ** -0.5
    ck = ck_ref[0]
    ck_hi = ck.astype(BF16)
    ck_lo = (ck - ck_hi.astype(F32)).astype(BF16)
    cv = cv_ref[0]
    cmp_end = _iota((1, N_CMP_PAD), 1) * NSA_CMP_STRIDE + (NSA_CMP_LEN - 1)
    cmp_valid = cmp_end <= t_col2
    blk_t = _iota((n_slc, tq), 0)
    cur_t = t_row // NSA_SEL_LEN
    forced_t = (blk_t == 0) | (blk_t == cur_t) | (blk_t == cur_t - 1)
    future_t = blk_t > cur_t
    row_in_tile = jnp.concatenate([_iota((tq, tq), 0)] * NSA_HPG, axis=0)
    col_in_tile = _iota((NSA_HPG * tq, tq), 1)
    diag_ok = col_in_tile <= row_in_tile
    n_wt = NSA_WINDOW // tq + 1
    never = 2 * tq
    win_off = [pl.multiple_of(jnp.maximum(qi - (n_wt - 1 - w), 0) * tq, tq) for w in range(n_wt)]
    win_ok = [col_in_tile > row_in_tile + jnp.where(qi >= n_wt - 1, 0, never)]
    win_ok += [col_in_tile >= jnp.where(qi >= n_wt - 1 - w, 0, never) for w in range(1, n_wt - 1)]
    win_ok += [diag_ok]

    o_cmp = [jnp.zeros((tq, LANE), F32) for _ in range(NSA_HPG)]
    o_slc = [jnp.zeros((tq, LANE), F32) for _ in range(NSA_HPG)]
    o_win = [jnp.zeros((tq, LANE), F32) for _ in range(NSA_HPG)]
    qa = []
    for g in range(NSA_GROUPS):
        in_grp = lane_grp == g
        q2 = jnp.concatenate([jnp.where(in_grp, q[:, j * LANE:(j + 1) * LANE], 0.0)
                              for j in range(NSA_HPG)], axis=0)
        qb = q2.astype(BF16)
        p_c = _masked_softmax(_dot_nt(qb, ck_hi) + _dot_nt(qb, ck_lo), cmp_valid)
        oc = jnp.where(in_grp, _dot(p_c, cv), 0.0)
        imp_c = p_c[0:tq] + p_c[tq:2 * tq]
        imp_hi = imp_c.astype(BF16)
        imp_lo = (imp_c - imp_hi.astype(F32)).astype(BF16)
        imp_t = (_dot_nt(covt_ref[...], imp_hi) + _dot_nt(covt_ref[...], imp_lo))[0:n_slc]
        imp_t = jnp.where(forced_t, jnp.inf, jnp.where(future_t, -jnp.inf, imp_t))
        rank = jnp.zeros((n_slc, tq), jnp.int32)
        for m in range(n_slc):
            other = imp_t[m:m + 1, :]
            ahead = (other > imp_t) | ((other == imp_t) & (blk_t > m))
            rank = rank + ahead.astype(jnp.int32)
        bias_t = jnp.where(rank < NSA_N_SEL, 0.0, NEG_BIG)
        lo = (1 - g) * NSA_DH
        rows = ([jnp.zeros((lo, tq), F32)] if lo else []) + [bias_t, jnp.zeros((LANE - lo - n_slc, tq), F32)]
        bias = jnp.concatenate(rows, axis=0).T
        qa.append(jnp.where(in_grp, q2, jnp.concatenate([bias] * NSA_HPG, axis=0)).astype(BF16))
        s_w = [jnp.where(ok, _dot_nt(qb, kwv_ref[0, pl.ds(off, tq), 0:LANE]), -jnp.inf)
               for ok, off in zip(win_ok, win_off)]
        m_w = s_w[0]
        for s_t in s_w[1:]:
            m_w = jnp.maximum(m_w, s_t)
        m_w = jnp.max(m_w, axis=-1, keepdims=True)
        m_w = jnp.where(jnp.isfinite(m_w), m_w, 0.0)
        ow = jnp.zeros((NSA_HPG * tq, LANE), F32)
        for s_t, off in zip(s_w, win_off):
            ow = ow + jnp.dot(_exp_weights(s_t, m_w), vwaug_ref[g, pl.ds(off, tq), :],
                              preferred_element_type=F32)
        ow = _normalise_aug(ow, in_grp)
        for j in range(NSA_HPG):
            rows_j = slice(j * tq, (j + 1) * tq)
            o_cmp[j] = o_cmp[j] + oc[rows_j]
            o_win[j] = o_win[j] + ow[rows_j]

    groups = range(NSA_GROUPS)
    half_max = lambda s_t: jnp.maximum(s_t[:, 0:LANE], s_t[:, LANE:2 * LANE])
    mx_ref[...] = jnp.full(mx_ref.shape, NEG_BIG, F32)

    def score_tiles(tiles):
        offs = [pl.multiple_of(kt * tq, tq) for kt in tiles]
        s_new = [[_dot_nt(qa[g], kaug_ref[g, pl.ds(off, tq), :]) for g in groups] for off in offs]
        for kt, s_kt in zip(tiles, s_new):
            for g in groups:
                s_ref[g, kt] = s_kt[g]
        for g in groups:
            m_new = half_max(s_new[0][g])
            for s_kt in s_new[1:]:
                m_new = jnp.maximum(m_new, half_max(s_kt[g]))
            mx_ref[g] = jnp.maximum(mx_ref[g], m_new)

    def score_pair(p, carry):
        score_tiles([2 * p, 2 * p + 1])
        return carry

    lax.fori_loop(0, qi // 2, score_pair, 0)

    @pl.when(qi % 2 == 1)
    def _():
        score_tiles([qi - 1])

    m_s = []
    for g in groups:
        s_d = jnp.where(diag_ok, _dot_nt(qa[g], kaug_ref[g, pl.ds(s0, tq), :]), NEG_BIG)
        s_ref[g, qi] = s_d
        m_g = jnp.max(jnp.maximum(mx_ref[g], half_max(s_d)), axis=-1, keepdims=True)
        m_s.append(jnp.where(jnp.isfinite(m_g), m_g, 0.0))
    acc_ref[...] = jnp.zeros(acc_ref.shape, F32)

    def attend_tiles(tiles):
        offs = [pl.multiple_of(kt * tq, tq) for kt in tiles]
        pv = [[jnp.dot(_exp_weights(s_ref[g, kt], m_s[g]), vaug_ref[g, pl.ds(off, tq), :],
                       preferred_element_type=F32) for g in groups] for kt, off in zip(tiles, offs)]
        for g in groups:
            upd = pv[0][g]
            for pv_kt in pv[1:]:
                upd = upd + pv_kt[g]
            acc_ref[g] += upd

    def attend_pair(p, carry):
        attend_tiles([2 * p, 2 * p + 1])
        return carry

    lax.fori_loop(0, (qi + 1) // 2, attend_pair, 0)

    @pl.when(qi % 2 == 0)
    def _():
        attend_tiles([qi])

    for g in groups:
        os_ = _normalise_aug(acc_ref[g], lane_grp == g)
        for j in range(NSA_HPG):
            o_slc[j] = o_slc[j] + os_[j * tq:(j + 1) * tq]

    gates = _split2_dot(jax.nn.sigmoid(zm_ref[0]), gexp_ref[...])
    o = (gates[:, 0:MIX_W] * jnp.concatenate(o_cmp, axis=1)
         + gates[:, MIX_W:2 * MIX_W] * jnp.concatenate(o_slc, axis=1)
         + gates[:, 2 * MIX_W:3 * MIX_W] * jnp.concatenate(o_win, axis=1))
    o_ref[0] = o.astype(o_ref.dtype)


def _nsa(zn, zm, ck, cv, covt, gexp):
    b, s, _ = zn.shape
    tq = TQ_NSA
    return pl.pallas_call(
        _nsa_kernel,
        grid=(b, s // tq),
        in_specs=[pl.BlockSpec((1, tq, 256), lambda i, j: (i, j, 0)),
                  pl.BlockSpec((1, s, 256), lambda i, j: (i, 0, 1)),
                  pl.BlockSpec((1, s, 256), lambda i, j: (i, 0, 2)),
                  pl.BlockSpec((1, N_CMP_PAD, LANE), lambda i, j: (i, 0, 0)),
                  pl.BlockSpec((1, N_CMP_PAD, LANE), lambda i, j: (i, 0, 0)),
                  pl.BlockSpec((1, tq, ZM_W), lambda i, j: (i, j, 0)),
                  pl.BlockSpec((N_SLC_PAD, N_CMP_PAD), lambda i, j: (0, 0)),
                  pl.BlockSpec((ZM_W, 3 * MIX_W), lambda i, j: (0, 0))],
        out_specs=pl.BlockSpec((1, tq, MIX_W), lambda i, j: (i, j, 0)),
        out_shape=jax.ShapeDtypeStruct((b, s, MIX_W), BF16),
        scratch_shapes=[pltpu.VMEM((NSA_GROUPS, s, LANE), BF16)] * 3
                       + [pltpu.VMEM((NSA_GROUPS, s // tq, NSA_HPG * tq, tq), F32),
                          pltpu.VMEM((NSA_GROUPS, NSA_HPG * tq, LANE), F32),
                          pltpu.VMEM((NSA_GROUPS, NSA_HPG * tq, LANE), F32)],
        compiler_params=_cparams(("parallel", "arbitrary")),
        name="nsa_attn",
    )(zn, zn, zn, ck, cv, zm, covt, gexp)


def _combine_kernel(x_ref, g_ref, op_ref, oa_ref, od_ref, on_ref, wg_ref, bg_ref, wb_ref, wo_ref, o_ref):
    x = x_ref[...]
    h = _rms(x, g_ref[...]).astype(BF16)
    y = jnp.zeros(x.shape, F32)
    for i, br_ref in enumerate((op_ref, oa_ref, od_ref, on_ref)):
        gate = jax.nn.sigmoid(jnp.dot(h, wg_ref[i], preferred_element_type=F32) + bg_ref[i])
        y = y + gate * jnp.dot(br_ref[...], wb_ref[i], preferred_element_type=F32)
    o_ref[...] = x + _dot(y, wo_ref[...])


def _combine(x2, g, branches, wg, bg, wb, wo):
    t = x2.shape[0]
    tm = TM_COMB
    return pl.pallas_call(
        _combine_kernel,
        grid=(t // tm,),
        in_specs=[pl.BlockSpec((tm, D_MODEL), lambda i: (i, 0)),
                  pl.BlockSpec((1, D_MODEL), lambda i: (0, 0))]
                 + [pl.BlockSpec((tm, MIX_W), lambda i: (i, 0))] * N_BRANCH
                 + [pl.BlockSpec((N_BRANCH, D_MODEL, D_MODEL), lambda i: (0, 0, 0)),
                    pl.BlockSpec((N_BRANCH, 1, D_MODEL), lambda i: (0, 0, 0)),
                    pl.BlockSpec((N_BRANCH, MIX_W, D_MODEL), lambda i: (0, 0, 0)),
                    pl.BlockSpec((D_MODEL, D_MODEL), lambda i: (0, 0))],
        out_specs=pl.BlockSpec((tm, D_MODEL), lambda i: (i, 0)),
        out_shape=jax.ShapeDtypeStruct((t, D_MODEL), F32),
        compiler_params=_cparams(("parallel",)),
        name="combine",
    )(x2, g, *branches, wg, bg, wb, wo)


def _memkv_kernel(m_ref, g_ref, w_ref, o_ref):
    o_ref[0] = _dot(_rms(m_ref[0], g_ref[...]), w_ref[...]).astype(o_ref.dtype)


def _memkv(mem, g, w):
    b, m, _ = mem.shape
    n = 2 * X_HEADS * X_DH
    return pl.pallas_call(
        _memkv_kernel,
        grid=(b,),
        in_specs=[pl.BlockSpec((1, m, D_MODEL), lambda i: (i, 0, 0)),
                  pl.BlockSpec((1, D_MODEL), lambda i: (0, 0)),
                  pl.BlockSpec((D_MODEL, n), lambda i: (0, 0))],
        out_specs=pl.BlockSpec((1, m, n), lambda i: (i, 0, 0)),
        out_shape=jax.ShapeDtypeStruct((b, m, n), BF16),
        compiler_params=_cparams(("parallel",)),
        name="mem_kv",
    )(mem, g, w)


def _cross_kernel(x_ref, g_ref, wq_ref, kv_ref, wo_ref, o_ref):
    x = x_ref[0]
    hn = _rms(x, g_ref[...])
    q = _dot(hn, wq_ref[...])
    n_k = X_HEADS * X_DH
    outs = []
    for h in range(X_HEADS):
        k_h = kv_ref[0, :, h * X_DH:(h + 1) * X_DH]
        v_h = kv_ref[0, :, n_k + h * X_DH:n_k + (h + 1) * X_DH]
        sc = _dot_nt(q[:, h * X_DH:(h + 1) * X_DH], k_h) * X_DH ** -0.5
        e = jnp.exp(sc - jnp.max(sc, axis=-1, keepdims=True))
        p = e / jnp.sum(e, axis=-1, keepdims=True)
        outs.append(_dot(p, v_h))
    o_ref[0] = x + _dot(jnp.concatenate(outs, axis=1), wo_ref[...])


def _cross(x3, g, wq, kv, wo):
    b, s, _ = x3.shape
    tm = TM_CROSS
    m = kv.shape[1]
    n_k = X_HEADS * X_DH
    return pl.pallas_call(
        _cross_kernel,
        grid=(b, s // tm),
        in_specs=[pl.BlockSpec((1, tm, D_MODEL), lambda i, j: (i, j, 0)),
                  pl.BlockSpec((1, D_MODEL), lambda i, j: (0, 0)),
                  pl.BlockSpec((D_MODEL, n_k), lambda i, j: (0, 0)),
                  pl.BlockSpec((1, m, 2 * n_k), lambda i, j: (i, 0, 0)),
                  pl.BlockSpec((n_k, D_MODEL), lambda i, j: (0, 0))],
        out_specs=pl.BlockSpec((1, tm, D_MODEL), lambda i, j: (i, j, 0)),
        out_shape=jax.ShapeDtypeStruct((b, s, D_MODEL), F32),
        compiler_params=_cparams(("parallel", "parallel")),
        name="cross_attn",
    )(x3, g, wq, kv, wo)


def _ffn_kernel(x_ref, g_ref, wup_ref, cw_ref, cb_ref, wd_ref, gf_ref, o_ref, tail_ref, act_ref, *, final):
    @pl.when(pl.program_id(1) == 0)
    def _():
        tail_ref[...] = jnp.zeros_like(tail_ref)

    ts = x_ref.shape[1]
    x = x_ref[0]
    hn = _rms(x, g_ref[...]).astype(BF16)
    for c in range(D_FF // FF_CHUNK):
        cols = slice(c * FF_CHUNK, (c + 1) * FF_CHUNK)
        gcols = slice(D_FF + c * FF_CHUNK, D_FF + (c + 1) * FF_CHUNK)
        u = jnp.dot(hn, wup_ref[:, cols], preferred_element_type=F32)
        v = jnp.dot(hn, wup_ref[:, gcols], preferred_element_type=F32)
        ext = jnp.concatenate([tail_ref[:, cols], u], axis=0)
        tail_ref[:, cols] = u[ts - SUBLANE:, :]
        cw = cw_ref[:, cols]
        y = (cw[2:3] * u + cw[1:2] * pltpu.roll(ext, 1, 0)[SUBLANE:]
             + cw[0:1] * pltpu.roll(ext, 2, 0)[SUBLANE:] + cb_ref[:, cols])
        act_ref[:, cols] = (jax.nn.gelu(y) * v).astype(BF16)
    out = x + jnp.dot(act_ref[...], wd_ref[...], preferred_element_type=F32)
    if final:
        out = _rms(out, gf_ref[...])
    o_ref[0] = out


def _ffn(x3, g, wup, cw, cb, wd, gf, final):
    b, s, _ = x3.shape
    ts = TS_FFN
    return pl.pallas_call(
        functools.partial(_ffn_kernel, final=final),
        grid=(b, s // ts),
        in_specs=[pl.BlockSpec((1, ts, D_MODEL), lambda i, j: (i, j, 0)),
                  pl.BlockSpec((1, D_MODEL), lambda i, j: (0, 0)),
                  pl.BlockSpec((D_MODEL, 2 * D_FF), lambda i, j: (0, 0), pipeline_mode=pl.Buffered(1)),
                  pl.BlockSpec((FFN_CONV, D_FF), lambda i, j: (0, 0)),
                  pl.BlockSpec((1, D_FF), lambda i, j: (0, 0)),
                  pl.BlockSpec((D_FF, D_MODEL), lambda i, j: (0, 0), pipeline_mode=pl.Buffered(1)),
                  pl.BlockSpec((1, D_MODEL), lambda i, j: (0, 0))],
        out_specs=pl.BlockSpec((1, ts, D_MODEL), lambda i, j: (i, j, 0)),
        out_shape=jax.ShapeDtypeStruct((b, s, D_MODEL), F32),
        scratch_shapes=[pltpu.VMEM((SUBLANE, D_FF), F32), pltpu.VMEM((ts, D_FF), BF16)],
        compiler_params=_cparams(("parallel", "arbitrary")),
        name="conv_ffn",
    )(x3, g, wup, cw, cb, wd, gf)


def _inproj_columns():
    starts = np.concatenate([[0], np.cumsum(IN_SPLITS)])
    (p_in, a_q, a_k, a_v, a_r, a_lr, d_q, d_k, d_v, d_b, d_a, d_g,
     n_q, n_kc, n_vc, n_ks, n_vs, n_kw, n_vw, n_g) = [np.arange(starts[i], starts[i + 1])
                                                      for i in range(len(IN_SPLITS))]
    n_q = n_q.reshape(NSA_GROUPS, NSA_HPG, NSA_DH).transpose(1, 0, 2).reshape(-1)
    misc = np.full((ZM_W,), N_IN)
    misc[MISC_LR:MISC_LR + GLA_LOWRANK] = a_lr
    misc[MISC_B:MISC_B + GDN_HEADS] = d_b
    misc[MISC_A:MISC_A + GDN_HEADS] = d_a
    misc[MISC_G:MISC_G + 3 * NSA_HEADS] = n_g
    cols = np.concatenate([p_in, a_q, a_k, a_v, a_r, d_q, d_k, d_v, d_g,
                           n_q, n_ks, n_vs, n_kw, n_vw, n_kc, n_vc, misc])
    assert cols.shape[0] == Z_W
    return cols


def _head_expand(offset, n_heads, width):
    e = np.zeros((ZM_W, n_heads * width), np.float32)
    for h in range(n_heads):
        e[offset + h, h * width:(h + 1) * width] = 1.0
    return e


def _nsa_constants(s):
    n_cmp = s // NSA_CMP_STRIDE - NSA_CMP_LEN // NSA_CMP_STRIDE + 1
    n_slc = s // NSA_SEL_LEN
    c_start = np.arange(n_cmp) * NSA_CMP_STRIDE
    s_start = np.arange(n_slc) * NSA_SEL_LEN
    cover = np.zeros((N_CMP_PAD, N_SLC_PAD), np.float32)
    cover[:n_cmp, :n_slc] = ((c_start[:, None] <= s_start[None, :] + NSA_SEL_LEN - 1)
                             & (c_start[:, None] + NSA_CMP_LEN - 1 >= s_start[None, :]))
    gexp = np.zeros((ZM_W, 3, MIX_W), np.float32)
    for g in range(NSA_GROUPS):
        for j in range(NSA_HPG):
            slot = j * NSA_GROUPS + g
            for c in range(3):
                gexp[MISC_G + (g * NSA_HPG + j) * 3 + c, c, slot * NSA_DH:(slot + 1) * NSA_DH] = 1.0
    return jnp.asarray(cover.T, dtype=BF16), jnp.asarray(gexp.reshape(ZM_W, 3 * MIX_W), dtype=BF16)


def _block_diag(blocks):
    n, a, b = blocks.shape
    return jnp.einsum('gh,gab->gahb', jnp.eye(n, dtype=blocks.dtype), blocks).reshape(n * a, n * b)


def kernel(x, mem, g_mix, w_in, pool_w, pool_scale, gla_w_lr, gla_b_lr, gla_g_norm, gdn_conv, gdn_a_log,
           gdn_dt_bias, gdn_g_norm, nsa_pe, nsa_cmp_w1, nsa_cmp_w2, w_branch, w_gate, b_gate, w_out, g_cross,
           g_mem, w_xq, w_mem_kv, w_xo, g_ffn, w_up, ffn_conv, ffn_conv_b, w_down, g_final):
    b, s, d = x.shape
    depth = w_in.shape[0]
    t = b * s
    cols = _inproj_columns()
    covt, gexp = _nsa_constants(s)
    eba = jnp.asarray(np.concatenate([_head_expand(MISC_B, GDN_HEADS, GDN_DH),
                                      _head_expand(MISC_A, GDN_HEADS, GDN_DH)], axis=1), dtype=BF16)
    ones_bd = _block_diag(jnp.ones((GDN_HEADS, GDN_DH, GDN_DH), BF16))
    nsa_rows = np.arange(MIX_W).reshape(NSA_GROUPS, NSA_HPG, NSA_DH).transpose(1, 0, 2).reshape(-1)
    row = lambda v: v.reshape(1, -1).astype(F32)
    misc_a = lambda v: jnp.zeros((1, ZM_W), F32).at[0, MISC_A:MISC_A + GDN_HEADS].set(v)

    x2 = x.reshape(t, d)
    for l in range(depth):
        w_in_r = jnp.concatenate([w_in[l], jnp.zeros((d, 1), F32)], axis=1)[:, cols].astype(BF16)
        zp, zg, zd, zn, zc, zm = (z.reshape(b, s, -1) for z in _inproj(x2, row(g_mix[l]), w_in_r))

        o_pool = _pool(zp, _block_diag(pool_w[l]).astype(BF16), row(pool_scale[l]))

        wlr = jnp.zeros((ZM_W, GLA_HEADS * GLA_DK), F32).at[MISC_LR:MISC_LR + GLA_LOWRANK].set(gla_w_lr[l])
        wlr_hi = wlr.astype(BF16)
        wlr2 = jnp.stack([wlr_hi, (wlr - wlr_hi.astype(F32)).astype(BF16)])
        o_gla, o_gdn = _recurrent(zg, zd, zm, wlr2, row(gla_b_lr[l]), row(jnp.tile(gla_g_norm[l], GLA_HEADS)),
                                  gdn_conv[l], eba, misc_a(gdn_a_log[l]), misc_a(gdn_dt_bias[l]),
                                  row(jnp.tile(gdn_g_norm[l], GDN_HEADS)), ones_bd)

        pe_x = jnp.tile(nsa_pe[l], (1, 1, NSA_GROUPS))
        w1 = nsa_cmp_w1[l].reshape(2, NSA_CMP_LEN, NSA_DH, NSA_DH)
        eye_g = jnp.eye(NSA_GROUPS, dtype=F32)
        w1_bd = jnp.einsum('gh,kpde->kpgdhe', eye_g, w1).reshape(2, NSA_CMP_LEN, LANE, LANE).astype(BF16)
        w2_bd = jnp.einsum('gh,kde->kgdhe', eye_g, nsa_cmp_w2[l]).reshape(2, LANE, LANE).astype(BF16)
        ck, cv = _nsa_compress(zc, pe_x, w1_bd, w2_bd)
        o_nsa = _nsa(zn, zm, ck, cv, covt, gexp)

        wb = jnp.concatenate([w_branch[l, :3], w_branch[l, 3][nsa_rows][None]], axis=0).astype(BF16)
        branches = [o.reshape(t, MIX_W) for o in (o_pool, o_gla, o_gdn, o_nsa)]
        x2 = _combine(x2, row(g_mix[l]), branches, w_gate[l].astype(BF16),
                      b_gate[l].reshape(N_BRANCH, 1, d), wb, w_out[l].astype(BF16))

        kv = _memkv(mem, row(g_mem[l]), w_mem_kv[l].astype(BF16))
        x3 = _cross(x2.reshape(b, s, d), row(g_cross[l]), w_xq[l].astype(BF16), kv, w_xo[l].astype(BF16))

        x3 = _ffn(x3, row(g_ffn[l]), w_up[l].astype(BF16), ffn_conv[l], row(ffn_conv_b[l]),
                  w_down[l].astype(BF16), row(g_final), final=(l == depth - 1))
        x2 = x3.reshape(t, d)
    return x2.reshape(b, s, d)
```

```python
import functools

import numpy as np
import jax
import jax.numpy as jnp
from jax import lax
from jax.experimental import pallas as pl
from jax.experimental.pallas import tpu as pltpu

F32 = jnp.float32
BF16 = jnp.bfloat16
HIGHEST = lax.Precision.HIGHEST

D_MODEL = 1024
MIX_W = 256
POOL_WINDOWS = (2, 4, 8, 16)
POOL_GW = 64
GLA_HEADS = 4
GLA_DK = 32
GLA_DV = 64
GLA_LOWRANK = 16
GLA_GATE_NORM = 16.0
CHUNK = 64
GDN_HEADS = 4
GDN_DH = 64
GDN_CONV = 4
NSA_HEADS = 4
NSA_GROUPS = 2
NSA_HPG = 2
NSA_DH = 64
NSA_KV = 128
NSA_CMP_LEN = 32
NSA_CMP_STRIDE = 16
NSA_SEL_LEN = 64
NSA_N_SEL = 16
NSA_WINDOW = 512
X_HEADS = 4
X_DH = 128
D_FF = 2816
FFN_CONV = 3
EPS = 1e-6
N_BRANCH = 4

IN_SPLITS = (MIX_W,
             128, 128, 256, 256, GLA_LOWRANK,
             MIX_W, MIX_W, MIX_W, GDN_HEADS, GDN_HEADS, MIX_W,
             256, NSA_KV, NSA_KV, NSA_KV, NSA_KV, NSA_KV, NSA_KV, 3 * NSA_HEADS)
N_IN = sum(IN_SPLITS)

MISC_LR = 0
MISC_B = 16
MISC_A = 20
MISC_G = 24
LANE = 128
SUBLANE = 8

Z_WIDTHS = (256, 768, 1024, 768, 256, 128)
Z_DTYPES = (F32, BF16, BF16, BF16, F32, F32)
ZP_W, ZG_W, ZD_W, ZN_W, ZC_W, ZM_W = Z_WIDTHS
Z_W = sum(Z_WIDTHS)
NEG_BIG = -1e30
F32_LOWEST = float(np.finfo(np.float32).min)

TM_PROJ = 256
TM_COMB = 256
TM_CROSS = 512
TS_FFN = 512
FF_CHUNK = 256
TQ_NSA = 256
REC_SEQS = 8
REC_LAG = 3
N_SLC_PAD = 128
N_CMP_PAD = 128

VMEM_LIMIT = 56 * 1024 * 1024


def _cparams(sem):
    return pltpu.CompilerParams(dimension_semantics=sem, vmem_limit_bytes=VMEM_LIMIT)


def _rms(x, g):
    return x * lax.rsqrt(jnp.mean(x * x, axis=-1, keepdims=True) + EPS) * g


def _dot(a, b):
    return jnp.dot(a.astype(BF16), b.astype(BF16), preferred_element_type=F32)


def _dot_nt(a, b):
    return lax.dot_general(a.astype(BF16), b.astype(BF16), (((1,), (1,)), ((), ())),
                           preferred_element_type=F32)


def _dot_tn(a, b):
    return lax.dot_general(a.astype(BF16), b.astype(BF16), (((0,), (0,)), ((), ())),
                           preferred_element_type=F32)


def _dot_hi(a, b):
    return jnp.dot(a, b, precision=HIGHEST, preferred_element_type=F32)


def _dot_nt_hi(a, b):
    return lax.dot_general(a, b, (((1,), (1,)), ((), ())), precision=HIGHEST,
                           preferred_element_type=F32)


def _split2_dot(a, b):
    hi = a.astype(BF16)
    lo = (a - hi.astype(F32)).astype(BF16)
    return jnp.dot(hi, b, preferred_element_type=F32) + jnp.dot(lo, b, preferred_element_type=F32)


def _split3_rhs_dot(a, b):
    hi = b.astype(BF16)
    r1 = b - hi.astype(F32)
    mid = r1.astype(BF16)
    lo = (r1 - mid.astype(F32)).astype(BF16)
    return ((jnp.dot(a, hi, preferred_element_type=F32) + jnp.dot(a, mid, preferred_element_type=F32))
            + jnp.dot(a, lo, preferred_element_type=F32))


def _split3_lhs_dot(a, b):
    hi = a.astype(BF16)
    r1 = a - hi.astype(F32)
    mid = r1.astype(BF16)
    lo = (r1 - mid.astype(F32)).astype(BF16)
    return ((jnp.dot(hi, b, preferred_element_type=F32) + jnp.dot(mid, b, preferred_element_type=F32))
            + jnp.dot(lo, b, preferred_element_type=F32))


def _dot3(a, w_hi, w_lo):
    a_hi = a.astype(BF16)
    a_lo = (a - a_hi.astype(F32)).astype(BF16)
    return (jnp.dot(a_hi, w_hi, preferred_element_type=F32)
            + (jnp.dot(a_lo, w_hi, preferred_element_type=F32) + jnp.dot(a_hi, w_lo, preferred_element_type=F32)))


def _iota(shape, axis):
    return lax.broadcasted_iota(jnp.int32, shape, axis)


def _block_mask(rows, cols, rb, cb):
    return (_iota((rows, cols), 0) // rb) == (_iota((rows, cols), 1) // cb)


def _shift_rows(x, k):
    t = _iota(x.shape, 0)
    return jnp.where(t >= k, pltpu.roll(x, k, 0), 0.0)


def _cumsum_rows(x):
    k = 1
    while k < x.shape[0]:
        x = x + _shift_rows(x, k)
        k *= 2
    return x


def _softplus(x):
    return jnp.maximum(x, 0.0) + jnp.log1p(jnp.exp(-jnp.abs(x)))


def _log_sigmoid(x):
    return -_softplus(-x)


def _silu(x):
    return x * jax.nn.sigmoid(x)


def _masked_softmax(s, mask):
    s = jnp.where(mask, s, -jnp.inf)
    m = jnp.maximum(jnp.max(s, axis=-1, keepdims=True), F32_LOWEST)
    e = jnp.exp(s - m)
    den = jnp.sum(e, axis=-1, keepdims=True)
    return e / jnp.where(den > 0, den, 1.0)


def _tile4(x):
    return jnp.concatenate([x, x, x, x], axis=0)


def _inproj_kernel(x_ref, g_ref, w_ref, *z_refs):
    h = _rms(x_ref[...], g_ref[...]).astype(BF16)
    off = 0
    for ref in z_refs:
        n = ref.shape[-1]
        ref[...] = jnp.dot(h, w_ref[:, off:off + n], preferred_element_type=F32).astype(ref.dtype)
        off += n


def _inproj(x2, g, w):
    t = x2.shape[0]
    return pl.pallas_call(
        _inproj_kernel,
        grid=(t // TM_PROJ,),
        in_specs=[pl.BlockSpec((TM_PROJ, D_MODEL), lambda i: (i, 0)),
                  pl.BlockSpec((1, D_MODEL), lambda i: (0, 0)),
                  pl.BlockSpec((D_MODEL, Z_W), lambda i: (0, 0))],
        out_specs=[pl.BlockSpec((TM_PROJ, n), lambda i: (i, 0)) for n in Z_WIDTHS],
        out_shape=[jax.ShapeDtypeStruct((t, n), dt) for n, dt in zip(Z_WIDTHS, Z_DTYPES)],
        compiler_params=_cparams(("parallel",)),
        name="inproj",
    )(x2, g, w)


def _pool_kernel(u_ref, w_ref, sc_ref, o_ref):
    u = u_ref[0]
    s2 = u + _shift_rows(u, 1)
    s4 = s2 + _shift_rows(s2, 2)
    s8 = s4 + _shift_rows(s4, 4)
    s16 = s8 + _shift_rows(s8, 8)
    grp = _iota(u.shape, 1) // POOL_GW
    win = jnp.where(grp == 0, s2, jnp.where(grp == 1, s4, jnp.where(grp == 2, s8, s16)))
    width = jnp.where(grp == 0, POOL_WINDOWS[0],
                      jnp.where(grp == 1, POOL_WINDOWS[1],
                                jnp.where(grp == 2, POOL_WINDOWS[2], POOL_WINDOWS[3])))
    cnt = jnp.minimum(_iota(u.shape, 0) + 1, width).astype(F32)
    diff = win / cnt - u
    o_ref[0] = (_dot(diff, w_ref[...]) * sc_ref[...]).astype(o_ref.dtype)


def _pool(zp, w_bd, scale):
    b, s, _ = zp.shape
    return pl.pallas_call(
        _pool_kernel,
        grid=(b,),
        in_specs=[pl.BlockSpec((1, s, MIX_W), lambda i: (i, 0, 0)),
                  pl.BlockSpec((MIX_W, MIX_W), lambda i: (0, 0)),
                  pl.BlockSpec((1, MIX_W), lambda i: (0, 0))],
        out_specs=pl.BlockSpec((1, s, MIX_W), lambda i: (i, 0, 0)),
        out_shape=jax.ShapeDtypeStruct((b, s, MIX_W), BF16),
        compiler_params=_cparams(("parallel",)),
        name="pool",
    )(zp, w_bd, scale)


def _gla_masks():
    return (_block_mask(4 * CHUNK, 128, CHUNK, GLA_DK),
            _block_mask(4 * CHUNK, 256, CHUNK, GLA_DV),
            _block_mask(256, 128, GLA_DV, GLA_DK),
            (_iota((CHUNK, 256), 1) % CHUNK) <= _iota((CHUNK, 256), 0))


def _seqs(f, *lists):
    return [f(*args) for args in zip(*lists)]


def _run_staggered(stage_gens, starts):
    live = list(range(len(stage_gens)))
    tick = 0
    while live:
        for i in list(live):
            if tick >= starts[i]:
                try:
                    next(stage_gens[i])
                except StopIteration:
                    live.remove(i)
        tick += 1


def _gla_chunk(masks, zg, zm, wlr_ref, blr_ref, gn_ref, ones_ref, st_refs, out):
    mask_k, mask_v, mask_st, causal = masks
    q = [z[:, 0:128] * GLA_DK ** -0.5 for z in zg]
    k = [z[:, 128:256] for z in zg]
    v = [z[:, 256:512] for z in zg]
    r = [z[:, 512:768] for z in zg]
    pre = _seqs(lambda a: _dot3(a, wlr_ref[0], wlr_ref[1]), zm)
    yield
    bc = _seqs(lambda a: _cumsum_rows(_log_sigmoid(a + blr_ref[...]) / GLA_GATE_NORM), pre)
    bl = [a[CHUNK - 1:CHUNK, :] for a in bc]
    q_e = _seqs(lambda a, c: a * jnp.exp(c), q, bc)
    k_e = _seqs(lambda a, c: a * jnp.exp(-c), k, bc)
    k_u = _seqs(lambda a, c, l: a * jnp.exp(l - c), k, bc, bl)
    st = [ref[...] for ref in st_refs]
    yield
    att = _seqs(lambda a, b: jnp.where(causal, _dot_nt(a, jnp.where(mask_k, _tile4(b), 0.0)), 0.0), q_e, k_e)
    inter = _seqs(_dot_nt, q_e, st)
    kv = _seqs(_dot_tn, v, k_u)
    yield
    o = _seqs(lambda a, b, c: _dot(a, _tile4(b.astype(BF16)) * ones_ref[...]) + c, att, v, inter)
    for ref, s_old, l, upd in zip(st_refs, st, bl, kv):
        ref[...] = s_old * jnp.exp(l) + jnp.where(mask_st, upd, 0.0)
    yield
    ms = _seqs(lambda a: _dot(a * a, ones_ref[...]) * (1.0 / GLA_DV), o)
    yield
    out[:] = _seqs(lambda a, m, g: a * lax.rsqrt(m + EPS) * gn_ref[...] * _silu(g), o, ms, r)


GDN_HALO = 16


def _gdn_masks():
    c4 = 4 * CHUNK
    col = _iota((CHUNK, c4), 1) % CHUNK
    row = _iota((CHUNK, c4), 0)
    return (_block_mask(c4, c4, CHUNK, CHUNK), col <= row, col < row, col == row)


def _gdn_chunk(masks, halo, zd, zm, cw_ref, eba_ref, alog_ref, dtb_ref, gn_ref, ones_ref, st_refs, out):
    mask_bd, incl, strict, diag = masks
    cw = cw_ref[...]
    bd = lambda a: _tile4(a.astype(BF16)) * ones_ref[...]

    def conv_silu(h, z):
        ext = jnp.concatenate([h, z[:, 0:768]], axis=0)
        conv = (cw[3:4] * ext + cw[2:3] * pltpu.roll(ext, 1, 0)
                + cw[1:2] * pltpu.roll(ext, 2, 0) + cw[0:1] * pltpu.roll(ext, 3, 0))
        return _silu(conv[GDN_HALO:, :])

    qkv = _seqs(conv_silu, halo, zd)
    yield
    v = [a[:, 512:768] for a in qkv]
    gate = [z[:, 768:1024] for z in zd]
    ssq = _seqs(lambda a: _dot(jnp.concatenate([a[:, 0:256] * a[:, 0:256], a[:, 256:512] * a[:, 256:512]],
                                               axis=0), ones_ref[...]), qkv)
    yield
    q = _seqs(lambda a, s: a[:, 0:256] * lax.rsqrt(s[0:CHUNK] + EPS) * GDN_DH ** -0.5, qkv, ssq)
    k = _seqs(lambda a, s: a[:, 256:512] * lax.rsqrt(s[CHUNK:2 * CHUNK] + EPS), qkv, ssq)
    beta = _seqs(lambda a: _dot(jax.nn.sigmoid(a), eba_ref[:, 0:MIX_W]), zm)
    yield
    gc = _seqs(lambda a: _split3_lhs_dot(_cumsum_rows(-jnp.exp(alog_ref[...]) * _softplus(a + dtb_ref[...])),
                                         eba_ref[:, MIX_W:2 * MIX_W]), zm)
    yield
    gl = [a[CHUNK - 1:CHUNK, :] for a in gc]
    ones_cc = jnp.ones((CHUNK, CHUNK), BF16)
    g_row = _seqs(lambda a: _split3_rhs_dot(ones_cc, jnp.where(diag, a, 0.0)), gc)
    yield
    decay =_seqs(lambda a, b: jnp.exp(jnp.where(incl, a - b, -jnp.inf)), gc, g_row)
    kb = _seqs(lambda a, b: a * b, k, beta)
    vb = _seqs(lambda a, b: a * b, v, beta)
    kq = _seqs(lambda a, b, c: _dot_nt(jnp.concatenate([a, b], axis=0), bd(c)), kb, q, k)
    yield
    n_mat =_seqs(lambda s, d: jnp.where(strict, s[0:CHUNK] * d, 0.0), kq, decay)
    a_qk = _seqs(lambda s, d: s[CHUNK:2 * CHUNK] * d, kq, decay)
    m = _seqs(lambda a: -a, n_mat)
    x = _seqs(lambda a: jnp.where(diag, 1.0, 0.0) + a, m)
    m = _seqs(lambda a: _dot(a, bd(a)), m)
    yield
    p = 2
    while p < CHUNK // 2:
        xm = _seqs(lambda a, b: _dot(jnp.concatenate([a, b], axis=0), bd(b)), x, m)
        x = _seqs(lambda a, r: a + r[0:CHUNK], x, xm)
        m = [r[CHUNK:2 * CHUNK] for r in xm]
        p *= 2
        yield
    x = _seqs(lambda a, b: a + _dot(a, bd(b)), x, m)
    yield
    uw = _seqs(lambda a, b, c, g: _dot(a, jnp.concatenate([bd(b), bd(c * jnp.exp(g))], axis=1)),
               x, vb, kb, gc)
    yield
    st = [ref[...] for ref in st_refs]
    ws = _seqs(lambda a, b, g, s: _dot(jnp.concatenate([a[:, 256:512], b * jnp.exp(g)], axis=0), s),
               uw, q, gc, st)
    yield
    v_new = _seqs(lambda a, r: a[:, 0:256] - r[0:CHUNK], uw, ws)
    o = _seqs(lambda r, qk, vn: r[CHUNK:2 * CHUNK] + _dot(qk, bd(vn)), ws, a_qk, v_new)
    upd = _seqs(lambda a, g, l, vn: _dot_tn(a * jnp.exp(l - g), vn), k, gc, gl, v_new)
    yield
    for ref, s_old, l, u in zip(st_refs, st, gl, upd):
        ref[...] = s_old * jnp.exp(l) + jnp.where(mask_bd, u, 0.0)
    ms = _seqs(lambda a: _dot(a * a, ones_ref[...]) * (1.0 / GDN_DH), o)
    yield
    out[:] = _seqs(lambda a, m_, g: a * lax.rsqrt(m_ + EPS) * gn_ref[...] * _silu(g), o, ms, gate)


def _rec_kernel(zg_ref, zd_ref, zm_ref, wlr_ref, blr_ref, gng_ref, cw_ref, eba_ref, alog_ref, dtb_ref,
                gnd_ref, ones_ref, og_ref, od_ref, halo_ref, *st_refs):
    nb = zg_ref.shape[0]

    @pl.when(pl.program_id(1) == 0)
    def _():
        halo_ref[...] = jnp.zeros_like(halo_ref)
        for st_ref in st_refs:
            st_ref[...] = jnp.zeros_like(st_ref)

    zm = [zm_ref[i] for i in range(nb)]
    zg = [zg_ref[i].astype(F32) for i in range(nb)]
    zd = [zd_ref[i].astype(F32) for i in range(nb)]
    halo = [halo_ref[i] for i in range(nb)]
    for i in range(nb):
        halo_ref[i] = zd[i][CHUNK - GDN_HALO:, 0:768]
    halves = [slice(0, nb // 2), slice(nb // 2, nb)] if nb > 1 else [slice(0, nb)]
    og = [None] * nb
    od_parts = [[None] * (h.stop - h.start) for h in halves]
    gdn_masks = _gdn_masks()
    gens = [_gdn_chunk(gdn_masks, halo[h], zd[h], zm[h], cw_ref, eba_ref, alog_ref, dtb_ref, gnd_ref, ones_ref,
                       st_refs[nb:][h], part) for h, part in zip(halves, od_parts)]
    gens.append(_gla_chunk(_gla_masks(), zg, zm, wlr_ref, blr_ref, gng_ref, ones_ref, st_refs[:nb], og))
    _run_staggered(gens, [REC_LAG * i for i in range(len(gens))])
    od = [o for part in od_parts for o in part]
    for i in range(nb):
        og_ref[i] = og[i].astype(og_ref.dtype)
        od_ref[i] = od[i].astype(od_ref.dtype)


def _recurrent(zg, zd, zm, wlr2, blr, gn_gla, cw, eba, alog, dtb, gn_gdn, ones_bd):
    b, s, _ = zg.shape
    nb = REC_SEQS if b % REC_SEQS == 0 else 1
    const = lambda shape: pl.BlockSpec(shape, lambda i, c: (0,) * len(shape))
    return pl.pallas_call(
        _rec_kernel,
        grid=(b // nb, s // CHUNK),
        in_specs=[pl.BlockSpec((nb, CHUNK, ZG_W), lambda i, c: (i, c, 0)),
                  pl.BlockSpec((nb, CHUNK, ZD_W), lambda i, c: (i, c, 0)),
                  pl.BlockSpec((nb, CHUNK, ZM_W), lambda i, c: (i, c, 0)),
                  const((2, ZM_W, 128)), const((1, 128)), const((1, MIX_W)),
                  const((GDN_CONV, 768)), const((ZM_W, 2 * MIX_W)), const((1, ZM_W)), const((1, ZM_W)),
                  const((1, MIX_W)), const((MIX_W, MIX_W))],
        out_specs=[pl.BlockSpec((nb, CHUNK, MIX_W), lambda i, c: (i, c, 0))] * 2,
        out_shape=[jax.ShapeDtypeStruct((b, s, MIX_W), BF16)] * 2,
        scratch_shapes=([pltpu.VMEM((nb, GDN_HALO, 768), F32)] + [pltpu.VMEM((256, 128), F32)] * nb
                        + [pltpu.VMEM((256, 256), F32)] * nb),
        compiler_params=_cparams(("parallel", "arbitrary")),
        name="gla_gdn",
    )(zg, zd, zm, wlr2, blr, gn_gla, cw, eba, alog, dtb, gn_gdn, ones_bd)


def _cmp_kernel(kc_ref, vc_ref, pe_ref, w1_ref, w2_ref, ck_ref, cv_ref):
    n_sub = NSA_CMP_LEN // NSA_CMP_STRIDE
    n_chunks = kc_ref.shape[1] // NSA_CMP_STRIDE
    for which, (src_ref, out_ref) in enumerate(((kc_ref, ck_ref), (vc_ref, cv_ref))):
        parts = [jnp.zeros((n_chunks, LANE), F32) for _ in range(n_sub)]
        for i in range(NSA_CMP_STRIDE):
            slab = src_ref[0, pl.ds(i, n_chunks, stride=NSA_CMP_STRIDE), :]
            for sub in range(n_sub):
                p = sub * NSA_CMP_STRIDE + i
                parts[sub] = parts[sub] + _dot(slab + pe_ref[which, p:p + 1, :], w1_ref[which, p])
        pre = parts[0] + pltpu.roll(parts[1], n_chunks - 1, 0)
        out_ref[0] = _dot(jax.nn.gelu(pre), w2_ref[which])


def _nsa_compress(zc, pe_x, w1_bd, w2_bd):
    b, s, _ = zc.shape
    n_chunks = s // NSA_CMP_STRIDE
    return pl.pallas_call(
        _cmp_kernel,
        grid=(b,),
        in_specs=[pl.BlockSpec((1, s, LANE), lambda i: (i, 0, 0)),
                  pl.BlockSpec((1, s, LANE), lambda i: (i, 0, 1)),
                  pl.BlockSpec((2, NSA_CMP_LEN, LANE), lambda i: (0, 0, 0)),
                  pl.BlockSpec((2, NSA_CMP_LEN, LANE, LANE), lambda i: (0, 0, 0, 0)),
                  pl.BlockSpec((2, LANE, LANE), lambda i: (0, 0, 0))],
        out_specs=[pl.BlockSpec((1, n_chunks, LANE), lambda i: (i, 0, 0))] * 2,
        out_shape=[jax.ShapeDtypeStruct((b, n_chunks, LANE), F32)] * 2,
        compiler_params=_cparams(("parallel",)),
        name="nsa_compress",
    )(zc, zc, pe_x, w1_bd, w2_bd)


def _topk_rank(v):
    n, t = v.shape
    blocks = [v[r:r + SUBLANE] for r in range(0, n, SUBLANE)]
    sub = _iota((SUBLANE, t), 0)
    rank = [jnp.zeros((SUBLANE, t), jnp.int32) for _ in blocks]
    for m in range(n):
        other = v[m:m + 1, :]
        mb, ms = divmod(m, SUBLANE)
        for r, blk in enumerate(blocks):
            if r < mb:
                ahead = (other > blk).astype(jnp.int32)
            elif r > mb:
                ahead = (other >= blk).astype(jnp.int32)
            else:
                ahead = jnp.where(sub > ms, (other >= blk).astype(jnp.int32), (other > blk).astype(jnp.int32))
            rank[r] = rank[r] + ahead
    return jnp.concatenate(rank, axis=0)


def _exp_weights(s, m):
    return jnp.exp((s - m).astype(BF16))


def _normalise_aug(o_aug, in_grp):
    den = pltpu.roll(o_aug, NSA_DH, 1)
    return jnp.where(in_grp, o_aug / jnp.where(den > 0, den, 1.0), 0.0)


def _nsa_kernel(q_ref, ksv_ref, kwv_ref, ck_ref, cv_ref, zm_ref, covt_ref, gexp_ref, o_ref,
                kaug_ref, vaug_ref, vwaug_ref, s_ref, mx_ref, acc_ref):
    tq = q_ref.shape[1]
    s_len = ksv_ref.shape[1]
    n_slc = s_len // NSA_SEL_LEN
    qi = pl.program_id(1)
    s0 = pl.multiple_of(qi * tq, tq)
    lane_grp = _iota((1, LANE), 1) // NSA_DH

    @pl.when(qi == 0)
    def _():
        lane = _iota((s_len, LANE), 1)
        key_blk = _iota((s_len, LANE), 0) // NSA_SEL_LEN
        k = ksv_ref[0, :, 0:LANE].astype(F32)
        v = ksv_ref[0, :, LANE:2 * LANE].astype(F32)
        vw = kwv_ref[0, :, LANE:2 * LANE].astype(F32)
        for g in range(NSA_GROUPS):
            own = lane // NSA_DH == g
            onehot = jnp.where(lane - (1 - g) * NSA_DH == key_blk, 1.0, 0.0)
            kaug_ref[g] = jnp.where(own, k, onehot).astype(BF16)
            vaug_ref[g] = jnp.where(own, v, 1.0).astype(BF16)
            vwaug_ref[g] = jnp.where(own, vw, 1.0).astype(BF16)

    t_col = s0 + _iota((tq, 1), 0)
    t_col2 = jnp.concatenate([t_col, t_col], axis=0)
    t_row = s0 + _iota((1, tq), 1)
    q = q_ref[0].astype(F32) * NSA_DH ** -0.5
    ck = ck_ref[0]
    ck_hi = ck.astype(BF16)
    ck_lo = (ck - ck_hi.astype(F32)).astype(BF16)
    cv = cv_ref[0]
    cmp_end = _iota((1, N_CMP_PAD), 1) * NSA_CMP_STRIDE + (NSA_CMP_LEN - 1)
    cmp_valid = cmp_end <= t_col2
    blk_t = _iota((n_slc, tq), 0)
    cur_t = t_row // NSA_SEL_LEN
    forced_t = (blk_t == 0) | (blk_t == cur_t) | (blk_t == cur_t - 1)
    future_t = blk_t > cur_t
    row_in_tile = jnp.concatenate([_iota((tq, tq), 0)] * NSA_HPG, axis=0)
    col_in_tile = _iota((NSA_HPG * tq, tq), 1)
    diag_ok = col_in_tile <= row_in_tile
    n_wt = NSA_WINDOW // tq + 1
    never = 2 * tq
    win_off = [pl.multiple_of(jnp.maximum(qi - (n_wt - 1 - w), 0) * tq, tq) for w in range(n_wt)]
    win_ok = [col_in_tile > row_in_tile + jnp.where(qi >= n_wt - 1, 0, never)]
    win_ok += [col_in_tile >= jnp.where(qi >= n_wt - 1 - w, 0, never) for w in range(1, n_wt - 1)]
    win_ok += [diag_ok]

    o_cmp = [jnp.zeros((tq, LANE), F32) for _ in range(NSA_HPG)]
    o_slc = [jnp.zeros((tq, LANE), F32) for _ in range(NSA_HPG)]
    o_win = [jnp.zeros((tq, LANE), F32) for _ in range(NSA_HPG)]
    groups = range(NSA_GROUPS)
    in_grp = [lane_grp == g for g in groups]
    q2 = [jnp.concatenate([jnp.where(in_grp[g], q[:, j * LANE:(j + 1) * LANE], 0.0)
                           for j in range(NSA_HPG)], axis=0) for g in groups]
    qb = [a.astype(BF16) for a in q2]
    p_c = [_masked_softmax(_dot_nt(qb[g], ck_hi) + _dot_nt(qb[g], ck_lo), cmp_valid) for g in groups]
    oc = [jnp.where(in_grp[g], _dot(p_c[g], cv), 0.0) for g in groups]
    imp_c = [a[0:tq] + a[tq:2 * tq] for a in p_c]
    imp_hi = [a.astype(BF16) for a in imp_c]
    imp_lo = [(a - h.astype(F32)).astype(BF16) for a, h in zip(imp_c, imp_hi)]
    imp_t = [(_dot_nt(covt_ref[...], h) + _dot_nt(covt_ref[...], l_))[0:n_slc]
             for h, l_ in zip(imp_hi, imp_lo)]
    imp_t = [jnp.where(forced_t, jnp.inf, jnp.where(future_t, -jnp.inf, a)) for a in imp_t]
    s_w = [[jnp.where(ok, _dot_nt(qb[g], kwv_ref[0, pl.ds(off, tq), 0:LANE]), -jnp.inf)
            for ok, off in zip(win_ok, win_off)] for g in groups]
    qa = []
    for g in groups:
        bias_t = jnp.where(_topk_rank(imp_t[g]) < NSA_N_SEL, 0.0, NEG_BIG)
        lo = (1 - g) * NSA_DH
        rows = ([jnp.zeros((lo, tq), F32)] if lo else []) + [bias_t, jnp.zeros((LANE - lo - n_slc, tq), F32)]
        bias = jnp.concatenate(rows, axis=0).T
        qa.append(jnp.where(in_grp[g], q2[g], jnp.concatenate([bias] * NSA_HPG, axis=0)).astype(BF16))
    m_w = []
    for g in groups:
        m_g = s_w[g][0]
        for s_t in s_w[g][1:]:
            m_g = jnp.maximum(m_g, s_t)
        m_w.append(jnp.maximum(jnp.max(m_g, axis=-1, keepdims=True), F32_LOWEST))
    ow = [jnp.zeros((NSA_HPG * tq, LANE), F32) for g in groups]
    for w, off in enumerate(win_off):
        for g in groups:
            ow[g] = ow[g] + jnp.dot(_exp_weights(s_w[g][w], m_w[g]), vwaug_ref[g, pl.ds(off, tq), :],
                                    preferred_element_type=F32)
    for g in groups:
        ow_g = _normalise_aug(ow[g], in_grp[g])
        for j in range(NSA_HPG):
            rows_j = slice(j * tq, (j + 1) * tq)
            o_cmp[j] = o_cmp[j] + oc[g][rows_j]
            o_win[j] = o_win[j] + ow_g[rows_j]

    half_max = lambda s_t: jnp.maximum(s_t[:, 0:LANE], s_t[:, LANE:2 * LANE])
    mx_ref[...] = jnp.full(mx_ref.shape, NEG_BIG, F32)

    def score_tiles(tiles):
        offs = [pl.multiple_of(kt * tq, tq) for kt in tiles]
        s_new = [[_dot_nt(qa[g], kaug_ref[g, pl.ds(off, tq), :]) for g in groups] for off in offs]
        for kt, s_kt in zip(tiles, s_new):
            for g in groups:
                s_ref[g, kt] = s_kt[g]
        for g in groups:
            m_new = half_max(s_new[0][g])
            for s_kt in s_new[1:]:
                m_new = jnp.maximum(m_new, half_max(s_kt[g]))
            mx_ref[g] = jnp.maximum(mx_ref[g], m_new)

    def score_pair(p, carry):
        score_tiles([2 * p, 2 * p + 1])
        return carry

    lax.fori_loop(0, qi // 2, score_pair, 0)

    @pl.when(qi % 2 == 1)
    def _():
        score_tiles([qi - 1])

    m_s = []
    for g in groups:
        s_d = jnp.where(diag_ok, _dot_nt(qa[g], kaug_ref[g, pl.ds(s0, tq), :]), NEG_BIG)
        s_ref[g, qi] = s_d
        m_g = jnp.max(jnp.maximum(mx_ref[g], half_max(s_d)), axis=-1, keepdims=True)
        m_s.append(jnp.maximum(m_g, F32_LOWEST))
    acc_ref[...] = jnp.zeros(acc_ref.shape, F32)

    def attend_tiles(tiles):
        offs = [pl.multiple_of(kt * tq, tq) for kt in tiles]
        pv = [[jnp.dot(_exp_weights(s_ref[g, kt], m_s[g]), vaug_ref[g, pl.ds(off, tq), :],
                       preferred_element_type=F32) for g in groups] for kt, off in zip(tiles, offs)]
        for g in groups:
            upd = pv[0][g]
            for pv_kt in pv[1:]:
                upd = upd + pv_kt[g]
            acc_ref[g] += upd

    def attend_pair(p, carry):
        attend_tiles([2 * p, 2 * p + 1])
        return carry

    lax.fori_loop(0, (qi + 1) // 2, attend_pair, 0)

    @pl.when(qi % 2 == 0)
    def _():
        attend_tiles([qi])

    for g in groups:
        os_ = _normalise_aug(acc_ref[g], lane_grp == g)
        for j in range(NSA_HPG):
            o_slc[j] = o_slc[j] + os_[j * tq:(j + 1) * tq]

    gates = _split2_dot(jax.nn.sigmoid(zm_ref[0]), gexp_ref[...])
    o = (gates[:, 0:MIX_W] * jnp.concatenate(o_cmp, axis=1)
         + gates[:, MIX_W:2 * MIX_W] * jnp.concatenate(o_slc, axis=1)
         + gates[:, 2 * MIX_W:3 * MIX_W] * jnp.concatenate(o_win, axis=1))
    o_ref[0] = o.astype(o_ref.dtype)


def _nsa(zn, zm, ck, cv, covt, gexp):
    b, s, _ = zn.shape
    tq = TQ_NSA
    return pl.pallas_call(
        _nsa_kernel,
        grid=(b, s // tq),
        in_specs=[pl.BlockSpec((1, tq, 256), lambda i, j: (i, j, 0)),
                  pl.BlockSpec((1, s, 256), lambda i, j: (i, 0, 1)),
                  pl.BlockSpec((1, s, 256), lambda i, j: (i, 0, 2)),
                  pl.BlockSpec((1, N_CMP_PAD, LANE), lambda i, j: (i, 0, 0)),
                  pl.BlockSpec((1, N_CMP_PAD, LANE), lambda i, j: (i, 0, 0)),
                  pl.BlockSpec((1, tq, ZM_W), lambda i, j: (i, j, 0)),
                  pl.BlockSpec((N_SLC_PAD, N_CMP_PAD), lambda i, j: (0, 0)),
                  pl.BlockSpec((ZM_W, 3 * MIX_W), lambda i, j: (0, 0))],
        out_specs=pl.BlockSpec((1, tq, MIX_W), lambda i, j: (i, j, 0)),
        out_shape=jax.ShapeDtypeStruct((b, s, MIX_W), BF16),
        scratch_shapes=[pltpu.VMEM((NSA_GROUPS, s, LANE), BF16)] * 3
                       + [pltpu.VMEM((NSA_GROUPS, s // tq, NSA_HPG * tq, tq), F32),
                          pltpu.VMEM((NSA_GROUPS, NSA_HPG * tq, LANE), F32),
                          pltpu.VMEM((NSA_GROUPS, NSA_HPG * tq, LANE), F32)],
        compiler_params=_cparams(("parallel", "arbitrary")),
        name="nsa_attn",
    )(zn, zn, zn, ck, cv, zm, covt, gexp)


def _combine_kernel(x_ref, g_ref, op_ref, oa_ref, od_ref, on_ref, wg_ref, bg_ref, wb_ref, wo_ref, o_ref):
    x = x_ref[...]
    h = _rms(x, g_ref[...]).astype(BF16)
    y = jnp.zeros(x.shape, F32)
    for i, br_ref in enumerate((op_ref, oa_ref, od_ref, on_ref)):
        gate = jax.nn.sigmoid(jnp.dot(h, wg_ref[i], preferred_element_type=F32) + bg_ref[i])
        y = y + gate * jnp.dot(br_ref[...], wb_ref[i], preferred_element_type=F32)
    o_ref[...] = x + _dot(y, wo_ref[...])


def _combine(x2, g, branches, wg, bg, wb, wo):
    t = x2.shape[0]
    tm = TM_COMB
    return pl.pallas_call(
        _combine_kernel,
        grid=(t // tm,),
        in_specs=[pl.BlockSpec((tm, D_MODEL), lambda i: (i, 0)),
                  pl.BlockSpec((1, D_MODEL), lambda i: (0, 0))]
                 + [pl.BlockSpec((tm, MIX_W), lambda i: (i, 0))] * N_BRANCH
                 + [pl.BlockSpec((N_BRANCH, D_MODEL, D_MODEL), lambda i: (0, 0, 0)),
                    pl.BlockSpec((N_BRANCH, 1, D_MODEL), lambda i: (0, 0, 0)),
                    pl.BlockSpec((N_BRANCH, MIX_W, D_MODEL), lambda i: (0, 0, 0)),
                    pl.BlockSpec((D_MODEL, D_MODEL), lambda i: (0, 0))],
        out_specs=pl.BlockSpec((tm, D_MODEL), lambda i: (i, 0)),
        out_shape=jax.ShapeDtypeStruct((t, D_MODEL), F32),
        compiler_params=_cparams(("parallel",)),
        name="combine",
    )(x2, g, *branches, wg, bg, wb, wo)


def _memkv_kernel(m_ref, g_ref, w_ref, o_ref):
    o_ref[0] = _dot(_rms(m_ref[0], g_ref[...]), w_ref[...]).astype(o_ref.dtype)


def _memkv(mem, g, w):
    b, m, _ = mem.shape
    n = 2 * X_HEADS * X_DH
    return pl.pallas_call(
        _memkv_kernel,
        grid=(b,),
        in_specs=[pl.BlockSpec((1, m, D_MODEL), lambda i: (i, 0, 0)),
                  pl.BlockSpec((1, D_MODEL), lambda i: (0, 0)),
                  pl.BlockSpec((D_MODEL, n), lambda i: (0, 0))],
        out_specs=pl.BlockSpec((1, m, n), lambda i: (i, 0, 0)),
        out_shape=jax.ShapeDtypeStruct((b, m, n), BF16),
        compiler_params=_cparams(("parallel",)),
        name="mem_kv",
    )(mem, g, w)


def _cross_kernel(x_ref, g_ref, wq_ref, kv_ref, wo_ref, o_ref):
    x = x_ref[0]
    hn = _rms(x, g_ref[...])
    q = _dot(hn, wq_ref[...])
    n_k = X_HEADS * X_DH
    outs = []
    for h in range(X_HEADS):
        k_h = kv_ref[0, :, h * X_DH:(h + 1) * X_DH]
        v_h = kv_ref[0, :, n_k + h * X_DH:n_k + (h + 1) * X_DH]
        sc = _dot_nt(q[:, h * X_DH:(h + 1) * X_DH], k_h) * X_DH ** -0.5
        e = jnp.exp(sc - jnp.max(sc, axis=-1, keepdims=True))
        p = e / jnp.sum(e, axis=-1, keepdims=True)
        outs.append(_dot(p, v_h))
    o_ref[0] = x + _dot(jnp.concatenate(outs, axis=1), wo_ref[...])


def _cross(x3, g, wq, kv, wo):
    b, s, _ = x3.shape
    tm = TM_CROSS
    m = kv.shape[1]
    n_k = X_HEADS * X_DH
    return pl.pallas_call(
        _cross_kernel,
        grid=(b, s // tm),
        in_specs=[pl.BlockSpec((1, tm, D_MODEL), lambda i, j: (i, j, 0)),
                  pl.BlockSpec((1, D_MODEL), lambda i, j: (0, 0)),
                  pl.BlockSpec((D_MODEL, n_k), lambda i, j: (0, 0)),
                  pl.BlockSpec((1, m, 2 * n_k), lambda i, j: (i, 0, 0)),
                  pl.BlockSpec((n_k, D_MODEL), lambda i, j: (0, 0))],
        out_specs=pl.BlockSpec((1, tm, D_MODEL), lambda i, j: (i, j, 0)),
        out_shape=jax.ShapeDtypeStruct((b, s, D_MODEL), F32),
        compiler_params=_cparams(("parallel", "parallel")),
        name="cross_attn",
    )(x3, g, wq, kv, wo)


def _ffn_kernel(x_ref, g_ref, wup_ref, cw_ref, cb_ref, wd_ref, gf_ref, o_ref, tail_ref, act_ref, *, final):
    @pl.when(pl.program_id(1) == 0)
    def _():
        tail_ref[...] = jnp.zeros_like(tail_ref)

    ts = x_ref.shape[1]
    x = x_ref[0]
    hn = _rms(x, g_ref[...]).astype(BF16)
    for c in range(D_FF // FF_CHUNK):
        cols = slice(c * FF_CHUNK, (c + 1) * FF_CHUNK)
        gcols = slice(D_FF + c * FF_CHUNK, D_FF + (c + 1) * FF_CHUNK)
        u = jnp.dot(hn, wup_ref[:, cols], preferred_element_type=F32)
        v = jnp.dot(hn, wup_ref[:, gcols], preferred_element_type=F32)
        ext = jnp.concatenate([tail_ref[:, cols], u], axis=0)
        tail_ref[:, cols] = u[ts - SUBLANE:, :]
        cw = cw_ref[:, cols]
        y = (cw[2:3] * u + cw[1:2] * pltpu.roll(ext, 1, 0)[SUBLANE:]
             + cw[0:1] * pltpu.roll(ext, 2, 0)[SUBLANE:] + cb_ref[:, cols])
        act_ref[:, cols] = (jax.nn.gelu(y) * v).astype(BF16)
    out = x + jnp.dot(act_ref[...], wd_ref[...], preferred_element_type=F32)
    if final:
        out = _rms(out, gf_ref[...])
    o_ref[0] = out


def _ffn(x3, g, wup, cw, cb, wd, gf, final):
    b, s, _ = x3.shape
    ts = TS_FFN
    return pl.pallas_call(
        functools.partial(_ffn_kernel, final=final),
        grid=(b, s // ts),
        in_specs=[pl.BlockSpec((1, ts, D_MODEL), lambda i, j: (i, j, 0)),
                  pl.BlockSpec((1, D_MODEL), lambda i, j: (0, 0)),
                  pl.BlockSpec((D_MODEL, 2 * D_FF), lambda i, j: (0, 0), pipeline_mode=pl.Buffered(1)),
                  pl.BlockSpec((FFN_CONV, D_FF), lambda i, j: (0, 0)),
                  pl.BlockSpec((1, D_FF), lambda i, j: (0, 0)),
                  pl.BlockSpec((D_FF, D_MODEL), lambda i, j: (0, 0), pipeline_mode=pl.Buffered(1)),
                  pl.BlockSpec((1, D_MODEL), lambda i, j: (0, 0))],
        out_specs=pl.BlockSpec((1, ts, D_MODEL), lambda i, j: (i, j, 0)),
        out_shape=jax.ShapeDtypeStruct((b, s, D_MODEL), F32),
        scratch_shapes=[pltpu.VMEM((SUBLANE, D_FF), F32), pltpu.VMEM((ts, D_FF), BF16)],
        compiler_params=_cparams(("parallel", "arbitrary")),
        name="conv_ffn",
    )(x3, g, wup, cw, cb, wd, gf)


def _inproj_columns():
    starts = np.concatenate([[0], np.cumsum(IN_SPLITS)])
    (p_in, a_q, a_k, a_v, a_r, a_lr, d_q, d_k, d_v, d_b, d_a, d_g,
     n_q, n_kc, n_vc, n_ks, n_vs, n_kw, n_vw, n_g) = [np.arange(starts[i], starts[i + 1])
                                                      for i in range(len(IN_SPLITS))]
    n_q = n_q.reshape(NSA_GROUPS, NSA_HPG, NSA_DH).transpose(1, 0, 2).reshape(-1)
    misc = np.full((ZM_W,), N_IN)
    misc[MISC_LR:MISC_LR + GLA_LOWRANK] = a_lr
    misc[MISC_B:MISC_B + GDN_HEADS] = d_b
    misc[MISC_A:MISC_A + GDN_HEADS] = d_a
    misc[MISC_G:MISC_G + 3 * NSA_HEADS] = n_g
    cols = np.concatenate([p_in, a_q, a_k, a_v, a_r, d_q, d_k, d_v, d_g,
                           n_q, n_ks, n_vs, n_kw, n_vw, n_kc, n_vc, misc])
    assert cols.shape[0] == Z_W
    return cols


def _head_expand(offset, n_heads, width):
    e = np.zeros((ZM_W, n_heads * width), np.float32)
    for h in range(n_heads):
        e[offset + h, h * width:(h + 1) * width] = 1.0
    return e


def _nsa_constants(s):
    n_cmp = s // NSA_CMP_STRIDE - NSA_CMP_LEN // NSA_CMP_STRIDE + 1
    n_slc = s // NSA_SEL_LEN
    c_start = np.arange(n_cmp) * NSA_CMP_STRIDE
    s_start = np.arange(n_slc) * NSA_SEL_LEN
    cover = np.zeros((N_CMP_PAD, N_SLC_PAD), np.float32)
    cover[:n_cmp, :n_slc] = ((c_start[:, None] <= s_start[None, :] + NSA_SEL_LEN - 1)
                             & (c_start[:, None] + NSA_CMP_LEN - 1 >= s_start[None, :]))
    gexp = np.zeros((ZM_W, 3, MIX_W), np.float32)
    for g in range(NSA_GROUPS):
        for j in range(NSA_HPG):
            slot = j * NSA_GROUPS + g
            for c in range(3):
                gexp[MISC_G + (g * NSA_HPG + j) * 3 + c, c, slot * NSA_DH:(slot + 1) * NSA_DH] = 1.0
    return jnp.asarray(cover.T, dtype=BF16), jnp.asarray(gexp.reshape(ZM_W, 3 * MIX_W), dtype=BF16)


def _block_diag(blocks):
    n, a, b = blocks.shape
    return jnp.einsum('gh,gab->gahb', jnp.eye(n, dtype=blocks.dtype), blocks).reshape(n * a, n * b)


def kernel(x, mem, g_mix, w_in, pool_w, pool_scale, gla_w_lr, gla_b_lr, gla_g_norm, gdn_conv, gdn_a_log,
           gdn_dt_bias, gdn_g_norm, nsa_pe, nsa_cmp_w1, nsa_cmp_w2, w_branch, w_gate, b_gate, w_out, g_cross,
           g_mem, w_xq, w_mem_kv, w_xo, g_ffn, w_up, ffn_conv, ffn_conv_b, w_down, g_final):
    b, s, d = x.shape
    depth = w_in.shape[0]
    t = b * s
    cols = _inproj_columns()
    covt, gexp = _nsa_constants(s)
    eba = jnp.asarray(np.concatenate([_head_expand(MISC_B, GDN_HEADS, GDN_DH),
                                      _head_expand(MISC_A, GDN_HEADS, GDN_DH)], axis=1), dtype=BF16)
    ones_bd = _block_diag(jnp.ones((GDN_HEADS, GDN_DH, GDN_DH), BF16))
    nsa_rows = np.arange(MIX_W).reshape(NSA_GROUPS, NSA_HPG, NSA_DH).transpose(1, 0, 2).reshape(-1)
    row = lambda v: v.reshape(1, -1).astype(F32)
    misc_a = lambda v: jnp.zeros((1, ZM_W), F32).at[0, MISC_A:MISC_A + GDN_HEADS].set(v)

    x2 = x.reshape(t, d)
    for l in range(depth):
        w_in_r = jnp.concatenate([w_in[l], jnp.zeros((d, 1), F32)], axis=1)[:, cols].astype(BF16)
        zp, zg, zd, zn, zc, zm = (z.reshape(b, s, -1) for z in _inproj(x2, row(g_mix[l]), w_in_r))

        o_pool = _pool(zp, _block_diag(pool_w[l]).astype(BF16), row(pool_scale[l]))

        wlr = jnp.zeros((ZM_W, GLA_HEADS * GLA_DK), F32).at[MISC_LR:MISC_LR + GLA_LOWRANK].set(gla_w_lr[l])
        wlr_hi = wlr.astype(BF16)
        wlr2 = jnp.stack([wlr_hi, (wlr - wlr_hi.astype(F32)).astype(BF16)])
        o_gla, o_gdn = _recurrent(zg, zd, zm, wlr2, row(gla_b_lr[l]), row(jnp.tile(gla_g_norm[l], GLA_HEADS)),
                                  gdn_conv[l], eba, misc_a(gdn_a_log[l]), misc_a(gdn_dt_bias[l]),
                                  row(jnp.tile(gdn_g_norm[l], GDN_HEADS)), ones_bd)

        pe_x = jnp.tile(nsa_pe[l], (1, 1, NSA_GROUPS))
        w1 = nsa_cmp_w1[l].reshape(2, NSA_CMP_LEN, NSA_DH, NSA_DH)
        eye_g = jnp.eye(NSA_GROUPS, dtype=F32)
        w1_bd = jnp.einsum('gh,kpde->kpgdhe', eye_g, w1).reshape(2, NSA_CMP_LEN, LANE, LANE).astype(BF16)
        w2_bd = jnp.einsum('gh,kde->kgdhe', eye_g, nsa_cmp_w2[l]).reshape(2, LANE, LANE).astype(BF16)
        ck, cv = _nsa_compress(zc, pe_x, w1_bd, w2_bd)
        o_nsa = _nsa(zn, zm, ck, cv, covt, gexp)

        wb = jnp.concatenate([w_branch[l, :3], w_branch[l, 3][nsa_rows][None]], axis=0).astype(BF16)
        branches = [o.reshape(t, MIX_W) for o in (o_pool, o_gla, o_gdn, o_nsa)]
        x2 = _combine(x2, row(g_mix[l]), branches, w_gate[l].astype(BF16),
                      b_gate[l].reshape(N_BRANCH, 1, d), wb, w_out[l].astype(BF16))

        kv = _memkv(mem, row(g_mem[l]), w_mem_kv[l].astype(BF16))
        x3 = _cross(x2.reshape(b, s, d), row(g_cross[l]), w_xq[l].astype(BF16), kv, w_xo[l].astype(BF16))

        x3 = _ffn(x3, row(g_ffn[l]), w_up[l].astype(BF16), ffn_conv[l], row(ffn_conv_b[l]),
                  w_down[l].astype(BF16), row(g_final), final=(l == depth - 1))
        x2 = x3.reshape(t, d)
    return x2.reshape(b, s, d)
```

```python
import functools

import numpy as np
import jax
import jax.numpy as jnp
from jax import lax
from jax.experimental import pallas as pl
from jax.experimental.pallas import tpu as pltpu

F32 = jnp.float32
BF16 = jnp.bfloat16
HIGHEST = lax.Precision.HIGHEST

D_MODEL = 1024
MIX_W = 256
POOL_WINDOWS = (2, 4, 8, 16)
POOL_GW = 64
GLA_HEADS = 4
GLA_DK = 32
GLA_DV = 64
GLA_LOWRANK = 16
GLA_GATE_NORM = 16.0
CHUNK = 64
GDN_HEADS = 4
GDN_DH = 64
GDN_CONV = 4
NSA_HEADS = 4
NSA_GROUPS = 2
NSA_HPG = 2
NSA_DH = 64
NSA_KV = 128
NSA_CMP_LEN = 32
NSA_CMP_STRIDE = 16
NSA_SEL_LEN = 64
NSA_N_SEL = 16
NSA_WINDOW = 512
X_HEADS = 4
X_DH = 128
D_FF = 2816
FFN_CONV = 3
EPS = 1e-6
N_BRANCH = 4

IN_SPLITS = (MIX_W,
             128, 128, 256, 256, GLA_LOWRANK,
             MIX_W, MIX_W, MIX_W, GDN_HEADS, GDN_HEADS, MIX_W,
             256, NSA_KV, NSA_KV, NSA_KV, NSA_KV, NSA_KV, NSA_KV, 3 * NSA_HEADS)
N_IN = sum(IN_SPLITS)

MISC_LR = 0
MISC_B = 16
MISC_A = 20
MISC_G = 24
LANE = 128
SUBLANE = 8

Z_WIDTHS = (256, 768, 1024, 768, 256, 128)
Z_DTYPES = (F32, BF16, BF16, BF16, F32, F32)
ZP_W, ZG_W, ZD_W, ZN_W, ZC_W, ZM_W = Z_WIDTHS
Z_W = sum(Z_WIDTHS)
NEG_BIG = -1e30
F32_LOWEST = float(np.finfo(np.float32).min)

TM_PROJ = 512
TM_COMB = 512
TM_CROSS = 512
TS_FFN = 512
FF_CHUNK = 256
TQ_NSA = 256
NSA_SEQS = 2
REC_SEQS = 8
REC_LAG = 3
N_SLC_PAD = 128
N_CMP_PAD = 128

VMEM_LIMIT = 56 * 1024 * 1024


def _cparams(sem):
    return pltpu.CompilerParams(dimension_semantics=sem, vmem_limit_bytes=VMEM_LIMIT)


def _rms(x, g):
    return x * lax.rsqrt(jnp.mean(x * x, axis=-1, keepdims=True) + EPS) * g


def _dot(a, b):
    return jnp.dot(a.astype(BF16), b.astype(BF16), preferred_element_type=F32)


def _dot_nt(a, b):
    return lax.dot_general(a.astype(BF16), b.astype(BF16), (((1,), (1,)), ((), ())),
                           preferred_element_type=F32)


def _dot_tn(a, b):
    return lax.dot_general(a.astype(BF16), b.astype(BF16), (((0,), (0,)), ((), ())),
                           preferred_element_type=F32)


def _dot_hi(a, b):
    return jnp.dot(a, b, precision=HIGHEST, preferred_element_type=F32)


def _dot_nt_hi(a, b):
    return lax.dot_general(a, b, (((1,), (1,)), ((), ())), precision=HIGHEST,
                           preferred_element_type=F32)


def _split2_dot(a, b):
    hi = a.astype(BF16)
    lo = (a - hi.astype(F32)).astype(BF16)
    return jnp.dot(hi, b, preferred_element_type=F32) + jnp.dot(lo, b, preferred_element_type=F32)


def _split3_rhs_dot(a, b):
    hi = b.astype(BF16)
    r1 = b - hi.astype(F32)
    mid = r1.astype(BF16)
    lo = (r1 - mid.astype(F32)).astype(BF16)
    return ((jnp.dot(a, hi, preferred_element_type=F32) + jnp.dot(a, mid, preferred_element_type=F32))
            + jnp.dot(a, lo, preferred_element_type=F32))


def _split3_lhs_dot(a, b):
    hi = a.astype(BF16)
    r1 = a - hi.astype(F32)
    mid = r1.astype(BF16)
    lo = (r1 - mid.astype(F32)).astype(BF16)
    return ((jnp.dot(hi, b, preferred_element_type=F32) + jnp.dot(mid, b, preferred_element_type=F32))
            + jnp.dot(lo, b, preferred_element_type=F32))


def _dot3(a, w_hi, w_lo):
    a_hi = a.astype(BF16)
    a_lo = (a - a_hi.astype(F32)).astype(BF16)
    return (jnp.dot(a_hi, w_hi, preferred_element_type=F32)
            + (jnp.dot(a_lo, w_hi, preferred_element_type=F32) + jnp.dot(a_hi, w_lo, preferred_element_type=F32)))


def _iota(shape, axis):
    return lax.broadcasted_iota(jnp.int32, shape, axis)


def _block_mask(rows, cols, rb, cb):
    return (_iota((rows, cols), 0) // rb) == (_iota((rows, cols), 1) // cb)


def _shift_rows(x, k):
    t = _iota(x.shape, 0)
    return jnp.where(t >= k, pltpu.roll(x, k, 0), 0.0)


def _cumsum_rows(x):
    k = 1
    while k < x.shape[0]:
        x = x + _shift_rows(x, k)
        k *= 2
    return x


def _softplus(x):
    return jnp.maximum(x, 0.0) + jnp.log1p(jnp.exp(-jnp.abs(x)))


def _log_sigmoid(x):
    return -_softplus(-x)


def _silu(x):
    return x * jax.nn.sigmoid(x)


def _masked_softmax(s, mask):
    s = jnp.where(mask, s, -jnp.inf)
    m = jnp.maximum(jnp.max(s, axis=-1, keepdims=True), F32_LOWEST)
    e = jnp.exp(s - m)
    den = jnp.sum(e, axis=-1, keepdims=True)
    return e / jnp.where(den > 0, den, 1.0)


def _tile4(x):
    return jnp.concatenate([x, x, x, x], axis=0)


def _inproj_kernel(x_ref, g_ref, w_ref, *z_refs):
    h = _rms(x_ref[...], g_ref[...]).astype(BF16)
    off = 0
    for ref in z_refs:
        n = ref.shape[-1]
        ref[...] = jnp.dot(h, w_ref[:, off:off + n], preferred_element_type=F32).astype(ref.dtype)
        off += n


def _inproj(x2, g, w):
    t = x2.shape[0]
    return pl.pallas_call(
        _inproj_kernel,
        grid=(t // TM_PROJ,),
        in_specs=[pl.BlockSpec((TM_PROJ, D_MODEL), lambda i: (i, 0)),
                  pl.BlockSpec((1, D_MODEL), lambda i: (0, 0)),
                  pl.BlockSpec((D_MODEL, Z_W), lambda i: (0, 0))],
        out_specs=[pl.BlockSpec((TM_PROJ, n), lambda i: (i, 0)) for n in Z_WIDTHS],
        out_shape=[jax.ShapeDtypeStruct((t, n), dt) for n, dt in zip(Z_WIDTHS, Z_DTYPES)],
        compiler_params=_cparams(("parallel",)),
        name="inproj",
    )(x2, g, w)


def _pool_kernel(u_ref, w_ref, sc_ref, o_ref):
    u = u_ref[0]
    s2 = u + _shift_rows(u, 1)
    s4 = s2 + _shift_rows(s2, 2)
    s8 = s4 + _shift_rows(s4, 4)
    s16 = s8 + _shift_rows(s8, 8)
    grp = _iota(u.shape, 1) // POOL_GW
    win = jnp.where(grp == 0, s2, jnp.where(grp == 1, s4, jnp.where(grp == 2, s8, s16)))
    width = jnp.where(grp == 0, POOL_WINDOWS[0],
                      jnp.where(grp == 1, POOL_WINDOWS[1],
                                jnp.where(grp == 2, POOL_WINDOWS[2], POOL_WINDOWS[3])))
    cnt = jnp.minimum(_iota(u.shape, 0) + 1, width).astype(F32)
    diff = win / cnt - u
    o_ref[0] = (_dot(diff, w_ref[...]) * sc_ref[...]).astype(o_ref.dtype)


def _pool(zp, w_bd, scale):
    b, s, _ = zp.shape
    return pl.pallas_call(
        _pool_kernel,
        grid=(b,),
        in_specs=[pl.BlockSpec((1, s, MIX_W), lambda i: (i, 0, 0)),
                  pl.BlockSpec((MIX_W, MIX_W), lambda i: (0, 0)),
                  pl.BlockSpec((1, MIX_W), lambda i: (0, 0))],
        out_specs=pl.BlockSpec((1, s, MIX_W), lambda i: (i, 0, 0)),
        out_shape=jax.ShapeDtypeStruct((b, s, MIX_W), BF16),
        compiler_params=_cparams(("parallel",)),
        name="pool",
    )(zp, w_bd, scale)


def _gla_masks():
    return (_block_mask(4 * CHUNK, 128, CHUNK, GLA_DK),
            _block_mask(4 * CHUNK, 256, CHUNK, GLA_DV),
            _block_mask(256, 128, GLA_DV, GLA_DK),
            (_iota((CHUNK, 256), 1) % CHUNK) <= _iota((CHUNK, 256), 0))


def _seqs(f, *lists):
    return [f(*args) for args in zip(*lists)]


def _run_staggered(stage_gens, starts):
    live = list(range(len(stage_gens)))
    tick = 0
    while live:
        for i in list(live):
            if tick >= starts[i]:
                try:
                    next(stage_gens[i])
                except StopIteration:
                    live.remove(i)
        tick += 1


def _gla_chunk(masks, zg, zm, wlr_ref, blr_ref, gn_ref, ones_ref, st_refs, out):
    mask_k, mask_v, mask_st, causal = masks
    q = [z[:, 0:128] * GLA_DK ** -0.5 for z in zg]
    k = [z[:, 128:256] for z in zg]
    v = [z[:, 256:512] for z in zg]
    r = [z[:, 512:768] for z in zg]
    pre = _seqs(lambda a: _dot3(a, wlr_ref[0], wlr_ref[1]), zm)
    yield
    bc = _seqs(lambda a: _cumsum_rows(_log_sigmoid(a + blr_ref[...]) / GLA_GATE_NORM), pre)
    bl = [a[CHUNK - 1:CHUNK, :] for a in bc]
    q_e = _seqs(lambda a, c: a * jnp.exp(c), q, bc)
    k_e = _seqs(lambda a, c: a * jnp.exp(-c), k, bc)
    k_u = _seqs(lambda a, c, l: a * jnp.exp(l - c), k, bc, bl)
    st = [ref[...] for ref in st_refs]
    yield
    att = _seqs(lambda a, b: jnp.where(causal, _dot_nt(a, jnp.where(mask_k, _tile4(b), 0.0)), 0.0), q_e, k_e)
    inter = _seqs(_dot_nt, q_e, st)
    kv = _seqs(_dot_tn, v, k_u)
    yield
    o = _seqs(lambda a, b, c: _dot(a, _tile4(b.astype(BF16)) * ones_ref[...]) + c, att, v, inter)
    for ref, s_old, l, upd in zip(st_refs, st, bl, kv):
        ref[...] = s_old * jnp.exp(l) + jnp.where(mask_st, upd, 0.0)
    yield
    ms = _seqs(lambda a: _dot(a * a, ones_ref[...]) * (1.0 / GLA_DV), o)
    yield
    out[:] = _seqs(lambda a, m, g: a * lax.rsqrt(m + EPS) * gn_ref[...] * _silu(g), o, ms, r)


GDN_HALO = 16


def _gdn_masks():
    c4 = 4 * CHUNK
    col = _iota((CHUNK, c4), 1) % CHUNK
    row = _iota((CHUNK, c4), 0)
    return (_block_mask(c4, c4, CHUNK, CHUNK), col <= row, col < row, col == row)


def _gdn_chunk(masks, halo, zd, zm, cw_ref, eba_ref, alog_ref, dtb_ref, gn_ref, ones_ref, st_refs, out):
    mask_bd, incl, strict, diag = masks
    cw = cw_ref[...]
    bd = lambda a: _tile4(a.astype(BF16)) * ones_ref[...]

    def conv_silu(h, z):
        ext = jnp.concatenate([h, z[:, 0:768]], axis=0)
        conv = (cw[3:4] * ext + cw[2:3] * pltpu.roll(ext, 1, 0)
                + cw[1:2] * pltpu.roll(ext, 2, 0) + cw[0:1] * pltpu.roll(ext, 3, 0))
        return _silu(conv[GDN_HALO:, :])

    qkv = _seqs(conv_silu, halo, zd)
    yield
    v = [a[:, 512:768] for a in qkv]
    gate = [z[:, 768:1024] for z in zd]
    ssq = _seqs(lambda a: _dot(jnp.concatenate([a[:, 0:256] * a[:, 0:256], a[:, 256:512] * a[:, 256:512]],
                                               axis=0), ones_ref[...]), qkv)
    yield
    q = _seqs(lambda a, s: a[:, 0:256] * lax.rsqrt(s[0:CHUNK] + EPS) * GDN_DH ** -0.5, qkv, ssq)
    k = _seqs(lambda a, s: a[:, 256:512] * lax.rsqrt(s[CHUNK:2 * CHUNK] + EPS), qkv, ssq)
    beta = _seqs(lambda a: _dot(jax.nn.sigmoid(a), eba_ref[:, 0:MIX_W]), zm)
    yield
    gc = _seqs(lambda a: _split3_lhs_dot(_cumsum_rows(-jnp.exp(alog_ref[...]) * _softplus(a + dtb_ref[...])),
                                         eba_ref[:, MIX_W:2 * MIX_W]), zm)
    yield
    gl = [a[CHUNK - 1:CHUNK, :] for a in gc]
    ones_cc = jnp.ones((CHUNK, CHUNK), BF16)
    g_row = _seqs(lambda a: _split3_rhs_dot(ones_cc, jnp.where(diag, a, 0.0)), gc)
    yield
    decay =_seqs(lambda a, b: jnp.exp(jnp.where(incl, a - b, -jnp.inf)), gc, g_row)
    kb = _seqs(lambda a, b: a * b, k, beta)
    vb = _seqs(lambda a, b: a * b, v, beta)
    kq = _seqs(lambda a, b, c: _dot_nt(jnp.concatenate([a, b], axis=0), bd(c)), kb, q, k)
    yield
    n_mat =_seqs(lambda s, d: jnp.where(strict, s[0:CHUNK] * d, 0.0), kq, decay)
    a_qk = _seqs(lambda s, d: s[CHUNK:2 * CHUNK] * d, kq, decay)
    m = _seqs(lambda a: -a, n_mat)
    x = _seqs(lambda a: jnp.where(diag, 1.0, 0.0) + a, m)
    m = _seqs(lambda a: _dot(a, bd(a)), m)
    yield
    p = 2
    while p < CHUNK // 2:
        xm = _seqs(lambda a, b: _dot(jnp.concatenate([a, b], axis=0), bd(b)), x, m)
        x = _seqs(lambda a, r: a + r[0:CHUNK], x, xm)
        m = [r[CHUNK:2 * CHUNK] for r in xm]
        p *= 2
        yield
    x = _seqs(lambda a, b: a + _dot(a, bd(b)), x, m)
    yield
    uw = _seqs(lambda a, b, c, g: _dot(a, jnp.concatenate([bd(b), bd(c * jnp.exp(g))], axis=1)),
               x, vb, kb, gc)
    yield
    st = [ref[...] for ref in st_refs]
    ws = _seqs(lambda a, b, g, s: _dot(jnp.concatenate([a[:, 256:512], b * jnp.exp(g)], axis=0), s),
               uw, q, gc, st)
    yield
    v_new = _seqs(lambda a, r: a[:, 0:256] - r[0:CHUNK], uw, ws)
    o = _seqs(lambda r, qk, vn: r[CHUNK:2 * CHUNK] + _dot(qk, bd(vn)), ws, a_qk, v_new)
    upd = _seqs(lambda a, g, l, vn: _dot_tn(a * jnp.exp(l - g), vn), k, gc, gl, v_new)
    yield
    for ref, s_old, l, u in zip(st_refs, st, gl, upd):
        ref[...] = s_old * jnp.exp(l) + jnp.where(mask_bd, u, 0.0)
    ms = _seqs(lambda a: _dot(a * a, ones_ref[...]) * (1.0 / GDN_DH), o)
    yield
    out[:] = _seqs(lambda a, m_, g: a * lax.rsqrt(m_ + EPS) * gn_ref[...] * _silu(g), o, ms, gate)


def _rec_kernel(zg_ref, zd_ref, zm_ref, wlr_ref, blr_ref, gng_ref, cw_ref, eba_ref, alog_ref, dtb_ref,
                gnd_ref, ones_ref, og_ref, od_ref, halo_ref, *st_refs):
    nb = zg_ref.shape[0]

    @pl.when(pl.program_id(1) == 0)
    def _():
        halo_ref[...] = jnp.zeros_like(halo_ref)
        for st_ref in st_refs:
            st_ref[...] = jnp.zeros_like(st_ref)

    zm = [zm_ref[i] for i in range(nb)]
    zg = [zg_ref[i].astype(F32) for i in range(nb)]
    zd = [zd_ref[i].astype(F32) for i in range(nb)]
    halo = [halo_ref[i] for i in range(nb)]
    for i in range(nb):
        halo_ref[i] = zd[i][CHUNK - GDN_HALO:, 0:768]
    halves = [slice(0, nb // 2), slice(nb // 2, nb)] if nb > 1 else [slice(0, nb)]
    og = [None] * nb
    od_parts = [[None] * (h.stop - h.start) for h in halves]
    gdn_masks = _gdn_masks()
    gens = [_gdn_chunk(gdn_masks, halo[h], zd[h], zm[h], cw_ref, eba_ref, alog_ref, dtb_ref, gnd_ref, ones_ref,
                       st_refs[nb:][h], part) for h, part in zip(halves, od_parts)]
    gens.append(_gla_chunk(_gla_masks(), zg, zm, wlr_ref, blr_ref, gng_ref, ones_ref, st_refs[:nb], og))
    _run_staggered(gens, [REC_LAG * i for i in range(len(gens))])
    od = [o for part in od_parts for o in part]
    for i in range(nb):
        og_ref[i] = og[i].astype(og_ref.dtype)
        od_ref[i] = od[i].astype(od_ref.dtype)


def _recurrent(zg, zd, zm, wlr2, blr, gn_gla, cw, eba, alog, dtb, gn_gdn, ones_bd):
    b, s, _ = zg.shape
    nb = REC_SEQS if b % REC_SEQS == 0 else 1
    const = lambda shape: pl.BlockSpec(shape, lambda i, c: (0,) * len(shape))
    return pl.pallas_call(
        _rec_kernel,
        grid=(b // nb, s // CHUNK),
        in_specs=[pl.BlockSpec((nb, CHUNK, ZG_W), lambda i, c: (i, c, 0)),
                  pl.BlockSpec((nb, CHUNK, ZD_W), lambda i, c: (i, c, 0)),
                  pl.BlockSpec((nb, CHUNK, ZM_W), lambda i, c: (i, c, 0)),
                  const((2, ZM_W, 128)), const((1, 128)), const((1, MIX_W)),
                  const((GDN_CONV, 768)), const((ZM_W, 2 * MIX_W)), const((1, ZM_W)), const((1, ZM_W)),
                  const((1, MIX_W)), const((MIX_W, MIX_W))],
        out_specs=[pl.BlockSpec((nb, CHUNK, MIX_W), lambda i, c: (i, c, 0))] * 2,
        out_shape=[jax.ShapeDtypeStruct((b, s, MIX_W), BF16)] * 2,
        scratch_shapes=([pltpu.VMEM((nb, GDN_HALO, 768), F32)] + [pltpu.VMEM((256, 128), F32)] * nb
                        + [pltpu.VMEM((256, 256), F32)] * nb),
        compiler_params=_cparams(("parallel", "arbitrary")),
        name="gla_gdn",
    )(zg, zd, zm, wlr2, blr, gn_gla, cw, eba, alog, dtb, gn_gdn, ones_bd)


def _cmp_kernel(kc_ref, vc_ref, pe_ref, w1_ref, w2_ref, ck_ref, cv_ref):
    n_sub = NSA_CMP_LEN // NSA_CMP_STRIDE
    n_chunks = kc_ref.shape[1] // NSA_CMP_STRIDE
    for which, (src_ref, out_ref) in enumerate(((kc_ref, ck_ref), (vc_ref, cv_ref))):
        parts = [jnp.zeros((n_chunks, LANE), F32) for _ in range(n_sub)]
        for i in range(NSA_CMP_STRIDE):
            slab = src_ref[0, pl.ds(i, n_chunks, stride=NSA_CMP_STRIDE), :]
            for sub in range(n_sub):
                p = sub * NSA_CMP_STRIDE + i
                parts[sub] = parts[sub] + _dot(slab + pe_ref[which, p:p + 1, :], w1_ref[which, p])
        pre = parts[0] + pltpu.roll(parts[1], n_chunks - 1, 0)
        out_ref[0] = _dot(jax.nn.gelu(pre), w2_ref[which])


def _nsa_compress(zc, pe_x, w1_bd, w2_bd):
    b, s, _ = zc.shape
    n_chunks = s // NSA_CMP_STRIDE
    return pl.pallas_call(
        _cmp_kernel,
        grid=(b,),
        in_specs=[pl.BlockSpec((1, s, LANE), lambda i: (i, 0, 0)),
                  pl.BlockSpec((1, s, LANE), lambda i: (i, 0, 1)),
                  pl.BlockSpec((2, NSA_CMP_LEN, LANE), lambda i: (0, 0, 0)),
                  pl.BlockSpec((2, NSA_CMP_LEN, LANE, LANE), lambda i: (0, 0, 0, 0)),
                  pl.BlockSpec((2, LANE, LANE), lambda i: (0, 0, 0))],
        out_specs=[pl.BlockSpec((1, n_chunks, LANE), lambda i: (i, 0, 0))] * 2,
        out_shape=[jax.ShapeDtypeStruct((b, n_chunks, LANE), F32)] * 2,
        compiler_params=_cparams(("parallel",)),
        name="nsa_compress",
    )(zc, zc, pe_x, w1_bd, w2_bd)


def _topk_rank(v):
    n, t = v.shape
    blocks = [v[r:r + SUBLANE] for r in range(0, n, SUBLANE)]
    sub = _iota((SUBLANE, t), 0)
    rank = [jnp.zeros((SUBLANE, t), jnp.int32) for _ in blocks]
    for m in range(n):
        other = v[m:m + 1, :]
        mb, ms = divmod(m, SUBLANE)
        for r, blk in enumerate(blocks):
            if r < mb:
                ahead = (other > blk).astype(jnp.int32)
            elif r > mb:
                ahead = (other >= blk).astype(jnp.int32)
            else:
                ahead = jnp.where(sub > ms, (other >= blk).astype(jnp.int32), (other > blk).astype(jnp.int32))
            rank[r] = rank[r] + ahead
    return jnp.concatenate(rank, axis=0)


def _exp_weights(s, m):
    return jnp.exp((s - m).astype(BF16))


def _normalise_aug(o_aug, in_grp):
    den = pltpu.roll(o_aug, NSA_DH, 1)
    return jnp.where(in_grp, o_aug / jnp.where(den > 0, den, 1.0), 0.0)


def _nsa_kernel(q_ref, ksv_ref, kwv_ref, ck_ref, cv_ref, zm_ref, covt_ref, gexp_ref, o_ref,
                kaug_ref, vaug_ref, vwaug_ref, s_ref, mx_ref, acc_ref):
    nb = q_ref.shape[0]
    tq = q_ref.shape[1]
    s_len = ksv_ref.shape[1]
    n_slc = s_len // NSA_SEL_LEN
    qi = pl.program_id(1)
    s0 = pl.multiple_of(qi * tq, tq)
    lane_grp = _iota((1, LANE), 1) // NSA_DH
    chains = [(bi, g) for bi in range(nb) for g in range(NSA_GROUPS)]
    n_chains = range(len(chains))

    @pl.when(qi == 0)
    def _():
        lane = _iota((s_len, LANE), 1)
        key_blk = _iota((s_len, LANE), 0) // NSA_SEL_LEN
        for bi in range(nb):
            k = ksv_ref[bi, :, 0:LANE].astype(F32)
            v = ksv_ref[bi, :, LANE:2 * LANE].astype(F32)
            vw = kwv_ref[bi, :, LANE:2 * LANE].astype(F32)
            for g in range(NSA_GROUPS):
                own = lane // NSA_DH == g
                onehot = jnp.where(lane - (1 - g) * NSA_DH == key_blk, 1.0, 0.0)
                kaug_ref[bi * NSA_GROUPS + g] = jnp.where(own, k, onehot).astype(BF16)
                vaug_ref[bi * NSA_GROUPS + g] = jnp.where(own, v, 1.0).astype(BF16)
                vwaug_ref[bi * NSA_GROUPS + g] = jnp.where(own, vw, 1.0).astype(BF16)

    t_col = s0 + _iota((tq, 1), 0)
    t_col2 = jnp.concatenate([t_col, t_col], axis=0)
    t_row = s0 + _iota((1, tq), 1)
    q = [q_ref[bi].astype(F32) * NSA_DH ** -0.5 for bi in range(nb)]
    ck_hi = [ck_ref[bi].astype(BF16) for bi in range(nb)]
    ck_lo = [(ck_ref[bi] - ck_hi[bi].astype(F32)).astype(BF16) for bi in range(nb)]
    cv = [cv_ref[bi] for bi in range(nb)]
    cmp_end = _iota((1, N_CMP_PAD), 1) * NSA_CMP_STRIDE + (NSA_CMP_LEN - 1)
    cmp_valid = cmp_end <= t_col2
    blk_t = _iota((n_slc, tq), 0)
    cur_t = t_row // NSA_SEL_LEN
    forced_t = (blk_t == 0) | (blk_t == cur_t) | (blk_t == cur_t - 1)
    future_t = blk_t > cur_t
    row_in_tile = jnp.concatenate([_iota((tq, tq), 0)] * NSA_HPG, axis=0)
    col_in_tile = _iota((NSA_HPG * tq, tq), 1)
    diag_ok = col_in_tile <= row_in_tile
    n_wt = NSA_WINDOW // tq + 1
    never = 2 * tq
    win_off = [pl.multiple_of(jnp.maximum(qi - (n_wt - 1 - w), 0) * tq, tq) for w in range(n_wt)]
    win_ok = [col_in_tile > row_in_tile + jnp.where(qi >= n_wt - 1, 0, never)]
    win_ok += [col_in_tile >= jnp.where(qi >= n_wt - 1 - w, 0, never) for w in range(1, n_wt - 1)]
    win_ok += [diag_ok]

    o_cmp = [[jnp.zeros((tq, LANE), F32) for _ in range(NSA_HPG)] for _ in range(nb)]
    o_slc = [[jnp.zeros((tq, LANE), F32) for _ in range(NSA_HPG)] for _ in range(nb)]
    o_win = [[jnp.zeros((tq, LANE), F32) for _ in range(NSA_HPG)] for _ in range(nb)]
    in_grp = [lane_grp == g for _, g in chains]
    q2 = [jnp.concatenate([jnp.where(in_grp[c], q[bi][:, j * LANE:(j + 1) * LANE], 0.0)
                           for j in range(NSA_HPG)], axis=0) for c, (bi, _) in enumerate(chains)]
    qb = [a.astype(BF16) for a in q2]
    p_c = [_masked_softmax(_dot_nt(qb[c], ck_hi[bi]) + _dot_nt(qb[c], ck_lo[bi]), cmp_valid)
           for c, (bi, _) in enumerate(chains)]
    oc = [jnp.where(in_grp[c], _dot(p_c[c], cv[bi]), 0.0) for c, (bi, _) in enumerate(chains)]
    imp_c = [a[0:tq] + a[tq:2 * tq] for a in p_c]
    imp_hi = [a.astype(BF16) for a in imp_c]
    imp_lo = [(a - h.astype(F32)).astype(BF16) for a, h in zip(imp_c, imp_hi)]
    imp_t = [(_dot_nt(covt_ref[...], h) + _dot_nt(covt_ref[...], l_))[0:n_slc]
             for h, l_ in zip(imp_hi, imp_lo)]
    imp_t = [jnp.where(forced_t, jnp.inf, jnp.where(future_t, -jnp.inf, a)) for a in imp_t]
    s_w = [[jnp.where(ok, _dot_nt(qb[c], kwv_ref[bi, pl.ds(off, tq), 0:LANE]), -jnp.inf)
            for ok, off in zip(win_ok, win_off)] for c, (bi, _) in enumerate(chains)]
    qa = []
    for c, (_, g) in enumerate(chains):
        bias_t = jnp.where(_topk_rank(imp_t[c]) < NSA_N_SEL, 0.0, NEG_BIG)
        lo = (1 - g) * NSA_DH
        rows = ([jnp.zeros((lo, tq), F32)] if lo else []) + [bias_t, jnp.zeros((LANE - lo - n_slc, tq), F32)]
        bias = jnp.concatenate(rows, axis=0).T
        qa.append(jnp.where(in_grp[c], q2[c], jnp.concatenate([bias] * NSA_HPG, axis=0)).astype(BF16))
    m_w = []
    for c in n_chains:
        m_c = s_w[c][0]
        for s_t in s_w[c][1:]:
            m_c = jnp.maximum(m_c, s_t)
        m_w.append(jnp.maximum(jnp.max(m_c, axis=-1, keepdims=True), F32_LOWEST))
    ow = [jnp.zeros((NSA_HPG * tq, LANE), F32) for _ in n_chains]
    for w, off in enumerate(win_off):
        for c in n_chains:
            ow[c] = ow[c] + jnp.dot(_exp_weights(s_w[c][w], m_w[c]), vwaug_ref[c, pl.ds(off, tq), :],
                                    preferred_element_type=F32)
    for c, (bi, _) in enumerate(chains):
        ow_c = _normalise_aug(ow[c], in_grp[c])
        for j in range(NSA_HPG):
            rows_j = slice(j * tq, (j + 1) * tq)
            o_cmp[bi][j] = o_cmp[bi][j] + oc[c][rows_j]
            o_win[bi][j] = o_win[bi][j] + ow_c[rows_j]

    half_max = lambda s_t: jnp.maximum(s_t[:, 0:LANE], s_t[:, LANE:2 * LANE])
    mx_ref[...] = jnp.full(mx_ref.shape, NEG_BIG, F32)
    groups = n_chains

    def score_tiles(tiles):
        offs = [pl.multiple_of(kt * tq, tq) for kt in tiles]
        s_new = [[_dot_nt(qa[g], kaug_ref[g, pl.ds(off, tq), :]) for g in groups] for off in offs]
        for kt, s_kt in zip(tiles, s_new):
            for g in groups:
                s_ref[g, kt] = s_kt[g]
        for g in groups:
            m_new = half_max(s_new[0][g])
            for s_kt in s_new[1:]:
                m_new = jnp.maximum(m_new, half_max(s_kt[g]))
            mx_ref[g] = jnp.maximum(mx_ref[g], m_new)

    def score_pair(p, carry):
        score_tiles([2 * p, 2 * p + 1])
        return carry

    lax.fori_loop(0, qi // 2, score_pair, 0)

    @pl.when(qi % 2 == 1)
    def _():
        score_tiles([qi - 1])

    m_s = []
    for g in groups:
        s_d = jnp.where(diag_ok, _dot_nt(qa[g], kaug_ref[g, pl.ds(s0, tq), :]), NEG_BIG)
        s_ref[g, qi] = s_d
        m_g = jnp.max(jnp.maximum(mx_ref[g], half_max(s_d)), axis=-1, keepdims=True)
        m_s.append(jnp.maximum(m_g, F32_LOWEST))
    acc_ref[...] = jnp.zeros(acc_ref.shape, F32)

    def attend_tiles(tiles):
        offs = [pl.multiple_of(kt * tq, tq) for kt in tiles]
        pv = [[jnp.dot(_exp_weights(s_ref[g, kt], m_s[g]), vaug_ref[g, pl.ds(off, tq), :],
                       preferred_element_type=F32) for g in groups] for kt, off in zip(tiles, offs)]
        for g in groups:
            upd = pv[0][g]
            for pv_kt in pv[1:]:
                upd = upd + pv_kt[g]
            acc_ref[g] += upd

    def attend_pair(p, carry):
        attend_tiles([2 * p, 2 * p + 1])
        return carry

    lax.fori_loop(0, (qi + 1) // 2, attend_pair, 0)

    @pl.when(qi % 2 == 0)
    def _():
        attend_tiles([qi])

    for c, (bi, _) in enumerate(chains):
        os_ = _normalise_aug(acc_ref[c], in_grp[c])
        for j in range(NSA_HPG):
            o_slc[bi][j] = o_slc[bi][j] + os_[j * tq:(j + 1) * tq]

    for bi in range(nb):
        gates = _split2_dot(jax.nn.sigmoid(zm_ref[bi]), gexp_ref[...])
        o = (gates[:, 0:MIX_W] * jnp.concatenate(o_cmp[bi], axis=1)
             + gates[:, MIX_W:2 * MIX_W] * jnp.concatenate(o_slc[bi], axis=1)
             + gates[:, 2 * MIX_W:3 * MIX_W] * jnp.concatenate(o_win[bi], axis=1))
        o_ref[bi] = o.astype(o_ref.dtype)


def _nsa(zn, zm, ck, cv, covt, gexp):
    b, s, _ = zn.shape
    tq = TQ_NSA
    nb = NSA_SEQS if b % NSA_SEQS == 0 else 1
    n_chains = nb * NSA_GROUPS
    return pl.pallas_call(
        _nsa_kernel,
        grid=(b // nb, s // tq),
        in_specs=[pl.BlockSpec((nb, tq, 256), lambda i, j: (i, j, 0)),
                  pl.BlockSpec((nb, s, 256), lambda i, j: (i, 0, 1)),
                  pl.BlockSpec((nb, s, 256), lambda i, j: (i, 0, 2)),
                  pl.BlockSpec((nb, N_CMP_PAD, LANE), lambda i, j: (i, 0, 0)),
                  pl.BlockSpec((nb, N_CMP_PAD, LANE), lambda i, j: (i, 0, 0)),
                  pl.BlockSpec((nb, tq, ZM_W), lambda i, j: (i, j, 0)),
                  pl.BlockSpec((N_SLC_PAD, N_CMP_PAD), lambda i, j: (0, 0)),
                  pl.BlockSpec((ZM_W, 3 * MIX_W), lambda i, j: (0, 0))],
        out_specs=pl.BlockSpec((nb, tq, MIX_W), lambda i, j: (i, j, 0)),
        out_shape=jax.ShapeDtypeStruct((b, s, MIX_W), BF16),
        scratch_shapes=[pltpu.VMEM((n_chains, s, LANE), BF16)] * 3
                       + [pltpu.VMEM((n_chains, s // tq, NSA_HPG * tq, tq), F32),
                          pltpu.VMEM((n_chains, NSA_HPG * tq, LANE), F32),
                          pltpu.VMEM((n_chains, NSA_HPG * tq, LANE), F32)],
        compiler_params=_cparams(("parallel", "arbitrary")),
        name="nsa_attn",
    )(zn, zn, zn, ck, cv, zm, covt, gexp)


def _combine_kernel(x_ref, g_ref, op_ref, oa_ref, od_ref, on_ref, wg_ref, bg_ref, wb_ref, wo_ref, o_ref):
    x = x_ref[...]
    h = _rms(x, g_ref[...]).astype(BF16)
    y = jnp.zeros(x.shape, F32)
    for i, br_ref in enumerate((op_ref, oa_ref, od_ref, on_ref)):
        gate = jax.nn.sigmoid(jnp.dot(h, wg_ref[i], preferred_element_type=F32) + bg_ref[i])
        y = y + gate * jnp.dot(br_ref[...], wb_ref[i], preferred_element_type=F32)
    o_ref[...] = x + _dot(y, wo_ref[...])


def _combine(x2, g, branches, wg, bg, wb, wo):
    t = x2.shape[0]
    tm = TM_COMB
    return pl.pallas_call(
        _combine_kernel,
        grid=(t // tm,),
        in_specs=[pl.BlockSpec((tm, D_MODEL), lambda i: (i, 0)),
                  pl.BlockSpec((1, D_MODEL), lambda i: (0, 0))]
                 + [pl.BlockSpec((tm, MIX_W), lambda i: (i, 0))] * N_BRANCH
                 + [pl.BlockSpec((N_BRANCH, D_MODEL, D_MODEL), lambda i: (0, 0, 0)),
                    pl.BlockSpec((N_BRANCH, 1, D_MODEL), lambda i: (0, 0, 0)),
                    pl.BlockSpec((N_BRANCH, MIX_W, D_MODEL), lambda i: (0, 0, 0)),
                    pl.BlockSpec((D_MODEL, D_MODEL), lambda i: (0, 0))],
        out_specs=pl.BlockSpec((tm, D_MODEL), lambda i: (i, 0)),
        out_shape=jax.ShapeDtypeStruct((t, D_MODEL), F32),
        compiler_params=_cparams(("parallel",)),
        name="combine",
    )(x2, g, *branches, wg, bg, wb, wo)


def _memkv_kernel(m_ref, g_ref, w_ref, o_ref):
    o_ref[0] = _dot(_rms(m_ref[0], g_ref[...]), w_ref[...]).astype(o_ref.dtype)


def _memkv(mem, g, w):
    b, m, _ = mem.shape
    n = 2 * X_HEADS * X_DH
    return pl.pallas_call(
        _memkv_kernel,
        grid=(b,),
        in_specs=[pl.BlockSpec((1, m, D_MODEL), lambda i: (i, 0, 0)),
                  pl.BlockSpec((1, D_MODEL), lambda i: (0, 0)),
                  pl.BlockSpec((D_MODEL, n), lambda i: (0, 0))],
        out_specs=pl.BlockSpec((1, m, n), lambda i: (i, 0, 0)),
        out_shape=jax.ShapeDtypeStruct((b, m, n), BF16),
        compiler_params=_cparams(("parallel",)),
        name="mem_kv",
    )(mem, g, w)


def _cross_kernel(x_ref, g_ref, wq_ref, kv_ref, wo_ref, o_ref):
    x = x_ref[0]
    hn = _rms(x, g_ref[...])
    q = _dot(hn, wq_ref[...])
    n_k = X_HEADS * X_DH
    outs = []
    for h in range(X_HEADS):
        k_h = kv_ref[0, :, h * X_DH:(h + 1) * X_DH]
        v_h = kv_ref[0, :, n_k + h * X_DH:n_k + (h + 1) * X_DH]
        sc = _dot_nt(q[:, h * X_DH:(h + 1) * X_DH], k_h) * X_DH ** -0.5
        e = jnp.exp(sc - jnp.max(sc, axis=-1, keepdims=True))
        p = e / jnp.sum(e, axis=-1, keepdims=True)
        outs.append(_dot(p, v_h))
    o_ref[0] = x + _dot(jnp.concatenate(outs, axis=1), wo_ref[...])


def _cross(x3, g, wq, kv, wo):
    b, s, _ = x3.shape
    tm = TM_CROSS
    m = kv.shape[1]
    n_k = X_HEADS * X_DH
    return pl.pallas_call(
        _cross_kernel,
        grid=(b, s // tm),
        in_specs=[pl.BlockSpec((1, tm, D_MODEL), lambda i, j: (i, j, 0)),
                  pl.BlockSpec((1, D_MODEL), lambda i, j: (0, 0)),
                  pl.BlockSpec((D_MODEL, n_k), lambda i, j: (0, 0)),
                  pl.BlockSpec((1, m, 2 * n_k), lambda i, j: (i, 0, 0)),
                  pl.BlockSpec((n_k, D_MODEL), lambda i, j: (0, 0))],
        out_specs=pl.BlockSpec((1, tm, D_MODEL), lambda i, j: (i, j, 0)),
        out_shape=jax.ShapeDtypeStruct((b, s, D_MODEL), F32),
        compiler_params=_cparams(("parallel", "parallel")),
        name="cross_attn",
    )(x3, g, wq, kv, wo)


def _ffn_kernel(x_ref, g_ref, wup_ref, cw_ref, cb_ref, wd_ref, gf_ref, o_ref, tail_ref, act_ref, *, final):
    @pl.when(pl.program_id(1) == 0)
    def _():
        tail_ref[...] = jnp.zeros_like(tail_ref)

    ts = x_ref.shape[1]
    x = x_ref[0]
    hn = _rms(x, g_ref[...]).astype(BF16)
    for c in range(D_FF // FF_CHUNK):
        cols = slice(c * FF_CHUNK, (c + 1) * FF_CHUNK)
        gcols = slice(D_FF + c * FF_CHUNK, D_FF + (c + 1) * FF_CHUNK)
        u = jnp.dot(hn, wup_ref[:, cols], preferred_element_type=F32)
        v = jnp.dot(hn, wup_ref[:, gcols], preferred_element_type=F32)
        ext = jnp.concatenate([tail_ref[:, cols], u], axis=0)
        tail_ref[:, cols] = u[ts - SUBLANE:, :]
        cw = cw_ref[:, cols]
        y = (cw[2:3] * u + cw[1:2] * pltpu.roll(ext, 1, 0)[SUBLANE:]
             + cw[0:1] * pltpu.roll(ext, 2, 0)[SUBLANE:] + cb_ref[:, cols])
        act_ref[:, cols] = (jax.nn.gelu(y) * v).astype(BF16)
    out = x + jnp.dot(act_ref[...], wd_ref[...], preferred_element_type=F32)
    if final:
        out = _rms(out, gf_ref[...])
    o_ref[0] = out


def _ffn(x3, g, wup, cw, cb, wd, gf, final):
    b, s, _ = x3.shape
    ts = TS_FFN
    return pl.pallas_call(
        functools.partial(_ffn_kernel, final=final),
        grid=(b, s // ts),
        in_specs=[pl.BlockSpec((1, ts, D_MODEL), lambda i, j: (i, j, 0)),
                  pl.BlockSpec((1, D_MODEL), lambda i, j: (0, 0)),
                  pl.BlockSpec((D_MODEL, 2 * D_FF), lambda i, j: (0, 0), pipeline_mode=pl.Buffered(1)),
                  pl.BlockSpec((FFN_CONV, D_FF), lambda i, j: (0, 0)),
                  pl.BlockSpec((1, D_FF), lambda i, j: (0, 0)),
                  pl.BlockSpec((D_FF, D_MODEL), lambda i, j: (0, 0), pipeline_mode=pl.Buffered(1)),
                  pl.BlockSpec((1, D_MODEL), lambda i, j: (0, 0))],
        out_specs=pl.BlockSpec((1, ts, D_MODEL), lambda i, j: (i, j, 0)),
        out_shape=jax.ShapeDtypeStruct((b, s, D_MODEL), F32),
        scratch_shapes=[pltpu.VMEM((SUBLANE, D_FF), F32), pltpu.VMEM((ts, D_FF), BF16)],
        compiler_params=_cparams(("parallel", "arbitrary")),
        name="conv_ffn",
    )(x3, g, wup, cw, cb, wd, gf)


def _inproj_columns():
    starts = np.concatenate([[0], np.cumsum(IN_SPLITS)])
    (p_in, a_q, a_k, a_v, a_r, a_lr, d_q, d_k, d_v, d_b, d_a, d_g,
     n_q, n_kc, n_vc, n_ks, n_vs, n_kw, n_vw, n_g) = [np.arange(starts[i], starts[i + 1])
                                                      for i in range(len(IN_SPLITS))]
    n_q = n_q.reshape(NSA_GROUPS, NSA_HPG, NSA_DH).transpose(1, 0, 2).reshape(-1)
    misc = np.full((ZM_W,), N_IN)
    misc[MISC_LR:MISC_LR + GLA_LOWRANK] = a_lr
    misc[MISC_B:MISC_B + GDN_HEADS] = d_b
    misc[MISC_A:MISC_A + GDN_HEADS] = d_a
    misc[MISC_G:MISC_G + 3 * NSA_HEADS] = n_g
    cols = np.concatenate([p_in, a_q, a_k, a_v, a_r, d_q, d_k, d_v, d_g,
                           n_q, n_ks, n_vs, n_kw, n_vw, n_kc, n_vc, misc])
    assert cols.shape[0] == Z_W
    return cols


def _head_expand(offset, n_heads, width):
    e = np.zeros((ZM_W, n_heads * width), np.float32)
    for h in range(n_heads):
        e[offset + h, h * width:(h + 1) * width] = 1.0
    return e


def _nsa_constants(s):
    n_cmp = s // NSA_CMP_STRIDE - NSA_CMP_LEN // NSA_CMP_STRIDE + 1
    n_slc = s // NSA_SEL_LEN
    c_start = np.arange(n_cmp) * NSA_CMP_STRIDE
    s_start = np.arange(n_slc) * NSA_SEL_LEN
    cover = np.zeros((N_CMP_PAD, N_SLC_PAD), np.float32)
    cover[:n_cmp, :n_slc] = ((c_start[:, None] <= s_start[None, :] + NSA_SEL_LEN - 1)
                             & (c_start[:, None] + NSA_CMP_LEN - 1 >= s_start[None, :]))
    gexp = np.zeros((ZM_W, 3, MIX_W), np.float32)
    for g in range(NSA_GROUPS):
        for j in range(NSA_HPG):
            slot = j * NSA_GROUPS + g
            for c in range(3):
                gexp[MISC_G + (g * NSA_HPG + j) * 3 + c, c, slot * NSA_DH:(slot + 1) * NSA_DH] = 1.0
    return jnp.asarray(cover.T, dtype=BF16), jnp.asarray(gexp.reshape(ZM_W, 3 * MIX_W), dtype=BF16)


def _block_diag(blocks):
    n, a, b = blocks.shape
    return jnp.einsum('gh,gab->gahb', jnp.eye(n, dtype=blocks.dtype), blocks).reshape(n * a, n * b)


def kernel(x, mem, g_mix, w_in, pool_w, pool_scale, gla_w_lr, gla_b_lr, gla_g_norm, gdn_conv, gdn_a_log,
           gdn_dt_bias, gdn_g_norm, nsa_pe, nsa_cmp_w1, nsa_cmp_w2, w_branch, w_gate, b_gate, w_out, g_cross,
           g_mem, w_xq, w_mem_kv, w_xo, g_ffn, w_up, ffn_conv, ffn_conv_b, w_down, g_final):
    b, s, d = x.shape
    depth = w_in.shape[0]
    t = b * s
    cols = _inproj_columns()
    covt, gexp = _nsa_constants(s)
    eba = jnp.asarray(np.concatenate([_head_expand(MISC_B, GDN_HEADS, GDN_DH),
                                      _head_expand(MISC_A, GDN_HEADS, GDN_DH)], axis=1), dtype=BF16)
    ones_bd = _block_diag(jnp.ones((GDN_HEADS, GDN_DH, GDN_DH), BF16))
    nsa_rows = np.arange(MIX_W).reshape(NSA_GROUPS, NSA_HPG, NSA_DH).transpose(1, 0, 2).reshape(-1)
    row = lambda v: v.reshape(1, -1).astype(F32)
    misc_a = lambda v: jnp.zeros((1, ZM_W), F32).at[0, MISC_A:MISC_A + GDN_HEADS].set(v)

    x2 = x.reshape(t, d)
    for l in range(depth):
        w_in_r = jnp.concatenate([w_in[l], jnp.zeros((d, 1), F32)], axis=1)[:, cols].astype(BF16)
        zp, zg, zd, zn, zc, zm = (z.reshape(b, s, -1) for z in _inproj(x2, row(g_mix[l]), w_in_r))

        o_pool = _pool(zp, _block_diag(pool_w[l]).astype(BF16), row(pool_scale[l]))

        wlr = jnp.zeros((ZM_W, GLA_HEADS * GLA_DK), F32).at[MISC_LR:MISC_LR + GLA_LOWRANK].set(gla_w_lr[l])
        wlr_hi = wlr.astype(BF16)
        wlr2 = jnp.stack([wlr_hi, (wlr - wlr_hi.astype(F32)).astype(BF16)])
        o_gla, o_gdn = _recurrent(zg, zd, zm, wlr2, row(gla_b_lr[l]), row(jnp.tile(gla_g_norm[l], GLA_HEADS)),
                                  gdn_conv[l], eba, misc_a(gdn_a_log[l]), misc_a(gdn_dt_bias[l]),
                                  row(jnp.tile(gdn_g_norm[l], GDN_HEADS)), ones_bd)

        pe_x = jnp.tile(nsa_pe[l], (1, 1, NSA_GROUPS))
        w1 = nsa_cmp_w1[l].reshape(2, NSA_CMP_LEN, NSA_DH, NSA_DH)
        eye_g = jnp.eye(NSA_GROUPS, dtype=F32)
        w1_bd = jnp.einsum('gh,kpde->kpgdhe', eye_g, w1).reshape(2, NSA_CMP_LEN, LANE, LANE).astype(BF16)
        w2_bd = jnp.einsum('gh,kde->kgdhe', eye_g, nsa_cmp_w2[l]).reshape(2, LANE, LANE).astype(BF16)
        ck, cv = _nsa_compress(zc, pe_x, w1_bd, w2_bd)
        o_nsa = _nsa(zn, zm, ck, cv, covt, gexp)

        wb = jnp.concatenate([w_branch[l, :3], w_branch[l, 3][nsa_rows][None]], axis=0).astype(BF16)
        branches = [o.reshape(t, MIX_W) for o in (o_pool, o_gla, o_gdn, o_nsa)]
        x2 = _combine(x2, row(g_mix[l]), branches, w_gate[l].astype(BF16),
                      b_gate[l].reshape(N_BRANCH, 1, d), wb, w_out[l].astype(BF16))

        kv = _memkv(mem, row(g_mem[l]), w_mem_kv[l].astype(BF16))
        x3 = _cross(x2.reshape(b, s, d), row(g_cross[l]), w_xq[l].astype(BF16), kv, w_xo[l].astype(BF16))

        x3 = _ffn(x3, row(g_ffn[l]), w_up[l].astype(BF16), ffn_conv[l], row(ffn_conv_b[l]),
                  w_down[l].astype(BF16), row(g_final), final=(l == depth - 1))
        x2 = x3.reshape(t, d)
    return x2.reshape(b, s, d)
```

```python
import functools

import numpy as np
import jax
import jax.numpy as jnp
from jax import lax
from jax.experimental import pallas as pl
from jax.experimental.pallas import tpu as pltpu

F32 = jnp.float32
BF16 = jnp.bfloat16
HIGHEST = lax.Precision.HIGHEST

D_MODEL = 1024
MIX_W = 256
POOL_WINDOWS = (2, 4, 8, 16)
POOL_GW = 64
GLA_HEADS = 4
GLA_DK = 32
GLA_DV = 64
GLA_LOWRANK = 16
GLA_GATE_NORM = 16.0
CHUNK = 64
GDN_HEADS = 4
GDN_DH = 64
GDN_CONV = 4
NSA_HEADS = 4
NSA_GROUPS = 2
NSA_HPG = 2
NSA_DH = 64
NSA_KV = 128
NSA_CMP_LEN = 32
NSA_CMP_STRIDE = 16
NSA_SEL_LEN = 64
NSA_N_SEL = 16
NSA_WINDOW = 512
X_HEADS = 4
X_DH = 128
D_FF = 2816
FFN_CONV = 3
EPS = 1e-6
N_BRANCH = 4

IN_SPLITS = (MIX_W,
             128, 128, 256, 256, GLA_LOWRANK,
             MIX_W, MIX_W, MIX_W, GDN_HEADS, GDN_HEADS, MIX_W,
             256, NSA_KV, NSA_KV, NSA_KV, NSA_KV, NSA_KV, NSA_KV, 3 * NSA_HEADS)
N_IN = sum(IN_SPLITS)

MISC_LR = 0
MISC_B = 16
MISC_A = 20
MISC_G = 24
LANE = 128
SUBLANE = 8

Z_WIDTHS = (256, 768, 1024, 768, 256, 128)
Z_DTYPES = (F32, BF16, BF16, BF16, F32, F32)
ZP_W, ZG_W, ZD_W, ZN_W, ZC_W, ZM_W = Z_WIDTHS
Z_W = sum(Z_WIDTHS)
NEG_BIG = -1e30
F32_LOWEST = float(np.finfo(np.float32).min)

TM_PROJ = 512
TM_COMB = 512
TM_CROSS = 1024
CROSS_SUB = 512
TS_FFN = 512
FF_CHUNK = 256
TQ_NSA = 256
NSA_SEQS = 2
REC_SEQS = 8
REC_LAG = 3
N_SLC_PAD = 128
N_CMP_PAD = 128

VMEM_LIMIT = 56 * 1024 * 1024


def _cparams(sem):
    return pltpu.CompilerParams(dimension_semantics=sem, vmem_limit_bytes=VMEM_LIMIT)


def _rms(x, g):
    return x * lax.rsqrt(jnp.mean(x * x, axis=-1, keepdims=True) + EPS) * g


def _dot(a, b):
    return jnp.dot(a.astype(BF16), b.astype(BF16), preferred_element_type=F32)


def _dot_nt(a, b):
    return lax.dot_general(a.astype(BF16), b.astype(BF16), (((1,), (1,)), ((), ())),
                           preferred_element_type=F32)


def _dot_tn(a, b):
    return lax.dot_general(a.astype(BF16), b.astype(BF16), (((0,), (0,)), ((), ())),
                           preferred_element_type=F32)


def _dot_hi(a, b):
    return jnp.dot(a, b, precision=HIGHEST, preferred_element_type=F32)


def _dot_nt_hi(a, b):
    return lax.dot_general(a, b, (((1,), (1,)), ((), ())), precision=HIGHEST,
                           preferred_element_type=F32)


def _split2_dot(a, b):
    hi = a.astype(BF16)
    lo = (a - hi.astype(F32)).astype(BF16)
    return jnp.dot(hi, b, preferred_element_type=F32) + jnp.dot(lo, b, preferred_element_type=F32)


def _split3_rhs_dot(a, b):
    hi = b.astype(BF16)
    r1 = b - hi.astype(F32)
    mid = r1.astype(BF16)
    lo = (r1 - mid.astype(F32)).astype(BF16)
    return ((jnp.dot(a, hi, preferred_element_type=F32) + jnp.dot(a, mid, preferred_element_type=F32))
            + jnp.dot(a, lo, preferred_element_type=F32))


def _split3_lhs_dot(a, b):
    hi = a.astype(BF16)
    r1 = a - hi.astype(F32)
    mid = r1.astype(BF16)
    lo = (r1 - mid.astype(F32)).astype(BF16)
    return ((jnp.dot(hi, b, preferred_element_type=F32) + jnp.dot(mid, b, preferred_element_type=F32))
            + jnp.dot(lo, b, preferred_element_type=F32))


def _dot3(a, w_hi, w_lo):
    a_hi = a.astype(BF16)
    a_lo = (a - a_hi.astype(F32)).astype(BF16)
    return (jnp.dot(a_hi, w_hi, preferred_element_type=F32)
            + (jnp.dot(a_lo, w_hi, preferred_element_type=F32) + jnp.dot(a_hi, w_lo, preferred_element_type=F32)))


def _iota(shape, axis):
    return lax.broadcasted_iota(jnp.int32, shape, axis)


def _block_mask(rows, cols, rb, cb):
    return (_iota((rows, cols), 0) // rb) == (_iota((rows, cols), 1) // cb)


def _shift_rows(x, k):
    t = _iota(x.shape, 0)
    return jnp.where(t >= k, pltpu.roll(x, k, 0), 0.0)


def _cumsum_rows(x):
    k = 1
    while k < x.shape[0]:
        x = x + _shift_rows(x, k)
        k *= 2
    return x


def _softplus(x):
    return jnp.maximum(x, 0.0) + jnp.log1p(jnp.exp(-jnp.abs(x)))


def _log_sigmoid(x):
    return -_softplus(-x)


def _silu(x):
    return x * jax.nn.sigmoid(x)


def _masked_softmax(s, mask):
    s = jnp.where(mask, s, -jnp.inf)
    m = jnp.maximum(jnp.max(s, axis=-1, keepdims=True), F32_LOWEST)
    e = jnp.exp(s - m)
    den = jnp.sum(e, axis=-1, keepdims=True)
    return e / jnp.where(den > 0, den, 1.0)


def _tile4(x):
    return jnp.concatenate([x, x, x, x], axis=0)


POOL_HALO = 16


def _inproj_kernel(x_ref, g_ref, w_ref, pw_ref, ps_ref, op_ref, *rest, tiles_per_seq):
    z_refs, tail_ref = rest[:-1], rest[-1]
    tile = pl.program_id(0) % tiles_per_seq

    @pl.when(tile == 0)
    def _():
        tail_ref[...] = jnp.zeros_like(tail_ref)

    h = _rms(x_ref[...], g_ref[...]).astype(BF16)
    u = jnp.dot(h, w_ref[:, 0:ZP_W], preferred_element_type=F32)
    tm = u.shape[0]
    ext = jnp.concatenate([tail_ref[...], u], axis=0)
    tail_ref[...] = u[tm - POOL_HALO:, :]
    s2 = ext + pltpu.roll(ext, 1, 0)
    s4 = s2 + pltpu.roll(s2, 2, 0)
    s8 = s4 + pltpu.roll(s4, 4, 0)
    s16 = s8 + pltpu.roll(s8, 8, 0)
    grp = _iota(u.shape, 1) // POOL_GW
    win = jnp.where(grp == 0, s2[POOL_HALO:], jnp.where(grp == 1, s4[POOL_HALO:],
                                                        jnp.where(grp == 2, s8[POOL_HALO:], s16[POOL_HALO:])))
    width = jnp.where(grp == 0, POOL_WINDOWS[0],
                      jnp.where(grp == 1, POOL_WINDOWS[1],
                                jnp.where(grp == 2, POOL_WINDOWS[2], POOL_WINDOWS[3])))
    cnt = jnp.minimum(tile * tm + _iota(u.shape, 0) + 1, width).astype(F32)
    diff = win / cnt - u
    off = ZP_W
    for ref in z_refs:
        n = ref.shape[-1]
        ref[...] = jnp.dot(h, w_ref[:, off:off + n], preferred_element_type=F32).astype(ref.dtype)
        off += n
    op_ref[...] = (_dot(diff, pw_ref[...]) * ps_ref[...]).astype(op_ref.dtype)


def _inproj(x2, g, w, pool_w_bd, pool_scale, seq_len):
    t = x2.shape[0]
    widths, dtypes = (MIX_W,) + Z_WIDTHS[1:], (BF16,) + Z_DTYPES[1:]
    return pl.pallas_call(
        functools.partial(_inproj_kernel, tiles_per_seq=seq_len // TM_PROJ),
        grid=(t // TM_PROJ,),
        in_specs=[pl.BlockSpec((TM_PROJ, D_MODEL), lambda i: (i, 0)),
                  pl.BlockSpec((1, D_MODEL), lambda i: (0, 0)),
                  pl.BlockSpec((D_MODEL, Z_W), lambda i: (0, 0)),
                  pl.BlockSpec((MIX_W, MIX_W), lambda i: (0, 0)),
                  pl.BlockSpec((1, MIX_W), lambda i: (0, 0))],
        out_specs=[pl.BlockSpec((TM_PROJ, n), lambda i: (i, 0)) for n in widths],
        out_shape=[jax.ShapeDtypeStruct((t, n), dt) for n, dt in zip(widths, dtypes)],
        scratch_shapes=[pltpu.VMEM((POOL_HALO, MIX_W), F32)],
        compiler_params=_cparams(("arbitrary",)),
        name="inproj_pool",
    )(x2, g, w, pool_w_bd, pool_scale)


def _gla_masks():
    return (_block_mask(4 * CHUNK, 128, CHUNK, GLA_DK),
            _block_mask(4 * CHUNK, 256, CHUNK, GLA_DV),
            _block_mask(256, 128, GLA_DV, GLA_DK),
            (_iota((CHUNK, 256), 1) % CHUNK) <= _iota((CHUNK, 256), 0))


def _seqs(f, *lists):
    return [f(*args) for args in zip(*lists)]


def _run_staggered(stage_gens, starts):
    live = list(range(len(stage_gens)))
    tick = 0
    while live:
        for i in list(live):
            if tick >= starts[i]:
                try:
                    next(stage_gens[i])
                except StopIteration:
                    live.remove(i)
        tick += 1


def _gla_chunk(masks, zg, zm, wlr_ref, blr_ref, gn_ref, ones_ref, st_refs, out):
    mask_k, mask_v, mask_st, causal = masks
    q = [z[:, 0:128] * GLA_DK ** -0.5 for z in zg]
    k = [z[:, 128:256] for z in zg]
    v = [z[:, 256:512] for z in zg]
    r = [z[:, 512:768] for z in zg]
    pre = _seqs(lambda a: _dot3(a, wlr_ref[0], wlr_ref[1]), zm)
    yield
    bc = _seqs(lambda a: _cumsum_rows(_log_sigmoid(a + blr_ref[...]) / GLA_GATE_NORM), pre)
    bl = [a[CHUNK - 1:CHUNK, :] for a in bc]
    q_e = _seqs(lambda a, c: a * jnp.exp(c), q, bc)
    k_e = _seqs(lambda a, c: a * jnp.exp(-c), k, bc)
    k_u = _seqs(lambda a, c, l: a * jnp.exp(l - c), k, bc, bl)
    st = [ref[...] for ref in st_refs]
    yield
    att = _seqs(lambda a, b: jnp.where(causal, _dot_nt(a, jnp.where(mask_k, _tile4(b), 0.0)), 0.0), q_e, k_e)
    inter = _seqs(_dot_nt, q_e, st)
    kv = _seqs(_dot_tn, v, k_u)
    yield
    o = _seqs(lambda a, b, c: _dot(a, _tile4(b.astype(BF16)) * ones_ref[...]) + c, att, v, inter)
    for ref, s_old, l, upd in zip(st_refs, st, bl, kv):
        ref[...] = s_old * jnp.exp(l) + jnp.where(mask_st, upd, 0.0)
    yield
    ms = _seqs(lambda a: _dot(a * a, ones_ref[...]) * (1.0 / GLA_DV), o)
    yield
    out[:] = _seqs(lambda a, m, g: a * lax.rsqrt(m + EPS) * gn_ref[...] * _silu(g), o, ms, r)


GDN_HALO = SUBLANE


def _gdn_masks():
    c4 = 4 * CHUNK
    col = _iota((CHUNK, c4), 1) % CHUNK
    row = _iota((CHUNK, c4), 0)
    return (_block_mask(c4, c4, CHUNK, CHUNK), col <= row, col < row, col == row)


def _gdn_chunk(masks, halo, zd, zm, cw_ref, eba_ref, alog_ref, dtb_ref, gn_ref, ones_ref, st_refs, out):
    mask_bd, incl, strict, diag = masks
    cw = cw_ref[...]
    bd = lambda a: _tile4(a.astype(BF16)) * ones_ref[...]

    def conv_silu(h, z):
        cur = z[:, 0:768]
        ext = jnp.concatenate([h, cur], axis=0)
        conv = (cw[3:4] * cur + cw[2:3] * pltpu.roll(ext, 1, 0)[GDN_HALO:]
                + cw[1:2] * pltpu.roll(ext, 2, 0)[GDN_HALO:] + cw[0:1] * pltpu.roll(ext, 3, 0)[GDN_HALO:])
        return _silu(conv)

    qkv = _seqs(conv_silu, halo, zd)
    yield
    v = [a[:, 512:768] for a in qkv]
    gate = [z[:, 768:1024] for z in zd]
    ssq = _seqs(lambda a: _dot(jnp.concatenate([a[:, 0:256] * a[:, 0:256], a[:, 256:512] * a[:, 256:512]],
                                               axis=0), ones_ref[...]), qkv)
    yield
    q = _seqs(lambda a, s: a[:, 0:256] * lax.rsqrt(s[0:CHUNK] + EPS) * GDN_DH ** -0.5, qkv, ssq)
    k = _seqs(lambda a, s: a[:, 256:512] * lax.rsqrt(s[CHUNK:2 * CHUNK] + EPS), qkv, ssq)
    beta = _seqs(lambda a: _dot(jax.nn.sigmoid(a), eba_ref[:, 0:MIX_W]), zm)
    yield
    gc = _seqs(lambda a: _split3_lhs_dot(_cumsum_rows(-jnp.exp(alog_ref[...]) * _softplus(a + dtb_ref[...])),
                                         eba_ref[:, MIX_W:2 * MIX_W]), zm)
    yield
    gl = [a[CHUNK - 1:CHUNK, :] for a in gc]
    ones_cc = jnp.ones((CHUNK, CHUNK), BF16)
    g_row = _seqs(lambda a: _split3_rhs_dot(ones_cc, jnp.where(diag, a, 0.0)), gc)
    yield
    decay =_seqs(lambda a, b: jnp.exp(jnp.where(incl, a - b, -jnp.inf)), gc, g_row)
    kb = _seqs(lambda a, b: a * b, k, beta)
    vb = _seqs(lambda a, b: a * b, v, beta)
    kq = _seqs(lambda a, b, c: _dot_nt(jnp.concatenate([a, b], axis=0), bd(c)), kb, q, k)
    yield
    n_mat =_seqs(lambda s, d: jnp.where(strict, s[0:CHUNK] * d, 0.0), kq, decay)
    a_qk = _seqs(lambda s, d: s[CHUNK:2 * CHUNK] * d, kq, decay)
    m = _seqs(lambda a: -a, n_mat)
    x = _seqs(lambda a: jnp.where(diag, 1.0, 0.0) + a, m)
    m = _seqs(lambda a: _dot(a, bd(a)), m)
    yield
    p = 2
    while p < CHUNK // 2:
        xm = _seqs(lambda a, b: _dot(jnp.concatenate([a, b], axis=0), bd(b)), x, m)
        x = _seqs(lambda a, r: a + r[0:CHUNK], x, xm)
        m = [r[CHUNK:2 * CHUNK] for r in xm]
        p *= 2
        yield
    x = _seqs(lambda a, b: a + _dot(a, bd(b)), x, m)
    yield
    uw = _seqs(lambda a, b, c, g: _dot(a, jnp.concatenate([bd(b), bd(c * jnp.exp(g))], axis=1)),
               x, vb, kb, gc)
    yield
    st = [ref[...] for ref in st_refs]
    ws = _seqs(lambda a, b, g, s: _dot(jnp.concatenate([a[:, 256:512], b * jnp.exp(g)], axis=0), s),
               uw, q, gc, st)
    yield
    v_new = _seqs(lambda a, r: a[:, 0:256] - r[0:CHUNK], uw, ws)
    o = _seqs(lambda r, qk, vn: r[CHUNK:2 * CHUNK] + _dot(qk, bd(vn)), ws, a_qk, v_new)
    upd = _seqs(lambda a, g, l, vn: _dot_tn(a * jnp.exp(l - g), vn), k, gc, gl, v_new)
    yield
    for ref, s_old, l, u in zip(st_refs, st, gl, upd):
        ref[...] = s_old * jnp.exp(l) + jnp.where(mask_bd, u, 0.0)
    ms = _seqs(lambda a: _dot(a * a, ones_ref[...]) * (1.0 / GDN_DH), o)
    yield
    out[:] = _seqs(lambda a, m_, g: a * lax.rsqrt(m_ + EPS) * gn_ref[...] * _silu(g), o, ms, gate)


def _rec_kernel(zg_ref, zd_ref, zm_ref, wlr_ref, blr_ref, gng_ref, cw_ref, eba_ref, alog_ref, dtb_ref,
                gnd_ref, ones_ref, og_ref, od_ref, halo_ref, *st_refs):
    nb = zg_ref.shape[0]

    @pl.when(pl.program_id(1) == 0)
    def _():
        halo_ref[...] = jnp.zeros_like(halo_ref)
        for st_ref in st_refs:
            st_ref[...] = jnp.zeros_like(st_ref)

    zm = [zm_ref[i] for i in range(nb)]
    zg = [zg_ref[i].astype(F32) for i in range(nb)]
    zd = [zd_ref[i].astype(F32) for i in range(nb)]
    halo = [halo_ref[i] for i in range(nb)]
    for i in range(nb):
        halo_ref[i] = zd[i][CHUNK - GDN_HALO:, 0:768]
    halves = [slice(0, nb // 2), slice(nb // 2, nb)] if nb > 1 else [slice(0, nb)]
    og = [None] * nb
    od_parts = [[None] * (h.stop - h.start) for h in halves]
    gdn_masks = _gdn_masks()
    gens = [_gdn_chunk(gdn_masks, halo[h], zd[h], zm[h], cw_ref, eba_ref, alog_ref, dtb_ref, gnd_ref, ones_ref,
                       st_refs[nb:][h], part) for h, part in zip(halves, od_parts)]
    gens.append(_gla_chunk(_gla_masks(), zg, zm, wlr_ref, blr_ref, gng_ref, ones_ref, st_refs[:nb], og))
    _run_staggered(gens, [REC_LAG * i for i in range(len(gens))])
    od = [o for part in od_parts for o in part]
    for i in range(nb):
        og_ref[i] = og[i].astype(og_ref.dtype)
        od_ref[i] = od[i].astype(od_ref.dtype)


def _recurrent(zg, zd, zm, wlr2, blr, gn_gla, cw, eba, alog, dtb, gn_gdn, ones_bd):
    b, s, _ = zg.shape
    nb = REC_SEQS if b % REC_SEQS == 0 else 1
    const = lambda shape: pl.BlockSpec(shape, lambda i, c: (0,) * len(shape))
    return pl.pallas_call(
        _rec_kernel,
        grid=(b // nb, s // CHUNK),
        in_specs=[pl.BlockSpec((nb, CHUNK, ZG_W), lambda i, c: (i, c, 0)),
                  pl.BlockSpec((nb, CHUNK, ZD_W), lambda i, c: (i, c, 0)),
                  pl.BlockSpec((nb, CHUNK, ZM_W), lambda i, c: (i, c, 0)),
                  const((2, ZM_W, 128)), const((1, 128)), const((1, MIX_W)),
                  const((GDN_CONV, 768)), const((ZM_W, 2 * MIX_W)), const((1, ZM_W)), const((1, ZM_W)),
                  const((1, MIX_W)), const((MIX_W, MIX_W))],
        out_specs=[pl.BlockSpec((nb, CHUNK, MIX_W), lambda i, c: (i, c, 0))] * 2,
        out_shape=[jax.ShapeDtypeStruct((b, s, MIX_W), BF16)] * 2,
        scratch_shapes=([pltpu.VMEM((nb, GDN_HALO, 768), F32)] + [pltpu.VMEM((256, 128), F32)] * nb
                        + [pltpu.VMEM((256, 256), F32)] * nb),
        compiler_params=_cparams(("parallel", "arbitrary")),
        name="gla_gdn",
    )(zg, zd, zm, wlr2, blr, gn_gla, cw, eba, alog, dtb, gn_gdn, ones_bd)


def _cmp_kernel(kc_ref, vc_ref, pe_ref, w1_ref, w2_ref, ck_ref, cv_ref):
    n_sub = NSA_CMP_LEN // NSA_CMP_STRIDE
    n_chunks = kc_ref.shape[1] // NSA_CMP_STRIDE
    for which, (src_ref, out_ref) in enumerate(((kc_ref, ck_ref), (vc_ref, cv_ref))):
        parts = [jnp.zeros((n_chunks, LANE), F32) for _ in range(n_sub)]
        for i in range(NSA_CMP_STRIDE):
            slab = src_ref[0, pl.ds(i, n_chunks, stride=NSA_CMP_STRIDE), :]
            for sub in range(n_sub):
                p = sub * NSA_CMP_STRIDE + i
                parts[sub] = parts[sub] + _dot(slab + pe_ref[which, p:p + 1, :], w1_ref[which, p])
        pre = parts[0] + pltpu.roll(parts[1], n_chunks - 1, 0)
        out_ref[0] = _dot(jax.nn.gelu(pre), w2_ref[which])


def _nsa_compress(zc, pe_x, w1_bd, w2_bd):
    b, s, _ = zc.shape
    n_chunks = s // NSA_CMP_STRIDE
    return pl.pallas_call(
        _cmp_kernel,
        grid=(b,),
        in_specs=[pl.BlockSpec((1, s, LANE), lambda i: (i, 0, 0)),
                  pl.BlockSpec((1, s, LANE), lambda i: (i, 0, 1)),
                  pl.BlockSpec((2, NSA_CMP_LEN, LANE), lambda i: (0, 0, 0)),
                  pl.BlockSpec((2, NSA_CMP_LEN, LANE, LANE), lambda i: (0, 0, 0, 0)),
                  pl.BlockSpec((2, LANE, LANE), lambda i: (0, 0, 0))],
        out_specs=[pl.BlockSpec((1, n_chunks, LANE), lambda i: (i, 0, 0))] * 2,
        out_shape=[jax.ShapeDtypeStruct((b, n_chunks, LANE), F32)] * 2,
        compiler_params=_cparams(("parallel",)),
        name="nsa_compress",
    )(zc, zc, pe_x, w1_bd, w2_bd)


def _topk_rank(v):
    n, t = v.shape
    blocks = [v[r:r + SUBLANE] for r in range(0, n, SUBLANE)]
    sub = _iota((SUBLANE, t), 0)
    rank = [jnp.zeros((SUBLANE, t), jnp.int32) for _ in blocks]
    for m in range(n):
        other = v[m:m + 1, :]
        mb, ms = divmod(m, SUBLANE)
        for r, blk in enumerate(blocks):
            if r < mb:
                ahead = (other > blk).astype(jnp.int32)
            elif r > mb:
                ahead = (other >= blk).astype(jnp.int32)
            else:
                ahead = jnp.where(sub > ms, (other >= blk).astype(jnp.int32), (other > blk).astype(jnp.int32))
            rank[r] = rank[r] + ahead
    return jnp.concatenate(rank, axis=0)


def _exp_weights(s, m):
    return jnp.exp((s - m).astype(BF16))


def _normalise_aug(o_aug, in_grp):
    den = pltpu.roll(o_aug, NSA_DH, 1)
    return jnp.where(in_grp, o_aug / jnp.where(den > 0, den, 1.0), 0.0)


def _nsa_kernel(q_ref, ksv_ref, kwv_ref, ck_ref, cv_ref, zm_ref, covt_ref, gexp_ref, o_ref,
                kaug_ref, vaug_ref, vwaug_ref, s_ref, mx_ref, acc_ref):
    nb = q_ref.shape[0]
    tq = q_ref.shape[1]
    s_len = ksv_ref.shape[1]
    n_slc = s_len // NSA_SEL_LEN
    qi = pl.program_id(1)
    s0 = pl.multiple_of(qi * tq, tq)
    lane_grp = _iota((1, LANE), 1) // NSA_DH
    chains = [(bi, g) for bi in range(nb) for g in range(NSA_GROUPS)]
    n_chains = range(len(chains))

    @pl.when(qi == 0)
    def _():
        lane = _iota((s_len, LANE), 1)
        key_blk = _iota((s_len, LANE), 0) // NSA_SEL_LEN
        for bi in range(nb):
            k = ksv_ref[bi, :, 0:LANE].astype(F32)
            v = ksv_ref[bi, :, LANE:2 * LANE].astype(F32)
            vw = kwv_ref[bi, :, LANE:2 * LANE].astype(F32)
            for g in range(NSA_GROUPS):
                own = lane // NSA_DH == g
                onehot = jnp.where(lane - (1 - g) * NSA_DH == key_blk, 1.0, 0.0)
                kaug_ref[bi * NSA_GROUPS + g] = jnp.where(own, k, onehot).astype(BF16)
                vaug_ref[bi * NSA_GROUPS + g] = jnp.where(own, v, 1.0).astype(BF16)
                vwaug_ref[bi * NSA_GROUPS + g] = jnp.where(own, vw, 1.0).astype(BF16)

    t_col = s0 + _iota((tq, 1), 0)
    t_col2 = jnp.concatenate([t_col, t_col], axis=0)
    t_row = s0 + _iota((1, tq), 1)
    q = [q_ref[bi].astype(F32) * NSA_DH ** -0.5 for bi in range(nb)]
    ck_hi = [ck_ref[bi].astype(BF16) for bi in range(nb)]
    ck_lo = [(ck_ref[bi] - ck_hi[bi].astype(F32)).astype(BF16) for bi in range(nb)]
    cv = [cv_ref[bi] for bi in range(nb)]
    cmp_end = _iota((1, N_CMP_PAD), 1) * NSA_CMP_STRIDE + (NSA_CMP_LEN - 1)
    cmp_valid = cmp_end <= t_col2
    blk_t = _iota((n_slc, tq), 0)
    cur_t = t_row // NSA_SEL_LEN
    forced_t = (blk_t == 0) | (blk_t == cur_t) | (blk_t == cur_t - 1)
    future_t = blk_t > cur_t
    row_in_tile = jnp.concatenate([_iota((tq, tq), 0)] * NSA_HPG, axis=0)
    col_in_tile = _iota((NSA_HPG * tq, tq), 1)
    diag_ok = col_in_tile <= row_in_tile
    n_wt = NSA_WINDOW // tq + 1
    never = 2 * tq
    win_off = [pl.multiple_of(jnp.maximum(qi - (n_wt - 1 - w), 0) * tq, tq) for w in range(n_wt)]
    win_ok = [col_in_tile > row_in_tile + jnp.where(qi >= n_wt - 1, 0, never)]
    win_ok += [col_in_tile >= jnp.where(qi >= n_wt - 1 - w, 0, never) for w in range(1, n_wt - 1)]
    win_ok += [diag_ok]

    o_cmp = [[jnp.zeros((tq, LANE), F32) for _ in range(NSA_HPG)] for _ in range(nb)]
    o_slc = [[jnp.zeros((tq, LANE), F32) for _ in range(NSA_HPG)] for _ in range(nb)]
    o_win = [[jnp.zeros((tq, LANE), F32) for _ in range(NSA_HPG)] for _ in range(nb)]
    in_grp = [lane_grp == g for _, g in chains]
    q2 = [jnp.concatenate([jnp.where(in_grp[c], q[bi][:, j * LANE:(j + 1) * LANE], 0.0)
                           for j in range(NSA_HPG)], axis=0) for c, (bi, _) in enumerate(chains)]
    qb = [a.astype(BF16) for a in q2]
    p_c = [_masked_softmax(_dot_nt(qb[c], ck_hi[bi]) + _dot_nt(qb[c], ck_lo[bi]), cmp_valid)
           for c, (bi, _) in enumerate(chains)]
    oc = [jnp.where(in_grp[c], _dot(p_c[c], cv[bi]), 0.0) for c, (bi, _) in enumerate(chains)]
    imp_c = [a[0:tq] + a[tq:2 * tq] for a in p_c]
    imp_hi = [a.astype(BF16) for a in imp_c]
    imp_lo = [(a - h.astype(F32)).astype(BF16) for a, h in zip(imp_c, imp_hi)]
    imp_t = [(_dot_nt(covt_ref[...], h) + _dot_nt(covt_ref[...], l_))[0:n_slc]
             for h, l_ in zip(imp_hi, imp_lo)]
    imp_t = [jnp.where(forced_t, jnp.inf, jnp.where(future_t, -jnp.inf, a)) for a in imp_t]
    s_w = [[jnp.where(ok, _dot_nt(qb[c], kwv_ref[bi, pl.ds(off, tq), 0:LANE]), -jnp.inf)
            for ok, off in zip(win_ok, win_off)] for c, (bi, _) in enumerate(chains)]
    qa = []
    for c, (_, g) in enumerate(chains):
        bias_t = jnp.where(_topk_rank(imp_t[c]) < NSA_N_SEL, 0.0, NEG_BIG)
        lo = (1 - g) * NSA_DH
        rows = ([jnp.zeros((lo, tq), F32)] if lo else []) + [bias_t, jnp.zeros((LANE - lo - n_slc, tq), F32)]
        bias = jnp.concatenate(rows, axis=0).T
        qa.append(jnp.where(in_grp[c], q2[c], jnp.concatenate([bias] * NSA_HPG, axis=0)).astype(BF16))
    m_w = []
    for c in n_chains:
        m_c = s_w[c][0]
        for s_t in s_w[c][1:]:
            m_c = jnp.maximum(m_c, s_t)
        m_w.append(jnp.maximum(jnp.max(m_c, axis=-1, keepdims=True), F32_LOWEST))
    ow = [jnp.zeros((NSA_HPG * tq, LANE), F32) for _ in n_chains]
    for w, off in enumerate(win_off):
        for c in n_chains:
            ow[c] = ow[c] + jnp.dot(_exp_weights(s_w[c][w], m_w[c]), vwaug_ref[c, pl.ds(off, tq), :],
                                    preferred_element_type=F32)
    for c, (bi, _) in enumerate(chains):
        ow_c = _normalise_aug(ow[c], in_grp[c])
        for j in range(NSA_HPG):
            rows_j = slice(j * tq, (j + 1) * tq)
            o_cmp[bi][j] = o_cmp[bi][j] + oc[c][rows_j]
            o_win[bi][j] = o_win[bi][j] + ow_c[rows_j]

    half_max = lambda s_t: jnp.maximum(s_t[:, 0:LANE], s_t[:, LANE:2 * LANE])
    mx_ref[...] = jnp.full(mx_ref.shape, NEG_BIG, F32)
    groups = n_chains

    def score_tiles(tiles):
        offs = [pl.multiple_of(kt * tq, tq) for kt in tiles]
        s_new = [[_dot_nt(qa[g], kaug_ref[g, pl.ds(off, tq), :]) for g in groups] for off in offs]
        for kt, s_kt in zip(tiles, s_new):
            for g in groups:
                s_ref[g, kt] = s_kt[g]
        for g in groups:
            m_new = half_max(s_new[0][g])
            for s_kt in s_new[1:]:
                m_new = jnp.maximum(m_new, half_max(s_kt[g]))
            mx_ref[g] = jnp.maximum(mx_ref[g], m_new)

    def score_pair(p, carry):
        score_tiles([2 * p, 2 * p + 1])
        return carry

    lax.fori_loop(0, qi // 2, score_pair, 0)

    @pl.when(qi % 2 == 1)
    def _():
        score_tiles([qi - 1])

    m_s = []
    for g in groups:
        s_d = jnp.where(diag_ok, _dot_nt(qa[g], kaug_ref[g, pl.ds(s0, tq), :]), NEG_BIG)
        s_ref[g, qi] = s_d
        m_g = jnp.max(jnp.maximum(mx_ref[g], half_max(s_d)), axis=-1, keepdims=True)
        m_s.append(jnp.maximum(m_g, F32_LOWEST))
    acc_ref[...] = jnp.zeros(acc_ref.shape, F32)

    def attend_tiles(tiles):
        offs = [pl.multiple_of(kt * tq, tq) for kt in tiles]
        pv = [[jnp.dot(_exp_weights(s_ref[g, kt], m_s[g]), vaug_ref[g, pl.ds(off, tq), :],
                       preferred_element_type=F32) for g in groups] for kt, off in zip(tiles, offs)]
        for g in groups:
            upd = pv[0][g]
            for pv_kt in pv[1:]:
                upd = upd + pv_kt[g]
            acc_ref[g] += upd

    def attend_pair(p, carry):
        attend_tiles([2 * p, 2 * p + 1])
        return carry

    lax.fori_loop(0, (qi + 1) // 2, attend_pair, 0)

    @pl.when(qi % 2 == 0)
    def _():
        attend_tiles([qi])

    for c, (bi, _) in enumerate(chains):
        os_ = _normalise_aug(acc_ref[c], in_grp[c])
        for j in range(NSA_HPG):
            o_slc[bi][j] = o_slc[bi][j] + os_[j * tq:(j + 1) * tq]

    for bi in range(nb):
        gates = _split2_dot(jax.nn.sigmoid(zm_ref[bi]), gexp_ref[...])
        o = (gates[:, 0:MIX_W] * jnp.concatenate(o_cmp[bi], axis=1)
             + gates[:, MIX_W:2 * MIX_W] * jnp.concatenate(o_slc[bi], axis=1)
             + gates[:, 2 * MIX_W:3 * MIX_W] * jnp.concatenate(o_win[bi], axis=1))
        o_ref[bi] = o.astype(o_ref.dtype)


def _nsa(zn, zm, ck, cv, covt, gexp):
    b, s, _ = zn.shape
    tq = TQ_NSA
    nb = NSA_SEQS if b % NSA_SEQS == 0 else 1
    n_chains = nb * NSA_GROUPS
    return pl.pallas_call(
        _nsa_kernel,
        grid=(b // nb, s // tq),
        in_specs=[pl.BlockSpec((nb, tq, 256), lambda i, j: (i, j, 0)),
                  pl.BlockSpec((nb, s, 256), lambda i, j: (i, 0, 1)),
                  pl.BlockSpec((nb, s, 256), lambda i, j: (i, 0, 2)),
                  pl.BlockSpec((nb, N_CMP_PAD, LANE), lambda i, j: (i, 0, 0)),
                  pl.BlockSpec((nb, N_CMP_PAD, LANE), lambda i, j: (i, 0, 0)),
                  pl.BlockSpec((nb, tq, ZM_W), lambda i, j: (i, j, 0)),
                  pl.BlockSpec((N_SLC_PAD, N_CMP_PAD), lambda i, j: (0, 0)),
                  pl.BlockSpec((ZM_W, 3 * MIX_W), lambda i, j: (0, 0))],
        out_specs=pl.BlockSpec((nb, tq, MIX_W), lambda i, j: (i, j, 0)),
        out_shape=jax.ShapeDtypeStruct((b, s, MIX_W), BF16),
        scratch_shapes=[pltpu.VMEM((n_chains, s, LANE), BF16)] * 3
                       + [pltpu.VMEM((n_chains, s // tq, NSA_HPG * tq, tq), F32),
                          pltpu.VMEM((n_chains, NSA_HPG * tq, LANE), F32),
                          pltpu.VMEM((n_chains, NSA_HPG * tq, LANE), F32)],
        compiler_params=_cparams(("parallel", "arbitrary")),
        name="nsa_attn",
    )(zn, zn, zn, ck, cv, zm, covt, gexp)


def _combine_kernel(x_ref, g_ref, op_ref, oa_ref, od_ref, on_ref, wg_ref, bg_ref, wb_ref, wo_ref, o_ref):
    x = x_ref[...]
    h = _rms(x, g_ref[...]).astype(BF16)
    y = jnp.zeros(x.shape, F32)
    for i, br_ref in enumerate((op_ref, oa_ref, od_ref, on_ref)):
        gate = jax.nn.sigmoid(jnp.dot(h, wg_ref[i], preferred_element_type=F32) + bg_ref[i])
        y = y + gate * jnp.dot(br_ref[...], wb_ref[i], preferred_element_type=F32)
    o_ref[...] = x + _dot(y, wo_ref[...])


def _combine(x2, g, branches, wg, bg, wb, wo):
    t = x2.shape[0]
    tm = TM_COMB
    return pl.pallas_call(
        _combine_kernel,
        grid=(t // tm,),
        in_specs=[pl.BlockSpec((tm, D_MODEL), lambda i: (i, 0)),
                  pl.BlockSpec((1, D_MODEL), lambda i: (0, 0))]
                 + [pl.BlockSpec((tm, MIX_W), lambda i: (i, 0))] * N_BRANCH
                 + [pl.BlockSpec((N_BRANCH, D_MODEL, D_MODEL), lambda i: (0, 0, 0)),
                    pl.BlockSpec((N_BRANCH, 1, D_MODEL), lambda i: (0, 0, 0)),
                    pl.BlockSpec((N_BRANCH, MIX_W, D_MODEL), lambda i: (0, 0, 0)),
                    pl.BlockSpec((D_MODEL, D_MODEL), lambda i: (0, 0))],
        out_specs=pl.BlockSpec((tm, D_MODEL), lambda i: (i, 0)),
        out_shape=jax.ShapeDtypeStruct((t, D_MODEL), F32),
        compiler_params=_cparams(("parallel",)),
        name="combine",
    )(x2, g, *branches, wg, bg, wb, wo)


def _memkv_kernel(m_ref, g_ref, w_ref, o_ref):
    o_ref[0] = _dot(_rms(m_ref[0], g_ref[...]), w_ref[...]).astype(o_ref.dtype)


def _memkv(mem, g, w):
    b, m, _ = mem.shape
    n = 2 * X_HEADS * X_DH
    return pl.pallas_call(
        _memkv_kernel,
        grid=(b,),
        in_specs=[pl.BlockSpec((1, m, D_MODEL), lambda i: (i, 0, 0)),
                  pl.BlockSpec((1, D_MODEL), lambda i: (0, 0)),
                  pl.BlockSpec((D_MODEL, n), lambda i: (0, 0))],
        out_specs=pl.BlockSpec((1, m, n), lambda i: (i, 0, 0)),
        out_shape=jax.ShapeDtypeStruct((b, m, n), BF16),
        compiler_params=_cparams(("parallel",)),
        name="mem_kv",
    )(mem, g, w)


def _cross_kernel(x_ref, g_ref, wq_ref, kv_ref, wo_ref, o_ref):
    n_k = X_HEADS * X_DH
    subs = [slice(i * CROSS_SUB, (i + 1) * CROSS_SUB) for i in range(x_ref.shape[1] // CROSS_SUB)]
    x = [x_ref[0, r, :] for r in subs]
    q = [_dot(_rms(a, g_ref[...]), wq_ref[...]) * X_DH ** -0.5 for a in x]
    pairs = [(i, h) for h in range(X_HEADS) for i in range(len(subs))]
    head = lambda h: slice(h * X_DH, (h + 1) * X_DH)
    sc = [_dot_nt(q[i][:, head(h)], kv_ref[0, :, head(h)]) for i, h in pairs]
    e = [jnp.exp(s - jnp.max(s, axis=-1, keepdims=True)) for s in sc]
    o = [_dot(e_ih, kv_ref[0, :, n_k + h * X_DH:n_k + (h + 1) * X_DH]) / jnp.sum(e_ih, axis=-1, keepdims=True)
         for e_ih, (i, h) in zip(e, pairs)]
    for i, r in enumerate(subs):
        o_i = jnp.concatenate([o[pairs.index((i, h))] for h in range(X_HEADS)], axis=1)
        o_ref[0, r, :] = x[i] + _dot(o_i, wo_ref[...])


def _cross(x3, g, wq, kv, wo):
    b, s, _ = x3.shape
    tm = TM_CROSS
    m = kv.shape[1]
    n_k = X_HEADS * X_DH
    return pl.pallas_call(
        _cross_kernel,
        grid=(b, s // tm),
        in_specs=[pl.BlockSpec((1, tm, D_MODEL), lambda i, j: (i, j, 0)),
                  pl.BlockSpec((1, D_MODEL), lambda i, j: (0, 0)),
                  pl.BlockSpec((D_MODEL, n_k), lambda i, j: (0, 0)),
                  pl.BlockSpec((1, m, 2 * n_k), lambda i, j: (i, 0, 0)),
                  pl.BlockSpec((n_k, D_MODEL), lambda i, j: (0, 0))],
        out_specs=pl.BlockSpec((1, tm, D_MODEL), lambda i, j: (i, j, 0)),
        out_shape=jax.ShapeDtypeStruct((b, s, D_MODEL), F32),
        compiler_params=_cparams(("parallel", "parallel")),
        name="cross_attn",
    )(x3, g, wq, kv, wo)


def _ffn_kernel(x_ref, g_ref, wup_ref, cw_ref, cb_ref, wd_ref, gf_ref, o_ref, tail_ref, act_ref, *, final):
    @pl.when(pl.program_id(1) == 0)
    def _():
        tail_ref[...] = jnp.zeros_like(tail_ref)

    ts = x_ref.shape[1]
    x = x_ref[0]
    hn = _rms(x, g_ref[...]).astype(BF16)
    for c in range(D_FF // FF_CHUNK):
        cols = slice(c * FF_CHUNK, (c + 1) * FF_CHUNK)
        gcols = slice(D_FF + c * FF_CHUNK, D_FF + (c + 1) * FF_CHUNK)
        u = jnp.dot(hn, wup_ref[:, cols], preferred_element_type=F32)
        v = jnp.dot(hn, wup_ref[:, gcols], preferred_element_type=F32)
        ext = jnp.concatenate([tail_ref[:, cols], u], axis=0)
        tail_ref[:, cols] = u[ts - SUBLANE:, :]
        cw = cw_ref[:, cols]
        y = (cw[2:3] * u + cw[1:2] * pltpu.roll(ext, 1, 0)[SUBLANE:]
             + cw[0:1] * pltpu.roll(ext, 2, 0)[SUBLANE:] + cb_ref[:, cols])
        act_ref[:, cols] = (jax.nn.gelu(y) * v).astype(BF16)
    out = x + jnp.dot(act_ref[...], wd_ref[...], preferred_element_type=F32)
    if final:
        out = _rms(out, gf_ref[...])
    o_ref[0] = out


def _ffn(x3, g, wup, cw, cb, wd, gf, final):
    b, s, _ = x3.shape
    ts = TS_FFN
    return pl.pallas_call(
        functools.partial(_ffn_kernel, final=final),
        grid=(b, s // ts),
        in_specs=[pl.BlockSpec((1, ts, D_MODEL), lambda i, j: (i, j, 0)),
                  pl.BlockSpec((1, D_MODEL), lambda i, j: (0, 0)),
                  pl.BlockSpec((D_MODEL, 2 * D_FF), lambda i, j: (0, 0), pipeline_mode=pl.Buffered(1)),
                  pl.BlockSpec((FFN_CONV, D_FF), lambda i, j: (0, 0)),
                  pl.BlockSpec((1, D_FF), lambda i, j: (0, 0)),
                  pl.BlockSpec((D_FF, D_MODEL), lambda i, j: (0, 0), pipeline_mode=pl.Buffered(1)),
                  pl.BlockSpec((1, D_MODEL), lambda i, j: (0, 0))],
        out_specs=pl.BlockSpec((1, ts, D_MODEL), lambda i, j: (i, j, 0)),
        out_shape=jax.ShapeDtypeStruct((b, s, D_MODEL), F32),
        scratch_shapes=[pltpu.VMEM((SUBLANE, D_FF), F32), pltpu.VMEM((ts, D_FF), BF16)],
        compiler_params=_cparams(("parallel", "arbitrary")),
        name="conv_ffn",
    )(x3, g, wup, cw, cb, wd, gf)


def _inproj_columns():
    starts = np.concatenate([[0], np.cumsum(IN_SPLITS)])
    (p_in, a_q, a_k, a_v, a_r, a_lr, d_q, d_k, d_v, d_b, d_a, d_g,
     n_q, n_kc, n_vc, n_ks, n_vs, n_kw, n_vw, n_g) = [np.arange(starts[i], starts[i + 1])
                                                      for i in range(len(IN_SPLITS))]
    n_q = n_q.reshape(NSA_GROUPS, NSA_HPG, NSA_DH).transpose(1, 0, 2).reshape(-1)
    misc = np.full((ZM_W,), N_IN)
    misc[MISC_LR:MISC_LR + GLA_LOWRANK] = a_lr
    misc[MISC_B:MISC_B + GDN_HEADS] = d_b
    misc[MISC_A:MISC_A + GDN_HEADS] = d_a
    misc[MISC_G:MISC_G + 3 * NSA_HEADS] = n_g
    cols = np.concatenate([p_in, a_q, a_k, a_v, a_r, d_q, d_k, d_v, d_g,
                           n_q, n_ks, n_vs, n_kw, n_vw, n_kc, n_vc, misc])
    assert cols.shape[0] == Z_W
    return cols


def _head_expand(offset, n_heads, width):
    e = np.zeros((ZM_W, n_heads * width), np.float32)
    for h in range(n_heads):
        e[offset + h, h * width:(h + 1) * width] = 1.0
    return e


def _nsa_constants(s):
    n_cmp = s // NSA_CMP_STRIDE - NSA_CMP_LEN // NSA_CMP_STRIDE + 1
    n_slc = s // NSA_SEL_LEN
    c_start = np.arange(n_cmp) * NSA_CMP_STRIDE
    s_start = np.arange(n_slc) * NSA_SEL_LEN
    cover = np.zeros((N_CMP_PAD, N_SLC_PAD), np.float32)
    cover[:n_cmp, :n_slc] = ((c_start[:, None] <= s_start[None, :] + NSA_SEL_LEN - 1)
                             & (c_start[:, None] + NSA_CMP_LEN - 1 >= s_start[None, :]))
    gexp = np.zeros((ZM_W, 3, MIX_W), np.float32)
    for g in range(NSA_GROUPS):
        for j in range(NSA_HPG):
            slot = j * NSA_GROUPS + g
            for c in range(3):
                gexp[MISC_G + (g * NSA_HPG + j) * 3 + c, c, slot * NSA_DH:(slot + 1) * NSA_DH] = 1.0
    return jnp.asarray(cover.T, dtype=BF16), jnp.asarray(gexp.reshape(ZM_W, 3 * MIX_W), dtype=BF16)


def _block_diag(blocks):
    n, a, b = blocks.shape
    return jnp.einsum('gh,gab->gahb', jnp.eye(n, dtype=blocks.dtype), blocks).reshape(n * a, n * b)


def kernel(x, mem, g_mix, w_in, pool_w, pool_scale, gla_w_lr, gla_b_lr, gla_g_norm, gdn_conv, gdn_a_log,
           gdn_dt_bias, gdn_g_norm, nsa_pe, nsa_cmp_w1, nsa_cmp_w2, w_branch, w_gate, b_gate, w_out, g_cross,
           g_mem, w_xq, w_mem_kv, w_xo, g_ffn, w_up, ffn_conv, ffn_conv_b, w_down, g_final):
    b, s, d = x.shape
    depth = w_in.shape[0]
    t = b * s
    cols = _inproj_columns()
    covt, gexp = _nsa_constants(s)
    eba = jnp.asarray(np.concatenate([_head_expand(MISC_B, GDN_HEADS, GDN_DH),
                                      _head_expand(MISC_A, GDN_HEADS, GDN_DH)], axis=1), dtype=BF16)
    ones_bd = _block_diag(jnp.ones((GDN_HEADS, GDN_DH, GDN_DH), BF16))
    nsa_rows = np.arange(MIX_W).reshape(NSA_GROUPS, NSA_HPG, NSA_DH).transpose(1, 0, 2).reshape(-1)
    row = lambda v: v.reshape(1, -1).astype(F32)
    misc_a = lambda v: jnp.zeros((1, ZM_W), F32).at[0, MISC_A:MISC_A + GDN_HEADS].set(v)

    x2 = x.reshape(t, d)
    for l in range(depth):
        w_in_r = jnp.concatenate([w_in[l], jnp.zeros((d, 1), F32)], axis=1)[:, cols].astype(BF16)
        o_pool, zg, zd, zn, zc, zm = (z.reshape(b, s, -1) for z in _inproj(
            x2, row(g_mix[l]), w_in_r, _block_diag(pool_w[l]).astype(BF16), row(pool_scale[l]), s))

        wlr = jnp.zeros((ZM_W, GLA_HEADS * GLA_DK), F32).at[MISC_LR:MISC_LR + GLA_LOWRANK].set(gla_w_lr[l])
        wlr_hi = wlr.astype(BF16)
        wlr2 = jnp.stack([wlr_hi, (wlr - wlr_hi.astype(F32)).astype(BF16)])
        o_gla, o_gdn = _recurrent(zg, zd, zm, wlr2, row(gla_b_lr[l]), row(jnp.tile(gla_g_norm[l], GLA_HEADS)),
                                  gdn_conv[l], eba, misc_a(gdn_a_log[l]), misc_a(gdn_dt_bias[l]),
                                  row(jnp.tile(gdn_g_norm[l], GDN_HEADS)), ones_bd)

        pe_x = jnp.tile(nsa_pe[l], (1, 1, NSA_GROUPS))
        w1 = nsa_cmp_w1[l].reshape(2, NSA_CMP_LEN, NSA_DH, NSA_DH)
        eye_g = jnp.eye(NSA_GROUPS, dtype=F32)
        w1_bd = jnp.einsum('gh,kpde->kpgdhe', eye_g, w1).reshape(2, NSA_CMP_LEN, LANE, LANE).astype(BF16)
        w2_bd = jnp.einsum('gh,kde->kgdhe', eye_g, nsa_cmp_w2[l]).reshape(2, LANE, LANE).astype(BF16)
        ck, cv = _nsa_compress(zc, pe_x, w1_bd, w2_bd)
        o_nsa = _nsa(zn, zm, ck, cv, covt, gexp)

        wb = jnp.concatenate([w_branch[l, :3], w_branch[l, 3][nsa_rows][None]], axis=0).astype(BF16)
        branches = [o.reshape(t, MIX_W) for o in (o_pool, o_gla, o_gdn, o_nsa)]
        x2 = _combine(x2, row(g_mix[l]), branches, w_gate[l].astype(BF16),
                      b_gate[l].reshape(N_BRANCH, 1, d), wb, w_out[l].astype(BF16))

        kv = _memkv(mem, row(g_mem[l]), w_mem_kv[l].astype(BF16))
        x3 = _cross(x2.reshape(b, s, d), row(g_cross[l]), w_xq[l].astype(BF16), kv, w_xo[l].astype(BF16))

        x3 = _ffn(x3, row(g_ffn[l]), w_up[l].astype(BF16), ffn_conv[l], row(ffn_conv_b[l]),
                  w_down[l].astype(BF16), row(g_final), final=(l == depth - 1))
        x2 = x3.reshape(t, d)
    return x2.reshape(b, s, d)
```

```python
import functools

import numpy as np
import jax
import jax.numpy as jnp
from jax import lax
from jax.experimental import pallas as pl
from jax.experimental.pallas import tpu as pltpu

F32 = jnp.float32
BF16 = jnp.bfloat16
HIGHEST = lax.Precision.HIGHEST

D_MODEL = 1024
MIX_W = 256
POOL_WINDOWS = (2, 4, 8, 16)
POOL_GW = 64
GLA_HEADS = 4
GLA_DK = 32
GLA_DV = 64
GLA_LOWRANK = 16
GLA_GATE_NORM = 16.0
CHUNK = 64
GDN_HEADS = 4
GDN_DH = 64
GDN_CONV = 4
NSA_HEADS = 4
NSA_GROUPS = 2
NSA_HPG = 2
NSA_DH = 64
NSA_KV = 128
NSA_CMP_LEN = 32
NSA_CMP_STRIDE = 16
NSA_SEL_LEN = 64
NSA_N_SEL = 16
NSA_WINDOW = 512
X_HEADS = 4
X_DH = 128
D_FF = 2816
FFN_CONV = 3
EPS = 1e-6
N_BRANCH = 4

IN_SPLITS = (MIX_W,
             128, 128, 256, 256, GLA_LOWRANK,
             MIX_W, MIX_W, MIX_W, GDN_HEADS, GDN_HEADS, MIX_W,
             256, NSA_KV, NSA_KV, NSA_KV, NSA_KV, NSA_KV, NSA_KV, 3 * NSA_HEADS)
N_IN = sum(IN_SPLITS)

MISC_LR = 0
MISC_B = 16
MISC_A = 20
MISC_G = 24
LANE = 128
SUBLANE = 8

Z_WIDTHS = (256, 768, 1024, 768, 256, 128)
Z_DTYPES = (F32, BF16, BF16, BF16, F32, F32)
ZP_W, ZG_W, ZD_W, ZN_W, ZC_W, ZM_W = Z_WIDTHS
Z_W = sum(Z_WIDTHS)
NEG_BIG = -1e30
F32_LOWEST = float(np.finfo(np.float32).min)

TM_PROJ = 512
TM_COMB = 512
TM_CROSS = 1024
CROSS_SUB = 512
TS_FFN = 512
FF_CHUNK = 256
TQ_NSA = 256
NSA_SEQS = 2
REC_SEQS = 8
REC_LAG = 3
GLA_START = 6
N_SLC_PAD = 128
N_CMP_PAD = 128

VMEM_LIMIT = 56 * 1024 * 1024


def _cparams(sem):
    return pltpu.CompilerParams(dimension_semantics=sem, vmem_limit_bytes=VMEM_LIMIT)


def _rms(x, g):
    return x * lax.rsqrt(jnp.mean(x * x, axis=-1, keepdims=True) + EPS) * g


def _dot(a, b):
    return jnp.dot(a.astype(BF16), b.astype(BF16), preferred_element_type=F32)


def _dot_nt(a, b):
    return lax.dot_general(a.astype(BF16), b.astype(BF16), (((1,), (1,)), ((), ())),
                           preferred_element_type=F32)


def _dot_tn(a, b):
    return lax.dot_general(a.astype(BF16), b.astype(BF16), (((0,), (0,)), ((), ())),
                           preferred_element_type=F32)


def _dot_hi(a, b):
    return jnp.dot(a, b, precision=HIGHEST, preferred_element_type=F32)


def _dot_nt_hi(a, b):
    return lax.dot_general(a, b, (((1,), (1,)), ((), ())), precision=HIGHEST,
                           preferred_element_type=F32)


def _split2_dot(a, b):
    hi = a.astype(BF16)
    lo = (a - hi.astype(F32)).astype(BF16)
    return jnp.dot(hi, b, preferred_element_type=F32) + jnp.dot(lo, b, preferred_element_type=F32)


def _split3_rhs_dot(a, b):
    hi = b.astype(BF16)
    r1 = b - hi.astype(F32)
    mid = r1.astype(BF16)
    lo = (r1 - mid.astype(F32)).astype(BF16)
    return ((jnp.dot(a, hi, preferred_element_type=F32) + jnp.dot(a, mid, preferred_element_type=F32))
            + jnp.dot(a, lo, preferred_element_type=F32))


def _split3_lhs_dot(a, b):
    hi = a.astype(BF16)
    r1 = a - hi.astype(F32)
    mid = r1.astype(BF16)
    lo = (r1 - mid.astype(F32)).astype(BF16)
    return ((jnp.dot(hi, b, preferred_element_type=F32) + jnp.dot(mid, b, preferred_element_type=F32))
            + jnp.dot(lo, b, preferred_element_type=F32))


def _dot3(a, w_hi, w_lo):
    a_hi = a.astype(BF16)
    a_lo = (a - a_hi.astype(F32)).astype(BF16)
    return (jnp.dot(a_hi, w_hi, preferred_element_type=F32)
            + (jnp.dot(a_lo, w_hi, preferred_element_type=F32) + jnp.dot(a_hi, w_lo, preferred_element_type=F32)))


def _iota(shape, axis):
    return lax.broadcasted_iota(jnp.int32, shape, axis)


def _block_mask(rows, cols, rb, cb):
    return (_iota((rows, cols), 0) // rb) == (_iota((rows, cols), 1) // cb)


def _shift_rows(x, k):
    t = _iota(x.shape, 0)
    return jnp.where(t >= k, pltpu.roll(x, k, 0), 0.0)


def _cumsum_rows(x):
    k = 1
    while k < x.shape[0]:
        x = x + _shift_rows(x, k)
        k *= 2
    return x


def _softplus(x):
    return jnp.maximum(x, 0.0) + jnp.log1p(jnp.exp(-jnp.abs(x)))


def _log_sigmoid(x):
    return -_softplus(-x)


def _silu(x):
    return x * jax.nn.sigmoid(x)


def _masked_softmax(s, mask):
    s = jnp.where(mask, s, -jnp.inf)
    m = jnp.maximum(jnp.max(s, axis=-1, keepdims=True), F32_LOWEST)
    e = jnp.exp(s - m)
    den = jnp.sum(e, axis=-1, keepdims=True)
    return e / jnp.where(den > 0, den, 1.0)


def _tile4(x):
    return jnp.concatenate([x, x, x, x], axis=0)


POOL_HALO = 16


def _gla_masks():
    return (_block_mask(4 * CHUNK, 128, CHUNK, GLA_DK),
            _block_mask(4 * CHUNK, 256, CHUNK, GLA_DV),
            _block_mask(256, 128, GLA_DV, GLA_DK),
            (_iota((CHUNK, 256), 1) % CHUNK) <= _iota((CHUNK, 256), 0))


def _seqs(f, *lists):
    return [f(*args) for args in zip(*lists)]


def _run_staggered(stage_gens, starts, periods):
    live = list(range(len(stage_gens)))
    tick = 0
    while live:
        for i in list(live):
            if tick >= starts[i] and (tick - starts[i]) % periods[i] == 0:
                try:
                    next(stage_gens[i])
                except StopIteration:
                    live.remove(i)
        tick += 1


def _gla_chunk(masks, zqk, v, r, zm, wlr_ref, blr_ref, gn_ref, ones_ref, st_refs, out):
    mask_k, mask_v, mask_st, causal = masks
    q = [z[:, 0:128] * GLA_DK ** -0.5 for z in zqk]
    k = [z[:, 128:256] for z in zqk]
    pre = _seqs(lambda a: _dot3(a, wlr_ref[0], wlr_ref[1]), zm)
    yield
    bc = _seqs(lambda a: _cumsum_rows(_log_sigmoid(a + blr_ref[...]) / GLA_GATE_NORM), pre)
    bl = [a[CHUNK - 1:CHUNK, :] for a in bc]
    q_e = _seqs(lambda a, c: a * jnp.exp(c), q, bc)
    k_e = _seqs(lambda a, c: a * jnp.exp(-c), k, bc)
    k_u = _seqs(lambda a, c, l: a * jnp.exp(l - c), k, bc, bl)
    st = [ref[...] for ref in st_refs]
    yield
    att = _seqs(lambda a, b: jnp.where(causal, _dot_nt(a, jnp.where(mask_k, _tile4(b), 0.0)), 0.0), q_e, k_e)
    inter = _seqs(_dot_nt, q_e, st)
    kv = _seqs(_dot_tn, v, k_u)
    yield
    o = _seqs(lambda a, b, c: _dot(a, _tile4(b.astype(BF16)) * ones_ref[...]) + c, att, v, inter)
    for ref, s_old, l, upd in zip(st_refs, st, bl, kv):
        ref[...] = s_old * jnp.exp(l) + jnp.where(mask_st, upd, 0.0)
    yield
    ms = _seqs(lambda a: _dot(a * a, ones_ref[...]) * (1.0 / GLA_DV), o)
    yield
    out[:] = _seqs(lambda a, m, g: a * lax.rsqrt(m + EPS) * gn_ref[...] * _silu(g), o, ms, r)


GDN_HALO = SUBLANE


def _gdn_masks():
    c4 = 4 * CHUNK
    col = _iota((CHUNK, c4), 1) % CHUNK
    row = _iota((CHUNK, c4), 0)
    return (_block_mask(c4, c4, CHUNK, CHUNK), col <= row, col < row, col == row)


def _gdn_chunk(masks, halo, zqkv, gate, zm, cw_ref, eba_ref, alog_ref, dtb_ref, gn_ref, ones_ref, st_refs, out):
    mask_bd, incl, strict, diag = masks
    cw = cw_ref[...]
    bd = lambda a: _tile4(a.astype(BF16)) * ones_ref[...]

    def conv_silu(h, cur):
        ext = jnp.concatenate([h, cur], axis=0)
        conv = (cw[3:4] * cur + cw[2:3] * pltpu.roll(ext, 1, 0)[GDN_HALO:]
                + cw[1:2] * pltpu.roll(ext, 2, 0)[GDN_HALO:] + cw[0:1] * pltpu.roll(ext, 3, 0)[GDN_HALO:])
        return _silu(conv)

    qkv = _seqs(conv_silu, halo, zqkv)
    yield
    v = [a[:, 512:768] for a in qkv]
    ssq = _seqs(lambda a: _dot(jnp.concatenate([a[:, 0:256] * a[:, 0:256], a[:, 256:512] * a[:, 256:512]],
                                               axis=0), ones_ref[...]), qkv)
    yield
    q = _seqs(lambda a, s: a[:, 0:256] * lax.rsqrt(s[0:CHUNK] + EPS) * GDN_DH ** -0.5, qkv, ssq)
    k = _seqs(lambda a, s: a[:, 256:512] * lax.rsqrt(s[CHUNK:2 * CHUNK] + EPS), qkv, ssq)
    beta = _seqs(lambda a: _dot(jax.nn.sigmoid(a), eba_ref[:, 0:MIX_W]), zm)
    yield
    gc = _seqs(lambda a: _split3_lhs_dot(_cumsum_rows(-jnp.exp(alog_ref[...]) * _softplus(a + dtb_ref[...])),
                                         eba_ref[:, MIX_W:2 * MIX_W]), zm)
    yield
    gl = [a[CHUNK - 1:CHUNK, :] for a in gc]
    ones_cc = jnp.ones((CHUNK, CHUNK), BF16)
    g_row = _seqs(lambda a: _split3_rhs_dot(ones_cc, jnp.where(diag, a, 0.0)), gc)
    yield
    decay =_seqs(lambda a, b: jnp.exp(jnp.where(incl, a - b, -jnp.inf)), gc, g_row)
    kb = _seqs(lambda a, b: a * b, k, beta)
    vb = _seqs(lambda a, b: a * b, v, beta)
    kq = _seqs(lambda a, b, c: _dot_nt(jnp.concatenate([a, b], axis=0), bd(c)), kb, q, k)
    yield
    n_mat =_seqs(lambda s, d: jnp.where(strict, s[0:CHUNK] * d, 0.0), kq, decay)
    a_qk = _seqs(lambda s, d: s[CHUNK:2 * CHUNK] * d, kq, decay)
    m = _seqs(lambda a: -a, n_mat)
    x = _seqs(lambda a: jnp.where(diag, 1.0, 0.0) + a, m)
    m = _seqs(lambda a: _dot(a, bd(a)), m)
    yield
    p = 2
    while p < CHUNK // 2:
        xm = _seqs(lambda a, b: _dot(jnp.concatenate([a, b], axis=0), bd(b)), x, m)
        x = _seqs(lambda a, r: a + r[0:CHUNK], x, xm)
        m = [r[CHUNK:2 * CHUNK] for r in xm]
        p *= 2
        yield
    x = _seqs(lambda a, b: a + _dot(a, bd(b)), x, m)
    yield
    uw = _seqs(lambda a, b, c, g: _dot(a, jnp.concatenate([bd(b), bd(c * jnp.exp(g))], axis=1)),
               x, vb, kb, gc)
    yield
    st = [ref[...] for ref in st_refs]
    ws = _seqs(lambda a, b, g, s: _dot(jnp.concatenate([a[:, 256:512], b * jnp.exp(g)], axis=0), s),
               uw, q, gc, st)
    yield
    v_new = _seqs(lambda a, r: a[:, 0:256] - r[0:CHUNK], uw, ws)
    o = _seqs(lambda r, qk, vn: r[CHUNK:2 * CHUNK] + _dot(qk, bd(vn)), ws, a_qk, v_new)
    upd = _seqs(lambda a, g, l, vn: _dot_tn(a * jnp.exp(l - g), vn), k, gc, gl, v_new)
    yield
    for ref, s_old, l, u in zip(st_refs, st, gl, upd):
        ref[...] = s_old * jnp.exp(l) + jnp.where(mask_bd, u, 0.0)
    ms = _seqs(lambda a: _dot(a * a, ones_ref[...]) * (1.0 / GDN_DH), o)
    yield
    out[:] = _seqs(lambda a, m_, g: a * lax.rsqrt(m_ + EPS) * gn_ref[...] * _silu(g), o, ms, gate)


def _pool_diff(u, tail, pos0):
    ext = jnp.concatenate([tail, u], axis=0)
    s2 = ext + pltpu.roll(ext, 1, 0)
    s4 = s2 + pltpu.roll(s2, 2, 0)
    s8 = s4 + pltpu.roll(s4, 4, 0)
    s16 = s8 + pltpu.roll(s8, 8, 0)
    grp = _iota(u.shape, 1) // POOL_GW
    win = jnp.where(grp == 0, s2[POOL_HALO:], jnp.where(grp == 1, s4[POOL_HALO:],
                                                        jnp.where(grp == 2, s8[POOL_HALO:], s16[POOL_HALO:])))
    width = jnp.where(grp == 0, POOL_WINDOWS[0],
                      jnp.where(grp == 1, POOL_WINDOWS[1],
                                jnp.where(grp == 2, POOL_WINDOWS[2], POOL_WINDOWS[3])))
    cnt = jnp.minimum(pos0 + _iota(u.shape, 0) + 1, width).astype(F32)
    return win / cnt - u


def _mix_kernel(x_ref, g_ref, w_ref, pw_ref, ps_ref, wlr_ref, blr_ref, gng_ref, cw_ref, eba_ref, alog_ref, dtb_ref,
                gnd_ref, ones_ref, op_ref, zn_ref, zc_ref, zm_ref, og_ref, od_ref, ptail_ref, halo_ref, *st_refs):
    nb = x_ref.shape[0]
    chunk = pl.program_id(1)

    @pl.when(chunk == 0)
    def _():
        ptail_ref[...] = jnp.zeros_like(ptail_ref)
        halo_ref[...] = jnp.zeros_like(halo_ref)
        for st_ref in st_refs:
            st_ref[...] = jnp.zeros_like(st_ref)

    h = _rms(x_ref[...].reshape(nb * CHUNK, D_MODEL), g_ref[...]).astype(BF16)
    proj = lambda lo, n: jnp.dot(h, w_ref[:, lo:lo + n], preferred_element_type=F32)
    seqs = lambda a: [a[i * CHUNK:(i + 1) * CHUNK] for i in range(nb)]
    off_g, off_d = ZP_W, ZP_W + ZG_W
    off_n = off_d + ZD_W
    off_c = off_n + ZN_W
    off_m = off_c + ZC_W
    zqkv = seqs(proj(off_d, 768))
    zm = seqs(proj(off_m, ZM_W))
    halo = [halo_ref[i] for i in range(nb)]
    for i in range(nb):
        halo_ref[i] = zqkv[i][CHUNK - GDN_HALO:, :]
        zm_ref[i] = zm[i]
    halves = [slice(0, nb // 2), slice(nb // 2, nb)] if nb > 1 else [slice(0, nb)]
    gla_qk, gla_v, gla_r = [], [], []
    gate_parts = [[] for _ in halves]

    def other_projections():
        gla_qk.extend(seqs(proj(off_g, 256)))
        yield
        gla_v.extend(seqs(proj(off_g + 256, 256)))
        yield
        u = seqs(proj(0, ZP_W))
        yield
        diff = [_pool_diff(u[i], ptail_ref[i], chunk * CHUNK) for i in range(nb)]
        for i in range(nb):
            ptail_ref[i] = u[i][CHUNK - POOL_HALO:, :]
        yield
        for lo in range(0, ZN_W, 256):
            zn = seqs(proj(off_n + lo, 256))
            for i in range(nb):
                zn_ref[i, :, lo:lo + 256] = zn[i].astype(zn_ref.dtype)
            yield
        zc = seqs(proj(off_c, ZC_W))
        for i in range(nb):
            zc_ref[i] = zc[i]
        yield
        gla_r.extend(seqs(proj(off_g + 512, 256)))
        yield
        gate = seqs(proj(off_d + 768, MIX_W))
        for hs, part in zip(halves, gate_parts):
            part.extend(gate[hs])
        yield
        o_pool = seqs(_dot(jnp.concatenate(diff, axis=0), pw_ref[...]) * ps_ref[...])
        for i in range(nb):
            op_ref[i] = o_pool[i].astype(op_ref.dtype)

    og = [None] * nb
    od_parts = [[None] * (hs.stop - hs.start) for hs in halves]
    gdn_masks = _gdn_masks()
    gens = [_gdn_chunk(gdn_masks, halo[hs], zqkv[hs], gate_part, zm[hs], cw_ref, eba_ref, alog_ref, dtb_ref,
                       gnd_ref, ones_ref, st_refs[nb:][hs], part)
            for hs, part, gate_part in zip(halves, od_parts, gate_parts)]
    gens.append(_gla_chunk(_gla_masks(), gla_qk, gla_v, gla_r, zm, wlr_ref, blr_ref, gng_ref, ones_ref,
                           st_refs[:nb], og))
    starts = [REC_LAG * i for i in range(len(halves))] + [GLA_START]
    gens.append(other_projections())
    _run_staggered(gens, starts + [0], [1] * len(gens))
    od = [o for part in od_parts for o in part]
    for i in range(nb):
        og_ref[i] = og[i].astype(og_ref.dtype)
        od_ref[i] = od[i].astype(od_ref.dtype)


def _mixers(x3, g, w, pool_w_bd, pool_scale, wlr2, blr, gn_gla, cw, eba, alog, dtb, gn_gdn, ones_bd):
    b, s, _ = x3.shape
    nb = REC_SEQS if b % REC_SEQS == 0 else 1
    const = lambda shape: pl.BlockSpec(shape, lambda i, c: (0,) * len(shape))
    tok = lambda n: pl.BlockSpec((nb, CHUNK, n), lambda i, c: (i, c, 0))
    outs = ((MIX_W, BF16), (ZN_W, BF16), (ZC_W, F32), (ZM_W, F32), (MIX_W, BF16), (MIX_W, BF16))
    return pl.pallas_call(
        _mix_kernel,
        grid=(b // nb, s // CHUNK),
        in_specs=[tok(D_MODEL), const((1, D_MODEL)), const((D_MODEL, Z_W)), const((MIX_W, MIX_W)), const((1, MIX_W)),
                  const((2, ZM_W, 128)), const((1, 128)), const((1, MIX_W)),
                  const((GDN_CONV, 768)), const((ZM_W, 2 * MIX_W)), const((1, ZM_W)), const((1, ZM_W)),
                  const((1, MIX_W)), const((MIX_W, MIX_W))],
        out_specs=[tok(n) for n, _ in outs],
        out_shape=[jax.ShapeDtypeStruct((b, s, n), dt) for n, dt in outs],
        scratch_shapes=([pltpu.VMEM((nb, POOL_HALO, MIX_W), F32), pltpu.VMEM((nb, GDN_HALO, 768), F32)]
                        + [pltpu.VMEM((256, 128), F32)] * nb + [pltpu.VMEM((256, 256), F32)] * nb),
        compiler_params=_cparams(("parallel", "arbitrary")),
        name="inproj_mixers",
    )(x3, g, w, pool_w_bd, pool_scale, wlr2, blr, gn_gla, cw, eba, alog, dtb, gn_gdn, ones_bd)


def _cmp_kernel(kc_ref, vc_ref, pe_ref, w1_ref, w2_ref, ck_ref, cv_ref):
    n_sub = NSA_CMP_LEN // NSA_CMP_STRIDE
    n_chunks = kc_ref.shape[1] // NSA_CMP_STRIDE
    for which, (src_ref, out_ref) in enumerate(((kc_ref, ck_ref), (vc_ref, cv_ref))):
        parts = [jnp.zeros((n_chunks, LANE), F32) for _ in range(n_sub)]
        for i in range(NSA_CMP_STRIDE):
            slab = src_ref[0, pl.ds(i, n_chunks, stride=NSA_CMP_STRIDE), :]
            for sub in range(n_sub):
                p = sub * NSA_CMP_STRIDE + i
                parts[sub] = parts[sub] + _dot(slab + pe_ref[which, p:p + 1, :], w1_ref[which, p])
        pre = parts[0] + pltpu.roll(parts[1], n_chunks - 1, 0)
        out_ref[0] = _dot(jax.nn.gelu(pre), w2_ref[which])


def _nsa_compress(zc, pe_x, w1_bd, w2_bd):
    b, s, _ = zc.shape
    n_chunks = s // NSA_CMP_STRIDE
    return pl.pallas_call(
        _cmp_kernel,
        grid=(b,),
        in_specs=[pl.BlockSpec((1, s, LANE), lambda i: (i, 0, 0)),
                  pl.BlockSpec((1, s, LANE), lambda i: (i, 0, 1)),
                  pl.BlockSpec((2, NSA_CMP_LEN, LANE), lambda i: (0, 0, 0)),
                  pl.BlockSpec((2, NSA_CMP_LEN, LANE, LANE), lambda i: (0, 0, 0, 0)),
                  pl.BlockSpec((2, LANE, LANE), lambda i: (0, 0, 0))],
        out_specs=[pl.BlockSpec((1, n_chunks, LANE), lambda i: (i, 0, 0))] * 2,
        out_shape=[jax.ShapeDtypeStruct((b, n_chunks, LANE), F32)] * 2,
        compiler_params=_cparams(("parallel",)),
        name="nsa_compress",
    )(zc, zc, pe_x, w1_bd, w2_bd)


def _topk_rank(v):
    n, t = v.shape
    blocks = [v[r:r + SUBLANE] for r in range(0, n, SUBLANE)]
    sub = _iota((SUBLANE, t), 0)
    rank = [jnp.zeros((SUBLANE, t), jnp.int32) for _ in blocks]
    for m in range(n):
        other = v[m:m + 1, :]
        mb, ms = divmod(m, SUBLANE)
        for r, blk in enumerate(blocks):
            if r < mb:
                ahead = (other > blk).astype(jnp.int32)
            elif r > mb:
                ahead = (other >= blk).astype(jnp.int32)
            else:
                ahead = jnp.where(sub > ms, (other >= blk).astype(jnp.int32), (other > blk).astype(jnp.int32))
            rank[r] = rank[r] + ahead
    return jnp.concatenate(rank, axis=0)


def _exp_weights(s, m):
    return jnp.exp((s - m).astype(BF16))


def _normalise_aug(o_aug, in_grp):
    den = pltpu.roll(o_aug, NSA_DH, 1)
    return jnp.where(in_grp, o_aug / jnp.where(den > 0, den, 1.0), 0.0)


def _nsa_kernel(q_ref, ksv_ref, kwv_ref, ck_ref, cv_ref, zm_ref, covt_ref, gexp_ref, o_ref,
                kaug_ref, vaug_ref, vwaug_ref, s_ref, mx_ref, acc_ref):
    nb = q_ref.shape[0]
    tq = q_ref.shape[1]
    s_len = ksv_ref.shape[1]
    n_slc = s_len // NSA_SEL_LEN
    qi = pl.program_id(1)
    s0 = pl.multiple_of(qi * tq, tq)
    lane_grp = _iota((1, LANE), 1) // NSA_DH
    chains = [(bi, g) for bi in range(nb) for g in range(NSA_GROUPS)]
    n_chains = range(len(chains))

    @pl.when(qi == 0)
    def _():
        lane = _iota((s_len, LANE), 1)
        key_blk = _iota((s_len, LANE), 0) // NSA_SEL_LEN
        for bi in range(nb):
            k = ksv_ref[bi, :, 0:LANE].astype(F32)
            v = ksv_ref[bi, :, LANE:2 * LANE].astype(F32)
            vw = kwv_ref[bi, :, LANE:2 * LANE].astype(F32)
            for g in range(NSA_GROUPS):
                own = lane // NSA_DH == g
                onehot = jnp.where(lane - (1 - g) * NSA_DH == key_blk, 1.0, 0.0)
                kaug_ref[bi * NSA_GROUPS + g] = jnp.where(own, k, onehot).astype(BF16)
                vaug_ref[bi * NSA_GROUPS + g] = jnp.where(own, v, 1.0).astype(BF16)
                vwaug_ref[bi * NSA_GROUPS + g] = jnp.where(own, vw, 1.0).astype(BF16)

    t_col = s0 + _iota((tq, 1), 0)
    t_col2 = jnp.concatenate([t_col, t_col], axis=0)
    t_row = s0 + _iota((1, tq), 1)
    q = [q_ref[bi].astype(F32) * NSA_DH ** -0.5 for bi in range(nb)]
    ck_hi = [ck_ref[bi].astype(BF16) for bi in range(nb)]
    ck_lo = [(ck_ref[bi] - ck_hi[bi].astype(F32)).astype(BF16) for bi in range(nb)]
    cv = [cv_ref[bi] for bi in range(nb)]
    cmp_end = _iota((1, N_CMP_PAD), 1) * NSA_CMP_STRIDE + (NSA_CMP_LEN - 1)
    cmp_valid = cmp_end <= t_col2
    blk_t = _iota((n_slc, tq), 0)
    cur_t = t_row // NSA_SEL_LEN
    forced_t = (blk_t == 0) | (blk_t == cur_t) | (blk_t == cur_t - 1)
    future_t = blk_t > cur_t
    row_in_tile = jnp.concatenate([_iota((tq, tq), 0)] * NSA_HPG, axis=0)
    col_in_tile = _iota((NSA_HPG * tq, tq), 1)
    diag_ok = col_in_tile <= row_in_tile
    n_wt = NSA_WINDOW // tq + 1
    never = 2 * tq
    win_off = [pl.multiple_of(jnp.maximum(qi - (n_wt - 1 - w), 0) * tq, tq) for w in range(n_wt)]
    win_ok = [col_in_tile > row_in_tile + jnp.where(qi >= n_wt - 1, 0, never)]
    win_ok += [col_in_tile >= jnp.where(qi >= n_wt - 1 - w, 0, never) for w in range(1, n_wt - 1)]
    win_ok += [diag_ok]

    o_cmp = [[jnp.zeros((tq, LANE), F32) for _ in range(NSA_HPG)] for _ in range(nb)]
    o_slc = [[jnp.zeros((tq, LANE), F32) for _ in range(NSA_HPG)] for _ in range(nb)]
    o_win = [[jnp.zeros((tq, LANE), F32) for _ in range(NSA_HPG)] for _ in range(nb)]
    in_grp = [lane_grp == g for _, g in chains]
    q2 = [jnp.concatenate([jnp.where(in_grp[c], q[bi][:, j * LANE:(j + 1) * LANE], 0.0)
                           for j in range(NSA_HPG)], axis=0) for c, (bi, _) in enumerate(chains)]
    qb = [a.astype(BF16) for a in q2]
    p_c = [_masked_softmax(_dot_nt(qb[c], ck_hi[bi]) + _dot_nt(qb[c], ck_lo[bi]), cmp_valid)
           for c, (bi, _) in enumerate(chains)]
    oc = [jnp.where(in_grp[c], _dot(p_c[c], cv[bi]), 0.0) for c, (bi, _) in enumerate(chains)]
    imp_c = [a[0:tq] + a[tq:2 * tq] for a in p_c]
    imp_hi = [a.astype(BF16) for a in imp_c]
    imp_lo = [(a - h.astype(F32)).astype(BF16) for a, h in zip(imp_c, imp_hi)]
    imp_t = [(_dot_nt(covt_ref[...], h) + _dot_nt(covt_ref[...], l_))[0:n_slc]
             for h, l_ in zip(imp_hi, imp_lo)]
    imp_t = [jnp.where(forced_t, jnp.inf, jnp.where(future_t, -jnp.inf, a)) for a in imp_t]
    s_w = [[jnp.where(ok, _dot_nt(qb[c], kwv_ref[bi, pl.ds(off, tq), 0:LANE]), -jnp.inf)
            for ok, off in zip(win_ok, win_off)] for c, (bi, _) in enumerate(chains)]
    qa = []
    for c, (_, g) in enumerate(chains):
        bias_t = jnp.where(_topk_rank(imp_t[c]) < NSA_N_SEL, 0.0, NEG_BIG)
        lo = (1 - g) * NSA_DH
        rows = ([jnp.zeros((lo, tq), F32)] if lo else []) + [bias_t, jnp.zeros((LANE - lo - n_slc, tq), F32)]
        bias = jnp.concatenate(rows, axis=0).T
        qa.append(jnp.where(in_grp[c], q2[c], jnp.concatenate([bias] * NSA_HPG, axis=0)).astype(BF16))
    m_w = []
    for c in n_chains:
        m_c = s_w[c][0]
        for s_t in s_w[c][1:]:
            m_c = jnp.maximum(m_c, s_t)
        m_w.append(jnp.maximum(jnp.max(m_c, axis=-1, keepdims=True), F32_LOWEST))
    ow = [jnp.zeros((NSA_HPG * tq, LANE), F32) for _ in n_chains]
    for w, off in enumerate(win_off):
        for c in n_chains:
            ow[c] = ow[c] + jnp.dot(_exp_weights(s_w[c][w], m_w[c]), vwaug_ref[c, pl.ds(off, tq), :],
                                    preferred_element_type=F32)
    for c, (bi, _) in enumerate(chains):
        ow_c = _normalise_aug(ow[c], in_grp[c])
        for j in range(NSA_HPG):
            rows_j = slice(j * tq, (j + 1) * tq)
            o_cmp[bi][j] = o_cmp[bi][j] + oc[c][rows_j]
            o_win[bi][j] = o_win[bi][j] + ow_c[rows_j]

    half_max = lambda s_t: jnp.maximum(s_t[:, 0:LANE], s_t[:, LANE:2 * LANE])
    mx_ref[...] = jnp.full(mx_ref.shape, NEG_BIG, F32)
    groups = n_chains

    def score_tiles(tiles):
        offs = [pl.multiple_of(kt * tq, tq) for kt in tiles]
        s_new = [[_dot_nt(qa[g], kaug_ref[g, pl.ds(off, tq), :]) for g in groups] for off in offs]
        for kt, s_kt in zip(tiles, s_new):
            for g in groups:
                s_ref[g, kt] = s_kt[g]
        for g in groups:
            m_new = half_max(s_new[0][g])
            for s_kt in s_new[1:]:
                m_new = jnp.maximum(m_new, half_max(s_kt[g]))
            mx_ref[g] = jnp.maximum(mx_ref[g], m_new)

    def score_pair(p, carry):
        score_tiles([2 * p, 2 * p + 1])
        return carry

    lax.fori_loop(0, qi // 2, score_pair, 0)

    @pl.when(qi % 2 == 1)
    def _():
        score_tiles([qi - 1])

    m_s = []
    for g in groups:
        s_d = jnp.where(diag_ok, _dot_nt(qa[g], kaug_ref[g, pl.ds(s0, tq), :]), NEG_BIG)
        s_ref[g, qi] = s_d
        m_g = jnp.max(jnp.maximum(mx_ref[g], half_max(s_d)), axis=-1, keepdims=True)
        m_s.append(jnp.maximum(m_g, F32_LOWEST))
    acc_ref[...] = jnp.zeros(acc_ref.shape, F32)

    def attend_tiles(tiles):
        offs = [pl.multiple_of(kt * tq, tq) for kt in tiles]
        pv = [[jnp.dot(_exp_weights(s_ref[g, kt], m_s[g]), vaug_ref[g, pl.ds(off, tq), :],
                       preferred_element_type=F32) for g in groups] for kt, off in zip(tiles, offs)]
        for g in groups:
            upd = pv[0][g]
            for pv_kt in pv[1:]:
                upd = upd + pv_kt[g]
            acc_ref[g] += upd

    def attend_pair(p, carry):
        attend_tiles([2 * p, 2 * p + 1])
        return carry

    lax.fori_loop(0, (qi + 1) // 2, attend_pair, 0)

    @pl.when(qi % 2 == 0)
    def _():
        attend_tiles([qi])

    for c, (bi, _) in enumerate(chains):
        os_ = _normalise_aug(acc_ref[c], in_grp[c])
        for j in range(NSA_HPG):
            o_slc[bi][j] = o_slc[bi][j] + os_[j * tq:(j + 1) * tq]

    for bi in range(nb):
        gates = _split2_dot(jax.nn.sigmoid(zm_ref[bi]), gexp_ref[...])
        o = (gates[:, 0:MIX_W] * jnp.concatenate(o_cmp[bi], axis=1)
             + gates[:, MIX_W:2 * MIX_W] * jnp.concatenate(o_slc[bi], axis=1)
             + gates[:, 2 * MIX_W:3 * MIX_W] * jnp.concatenate(o_win[bi], axis=1))
        o_ref[bi] = o.astype(o_ref.dtype)


def _nsa(zn, zm, ck, cv, covt, gexp):
    b, s, _ = zn.shape
    tq = TQ_NSA
    nb = NSA_SEQS if b % NSA_SEQS == 0 else 1
    n_chains = nb * NSA_GROUPS
    return pl.pallas_call(
        _nsa_kernel,
        grid=(b // nb, s // tq),
        in_specs=[pl.BlockSpec((nb, tq, 256), lambda i, j: (i, j, 0)),
                  pl.BlockSpec((nb, s, 256), lambda i, j: (i, 0, 1)),
                  pl.BlockSpec((nb, s, 256), lambda i, j: (i, 0, 2)),
                  pl.BlockSpec((nb, N_CMP_PAD, LANE), lambda i, j: (i, 0, 0)),
                  pl.BlockSpec((nb, N_CMP_PAD, LANE), lambda i, j: (i, 0, 0)),
                  pl.BlockSpec((nb, tq, ZM_W), lambda i, j: (i, j, 0)),
                  pl.BlockSpec((N_SLC_PAD, N_CMP_PAD), lambda i, j: (0, 0)),
                  pl.BlockSpec((ZM_W, 3 * MIX_W), lambda i, j: (0, 0))],
        out_specs=pl.BlockSpec((nb, tq, MIX_W), lambda i, j: (i, j, 0)),
        out_shape=jax.ShapeDtypeStruct((b, s, MIX_W), BF16),
        scratch_shapes=[pltpu.VMEM((n_chains, s, LANE), BF16)] * 3
                       + [pltpu.VMEM((n_chains, s // tq, NSA_HPG * tq, tq), F32),
                          pltpu.VMEM((n_chains, NSA_HPG * tq, LANE), F32),
                          pltpu.VMEM((n_chains, NSA_HPG * tq, LANE), F32)],
        compiler_params=_cparams(("parallel", "arbitrary")),
        name="nsa_attn",
    )(zn, zn, zn, ck, cv, zm, covt, gexp)


def _combine_kernel(x_ref, g_ref, op_ref, oa_ref, od_ref, on_ref, wg_ref, bg_ref, wb_ref, wo_ref, o_ref):
    x = x_ref[...]
    h = _rms(x, g_ref[...]).astype(BF16)
    y = jnp.zeros(x.shape, F32)
    for i, br_ref in enumerate((op_ref, oa_ref, od_ref, on_ref)):
        gate = jax.nn.sigmoid(jnp.dot(h, wg_ref[i], preferred_element_type=F32) + bg_ref[i])
        y = y + gate * jnp.dot(br_ref[...], wb_ref[i], preferred_element_type=F32)
    o_ref[...] = x + _dot(y, wo_ref[...])


def _combine(x2, g, branches, wg, bg, wb, wo):
    t = x2.shape[0]
    tm = TM_COMB
    return pl.pallas_call(
        _combine_kernel,
        grid=(t // tm,),
        in_specs=[pl.BlockSpec((tm, D_MODEL), lambda i: (i, 0)),
                  pl.BlockSpec((1, D_MODEL), lambda i: (0, 0))]
                 + [pl.BlockSpec((tm, MIX_W), lambda i: (i, 0))] * N_BRANCH
                 + [pl.BlockSpec((N_BRANCH, D_MODEL, D_MODEL), lambda i: (0, 0, 0)),
                    pl.BlockSpec((N_BRANCH, 1, D_MODEL), lambda i: (0, 0, 0)),
                    pl.BlockSpec((N_BRANCH, MIX_W, D_MODEL), lambda i: (0, 0, 0)),
                    pl.BlockSpec((D_MODEL, D_MODEL), lambda i: (0, 0))],
        out_specs=pl.BlockSpec((tm, D_MODEL), lambda i: (i, 0)),
        out_shape=jax.ShapeDtypeStruct((t, D_MODEL), F32),
        compiler_params=_cparams(("parallel",)),
        name="combine",
    )(x2, g, *branches, wg, bg, wb, wo)


def _memkv_kernel(m_ref, g_ref, w_ref, o_ref):
    o_ref[0] = _dot(_rms(m_ref[0], g_ref[...]), w_ref[...]).astype(o_ref.dtype)


def _memkv(mem, g, w):
    b, m, _ = mem.shape
    n = 2 * X_HEADS * X_DH
    return pl.pallas_call(
        _memkv_kernel,
        grid=(b,),
        in_specs=[pl.BlockSpec((1, m, D_MODEL), lambda i: (i, 0, 0)),
                  pl.BlockSpec((1, D_MODEL), lambda i: (0, 0)),
                  pl.BlockSpec((D_MODEL, n), lambda i: (0, 0))],
        out_specs=pl.BlockSpec((1, m, n), lambda i: (i, 0, 0)),
        out_shape=jax.ShapeDtypeStruct((b, m, n), BF16),
        compiler_params=_cparams(("parallel",)),
        name="mem_kv",
    )(mem, g, w)


def _cross_kernel(x_ref, g_ref, wq_ref, kv_ref, wo_ref, o_ref):
    n_k = X_HEADS * X_DH
    subs = [slice(i * CROSS_SUB, (i + 1) * CROSS_SUB) for i in range(x_ref.shape[1] // CROSS_SUB)]
    x = [x_ref[0, r, :] for r in subs]
    q = [_dot(_rms(a, g_ref[...]), wq_ref[...]) * X_DH ** -0.5 for a in x]
    pairs = [(i, h) for h in range(X_HEADS) for i in range(len(subs))]
    head = lambda h: slice(h * X_DH, (h + 1) * X_DH)
    sc = [_dot_nt(q[i][:, head(h)], kv_ref[0, :, head(h)]) for i, h in pairs]
    e = [jnp.exp(s - jnp.max(s, axis=-1, keepdims=True)) for s in sc]
    o = [_dot(e_ih, kv_ref[0, :, n_k + h * X_DH:n_k + (h + 1) * X_DH]) / jnp.sum(e_ih, axis=-1, keepdims=True)
         for e_ih, (i, h) in zip(e, pairs)]
    for i, r in enumerate(subs):
        o_i = jnp.concatenate([o[pairs.index((i, h))] for h in range(X_HEADS)], axis=1)
        o_ref[0, r, :] = x[i] + _dot(o_i, wo_ref[...])


def _cross(x3, g, wq, kv, wo):
    b, s, _ = x3.shape
    tm = TM_CROSS
    m = kv.shape[1]
    n_k = X_HEADS * X_DH
    return pl.pallas_call(
        _cross_kernel,
        grid=(b, s // tm),
        in_specs=[pl.BlockSpec((1, tm, D_MODEL), lambda i, j: (i, j, 0)),
                  pl.BlockSpec((1, D_MODEL), lambda i, j: (0, 0)),
                  pl.BlockSpec((D_MODEL, n_k), lambda i, j: (0, 0)),
                  pl.BlockSpec((1, m, 2 * n_k), lambda i, j: (i, 0, 0)),
                  pl.BlockSpec((n_k, D_MODEL), lambda i, j: (0, 0))],
        out_specs=pl.BlockSpec((1, tm, D_MODEL), lambda i, j: (i, j, 0)),
        out_shape=jax.ShapeDtypeStruct((b, s, D_MODEL), F32),
        compiler_params=_cparams(("parallel", "parallel")),
        name="cross_attn",
    )(x3, g, wq, kv, wo)


def _ffn_kernel(x_ref, g_ref, wup_ref, cw_ref, cb_ref, wd_ref, gf_ref, o_ref, tail_ref, act_ref, *, final):
    @pl.when(pl.program_id(1) == 0)
    def _():
        tail_ref[...] = jnp.zeros_like(tail_ref)

    ts = x_ref.shape[1]
    x = x_ref[0]
    hn = _rms(x, g_ref[...]).astype(BF16)
    for c in range(D_FF // FF_CHUNK):
        cols = slice(c * FF_CHUNK, (c + 1) * FF_CHUNK)
        gcols = slice(D_FF + c * FF_CHUNK, D_FF + (c + 1) * FF_CHUNK)
        u = jnp.dot(hn, wup_ref[:, cols], preferred_element_type=F32)
        v = jnp.dot(hn, wup_ref[:, gcols], preferred_element_type=F32)
        ext = jnp.concatenate([tail_ref[:, cols], u], axis=0)
        tail_ref[:, cols] = u[ts - SUBLANE:, :]
        cw = cw_ref[:, cols]
        y = (cw[2:3] * u + cw[1:2] * pltpu.roll(ext, 1, 0)[SUBLANE:]
             + cw[0:1] * pltpu.roll(ext, 2, 0)[SUBLANE:] + cb_ref[:, cols])
        act_ref[:, cols] = (jax.nn.gelu(y) * v).astype(BF16)
    out = x + jnp.dot(act_ref[...], wd_ref[...], preferred_element_type=F32)
    if final:
        out = _rms(out, gf_ref[...])
    o_ref[0] = out


def _ffn(x3, g, wup, cw, cb, wd, gf, final):
    b, s, _ = x3.shape
    ts = TS_FFN
    return pl.pallas_call(
        functools.partial(_ffn_kernel, final=final),
        grid=(b, s // ts),
        in_specs=[pl.BlockSpec((1, ts, D_MODEL), lambda i, j: (i, j, 0)),
                  pl.BlockSpec((1, D_MODEL), lambda i, j: (0, 0)),
                  pl.BlockSpec((D_MODEL, 2 * D_FF), lambda i, j: (0, 0), pipeline_mode=pl.Buffered(1)),
                  pl.BlockSpec((FFN_CONV, D_FF), lambda i, j: (0, 0)),
                  pl.BlockSpec((1, D_FF), lambda i, j: (0, 0)),
                  pl.BlockSpec((D_FF, D_MODEL), lambda i, j: (0, 0), pipeline_mode=pl.Buffered(1)),
                  pl.BlockSpec((1, D_MODEL), lambda i, j: (0, 0))],
        out_specs=pl.BlockSpec((1, ts, D_MODEL), lambda i, j: (i, j, 0)),
        out_shape=jax.ShapeDtypeStruct((b, s, D_MODEL), F32),
        scratch_shapes=[pltpu.VMEM((SUBLANE, D_FF), F32), pltpu.VMEM((ts, D_FF), BF16)],
        compiler_params=_cparams(("parallel", "arbitrary")),
        name="conv_ffn",
    )(x3, g, wup, cw, cb, wd, gf)


def _inproj_columns():
    starts = np.concatenate([[0], np.cumsum(IN_SPLITS)])
    (p_in, a_q, a_k, a_v, a_r, a_lr, d_q, d_k, d_v, d_b, d_a, d_g,
     n_q, n_kc, n_vc, n_ks, n_vs, n_kw, n_vw, n_g) = [np.arange(starts[i], starts[i + 1])
                                                      for i in range(len(IN_SPLITS))]
    n_q = n_q.reshape(NSA_GROUPS, NSA_HPG, NSA_DH).transpose(1, 0, 2).reshape(-1)
    misc = np.full((ZM_W,), N_IN)
    misc[MISC_LR:MISC_LR + GLA_LOWRANK] = a_lr
    misc[MISC_B:MISC_B + GDN_HEADS] = d_b
    misc[MISC_A:MISC_A + GDN_HEADS] = d_a
    misc[MISC_G:MISC_G + 3 * NSA_HEADS] = n_g
    cols = np.concatenate([p_in, a_q, a_k, a_v, a_r, d_q, d_k, d_v, d_g,
                           n_q, n_ks, n_vs, n_kw, n_vw, n_kc, n_vc, misc])
    assert cols.shape[0] == Z_W
    return cols


def _head_expand(offset, n_heads, width):
    e = np.zeros((ZM_W, n_heads * width), np.float32)
    for h in range(n_heads):
        e[offset + h, h * width:(h + 1) * width] = 1.0
    return e


def _nsa_constants(s):
    n_cmp = s // NSA_CMP_STRIDE - NSA_CMP_LEN // NSA_CMP_STRIDE + 1
    n_slc = s // NSA_SEL_LEN
    c_start = np.arange(n_cmp) * NSA_CMP_STRIDE
    s_start = np.arange(n_slc) * NSA_SEL_LEN
    cover = np.zeros((N_CMP_PAD, N_SLC_PAD), np.float32)
    cover[:n_cmp, :n_slc] = ((c_start[:, None] <= s_start[None, :] + NSA_SEL_LEN - 1)
                             & (c_start[:, None] + NSA_CMP_LEN - 1 >= s_start[None, :]))
    gexp = np.zeros((ZM_W, 3, MIX_W), np.float32)
    for g in range(NSA_GROUPS):
        for j in range(NSA_HPG):
            slot = j * NSA_GROUPS + g
            for c in range(3):
                gexp[MISC_G + (g * NSA_HPG + j) * 3 + c, c, slot * NSA_DH:(slot + 1) * NSA_DH] = 1.0
    return jnp.asarray(cover.T, dtype=BF16), jnp.asarray(gexp.reshape(ZM_W, 3 * MIX_W), dtype=BF16)


def _block_diag(blocks):
    n, a, b = blocks.shape
    return jnp.einsum('gh,gab->gahb', jnp.eye(n, dtype=blocks.dtype), blocks).reshape(n * a, n * b)


def kernel(x, mem, g_mix, w_in, pool_w, pool_scale, gla_w_lr, gla_b_lr, gla_g_norm, gdn_conv, gdn_a_log,
           gdn_dt_bias, gdn_g_norm, nsa_pe, nsa_cmp_w1, nsa_cmp_w2, w_branch, w_gate, b_gate, w_out, g_cross,
           g_mem, w_xq, w_mem_kv, w_xo, g_ffn, w_up, ffn_conv, ffn_conv_b, w_down, g_final):
    b, s, d = x.shape
    depth = w_in.shape[0]
    t = b * s
    cols = _inproj_columns()
    covt, gexp = _nsa_constants(s)
    eba = jnp.asarray(np.concatenate([_head_expand(MISC_B, GDN_HEADS, GDN_DH),
                                      _head_expand(MISC_A, GDN_HEADS, GDN_DH)], axis=1), dtype=BF16)
    ones_bd = _block_diag(jnp.ones((GDN_HEADS, GDN_DH, GDN_DH), BF16))
    nsa_rows = np.arange(MIX_W).reshape(NSA_GROUPS, NSA_HPG, NSA_DH).transpose(1, 0, 2).reshape(-1)
    row = lambda v: v.reshape(1, -1).astype(F32)
    misc_a = lambda v: jnp.zeros((1, ZM_W), F32).at[0, MISC_A:MISC_A + GDN_HEADS].set(v)

    x2 = x.reshape(t, d)
    for l in range(depth):
        w_in_r = jnp.concatenate([w_in[l], jnp.zeros((d, 1), F32)], axis=1)[:, cols].astype(BF16)
        wlr = jnp.zeros((ZM_W, GLA_HEADS * GLA_DK), F32).at[MISC_LR:MISC_LR + GLA_LOWRANK].set(gla_w_lr[l])
        wlr_hi = wlr.astype(BF16)
        wlr2 = jnp.stack([wlr_hi, (wlr - wlr_hi.astype(F32)).astype(BF16)])
        o_pool, zn, zc, zm, o_gla, o_gdn = _mixers(
            x2.reshape(b, s, d), row(g_mix[l]), w_in_r, _block_diag(pool_w[l]).astype(BF16), row(pool_scale[l]),
            wlr2, row(gla_b_lr[l]), row(jnp.tile(gla_g_norm[l], GLA_HEADS)),
            gdn_conv[l], eba, misc_a(gdn_a_log[l]), misc_a(gdn_dt_bias[l]),
            row(jnp.tile(gdn_g_norm[l], GDN_HEADS)), ones_bd)

        pe_x = jnp.tile(nsa_pe[l], (1, 1, NSA_GROUPS))
        w1 = nsa_cmp_w1[l].reshape(2, NSA_CMP_LEN, NSA_DH, NSA_DH)
        eye_g = jnp.eye(NSA_GROUPS, dtype=F32)
        w1_bd = jnp.einsum('gh,kpde->kpgdhe', eye_g, w1).reshape(2, NSA_CMP_LEN, LANE, LANE).astype(BF16)
        w2_bd = jnp.einsum('gh,kde->kgdhe', eye_g, nsa_cmp_w2[l]).reshape(2, LANE, LANE).astype(BF16)
        ck, cv = _nsa_compress(zc, pe_x, w1_bd, w2_bd)
        o_nsa = _nsa(zn, zm, ck, cv, covt, gexp)

        wb = jnp.concatenate([w_branch[l, :3], w_branch[l, 3][nsa_rows][None]], axis=0).astype(BF16)
        branches = [o.reshape(t, MIX_W) for o in (o_pool, o_gla, o_gdn, o_nsa)]
        x2 = _combine(x2, row(g_mix[l]), branches, w_gate[l].astype(BF16),
                      b_gate[l].reshape(N_BRANCH, 1, d), wb, w_out[l].astype(BF16))

        kv = _memkv(mem, row(g_mem[l]), w_mem_kv[l].astype(BF16))
        x3 = _cross(x2.reshape(b, s, d), row(g_cross[l]), w_xq[l].astype(BF16), kv, w_xo[l].astype(BF16))

        x3 = _ffn(x3, row(g_ffn[l]), w_up[l].astype(BF16), ffn_conv[l], row(ffn_conv_b[l]),
                  w_down[l].astype(BF16), row(g_final), final=(l == depth - 1))
        x2 = x3.reshape(t, d)
    return x2.reshape(b, s, d)
```

```python
import functools

import numpy as np
import jax
import jax.numpy as jnp
from jax import lax
from jax.experimental import pallas as pl
from jax.experimental.pallas import tpu as pltpu

F32 = jnp.float32
BF16 = jnp.bfloat16
HIGHEST = lax.Precision.HIGHEST

D_MODEL = 1024
MIX_W = 256
POOL_WINDOWS = (2, 4, 8, 16)
POOL_GW = 64
GLA_HEADS = 4
GLA_DK = 32
GLA_DV = 64
GLA_LOWRANK = 16
GLA_GATE_NORM = 16.0
CHUNK = 64
GDN_HEADS = 4
GDN_DH = 64
GDN_CONV = 4
NSA_HEADS = 4
NSA_GROUPS = 2
NSA_HPG = 2
NSA_DH = 64
NSA_KV = 128
NSA_CMP_LEN = 32
NSA_CMP_STRIDE = 16
NSA_SEL_LEN = 64
NSA_N_SEL = 16
NSA_WINDOW = 512
X_HEADS = 4
X_DH = 128
D_FF = 2816
FFN_CONV = 3
EPS = 1e-6
N_BRANCH = 4

IN_SPLITS = (MIX_W,
             128, 128, 256, 256, GLA_LOWRANK,
             MIX_W, MIX_W, MIX_W, GDN_HEADS, GDN_HEADS, MIX_W,
             256, NSA_KV, NSA_KV, NSA_KV, NSA_KV, NSA_KV, NSA_KV, 3 * NSA_HEADS)
N_IN = sum(IN_SPLITS)

MISC_LR = 0
MISC_B = 16
MISC_A = 20
MISC_G = 24
LANE = 128
SUBLANE = 8

Z_WIDTHS = (256, 768, 1024, 768, 256, 128)
Z_DTYPES = (F32, BF16, BF16, BF16, F32, F32)
ZP_W, ZG_W, ZD_W, ZN_W, ZC_W, ZM_W = Z_WIDTHS
Z_W = sum(Z_WIDTHS)
NEG_BIG = -1e30
F32_LOWEST = float(np.finfo(np.float32).min)

MERGE_COLS = 256
TM_CROSS = 1024
CROSS_SUB = 512
TS_FFN = 512
FF_CHUNK = 256
TQ_NSA = 256
NSA_SEQS = 1
REC_SEQS = 8
REC_LAG = 3
GLA_START = 6
N_SLC_PAD = 128
N_CMP_PAD = 128

VMEM_LIMIT = 56 * 1024 * 1024


def _cparams(sem):
    return pltpu.CompilerParams(dimension_semantics=sem, vmem_limit_bytes=VMEM_LIMIT)


def _rms(x, g):
    return x * lax.rsqrt(jnp.mean(x * x, axis=-1, keepdims=True) + EPS) * g


def _dot(a, b):
    return jnp.dot(a.astype(BF16), b.astype(BF16), preferred_element_type=F32)


def _dot_nt(a, b):
    return lax.dot_general(a.astype(BF16), b.astype(BF16), (((1,), (1,)), ((), ())),
                           preferred_element_type=F32)


def _dot_tn(a, b):
    return lax.dot_general(a.astype(BF16), b.astype(BF16), (((0,), (0,)), ((), ())),
                           preferred_element_type=F32)


def _dot_hi(a, b):
    return jnp.dot(a, b, precision=HIGHEST, preferred_element_type=F32)


def _dot_nt_hi(a, b):
    return lax.dot_general(a, b, (((1,), (1,)), ((), ())), precision=HIGHEST,
                           preferred_element_type=F32)


def _split2_dot(a, b):
    hi = a.astype(BF16)
    lo = (a - hi.astype(F32)).astype(BF16)
    return jnp.dot(hi, b, preferred_element_type=F32) + jnp.dot(lo, b, preferred_element_type=F32)


def _split3_lhs_dot(a, b):
    hi = a.astype(BF16)
    r1 = a - hi.astype(F32)
    mid = r1.astype(BF16)
    lo = (r1 - mid.astype(F32)).astype(BF16)
    return ((jnp.dot(hi, b, preferred_element_type=F32) + jnp.dot(mid, b, preferred_element_type=F32))
            + jnp.dot(lo, b, preferred_element_type=F32))


def _dot3(a, w_hi, w_lo):
    a_hi = a.astype(BF16)
    a_lo = (a - a_hi.astype(F32)).astype(BF16)
    return (jnp.dot(a_hi, w_hi, preferred_element_type=F32)
            + (jnp.dot(a_lo, w_hi, preferred_element_type=F32) + jnp.dot(a_hi, w_lo, preferred_element_type=F32)))


def _iota(shape, axis):
    return lax.broadcasted_iota(jnp.int32, shape, axis)


def _block_mask(rows, cols, rb, cb):
    return (_iota((rows, cols), 0) // rb) == (_iota((rows, cols), 1) // cb)


def _shift_rows(x, k):
    t = _iota(x.shape, 0)
    return jnp.where(t >= k, pltpu.roll(x, k, 0), 0.0)


def _cumsum_rows(x):
    k = 1
    while k < x.shape[0]:
        x = x + _shift_rows(x, k)
        k *= 2
    return x


def _softplus(x):
    return jnp.maximum(x, 0.0) + jnp.log1p(jnp.exp(-jnp.abs(x)))


def _log_sigmoid(x):
    return -_softplus(-x)


def _silu(x):
    return x * jax.nn.sigmoid(x)


def _masked_softmax(s, mask):
    s = jnp.where(mask, s, -jnp.inf)
    m = jnp.maximum(jnp.max(s, axis=-1, keepdims=True), F32_LOWEST)
    e = jnp.exp(s - m)
    den = jnp.sum(e, axis=-1, keepdims=True)
    return e / jnp.where(den > 0, den, 1.0)


def _tile4(x):
    return jnp.concatenate([x, x, x, x], axis=0)


POOL_HALO = 16


def _gla_masks():
    return (_block_mask(4 * CHUNK, 128, CHUNK, GLA_DK),
            _block_mask(4 * CHUNK, 256, CHUNK, GLA_DV),
            _block_mask(256, 128, GLA_DV, GLA_DK),
            (_iota((CHUNK, 256), 1) % CHUNK) <= _iota((CHUNK, 256), 0))


def _seqs(f, *lists):
    return [f(*args) for args in zip(*lists)]


def _run_staggered(stage_gens, starts, periods):
    live = list(range(len(stage_gens)))
    tick = 0
    while live:
        for i in list(live):
            if tick >= starts[i] and (tick - starts[i]) % periods[i] == 0:
                try:
                    next(stage_gens[i])
                except StopIteration:
                    live.remove(i)
        tick += 1


def _gla_chunk(masks, zqk, v, r, zm, wlr_ref, blr_ref, gn_ref, ones_ref, st_refs, out):
    mask_k, mask_v, mask_st, causal = masks
    q = [z[:, 0:128] * GLA_DK ** -0.5 for z in zqk]
    k = [z[:, 128:256] for z in zqk]
    pre = _seqs(lambda a: _dot3(a, wlr_ref[0], wlr_ref[1]), zm)
    yield
    bc = _seqs(lambda a: _cumsum_rows(_log_sigmoid(a + blr_ref[...]) / GLA_GATE_NORM), pre)
    bl = [a[CHUNK - 1:CHUNK, :] for a in bc]
    q_e = _seqs(lambda a, c: a * jnp.exp(c), q, bc)
    k_e = _seqs(lambda a, c: a * jnp.exp(-c), k, bc)
    k_u = _seqs(lambda a, c, l: a * jnp.exp(l - c), k, bc, bl)
    st = [ref[...] for ref in st_refs]
    yield
    att = _seqs(lambda a, b: jnp.where(causal, _dot_nt(a, jnp.where(mask_k, _tile4(b), 0.0)), 0.0), q_e, k_e)
    inter = _seqs(_dot_nt, q_e, st)
    kv = _seqs(_dot_tn, v, k_u)
    yield
    o = _seqs(lambda a, b, c: _dot(a, _tile4(b.astype(BF16)) * ones_ref[...]) + c, att, v, inter)
    for ref, s_old, l, upd in zip(st_refs, st, bl, kv):
        ref[...] = s_old * jnp.exp(l) + jnp.where(mask_st, upd, 0.0)
    yield
    ms = _seqs(lambda a: _dot(a * a, ones_ref[...]) * (1.0 / GLA_DV), o)
    yield
    out[:] = _seqs(lambda a, m, g: a * lax.rsqrt(m + EPS) * gn_ref[...] * _silu(g), o, ms, r)


GDN_HALO = SUBLANE


def _gdn_masks():
    c4 = 4 * CHUNK
    col = _iota((CHUNK, c4), 1) % CHUNK
    row = _iota((CHUNK, c4), 0)
    return (_block_mask(c4, c4, CHUNK, CHUNK), col <= row, col < row, col == row)


def _gdn_chunk(masks, halo, zqkv, gate, zm, cw_ref, eba_ref, alog_ref, dtb_ref, gn_ref, ones_ref, st_refs, out):
    mask_bd, incl, strict, diag = masks
    cw = cw_ref[...]
    bd = lambda a: _tile4(a.astype(BF16)) * ones_ref[...]

    def conv_silu(h, cur):
        ext = jnp.concatenate([h, cur], axis=0)
        conv = (cw[3:4] * cur + cw[2:3] * pltpu.roll(ext, 1, 0)[GDN_HALO:]
                + cw[1:2] * pltpu.roll(ext, 2, 0)[GDN_HALO:] + cw[0:1] * pltpu.roll(ext, 3, 0)[GDN_HALO:])
        return _silu(conv)

    qkv = _seqs(conv_silu, halo, zqkv)
    yield
    v = [a[:, 512:768] for a in qkv]
    ssq = _seqs(lambda a: _dot(jnp.concatenate([a[:, 0:256] * a[:, 0:256], a[:, 256:512] * a[:, 256:512]],
                                               axis=0), ones_ref[...]), qkv)
    yield
    q = _seqs(lambda a, s: a[:, 0:256] * lax.rsqrt(s[0:CHUNK] + EPS) * GDN_DH ** -0.5, qkv, ssq)
    k = _seqs(lambda a, s: a[:, 256:512] * lax.rsqrt(s[CHUNK:2 * CHUNK] + EPS), qkv, ssq)
    beta = _seqs(lambda a: _dot(jax.nn.sigmoid(a), eba_ref[:, 0:MIX_W]), zm)
    yield
    gc = _seqs(lambda a: _split3_lhs_dot(_cumsum_rows(-jnp.exp(alog_ref[...]) * _softplus(a + dtb_ref[...])),
                                         eba_ref[:, MIX_W:2 * MIX_W]), zm)
    yield
    gl = [a[CHUNK - 1:CHUNK, :] for a in gc]
    g_row = _seqs(lambda a: jnp.sum(jnp.where(diag, a, 0.0), axis=0, keepdims=True), gc)
    yield
    decay =_seqs(lambda a, b: jnp.exp(jnp.where(incl, a - b, -jnp.inf)), gc, g_row)
    kb = _seqs(lambda a, b: a * b, k, beta)
    vb = _seqs(lambda a, b: a * b, v, beta)
    kq = _seqs(lambda a, b, c: _dot_nt(jnp.concatenate([a, b], axis=0), bd(c)), kb, q, k)
    yield
    n_mat =_seqs(lambda s, d: jnp.where(strict, s[0:CHUNK] * d, 0.0), kq, decay)
    a_qk = _seqs(lambda s, d: s[CHUNK:2 * CHUNK] * d, kq, decay)
    m = _seqs(lambda a: -a, n_mat)
    x = _seqs(lambda a: jnp.where(diag, 1.0, 0.0) + a, m)
    m = _seqs(lambda a: _dot(a, bd(a)), m)
    yield
    p = 2
    while p < CHUNK // 2:
        xm = _seqs(lambda a, b: _dot(jnp.concatenate([a, b], axis=0), bd(b)), x, m)
        x = _seqs(lambda a, r: a + r[0:CHUNK], x, xm)
        m = [r[CHUNK:2 * CHUNK] for r in xm]
        p *= 2
        yield
    x = _seqs(lambda a, b: a + _dot(a, bd(b)), x, m)
    yield
    uw = _seqs(lambda a, b, c, g: _dot(a, jnp.concatenate([bd(b), bd(c * jnp.exp(g))], axis=1)),
               x, vb, kb, gc)
    yield
    st = [ref[...] for ref in st_refs]
    ws = _seqs(lambda a, b, g, s: _dot(jnp.concatenate([a[:, 256:512], b * jnp.exp(g)], axis=0), s),
               uw, q, gc, st)
    yield
    v_new = _seqs(lambda a, r: a[:, 0:256] - r[0:CHUNK], uw, ws)
    o = _seqs(lambda r, qk, vn: r[CHUNK:2 * CHUNK] + _dot(qk, bd(vn)), ws, a_qk, v_new)
    upd = _seqs(lambda a, g, l, vn: _dot_tn(a * jnp.exp(l - g), vn), k, gc, gl, v_new)
    yield
    for ref, s_old, l, u in zip(st_refs, st, gl, upd):
        ref[...] = s_old * jnp.exp(l) + jnp.where(mask_bd, u, 0.0)
    ms = _seqs(lambda a: _dot(a * a, ones_ref[...]) * (1.0 / GDN_DH), o)
    yield
    out[:] = _seqs(lambda a, m_, g: a * lax.rsqrt(m_ + EPS) * gn_ref[...] * _silu(g), o, ms, gate)


def _pool_diff(u, tail, pos0):
    ext = jnp.concatenate([tail, u], axis=0)
    s2 = ext + pltpu.roll(ext, 1, 0)
    s4 = s2 + pltpu.roll(s2, 2, 0)
    s8 = s4 + pltpu.roll(s4, 4, 0)
    s16 = s8 + pltpu.roll(s8, 8, 0)
    grp = _iota(u.shape, 1) // POOL_GW
    win = jnp.where(grp == 0, s2[POOL_HALO:], jnp.where(grp == 1, s4[POOL_HALO:],
                                                        jnp.where(grp == 2, s8[POOL_HALO:], s16[POOL_HALO:])))
    width = jnp.where(grp == 0, POOL_WINDOWS[0],
                      jnp.where(grp == 1, POOL_WINDOWS[1],
                                jnp.where(grp == 2, POOL_WINDOWS[2], POOL_WINDOWS[3])))
    cnt = jnp.minimum(pos0 + _iota(u.shape, 0) + 1, width).astype(F32)
    return win / cnt - u


def _mix_kernel(x_ref, g_ref, w_ref, pw_ref, ps_ref, wlr_ref, blr_ref, gng_ref, cw_ref, eba_ref, alog_ref, dtb_ref,
                gnd_ref, ones_ref, op_ref, zn_ref, zc_ref, zm_ref, og_ref, od_ref, ptail_ref, halo_ref, *st_refs):
    nb = x_ref.shape[0]
    chunk = pl.program_id(1)

    @pl.when(chunk == 0)
    def _():
        ptail_ref[...] = jnp.zeros_like(ptail_ref)
        halo_ref[...] = jnp.zeros_like(halo_ref)
        for st_ref in st_refs:
            st_ref[...] = jnp.zeros_like(st_ref)

    h = _rms(x_ref[...].reshape(nb * CHUNK, D_MODEL), g_ref[...]).astype(BF16)
    proj = lambda lo, n: jnp.dot(h, w_ref[:, lo:lo + n], preferred_element_type=F32)
    seqs = lambda a: [a[i * CHUNK:(i + 1) * CHUNK] for i in range(nb)]
    off_g, off_d = ZP_W, ZP_W + ZG_W
    off_n = off_d + ZD_W
    off_c = off_n + ZN_W
    off_m = off_c + ZC_W
    zqkv = seqs(proj(off_d, 768))
    zm = seqs(proj(off_m, ZM_W))
    halo = [halo_ref[i] for i in range(nb)]
    for i in range(nb):
        halo_ref[i] = zqkv[i][CHUNK - GDN_HALO:, :]
        zm_ref[i] = zm[i]
    halves = [slice(0, nb // 2), slice(nb // 2, nb)] if nb > 1 else [slice(0, nb)]
    gla_qk, gla_v, gla_r = [], [], []
    gate_parts = [[] for _ in halves]

    def other_projections():
        gla_qk.extend(seqs(proj(off_g, 256)))
        yield
        gla_v.extend(seqs(proj(off_g + 256, 256)))
        yield
        u = seqs(proj(0, ZP_W))
        yield
        diff = [_pool_diff(u[i], ptail_ref[i], chunk * CHUNK) for i in range(nb)]
        for i in range(nb):
            ptail_ref[i] = u[i][CHUNK - POOL_HALO:, :]
        yield
        for lo in range(0, ZN_W, 256):
            zn = seqs(proj(off_n + lo, 256))
            for i in range(nb):
                zn_ref[i, :, lo:lo + 256] = zn[i].astype(zn_ref.dtype)
            yield
        zc = seqs(proj(off_c, ZC_W))
        for i in range(nb):
            zc_ref[i] = zc[i]
        yield
        gla_r.extend(seqs(proj(off_g + 512, 256)))
        yield
        gate = seqs(proj(off_d + 768, MIX_W))
        for hs, part in zip(halves, gate_parts):
            part.extend(gate[hs])
        yield
        o_pool = seqs(_dot(jnp.concatenate(diff, axis=0), pw_ref[...]) * ps_ref[...])
        for i in range(nb):
            op_ref[i] = o_pool[i].astype(op_ref.dtype)

    og = [None] * nb
    od_parts = [[None] * (hs.stop - hs.start) for hs in halves]
    gdn_masks = _gdn_masks()
    gens = [_gdn_chunk(gdn_masks, halo[hs], zqkv[hs], gate_part, zm[hs], cw_ref, eba_ref, alog_ref, dtb_ref,
                       gnd_ref, ones_ref, st_refs[nb:][hs], part)
            for hs, part, gate_part in zip(halves, od_parts, gate_parts)]
    gens.append(_gla_chunk(_gla_masks(), gla_qk, gla_v, gla_r, zm, wlr_ref, blr_ref, gng_ref, ones_ref,
                           st_refs[:nb], og))
    starts = [REC_LAG * i for i in range(len(halves))] + [GLA_START]
    gens.append(other_projections())
    _run_staggered(gens, starts + [0], [1] * len(gens))
    od = [o for part in od_parts for o in part]
    for i in range(nb):
        og_ref[i] = og[i].astype(og_ref.dtype)
        od_ref[i] = od[i].astype(od_ref.dtype)


def _mixers(x3, g, w, pool_w_bd, pool_scale, wlr2, blr, gn_gla, cw, eba, alog, dtb, gn_gdn, ones_bd):
    b, s, _ = x3.shape
    nb = REC_SEQS if b % REC_SEQS == 0 else 1
    const = lambda shape: pl.BlockSpec(shape, lambda i, c: (0,) * len(shape))
    tok = lambda n: pl.BlockSpec((nb, CHUNK, n), lambda i, c: (i, c, 0))
    outs = ((MIX_W, BF16), (ZN_W, BF16), (ZC_W, F32), (ZM_W, F32), (MIX_W, BF16), (MIX_W, BF16))
    return pl.pallas_call(
        _mix_kernel,
        grid=(b // nb, s // CHUNK),
        in_specs=[tok(D_MODEL), const((1, D_MODEL)), const((D_MODEL, Z_W)), const((MIX_W, MIX_W)), const((1, MIX_W)),
                  const((2, ZM_W, 128)), const((1, 128)), const((1, MIX_W)),
                  const((GDN_CONV, 768)), const((ZM_W, 2 * MIX_W)), const((1, ZM_W)), const((1, ZM_W)),
                  const((1, MIX_W)), const((MIX_W, MIX_W))],
        out_specs=[tok(n) for n, _ in outs],
        out_shape=[jax.ShapeDtypeStruct((b, s, n), dt) for n, dt in outs],
        scratch_shapes=([pltpu.VMEM((nb, POOL_HALO, MIX_W), F32), pltpu.VMEM((nb, GDN_HALO, 768), F32)]
                        + [pltpu.VMEM((256, 128), F32)] * nb + [pltpu.VMEM((256, 256), F32)] * nb),
        compiler_params=_cparams(("parallel", "arbitrary")),
        name="inproj_mixers",
    )(x3, g, w, pool_w_bd, pool_scale, wlr2, blr, gn_gla, cw, eba, alog, dtb, gn_gdn, ones_bd)


def _cmp_kernel(kc_ref, vc_ref, pe_ref, w1_ref, w2_ref, ck_ref, cv_ref):
    n_sub = NSA_CMP_LEN // NSA_CMP_STRIDE
    n_chunks = kc_ref.shape[1] // NSA_CMP_STRIDE
    for which, (src_ref, out_ref) in enumerate(((kc_ref, ck_ref), (vc_ref, cv_ref))):
        parts = [jnp.zeros((n_chunks, LANE), F32) for _ in range(n_sub)]
        for i in range(NSA_CMP_STRIDE):
            slab = src_ref[0, pl.ds(i, n_chunks, stride=NSA_CMP_STRIDE), :]
            for sub in range(n_sub):
                p = sub * NSA_CMP_STRIDE + i
                parts[sub] = parts[sub] + _dot(slab + pe_ref[which, p:p + 1, :], w1_ref[which, p])
        pre = parts[0] + pltpu.roll(parts[1], n_chunks - 1, 0)
        out_ref[0] = _dot(jax.nn.gelu(pre), w2_ref[which])


def _nsa_compress(zc, pe_x, w1_bd, w2_bd):
    b, s, _ = zc.shape
    n_chunks = s // NSA_CMP_STRIDE
    return pl.pallas_call(
        _cmp_kernel,
        grid=(b,),
        in_specs=[pl.BlockSpec((1, s, LANE), lambda i: (i, 0, 0)),
                  pl.BlockSpec((1, s, LANE), lambda i: (i, 0, 1)),
                  pl.BlockSpec((2, NSA_CMP_LEN, LANE), lambda i: (0, 0, 0)),
                  pl.BlockSpec((2, NSA_CMP_LEN, LANE, LANE), lambda i: (0, 0, 0, 0)),
                  pl.BlockSpec((2, LANE, LANE), lambda i: (0, 0, 0))],
        out_specs=[pl.BlockSpec((1, n_chunks, LANE), lambda i: (i, 0, 0))] * 2,
        out_shape=[jax.ShapeDtypeStruct((b, n_chunks, LANE), F32)] * 2,
        compiler_params=_cparams(("parallel",)),
        name="nsa_compress",
    )(zc, zc, pe_x, w1_bd, w2_bd)


def _topk_rank(v):
    n, t = v.shape
    blocks = [v[r:r + SUBLANE] for r in range(0, n, SUBLANE)]
    sub = _iota((SUBLANE, t), 0)
    rank = [jnp.zeros((SUBLANE, t), jnp.int32) for _ in blocks]
    for m in range(n):
        other = v[m:m + 1, :]
        mb, ms = divmod(m, SUBLANE)
        for r, blk in enumerate(blocks):
            if r < mb:
                ahead = (other > blk).astype(jnp.int32)
            elif r > mb:
                ahead = (other >= blk).astype(jnp.int32)
            else:
                ahead = jnp.where(sub > ms, (other >= blk).astype(jnp.int32), (other > blk).astype(jnp.int32))
            rank[r] = rank[r] + ahead
    return jnp.concatenate(rank, axis=0)


def _exp_weights(s, m):
    return jnp.exp((s - m).astype(BF16))


def _normalise_aug(o_aug, in_grp):
    den = pltpu.roll(o_aug, NSA_DH, 1)
    return jnp.where(in_grp, o_aug / jnp.where(den > 0, den, 1.0), 0.0)


def _nsa_kernel(q_ref, ksv_ref, kwv_ref, ck_ref, cv_ref, zm_ref, covt_ref, gexp_ref,
                x_ref, gmix_ref, op_ref, oa_ref, od_ref, wg_ref, bg_ref, wb_ref, wo_ref, o_ref,
                kaug_ref, vaug_ref, vwaug_ref, s_ref, mx_ref, acc_ref, y_ref, g3_ref):
    nb = q_ref.shape[0]
    tq = q_ref.shape[1]
    s_len = ksv_ref.shape[1]
    n_slc = s_len // NSA_SEL_LEN
    qi = pl.program_id(1)
    s0 = pl.multiple_of(qi * tq, tq)
    lane_grp = _iota((1, LANE), 1) // NSA_DH
    chains = [(bi, g) for bi in range(nb) for g in range(NSA_GROUPS)]
    n_chains = range(len(chains))

    @pl.when(qi == 0)
    def _():
        lane = _iota((s_len, LANE), 1)
        key_blk = _iota((s_len, LANE), 0) // NSA_SEL_LEN
        for bi in range(nb):
            k = ksv_ref[bi, :, 0:LANE].astype(F32)
            v = ksv_ref[bi, :, LANE:2 * LANE].astype(F32)
            vw = kwv_ref[bi, :, LANE:2 * LANE].astype(F32)
            for g in range(NSA_GROUPS):
                own = lane // NSA_DH == g
                onehot = jnp.where(lane - (1 - g) * NSA_DH == key_blk, 1.0, 0.0)
                kaug_ref[bi * NSA_GROUPS + g] = jnp.where(own, k, onehot).astype(BF16)
                vaug_ref[bi * NSA_GROUPS + g] = jnp.where(own, v, 1.0).astype(BF16)
                vwaug_ref[bi * NSA_GROUPS + g] = jnp.where(own, vw, 1.0).astype(BF16)

    x_res = x_ref[...].reshape(nb * tq, D_MODEL)
    h_mix = _rms(x_res, gmix_ref[...]).astype(BF16)
    branch_refs = (op_ref, oa_ref, od_ref)

    def merge_steps():
        for c0 in range(0, D_MODEL, MERGE_COLS):
            cols = slice(c0, c0 + MERGE_COLS)
            y_c = None
            for i in range(N_BRANCH):
                gate = jax.nn.sigmoid(jnp.dot(h_mix, wg_ref[i, :, cols], preferred_element_type=F32)
                                      + bg_ref[i, :, cols])
                if i < len(branch_refs):
                    br = jnp.dot(branch_refs[i][...].reshape(nb * tq, MIX_W), wb_ref[i, :, cols],
                                 preferred_element_type=F32)
                    y_c = gate * br if y_c is None else y_c + gate * br
                else:
                    g3_ref[:, cols] = gate
                yield
            y_ref[:, cols] = y_c

    merge = merge_steps()

    def fill(n):
        for _ in range(n):
            next(merge, None)

    t_col = s0 + _iota((tq, 1), 0)
    t_col2 = jnp.concatenate([t_col, t_col], axis=0)
    t_row = s0 + _iota((1, tq), 1)
    q = [q_ref[bi].astype(F32) * NSA_DH ** -0.5 for bi in range(nb)]
    ck_hi = [ck_ref[bi].astype(BF16) for bi in range(nb)]
    ck_lo = [(ck_ref[bi] - ck_hi[bi].astype(F32)).astype(BF16) for bi in range(nb)]
    cv = [cv_ref[bi] for bi in range(nb)]
    cmp_end = _iota((1, N_CMP_PAD), 1) * NSA_CMP_STRIDE + (NSA_CMP_LEN - 1)
    cmp_valid = cmp_end <= t_col2
    blk_t = _iota((n_slc, tq), 0)
    cur_t = t_row // NSA_SEL_LEN
    forced_t = (blk_t == 0) | (blk_t == cur_t) | (blk_t == cur_t - 1)
    future_t = blk_t > cur_t
    row_in_tile = jnp.concatenate([_iota((tq, tq), 0)] * NSA_HPG, axis=0)
    col_in_tile = _iota((NSA_HPG * tq, tq), 1)
    diag_ok = col_in_tile <= row_in_tile
    n_wt = NSA_WINDOW // tq + 1
    never = 2 * tq
    win_off = [pl.multiple_of(jnp.maximum(qi - (n_wt - 1 - w), 0) * tq, tq) for w in range(n_wt)]
    win_ok = [col_in_tile > row_in_tile + jnp.where(qi >= n_wt - 1, 0, never)]
    win_ok += [col_in_tile >= jnp.where(qi >= n_wt - 1 - w, 0, never) for w in range(1, n_wt - 1)]
    win_ok += [diag_ok]

    o_cmp = [[jnp.zeros((tq, LANE), F32) for _ in range(NSA_HPG)] for _ in range(nb)]
    o_slc = [[jnp.zeros((tq, LANE), F32) for _ in range(NSA_HPG)] for _ in range(nb)]
    o_win = [[jnp.zeros((tq, LANE), F32) for _ in range(NSA_HPG)] for _ in range(nb)]
    in_grp = [lane_grp == g for _, g in chains]
    q2 = [jnp.concatenate([jnp.where(in_grp[c], q[bi][:, j * LANE:(j + 1) * LANE], 0.0)
                           for j in range(NSA_HPG)], axis=0) for c, (bi, _) in enumerate(chains)]
    qb = [a.astype(BF16) for a in q2]
    p_c = [_masked_softmax(_dot_nt(qb[c], ck_hi[bi]) + _dot_nt(qb[c], ck_lo[bi]), cmp_valid)
           for c, (bi, _) in enumerate(chains)]
    fill(2)
    oc = [jnp.where(in_grp[c], _dot(p_c[c], cv[bi]), 0.0) for c, (bi, _) in enumerate(chains)]
    imp_c = [a[0:tq] + a[tq:2 * tq] for a in p_c]
    imp_hi = [a.astype(BF16) for a in imp_c]
    imp_lo = [(a - h.astype(F32)).astype(BF16) for a, h in zip(imp_c, imp_hi)]
    fill(2)
    imp_t = [(_dot_nt(covt_ref[...], h) + _dot_nt(covt_ref[...], l_))[0:n_slc]
             for h, l_ in zip(imp_hi, imp_lo)]
    imp_t = [jnp.where(forced_t, jnp.inf, jnp.where(future_t, -jnp.inf, a)) for a in imp_t]
    fill(2)
    s_w = [[jnp.where(ok, _dot_nt(qb[c], kwv_ref[bi, pl.ds(off, tq), 0:LANE]), -jnp.inf)
            for ok, off in zip(win_ok, win_off)] for c, (bi, _) in enumerate(chains)]
    fill(3)
    qa = []
    for c, (_, g) in enumerate(chains):
        bias_t = jnp.where(_topk_rank(imp_t[c]) < NSA_N_SEL, 0.0, NEG_BIG)
        lo = (1 - g) * NSA_DH
        rows = ([jnp.zeros((lo, tq), F32)] if lo else []) + [bias_t, jnp.zeros((LANE - lo - n_slc, tq), F32)]
        bias = jnp.concatenate(rows, axis=0).T
        qa.append(jnp.where(in_grp[c], q2[c], jnp.concatenate([bias] * NSA_HPG, axis=0)).astype(BF16))
    m_w = []
    for c in n_chains:
        m_c = s_w[c][0]
        for s_t in s_w[c][1:]:
            m_c = jnp.maximum(m_c, s_t)
        m_w.append(jnp.maximum(jnp.max(m_c, axis=-1, keepdims=True), F32_LOWEST))
    fill(3)
    ow = [jnp.zeros((NSA_HPG * tq, LANE), F32) for _ in n_chains]
    for w, off in enumerate(win_off):
        for c in n_chains:
            ow[c] = ow[c] + jnp.dot(_exp_weights(s_w[c][w], m_w[c]), vwaug_ref[c, pl.ds(off, tq), :],
                                    preferred_element_type=F32)
        fill(1)
    for _ in merge:
        pass
    for c, (bi, _) in enumerate(chains):
        ow_c = _normalise_aug(ow[c], in_grp[c])
        for j in range(NSA_HPG):
            rows_j = slice(j * tq, (j + 1) * tq)
            o_cmp[bi][j] = o_cmp[bi][j] + oc[c][rows_j]
            o_win[bi][j] = o_win[bi][j] + ow_c[rows_j]

    half_max = lambda s_t: jnp.maximum(s_t[:, 0:LANE], s_t[:, LANE:2 * LANE])
    mx_ref[...] = jnp.full(mx_ref.shape, NEG_BIG, F32)
    groups = n_chains

    def score_tiles(tiles):
        offs = [pl.multiple_of(kt * tq, tq) for kt in tiles]
        s_new = [[_dot_nt(qa[g], kaug_ref[g, pl.ds(off, tq), :]) for g in groups] for off in offs]
        for kt, s_kt in zip(tiles, s_new):
            for g in groups:
                s_ref[g, kt] = s_kt[g]
        for g in groups:
            m_new = half_max(s_new[0][g])
            for s_kt in s_new[1:]:
                m_new = jnp.maximum(m_new, half_max(s_kt[g]))
            mx_ref[g] = jnp.maximum(mx_ref[g], m_new)

    def score_pair(p, carry):
        score_tiles([2 * p, 2 * p + 1])
        return carry

    lax.fori_loop(0, qi // 2, score_pair, 0)

    @pl.when(qi % 2 == 1)
    def _():
        score_tiles([qi - 1])

    m_s = []
    for g in groups:
        s_d = jnp.where(diag_ok, _dot_nt(qa[g], kaug_ref[g, pl.ds(s0, tq), :]), NEG_BIG)
        s_ref[g, qi] = s_d
        m_g = jnp.max(jnp.maximum(mx_ref[g], half_max(s_d)), axis=-1, keepdims=True)
        m_s.append(jnp.maximum(m_g, F32_LOWEST))
    acc_ref[...] = jnp.zeros(acc_ref.shape, F32)

    def attend_tiles(tiles):
        offs = [pl.multiple_of(kt * tq, tq) for kt in tiles]
        pv = [[jnp.dot(_exp_weights(s_ref[g, kt], m_s[g]), vaug_ref[g, pl.ds(off, tq), :],
                       preferred_element_type=F32) for g in groups] for kt, off in zip(tiles, offs)]
        for g in groups:
            upd = pv[0][g]
            for pv_kt in pv[1:]:
                upd = upd + pv_kt[g]
            acc_ref[g] += upd

    def attend_pair(p, carry):
        attend_tiles([2 * p, 2 * p + 1])
        return carry

    lax.fori_loop(0, (qi + 1) // 2, attend_pair, 0)

    @pl.when(qi % 2 == 0)
    def _():
        attend_tiles([qi])

    for c, (bi, _) in enumerate(chains):
        os_ = _normalise_aug(acc_ref[c], in_grp[c])
        for j in range(NSA_HPG):
            o_slc[bi][j] = o_slc[bi][j] + os_[j * tq:(j + 1) * tq]

    o_nsa = []
    for bi in range(nb):
        gates = _split2_dot(jax.nn.sigmoid(zm_ref[bi]), gexp_ref[...])
        o_nsa.append((gates[:, 0:MIX_W] * jnp.concatenate(o_cmp[bi], axis=1)
                      + gates[:, MIX_W:2 * MIX_W] * jnp.concatenate(o_slc[bi], axis=1)
                      + gates[:, 2 * MIX_W:3 * MIX_W] * jnp.concatenate(o_win[bi], axis=1)).astype(BF16))
    o_nsa = jnp.concatenate(o_nsa, axis=0)
    y = [(y_ref[:, c0:c0 + MERGE_COLS] + g3_ref[:, c0:c0 + MERGE_COLS]
          * jnp.dot(o_nsa, wb_ref[N_BRANCH - 1, :, c0:c0 + MERGE_COLS], preferred_element_type=F32)).astype(BF16)
         for c0 in range(0, D_MODEL, MERGE_COLS)]
    out = x_res + jnp.dot(jnp.concatenate(y, axis=1), wo_ref[...], preferred_element_type=F32)
    o_ref[...] = out.reshape(nb, tq, D_MODEL)


def _nsa_merge(zn, zm, ck, cv, covt, gexp, x3, g, o_pool, o_gla, o_gdn, wg, bg, wb, wo):
    b, s, _ = zn.shape
    tq = TQ_NSA
    nb = NSA_SEQS if b % NSA_SEQS == 0 else 1
    n_chains = nb * NSA_GROUPS
    tok = lambda n: pl.BlockSpec((nb, tq, n), lambda i, j: (i, j, 0))
    whole = lambda shape: pl.BlockSpec(shape, lambda i, j: (0,) * len(shape), pipeline_mode=pl.Buffered(1))
    return pl.pallas_call(
        _nsa_kernel,
        grid=(b // nb, s // tq),
        in_specs=[tok(256),
                  pl.BlockSpec((nb, s, 256), lambda i, j: (i, 0, 1)),
                  pl.BlockSpec((nb, s, 256), lambda i, j: (i, 0, 2)),
                  pl.BlockSpec((nb, N_CMP_PAD, LANE), lambda i, j: (i, 0, 0)),
                  pl.BlockSpec((nb, N_CMP_PAD, LANE), lambda i, j: (i, 0, 0)),
                  tok(ZM_W),
                  pl.BlockSpec((N_SLC_PAD, N_CMP_PAD), lambda i, j: (0, 0)),
                  pl.BlockSpec((ZM_W, 3 * MIX_W), lambda i, j: (0, 0)),
                  tok(D_MODEL),
                  pl.BlockSpec((1, D_MODEL), lambda i, j: (0, 0)),
                  tok(MIX_W), tok(MIX_W), tok(MIX_W),
                  whole((N_BRANCH, D_MODEL, D_MODEL)),
                  pl.BlockSpec((N_BRANCH, 1, D_MODEL), lambda i, j: (0, 0, 0)),
                  whole((N_BRANCH, MIX_W, D_MODEL)),
                  whole((D_MODEL, D_MODEL))],
        out_specs=tok(D_MODEL),
        out_shape=jax.ShapeDtypeStruct((b, s, D_MODEL), F32),
        scratch_shapes=[pltpu.VMEM((n_chains, s, LANE), BF16)] * 3
                       + [pltpu.VMEM((n_chains, s // tq, NSA_HPG * tq, tq), F32),
                          pltpu.VMEM((n_chains, NSA_HPG * tq, LANE), F32),
                          pltpu.VMEM((n_chains, NSA_HPG * tq, LANE), F32),
                          pltpu.VMEM((nb * tq, D_MODEL), F32),
                          pltpu.VMEM((nb * tq, D_MODEL), F32)],
        compiler_params=_cparams(("parallel", "arbitrary")),
        name="nsa_merge",
    )(zn, zn, zn, ck, cv, zm, covt, gexp, x3, g, o_pool, o_gla, o_gdn, wg, bg, wb, wo)


def _memkv_kernel(m_ref, g_ref, w_ref, o_ref):
    o_ref[0] = _dot(_rms(m_ref[0], g_ref[...]), w_ref[...]).astype(o_ref.dtype)


def _memkv(mem, g, w):
    b, m, _ = mem.shape
    n = 2 * X_HEADS * X_DH
    return pl.pallas_call(
        _memkv_kernel,
        grid=(b,),
        in_specs=[pl.BlockSpec((1, m, D_MODEL), lambda i: (i, 0, 0)),
                  pl.BlockSpec((1, D_MODEL), lambda i: (0, 0)),
                  pl.BlockSpec((D_MODEL, n), lambda i: (0, 0))],
        out_specs=pl.BlockSpec((1, m, n), lambda i: (i, 0, 0)),
        out_shape=jax.ShapeDtypeStruct((b, m, n), BF16),
        compiler_params=_cparams(("parallel",)),
        name="mem_kv",
    )(mem, g, w)


def _cross_kernel(x_ref, g_ref, wq_ref, kv_ref, wo_ref, o_ref):
    n_k = X_HEADS * X_DH
    subs = [slice(i * CROSS_SUB, (i + 1) * CROSS_SUB) for i in range(x_ref.shape[1] // CROSS_SUB)]
    x = [x_ref[0, r, :] for r in subs]
    q = [_dot(_rms(a, g_ref[...]), wq_ref[...]) * X_DH ** -0.5 for a in x]
    pairs = [(i, h) for h in range(X_HEADS) for i in range(len(subs))]
    head = lambda h: slice(h * X_DH, (h + 1) * X_DH)
    sc = [_dot_nt(q[i][:, head(h)], kv_ref[0, :, head(h)]) for i, h in pairs]
    e = [jnp.exp(s - jnp.max(s, axis=-1, keepdims=True)) for s in sc]
    o = [_dot(e_ih, kv_ref[0, :, n_k + h * X_DH:n_k + (h + 1) * X_DH]) / jnp.sum(e_ih, axis=-1, keepdims=True)
         for e_ih, (i, h) in zip(e, pairs)]
    for i, r in enumerate(subs):
        o_i = jnp.concatenate([o[pairs.index((i, h))] for h in range(X_HEADS)], axis=1)
        o_ref[0, r, :] = x[i] + _dot(o_i, wo_ref[...])


def _cross(x3, g, wq, kv, wo):
    b, s, _ = x3.shape
    tm = TM_CROSS
    m = kv.shape[1]
    n_k = X_HEADS * X_DH
    return pl.pallas_call(
        _cross_kernel,
        grid=(b, s // tm),
        in_specs=[pl.BlockSpec((1, tm, D_MODEL), lambda i, j: (i, j, 0)),
                  pl.BlockSpec((1, D_MODEL), lambda i, j: (0, 0)),
                  pl.BlockSpec((D_MODEL, n_k), lambda i, j: (0, 0)),
                  pl.BlockSpec((1, m, 2 * n_k), lambda i, j: (i, 0, 0)),
                  pl.BlockSpec((n_k, D_MODEL), lambda i, j: (0, 0))],
        out_specs=pl.BlockSpec((1, tm, D_MODEL), lambda i, j: (i, j, 0)),
        out_shape=jax.ShapeDtypeStruct((b, s, D_MODEL), F32),
        compiler_params=_cparams(("parallel", "parallel")),
        name="cross_attn",
    )(x3, g, wq, kv, wo)


def _ffn_kernel(x_ref, g_ref, wup_ref, cw_ref, cb_ref, wd_ref, gf_ref, o_ref, tail_ref, act_ref, *, final):
    @pl.when(pl.program_id(1) == 0)
    def _():
        tail_ref[...] = jnp.zeros_like(tail_ref)

    ts = x_ref.shape[1]
    x = x_ref[0]
    hn = _rms(x, g_ref[...]).astype(BF16)
    for c in range(D_FF // FF_CHUNK):
        cols = slice(c * FF_CHUNK, (c + 1) * FF_CHUNK)
        gcols = slice(D_FF + c * FF_CHUNK, D_FF + (c + 1) * FF_CHUNK)
        u = jnp.dot(hn, wup_ref[:, cols], preferred_element_type=F32)
        v = jnp.dot(hn, wup_ref[:, gcols], preferred_element_type=F32)
        ext = jnp.concatenate([tail_ref[:, cols], u], axis=0)
        tail_ref[:, cols] = u[ts - SUBLANE:, :]
        cw = cw_ref[:, cols]
        y = (cw[2:3] * u + cw[1:2] * pltpu.roll(ext, 1, 0)[SUBLANE:]
             + cw[0:1] * pltpu.roll(ext, 2, 0)[SUBLANE:] + cb_ref[:, cols])
        act_ref[:, cols] = (jax.nn.gelu(y) * v).astype(BF16)
    out = x + jnp.dot(act_ref[...], wd_ref[...], preferred_element_type=F32)
    if final:
        out = _rms(out, gf_ref[...])
    o_ref[0] = out


def _ffn(x3, g, wup, cw, cb, wd, gf, final):
    b, s, _ = x3.shape
    ts = TS_FFN
    return pl.pallas_call(
        functools.partial(_ffn_kernel, final=final),
        grid=(b, s // ts),
        in_specs=[pl.BlockSpec((1, ts, D_MODEL), lambda i, j: (i, j, 0)),
                  pl.BlockSpec((1, D_MODEL), lambda i, j: (0, 0)),
                  pl.BlockSpec((D_MODEL, 2 * D_FF), lambda i, j: (0, 0), pipeline_mode=pl.Buffered(1)),
                  pl.BlockSpec((FFN_CONV, D_FF), lambda i, j: (0, 0)),
                  pl.BlockSpec((1, D_FF), lambda i, j: (0, 0)),
                  pl.BlockSpec((D_FF, D_MODEL), lambda i, j: (0, 0), pipeline_mode=pl.Buffered(1)),
                  pl.BlockSpec((1, D_MODEL), lambda i, j: (0, 0))],
        out_specs=pl.BlockSpec((1, ts, D_MODEL), lambda i, j: (i, j, 0)),
        out_shape=jax.ShapeDtypeStruct((b, s, D_MODEL), F32),
        scratch_shapes=[pltpu.VMEM((SUBLANE, D_FF), F32), pltpu.VMEM((ts, D_FF), BF16)],
        compiler_params=_cparams(("parallel", "arbitrary")),
        name="conv_ffn",
    )(x3, g, wup, cw, cb, wd, gf)


def _inproj_columns():
    starts = np.concatenate([[0], np.cumsum(IN_SPLITS)])
    (p_in, a_q, a_k, a_v, a_r, a_lr, d_q, d_k, d_v, d_b, d_a, d_g,
     n_q, n_kc, n_vc, n_ks, n_vs, n_kw, n_vw, n_g) = [np.arange(starts[i], starts[i + 1])
                                                      for i in range(len(IN_SPLITS))]
    n_q = n_q.reshape(NSA_GROUPS, NSA_HPG, NSA_DH).transpose(1, 0, 2).reshape(-1)
    misc = np.full((ZM_W,), N_IN)
    misc[MISC_LR:MISC_LR + GLA_LOWRANK] = a_lr
    misc[MISC_B:MISC_B + GDN_HEADS] = d_b
    misc[MISC_A:MISC_A + GDN_HEADS] = d_a
    misc[MISC_G:MISC_G + 3 * NSA_HEADS] = n_g
    cols = np.concatenate([p_in, a_q, a_k, a_v, a_r, d_q, d_k, d_v, d_g,
                           n_q, n_ks, n_vs, n_kw, n_vw, n_kc, n_vc, misc])
    assert cols.shape[0] == Z_W
    return cols


def _head_expand(offset, n_heads, width):
    e = np.zeros((ZM_W, n_heads * width), np.float32)
    for h in range(n_heads):
        e[offset + h, h * width:(h + 1) * width] = 1.0
    return e


def _nsa_constants(s):
    n_cmp = s // NSA_CMP_STRIDE - NSA_CMP_LEN // NSA_CMP_STRIDE + 1
    n_slc = s // NSA_SEL_LEN
    c_start = np.arange(n_cmp) * NSA_CMP_STRIDE
    s_start = np.arange(n_slc) * NSA_SEL_LEN
    cover = np.zeros((N_CMP_PAD, N_SLC_PAD), np.float32)
    cover[:n_cmp, :n_slc] = ((c_start[:, None] <= s_start[None, :] + NSA_SEL_LEN - 1)
                             & (c_start[:, None] + NSA_CMP_LEN - 1 >= s_start[None, :]))
    gexp = np.zeros((ZM_W, 3, MIX_W), np.float32)
    for g in range(NSA_GROUPS):
        for j in range(NSA_HPG):
            slot = j * NSA_GROUPS + g
            for c in range(3):
                gexp[MISC_G + (g * NSA_HPG + j) * 3 + c, c, slot * NSA_DH:(slot + 1) * NSA_DH] = 1.0
    return jnp.asarray(cover.T, dtype=BF16), jnp.asarray(gexp.reshape(ZM_W, 3 * MIX_W), dtype=BF16)


def _block_diag(blocks):
    n, a, b = blocks.shape
    return jnp.einsum('gh,gab->gahb', jnp.eye(n, dtype=blocks.dtype), blocks).reshape(n * a, n * b)


def kernel(x, mem, g_mix, w_in, pool_w, pool_scale, gla_w_lr, gla_b_lr, gla_g_norm, gdn_conv, gdn_a_log,
           gdn_dt_bias, gdn_g_norm, nsa_pe, nsa_cmp_w1, nsa_cmp_w2, w_branch, w_gate, b_gate, w_out, g_cross,
           g_mem, w_xq, w_mem_kv, w_xo, g_ffn, w_up, ffn_conv, ffn_conv_b, w_down, g_final):
    b, s, d = x.shape
    depth = w_in.shape[0]
    t = b * s
    cols = _inproj_columns()
    covt, gexp = _nsa_constants(s)
    eba = jnp.asarray(np.concatenate([_head_expand(MISC_B, GDN_HEADS, GDN_DH),
                                      _head_expand(MISC_A, GDN_HEADS, GDN_DH)], axis=1), dtype=BF16)
    ones_bd = _block_diag(jnp.ones((GDN_HEADS, GDN_DH, GDN_DH), BF16))
    nsa_rows = np.arange(MIX_W).reshape(NSA_GROUPS, NSA_HPG, NSA_DH).transpose(1, 0, 2).reshape(-1)
    row = lambda v: v.reshape(1, -1).astype(F32)
    misc_a = lambda v: jnp.zeros((1, ZM_W), F32).at[0, MISC_A:MISC_A + GDN_HEADS].set(v)

    x2 = x.reshape(t, d)
    for l in range(depth):
        w_in_r = jnp.concatenate([w_in[l], jnp.zeros((d, 1), F32)], axis=1)[:, cols].astype(BF16)
        wlr = jnp.zeros((ZM_W, GLA_HEADS * GLA_DK), F32).at[MISC_LR:MISC_LR + GLA_LOWRANK].set(gla_w_lr[l])
        wlr_hi = wlr.astype(BF16)
        wlr2 = jnp.stack([wlr_hi, (wlr - wlr_hi.astype(F32)).astype(BF16)])
        o_pool, zn, zc, zm, o_gla, o_gdn = _mixers(
            x2.reshape(b, s, d), row(g_mix[l]), w_in_r, _block_diag(pool_w[l]).astype(BF16), row(pool_scale[l]),
            wlr2, row(gla_b_lr[l]), row(jnp.tile(gla_g_norm[l], GLA_HEADS)),
            gdn_conv[l], eba, misc_a(gdn_a_log[l]), misc_a(gdn_dt_bias[l]),
            row(jnp.tile(gdn_g_norm[l], GDN_HEADS)), ones_bd)

        pe_x = jnp.tile(nsa_pe[l], (1, 1, NSA_GROUPS))
        w1 = nsa_cmp_w1[l].reshape(2, NSA_CMP_LEN, NSA_DH, NSA_DH)
        eye_g = jnp.eye(NSA_GROUPS, dtype=F32)
        w1_bd = jnp.einsum('gh,kpde->kpgdhe', eye_g, w1).reshape(2, NSA_CMP_LEN, LANE, LANE).astype(BF16)
        w2_bd = jnp.einsum('gh,kde->kgdhe', eye_g, nsa_cmp_w2[l]).reshape(2, LANE, LANE).astype(BF16)
        ck, cv = _nsa_compress(zc, pe_x, w1_bd, w2_bd)
        wb = jnp.concatenate([w_branch[l, :3], w_branch[l, 3][nsa_rows][None]], axis=0).astype(BF16)
        x2 = _nsa_merge(zn, zm, ck, cv, covt, gexp, x2.reshape(b, s, d), row(g_mix[l]), o_pool, o_gla, o_gdn,
                        w_gate[l].astype(BF16), b_gate[l].reshape(N_BRANCH, 1, d), wb,
                        w_out[l].astype(BF16)).reshape(t, d)

        kv = _memkv(mem, row(g_mem[l]), w_mem_kv[l].astype(BF16))
        x3 = _cross(x2.reshape(b, s, d), row(g_cross[l]), w_xq[l].astype(BF16), kv, w_xo[l].astype(BF16))

        x3 = _ffn(x3, row(g_ffn[l]), w_up[l].astype(BF16), ffn_conv[l], row(ffn_conv_b[l]),
                  w_down[l].astype(BF16), row(g_final), final=(l == depth - 1))
        x2 = x3.reshape(t, d)
    return x2.reshape(b, s, d)
```

```python
import functools

import numpy as np
import jax
import jax.numpy as jnp
from jax import lax
from jax.experimental import pallas as pl
from jax.experimental.pallas import tpu as pltpu

F32 = jnp.float32
BF16 = jnp.bfloat16
HIGHEST = lax.Precision.HIGHEST

D_MODEL = 1024
MIX_W = 256
POOL_WINDOWS = (2, 4, 8, 16)
POOL_GW = 64
GLA_HEADS = 4
GLA_DK = 32
GLA_DV = 64
GLA_LOWRANK = 16
GLA_GATE_NORM = 16.0
CHUNK = 64
GDN_HEADS = 4
GDN_DH = 64
GDN_CONV = 4
NSA_HEADS = 4
NSA_GROUPS = 2
NSA_HPG = 2
NSA_DH = 64
NSA_KV = 128
NSA_CMP_LEN = 32
NSA_CMP_STRIDE = 16
NSA_SEL_LEN = 64
NSA_N_SEL = 16
NSA_WINDOW = 512
X_HEADS = 4
X_DH = 128
D_FF = 2816
FFN_CONV = 3
EPS = 1e-6
N_BRANCH = 4

IN_SPLITS = (MIX_W,
             128, 128, 256, 256, GLA_LOWRANK,
             MIX_W, MIX_W, MIX_W, GDN_HEADS, GDN_HEADS, MIX_W,
             256, NSA_KV, NSA_KV, NSA_KV, NSA_KV, NSA_KV, NSA_KV, 3 * NSA_HEADS)
N_IN = sum(IN_SPLITS)

MISC_LR = 0
MISC_B = 16
MISC_A = 20
MISC_G = 24
LANE = 128
SUBLANE = 8

Z_WIDTHS = (256, 768, 1024, 768, 256, 128)
Z_DTYPES = (F32, BF16, BF16, BF16, F32, F32)
ZP_W, ZG_W, ZD_W, ZN_W, ZC_W, ZM_W = Z_WIDTHS
Z_W = sum(Z_WIDTHS)
NEG_BIG = -1e30
F32_LOWEST = float(np.finfo(np.float32).min)

TM_COMB = 512
TM_CROSS = 1024
CROSS_SUB = 512
TS_FFN = 1024
FF_CHUNK = 256
TQ_NSA = 256
NSA_SEQS = 2
REC_SEQS = 8
REC_LAG = 3
GLA_START = 6
N_SLC_PAD = 128
N_CMP_PAD = 128

VMEM_LIMIT = 56 * 1024 * 1024


def _cparams(sem):
    return pltpu.CompilerParams(dimension_semantics=sem, vmem_limit_bytes=VMEM_LIMIT)


def _rms(x, g):
    return x * lax.rsqrt(jnp.mean(x * x, axis=-1, keepdims=True) + EPS) * g


def _dot(a, b):
    return jnp.dot(a.astype(BF16), b.astype(BF16), preferred_element_type=F32)


def _dot_nt(a, b):
    return lax.dot_general(a.astype(BF16), b.astype(BF16), (((1,), (1,)), ((), ())),
                           preferred_element_type=F32)


def _dot_tn(a, b):
    return lax.dot_general(a.astype(BF16), b.astype(BF16), (((0,), (0,)), ((), ())),
                           preferred_element_type=F32)


def _dot_hi(a, b):
    return jnp.dot(a, b, precision=HIGHEST, preferred_element_type=F32)


def _dot_nt_hi(a, b):
    return lax.dot_general(a, b, (((1,), (1,)), ((), ())), precision=HIGHEST,
                           preferred_element_type=F32)


def _split2_dot(a, b):
    hi = a.astype(BF16)
    lo = (a - hi.astype(F32)).astype(BF16)
    return jnp.dot(hi, b, preferred_element_type=F32) + jnp.dot(lo, b, preferred_element_type=F32)


def _split3_lhs_dot(a, b):
    hi = a.astype(BF16)
    r1 = a - hi.astype(F32)
    mid = r1.astype(BF16)
    lo = (r1 - mid.astype(F32)).astype(BF16)
    return ((jnp.dot(hi, b, preferred_element_type=F32) + jnp.dot(mid, b, preferred_element_type=F32))
            + jnp.dot(lo, b, preferred_element_type=F32))


def _dot3(a, w_hi, w_lo):
    a_hi = a.astype(BF16)
    a_lo = (a - a_hi.astype(F32)).astype(BF16)
    return (jnp.dot(a_hi, w_hi, preferred_element_type=F32)
            + (jnp.dot(a_lo, w_hi, preferred_element_type=F32) + jnp.dot(a_hi, w_lo, preferred_element_type=F32)))


def _iota(shape, axis):
    return lax.broadcasted_iota(jnp.int32, shape, axis)


def _block_mask(rows, cols, rb, cb):
    return (_iota((rows, cols), 0) // rb) == (_iota((rows, cols), 1) // cb)


def _shift_rows(x, k):
    t = _iota(x.shape, 0)
    return jnp.where(t >= k, pltpu.roll(x, k, 0), 0.0)


def _cumsum_rows(x):
    k = 1
    while k < x.shape[0]:
        x = x + _shift_rows(x, k)
        k *= 2
    return x


def _softplus(x):
    return jnp.maximum(x, 0.0) + jnp.log1p(jnp.exp(-jnp.abs(x)))


def _log_sigmoid(x):
    return -_softplus(-x)


def _silu(x):
    return x * jax.nn.sigmoid(x)


def _masked_softmax(s, mask):
    s = jnp.where(mask, s, -jnp.inf)
    m = jnp.maximum(jnp.max(s, axis=-1, keepdims=True), F32_LOWEST)
    e = jnp.exp(s - m)
    den = jnp.sum(e, axis=-1, keepdims=True)
    return e / jnp.where(den > 0, den, 1.0)


def _tile4(x):
    return jnp.concatenate([x, x, x, x], axis=0)


POOL_HALO = 16


def _gla_masks():
    return (_block_mask(4 * CHUNK, 128, CHUNK, GLA_DK),
            _block_mask(4 * CHUNK, 256, CHUNK, GLA_DV),
            _block_mask(256, 128, GLA_DV, GLA_DK),
            (_iota((CHUNK, 256), 1) % CHUNK) <= _iota((CHUNK, 256), 0))


def _seqs(f, *lists):
    return [f(*args) for args in zip(*lists)]


def _run_staggered(stage_gens, starts, periods):
    live = list(range(len(stage_gens)))
    tick = 0
    while live:
        for i in list(live):
            if tick >= starts[i] and (tick - starts[i]) % periods[i] == 0:
                try:
                    next(stage_gens[i])
                except StopIteration:
                    live.remove(i)
        tick += 1


def _gla_chunk(masks, zqk, v, r, zm, wlr_ref, blr_ref, gn_ref, ones_ref, st_refs, out):
    mask_k, mask_v, mask_st, causal = masks
    q = [z[:, 0:128] * GLA_DK ** -0.5 for z in zqk]
    k = [z[:, 128:256] for z in zqk]
    pre = _seqs(lambda a: _dot3(a, wlr_ref[0], wlr_ref[1]), zm)
    yield
    bc = _seqs(lambda a: _cumsum_rows(_log_sigmoid(a + blr_ref[...]) / GLA_GATE_NORM), pre)
    bl = [a[CHUNK - 1:CHUNK, :] for a in bc]
    q_e = _seqs(lambda a, c: a * jnp.exp(c), q, bc)
    k_e = _seqs(lambda a, c: a * jnp.exp(-c), k, bc)
    k_u = _seqs(lambda a, c, l: a * jnp.exp(l - c), k, bc, bl)
    st = [ref[...] for ref in st_refs]
    yield
    att = _seqs(lambda a, b: jnp.where(causal, _dot_nt(a, jnp.where(mask_k, _tile4(b), 0.0)), 0.0), q_e, k_e)
    inter = _seqs(_dot_nt, q_e, st)
    kv = _seqs(_dot_tn, v, k_u)
    yield
    o = _seqs(lambda a, b, c: _dot(a, _tile4(b.astype(BF16)) * ones_ref[...]) + c, att, v, inter)
    for ref, s_old, l, upd in zip(st_refs, st, bl, kv):
        ref[...] = s_old * jnp.exp(l) + jnp.where(mask_st, upd, 0.0)
    yield
    ms = _seqs(lambda a: _dot(a * a, ones_ref[...]) * (1.0 / GLA_DV), o)
    yield
    out[:] = _seqs(lambda a, m, g: a * lax.rsqrt(m + EPS) * gn_ref[...] * _silu(g), o, ms, r)


GDN_HALO = SUBLANE


def _gdn_masks():
    c4 = 4 * CHUNK
    col = _iota((CHUNK, c4), 1) % CHUNK
    row = _iota((CHUNK, c4), 0)
    return (_block_mask(c4, c4, CHUNK, CHUNK), col <= row, col < row, col == row)


def _gdn_chunk(masks, halo, zqkv, gate, zm, cw_ref, eba_ref, alog_ref, dtb_ref, gn_ref, ones_ref, st_refs, out):
    mask_bd, incl, strict, diag = masks
    cw = cw_ref[...]
    bd = lambda a: _tile4(a.astype(BF16)) * ones_ref[...]

    def conv_silu(h, cur):
        ext = jnp.concatenate([h, cur], axis=0)
        conv = (cw[3:4] * cur + cw[2:3] * pltpu.roll(ext, 1, 0)[GDN_HALO:]
                + cw[1:2] * pltpu.roll(ext, 2, 0)[GDN_HALO:] + cw[0:1] * pltpu.roll(ext, 3, 0)[GDN_HALO:])
        return _silu(conv)

    qkv = _seqs(conv_silu, halo, zqkv)
    yield
    v = [a[:, 512:768] for a in qkv]
    ssq = _seqs(lambda a: _dot(jnp.concatenate([a[:, 0:256] * a[:, 0:256], a[:, 256:512] * a[:, 256:512]],
                                               axis=0), ones_ref[...]), qkv)
    yield
    q = _seqs(lambda a, s: a[:, 0:256] * lax.rsqrt(s[0:CHUNK] + EPS) * GDN_DH ** -0.5, qkv, ssq)
    k = _seqs(lambda a, s: a[:, 256:512] * lax.rsqrt(s[CHUNK:2 * CHUNK] + EPS), qkv, ssq)
    beta = _seqs(lambda a: _dot(jax.nn.sigmoid(a), eba_ref[:, 0:MIX_W]), zm)
    yield
    gc = _seqs(lambda a: _split3_lhs_dot(_cumsum_rows(-jnp.exp(alog_ref[...]) * _softplus(a + dtb_ref[...])),
                                         eba_ref[:, MIX_W:2 * MIX_W]), zm)
    yield
    gl = [a[CHUNK - 1:CHUNK, :] for a in gc]
    g_row = _seqs(lambda a: jnp.sum(jnp.where(diag, a, 0.0), axis=0, keepdims=True), gc)
    yield
    decay =_seqs(lambda a, b: jnp.exp(jnp.where(incl, a - b, -jnp.inf)), gc, g_row)
    kb = _seqs(lambda a, b: a * b, k, beta)
    vb = _seqs(lambda a, b: a * b, v, beta)
    kq = _seqs(lambda a, b, c: _dot_nt(jnp.concatenate([a, b], axis=0), bd(c)), kb, q, k)
    yield
    n_mat =_seqs(lambda s, d: jnp.where(strict, s[0:CHUNK] * d, 0.0), kq, decay)
    a_qk = _seqs(lambda s, d: s[CHUNK:2 * CHUNK] * d, kq, decay)
    m = _seqs(lambda a: -a, n_mat)
    x = _seqs(lambda a: jnp.where(diag, 1.0, 0.0) + a, m)
    m = _seqs(lambda a: _dot(a, bd(a)), m)
    yield
    p = 2
    while p < CHUNK // 2:
        xm = _seqs(lambda a, b: _dot(jnp.concatenate([a, b], axis=0), bd(b)), x, m)
        x = _seqs(lambda a, r: a + r[0:CHUNK], x, xm)
        m = [r[CHUNK:2 * CHUNK] for r in xm]
        p *= 2
        yield
    x = _seqs(lambda a, b: a + _dot(a, bd(b)), x, m)
    yield
    uw = _seqs(lambda a, b, c, g: _dot(a, jnp.concatenate([bd(b), bd(c * jnp.exp(g))], axis=1)),
               x, vb, kb, gc)
    yield
    st = [ref[...] for ref in st_refs]
    ws = _seqs(lambda a, b, g, s: _dot(jnp.concatenate([a[:, 256:512], b * jnp.exp(g)], axis=0), s),
               uw, q, gc, st)
    yield
    v_new = _seqs(lambda a, r: a[:, 0:256] - r[0:CHUNK], uw, ws)
    o = _seqs(lambda r, qk, vn: r[CHUNK:2 * CHUNK] + _dot(qk, bd(vn)), ws, a_qk, v_new)
    upd = _seqs(lambda a, g, l, vn: _dot_tn(a * jnp.exp(l - g), vn), k, gc, gl, v_new)
    yield
    for ref, s_old, l, u in zip(st_refs, st, gl, upd):
        ref[...] = s_old * jnp.exp(l) + jnp.where(mask_bd, u, 0.0)
    ms = _seqs(lambda a: _dot(a * a, ones_ref[...]) * (1.0 / GDN_DH), o)
    yield
    out[:] = _seqs(lambda a, m_, g: a * lax.rsqrt(m_ + EPS) * gn_ref[...] * _silu(g), o, ms, gate)


def _pool_diff(u, tail, pos0):
    ext = jnp.concatenate([tail, u], axis=0)
    s2 = ext + pltpu.roll(ext, 1, 0)
    s4 = s2 + pltpu.roll(s2, 2, 0)
    s8 = s4 + pltpu.roll(s4, 4, 0)
    s16 = s8 + pltpu.roll(s8, 8, 0)
    grp = _iota(u.shape, 1) // POOL_GW
    win = jnp.where(grp == 0, s2[POOL_HALO:], jnp.where(grp == 1, s4[POOL_HALO:],
                                                        jnp.where(grp == 2, s8[POOL_HALO:], s16[POOL_HALO:])))
    width = jnp.where(grp == 0, POOL_WINDOWS[0],
                      jnp.where(grp == 1, POOL_WINDOWS[1],
                                jnp.where(grp == 2, POOL_WINDOWS[2], POOL_WINDOWS[3])))
    cnt = jnp.minimum(pos0 + _iota(u.shape, 0) + 1, width).astype(F32)
    return win / cnt - u


def _mix_kernel(x_ref, g_ref, w_ref, pw_ref, ps_ref, wlr_ref, blr_ref, gng_ref, cw_ref, eba_ref, alog_ref, dtb_ref,
                gnd_ref, ones_ref, op_ref, zn_ref, zc_ref, zm_ref, og_ref, od_ref, ptail_ref, halo_ref, *st_refs):
    nb = x_ref.shape[0]
    chunk = pl.program_id(1)

    @pl.when(chunk == 0)
    def _():
        ptail_ref[...] = jnp.zeros_like(ptail_ref)
        halo_ref[...] = jnp.zeros_like(halo_ref)
        for st_ref in st_refs:
            st_ref[...] = jnp.zeros_like(st_ref)

    h = _rms(x_ref[...].reshape(nb * CHUNK, D_MODEL), g_ref[...]).astype(BF16)
    proj = lambda lo, n: jnp.dot(h, w_ref[:, lo:lo + n], preferred_element_type=F32)
    seqs = lambda a: [a[i * CHUNK:(i + 1) * CHUNK] for i in range(nb)]
    off_g, off_d = ZP_W, ZP_W + ZG_W
    off_n = off_d + ZD_W
    off_c = off_n + ZN_W
    off_m = off_c + ZC_W
    zqkv = seqs(proj(off_d, 768))
    zm = seqs(proj(off_m, ZM_W))
    halo = [halo_ref[i] for i in range(nb)]
    for i in range(nb):
        halo_ref[i] = zqkv[i][CHUNK - GDN_HALO:, :]
        zm_ref[i] = zm[i]
    halves = [slice(0, nb // 2), slice(nb // 2, nb)] if nb > 1 else [slice(0, nb)]
    gla_qk, gla_v, gla_r = [], [], []
    gate_parts = [[] for _ in halves]

    def other_projections():
        gla_qk.extend(seqs(proj(off_g, 256)))
        yield
        gla_v.extend(seqs(proj(off_g + 256, 256)))
        yield
        u = seqs(proj(0, ZP_W))
        yield
        diff = [_pool_diff(u[i], ptail_ref[i], chunk * CHUNK) for i in range(nb)]
        for i in range(nb):
            ptail_ref[i] = u[i][CHUNK - POOL_HALO:, :]
        yield
        for lo in range(0, ZN_W, 256):
            zn = seqs(proj(off_n + lo, 256))
            for i in range(nb):
                zn_ref[i, :, lo:lo + 256] = zn[i].astype(zn_ref.dtype)
            yield
        zc = seqs(proj(off_c, ZC_W))
        for i in range(nb):
            zc_ref[i] = zc[i]
        yield
        gla_r.extend(seqs(proj(off_g + 512, 256)))
        yield
        gate = seqs(proj(off_d + 768, MIX_W))
        for hs, part in zip(halves, gate_parts):
            part.extend(gate[hs])
        yield
        o_pool = seqs(_dot(jnp.concatenate(diff, axis=0), pw_ref[...]) * ps_ref[...])
        for i in range(nb):
            op_ref[i] = o_pool[i].astype(op_ref.dtype)

    og = [None] * nb
    od_parts = [[None] * (hs.stop - hs.start) for hs in halves]
    gdn_masks = _gdn_masks()
    gens = [_gdn_chunk(gdn_masks, halo[hs], zqkv[hs], gate_part, zm[hs], cw_ref, eba_ref, alog_ref, dtb_ref,
                       gnd_ref, ones_ref, st_refs[nb:][hs], part)
            for hs, part, gate_part in zip(halves, od_parts, gate_parts)]
    gens.append(_gla_chunk(_gla_masks(), gla_qk, gla_v, gla_r, zm, wlr_ref, blr_ref, gng_ref, ones_ref,
                           st_refs[:nb], og))
    starts = [REC_LAG * i for i in range(len(halves))] + [GLA_START]
    gens.append(other_projections())
    _run_staggered(gens, starts + [0], [1] * len(gens))
    od = [o for part in od_parts for o in part]
    for i in range(nb):
        og_ref[i] = og[i].astype(og_ref.dtype)
        od_ref[i] = od[i].astype(od_ref.dtype)


def _mixers(x3, g, w, pool_w_bd, pool_scale, wlr2, blr, gn_gla, cw, eba, alog, dtb, gn_gdn, ones_bd):
    b, s, _ = x3.shape
    nb = REC_SEQS if b % REC_SEQS == 0 else 1
    const = lambda shape: pl.BlockSpec(shape, lambda i, c: (0,) * len(shape))
    tok = lambda n: pl.BlockSpec((nb, CHUNK, n), lambda i, c: (i, c, 0))
    outs = ((MIX_W, BF16), (ZN_W, BF16), (ZC_W, F32), (ZM_W, F32), (MIX_W, BF16), (MIX_W, BF16))
    return pl.pallas_call(
        _mix_kernel,
        grid=(b // nb, s // CHUNK),
        in_specs=[tok(D_MODEL), const((1, D_MODEL)), const((D_MODEL, Z_W)), const((MIX_W, MIX_W)), const((1, MIX_W)),
                  const((2, ZM_W, 128)), const((1, 128)), const((1, MIX_W)),
                  const((GDN_CONV, 768)), const((ZM_W, 2 * MIX_W)), const((1, ZM_W)), const((1, ZM_W)),
                  const((1, MIX_W)), const((MIX_W, MIX_W))],
        out_specs=[tok(n) for n, _ in outs],
        out_shape=[jax.ShapeDtypeStruct((b, s, n), dt) for n, dt in outs],
        scratch_shapes=([pltpu.VMEM((nb, POOL_HALO, MIX_W), F32), pltpu.VMEM((nb, GDN_HALO, 768), F32)]
                        + [pltpu.VMEM((256, 128), F32)] * nb + [pltpu.VMEM((256, 256), F32)] * nb),
        compiler_params=_cparams(("parallel", "arbitrary")),
        name="inproj_mixers",
    )(x3, g, w, pool_w_bd, pool_scale, wlr2, blr, gn_gla, cw, eba, alog, dtb, gn_gdn, ones_bd)


def _cmp_kernel(kc_ref, vc_ref, pe_ref, w1_ref, w2_ref, ck_ref, cv_ref):
    n_sub = NSA_CMP_LEN // NSA_CMP_STRIDE
    n_chunks = kc_ref.shape[1] // NSA_CMP_STRIDE
    for which, (src_ref, out_ref) in enumerate(((kc_ref, ck_ref), (vc_ref, cv_ref))):
        parts = [jnp.zeros((n_chunks, LANE), F32) for _ in range(n_sub)]
        for i in range(NSA_CMP_STRIDE):
            slab = src_ref[0, pl.ds(i, n_chunks, stride=NSA_CMP_STRIDE), :]
            for sub in range(n_sub):
                p = sub * NSA_CMP_STRIDE + i
                parts[sub] = parts[sub] + _dot(slab + pe_ref[which, p:p + 1, :], w1_ref[which, p])
        pre = parts[0] + pltpu.roll(parts[1], n_chunks - 1, 0)
        out_ref[0] = _dot(jax.nn.gelu(pre), w2_ref[which])


def _nsa_compress(zc, pe_x, w1_bd, w2_bd):
    b, s, _ = zc.shape
    n_chunks = s // NSA_CMP_STRIDE
    return pl.pallas_call(
        _cmp_kernel,
        grid=(b,),
        in_specs=[pl.BlockSpec((1, s, LANE), lambda i: (i, 0, 0)),
                  pl.BlockSpec((1, s, LANE), lambda i: (i, 0, 1)),
                  pl.BlockSpec((2, NSA_CMP_LEN, LANE), lambda i: (0, 0, 0)),
                  pl.BlockSpec((2, NSA_CMP_LEN, LANE, LANE), lambda i: (0, 0, 0, 0)),
                  pl.BlockSpec((2, LANE, LANE), lambda i: (0, 0, 0))],
        out_specs=[pl.BlockSpec((1, n_chunks, LANE), lambda i: (i, 0, 0))] * 2,
        out_shape=[jax.ShapeDtypeStruct((b, n_chunks, LANE), F32)] * 2,
        compiler_params=_cparams(("parallel",)),
        name="nsa_compress",
    )(zc, zc, pe_x, w1_bd, w2_bd)


def _topk_rank(v):
    n, t = v.shape
    blocks = [v[r:r + SUBLANE] for r in range(0, n, SUBLANE)]
    sub = _iota((SUBLANE, t), 0)
    rank = [jnp.zeros((SUBLANE, t), jnp.int32) for _ in blocks]
    for m in range(n):
        other = v[m:m + 1, :]
        mb, ms = divmod(m, SUBLANE)
        for r, blk in enumerate(blocks):
            if r < mb:
                ahead = (other > blk).astype(jnp.int32)
            elif r > mb:
                ahead = (other >= blk).astype(jnp.int32)
            else:
                ahead = jnp.where(sub > ms, (other >= blk).astype(jnp.int32), (other > blk).astype(jnp.int32))
            rank[r] = rank[r] + ahead
    return jnp.concatenate(rank, axis=0)


def _exp_weights(s, m):
    return jnp.exp((s - m).astype(BF16))


def _normalise_aug(o_aug, in_grp):
    den = pltpu.roll(o_aug, NSA_DH, 1)
    return jnp.where(in_grp, o_aug / jnp.where(den > 0, den, 1.0), 0.0)


def _nsa_kernel(q_ref, ksv_ref, kwv_ref, ck_ref, cv_ref, zm_ref, covt_ref, gexp_ref, o_ref,
                kaug_ref, vaug_ref, vwaug_ref, s_ref, mx_ref, acc_ref):
    nb = q_ref.shape[0]
    tq = q_ref.shape[1]
    s_len = ksv_ref.shape[1]
    n_slc = s_len // NSA_SEL_LEN
    qi = pl.program_id(1)
    s0 = pl.multiple_of(qi * tq, tq)
    lane_grp = _iota((1, LANE), 1) // NSA_DH
    chains = [(bi, g) for bi in range(nb) for g in range(NSA_GROUPS)]
    n_chains = range(len(chains))

    @pl.when(qi == 0)
    def _():
        lane = _iota((s_len, LANE), 1)
        key_blk = _iota((s_len, LANE), 0) // NSA_SEL_LEN
        for bi in range(nb):
            k = ksv_ref[bi, :, 0:LANE].astype(F32)
            v = ksv_ref[bi, :, LANE:2 * LANE].astype(F32)
            vw = kwv_ref[bi, :, LANE:2 * LANE].astype(F32)
            for g in range(NSA_GROUPS):
                own = lane // NSA_DH == g
                onehot = jnp.where(lane - (1 - g) * NSA_DH == key_blk, 1.0, 0.0)
                kaug_ref[bi * NSA_GROUPS + g] = jnp.where(own, k, onehot).astype(BF16)
                vaug_ref[bi * NSA_GROUPS + g] = jnp.where(own, v, 1.0).astype(BF16)
                vwaug_ref[bi * NSA_GROUPS + g] = jnp.where(own, vw, 1.0).astype(BF16)

    t_col = s0 + _iota((tq, 1), 0)
    t_col2 = jnp.concatenate([t_col, t_col], axis=0)
    t_row = s0 + _iota((1, tq), 1)
    q = [q_ref[bi].astype(F32) * NSA_DH ** -0.5 for bi in range(nb)]
    ck_hi = [ck_ref[bi].astype(BF16) for bi in range(nb)]
    ck_lo = [(ck_ref[bi] - ck_hi[bi].astype(F32)).astype(BF16) for bi in range(nb)]
    cv = [cv_ref[bi] for bi in range(nb)]
    cmp_end = _iota((1, N_CMP_PAD), 1) * NSA_CMP_STRIDE + (NSA_CMP_LEN - 1)
    cmp_valid = cmp_end <= t_col2
    blk_t = _iota((n_slc, tq), 0)
    cur_t = t_row // NSA_SEL_LEN
    forced_t = (blk_t == 0) | (blk_t == cur_t) | (blk_t == cur_t - 1)
    future_t = blk_t > cur_t
    row_in_tile = jnp.concatenate([_iota((tq, tq), 0)] * NSA_HPG, axis=0)
    col_in_tile = _iota((NSA_HPG * tq, tq), 1)
    diag_ok = col_in_tile <= row_in_tile
    n_wt = NSA_WINDOW // tq + 1
    never = 2 * tq
    win_off = [pl.multiple_of(jnp.maximum(qi - (n_wt - 1 - w), 0) * tq, tq) for w in range(n_wt)]
    win_ok = [col_in_tile > row_in_tile + jnp.where(qi >= n_wt - 1, 0, never)]
    win_ok += [col_in_tile >= jnp.where(qi >= n_wt - 1 - w, 0, never) for w in range(1, n_wt - 1)]
    win_ok += [diag_ok]

    o_cmp = [[jnp.zeros((tq, LANE), F32) for _ in range(NSA_HPG)] for _ in range(nb)]
    o_slc = [[jnp.zeros((tq, LANE), F32) for _ in range(NSA_HPG)] for _ in range(nb)]
    o_win = [[jnp.zeros((tq, LANE), F32) for _ in range(NSA_HPG)] for _ in range(nb)]
    in_grp = [lane_grp == g for _, g in chains]
    q2 = [jnp.concatenate([jnp.where(in_grp[c], q[bi][:, j * LANE:(j + 1) * LANE], 0.0)
                           for j in range(NSA_HPG)], axis=0) for c, (bi, _) in enumerate(chains)]
    qb = [a.astype(BF16) for a in q2]
    p_c = [_masked_softmax(_dot_nt(qb[c], ck_hi[bi]) + _dot_nt(qb[c], ck_lo[bi]), cmp_valid)
           for c, (bi, _) in enumerate(chains)]
    oc = [jnp.where(in_grp[c], _dot(p_c[c], cv[bi]), 0.0) for c, (bi, _) in enumerate(chains)]
    imp_c = [a[0:tq] + a[tq:2 * tq] for a in p_c]
    imp_hi = [a.astype(BF16) for a in imp_c]
    imp_lo = [(a - h.astype(F32)).astype(BF16) for a, h in zip(imp_c, imp_hi)]
    imp_t = [(_dot_nt(covt_ref[...], h) + _dot_nt(covt_ref[...], l_))[0:n_slc]
             for h, l_ in zip(imp_hi, imp_lo)]
    imp_t = [jnp.where(forced_t, jnp.inf, jnp.where(future_t, -jnp.inf, a)) for a in imp_t]
    s_w = [[jnp.where(ok, _dot_nt(qb[c], kwv_ref[bi, pl.ds(off, tq), 0:LANE]), -jnp.inf)
            for ok, off in zip(win_ok, win_off)] for c, (bi, _) in enumerate(chains)]
    qa = []
    for c, (_, g) in enumerate(chains):
        bias_t = jnp.where(_topk_rank(imp_t[c]) < NSA_N_SEL, 0.0, NEG_BIG)
        lo = (1 - g) * NSA_DH
        rows = ([jnp.zeros((lo, tq), F32)] if lo else []) + [bias_t, jnp.zeros((LANE - lo - n_slc, tq), F32)]
        bias = jnp.concatenate(rows, axis=0).T
        qa.append(jnp.where(in_grp[c], q2[c], jnp.concatenate([bias] * NSA_HPG, axis=0)).astype(BF16))
    m_w = []
    for c in n_chains:
        m_c = s_w[c][0]
        for s_t in s_w[c][1:]:
            m_c = jnp.maximum(m_c, s_t)
        m_w.append(jnp.maximum(jnp.max(m_c, axis=-1, keepdims=True), F32_LOWEST))
    ow = [jnp.zeros((NSA_HPG * tq, LANE), F32) for _ in n_chains]
    for w, off in enumerate(win_off):
        for c in n_chains:
            ow[c] = ow[c] + jnp.dot(_exp_weights(s_w[c][w], m_w[c]), vwaug_ref[c, pl.ds(off, tq), :],
                                    preferred_element_type=F32)
    for c, (bi, _) in enumerate(chains):
        ow_c = _normalise_aug(ow[c], in_grp[c])
        for j in range(NSA_HPG):
            rows_j = slice(j * tq, (j + 1) * tq)
            o_cmp[bi][j] = o_cmp[bi][j] + oc[c][rows_j]
            o_win[bi][j] = o_win[bi][j] + ow_c[rows_j]

    half_max = lambda s_t: jnp.maximum(s_t[:, 0:LANE], s_t[:, LANE:2 * LANE])
    mx_ref[...] = jnp.full(mx_ref.shape, NEG_BIG, F32)
    groups = n_chains

    def score_tiles(tiles):
        offs = [pl.multiple_of(kt * tq, tq) for kt in tiles]
        s_new = [[_dot_nt(qa[g], kaug_ref[g, pl.ds(off, tq), :]) for g in groups] for off in offs]
        for kt, s_kt in zip(tiles, s_new):
            for g in groups:
                s_ref[g, kt] = s_kt[g]
        for g in groups:
            m_new = half_max(s_new[0][g])
            for s_kt in s_new[1:]:
                m_new = jnp.maximum(m_new, half_max(s_kt[g]))
            mx_ref[g] = jnp.maximum(mx_ref[g], m_new)

    def score_pair(p, carry):
        score_tiles([2 * p, 2 * p + 1])
        return carry

    lax.fori_loop(0, qi // 2, score_pair, 0)

    @pl.when(qi % 2 == 1)
    def _():
        score_tiles([qi - 1])

    m_s = []
    for g in groups:
        s_d = jnp.where(diag_ok, _dot_nt(qa[g], kaug_ref[g, pl.ds(s0, tq), :]), NEG_BIG)
        s_ref[g, qi] = s_d
        m_g = jnp.max(jnp.maximum(mx_ref[g], half_max(s_d)), axis=-1, keepdims=True)
        m_s.append(jnp.maximum(m_g, F32_LOWEST))
    acc_ref[...] = jnp.zeros(acc_ref.shape, F32)

    def attend_tiles(tiles):
        offs = [pl.multiple_of(kt * tq, tq) for kt in tiles]
        pv = [[jnp.dot(_exp_weights(s_ref[g, kt], m_s[g]), vaug_ref[g, pl.ds(off, tq), :],
                       preferred_element_type=F32) for g in groups] for kt, off in zip(tiles, offs)]
        for g in groups:
            upd = pv[0][g]
            for pv_kt in pv[1:]:
                upd = upd + pv_kt[g]
            acc_ref[g] += upd

    def attend_pair(p, carry):
        attend_tiles([2 * p, 2 * p + 1])
        return carry

    lax.fori_loop(0, (qi + 1) // 2, attend_pair, 0)

    @pl.when(qi % 2 == 0)
    def _():
        attend_tiles([qi])

    for c, (bi, _) in enumerate(chains):
        os_ = _normalise_aug(acc_ref[c], in_grp[c])
        for j in range(NSA_HPG):
            o_slc[bi][j] = o_slc[bi][j] + os_[j * tq:(j + 1) * tq]

    for bi in range(nb):
        gates = _split2_dot(jax.nn.sigmoid(zm_ref[bi]), gexp_ref[...])
        o = (gates[:, 0:MIX_W] * jnp.concatenate(o_cmp[bi], axis=1)
             + gates[:, MIX_W:2 * MIX_W] * jnp.concatenate(o_slc[bi], axis=1)
             + gates[:, 2 * MIX_W:3 * MIX_W] * jnp.concatenate(o_win[bi], axis=1))
        o_ref[bi] = o.astype(o_ref.dtype)


def _nsa(zn, zm, ck, cv, covt, gexp):
    b, s, _ = zn.shape
    tq = TQ_NSA
    nb = NSA_SEQS if b % NSA_SEQS == 0 else 1
    n_chains = nb * NSA_GROUPS
    return pl.pallas_call(
        _nsa_kernel,
        grid=(b // nb, s // tq),
        in_specs=[pl.BlockSpec((nb, tq, 256), lambda i, j: (i, j, 0)),
                  pl.BlockSpec((nb, s, 256), lambda i, j: (i, 0, 1)),
                  pl.BlockSpec((nb, s, 256), lambda i, j: (i, 0, 2)),
                  pl.BlockSpec((nb, N_CMP_PAD, LANE), lambda i, j: (i, 0, 0)),
                  pl.BlockSpec((nb, N_CMP_PAD, LANE), lambda i, j: (i, 0, 0)),
                  pl.BlockSpec((nb, tq, ZM_W), lambda i, j: (i, j, 0)),
                  pl.BlockSpec((N_SLC_PAD, N_CMP_PAD), lambda i, j: (0, 0)),
                  pl.BlockSpec((ZM_W, 3 * MIX_W), lambda i, j: (0, 0))],
        out_specs=pl.BlockSpec((nb, tq, MIX_W), lambda i, j: (i, j, 0)),
        out_shape=jax.ShapeDtypeStruct((b, s, MIX_W), BF16),
        scratch_shapes=[pltpu.VMEM((n_chains, s, LANE), BF16)] * 3
                       + [pltpu.VMEM((n_chains, s // tq, NSA_HPG * tq, tq), F32),
                          pltpu.VMEM((n_chains, NSA_HPG * tq, LANE), F32),
                          pltpu.VMEM((n_chains, NSA_HPG * tq, LANE), F32)],
        compiler_params=_cparams(("parallel", "arbitrary")),
        name="nsa_attn",
    )(zn, zn, zn, ck, cv, zm, covt, gexp)


def _combine_kernel(x_ref, g_ref, op_ref, oa_ref, od_ref, on_ref, wg_ref, bg_ref, wb_ref, wo_ref, o_ref):
    x = x_ref[...]
    h = _rms(x, g_ref[...]).astype(BF16)
    y = jnp.zeros(x.shape, F32)
    for i, br_ref in enumerate((op_ref, oa_ref, od_ref, on_ref)):
        gate = jax.nn.sigmoid(jnp.dot(h, wg_ref[i], preferred_element_type=F32) + bg_ref[i])
        y = y + gate * jnp.dot(br_ref[...], wb_ref[i], preferred_element_type=F32)
    o_ref[...] = x + _dot(y, wo_ref[...])


def _combine(x2, g, branches, wg, bg, wb, wo):
    t = x2.shape[0]
    tm = TM_COMB
    return pl.pallas_call(
        _combine_kernel,
        grid=(t // tm,),
        in_specs=[pl.BlockSpec((tm, D_MODEL), lambda i: (i, 0)),
                  pl.BlockSpec((1, D_MODEL), lambda i: (0, 0))]
                 + [pl.BlockSpec((tm, MIX_W), lambda i: (i, 0))] * N_BRANCH
                 + [pl.BlockSpec((N_BRANCH, D_MODEL, D_MODEL), lambda i: (0, 0, 0)),
                    pl.BlockSpec((N_BRANCH, 1, D_MODEL), lambda i: (0, 0, 0)),
                    pl.BlockSpec((N_BRANCH, MIX_W, D_MODEL), lambda i: (0, 0, 0)),
                    pl.BlockSpec((D_MODEL, D_MODEL), lambda i: (0, 0))],
        out_specs=pl.BlockSpec((tm, D_MODEL), lambda i: (i, 0)),
        out_shape=jax.ShapeDtypeStruct((t, D_MODEL), F32),
        compiler_params=_cparams(("parallel",)),
        name="combine",
    )(x2, g, *branches, wg, bg, wb, wo)


def _memkv_kernel(m_ref, g_ref, w_ref, o_ref):
    o_ref[0] = _dot(_rms(m_ref[0], g_ref[...]), w_ref[...]).astype(o_ref.dtype)


def _memkv(mem, g, w):
    b, m, _ = mem.shape
    n = 2 * X_HEADS * X_DH
    return pl.pallas_call(
        _memkv_kernel,
        grid=(b,),
        in_specs=[pl.BlockSpec((1, m, D_MODEL), lambda i: (i, 0, 0)),
                  pl.BlockSpec((1, D_MODEL), lambda i: (0, 0)),
                  pl.BlockSpec((D_MODEL, n), lambda i: (0, 0))],
        out_specs=pl.BlockSpec((1, m, n), lambda i: (i, 0, 0)),
        out_shape=jax.ShapeDtypeStruct((b, m, n), BF16),
        compiler_params=_cparams(("parallel",)),
        name="mem_kv",
    )(mem, g, w)


def _cross_kernel(x_ref, g_ref, wq_ref, kv_ref, wo_ref, o_ref):
    n_k = X_HEADS * X_DH
    subs = [slice(i * CROSS_SUB, (i + 1) * CROSS_SUB) for i in range(x_ref.shape[1] // CROSS_SUB)]
    x = [x_ref[0, r, :] for r in subs]
    q = [_dot(_rms(a, g_ref[...]), wq_ref[...]) * X_DH ** -0.5 for a in x]
    pairs = [(i, h) for h in range(X_HEADS) for i in range(len(subs))]
    head = lambda h: slice(h * X_DH, (h + 1) * X_DH)
    sc = [_dot_nt(q[i][:, head(h)], kv_ref[0, :, head(h)]) for i, h in pairs]
    e = [jnp.exp(s - jnp.max(s, axis=-1, keepdims=True)) for s in sc]
    o = [_dot(e_ih, kv_ref[0, :, n_k + h * X_DH:n_k + (h + 1) * X_DH]) / jnp.sum(e_ih, axis=-1, keepdims=True)
         for e_ih, (i, h) in zip(e, pairs)]
    for i, r in enumerate(subs):
        o_i = jnp.concatenate([o[pairs.index((i, h))] for h in range(X_HEADS)], axis=1)
        o_ref[0, r, :] = x[i] + _dot(o_i, wo_ref[...])


def _cross(x3, g, wq, kv, wo):
    b, s, _ = x3.shape
    tm = TM_CROSS
    m = kv.shape[1]
    n_k = X_HEADS * X_DH
    return pl.pallas_call(
        _cross_kernel,
        grid=(b, s // tm),
        in_specs=[pl.BlockSpec((1, tm, D_MODEL), lambda i, j: (i, j, 0)),
                  pl.BlockSpec((1, D_MODEL), lambda i, j: (0, 0)),
                  pl.BlockSpec((D_MODEL, n_k), lambda i, j: (0, 0)),
                  pl.BlockSpec((1, m, 2 * n_k), lambda i, j: (i, 0, 0)),
                  pl.BlockSpec((n_k, D_MODEL), lambda i, j: (0, 0))],
        out_specs=pl.BlockSpec((1, tm, D_MODEL), lambda i, j: (i, j, 0)),
        out_shape=jax.ShapeDtypeStruct((b, s, D_MODEL), F32),
        compiler_params=_cparams(("parallel", "parallel")),
        name="cross_attn",
    )(x3, g, wq, kv, wo)


def _ffn_kernel(x_ref, g_ref, wup_ref, cw_ref, cb_ref, wd_ref, gf_ref, o_ref, tail_ref, act_ref, *, final):
    @pl.when(pl.program_id(1) == 0)
    def _():
        tail_ref[...] = jnp.zeros_like(tail_ref)

    ts = x_ref.shape[1]
    x = x_ref[0]
    hn = _rms(x, g_ref[...]).astype(BF16)
    for c in range(D_FF // FF_CHUNK):
        cols = slice(c * FF_CHUNK, (c + 1) * FF_CHUNK)
        gcols = slice(D_FF + c * FF_CHUNK, D_FF + (c + 1) * FF_CHUNK)
        u = jnp.dot(hn, wup_ref[:, cols], preferred_element_type=F32)
        v = jnp.dot(hn, wup_ref[:, gcols], preferred_element_type=F32)
        ext = jnp.concatenate([tail_ref[:, cols], u], axis=0)
        tail_ref[:, cols] = u[ts - SUBLANE:, :]
        cw = cw_ref[:, cols]
        y = (cw[2:3] * u + cw[1:2] * pltpu.roll(ext, 1, 0)[SUBLANE:]
             + cw[0:1] * pltpu.roll(ext, 2, 0)[SUBLANE:] + cb_ref[:, cols])
        act_ref[:, cols] = (jax.nn.gelu(y) * v).astype(BF16)
    out = x + jnp.dot(act_ref[...], wd_ref[...], preferred_element_type=F32)
    if final:
        out = _rms(out, gf_ref[...])
    o_ref[0] = out


def _ffn(x3, g, wup, cw, cb, wd, gf, final):
    b, s, _ = x3.shape
    ts = TS_FFN
    return pl.pallas_call(
        functools.partial(_ffn_kernel, final=final),
        grid=(b, s // ts),
        in_specs=[pl.BlockSpec((1, ts, D_MODEL), lambda i, j: (i, j, 0)),
                  pl.BlockSpec((1, D_MODEL), lambda i, j: (0, 0)),
                  pl.BlockSpec((D_MODEL, 2 * D_FF), lambda i, j: (0, 0), pipeline_mode=pl.Buffered(1)),
                  pl.BlockSpec((FFN_CONV, D_FF), lambda i, j: (0, 0)),
                  pl.BlockSpec((1, D_FF), lambda i, j: (0, 0)),
                  pl.BlockSpec((D_FF, D_MODEL), lambda i, j: (0, 0), pipeline_mode=pl.Buffered(1)),
                  pl.BlockSpec((1, D_MODEL), lambda i, j: (0, 0))],
        out_specs=pl.BlockSpec((1, ts, D_MODEL), lambda i, j: (i, j, 0)),
        out_shape=jax.ShapeDtypeStruct((b, s, D_MODEL), F32),
        scratch_shapes=[pltpu.VMEM((SUBLANE, D_FF), F32), pltpu.VMEM((ts, D_FF), BF16)],
        compiler_params=_cparams(("parallel", "arbitrary")),
        name="conv_ffn",
    )(x3, g, wup, cw, cb, wd, gf)


def _inproj_columns():
    starts = np.concatenate([[0], np.cumsum(IN_SPLITS)])
    (p_in, a_q, a_k, a_v, a_r, a_lr, d_q, d_k, d_v, d_b, d_a, d_g,
     n_q, n_kc, n_vc, n_ks, n_vs, n_kw, n_vw, n_g) = [np.arange(starts[i], starts[i + 1])
                                                      for i in range(len(IN_SPLITS))]
    n_q = n_q.reshape(NSA_GROUPS, NSA_HPG, NSA_DH).transpose(1, 0, 2).reshape(-1)
    misc = np.full((ZM_W,), N_IN)
    misc[MISC_LR:MISC_LR + GLA_LOWRANK] = a_lr
    misc[MISC_B:MISC_B + GDN_HEADS] = d_b
    misc[MISC_A:MISC_A + GDN_HEADS] = d_a
    misc[MISC_G:MISC_G + 3 * NSA_HEADS] = n_g
    cols = np.concatenate([p_in, a_q, a_k, a_v, a_r, d_q, d_k, d_v, d_g,
                           n_q, n_ks, n_vs, n_kw, n_vw, n_kc, n_vc, misc])
    assert cols.shape[0] == Z_W
    return cols


def _regroup_columns(w, cols):
    runs, start = [], 0
    for i in range(1, len(cols) + 1):
        pad = cols[start] == N_IN
        if i == len(cols) or (cols[i] != N_IN if pad else (cols[i] == N_IN or cols[i] != cols[i - 1] + 1)):
            runs.append(jnp.zeros((w.shape[0], i - start), w.dtype) if pad
                        else w[:, int(cols[start]):int(cols[start]) + i - start])
            start = i
    return jnp.concatenate(runs, axis=1)


def _head_expand(offset, n_heads, width):
    e = np.zeros((ZM_W, n_heads * width), np.float32)
    for h in range(n_heads):
        e[offset + h, h * width:(h + 1) * width] = 1.0
    return e


def _nsa_constants(s):
    n_cmp = s // NSA_CMP_STRIDE - NSA_CMP_LEN // NSA_CMP_STRIDE + 1
    n_slc = s // NSA_SEL_LEN
    c_start = np.arange(n_cmp) * NSA_CMP_STRIDE
    s_start = np.arange(n_slc) * NSA_SEL_LEN
    cover = np.zeros((N_CMP_PAD, N_SLC_PAD), np.float32)
    cover[:n_cmp, :n_slc] = ((c_start[:, None] <= s_start[None, :] + NSA_SEL_LEN - 1)
                             & (c_start[:, None] + NSA_CMP_LEN - 1 >= s_start[None, :]))
    gexp = np.zeros((ZM_W, 3, MIX_W), np.float32)
    for g in range(NSA_GROUPS):
        for j in range(NSA_HPG):
            slot = j * NSA_GROUPS + g
            for c in range(3):
                gexp[MISC_G + (g * NSA_HPG + j) * 3 + c, c, slot * NSA_DH:(slot + 1) * NSA_DH] = 1.0
    return jnp.asarray(cover.T, dtype=BF16), jnp.asarray(gexp.reshape(ZM_W, 3 * MIX_W), dtype=BF16)


def _block_diag(blocks):
    n, a, b = blocks.shape
    return jnp.einsum('gh,gab->gahb', jnp.eye(n, dtype=blocks.dtype), blocks).reshape(n * a, n * b)


def kernel(x, mem, g_mix, w_in, pool_w, pool_scale, gla_w_lr, gla_b_lr, gla_g_norm, gdn_conv, gdn_a_log,
           gdn_dt_bias, gdn_g_norm, nsa_pe, nsa_cmp_w1, nsa_cmp_w2, w_branch, w_gate, b_gate, w_out, g_cross,
           g_mem, w_xq, w_mem_kv, w_xo, g_ffn, w_up, ffn_conv, ffn_conv_b, w_down, g_final):
    b, s, d = x.shape
    depth = w_in.shape[0]
    t = b * s
    cols = _inproj_columns()
    covt, gexp = _nsa_constants(s)
    eba = jnp.asarray(np.concatenate([_head_expand(MISC_B, GDN_HEADS, GDN_DH),
                                      _head_expand(MISC_A, GDN_HEADS, GDN_DH)], axis=1), dtype=BF16)
    ones_bd = _block_diag(jnp.ones((GDN_HEADS, GDN_DH, GDN_DH), BF16))
    nsa_rows = np.arange(MIX_W).reshape(NSA_GROUPS, NSA_HPG, NSA_DH).transpose(1, 0, 2).reshape(-1)
    row = lambda v: v.reshape(1, -1).astype(F32)
    misc_a = lambda v: jnp.zeros((1, ZM_W), F32).at[0, MISC_A:MISC_A + GDN_HEADS].set(v)

    x2 = x.reshape(t, d)
    for l in range(depth):
        w_in_r = _regroup_columns(w_in[l], cols).astype(BF16)
        wlr = jnp.zeros((ZM_W, GLA_HEADS * GLA_DK), F32).at[MISC_LR:MISC_LR + GLA_LOWRANK].set(gla_w_lr[l])
        wlr_hi = wlr.astype(BF16)
        wlr2 = jnp.stack([wlr_hi, (wlr - wlr_hi.astype(F32)).astype(BF16)])
        o_pool, zn, zc, zm, o_gla, o_gdn = _mixers(
            x2.reshape(b, s, d), row(g_mix[l]), w_in_r, _block_diag(pool_w[l]).astype(BF16), row(pool_scale[l]),
            wlr2, row(gla_b_lr[l]), row(jnp.tile(gla_g_norm[l], GLA_HEADS)),
            gdn_conv[l], eba, misc_a(gdn_a_log[l]), misc_a(gdn_dt_bias[l]),
            row(jnp.tile(gdn_g_norm[l], GDN_HEADS)), ones_bd)

        pe_x = jnp.tile(nsa_pe[l], (1, 1, NSA_GROUPS))
        w1 = nsa_cmp_w1[l].reshape(2, NSA_CMP_LEN, NSA_DH, NSA_DH)
        eye_g = jnp.eye(NSA_GROUPS, dtype=F32)
        w1_bd = jnp.einsum('gh,kpde->kpgdhe', eye_g, w1).reshape(2, NSA_CMP_LEN, LANE, LANE).astype(BF16)
        w2_bd = jnp.einsum('gh,kde->kgdhe', eye_g, nsa_cmp_w2[l]).reshape(2, LANE, LANE).astype(BF16)
        ck, cv = _nsa_compress(zc, pe_x, w1_bd, w2_bd)
        o_nsa = _nsa(zn, zm, ck, cv, covt, gexp)

        wb = jnp.concatenate([w_branch[l, :3], w_branch[l, 3][nsa_rows][None]], axis=0).astype(BF16)
        branches = [o.reshape(t, MIX_W) for o in (o_pool, o_gla, o_gdn, o_nsa)]
        x2 = _combine(x2, row(g_mix[l]), branches, w_gate[l].astype(BF16),
                      b_gate[l].reshape(N_BRANCH, 1, d), wb, w_out[l].astype(BF16))

        kv = _memkv(mem, row(g_mem[l]), w_mem_kv[l].astype(BF16))
        x3 = _cross(x2.reshape(b, s, d), row(g_cross[l]), w_xq[l].astype(BF16), kv, w_xo[l].astype(BF16))

        x3 = _ffn(x3, row(g_ffn[l]), w_up[l].astype(BF16), ffn_conv[l], row(ffn_conv_b[l]),
                  w_down[l].astype(BF16), row(g_final), final=(l == depth - 1))
        x2 = x3.reshape(t, d)
    return x2.reshape(b, s, d)
```

```python
import functools

import numpy as np
import jax
import jax.numpy as jnp
from jax import lax
from jax.experimental import pallas as pl
from jax.experimental.pallas import tpu as pltpu

F32 = jnp.float32
BF16 = jnp.bfloat16
HIGHEST = lax.Precision.HIGHEST

D_MODEL = 1024
MIX_W = 256
POOL_WINDOWS = (2, 4, 8, 16)
POOL_GW = 64
GLA_HEADS = 4
GLA_DK = 32
GLA_DV = 64
GLA_LOWRANK = 16
GLA_GATE_NORM = 16.0
CHUNK = 64
GDN_HEADS = 4
GDN_DH = 64
GDN_CONV = 4
NSA_HEADS = 4
NSA_GROUPS = 2
NSA_HPG = 2
NSA_DH = 64
NSA_KV = 128
NSA_CMP_LEN = 32
NSA_CMP_STRIDE = 16
NSA_SEL_LEN = 64
NSA_N_SEL = 16
NSA_WINDOW = 512
X_HEADS = 4
X_DH = 128
D_FF = 2816
FFN_CONV = 3
EPS = 1e-6
N_BRANCH = 4

IN_SPLITS = (MIX_W,
             128, 128, 256, 256, GLA_LOWRANK,
             MIX_W, MIX_W, MIX_W, GDN_HEADS, GDN_HEADS, MIX_W,
             256, NSA_KV, NSA_KV, NSA_KV, NSA_KV, NSA_KV, NSA_KV, 3 * NSA_HEADS)
N_IN = sum(IN_SPLITS)

MISC_LR = 0
MISC_B = 16
MISC_A = 20
MISC_G = 24
LANE = 128
SUBLANE = 8

Z_WIDTHS = (256, 768, 1024, 768, 256, 128)
Z_DTYPES = (F32, BF16, BF16, BF16, F32, F32)
ZP_W, ZG_W, ZD_W, ZN_W, ZC_W, ZM_W = Z_WIDTHS
Z_W = sum(Z_WIDTHS)
NEG_BIG = -1e30
F32_LOWEST = float(np.finfo(np.float32).min)

TM_COMB = 512
TM_CROSS = 1024
CROSS_SUB = 512
TS_FFN = 1024
FF_CHUNK = 256
TQ_NSA = 256
NSA_SEQS = 2
REC_SEQS = 8
REC_LAG = 3
GLA_START = 6
N_SLC_PAD = 128
N_CMP_PAD = 128

VMEM_LIMIT = 56 * 1024 * 1024


def _cparams(sem):
    return pltpu.CompilerParams(dimension_semantics=sem, vmem_limit_bytes=VMEM_LIMIT)


def _rms(x, g):
    return x * lax.rsqrt(jnp.mean(x * x, axis=-1, keepdims=True) + EPS) * g


def _dot(a, b):
    return jnp.dot(a.astype(BF16), b.astype(BF16), preferred_element_type=F32)


def _dot_nt(a, b):
    return lax.dot_general(a.astype(BF16), b.astype(BF16), (((1,), (1,)), ((), ())),
                           preferred_element_type=F32)


def _dot_tn(a, b):
    return lax.dot_general(a.astype(BF16), b.astype(BF16), (((0,), (0,)), ((), ())),
                           preferred_element_type=F32)


def _dot_hi(a, b):
    return jnp.dot(a, b, precision=HIGHEST, preferred_element_type=F32)


def _dot_nt_hi(a, b):
    return lax.dot_general(a, b, (((1,), (1,)), ((), ())), precision=HIGHEST,
                           preferred_element_type=F32)


def _split2_dot(a, b):
    hi = a.astype(BF16)
    lo = (a - hi.astype(F32)).astype(BF16)
    return jnp.dot(hi, b, preferred_element_type=F32) + jnp.dot(lo, b, preferred_element_type=F32)


def _split3_lhs_dot(a, b):
    hi = a.astype(BF16)
    r1 = a - hi.astype(F32)
    mid = r1.astype(BF16)
    lo = (r1 - mid.astype(F32)).astype(BF16)
    return ((jnp.dot(hi, b, preferred_element_type=F32) + jnp.dot(mid, b, preferred_element_type=F32))
            + jnp.dot(lo, b, preferred_element_type=F32))


def _dot3(a, w_hi, w_lo):
    a_hi = a.astype(BF16)
    a_lo = (a - a_hi.astype(F32)).astype(BF16)
    return (jnp.dot(a_hi, w_hi, preferred_element_type=F32)
            + (jnp.dot(a_lo, w_hi, preferred_element_type=F32) + jnp.dot(a_hi, w_lo, preferred_element_type=F32)))


def _iota(shape, axis):
    return lax.broadcasted_iota(jnp.int32, shape, axis)


def _block_mask(rows, cols, rb, cb):
    return (_iota((rows, cols), 0) // rb) == (_iota((rows, cols), 1) // cb)


def _shift_rows(x, k):
    t = _iota(x.shape, 0)
    return jnp.where(t >= k, pltpu.roll(x, k, 0), 0.0)


def _cumsum_rows(x):
    k = 1
    while k < x.shape[0]:
        x = x + _shift_rows(x, k)
        k *= 2
    return x


def _softplus(x):
    return jnp.maximum(x, 0.0) + jnp.log1p(jnp.exp(-jnp.abs(x)))


def _log_sigmoid(x):
    return -_softplus(-x)


def _silu(x):
    return x * jax.nn.sigmoid(x)


def _masked_softmax(s, mask):
    s = jnp.where(mask, s, -jnp.inf)
    m = jnp.maximum(jnp.max(s, axis=-1, keepdims=True), F32_LOWEST)
    e = jnp.exp(s - m)
    den = jnp.sum(e, axis=-1, keepdims=True)
    return e / jnp.where(den > 0, den, 1.0)


def _tile4(x):
    return jnp.concatenate([x, x, x, x], axis=0)


POOL_HALO = 16


def _gla_masks():
    return (_block_mask(4 * CHUNK, 128, CHUNK, GLA_DK),
            _block_mask(4 * CHUNK, 256, CHUNK, GLA_DV),
            _block_mask(256, 128, GLA_DV, GLA_DK),
            (_iota((CHUNK, 256), 1) % CHUNK) <= _iota((CHUNK, 256), 0))


def _seqs(f, *lists):
    return [f(*args) for args in zip(*lists)]


def _run_staggered(stage_gens, starts, periods):
    live = list(range(len(stage_gens)))
    tick = 0
    while live:
        for i in list(live):
            if tick >= starts[i] and (tick - starts[i]) % periods[i] == 0:
                try:
                    next(stage_gens[i])
                except StopIteration:
                    live.remove(i)
        tick += 1


def _gla_chunk(masks, zqk, v, r, zm, wlr_ref, blr_ref, gn_ref, ones_ref, st_refs, out):
    mask_k, mask_v, mask_st, causal = masks
    q = [z[:, 0:128] * GLA_DK ** -0.5 for z in zqk]
    k = [z[:, 128:256] for z in zqk]
    pre = _seqs(lambda a: _dot3(a, wlr_ref[0], wlr_ref[1]), zm)
    yield
    bc = _seqs(lambda a: _cumsum_rows(_log_sigmoid(a + blr_ref[...]) / GLA_GATE_NORM), pre)
    bl = [a[CHUNK - 1:CHUNK, :] for a in bc]
    q_e = _seqs(lambda a, c: a * jnp.exp(c), q, bc)
    k_e = _seqs(lambda a, c: a * jnp.exp(-c), k, bc)
    k_u = _seqs(lambda a, c, l: a * jnp.exp(l - c), k, bc, bl)
    st = [ref[...] for ref in st_refs]
    yield
    att = _seqs(lambda a, b: jnp.where(causal, _dot_nt(a, jnp.where(mask_k, _tile4(b), 0.0)), 0.0), q_e, k_e)
    inter = _seqs(_dot_nt, q_e, st)
    kv = _seqs(_dot_tn, v, k_u)
    yield
    o = _seqs(lambda a, b, c: _dot(a, _tile4(b.astype(BF16)) * ones_ref[...]) + c, att, v, inter)
    for ref, s_old, l, upd in zip(st_refs, st, bl, kv):
        ref[...] = s_old * jnp.exp(l) + jnp.where(mask_st, upd, 0.0)
    yield
    ms = _seqs(lambda a: _dot(a * a, ones_ref[...]) * (1.0 / GLA_DV), o)
    yield
    out[:] = _seqs(lambda a, m, g: a * lax.rsqrt(m + EPS) * gn_ref[...] * _silu(g), o, ms, r)


GDN_HALO = SUBLANE


def _gdn_masks():
    c4 = 4 * CHUNK
    col = _iota((CHUNK, c4), 1) % CHUNK
    row = _iota((CHUNK, c4), 0)
    return (_block_mask(c4, c4, CHUNK, CHUNK), col <= row, col < row, col == row)


def _gdn_chunk(masks, halo, zqkv, gate, zm, cw_ref, eba_ref, alog_ref, dtb_ref, gn_ref, ones_ref, st_refs, out):
    mask_bd, incl, strict, diag = masks
    cw = cw_ref[...]
    bd = lambda a: _tile4(a.astype(BF16)) * ones_ref[...]

    def conv_silu(h, cur):
        ext = jnp.concatenate([h, cur], axis=0)
        conv = (cw[3:4] * cur + cw[2:3] * pltpu.roll(ext, 1, 0)[GDN_HALO:]
                + cw[1:2] * pltpu.roll(ext, 2, 0)[GDN_HALO:] + cw[0:1] * pltpu.roll(ext, 3, 0)[GDN_HALO:])
        return _silu(conv)

    qkv = _seqs(conv_silu, halo, zqkv)
    yield
    v = [a[:, 512:768] for a in qkv]
    ssq = _seqs(lambda a: _dot(jnp.concatenate([a[:, 0:256] * a[:, 0:256], a[:, 256:512] * a[:, 256:512]],
                                               axis=0), ones_ref[...]), qkv)
    yield
    q = _seqs(lambda a, s: a[:, 0:256] * lax.rsqrt(s[0:CHUNK] + EPS) * GDN_DH ** -0.5, qkv, ssq)
    k = _seqs(lambda a, s: a[:, 256:512] * lax.rsqrt(s[CHUNK:2 * CHUNK] + EPS), qkv, ssq)
    beta = _seqs(lambda a: _dot(jax.nn.sigmoid(a), eba_ref[:, 0:MIX_W]), zm)
    yield
    gc = _seqs(lambda a: _split3_lhs_dot(_cumsum_rows(-jnp.exp(alog_ref[...]) * _softplus(a + dtb_ref[...])),
                                         eba_ref[:, MIX_W:2 * MIX_W]), zm)
    yield
    gl = [a[CHUNK - 1:CHUNK, :] for a in gc]
    g_row = _seqs(lambda a: jnp.sum(jnp.where(diag, a, 0.0), axis=0, keepdims=True), gc)
    yield
    decay =_seqs(lambda a, b: jnp.exp(jnp.where(incl, a - b, -jnp.inf)), gc, g_row)
    kb = _seqs(lambda a, b: a * b, k, beta)
    vb = _seqs(lambda a, b: a * b, v, beta)
    kq = _seqs(lambda a, b, c: _dot_nt(jnp.concatenate([a, b], axis=0), bd(c)), kb, q, k)
    yield
    n_mat =_seqs(lambda s, d: jnp.where(strict, s[0:CHUNK] * d, 0.0), kq, decay)
    a_qk = _seqs(lambda s, d: s[CHUNK:2 * CHUNK] * d, kq, decay)
    m = _seqs(lambda a: -a, n_mat)
    x = _seqs(lambda a: jnp.where(diag, 1.0, 0.0) + a, m)
    m = _seqs(lambda a: _dot(a, bd(a)), m)
    yield
    p = 2
    while p < CHUNK // 2:
        xm = _seqs(lambda a, b: _dot(jnp.concatenate([a, b], axis=0), bd(b)), x, m)
        x = _seqs(lambda a, r: a + r[0:CHUNK], x, xm)
        m = [r[CHUNK:2 * CHUNK] for r in xm]
        p *= 2
        yield
    x = _seqs(lambda a, b: a + _dot(a, bd(b)), x, m)
    yield
    uw = _seqs(lambda a, b, c, g: _dot(a, jnp.concatenate([bd(b), bd(c * jnp.exp(g))], axis=1)),
               x, vb, kb, gc)
    yield
    st = [ref[...] for ref in st_refs]
    ws = _seqs(lambda a, b, g, s: _dot(jnp.concatenate([a[:, 256:512], b * jnp.exp(g)], axis=0), s),
               uw, q, gc, st)
    yield
    v_new = _seqs(lambda a, r: a[:, 0:256] - r[0:CHUNK], uw, ws)
    o = _seqs(lambda r, qk, vn: r[CHUNK:2 * CHUNK] + _dot(qk, bd(vn)), ws, a_qk, v_new)
    upd = _seqs(lambda a, g, l, vn: _dot_tn(a * jnp.exp(l - g), vn), k, gc, gl, v_new)
    yield
    for ref, s_old, l, u in zip(st_refs, st, gl, upd):
        ref[...] = s_old * jnp.exp(l) + jnp.where(mask_bd, u, 0.0)
    ms = _seqs(lambda a: _dot(a * a, ones_ref[...]) * (1.0 / GDN_DH), o)
    yield
    out[:] = _seqs(lambda a, m_, g: a * lax.rsqrt(m_ + EPS) * gn_ref[...] * _silu(g), o, ms, gate)


def _pool_diff(u, tail, pos0):
    ext = jnp.concatenate([tail, u], axis=0)
    s2 = ext + pltpu.roll(ext, 1, 0)
    s4 = s2 + pltpu.roll(s2, 2, 0)
    s8 = s4 + pltpu.roll(s4, 4, 0)
    s16 = s8 + pltpu.roll(s8, 8, 0)
    grp = _iota(u.shape, 1) // POOL_GW
    win = jnp.where(grp == 0, s2[POOL_HALO:], jnp.where(grp == 1, s4[POOL_HALO:],
                                                        jnp.where(grp == 2, s8[POOL_HALO:], s16[POOL_HALO:])))
    width = jnp.where(grp == 0, POOL_WINDOWS[0],
                      jnp.where(grp == 1, POOL_WINDOWS[1],
                                jnp.where(grp == 2, POOL_WINDOWS[2], POOL_WINDOWS[3])))
    cnt = jnp.minimum(pos0 + _iota(u.shape, 0) + 1, width).astype(F32)
    return win / cnt - u


def _mix_kernel(x_ref, g_ref, w_ref, pw_ref, ps_ref, wlr_ref, blr_ref, gng_ref, cw_ref, eba_ref, alog_ref, dtb_ref,
                gnd_ref, ones_ref, op_ref, zn_ref, zc_ref, zm_ref, og_ref, od_ref, ptail_ref, halo_ref, *st_refs):
    nb = x_ref.shape[0]
    chunk = pl.program_id(1)

    @pl.when(chunk == 0)
    def _():
        ptail_ref[...] = jnp.zeros_like(ptail_ref)
        halo_ref[...] = jnp.zeros_like(halo_ref)
        for st_ref in st_refs:
            st_ref[...] = jnp.zeros_like(st_ref)

    h = _rms(x_ref[...].reshape(nb * CHUNK, D_MODEL), g_ref[...]).astype(BF16)
    proj = lambda lo, n: jnp.dot(h, w_ref[:, lo:lo + n], preferred_element_type=F32)
    seqs = lambda a: [a[i * CHUNK:(i + 1) * CHUNK] for i in range(nb)]
    off_g, off_d = ZP_W, ZP_W + ZG_W
    off_n = off_d + ZD_W
    off_c = off_n + ZN_W
    off_m = off_c + ZC_W
    zqkv = seqs(proj(off_d, 768))
    zm = seqs(proj(off_m, ZM_W))
    halo = [halo_ref[i] for i in range(nb)]
    for i in range(nb):
        halo_ref[i] = zqkv[i][CHUNK - GDN_HALO:, :]
        zm_ref[i] = zm[i]
    halves = [slice(0, nb // 2), slice(nb // 2, nb)] if nb > 1 else [slice(0, nb)]
    gla_qk, gla_v, gla_r = [], [], []
    gate_parts = [[] for _ in halves]

    def other_projections():
        gla_qk.extend(seqs(proj(off_g, 256)))
        yield
        gla_v.extend(seqs(proj(off_g + 256, 256)))
        yield
        u = seqs(proj(0, ZP_W))
        yield
        diff = [_pool_diff(u[i], ptail_ref[i], chunk * CHUNK) for i in range(nb)]
        for i in range(nb):
            ptail_ref[i] = u[i][CHUNK - POOL_HALO:, :]
        yield
        for lo in range(0, ZN_W, 256):
            zn = seqs(proj(off_n + lo, 256))
            for i in range(nb):
                zn_ref[i, :, lo:lo + 256] = zn[i].astype(zn_ref.dtype)
            yield
        zc = seqs(proj(off_c, ZC_W))
        for i in range(nb):
            zc_ref[i] = zc[i]
        yield
        gla_r.extend(seqs(proj(off_g + 512, 256)))
        yield
        gate = seqs(proj(off_d + 768, MIX_W))
        for hs, part in zip(halves, gate_parts):
            part.extend(gate[hs])
        yield
        o_pool = seqs(_dot(jnp.concatenate(diff, axis=0), pw_ref[...]) * ps_ref[...])
        for i in range(nb):
            op_ref[i] = o_pool[i].astype(op_ref.dtype)

    og = [None] * nb
    od_parts = [[None] * (hs.stop - hs.start) for hs in halves]
    gdn_masks = _gdn_masks()
    gens = [_gdn_chunk(gdn_masks, halo[hs], zqkv[hs], gate_part, zm[hs], cw_ref, eba_ref, alog_ref, dtb_ref,
                       gnd_ref, ones_ref, st_refs[nb:][hs], part)
            for hs, part, gate_part in zip(halves, od_parts, gate_parts)]
    gens.append(_gla_chunk(_gla_masks(), gla_qk, gla_v, gla_r, zm, wlr_ref, blr_ref, gng_ref, ones_ref,
                           st_refs[:nb], og))
    starts = [REC_LAG * i for i in range(len(halves))] + [GLA_START]
    gens.append(other_projections())
    _run_staggered(gens, starts + [0], [1] * len(gens))
    od = [o for part in od_parts for o in part]
    for i in range(nb):
        og_ref[i] = og[i].astype(og_ref.dtype)
        od_ref[i] = od[i].astype(od_ref.dtype)


def _mixers(x3, g, w, pool_w_bd, pool_scale, wlr2, blr, gn_gla, cw, eba, alog, dtb, gn_gdn, ones_bd):
    b, s, _ = x3.shape
    nb = REC_SEQS if b % REC_SEQS == 0 else 1
    const = lambda shape: pl.BlockSpec(shape, lambda i, c: (0,) * len(shape))
    tok = lambda n: pl.BlockSpec((nb, CHUNK, n), lambda i, c: (i, c, 0))
    outs = ((MIX_W, BF16), (ZN_W, BF16), (ZC_W, F32), (ZM_W, F32), (MIX_W, BF16), (MIX_W, BF16))
    return pl.pallas_call(
        _mix_kernel,
        grid=(b // nb, s // CHUNK),
        in_specs=[tok(D_MODEL), const((1, D_MODEL)), const((D_MODEL, Z_W)), const((MIX_W, MIX_W)), const((1, MIX_W)),
                  const((2, ZM_W, 128)), const((1, 128)), const((1, MIX_W)),
                  const((GDN_CONV, 768)), const((ZM_W, 2 * MIX_W)), const((1, ZM_W)), const((1, ZM_W)),
                  const((1, MIX_W)), const((MIX_W, MIX_W))],
        out_specs=[tok(n) for n, _ in outs],
        out_shape=[jax.ShapeDtypeStruct((b, s, n), dt) for n, dt in outs],
        scratch_shapes=([pltpu.VMEM((nb, POOL_HALO, MIX_W), F32), pltpu.VMEM((nb, GDN_HALO, 768), F32)]
                        + [pltpu.VMEM((256, 128), F32)] * nb + [pltpu.VMEM((256, 256), F32)] * nb),
        compiler_params=_cparams(("parallel", "arbitrary")),
        name="inproj_mixers",
    )(x3, g, w, pool_w_bd, pool_scale, wlr2, blr, gn_gla, cw, eba, alog, dtb, gn_gdn, ones_bd)


def _cmp_kernel(kc_ref, vc_ref, pe_ref, w1_ref, w2_ref, ck_ref, cv_ref):
    n_sub = NSA_CMP_LEN // NSA_CMP_STRIDE
    n_chunks = kc_ref.shape[1] // NSA_CMP_STRIDE
    for which, (src_ref, out_ref) in enumerate(((kc_ref, ck_ref), (vc_ref, cv_ref))):
        parts = [jnp.zeros((n_chunks, LANE), F32) for _ in range(n_sub)]
        for i in range(NSA_CMP_STRIDE):
            slab = src_ref[0, pl.ds(i, n_chunks, stride=NSA_CMP_STRIDE), :]
            for sub in range(n_sub):
                p = sub * NSA_CMP_STRIDE + i
                parts[sub] = parts[sub] + _dot(slab + pe_ref[which, p:p + 1, :], w1_ref[which, p])
        pre = parts[0] + pltpu.roll(parts[1], n_chunks - 1, 0)
        out_ref[0] = _dot(jax.nn.gelu(pre), w2_ref[which])


def _nsa_compress(zc, pe_x, w1_bd, w2_bd):
    b, s, _ = zc.shape
    n_chunks = s // NSA_CMP_STRIDE
    return pl.pallas_call(
        _cmp_kernel,
        grid=(b,),
        in_specs=[pl.BlockSpec((1, s, LANE), lambda i: (i, 0, 0)),
                  pl.BlockSpec((1, s, LANE), lambda i: (i, 0, 1)),
                  pl.BlockSpec((2, NSA_CMP_LEN, LANE), lambda i: (0, 0, 0)),
                  pl.BlockSpec((2, NSA_CMP_LEN, LANE, LANE), lambda i: (0, 0, 0, 0)),
                  pl.BlockSpec((2, LANE, LANE), lambda i: (0, 0, 0))],
        out_specs=[pl.BlockSpec((1, n_chunks, LANE), lambda i: (i, 0, 0))] * 2,
        out_shape=[jax.ShapeDtypeStruct((b, n_chunks, LANE), F32)] * 2,
        compiler_params=_cparams(("parallel",)),
        name="nsa_compress",
    )(zc, zc, pe_x, w1_bd, w2_bd)


def _topk_rank(v):
    n, t = v.shape
    blocks = [v[r:r + SUBLANE] for r in range(0, n, SUBLANE)]
    sub = _iota((SUBLANE, t), 0)
    rank = [jnp.zeros((SUBLANE, t), jnp.int32) for _ in blocks]
    for m in range(n):
        other = v[m:m + 1, :]
        mb, ms = divmod(m, SUBLANE)
        for r, blk in enumerate(blocks):
            if r < mb:
                ahead = (other > blk).astype(jnp.int32)
            elif r > mb:
                ahead = (other >= blk).astype(jnp.int32)
            else:
                ahead = jnp.where(sub > ms, (other >= blk).astype(jnp.int32), (other > blk).astype(jnp.int32))
            rank[r] = rank[r] + ahead
    return jnp.concatenate(rank, axis=0)


def _exp_weights(s, m):
    return jnp.exp((s - m).astype(BF16))


def _normalise_aug(o_aug, in_grp):
    den = pltpu.roll(o_aug, NSA_DH, 1)
    return jnp.where(in_grp, o_aug / jnp.where(den > 0, den, 1.0), 0.0)


def _nsa_kernel(q_ref, ksv_ref, kwv_ref, ck_ref, cv_ref, zm_ref, covt_ref, gexp_ref, o_ref,
                kaug_ref, vaug_ref, vwaug_ref, s_ref, mx_ref, acc_ref):
    nb = q_ref.shape[0]
    tq = q_ref.shape[1]
    s_len = ksv_ref.shape[1]
    n_slc = s_len // NSA_SEL_LEN
    qi = pl.program_id(1)
    s0 = pl.multiple_of(qi * tq, tq)
    lane_grp = _iota((1, LANE), 1) // NSA_DH
    chains = [(bi, g) for bi in range(nb) for g in range(NSA_GROUPS)]
    n_chains = range(len(chains))

    @pl.when(qi == 0)
    def _():
        lane = _iota((s_len, LANE), 1)
        key_blk = _iota((s_len, LANE), 0) // NSA_SEL_LEN
        for bi in range(nb):
            k = ksv_ref[bi, :, 0:LANE].astype(F32)
            v = ksv_ref[bi, :, LANE:2 * LANE].astype(F32)
            vw = kwv_ref[bi, :, LANE:2 * LANE].astype(F32)
            for g in range(NSA_GROUPS):
                own = lane // NSA_DH == g
                onehot = jnp.where(lane - (1 - g) * NSA_DH == key_blk, 1.0, 0.0)
                kaug_ref[bi * NSA_GROUPS + g] = jnp.where(own, k, onehot).astype(BF16)
                vaug_ref[bi * NSA_GROUPS + g] = jnp.where(own, v, 1.0).astype(BF16)
                vwaug_ref[bi * NSA_GROUPS + g] = jnp.where(own, vw, 1.0).astype(BF16)

    t_col = s0 + _iota((tq, 1), 0)
    t_col2 = jnp.concatenate([t_col, t_col], axis=0)
    t_row = s0 + _iota((1, tq), 1)
    q = [q_ref[bi].astype(F32) * NSA_DH ** -0.5 for bi in range(nb)]
    ck_hi = [ck_ref[bi].astype(BF16) for bi in range(nb)]
    ck_lo = [(ck_ref[bi] - ck_hi[bi].astype(F32)).astype(BF16) for bi in range(nb)]
    cv = [cv_ref[bi] for bi in range(nb)]
    cmp_end = _iota((1, N_CMP_PAD), 1) * NSA_CMP_STRIDE + (NSA_CMP_LEN - 1)
    cmp_valid = cmp_end <= t_col2
    blk_t = _iota((n_slc, tq), 0)
    cur_t = t_row // NSA_SEL_LEN
    forced_t = (blk_t == 0) | (blk_t == cur_t) | (blk_t == cur_t - 1)
    future_t = blk_t > cur_t
    row_in_tile = jnp.concatenate([_iota((tq, tq), 0)] * NSA_HPG, axis=0)
    col_in_tile = _iota((NSA_HPG * tq, tq), 1)
    diag_ok = col_in_tile <= row_in_tile
    n_wt = NSA_WINDOW // tq + 1
    never = 2 * tq
    win_off = [pl.multiple_of(jnp.maximum(qi - (n_wt - 1 - w), 0) * tq, tq) for w in range(n_wt)]
    win_ok = [col_in_tile > row_in_tile + jnp.where(qi >= n_wt - 1, 0, never)]
    win_ok += [col_in_tile >= jnp.where(qi >= n_wt - 1 - w, 0, never) for w in range(1, n_wt - 1)]
    win_ok += [diag_ok]

    o_cmp = [[jnp.zeros((tq, LANE), F32) for _ in range(NSA_HPG)] for _ in range(nb)]
    o_slc = [[jnp.zeros((tq, LANE), F32) for _ in range(NSA_HPG)] for _ in range(nb)]
    o_win = [[jnp.zeros((tq, LANE), F32) for _ in range(NSA_HPG)] for _ in range(nb)]
    in_grp = [lane_grp == g for _, g in chains]
    q2 = [jnp.concatenate([jnp.where(in_grp[c], q[bi][:, j * LANE:(j + 1) * LANE], 0.0)
                           for j in range(NSA_HPG)], axis=0) for c, (bi, _) in enumerate(chains)]
    qb = [a.astype(BF16) for a in q2]
    p_c = [_masked_softmax(_dot_nt(qb[c], ck_hi[bi]) + _dot_nt(qb[c], ck_lo[bi]), cmp_valid)
           for c, (bi, _) in enumerate(chains)]
    oc = [jnp.where(in_grp[c], _dot(p_c[c], cv[bi]), 0.0) for c, (bi, _) in enumerate(chains)]
    imp_c = [a[0:tq] + a[tq:2 * tq] for a in p_c]
    imp_hi = [a.astype(BF16) for a in imp_c]
    imp_lo = [(a - h.astype(F32)).astype(BF16) for a, h in zip(imp_c, imp_hi)]
    imp_t = [(_dot_nt(covt_ref[...], h) + _dot_nt(covt_ref[...], l_))[0:n_slc]
             for h, l_ in zip(imp_hi, imp_lo)]
    imp_t = [jnp.where(forced_t, jnp.inf, jnp.where(future_t, -jnp.inf, a)) for a in imp_t]
    s_w = [[jnp.where(ok, _dot_nt(qb[c], kwv_ref[bi, pl.ds(off, tq), 0:LANE]), -jnp.inf)
            for ok, off in zip(win_ok, win_off)] for c, (bi, _) in enumerate(chains)]
    qa = []
    for c, (_, g) in enumerate(chains):
        bias_t = jnp.where(_topk_rank(imp_t[c]) < NSA_N_SEL, 0.0, NEG_BIG)
        lo = (1 - g) * NSA_DH
        rows = ([jnp.zeros((lo, tq), F32)] if lo else []) + [bias_t, jnp.zeros((LANE - lo - n_slc, tq), F32)]
        bias = jnp.concatenate(rows, axis=0).T
        qa.append(jnp.where(in_grp[c], q2[c], jnp.concatenate([bias] * NSA_HPG, axis=0)).astype(BF16))
    m_w = []
    for c in n_chains:
        m_c = s_w[c][0]
        for s_t in s_w[c][1:]:
            m_c = jnp.maximum(m_c, s_t)
        m_w.append(jnp.maximum(jnp.max(m_c, axis=-1, keepdims=True), F32_LOWEST))
    ow = [jnp.zeros((NSA_HPG * tq, LANE), F32) for _ in n_chains]
    for w, off in enumerate(win_off):
        for c in n_chains:
            ow[c] = ow[c] + jnp.dot(_exp_weights(s_w[c][w], m_w[c]), vwaug_ref[c, pl.ds(off, tq), :],
                                    preferred_element_type=F32)
    for c, (bi, _) in enumerate(chains):
        ow_c = _normalise_aug(ow[c], in_grp[c])
        for j in range(NSA_HPG):
            rows_j = slice(j * tq, (j + 1) * tq)
            o_cmp[bi][j] = o_cmp[bi][j] + oc[c][rows_j]
            o_win[bi][j] = o_win[bi][j] + ow_c[rows_j]

    half_max = lambda s_t: jnp.maximum(s_t[:, 0:LANE], s_t[:, LANE:2 * LANE])
    mx_ref[...] = jnp.full(mx_ref.shape, NEG_BIG, F32)
    groups = n_chains

    def score_tiles(tiles):
        offs = [pl.multiple_of(kt * tq, tq) for kt in tiles]
        s_new = [[_dot_nt(qa[g], kaug_ref[g, pl.ds(off, tq), :]) for g in groups] for off in offs]
        for kt, s_kt in zip(tiles, s_new):
            for g in groups:
                s_ref[g, kt] = s_kt[g]
        for g in groups:
            m_new = half_max(s_new[0][g])
            for s_kt in s_new[1:]:
                m_new = jnp.maximum(m_new, half_max(s_kt[g]))
            mx_ref[g] = jnp.maximum(mx_ref[g], m_new)

    def score_pair(p, carry):
        score_tiles([2 * p, 2 * p + 1])
        return carry

    lax.fori_loop(0, qi // 2, score_pair, 0)

    @pl.when(qi % 2 == 1)
    def _():
        score_tiles([qi - 1])

    m_s = []
    for g in groups:
        s_d = jnp.where(diag_ok, _dot_nt(qa[g], kaug_ref[g, pl.ds(s0, tq), :]), NEG_BIG)
        s_ref[g, qi] = s_d
        m_g = jnp.max(jnp.maximum(mx_ref[g], half_max(s_d)), axis=-1, keepdims=True)
        m_s.append(jnp.maximum(m_g, F32_LOWEST))
    acc_ref[...] = jnp.zeros(acc_ref.shape, F32)

    def attend_tiles(tiles):
        offs = [pl.multiple_of(kt * tq, tq) for kt in tiles]
        pv = [[jnp.dot(_exp_weights(s_ref[g, kt], m_s[g]), vaug_ref[g, pl.ds(off, tq), :],
                       preferred_element_type=F32) for g in groups] for kt, off in zip(tiles, offs)]
        for g in groups:
            upd = pv[0][g]
            for pv_kt in pv[1:]:
                upd = upd + pv_kt[g]
            acc_ref[g] += upd

    def attend_pair(p, carry):
        attend_tiles([2 * p, 2 * p + 1])
        return carry

    lax.fori_loop(0, (qi + 1) // 2, attend_pair, 0)

    @pl.when(qi % 2 == 0)
    def _():
        attend_tiles([qi])

    for c, (bi, _) in enumerate(chains):
        os_ = _normalise_aug(acc_ref[c], in_grp[c])
        for j in range(NSA_HPG):
            o_slc[bi][j] = o_slc[bi][j] + os_[j * tq:(j + 1) * tq]

    for bi in range(nb):
        gates = _split2_dot(jax.nn.sigmoid(zm_ref[bi]), gexp_ref[...])
        o = (gates[:, 0:MIX_W] * jnp.concatenate(o_cmp[bi], axis=1)
             + gates[:, MIX_W:2 * MIX_W] * jnp.concatenate(o_slc[bi], axis=1)
             + gates[:, 2 * MIX_W:3 * MIX_W] * jnp.concatenate(o_win[bi], axis=1))
        o_ref[bi] = o.astype(o_ref.dtype)


def _nsa(zn, zm, ck, cv, covt, gexp):
    b, s, _ = zn.shape
    tq = TQ_NSA
    nb = NSA_SEQS if b % NSA_SEQS == 0 else 1
    n_chains = nb * NSA_GROUPS
    return pl.pallas_call(
        _nsa_kernel,
        grid=(b // nb, s // tq),
        in_specs=[pl.BlockSpec((nb, tq, 256), lambda i, j: (i, j, 0)),
                  pl.BlockSpec((nb, s, 256), lambda i, j: (i, 0, 1)),
                  pl.BlockSpec((nb, s, 256), lambda i, j: (i, 0, 2)),
                  pl.BlockSpec((nb, N_CMP_PAD, LANE), lambda i, j: (i, 0, 0)),
                  pl.BlockSpec((nb, N_CMP_PAD, LANE), lambda i, j: (i, 0, 0)),
                  pl.BlockSpec((nb, tq, ZM_W), lambda i, j: (i, j, 0)),
                  pl.BlockSpec((N_SLC_PAD, N_CMP_PAD), lambda i, j: (0, 0)),
                  pl.BlockSpec((ZM_W, 3 * MIX_W), lambda i, j: (0, 0))],
        out_specs=pl.BlockSpec((nb, tq, MIX_W), lambda i, j: (i, j, 0)),
        out_shape=jax.ShapeDtypeStruct((b, s, MIX_W), BF16),
        scratch_shapes=[pltpu.VMEM((n_chains, s, LANE), BF16)] * 3
                       + [pltpu.VMEM((n_chains, s // tq, NSA_HPG * tq, tq), F32),
                          pltpu.VMEM((n_chains, NSA_HPG * tq, LANE), F32),
                          pltpu.VMEM((n_chains, NSA_HPG * tq, LANE), F32)],
        compiler_params=_cparams(("parallel", "arbitrary")),
        name="nsa_attn",
    )(zn, zn, zn, ck, cv, zm, covt, gexp)


def _combine_kernel(x_ref, g_ref, op_ref, oa_ref, od_ref, on_ref, wg_ref, bg_ref, wb_ref, wo_ref, o_ref):
    x = x_ref[...]
    h = _rms(x, g_ref[...]).astype(BF16)
    y = jnp.zeros(x.shape, F32)
    for i, br_ref in enumerate((op_ref, oa_ref, od_ref, on_ref)):
        gate = jax.nn.sigmoid(jnp.dot(h, wg_ref[i], preferred_element_type=F32) + bg_ref[i])
        y = y + gate * jnp.dot(br_ref[...], wb_ref[i], preferred_element_type=F32)
    o_ref[...] = x + _dot(y, wo_ref[...])


def _combine(x2, g, branches, wg, bg, wb, wo):
    t = x2.shape[0]
    tm = TM_COMB
    return pl.pallas_call(
        _combine_kernel,
        grid=(t // tm,),
        in_specs=[pl.BlockSpec((tm, D_MODEL), lambda i: (i, 0)),
                  pl.BlockSpec((1, D_MODEL), lambda i: (0, 0))]
                 + [pl.BlockSpec((tm, MIX_W), lambda i: (i, 0))] * N_BRANCH
                 + [pl.BlockSpec((N_BRANCH, D_MODEL, D_MODEL), lambda i: (0, 0, 0)),
                    pl.BlockSpec((N_BRANCH, 1, D_MODEL), lambda i: (0, 0, 0)),
                    pl.BlockSpec((N_BRANCH, MIX_W, D_MODEL), lambda i: (0, 0, 0)),
                    pl.BlockSpec((D_MODEL, D_MODEL), lambda i: (0, 0))],
        out_specs=pl.BlockSpec((tm, D_MODEL), lambda i: (i, 0)),
        out_shape=jax.ShapeDtypeStruct((t, D_MODEL), F32),
        compiler_params=_cparams(("parallel",)),
        name="combine",
    )(x2, g, *branches, wg, bg, wb, wo)


def _memkv_kernel(m_ref, g_ref, w_ref, o_ref):
    o_ref[0] = _dot(_rms(m_ref[0], g_ref[...]), w_ref[...]).astype(o_ref.dtype)


def _memkv(mem, g, w):
    b, m, _ = mem.shape
    n = 2 * X_HEADS * X_DH
    return pl.pallas_call(
        _memkv_kernel,
        grid=(b,),
        in_specs=[pl.BlockSpec((1, m, D_MODEL), lambda i: (i, 0, 0)),
                  pl.BlockSpec((1, D_MODEL), lambda i: (0, 0)),
                  pl.BlockSpec((D_MODEL, n), lambda i: (0, 0))],
        out_specs=pl.BlockSpec((1, m, n), lambda i: (i, 0, 0)),
        out_shape=jax.ShapeDtypeStruct((b, m, n), BF16),
        compiler_params=_cparams(("parallel",)),
        name="mem_kv",
    )(mem, g, w)


def _cross_kernel(x_ref, g_ref, wq_ref, kv_ref, wo_ref, o_ref):
    n_k = X_HEADS * X_DH
    subs = [slice(i * CROSS_SUB, (i + 1) * CROSS_SUB) for i in range(x_ref.shape[1] // CROSS_SUB)]
    x = [x_ref[0, r, :] for r in subs]
    q = [_dot(_rms(a, g_ref[...]), wq_ref[...]) * X_DH ** -0.5 for a in x]
    pairs = [(i, h) for h in range(X_HEADS) for i in range(len(subs))]
    head = lambda h: slice(h * X_DH, (h + 1) * X_DH)
    sc = [_dot_nt(q[i][:, head(h)], kv_ref[0, :, head(h)]) for i, h in pairs]
    e = [jnp.exp(s - jnp.max(s, axis=-1, keepdims=True)) for s in sc]
    o = [_dot(e_ih, kv_ref[0, :, n_k + h * X_DH:n_k + (h + 1) * X_DH]) / jnp.sum(e_ih, axis=-1, keepdims=True)
         for e_ih, (i, h) in zip(e, pairs)]
    for i, r in enumerate(subs):
        o_i = jnp.concatenate([o[pairs.index((i, h))] for h in range(X_HEADS)], axis=1)
        o_ref[0, r, :] = x[i] + _dot(o_i, wo_ref[...])


def _cross(x3, g, wq, kv, wo):
    b, s, _ = x3.shape
    tm = TM_CROSS
    m = kv.shape[1]
    n_k = X_HEADS * X_DH
    return pl.pallas_call(
        _cross_kernel,
        grid=(b, s // tm),
        in_specs=[pl.BlockSpec((1, tm, D_MODEL), lambda i, j: (i, j, 0)),
                  pl.BlockSpec((1, D_MODEL), lambda i, j: (0, 0)),
                  pl.BlockSpec((D_MODEL, n_k), lambda i, j: (0, 0)),
                  pl.BlockSpec((1, m, 2 * n_k), lambda i, j: (i, 0, 0)),
                  pl.BlockSpec((n_k, D_MODEL), lambda i, j: (0, 0))],
        out_specs=pl.BlockSpec((1, tm, D_MODEL), lambda i, j: (i, j, 0)),
        out_shape=jax.ShapeDtypeStruct((b, s, D_MODEL), F32),
        compiler_params=_cparams(("parallel", "parallel")),
        name="cross_attn",
    )(x3, g, wq, kv, wo)


def _ffn_kernel(x_ref, g_ref, wup_ref, cw_ref, cb_ref, wd_ref, gf_ref, o_ref, tail_ref, act_ref, *, final):
    @pl.when(pl.program_id(1) == 0)
    def _():
        tail_ref[...] = jnp.zeros_like(tail_ref)

    ts = x_ref.shape[1]
    x = x_ref[0]
    hn = _rms(x, g_ref[...]).astype(BF16)
    for c in range(D_FF // FF_CHUNK):
        cols = slice(c * FF_CHUNK, (c + 1) * FF_CHUNK)
        gcols = slice(D_FF + c * FF_CHUNK, D_FF + (c + 1) * FF_CHUNK)
        u = jnp.dot(hn, wup_ref[:, cols], preferred_element_type=F32)
        v = jnp.dot(hn, wup_ref[:, gcols], preferred_element_type=F32)
        ext = jnp.concatenate([tail_ref[:, cols], u], axis=0)
        tail_ref[:, cols] = u[ts - SUBLANE:, :]
        cw = cw_ref[:, cols]
        y = (cw[2:3] * u + cw[1:2] * pltpu.roll(ext, 1, 0)[SUBLANE:]
             + cw[0:1] * pltpu.roll(ext, 2, 0)[SUBLANE:] + cb_ref[:, cols])
        act_ref[:, cols] = (jax.nn.gelu(y) * v).astype(BF16)
    out = x + jnp.dot(act_ref[...], wd_ref[...], preferred_element_type=F32)
    if final:
        out = _rms(out, gf_ref[...])
    o_ref[0] = out


def _ffn(x3, g, wup, cw, cb, wd, gf, final):
    b, s, _ = x3.shape
    ts = TS_FFN
    return pl.pallas_call(
        functools.partial(_ffn_kernel, final=final),
        grid=(b, s // ts),
        in_specs=[pl.BlockSpec((1, ts, D_MODEL), lambda i, j: (i, j, 0)),
                  pl.BlockSpec((1, D_MODEL), lambda i, j: (0, 0)),
                  pl.BlockSpec((D_MODEL, 2 * D_FF), lambda i, j: (0, 0), pipeline_mode=pl.Buffered(1)),
                  pl.BlockSpec((FFN_CONV, D_FF), lambda i, j: (0, 0)),
                  pl.BlockSpec((1, D_FF), lambda i, j: (0, 0)),
                  pl.BlockSpec((D_FF, D_MODEL), lambda i, j: (0, 0), pipeline_mode=pl.Buffered(1)),
                  pl.BlockSpec((1, D_MODEL), lambda i, j: (0, 0))],
        out_specs=pl.BlockSpec((1, ts, D_MODEL), lambda i, j: (i, j, 0)),
        out_shape=jax.ShapeDtypeStruct((b, s, D_MODEL), F32),
        scratch_shapes=[pltpu.VMEM((SUBLANE, D_FF), F32), pltpu.VMEM((ts, D_FF), BF16)],
        compiler_params=_cparams(("parallel", "arbitrary")),
        name="conv_ffn",
    )(x3, g, wup, cw, cb, wd, gf)


def _inproj_columns():
    starts = np.concatenate([[0], np.cumsum(IN_SPLITS)])
    (p_in, a_q, a_k, a_v, a_r, a_lr, d_q, d_k, d_v, d_b, d_a, d_g,
     n_q, n_kc, n_vc, n_ks, n_vs, n_kw, n_vw, n_g) = [np.arange(starts[i], starts[i + 1])
                                                      for i in range(len(IN_SPLITS))]
    n_q = n_q.reshape(NSA_GROUPS, NSA_HPG, NSA_DH).transpose(1, 0, 2).reshape(-1)
    misc = np.full((ZM_W,), N_IN)
    misc[MISC_LR:MISC_LR + GLA_LOWRANK] = a_lr
    misc[MISC_B:MISC_B + GDN_HEADS] = d_b
    misc[MISC_A:MISC_A + GDN_HEADS] = d_a
    misc[MISC_G:MISC_G + 3 * NSA_HEADS] = n_g
    cols = np.concatenate([p_in, a_q, a_k, a_v, a_r, d_q, d_k, d_v, d_g,
                           n_q, n_ks, n_vs, n_kw, n_vw, n_kc, n_vc, misc])
    assert cols.shape[0] == Z_W
    return cols


def _head_expand(offset, n_heads, width):
    e = np.zeros((ZM_W, n_heads * width), np.float32)
    for h in range(n_heads):
        e[offset + h, h * width:(h + 1) * width] = 1.0
    return e


def _nsa_constants(s):
    n_cmp = s // NSA_CMP_STRIDE - NSA_CMP_LEN // NSA_CMP_STRIDE + 1
    n_slc = s // NSA_SEL_LEN
    c_start = np.arange(n_cmp) * NSA_CMP_STRIDE
    s_start = np.arange(n_slc) * NSA_SEL_LEN
    cover = np.zeros((N_CMP_PAD, N_SLC_PAD), np.float32)
    cover[:n_cmp, :n_slc] = ((c_start[:, None] <= s_start[None, :] + NSA_SEL_LEN - 1)
                             & (c_start[:, None] + NSA_CMP_LEN - 1 >= s_start[None, :]))
    gexp = np.zeros((ZM_W, 3, MIX_W), np.float32)
    for g in range(NSA_GROUPS):
        for j in range(NSA_HPG):
            slot = j * NSA_GROUPS + g
            for c in range(3):
                gexp[MISC_G + (g * NSA_HPG + j) * 3 + c, c, slot * NSA_DH:(slot + 1) * NSA_DH] = 1.0
    return jnp.asarray(cover.T, dtype=BF16), jnp.asarray(gexp.reshape(ZM_W, 3 * MIX_W), dtype=BF16)


def _block_diag(blocks):
    n, a, b = blocks.shape
    return jnp.einsum('gh,gab->gahb', jnp.eye(n, dtype=blocks.dtype), blocks).reshape(n * a, n * b)


def kernel(x, mem, g_mix, w_in, pool_w, pool_scale, gla_w_lr, gla_b_lr, gla_g_norm, gdn_conv, gdn_a_log,
           gdn_dt_bias, gdn_g_norm, nsa_pe, nsa_cmp_w1, nsa_cmp_w2, w_branch, w_gate, b_gate, w_out, g_cross,
           g_mem, w_xq, w_mem_kv, w_xo, g_ffn, w_up, ffn_conv, ffn_conv_b, w_down, g_final):
    b, s, d = x.shape
    depth = w_in.shape[0]
    t = b * s
    cols = _inproj_columns()
    covt, gexp = _nsa_constants(s)
    eba = jnp.asarray(np.concatenate([_head_expand(MISC_B, GDN_HEADS, GDN_DH),
                                      _head_expand(MISC_A, GDN_HEADS, GDN_DH)], axis=1), dtype=BF16)
    ones_bd = _block_diag(jnp.ones((GDN_HEADS, GDN_DH, GDN_DH), BF16))
    nsa_rows = np.arange(MIX_W).reshape(NSA_GROUPS, NSA_HPG, NSA_DH).transpose(1, 0, 2).reshape(-1)
    row = lambda v: v.reshape(1, -1).astype(F32)
    misc_a = lambda v: jnp.zeros((1, ZM_W), F32).at[0, MISC_A:MISC_A + GDN_HEADS].set(v)

    x2 = x.reshape(t, d)
    for l in range(depth):
        w_in_r = jnp.concatenate([w_in[l], jnp.zeros((d, 1), F32)], axis=1)[:, cols].astype(BF16)
        wlr = jnp.zeros((ZM_W, GLA_HEADS * GLA_DK), F32).at[MISC_LR:MISC_LR + GLA_LOWRANK].set(gla_w_lr[l])
        wlr_hi = wlr.astype(BF16)
        wlr2 = jnp.stack([wlr_hi, (wlr - wlr_hi.astype(F32)).astype(BF16)])
        o_pool, zn, zc, zm, o_gla, o_gdn = _mixers(
            x2.reshape(b, s, d), row(g_mix[l]), w_in_r, _block_diag(pool_w[l]).astype(BF16), row(pool_scale[l]),
            wlr2, row(gla_b_lr[l]), row(jnp.tile(gla_g_norm[l], GLA_HEADS)),
            gdn_conv[l], eba, misc_a(gdn_a_log[l]), misc_a(gdn_dt_bias[l]),
            row(jnp.tile(gdn_g_norm[l], GDN_HEADS)), ones_bd)

        pe_x = jnp.tile(nsa_pe[l], (1, 1, NSA_GROUPS))
        w1 = nsa_cmp_w1[l].reshape(2, NSA_CMP_LEN, NSA_DH, NSA_DH)
        eye_g = jnp.eye(NSA_GROUPS, dtype=F32)
        w1_bd = jnp.einsum('gh,kpde->kpgdhe', eye_g, w1).reshape(2, NSA_CMP_LEN, LANE, LANE).astype(BF16)
        w2_bd = jnp.einsum('gh,kde->kgdhe', eye_g, nsa_cmp_w2[l]).reshape(2, LANE, LANE).astype(BF16)
        ck, cv = _nsa_compress(zc, pe_x, w1_bd, w2_bd)
        o_nsa = _nsa(zn, zm, ck, cv, covt, gexp)

        wb = jnp.concatenate([w_branch[l, :3], w_branch[l, 3][nsa_rows][None]], axis=0).astype(BF16)
        branches = [o.reshape(t, MIX_W) for o in (o_pool, o_gla, o_gdn, o_nsa)]
        x2 = _combine(x2, row(g_mix[l]), branches, w_gate[l].astype(BF16),
                      b_gate[l].reshape(N_BRANCH, 1, d), wb, w_out[l].astype(BF16))

        kv = _memkv(mem, row(g_mem[l]), w_mem_kv[l].astype(BF16))
        x3 = _cross(x2.reshape(b, s, d), row(g_cross[l]), w_xq[l].astype(BF16), kv, w_xo[l].astype(BF16))

        x3 = _ffn(x3, row(g_ffn[l]), w_up[l].astype(BF16), ffn_conv[l], row(ffn_conv_b[l]),
                  w_down[l].astype(BF16), row(g_final), final=(l == depth - 1))
        x2 = x3.reshape(t, d)
    return x2.reshape(b, s, d)
```

```python
import functools

import numpy as np
import jax
import jax.numpy as jnp
from jax import lax
from jax.experimental import pallas as pl
from jax.experimental.pallas import tpu as pltpu

F32 = jnp.float32
BF16 = jnp.bfloat16

D_MODEL = 1024
MIX_W = 256
POOL_WINDOWS = (2, 4, 8, 16)
POOL_GW = 64
GLA_HEADS = 4
GLA_DK = 32
GLA_DV = 64
GLA_LOWRANK = 16
GLA_GATE_NORM = 16.0
CHUNK = 64
GDN_HEADS = 4
GDN_DH = 64
GDN_CONV = 4
NSA_HEADS = 4
NSA_GROUPS = 2
NSA_HPG = 2
NSA_DH = 64
NSA_KV = 128
NSA_CMP_LEN = 32
NSA_CMP_STRIDE = 16
NSA_SEL_LEN = 64
NSA_N_SEL = 16
NSA_WINDOW = 512
X_HEADS = 4
X_DH = 128
D_FF = 2816
FFN_CONV = 3
EPS = 1e-6
N_BRANCH = 4

IN_SPLITS = (MIX_W,
             128, 128, 256, 256, GLA_LOWRANK,
             MIX_W, MIX_W, MIX_W, GDN_HEADS, GDN_HEADS, MIX_W,
             256, NSA_KV, NSA_KV, NSA_KV, NSA_KV, NSA_KV, NSA_KV, 3 * NSA_HEADS)
N_IN = sum(IN_SPLITS)

MISC_LR = 0
MISC_B = 16
MISC_A = 20
MISC_G = 24
LANE = 128
SUBLANE = 8

Z_WIDTHS = (256, 768, 1024, 768, 256, 128)
GDN_QKV_W = 3 * MIX_W
ZP_W, ZG_W, ZD_W, ZN_W, ZC_W, ZM_W = Z_WIDTHS
Z_W = sum(Z_WIDTHS)
NEG_BIG = -1e30
F32_LOWEST = float(np.finfo(np.float32).min)

TM_COMB = 512
TM_CROSS = 1024
CROSS_SUB = 512
TS_FFN = 1024
FF_CHUNK = 256
TQ_NSA = 256
NSA_SEQS = 2
REC_SEQS = 16
REC_LAG = 3
GLA_START = 6
N_SLC_PAD = 128
N_CMP_PAD = 128

VMEM_LIMIT = 56 * 1024 * 1024


def _cparams(sem):
    return pltpu.CompilerParams(dimension_semantics=sem, vmem_limit_bytes=VMEM_LIMIT)


def _rms(x, g):
    return x * lax.rsqrt(jnp.mean(x * x, axis=-1, keepdims=True) + EPS) * g


def _dot(a, b):
    return jnp.dot(a.astype(BF16), b.astype(BF16), preferred_element_type=F32)


def _dot_nt(a, b):
    return lax.dot_general(a.astype(BF16), b.astype(BF16), (((1,), (1,)), ((), ())),
                           preferred_element_type=F32)


def _dot_tn(a, b):
    return lax.dot_general(a.astype(BF16), b.astype(BF16), (((0,), (0,)), ((), ())),
                           preferred_element_type=F32)


def _split2_dot(a, b):
    hi = a.astype(BF16)
    lo = (a - hi.astype(F32)).astype(BF16)
    return jnp.dot(hi, b, preferred_element_type=F32) + jnp.dot(lo, b, preferred_element_type=F32)


def _split3_lhs_dot(a, b):
    hi = a.astype(BF16)
    r1 = a - hi.astype(F32)
    mid = r1.astype(BF16)
    lo = (r1 - mid.astype(F32)).astype(BF16)
    return ((jnp.dot(hi, b, preferred_element_type=F32) + jnp.dot(mid, b, preferred_element_type=F32))
            + jnp.dot(lo, b, preferred_element_type=F32))


def _dot3(a, w_hi, w_lo):
    a_hi = a.astype(BF16)
    a_lo = (a - a_hi.astype(F32)).astype(BF16)
    return (jnp.dot(a_hi, w_hi, preferred_element_type=F32)
            + (jnp.dot(a_lo, w_hi, preferred_element_type=F32) + jnp.dot(a_hi, w_lo, preferred_element_type=F32)))


def _iota(shape, axis):
    return lax.broadcasted_iota(jnp.int32, shape, axis)


def _block_mask(rows, cols, rb, cb):
    return (_iota((rows, cols), 0) // rb) == (_iota((rows, cols), 1) // cb)


def _shift_rows(x, k):
    t = _iota(x.shape, 0)
    return jnp.where(t >= k, pltpu.roll(x, k, 0), 0.0)


def _cumsum_rows(x):
    k = 1
    while k < x.shape[0]:
        x = x + _shift_rows(x, k)
        k *= 2
    return x


def _softplus(x):
    return jnp.maximum(x, 0.0) + jnp.log1p(jnp.exp(-jnp.abs(x)))


def _log_sigmoid(x):
    return -_softplus(-x)


def _silu(x):
    return x * jax.nn.sigmoid(x)


def _masked_softmax(s, mask):
    s = jnp.where(mask, s, -jnp.inf)
    m = jnp.maximum(jnp.max(s, axis=-1, keepdims=True), F32_LOWEST)
    e = jnp.exp(s - m)
    den = jnp.sum(e, axis=-1, keepdims=True)
    return e / jnp.where(den > 0, den, 1.0)


def _tile4(x):
    return jnp.concatenate([x, x, x, x], axis=0)


POOL_HALO = 16


def _gla_masks():
    return (_block_mask(4 * CHUNK, 128, CHUNK, GLA_DK),
            _block_mask(4 * CHUNK, 256, CHUNK, GLA_DV),
            _block_mask(256, 128, GLA_DV, GLA_DK),
            (_iota((CHUNK, 256), 1) % CHUNK) <= _iota((CHUNK, 256), 0))


def _seqs(f, *lists):
    return [f(*args) for args in zip(*lists)]


def _run_staggered(stage_gens, starts, periods):
    live = list(range(len(stage_gens)))
    tick = 0
    while live:
        for i in list(live):
            if tick >= starts[i] and (tick - starts[i]) % periods[i] == 0:
                try:
                    next(stage_gens[i])
                except StopIteration:
                    live.remove(i)
        tick += 1


def _gla_chunk(masks, zqk, v, r, zm, wlr_ref, blr_ref, gn_ref, ones_ref, st_refs, out):
    mask_k, mask_v, mask_st, causal = masks
    q = [z[:, 0:128] * GLA_DK ** -0.5 for z in zqk]
    k = [z[:, 128:256] for z in zqk]
    pre = _seqs(lambda a: _dot3(a, wlr_ref[0], wlr_ref[1]), zm)
    yield
    bc = _seqs(lambda a: _cumsum_rows(_log_sigmoid(a + blr_ref[...]) / GLA_GATE_NORM), pre)
    bl = [a[CHUNK - 1:CHUNK, :] for a in bc]
    q_e = _seqs(lambda a, c: a * jnp.exp(c), q, bc)
    k_e = _seqs(lambda a, c: a * jnp.exp(-c), k, bc)
    k_u = _seqs(lambda a, c, l: a * jnp.exp(l - c), k, bc, bl)
    st = [ref[...] for ref in st_refs]
    yield
    att = _seqs(lambda a, b: jnp.where(causal, _dot_nt(a, jnp.where(mask_k, _tile4(b), 0.0)), 0.0), q_e, k_e)
    inter = _seqs(_dot_nt, q_e, st)
    kv = _seqs(_dot_tn, v, k_u)
    yield
    o = _seqs(lambda a, b, c: _dot(a, _tile4(b.astype(BF16)) * ones_ref[...]) + c, att, v, inter)
    for ref, s_old, l, upd in zip(st_refs, st, bl, kv):
        ref[...] = s_old * jnp.exp(l) + jnp.where(mask_st, upd, 0.0)
    yield
    ms = _seqs(lambda a: _dot(a * a, ones_ref[...]) * (1.0 / GLA_DV), o)
    yield
    out[:] = _seqs(lambda a, m, g: a * lax.rsqrt(m + EPS) * gn_ref[...] * _silu(g), o, ms, r)


GDN_HALO = SUBLANE


def _gdn_masks():
    c4 = 4 * CHUNK
    col = _iota((CHUNK, c4), 1) % CHUNK
    row = _iota((CHUNK, c4), 0)
    return (_block_mask(c4, c4, CHUNK, CHUNK), col <= row, col < row, col == row)


def _gdn_chunk(masks, halo, zqkv, gate, zm, cw_ref, eba_ref, alog_ref, dtb_ref, gn_ref, ones_ref, st_refs, out):
    mask_bd, incl, strict, diag = masks
    cw = cw_ref[...]
    bd = lambda a: _tile4(a.astype(BF16)) * ones_ref[...]

    def conv_silu(h, cur):
        ext = jnp.concatenate([h, cur], axis=0)
        conv = (cw[3:4] * cur + cw[2:3] * pltpu.roll(ext, 1, 0)[GDN_HALO:]
                + cw[1:2] * pltpu.roll(ext, 2, 0)[GDN_HALO:] + cw[0:1] * pltpu.roll(ext, 3, 0)[GDN_HALO:])
        return _silu(conv)

    qkv = _seqs(conv_silu, halo, zqkv)
    yield
    v = [a[:, 512:768] for a in qkv]
    ssq = _seqs(lambda a: _dot(jnp.concatenate([a[:, 0:256] * a[:, 0:256], a[:, 256:512] * a[:, 256:512]],
                                               axis=0), ones_ref[...]), qkv)
    yield
    q = _seqs(lambda a, s: a[:, 0:256] * lax.rsqrt(s[0:CHUNK] + EPS) * GDN_DH ** -0.5, qkv, ssq)
    k = _seqs(lambda a, s: a[:, 256:512] * lax.rsqrt(s[CHUNK:2 * CHUNK] + EPS), qkv, ssq)
    beta = _seqs(lambda a: _dot(jax.nn.sigmoid(a), eba_ref[:, 0:MIX_W]), zm)
    yield
    gc = _seqs(lambda a: _split3_lhs_dot(_cumsum_rows(-jnp.exp(alog_ref[...]) * _softplus(a + dtb_ref[...])),
                                         eba_ref[:, MIX_W:2 * MIX_W]), zm)
    yield
    gl = [a[CHUNK - 1:CHUNK, :] for a in gc]
    g_row = _seqs(lambda a: jnp.sum(jnp.where(diag, a, 0.0), axis=0, keepdims=True), gc)
    yield
    decay =_seqs(lambda a, b: jnp.exp(jnp.where(incl, a - b, -jnp.inf)), gc, g_row)
    kb = _seqs(lambda a, b: a * b, k, beta)
    vb = _seqs(lambda a, b: a * b, v, beta)
    kq = _seqs(lambda a, b, c: _dot_nt(jnp.concatenate([a, b], axis=0), bd(c)), kb, q, k)
    yield
    n_mat =_seqs(lambda s, d: jnp.where(strict, s[0:CHUNK] * d, 0.0), kq, decay)
    a_qk = _seqs(lambda s, d: s[CHUNK:2 * CHUNK] * d, kq, decay)
    m = _seqs(lambda a: -a, n_mat)
    x = _seqs(lambda a: jnp.where(diag, 1.0, 0.0) + a, m)
    m = _seqs(lambda a: _dot(a, bd(a)), m)
    yield
    p = 2
    while p < CHUNK // 2:
        xm = _seqs(lambda a, b: _dot(jnp.concatenate([a, b], axis=0), bd(b)), x, m)
        x = _seqs(lambda a, r: a + r[0:CHUNK], x, xm)
        m = [r[CHUNK:2 * CHUNK] for r in xm]
        p *= 2
        yield
    x = _seqs(lambda a, b: a + _dot(a, bd(b)), x, m)
    yield
    uw = _seqs(lambda a, b, c, g: _dot(a, jnp.concatenate([bd(b), bd(c * jnp.exp(g))], axis=1)),
               x, vb, kb, gc)
    yield
    st = [ref[...] for ref in st_refs]
    ws = _seqs(lambda a, b, g, s: _dot(jnp.concatenate([a[:, 256:512], b * jnp.exp(g)], axis=0), s),
               uw, q, gc, st)
    yield
    v_new = _seqs(lambda a, r: a[:, 0:256] - r[0:CHUNK], uw, ws)
    o = _seqs(lambda r, qk, vn: r[CHUNK:2 * CHUNK] + _dot(qk, bd(vn)), ws, a_qk, v_new)
    upd = _seqs(lambda a, g, l, vn: _dot_tn(a * jnp.exp(l - g), vn), k, gc, gl, v_new)
    yield
    for ref, s_old, l, u in zip(st_refs, st, gl, upd):
        ref[...] = s_old * jnp.exp(l) + jnp.where(mask_bd, u, 0.0)
    ms = _seqs(lambda a: _dot(a * a, ones_ref[...]) * (1.0 / GDN_DH), o)
    yield
    out[:] = _seqs(lambda a, m_, g: a * lax.rsqrt(m_ + EPS) * gn_ref[...] * _silu(g), o, ms, gate)


def _pool_diff(u, tail, pos0):
    ext = jnp.concatenate([tail, u], axis=0)
    s2 = ext + pltpu.roll(ext, 1, 0)
    s4 = s2 + pltpu.roll(s2, 2, 0)
    s8 = s4 + pltpu.roll(s4, 4, 0)
    s16 = s8 + pltpu.roll(s8, 8, 0)
    grp = _iota(u.shape, 1) // POOL_GW
    win = jnp.where(grp == 0, s2[POOL_HALO:], jnp.where(grp == 1, s4[POOL_HALO:],
                                                        jnp.where(grp == 2, s8[POOL_HALO:], s16[POOL_HALO:])))
    width = jnp.where(grp == 0, POOL_WINDOWS[0],
                      jnp.where(grp == 1, POOL_WINDOWS[1],
                                jnp.where(grp == 2, POOL_WINDOWS[2], POOL_WINDOWS[3])))
    cnt = jnp.minimum(pos0 + _iota(u.shape, 0) + 1, width).astype(F32)
    return win / cnt - u


def _mix_kernel(x_ref, g_ref, w_ref, pw_ref, ps_ref, wlr_ref, blr_ref, gng_ref, cw_ref, eba_ref, alog_ref, dtb_ref,
                gnd_ref, ones_ref, op_ref, zn_ref, zc_ref, zm_ref, og_ref, od_ref, ptail_ref, halo_ref, *st_refs):
    nb = x_ref.shape[0]
    chunk = pl.program_id(1)

    @pl.when(chunk == 0)
    def _():
        ptail_ref[...] = jnp.zeros_like(ptail_ref)
        halo_ref[...] = jnp.zeros_like(halo_ref)
        for st_ref in st_refs:
            st_ref[...] = jnp.zeros_like(st_ref)

    h = _rms(x_ref[...].reshape(nb * CHUNK, D_MODEL), g_ref[...]).astype(BF16)
    proj = lambda lo, n: jnp.dot(h, w_ref[:, lo:lo + n], preferred_element_type=F32)
    seqs = lambda a: [a[i * CHUNK:(i + 1) * CHUNK] for i in range(nb)]
    off_g, off_d = ZP_W, ZP_W + ZG_W
    off_n = off_d + ZD_W
    off_c = off_n + ZN_W
    off_m = off_c + ZC_W
    zqkv = seqs(proj(off_d, GDN_QKV_W))
    zm = seqs(proj(off_m, ZM_W))
    halo = [halo_ref[i] for i in range(nb)]
    for i in range(nb):
        halo_ref[i] = zqkv[i][CHUNK - GDN_HALO:, :]
        zm_ref[i] = zm[i]
    halves = [slice(0, nb // 2), slice(nb // 2, nb)] if nb > 1 else [slice(0, nb)]
    gla_qk, gla_v, gla_r = [], [], []
    gate_parts = [[] for _ in halves]

    def other_projections():
        gla_qk.extend(seqs(proj(off_g, 256)))
        yield
        gla_v.extend(seqs(proj(off_g + 256, 256)))
        yield
        u = seqs(proj(0, ZP_W))
        yield
        diff = [_pool_diff(u[i], ptail_ref[i], chunk * CHUNK) for i in range(nb)]
        for i in range(nb):
            ptail_ref[i] = u[i][CHUNK - POOL_HALO:, :]
        yield
        for lo in range(0, ZN_W, 256):
            zn = seqs(proj(off_n + lo, 256))
            for i in range(nb):
                zn_ref[i, :, lo:lo + 256] = zn[i].astype(zn_ref.dtype)
            yield
        zc = seqs(proj(off_c, ZC_W))
        for i in range(nb):
            zc_ref[i] = zc[i]
        yield
        gla_r.extend(seqs(proj(off_g + 512, 256)))
        yield
        gate = seqs(proj(off_d + GDN_QKV_W, MIX_W))
        for hs, part in zip(halves, gate_parts):
            part.extend(gate[hs])
        yield
        o_pool = seqs(_dot(jnp.concatenate(diff, axis=0), pw_ref[...]) * ps_ref[...])
        for i in range(nb):
            op_ref[i] = o_pool[i].astype(op_ref.dtype)

    og = [None] * nb
    od_parts = [[None] * (hs.stop - hs.start) for hs in halves]
    gdn_masks = _gdn_masks()
    gens = [_gdn_chunk(gdn_masks, halo[hs], zqkv[hs], gate_part, zm[hs], cw_ref, eba_ref, alog_ref, dtb_ref,
                       gnd_ref, ones_ref, st_refs[nb:][hs], part)
            for hs, part, gate_part in zip(halves, od_parts, gate_parts)]
    gens.append(_gla_chunk(_gla_masks(), gla_qk, gla_v, gla_r, zm, wlr_ref, blr_ref, gng_ref, ones_ref,
                           st_refs[:nb], og))
    starts = [REC_LAG * i for i in range(len(halves))] + [GLA_START]
    gens.append(other_projections())
    _run_staggered(gens, starts + [0], [1] * len(gens))
    od = [o for part in od_parts for o in part]
    for i in range(nb):
        og_ref[i] = og[i].astype(og_ref.dtype)
        od_ref[i] = od[i].astype(od_ref.dtype)


def _mixers(x3, g, w, pool_w_bd, pool_scale, wlr2, blr, gn_gla, cw, eba, alog, dtb, gn_gdn, ones_bd):
    b, s, _ = x3.shape
    nb = REC_SEQS if b % REC_SEQS == 0 else 1
    const = lambda shape: pl.BlockSpec(shape, lambda i, c: (0,) * len(shape))
    tok = lambda n: pl.BlockSpec((nb, CHUNK, n), lambda i, c: (i, c, 0))
    outs = ((MIX_W, BF16), (ZN_W, BF16), (ZC_W, F32), (ZM_W, F32), (MIX_W, BF16), (MIX_W, BF16))
    return pl.pallas_call(
        _mix_kernel,
        grid=(b // nb, s // CHUNK),
        in_specs=[tok(D_MODEL), const((1, D_MODEL)), const((D_MODEL, Z_W)), const((MIX_W, MIX_W)), const((1, MIX_W)),
                  const((2, ZM_W, 128)), const((1, 128)), const((1, MIX_W)),
                  const((GDN_CONV, GDN_QKV_W)), const((ZM_W, 2 * MIX_W)), const((1, ZM_W)), const((1, ZM_W)),
                  const((1, MIX_W)), const((MIX_W, MIX_W))],
        out_specs=[tok(n) for n, _ in outs],
        out_shape=[jax.ShapeDtypeStruct((b, s, n), dt) for n, dt in outs],
        scratch_shapes=([pltpu.VMEM((nb, POOL_HALO, MIX_W), F32), pltpu.VMEM((nb, GDN_HALO, GDN_QKV_W), F32)]
                        + [pltpu.VMEM((256, 128), F32)] * nb + [pltpu.VMEM((256, 256), F32)] * nb),
        compiler_params=_cparams(("parallel", "arbitrary")),
        name="inproj_mixers",
    )(x3, g, w, pool_w_bd, pool_scale, wlr2, blr, gn_gla, cw, eba, alog, dtb, gn_gdn, ones_bd)


def _cmp_kernel(kc_ref, vc_ref, pe_ref, w1_ref, w2_ref, ck_ref, cv_ref):
    n_sub = NSA_CMP_LEN // NSA_CMP_STRIDE
    n_chunks = kc_ref.shape[1] // NSA_CMP_STRIDE
    for which, (src_ref, out_ref) in enumerate(((kc_ref, ck_ref), (vc_ref, cv_ref))):
        parts = [jnp.zeros((n_chunks, LANE), F32) for _ in range(n_sub)]
        for i in range(NSA_CMP_STRIDE):
            slab = src_ref[0, pl.ds(i, n_chunks, stride=NSA_CMP_STRIDE), :]
            for sub in range(n_sub):
                p = sub * NSA_CMP_STRIDE + i
                parts[sub] = parts[sub] + _dot(slab + pe_ref[which, p:p + 1, :], w1_ref[which, p])
        pre = parts[0] + pltpu.roll(parts[1], n_chunks - 1, 0)
        out_ref[0] = _dot(jax.nn.gelu(pre), w2_ref[which])


def _nsa_compress(zc, pe_x, w1_bd, w2_bd):
    b, s, _ = zc.shape
    n_chunks = s // NSA_CMP_STRIDE
    return pl.pallas_call(
        _cmp_kernel,
        grid=(b,),
        in_specs=[pl.BlockSpec((1, s, LANE), lambda i: (i, 0, 0)),
                  pl.BlockSpec((1, s, LANE), lambda i: (i, 0, 1)),
                  pl.BlockSpec((2, NSA_CMP_LEN, LANE), lambda i: (0, 0, 0)),
                  pl.BlockSpec((2, NSA_CMP_LEN, LANE, LANE), lambda i: (0, 0, 0, 0)),
                  pl.BlockSpec((2, LANE, LANE), lambda i: (0, 0, 0))],
        out_specs=[pl.BlockSpec((1, n_chunks, LANE), lambda i: (i, 0, 0))] * 2,
        out_shape=[jax.ShapeDtypeStruct((b, n_chunks, LANE), F32)] * 2,
        compiler_params=_cparams(("parallel",)),
        name="nsa_compress",
    )(zc, zc, pe_x, w1_bd, w2_bd)


def _topk_rank(v):
    n, t = v.shape
    blocks = [v[r:r + SUBLANE] for r in range(0, n, SUBLANE)]
    sub = _iota((SUBLANE, t), 0)
    rank = [jnp.zeros((SUBLANE, t), jnp.int32) for _ in blocks]
    for m in range(n):
        other = v[m:m + 1, :]
        mb, ms = divmod(m, SUBLANE)
        for r, blk in enumerate(blocks):
            if r < mb:
                ahead = (other > blk).astype(jnp.int32)
            elif r > mb:
                ahead = (other >= blk).astype(jnp.int32)
            else:
                ahead = jnp.where(sub > ms, (other >= blk).astype(jnp.int32), (other > blk).astype(jnp.int32))
            rank[r] = rank[r] + ahead
    return jnp.concatenate(rank, axis=0)


def _exp_weights(s, m):
    return jnp.exp((s - m).astype(BF16))


def _normalise_aug(o_aug, in_grp):
    den = pltpu.roll(o_aug, NSA_DH, 1)
    return jnp.where(in_grp, o_aug / jnp.where(den > 0, den, 1.0), 0.0)


def _nsa_kernel(q_ref, ksv_ref, kwv_ref, ck_ref, cv_ref, zm_ref, covt_ref, gexp_ref, o_ref,
                kaug_ref, vaug_ref, vwaug_ref, s_ref, mx_ref, acc_ref):
    nb = q_ref.shape[0]
    tq = q_ref.shape[1]
    s_len = ksv_ref.shape[1]
    n_slc = s_len // NSA_SEL_LEN
    qi = pl.program_id(1)
    s0 = pl.multiple_of(qi * tq, tq)
    lane_grp = _iota((1, LANE), 1) // NSA_DH
    chains = [(bi, g) for bi in range(nb) for g in range(NSA_GROUPS)]
    n_chains = range(len(chains))

    @pl.when(qi == 0)
    def _():
        lane = _iota((s_len, LANE), 1)
        key_blk = _iota((s_len, LANE), 0) // NSA_SEL_LEN
        for bi in range(nb):
            k = ksv_ref[bi, :, 0:LANE].astype(F32)
            v = ksv_ref[bi, :, LANE:2 * LANE].astype(F32)
            vw = kwv_ref[bi, :, LANE:2 * LANE].astype(F32)
            for g in range(NSA_GROUPS):
                own = lane // NSA_DH == g
                onehot = jnp.where(lane - (1 - g) * NSA_DH == key_blk, 1.0, 0.0)
                kaug_ref[bi * NSA_GROUPS + g] = jnp.where(own, k, onehot).astype(BF16)
                vaug_ref[bi * NSA_GROUPS + g] = jnp.where(own, v, 1.0).astype(BF16)
                vwaug_ref[bi * NSA_GROUPS + g] = jnp.where(own, vw, 1.0).astype(BF16)

    t_col = s0 + _iota((tq, 1), 0)
    t_col2 = jnp.concatenate([t_col, t_col], axis=0)
    t_row = s0 + _iota((1, tq), 1)
    q = [q_ref[bi].astype(F32) * NSA_DH ** -0.5 for bi in range(nb)]
    ck_hi = [ck_ref[bi].astype(BF16) for bi in range(nb)]
    ck_lo = [(ck_ref[bi] - ck_hi[bi].astype(F32)).astype(BF16) for bi in range(nb)]
    cv = [cv_ref[bi] for bi in range(nb)]
    cmp_end = _iota((1, N_CMP_PAD), 1) * NSA_CMP_STRIDE + (NSA_CMP_LEN - 1)
    cmp_valid = cmp_end <= t_col2
    blk_t = _iota((n_slc, tq), 0)
    cur_t = t_row // NSA_SEL_LEN
    forced_t = (blk_t == 0) | (blk_t == cur_t) | (blk_t == cur_t - 1)
    future_t = blk_t > cur_t
    row_in_tile = jnp.concatenate([_iota((tq, tq), 0)] * NSA_HPG, axis=0)
    col_in_tile = _iota((NSA_HPG * tq, tq), 1)
    diag_ok = col_in_tile <= row_in_tile
    n_wt = NSA_WINDOW // tq + 1
    never = 2 * tq
    win_off = [pl.multiple_of(jnp.maximum(qi - (n_wt - 1 - w), 0) * tq, tq) for w in range(n_wt)]
    win_ok = [col_in_tile > row_in_tile + jnp.where(qi >= n_wt - 1, 0, never)]
    win_ok += [col_in_tile >= jnp.where(qi >= n_wt - 1 - w, 0, never) for w in range(1, n_wt - 1)]
    win_ok += [diag_ok]

    o_cmp = [[jnp.zeros((tq, LANE), F32) for _ in range(NSA_HPG)] for _ in range(nb)]
    o_slc = [[jnp.zeros((tq, LANE), F32) for _ in range(NSA_HPG)] for _ in range(nb)]
    o_win = [[jnp.zeros((tq, LANE), F32) for _ in range(NSA_HPG)] for _ in range(nb)]
    in_grp = [lane_grp == g for _, g in chains]
    q2 = [jnp.concatenate([jnp.where(in_grp[c], q[bi][:, j * LANE:(j + 1) * LANE], 0.0)
                           for j in range(NSA_HPG)], axis=0) for c, (bi, _) in enumerate(chains)]
    qb = [a.astype(BF16) for a in q2]
    p_c = [_masked_softmax(_dot_nt(qb[c], ck_hi[bi]) + _dot_nt(qb[c], ck_lo[bi]), cmp_valid)
           for c, (bi, _) in enumerate(chains)]
    oc = [jnp.where(in_grp[c], _dot(p_c[c], cv[bi]), 0.0) for c, (bi, _) in enumerate(chains)]
    imp_c = [a[0:tq] + a[tq:2 * tq] for a in p_c]
    imp_hi = [a.astype(BF16) for a in imp_c]
    imp_lo = [(a - h.astype(F32)).astype(BF16) for a, h in zip(imp_c, imp_hi)]
    imp_t = [(_dot_nt(covt_ref[...], h) + _dot_nt(covt_ref[...], l_))[0:n_slc]
             for h, l_ in zip(imp_hi, imp_lo)]
    imp_t = [jnp.where(forced_t, jnp.inf, jnp.where(future_t, -jnp.inf, a)) for a in imp_t]
    s_w = [[jnp.where(ok, _dot_nt(qb[c], kwv_ref[bi, pl.ds(off, tq), 0:LANE]), -jnp.inf)
            for ok, off in zip(win_ok, win_off)] for c, (bi, _) in enumerate(chains)]
    qa = []
    for c, (_, g) in enumerate(chains):
        bias_t = jnp.where(_topk_rank(imp_t[c]) < NSA_N_SEL, 0.0, NEG_BIG)
        lo = (1 - g) * NSA_DH
        rows = ([jnp.zeros((lo, tq), F32)] if lo else []) + [bias_t, jnp.zeros((LANE - lo - n_slc, tq), F32)]
        bias = jnp.concatenate(rows, axis=0).T
        qa.append(jnp.where(in_grp[c], q2[c], jnp.concatenate([bias] * NSA_HPG, axis=0)).astype(BF16))
    m_w = []
    for c in n_chains:
        m_c = s_w[c][0]
        for s_t in s_w[c][1:]:
            m_c = jnp.maximum(m_c, s_t)
        m_w.append(jnp.maximum(jnp.max(m_c, axis=-1, keepdims=True), F32_LOWEST))
    ow = [jnp.zeros((NSA_HPG * tq, LANE), F32) for _ in n_chains]
    for w, off in enumerate(win_off):
        for c in n_chains:
            ow[c] = ow[c] + jnp.dot(_exp_weights(s_w[c][w], m_w[c]), vwaug_ref[c, pl.ds(off, tq), :],
                                    preferred_element_type=F32)
    for c, (bi, _) in enumerate(chains):
        ow_c = _normalise_aug(ow[c], in_grp[c])
        for j in range(NSA_HPG):
            rows_j = slice(j * tq, (j + 1) * tq)
            o_cmp[bi][j] = o_cmp[bi][j] + oc[c][rows_j]
            o_win[bi][j] = o_win[bi][j] + ow_c[rows_j]

    half_max = lambda s_t: jnp.maximum(s_t[:, 0:LANE], s_t[:, LANE:2 * LANE])
    mx_ref[...] = jnp.full(mx_ref.shape, NEG_BIG, F32)
    groups = n_chains

    def score_tiles(tiles):
        offs = [pl.multiple_of(kt * tq, tq) for kt in tiles]
        s_new = [[_dot_nt(qa[g], kaug_ref[g, pl.ds(off, tq), :]) for g in groups] for off in offs]
        for kt, s_kt in zip(tiles, s_new):
            for g in groups:
                s_ref[g, kt] = s_kt[g]
        for g in groups:
            m_new = half_max(s_new[0][g])
            for s_kt in s_new[1:]:
                m_new = jnp.maximum(m_new, half_max(s_kt[g]))
            mx_ref[g] = jnp.maximum(mx_ref[g], m_new)

    def score_pair(p, carry):
        score_tiles([2 * p, 2 * p + 1])
        return carry

    lax.fori_loop(0, qi // 2, score_pair, 0)

    @pl.when(qi % 2 == 1)
    def _():
        score_tiles([qi - 1])

    m_s = []
    for g in groups:
        s_d = jnp.where(diag_ok, _dot_nt(qa[g], kaug_ref[g, pl.ds(s0, tq), :]), NEG_BIG)
        s_ref[g, qi] = s_d
        m_g = jnp.max(jnp.maximum(mx_ref[g], half_max(s_d)), axis=-1, keepdims=True)
        m_s.append(jnp.maximum(m_g, F32_LOWEST))
    acc_ref[...] = jnp.zeros(acc_ref.shape, F32)

    def attend_tiles(tiles):
        offs = [pl.multiple_of(kt * tq, tq) for kt in tiles]
        pv = [[jnp.dot(_exp_weights(s_ref[g, kt], m_s[g]), vaug_ref[g, pl.ds(off, tq), :],
                       preferred_element_type=F32) for g in groups] for kt, off in zip(tiles, offs)]
        for g in groups:
            upd = pv[0][g]
            for pv_kt in pv[1:]:
                upd = upd + pv_kt[g]
            acc_ref[g] += upd

    def attend_pair(p, carry):
        attend_tiles([2 * p, 2 * p + 1])
        return carry

    lax.fori_loop(0, (qi + 1) // 2, attend_pair, 0)

    @pl.when(qi % 2 == 0)
    def _():
        attend_tiles([qi])

    for c, (bi, _) in enumerate(chains):
        os_ = _normalise_aug(acc_ref[c], in_grp[c])
        for j in range(NSA_HPG):
            o_slc[bi][j] = o_slc[bi][j] + os_[j * tq:(j + 1) * tq]

    for bi in range(nb):
        gates = _split2_dot(jax.nn.sigmoid(zm_ref[bi]), gexp_ref[...])
        o = (gates[:, 0:MIX_W] * jnp.concatenate(o_cmp[bi], axis=1)
             + gates[:, MIX_W:2 * MIX_W] * jnp.concatenate(o_slc[bi], axis=1)
             + gates[:, 2 * MIX_W:3 * MIX_W] * jnp.concatenate(o_win[bi], axis=1))
        o_ref[bi] = o.astype(o_ref.dtype)


def _nsa(zn, zm, ck, cv, covt, gexp):
    b, s, _ = zn.shape
    tq = TQ_NSA
    nb = NSA_SEQS if b % NSA_SEQS == 0 else 1
    n_chains = nb * NSA_GROUPS
    return pl.pallas_call(
        _nsa_kernel,
        grid=(b // nb, s // tq),
        in_specs=[pl.BlockSpec((nb, tq, 256), lambda i, j: (i, j, 0)),
                  pl.BlockSpec((nb, s, 256), lambda i, j: (i, 0, 1)),
                  pl.BlockSpec((nb, s, 256), lambda i, j: (i, 0, 2)),
                  pl.BlockSpec((nb, N_CMP_PAD, LANE), lambda i, j: (i, 0, 0)),
                  pl.BlockSpec((nb, N_CMP_PAD, LANE), lambda i, j: (i, 0, 0)),
                  pl.BlockSpec((nb, tq, ZM_W), lambda i, j: (i, j, 0)),
                  pl.BlockSpec((N_SLC_PAD, N_CMP_PAD), lambda i, j: (0, 0)),
                  pl.BlockSpec((ZM_W, 3 * MIX_W), lambda i, j: (0, 0))],
        out_specs=pl.BlockSpec((nb, tq, MIX_W), lambda i, j: (i, j, 0)),
        out_shape=jax.ShapeDtypeStruct((b, s, MIX_W), BF16),
        scratch_shapes=[pltpu.VMEM((n_chains, s, LANE), BF16)] * 3
                       + [pltpu.VMEM((n_chains, s // tq, NSA_HPG * tq, tq), F32),
                          pltpu.VMEM((n_chains, NSA_HPG * tq, LANE), F32),
                          pltpu.VMEM((n_chains, NSA_HPG * tq, LANE), F32)],
        compiler_params=_cparams(("parallel", "arbitrary")),
        name="nsa_attn",
    )(zn, zn, zn, ck, cv, zm, covt, gexp)


def _combine_kernel(x_ref, g_ref, op_ref, oa_ref, od_ref, on_ref, wg_ref, bg_ref, wb_ref, wo_ref, o_ref):
    x = x_ref[...]
    h = _rms(x, g_ref[...]).astype(BF16)
    y = jnp.zeros(x.shape, F32)
    for i, br_ref in enumerate((op_ref, oa_ref, od_ref, on_ref)):
        gate = jax.nn.sigmoid(jnp.dot(h, wg_ref[i], preferred_element_type=F32) + bg_ref[i])
        y = y + gate * jnp.dot(br_ref[...], wb_ref[i], preferred_element_type=F32)
    o_ref[...] = x + _dot(y, wo_ref[...])


def _combine(x2, g, branches, wg, bg, wb, wo):
    t = x2.shape[0]
    tm = TM_COMB
    return pl.pallas_call(
        _combine_kernel,
        grid=(t // tm,),
        in_specs=[pl.BlockSpec((tm, D_MODEL), lambda i: (i, 0)),
                  pl.BlockSpec((1, D_MODEL), lambda i: (0, 0))]
                 + [pl.BlockSpec((tm, MIX_W), lambda i: (i, 0))] * N_BRANCH
                 + [pl.BlockSpec((N_BRANCH, D_MODEL, D_MODEL), lambda i: (0, 0, 0)),
                    pl.BlockSpec((N_BRANCH, 1, D_MODEL), lambda i: (0, 0, 0)),
                    pl.BlockSpec((N_BRANCH, MIX_W, D_MODEL), lambda i: (0, 0, 0)),
                    pl.BlockSpec((D_MODEL, D_MODEL), lambda i: (0, 0))],
        out_specs=pl.BlockSpec((tm, D_MODEL), lambda i: (i, 0)),
        out_shape=jax.ShapeDtypeStruct((t, D_MODEL), F32),
        compiler_params=_cparams(("parallel",)),
        name="combine",
    )(x2, g, *branches, wg, bg, wb, wo)


def _memkv_kernel(m_ref, g_ref, w_ref, o_ref):
    o_ref[0] = _dot(_rms(m_ref[0], g_ref[...]), w_ref[...]).astype(o_ref.dtype)


def _memkv(mem, g, w):
    b, m, _ = mem.shape
    n = 2 * X_HEADS * X_DH
    return pl.pallas_call(
        _memkv_kernel,
        grid=(b,),
        in_specs=[pl.BlockSpec((1, m, D_MODEL), lambda i: (i, 0, 0)),
                  pl.BlockSpec((1, D_MODEL), lambda i: (0, 0)),
                  pl.BlockSpec((D_MODEL, n), lambda i: (0, 0))],
        out_specs=pl.BlockSpec((1, m, n), lambda i: (i, 0, 0)),
        out_shape=jax.ShapeDtypeStruct((b, m, n), BF16),
        compiler_params=_cparams(("parallel",)),
        name="mem_kv",
    )(mem, g, w)


def _cross_kernel(x_ref, g_ref, wq_ref, kv_ref, wo_ref, o_ref):
    n_k = X_HEADS * X_DH
    subs = [slice(i * CROSS_SUB, (i + 1) * CROSS_SUB) for i in range(x_ref.shape[1] // CROSS_SUB)]
    x = [x_ref[0, r, :] for r in subs]
    q = [_dot(_rms(a, g_ref[...]), wq_ref[...]) * X_DH ** -0.5 for a in x]
    pairs = [(i, h) for h in range(X_HEADS) for i in range(len(subs))]
    head = lambda h: slice(h * X_DH, (h + 1) * X_DH)
    sc = [_dot_nt(q[i][:, head(h)], kv_ref[0, :, head(h)]) for i, h in pairs]
    e = [jnp.exp(s - jnp.max(s, axis=-1, keepdims=True)) for s in sc]
    o = [_dot(e_ih, kv_ref[0, :, n_k + h * X_DH:n_k + (h + 1) * X_DH]) / jnp.sum(e_ih, axis=-1, keepdims=True)
         for e_ih, (i, h) in zip(e, pairs)]
    for i, r in enumerate(subs):
        o_i = jnp.concatenate([o[pairs.index((i, h))] for h in range(X_HEADS)], axis=1)
        o_ref[0, r, :] = x[i] + _dot(o_i, wo_ref[...])


def _cross(x3, g, wq, kv, wo):
    b, s, _ = x3.shape
    tm = TM_CROSS
    m = kv.shape[1]
    n_k = X_HEADS * X_DH
    return pl.pallas_call(
        _cross_kernel,
        grid=(b, s // tm),
        in_specs=[pl.BlockSpec((1, tm, D_MODEL), lambda i, j: (i, j, 0)),
                  pl.BlockSpec((1, D_MODEL), lambda i, j: (0, 0)),
                  pl.BlockSpec((D_MODEL, n_k), lambda i, j: (0, 0)),
                  pl.BlockSpec((1, m, 2 * n_k), lambda i, j: (i, 0, 0)),
                  pl.BlockSpec((n_k, D_MODEL), lambda i, j: (0, 0))],
        out_specs=pl.BlockSpec((1, tm, D_MODEL), lambda i, j: (i, j, 0)),
        out_shape=jax.ShapeDtypeStruct((b, s, D_MODEL), F32),
        compiler_params=_cparams(("parallel", "parallel")),
        name="cross_attn",
    )(x3, g, wq, kv, wo)


def _ffn_kernel(x_ref, g_ref, wup_ref, cw_ref, cb_ref, wd_ref, gf_ref, o_ref, tail_ref, act_ref, *, final):
    @pl.when(pl.program_id(1) == 0)
    def _():
        tail_ref[...] = jnp.zeros_like(tail_ref)

    ts = x_ref.shape[1]
    x = x_ref[0]
    hn = _rms(x, g_ref[...]).astype(BF16)
    for c in range(D_FF // FF_CHUNK):
        cols = slice(c * FF_CHUNK, (c + 1) * FF_CHUNK)
        gcols = slice(D_FF + c * FF_CHUNK, D_FF + (c + 1) * FF_CHUNK)
        u = jnp.dot(hn, wup_ref[:, cols], preferred_element_type=F32)
        v = jnp.dot(hn, wup_ref[:, gcols], preferred_element_type=F32)
        ext = jnp.concatenate([tail_ref[:, cols], u], axis=0)
        tail_ref[:, cols] = u[ts - SUBLANE:, :]
        cw = cw_ref[:, cols]
        y = (cw[2:3] * u + cw[1:2] * pltpu.roll(ext, 1, 0)[SUBLANE:]
             + cw[0:1] * pltpu.roll(ext, 2, 0)[SUBLANE:] + cb_ref[:, cols])
        act_ref[:, cols] = (jax.nn.gelu(y) * v).astype(BF16)
    out = x + jnp.dot(act_ref[...], wd_ref[...], preferred_element_type=F32)
    if final:
        out = _rms(out, gf_ref[...])
    o_ref[0] = out


def _ffn(x3, g, wup, cw, cb, wd, gf, final):
    b, s, _ = x3.shape
    ts = TS_FFN
    return pl.pallas_call(
        functools.partial(_ffn_kernel, final=final),
        grid=(b, s // ts),
        in_specs=[pl.BlockSpec((1, ts, D_MODEL), lambda i, j: (i, j, 0)),
                  pl.BlockSpec((1, D_MODEL), lambda i, j: (0, 0)),
                  pl.BlockSpec((D_MODEL, 2 * D_FF), lambda i, j: (0, 0), pipeline_mode=pl.Buffered(1)),
                  pl.BlockSpec((FFN_CONV, D_FF), lambda i, j: (0, 0)),
                  pl.BlockSpec((1, D_FF), lambda i, j: (0, 0)),
                  pl.BlockSpec((D_FF, D_MODEL), lambda i, j: (0, 0), pipeline_mode=pl.Buffered(1)),
                  pl.BlockSpec((1, D_MODEL), lambda i, j: (0, 0))],
        out_specs=pl.BlockSpec((1, ts, D_MODEL), lambda i, j: (i, j, 0)),
        out_shape=jax.ShapeDtypeStruct((b, s, D_MODEL), F32),
        scratch_shapes=[pltpu.VMEM((SUBLANE, D_FF), F32), pltpu.VMEM((ts, D_FF), BF16)],
        compiler_params=_cparams(("parallel", "arbitrary")),
        name="conv_ffn",
    )(x3, g, wup, cw, cb, wd, gf)


def _inproj_columns():
    starts = np.concatenate([[0], np.cumsum(IN_SPLITS)])
    (p_in, a_q, a_k, a_v, a_r, a_lr, d_q, d_k, d_v, d_b, d_a, d_g,
     n_q, n_kc, n_vc, n_ks, n_vs, n_kw, n_vw, n_g) = [np.arange(starts[i], starts[i + 1])
                                                      for i in range(len(IN_SPLITS))]
    n_q = n_q.reshape(NSA_GROUPS, NSA_HPG, NSA_DH).transpose(1, 0, 2).reshape(-1)
    misc = np.full((ZM_W,), N_IN)
    misc[MISC_LR:MISC_LR + GLA_LOWRANK] = a_lr
    misc[MISC_B:MISC_B + GDN_HEADS] = d_b
    misc[MISC_A:MISC_A + GDN_HEADS] = d_a
    misc[MISC_G:MISC_G + 3 * NSA_HEADS] = n_g
    cols = np.concatenate([p_in, a_q, a_k, a_v, a_r, d_q, d_k, d_v, d_g,
                           n_q, n_ks, n_vs, n_kw, n_vw, n_kc, n_vc, misc])
    assert cols.shape[0] == Z_W
    return cols


def _head_expand(offset, n_heads, width):
    e = np.zeros((ZM_W, n_heads * width), np.float32)
    for h in range(n_heads):
        e[offset + h, h * width:(h + 1) * width] = 1.0
    return e


def _nsa_constants(s):
    n_cmp = s // NSA_CMP_STRIDE - NSA_CMP_LEN // NSA_CMP_STRIDE + 1
    n_slc = s // NSA_SEL_LEN
    c_start = np.arange(n_cmp) * NSA_CMP_STRIDE
    s_start = np.arange(n_slc) * NSA_SEL_LEN
    cover = np.zeros((N_CMP_PAD, N_SLC_PAD), np.float32)
    cover[:n_cmp, :n_slc] = ((c_start[:, None] <= s_start[None, :] + NSA_SEL_LEN - 1)
                             & (c_start[:, None] + NSA_CMP_LEN - 1 >= s_start[None, :]))
    gexp = np.zeros((ZM_W, 3, MIX_W), np.float32)
    for g in range(NSA_GROUPS):
        for j in range(NSA_HPG):
            slot = j * NSA_GROUPS + g
            for c in range(3):
                gexp[MISC_G + (g * NSA_HPG + j) * 3 + c, c, slot * NSA_DH:(slot + 1) * NSA_DH] = 1.0
    return jnp.asarray(cover.T, dtype=BF16), jnp.asarray(gexp.reshape(ZM_W, 3 * MIX_W), dtype=BF16)


def _block_diag(blocks):
    n, a, b = blocks.shape
    return jnp.einsum('gh,gab->gahb', jnp.eye(n, dtype=blocks.dtype), blocks).reshape(n * a, n * b)


def kernel(x, mem, g_mix, w_in, pool_w, pool_scale, gla_w_lr, gla_b_lr, gla_g_norm, gdn_conv, gdn_a_log,
           gdn_dt_bias, gdn_g_norm, nsa_pe, nsa_cmp_w1, nsa_cmp_w2, w_branch, w_gate, b_gate, w_out, g_cross,
           g_mem, w_xq, w_mem_kv, w_xo, g_ffn, w_up, ffn_conv, ffn_conv_b, w_down, g_final):
    b, s, d = x.shape
    depth = w_in.shape[0]
    t = b * s
    cols = _inproj_columns()
    covt, gexp = _nsa_constants(s)
    eba = jnp.asarray(np.concatenate([_head_expand(MISC_B, GDN_HEADS, GDN_DH),
                                      _head_expand(MISC_A, GDN_HEADS, GDN_DH)], axis=1), dtype=BF16)
    ones_bd = _block_diag(jnp.ones((GDN_HEADS, GDN_DH, GDN_DH), BF16))
    nsa_rows = np.arange(MIX_W).reshape(NSA_GROUPS, NSA_HPG, NSA_DH).transpose(1, 0, 2).reshape(-1)
    row = lambda v: v.reshape(1, -1).astype(F32)
    misc_a = lambda v: jnp.zeros((1, ZM_W), F32).at[0, MISC_A:MISC_A + GDN_HEADS].set(v)

    x2 = x.reshape(t, d)
    for l in range(depth):
        w_in_r = jnp.concatenate([w_in[l], jnp.zeros((d, 1), F32)], axis=1)[:, cols].astype(BF16)
        wlr = jnp.zeros((ZM_W, GLA_HEADS * GLA_DK), F32).at[MISC_LR:MISC_LR + GLA_LOWRANK].set(gla_w_lr[l])
        wlr_hi = wlr.astype(BF16)
        wlr2 = jnp.stack([wlr_hi, (wlr - wlr_hi.astype(F32)).astype(BF16)])
        o_pool, zn, zc, zm, o_gla, o_gdn = _mixers(
            x2.reshape(b, s, d), row(g_mix[l]), w_in_r, _block_diag(pool_w[l]).astype(BF16), row(pool_scale[l]),
            wlr2, row(gla_b_lr[l]), row(jnp.tile(gla_g_norm[l], GLA_HEADS)),
            gdn_conv[l], eba, misc_a(gdn_a_log[l]), misc_a(gdn_dt_bias[l]),
            row(jnp.tile(gdn_g_norm[l], GDN_HEADS)), ones_bd)

        pe_x = jnp.tile(nsa_pe[l], (1, 1, NSA_GROUPS))
        w1 = nsa_cmp_w1[l].reshape(2, NSA_CMP_LEN, NSA_DH, NSA_DH)
        eye_g = jnp.eye(NSA_GROUPS, dtype=F32)
        w1_bd = jnp.einsum('gh,kpde->kpgdhe', eye_g, w1).reshape(2, NSA_CMP_LEN, LANE, LANE).astype(BF16)
        w2_bd = jnp.einsum('gh,kde->kgdhe', eye_g, nsa_cmp_w2[l]).reshape(2, LANE, LANE).astype(BF16)
        ck, cv = _nsa_compress(zc, pe_x, w1_bd, w2_bd)
        o_nsa = _nsa(zn, zm, ck, cv, covt, gexp)

        wb = jnp.concatenate([w_branch[l, :3], w_branch[l, 3][nsa_rows][None]], axis=0).astype(BF16)
        branches = [o.reshape(t, MIX_W) for o in (o_pool, o_gla, o_gdn, o_nsa)]
        x2 = _combine(x2, row(g_mix[l]), branches, w_gate[l].astype(BF16),
                      b_gate[l].reshape(N_BRANCH, 1, d), wb, w_out[l].astype(BF16))

        kv = _memkv(mem, row(g_mem[l]), w_mem_kv[l].astype(BF16))
        x3 = _cross(x2.reshape(b, s, d), row(g_cross[l]), w_xq[l].astype(BF16), kv, w_xo[l].astype(BF16))

        x3 = _ffn(x3, row(g_ffn[l]), w_up[l].astype(BF16), ffn_conv[l], row(ffn_conv_b[l]),
                  w_down[l].astype(BF16), row(g_final), final=(l == depth - 1))
        x2 = x3.reshape(t, d)
    return x2.reshape(b, s, d)
```

```python
import functools

import numpy as np
import jax
import jax.numpy as jnp
from jax import lax
from jax.experimental import pallas as pl
from jax.experimental.pallas import tpu as pltpu

F32 = jnp.float32
BF16 = jnp.bfloat16

D_MODEL = 1024
MIX_W = 256
POOL_WINDOWS = (2, 4, 8, 16)
POOL_GW = 64
GLA_HEADS = 4
GLA_DK = 32
GLA_DV = 64
GLA_LOWRANK = 16
GLA_GATE_NORM = 16.0
CHUNK = 64
GDN_HEADS = 4
GDN_DH = 64
GDN_CONV = 4
NSA_HEADS = 4
NSA_GROUPS = 2
NSA_HPG = 2
NSA_DH = 64
NSA_KV = 128
NSA_CMP_LEN = 32
NSA_CMP_STRIDE = 16
NSA_SEL_LEN = 64
NSA_N_SEL = 16
NSA_WINDOW = 512
X_HEADS = 4
X_DH = 128
D_FF = 2816
FFN_CONV = 3
EPS = 1e-6
N_BRANCH = 4

IN_SPLITS = (MIX_W,
             128, 128, 256, 256, GLA_LOWRANK,
             MIX_W, MIX_W, MIX_W, GDN_HEADS, GDN_HEADS, MIX_W,
             256, NSA_KV, NSA_KV, NSA_KV, NSA_KV, NSA_KV, NSA_KV, 3 * NSA_HEADS)
N_IN = sum(IN_SPLITS)

MISC_LR = 0
MISC_B = 16
MISC_A = 20
MISC_G = 24
LANE = 128
SUBLANE = 8

Z_WIDTHS = (256, 768, 1024, 768, 256, 128)
GDN_QKV_W = 3 * MIX_W
ZP_W, ZG_W, ZD_W, ZN_W, ZC_W, ZM_W = Z_WIDTHS
Z_W = sum(Z_WIDTHS)
NEG_BIG = -1e30
F32_LOWEST = float(np.finfo(np.float32).min)

TM_COMB = 512
TM_CROSS = 1024
CROSS_SUB = 512
TS_FFN = 1024
FF_CHUNK = 256
TQ_NSA = 256
NSA_SEQS = 2
REC_SEQS = 16
REC_LAG = 3
GLA_START = 6
N_SLC_PAD = 128
N_CMP_PAD = 128

VMEM_LIMIT = 56 * 1024 * 1024


def _cparams(sem):
    return pltpu.CompilerParams(dimension_semantics=sem, vmem_limit_bytes=VMEM_LIMIT)


def _rms(x, g):
    return x * lax.rsqrt(jnp.mean(x * x, axis=-1, keepdims=True) + EPS) * g


def _dot(a, b):
    return jnp.dot(a.astype(BF16), b.astype(BF16), preferred_element_type=F32)


def _dot_nt(a, b):
    return lax.dot_general(a.astype(BF16), b.astype(BF16), (((1,), (1,)), ((), ())),
                           preferred_element_type=F32)


def _dot_tn(a, b):
    return lax.dot_general(a.astype(BF16), b.astype(BF16), (((0,), (0,)), ((), ())),
                           preferred_element_type=F32)


def _split2_dot(a, b):
    hi = a.astype(BF16)
    lo = (a - hi.astype(F32)).astype(BF16)
    return jnp.dot(hi, b, preferred_element_type=F32) + jnp.dot(lo, b, preferred_element_type=F32)


def _split3_lhs_dot(a, b):
    hi = a.astype(BF16)
    r1 = a - hi.astype(F32)
    mid = r1.astype(BF16)
    lo = (r1 - mid.astype(F32)).astype(BF16)
    return ((jnp.dot(hi, b, preferred_element_type=F32) + jnp.dot(mid, b, preferred_element_type=F32))
            + jnp.dot(lo, b, preferred_element_type=F32))


def _dot3(a, w_hi, w_lo):
    a_hi = a.astype(BF16)
    a_lo = (a - a_hi.astype(F32)).astype(BF16)
    return (jnp.dot(a_hi, w_hi, preferred_element_type=F32)
            + (jnp.dot(a_lo, w_hi, preferred_element_type=F32) + jnp.dot(a_hi, w_lo, preferred_element_type=F32)))


def _iota(shape, axis):
    return lax.broadcasted_iota(jnp.int32, shape, axis)


def _block_mask(rows, cols, rb, cb):
    return (_iota((rows, cols), 0) // rb) == (_iota((rows, cols), 1) // cb)


def _shift_rows(x, k):
    t = _iota(x.shape, 0)
    return jnp.where(t >= k, pltpu.roll(x, k, 0), 0.0)


def _cumsum_rows(x):
    k = 1
    while k < x.shape[0]:
        x = x + _shift_rows(x, k)
        k *= 2
    return x


def _softplus(x):
    return jnp.maximum(x, 0.0) + jnp.log1p(jnp.exp(-jnp.abs(x)))


def _log_sigmoid(x):
    return -_softplus(-x)


def _silu(x):
    return x * jax.nn.sigmoid(x)


def _masked_softmax(s, mask):
    s = jnp.where(mask, s, -jnp.inf)
    m = jnp.maximum(jnp.max(s, axis=-1, keepdims=True), F32_LOWEST)
    e = jnp.exp(s - m)
    den = jnp.sum(e, axis=-1, keepdims=True)
    return e / jnp.where(den > 0, den, 1.0)


def _tile4(x):
    return jnp.concatenate([x, x, x, x], axis=0)


POOL_HALO = 16


def _gla_masks():
    return (_block_mask(4 * CHUNK, 128, CHUNK, GLA_DK),
            _block_mask(4 * CHUNK, 256, CHUNK, GLA_DV),
            _block_mask(256, 128, GLA_DV, GLA_DK),
            (_iota((CHUNK, 256), 1) % CHUNK) <= _iota((CHUNK, 256), 0))


def _seqs(f, *lists):
    return [f(*args) for args in zip(*lists)]


def _run_staggered(stage_gens, starts, periods):
    live = list(range(len(stage_gens)))
    tick = 0
    while live:
        for i in list(live):
            if tick >= starts[i] and (tick - starts[i]) % periods[i] == 0:
                try:
                    next(stage_gens[i])
                except StopIteration:
                    live.remove(i)
        tick += 1


def _gla_chunk(masks, zqk, v, r, zm, wlr_ref, blr_ref, gn_ref, ones_ref, st_refs, out):
    mask_k, mask_v, mask_st, causal = masks
    q = [z[:, 0:128] * GLA_DK ** -0.5 for z in zqk]
    k = [z[:, 128:256] for z in zqk]
    pre = _seqs(lambda a: _dot3(a, wlr_ref[0], wlr_ref[1]), zm)
    yield
    bc = _seqs(lambda a: _cumsum_rows(_log_sigmoid(a + blr_ref[...]) / GLA_GATE_NORM), pre)
    bl = [a[CHUNK - 1:CHUNK, :] for a in bc]
    q_e = _seqs(lambda a, c: a * jnp.exp(c), q, bc)
    k_e = _seqs(lambda a, c: a * jnp.exp(-c), k, bc)
    k_u = _seqs(lambda a, c, l: a * jnp.exp(l - c), k, bc, bl)
    st = [ref[...] for ref in st_refs]
    yield
    att = _seqs(lambda a, b: jnp.where(causal, _dot_nt(a, jnp.where(mask_k, _tile4(b), 0.0)), 0.0), q_e, k_e)
    inter = _seqs(_dot_nt, q_e, st)
    kv = _seqs(_dot_tn, v, k_u)
    yield
    o = _seqs(lambda a, b, c: _dot(a, _tile4(b.astype(BF16)) * ones_ref[...]) + c, att, v, inter)
    for ref, s_old, l, upd in zip(st_refs, st, bl, kv):
        ref[...] = s_old * jnp.exp(l) + jnp.where(mask_st, upd, 0.0)
    yield
    ms = _seqs(lambda a: _dot(a * a, ones_ref[...]) * (1.0 / GLA_DV), o)
    yield
    out[:] = _seqs(lambda a, m, g: a * lax.rsqrt(m + EPS) * gn_ref[...] * _silu(g), o, ms, r)


GDN_HALO = SUBLANE


def _gdn_masks():
    c4 = 4 * CHUNK
    col = _iota((CHUNK, c4), 1) % CHUNK
    row = _iota((CHUNK, c4), 0)
    return (_block_mask(c4, c4, CHUNK, CHUNK), col <= row, col < row, col == row)


def _gdn_chunk(masks, halo, zqkv, gate, zm, cw_ref, eba_ref, alog_ref, dtb_ref, gn_ref, ones_ref, st_refs, out):
    mask_bd, incl, strict, diag = masks
    cw = cw_ref[...]
    bd = lambda a: _tile4(a.astype(BF16)) * ones_ref[...]

    def conv_silu(h, cur):
        ext = jnp.concatenate([h, cur], axis=0)
        conv = (cw[3:4] * cur + cw[2:3] * pltpu.roll(ext, 1, 0)[GDN_HALO:]
                + cw[1:2] * pltpu.roll(ext, 2, 0)[GDN_HALO:] + cw[0:1] * pltpu.roll(ext, 3, 0)[GDN_HALO:])
        return _silu(conv)

    qkv = _seqs(conv_silu, halo, zqkv)
    yield
    v = [a[:, 512:768] for a in qkv]
    ssq = _seqs(lambda a: _dot(jnp.concatenate([a[:, 0:256] * a[:, 0:256], a[:, 256:512] * a[:, 256:512]],
                                               axis=0), ones_ref[...]), qkv)
    yield
    q = _seqs(lambda a, s: a[:, 0:256] * lax.rsqrt(s[0:CHUNK] + EPS) * GDN_DH ** -0.5, qkv, ssq)
    k = _seqs(lambda a, s: a[:, 256:512] * lax.rsqrt(s[CHUNK:2 * CHUNK] + EPS), qkv, ssq)
    beta = _seqs(lambda a: _dot(jax.nn.sigmoid(a), eba_ref[:, 0:MIX_W]), zm)
    yield
    gc = _seqs(lambda a: _split3_lhs_dot(_cumsum_rows(-jnp.exp(alog_ref[...]) * _softplus(a + dtb_ref[...])),
                                         eba_ref[:, MIX_W:2 * MIX_W]), zm)
    yield
    gl = [a[CHUNK - 1:CHUNK, :] for a in gc]
    g_row = _seqs(lambda a: jnp.sum(jnp.where(diag, a, 0.0), axis=0, keepdims=True), gc)
    yield
    decay =_seqs(lambda a, b: jnp.exp(jnp.where(incl, a - b, -jnp.inf)), gc, g_row)
    kb = _seqs(lambda a, b: a * b, k, beta)
    vb = _seqs(lambda a, b: a * b, v, beta)
    kq = _seqs(lambda a, b, c: _dot_nt(jnp.concatenate([a, b], axis=0), bd(c)), kb, q, k)
    yield
    n_mat =_seqs(lambda s, d: jnp.where(strict, s[0:CHUNK] * d, 0.0), kq, decay)
    a_qk = _seqs(lambda s, d: s[CHUNK:2 * CHUNK] * d, kq, decay)
    m = _seqs(lambda a: -a, n_mat)
    x = _seqs(lambda a: jnp.where(diag, 1.0, 0.0) + a, m)
    m = _seqs(lambda a: _dot(a, bd(a)), m)
    yield
    p = 2
    while p < CHUNK // 2:
        xm = _seqs(lambda a, b: _dot(jnp.concatenate([a, b], axis=0), bd(b)), x, m)
        x = _seqs(lambda a, r: a + r[0:CHUNK], x, xm)
        m = [r[CHUNK:2 * CHUNK] for r in xm]
        p *= 2
        yield
    x = _seqs(lambda a, b: a + _dot(a, bd(b)), x, m)
    yield
    uw = _seqs(lambda a, b, c, g: _dot(a, jnp.concatenate([bd(b), bd(c * jnp.exp(g))], axis=1)),
               x, vb, kb, gc)
    yield
    st = [ref[...] for ref in st_refs]
    ws = _seqs(lambda a, b, g, s: _dot(jnp.concatenate([a[:, 256:512], b * jnp.exp(g)], axis=0), s),
               uw, q, gc, st)
    yield
    v_new = _seqs(lambda a, r: a[:, 0:256] - r[0:CHUNK], uw, ws)
    o = _seqs(lambda r, qk, vn: r[CHUNK:2 * CHUNK] + _dot(qk, bd(vn)), ws, a_qk, v_new)
    upd = _seqs(lambda a, g, l, vn: _dot_tn(a * jnp.exp(l - g), vn), k, gc, gl, v_new)
    yield
    for ref, s_old, l, u in zip(st_refs, st, gl, upd):
        ref[...] = s_old * jnp.exp(l) + jnp.where(mask_bd, u, 0.0)
    ms = _seqs(lambda a: _dot(a * a, ones_ref[...]) * (1.0 / GDN_DH), o)
    yield
    out[:] = _seqs(lambda a, m_, g: a * lax.rsqrt(m_ + EPS) * gn_ref[...] * _silu(g), o, ms, gate)


def _pool_diff(u, tail, pos0):
    ext = jnp.concatenate([tail, u], axis=0)
    s2 = ext + pltpu.roll(ext, 1, 0)
    s4 = s2 + pltpu.roll(s2, 2, 0)
    s8 = s4 + pltpu.roll(s4, 4, 0)
    s16 = s8 + pltpu.roll(s8, 8, 0)
    grp = _iota(u.shape, 1) // POOL_GW
    win = jnp.where(grp == 0, s2[POOL_HALO:], jnp.where(grp == 1, s4[POOL_HALO:],
                                                        jnp.where(grp == 2, s8[POOL_HALO:], s16[POOL_HALO:])))
    width = jnp.where(grp == 0, POOL_WINDOWS[0],
                      jnp.where(grp == 1, POOL_WINDOWS[1],
                                jnp.where(grp == 2, POOL_WINDOWS[2], POOL_WINDOWS[3])))
    cnt = jnp.minimum(pos0 + _iota(u.shape, 0) + 1, width).astype(F32)
    return win / cnt - u


def _mix_kernel(x_ref, g_ref, w_ref, pw_ref, ps_ref, wlr_ref, blr_ref, gng_ref, cw_ref, eba_ref, alog_ref, dtb_ref,
                gnd_ref, ones_ref, op_ref, zn_ref, zc_ref, zm_ref, og_ref, od_ref, ptail_ref, halo_ref, *st_refs):
    nb = x_ref.shape[0]
    chunk = pl.program_id(1)

    @pl.when(chunk == 0)
    def _():
        ptail_ref[...] = jnp.zeros_like(ptail_ref)
        halo_ref[...] = jnp.zeros_like(halo_ref)
        for st_ref in st_refs:
            st_ref[...] = jnp.zeros_like(st_ref)

    h = _rms(x_ref[...].reshape(nb * CHUNK, D_MODEL), g_ref[...]).astype(BF16)
    proj = lambda lo, n: jnp.dot(h, w_ref[:, lo:lo + n], preferred_element_type=F32)
    seqs = lambda a: [a[i * CHUNK:(i + 1) * CHUNK] for i in range(nb)]
    off_g, off_d = ZP_W, ZP_W + ZG_W
    off_n = off_d + ZD_W
    off_c = off_n + ZN_W
    off_m = off_c + ZC_W
    zqkv = seqs(proj(off_d, GDN_QKV_W))
    zm = seqs(proj(off_m, ZM_W))
    halo = [halo_ref[i] for i in range(nb)]
    for i in range(nb):
        halo_ref[i] = zqkv[i][CHUNK - GDN_HALO:, :]
        zm_ref[i] = zm[i]
    halves = [slice(0, nb // 2), slice(nb // 2, nb)] if nb > 1 else [slice(0, nb)]
    gla_qk, gla_v, gla_r = [], [], []
    gate_parts = [[] for _ in halves]

    def other_projections():
        gla_qk.extend(seqs(proj(off_g, 256)))
        yield
        gla_v.extend(seqs(proj(off_g + 256, 256)))
        yield
        u = seqs(proj(0, ZP_W))
        yield
        diff = [_pool_diff(u[i], ptail_ref[i], chunk * CHUNK) for i in range(nb)]
        for i in range(nb):
            ptail_ref[i] = u[i][CHUNK - POOL_HALO:, :]
        yield
        for lo in range(0, ZN_W, 256):
            zn = seqs(proj(off_n + lo, 256))
            for i in range(nb):
                zn_ref[i, :, lo:lo + 256] = zn[i].astype(zn_ref.dtype)
            yield
        zc = seqs(proj(off_c, ZC_W))
        for i in range(nb):
            zc_ref[i] = zc[i]
        yield
        gla_r.extend(seqs(proj(off_g + 512, 256)))
        yield
        gate = seqs(proj(off_d + GDN_QKV_W, MIX_W))
        for hs, part in zip(halves, gate_parts):
            part.extend(gate[hs])
        yield
        o_pool = seqs(_dot(jnp.concatenate(diff, axis=0), pw_ref[...]) * ps_ref[...])
        for i in range(nb):
            op_ref[i] = o_pool[i].astype(op_ref.dtype)

    og = [None] * nb
    od_parts = [[None] * (hs.stop - hs.start) for hs in halves]
    gdn_masks = _gdn_masks()
    gens = [_gdn_chunk(gdn_masks, halo[hs], zqkv[hs], gate_part, zm[hs], cw_ref, eba_ref, alog_ref, dtb_ref,
                       gnd_ref, ones_ref, st_refs[nb:][hs], part)
            for hs, part, gate_part in zip(halves, od_parts, gate_parts)]
    gens.append(_gla_chunk(_gla_masks(), gla_qk, gla_v, gla_r, zm, wlr_ref, blr_ref, gng_ref, ones_ref,
                           st_refs[:nb], og))
    starts = [REC_LAG * i for i in range(len(halves))] + [GLA_START]
    gens.append(other_projections())
    _run_staggered(gens, starts + [0], [1] * len(gens))
    od = [o for part in od_parts for o in part]
    for i in range(nb):
        og_ref[i] = og[i].astype(og_ref.dtype)
        od_ref[i] = od[i].astype(od_ref.dtype)


def _mixers(x3, g, w, pool_w_bd, pool_scale, wlr2, blr, gn_gla, cw, eba, alog, dtb, gn_gdn, ones_bd):
    b, s, _ = x3.shape
    nb = REC_SEQS if b % REC_SEQS == 0 else 1
    const = lambda shape: pl.BlockSpec(shape, lambda i, c: (0,) * len(shape))
    tok = lambda n: pl.BlockSpec((nb, CHUNK, n), lambda i, c: (i, c, 0))
    outs = ((MIX_W, BF16), (ZN_W, BF16), (ZC_W, F32), (ZM_W, F32), (MIX_W, BF16), (MIX_W, BF16))
    return pl.pallas_call(
        _mix_kernel,
        grid=(b // nb, s // CHUNK),
        in_specs=[tok(D_MODEL), const((1, D_MODEL)), const((D_MODEL, Z_W)), const((MIX_W, MIX_W)), const((1, MIX_W)),
                  const((2, ZM_W, 128)), const((1, 128)), const((1, MIX_W)),
                  const((GDN_CONV, GDN_QKV_W)), const((ZM_W, 2 * MIX_W)), const((1, ZM_W)), const((1, ZM_W)),
                  const((1, MIX_W)), const((MIX_W, MIX_W))],
        out_specs=[tok(n) for n, _ in outs],
        out_shape=[jax.ShapeDtypeStruct((b, s, n), dt) for n, dt in outs],
        scratch_shapes=([pltpu.VMEM((nb, POOL_HALO, MIX_W), F32), pltpu.VMEM((nb, GDN_HALO, GDN_QKV_W), F32)]
                        + [pltpu.VMEM((256, 128), F32)] * nb + [pltpu.VMEM((256, 256), F32)] * nb),
        compiler_params=_cparams(("parallel", "arbitrary")),
        name="inproj_mixers",
    )(x3, g, w, pool_w_bd, pool_scale, wlr2, blr, gn_gla, cw, eba, alog, dtb, gn_gdn, ones_bd)


def _cmp_kernel(kc_ref, vc_ref, pe_ref, w1_ref, w2_ref, ck_ref, cv_ref):
    n_sub = NSA_CMP_LEN // NSA_CMP_STRIDE
    n_chunks = kc_ref.shape[1] // NSA_CMP_STRIDE
    for which, (src_ref, out_ref) in enumerate(((kc_ref, ck_ref), (vc_ref, cv_ref))):
        parts = [jnp.zeros((n_chunks, LANE), F32) for _ in range(n_sub)]
        for i in range(NSA_CMP_STRIDE):
            slab = src_ref[0, pl.ds(i, n_chunks, stride=NSA_CMP_STRIDE), :]
            for sub in range(n_sub):
                p = sub * NSA_CMP_STRIDE + i
                parts[sub] = parts[sub] + _dot(slab + pe_ref[which, p:p + 1, :], w1_ref[which, p])
        pre = parts[0] + pltpu.roll(parts[1], n_chunks - 1, 0)
        out_ref[0] = _dot(jax.nn.gelu(pre), w2_ref[which])


def _nsa_compress(zc, pe_x, w1_bd, w2_bd):
    b, s, _ = zc.shape
    n_chunks = s // NSA_CMP_STRIDE
    return pl.pallas_call(
        _cmp_kernel,
        grid=(b,),
        in_specs=[pl.BlockSpec((1, s, LANE), lambda i: (i, 0, 0)),
                  pl.BlockSpec((1, s, LANE), lambda i: (i, 0, 1)),
                  pl.BlockSpec((2, NSA_CMP_LEN, LANE), lambda i: (0, 0, 0)),
                  pl.BlockSpec((2, NSA_CMP_LEN, LANE, LANE), lambda i: (0, 0, 0, 0)),
                  pl.BlockSpec((2, LANE, LANE), lambda i: (0, 0, 0))],
        out_specs=[pl.BlockSpec((1, n_chunks, LANE), lambda i: (i, 0, 0))] * 2,
        out_shape=[jax.ShapeDtypeStruct((b, n_chunks, LANE), F32)] * 2,
        compiler_params=_cparams(("parallel",)),
        name="nsa_compress",
    )(zc, zc, pe_x, w1_bd, w2_bd)


def _topk_rank(v):
    n, t = v.shape
    blocks = [v[r:r + SUBLANE] for r in range(0, n, SUBLANE)]
    sub = _iota((SUBLANE, t), 0)
    rank = [jnp.zeros((SUBLANE, t), jnp.int32) for _ in blocks]
    for m in range(n):
        other = v[m:m + 1, :]
        mb, ms = divmod(m, SUBLANE)
        for r, blk in enumerate(blocks):
            if r < mb:
                ahead = (other > blk).astype(jnp.int32)
            elif r > mb:
                ahead = (other >= blk).astype(jnp.int32)
            else:
                ahead = jnp.where(sub > ms, (other >= blk).astype(jnp.int32), (other > blk).astype(jnp.int32))
            rank[r] = rank[r] + ahead
    return jnp.concatenate(rank, axis=0)


def _exp_weights(s, m):
    return jnp.exp((s - m).astype(BF16))


def _normalise_aug(o_aug, in_grp):
    den = pltpu.roll(o_aug, NSA_DH, 1)
    return jnp.where(in_grp, o_aug / jnp.where(den > 0, den, 1.0), 0.0)


def _nsa_kernel(q_ref, ksv_ref, kwv_ref, ck_ref, cv_ref, zm_ref, covt_ref, gexp_ref, o_ref,
                kaug_ref, vaug_ref, vwaug_ref, s_ref, mx_ref, acc_ref):
    nb = q_ref.shape[0]
    tq = q_ref.shape[1]
    s_len = ksv_ref.shape[1]
    n_slc = s_len // NSA_SEL_LEN
    qi = pl.program_id(1)
    s0 = pl.multiple_of(qi * tq, tq)
    lane_grp = _iota((1, LANE), 1) // NSA_DH
    chains = [(bi, g) for bi in range(nb) for g in range(NSA_GROUPS)]
    n_chains = range(len(chains))

    @pl.when(qi == 0)
    def _():
        lane = _iota((s_len, LANE), 1)
        key_blk = _iota((s_len, LANE), 0) // NSA_SEL_LEN
        for bi in range(nb):
            k = ksv_ref[bi, :, 0:LANE].astype(F32)
            v = ksv_ref[bi, :, LANE:2 * LANE].astype(F32)
            vw = kwv_ref[bi, :, LANE:2 * LANE].astype(F32)
            for g in range(NSA_GROUPS):
                own = lane // NSA_DH == g
                onehot = jnp.where(lane - (1 - g) * NSA_DH == key_blk, 1.0, 0.0)
                kaug_ref[bi * NSA_GROUPS + g] = jnp.where(own, k, onehot).astype(BF16)
                vaug_ref[bi * NSA_GROUPS + g] = jnp.where(own, v, 1.0).astype(BF16)
                vwaug_ref[bi * NSA_GROUPS + g] = jnp.where(own, vw, 1.0).astype(BF16)

    t_col = s0 + _iota((tq, 1), 0)
    t_col2 = jnp.concatenate([t_col, t_col], axis=0)
    t_row = s0 + _iota((1, tq), 1)
    q = [q_ref[bi].astype(F32) * NSA_DH ** -0.5 for bi in range(nb)]
    ck_hi = [ck_ref[bi].astype(BF16) for bi in range(nb)]
    ck_lo = [(ck_ref[bi] - ck_hi[bi].astype(F32)).astype(BF16) for bi in range(nb)]
    cv = [cv_ref[bi] for bi in range(nb)]
    cmp_end = _iota((1, N_CMP_PAD), 1) * NSA_CMP_STRIDE + (NSA_CMP_LEN - 1)
    cmp_valid = cmp_end <= t_col2
    blk_t = _iota((n_slc, tq), 0)
    cur_t = t_row // NSA_SEL_LEN
    forced_t = (blk_t == 0) | (blk_t == cur_t) | (blk_t == cur_t - 1)
    future_t = blk_t > cur_t
    row_in_tile = jnp.concatenate([_iota((tq, tq), 0)] * NSA_HPG, axis=0)
    col_in_tile = _iota((NSA_HPG * tq, tq), 1)
    diag_ok = col_in_tile <= row_in_tile
    n_wt = NSA_WINDOW // tq + 1
    never = 2 * tq
    win_off = [pl.multiple_of(jnp.maximum(qi - (n_wt - 1 - w), 0) * tq, tq) for w in range(n_wt)]
    win_ok = [col_in_tile > row_in_tile + jnp.where(qi >= n_wt - 1, 0, never)]
    win_ok += [col_in_tile >= jnp.where(qi >= n_wt - 1 - w, 0, never) for w in range(1, n_wt - 1)]
    win_ok += [diag_ok]

    o_cmp = [[jnp.zeros((tq, LANE), F32) for _ in range(NSA_HPG)] for _ in range(nb)]
    o_slc = [[jnp.zeros((tq, LANE), F32) for _ in range(NSA_HPG)] for _ in range(nb)]
    o_win = [[jnp.zeros((tq, LANE), F32) for _ in range(NSA_HPG)] for _ in range(nb)]
    in_grp = [lane_grp == g for _, g in chains]
    q2 = [jnp.concatenate([jnp.where(in_grp[c], q[bi][:, j * LANE:(j + 1) * LANE], 0.0)
                           for j in range(NSA_HPG)], axis=0) for c, (bi, _) in enumerate(chains)]
    qb = [a.astype(BF16) for a in q2]
    p_c = [_masked_softmax(_dot_nt(qb[c], ck_hi[bi]) + _dot_nt(qb[c], ck_lo[bi]), cmp_valid)
           for c, (bi, _) in enumerate(chains)]
    oc = [jnp.where(in_grp[c], _dot(p_c[c], cv[bi]), 0.0) for c, (bi, _) in enumerate(chains)]
    imp_c = [a[0:tq] + a[tq:2 * tq] for a in p_c]
    imp_hi = [a.astype(BF16) for a in imp_c]
    imp_lo = [(a - h.astype(F32)).astype(BF16) for a, h in zip(imp_c, imp_hi)]
    imp_t = [(_dot_nt(covt_ref[...], h) + _dot_nt(covt_ref[...], l_))[0:n_slc]
             for h, l_ in zip(imp_hi, imp_lo)]
    imp_t = [jnp.where(forced_t, jnp.inf, jnp.where(future_t, -jnp.inf, a)) for a in imp_t]
    s_w = [[jnp.where(ok, _dot_nt(qb[c], kwv_ref[bi, pl.ds(off, tq), 0:LANE]), -jnp.inf)
            for ok, off in zip(win_ok, win_off)] for c, (bi, _) in enumerate(chains)]
    qa = []
    for c, (_, g) in enumerate(chains):
        bias_t = jnp.where(_topk_rank(imp_t[c]) < NSA_N_SEL, 0.0, NEG_BIG)
        lo = (1 - g) * NSA_DH
        rows = ([jnp.zeros((lo, tq), F32)] if lo else []) + [bias_t, jnp.zeros((LANE - lo - n_slc, tq), F32)]
        bias = jnp.concatenate(rows, axis=0).T
        qa.append(jnp.where(in_grp[c], q2[c], jnp.concatenate([bias] * NSA_HPG, axis=0)).astype(BF16))
    m_w = []
    for c in n_chains:
        m_c = s_w[c][0]
        for s_t in s_w[c][1:]:
            m_c = jnp.maximum(m_c, s_t)
        m_w.append(jnp.maximum(jnp.max(m_c, axis=-1, keepdims=True), F32_LOWEST))
    ow = [jnp.zeros((NSA_HPG * tq, LANE), F32) for _ in n_chains]
    for w, off in enumerate(win_off):
        for c in n_chains:
            ow[c] = ow[c] + jnp.dot(_exp_weights(s_w[c][w], m_w[c]), vwaug_ref[c, pl.ds(off, tq), :],
                                    preferred_element_type=F32)
    for c, (bi, _) in enumerate(chains):
        ow_c = _normalise_aug(ow[c], in_grp[c])
        for j in range(NSA_HPG):
            rows_j = slice(j * tq, (j + 1) * tq)
            o_cmp[bi][j] = o_cmp[bi][j] + oc[c][rows_j]
            o_win[bi][j] = o_win[bi][j] + ow_c[rows_j]

    half_max = lambda s_t: jnp.maximum(s_t[:, 0:LANE], s_t[:, LANE:2 * LANE])
    mx_ref[...] = jnp.full(mx_ref.shape, NEG_BIG, F32)
    groups = n_chains

    def score_tiles(tiles):
        offs = [pl.multiple_of(kt * tq, tq) for kt in tiles]
        s_new = [[_dot_nt(qa[g], kaug_ref[g, pl.ds(off, tq), :]) for g in groups] for off in offs]
        for kt, s_kt in zip(tiles, s_new):
            for g in groups:
                s_ref[g, kt] = s_kt[g]
        for g in groups:
            m_new = half_max(s_new[0][g])
            for s_kt in s_new[1:]:
                m_new = jnp.maximum(m_new, half_max(s_kt[g]))
            mx_ref[g] = jnp.maximum(mx_ref[g], m_new)

    def score_pair(p, carry):
        score_tiles([2 * p, 2 * p + 1])
        return carry

    lax.fori_loop(0, qi // 2, score_pair, 0)

    @pl.when(qi % 2 == 1)
    def _():
        score_tiles([qi - 1])

    m_s = []
    for g in groups:
        s_d = jnp.where(diag_ok, _dot_nt(qa[g], kaug_ref[g, pl.ds(s0, tq), :]), NEG_BIG)
        s_ref[g, qi] = s_d
        m_g = jnp.max(jnp.maximum(mx_ref[g], half_max(s_d)), axis=-1, keepdims=True)
        m_s.append(jnp.maximum(m_g, F32_LOWEST))
    acc_ref[...] = jnp.zeros(acc_ref.shape, F32)

    def attend_tiles(tiles):
        offs = [pl.multiple_of(kt * tq, tq) for kt in tiles]
        pv = [[jnp.dot(_exp_weights(s_ref[g, kt], m_s[g]), vaug_ref[g, pl.ds(off, tq), :],
                       preferred_element_type=F32) for g in groups] for kt, off in zip(tiles, offs)]
        for g in groups:
            upd = pv[0][g]
            for pv_kt in pv[1:]:
                upd = upd + pv_kt[g]
            acc_ref[g] += upd

    def attend_pair(p, carry):
        attend_tiles([2 * p, 2 * p + 1])
        return carry

    lax.fori_loop(0, (qi + 1) // 2, attend_pair, 0)

    @pl.when(qi % 2 == 0)
    def _():
        attend_tiles([qi])

    for c, (bi, _) in enumerate(chains):
        os_ = _normalise_aug(acc_ref[c], in_grp[c])
        for j in range(NSA_HPG):
            o_slc[bi][j] = o_slc[bi][j] + os_[j * tq:(j + 1) * tq]

    for bi in range(nb):
        gates = _split2_dot(jax.nn.sigmoid(zm_ref[bi]), gexp_ref[...])
        o = (gates[:, 0:MIX_W] * jnp.concatenate(o_cmp[bi], axis=1)
             + gates[:, MIX_W:2 * MIX_W] * jnp.concatenate(o_slc[bi], axis=1)
             + gates[:, 2 * MIX_W:3 * MIX_W] * jnp.concatenate(o_win[bi], axis=1))
        o_ref[bi] = o.astype(o_ref.dtype)


def _nsa(zn, zm, ck, cv, covt, gexp):
    b, s, _ = zn.shape
    tq = TQ_NSA
    nb = NSA_SEQS if b % NSA_SEQS == 0 else 1
    n_chains = nb * NSA_GROUPS
    return pl.pallas_call(
        _nsa_kernel,
        grid=(b // nb, s // tq),
        in_specs=[pl.BlockSpec((nb, tq, 256), lambda i, j: (i, j, 0)),
                  pl.BlockSpec((nb, s, 256), lambda i, j: (i, 0, 1)),
                  pl.BlockSpec((nb, s, 256), lambda i, j: (i, 0, 2)),
                  pl.BlockSpec((nb, N_CMP_PAD, LANE), lambda i, j: (i, 0, 0)),
                  pl.BlockSpec((nb, N_CMP_PAD, LANE), lambda i, j: (i, 0, 0)),
                  pl.BlockSpec((nb, tq, ZM_W), lambda i, j: (i, j, 0)),
                  pl.BlockSpec((N_SLC_PAD, N_CMP_PAD), lambda i, j: (0, 0)),
                  pl.BlockSpec((ZM_W, 3 * MIX_W), lambda i, j: (0, 0))],
        out_specs=pl.BlockSpec((nb, tq, MIX_W), lambda i, j: (i, j, 0)),
        out_shape=jax.ShapeDtypeStruct((b, s, MIX_W), BF16),
        scratch_shapes=[pltpu.VMEM((n_chains, s, LANE), BF16)] * 3
                       + [pltpu.VMEM((n_chains, s // tq, NSA_HPG * tq, tq), F32),
                          pltpu.VMEM((n_chains, NSA_HPG * tq, LANE), F32),
                          pltpu.VMEM((n_chains, NSA_HPG * tq, LANE), F32)],
        compiler_params=_cparams(("parallel", "arbitrary")),
        name="nsa_attn",
    )(zn, zn, zn, ck, cv, zm, covt, gexp)


def _combine_kernel(x_ref, g_ref, op_ref, oa_ref, od_ref, on_ref, wg_ref, bg_ref, wb_ref, wo_ref, o_ref):
    x = x_ref[...]
    h = _rms(x, g_ref[...]).astype(BF16)
    y = jnp.zeros(x.shape, F32)
    for i, br_ref in enumerate((op_ref, oa_ref, od_ref, on_ref)):
        gate = jax.nn.sigmoid(jnp.dot(h, wg_ref[i], preferred_element_type=F32) + bg_ref[i])
        y = y + gate * jnp.dot(br_ref[...], wb_ref[i], preferred_element_type=F32)
    o_ref[...] = x + _dot(y, wo_ref[...])


def _combine(x2, g, branches, wg, bg, wb, wo):
    t = x2.shape[0]
    tm = TM_COMB
    return pl.pallas_call(
        _combine_kernel,
        grid=(t // tm,),
        in_specs=[pl.BlockSpec((tm, D_MODEL), lambda i: (i, 0)),
                  pl.BlockSpec((1, D_MODEL), lambda i: (0, 0))]
                 + [pl.BlockSpec((tm, MIX_W), lambda i: (i, 0))] * N_BRANCH
                 + [pl.BlockSpec((N_BRANCH, D_MODEL, D_MODEL), lambda i: (0, 0, 0)),
                    pl.BlockSpec((N_BRANCH, 1, D_MODEL), lambda i: (0, 0, 0)),
                    pl.BlockSpec((N_BRANCH, MIX_W, D_MODEL), lambda i: (0, 0, 0)),
                    pl.BlockSpec((D_MODEL, D_MODEL), lambda i: (0, 0))],
        out_specs=pl.BlockSpec((tm, D_MODEL), lambda i: (i, 0)),
        out_shape=jax.ShapeDtypeStruct((t, D_MODEL), F32),
        compiler_params=_cparams(("parallel",)),
        name="combine",
    )(x2, g, *branches, wg, bg, wb, wo)


def _cross_kernel(x_ref, g_ref, wq_ref, mem_ref, gm_ref, wkv_ref, wo_ref, o_ref, kv_ref):
    @pl.when(pl.program_id(1) == 0)
    def _():
        kv_ref[...] = _dot(_rms(mem_ref[0], gm_ref[...]), wkv_ref[...]).astype(kv_ref.dtype)

    n_k = X_HEADS * X_DH
    subs = [slice(i * CROSS_SUB, (i + 1) * CROSS_SUB) for i in range(x_ref.shape[1] // CROSS_SUB)]
    x = [x_ref[0, r, :] for r in subs]
    q = [_dot(_rms(a, g_ref[...]), wq_ref[...]) * X_DH ** -0.5 for a in x]
    pairs = [(i, h) for h in range(X_HEADS) for i in range(len(subs))]
    head = lambda h: slice(h * X_DH, (h + 1) * X_DH)
    sc = [_dot_nt(q[i][:, head(h)], kv_ref[:, head(h)]) for i, h in pairs]
    e = [jnp.exp(s - jnp.max(s, axis=-1, keepdims=True)) for s in sc]
    o = [_dot(e_ih, kv_ref[:, n_k + h * X_DH:n_k + (h + 1) * X_DH]) / jnp.sum(e_ih, axis=-1, keepdims=True)
         for e_ih, (i, h) in zip(e, pairs)]
    for i, r in enumerate(subs):
        o_i = jnp.concatenate([o[pairs.index((i, h))] for h in range(X_HEADS)], axis=1)
        o_ref[0, r, :] = x[i] + _dot(o_i, wo_ref[...])


def _cross(x3, g, wq, mem, g_mem, wkv, wo):
    b, s, _ = x3.shape
    tm = TM_CROSS
    m = mem.shape[1]
    n_k = X_HEADS * X_DH
    return pl.pallas_call(
        _cross_kernel,
        grid=(b, s // tm),
        in_specs=[pl.BlockSpec((1, tm, D_MODEL), lambda i, j: (i, j, 0)),
                  pl.BlockSpec((1, D_MODEL), lambda i, j: (0, 0)),
                  pl.BlockSpec((D_MODEL, n_k), lambda i, j: (0, 0)),
                  pl.BlockSpec((1, m, D_MODEL), lambda i, j: (i, 0, 0)),
                  pl.BlockSpec((1, D_MODEL), lambda i, j: (0, 0)),
                  pl.BlockSpec((D_MODEL, 2 * n_k), lambda i, j: (0, 0)),
                  pl.BlockSpec((n_k, D_MODEL), lambda i, j: (0, 0))],
        out_specs=pl.BlockSpec((1, tm, D_MODEL), lambda i, j: (i, j, 0)),
        out_shape=jax.ShapeDtypeStruct((b, s, D_MODEL), F32),
        scratch_shapes=[pltpu.VMEM((m, 2 * n_k), BF16)],
        compiler_params=_cparams(("parallel", "arbitrary")),
        name="cross_attn",
    )(x3, g, wq, mem, g_mem, wkv, wo)


def _ffn_kernel(x_ref, g_ref, wup_ref, cw_ref, cb_ref, wd_ref, gf_ref, o_ref, tail_ref, act_ref, *, final):
    @pl.when(pl.program_id(1) == 0)
    def _():
        tail_ref[...] = jnp.zeros_like(tail_ref)

    ts = x_ref.shape[1]
    x = x_ref[0]
    hn = _rms(x, g_ref[...]).astype(BF16)
    for c in range(D_FF // FF_CHUNK):
        cols = slice(c * FF_CHUNK, (c + 1) * FF_CHUNK)
        gcols = slice(D_FF + c * FF_CHUNK, D_FF + (c + 1) * FF_CHUNK)
        u = jnp.dot(hn, wup_ref[:, cols], preferred_element_type=F32)
        v = jnp.dot(hn, wup_ref[:, gcols], preferred_element_type=F32)
        ext = jnp.concatenate([tail_ref[:, cols], u], axis=0)
        tail_ref[:, cols] = u[ts - SUBLANE:, :]
        cw = cw_ref[:, cols]
        y = (cw[2:3] * u + cw[1:2] * pltpu.roll(ext, 1, 0)[SUBLANE:]
             + cw[0:1] * pltpu.roll(ext, 2, 0)[SUBLANE:] + cb_ref[:, cols])
        act_ref[:, cols] = (jax.nn.gelu(y) * v).astype(BF16)
    out = x + jnp.dot(act_ref[...], wd_ref[...], preferred_element_type=F32)
    if final:
        out = _rms(out, gf_ref[...])
    o_ref[0] = out


def _ffn(x3, g, wup, cw, cb, wd, gf, final):
    b, s, _ = x3.shape
    ts = TS_FFN
    return pl.pallas_call(
        functools.partial(_ffn_kernel, final=final),
        grid=(b, s // ts),
        in_specs=[pl.BlockSpec((1, ts, D_MODEL), lambda i, j: (i, j, 0)),
                  pl.BlockSpec((1, D_MODEL), lambda i, j: (0, 0)),
                  pl.BlockSpec((D_MODEL, 2 * D_FF), lambda i, j: (0, 0), pipeline_mode=pl.Buffered(1)),
                  pl.BlockSpec((FFN_CONV, D_FF), lambda i, j: (0, 0)),
                  pl.BlockSpec((1, D_FF), lambda i, j: (0, 0)),
                  pl.BlockSpec((D_FF, D_MODEL), lambda i, j: (0, 0), pipeline_mode=pl.Buffered(1)),
                  pl.BlockSpec((1, D_MODEL), lambda i, j: (0, 0))],
        out_specs=pl.BlockSpec((1, ts, D_MODEL), lambda i, j: (i, j, 0)),
        out_shape=jax.ShapeDtypeStruct((b, s, D_MODEL), F32),
        scratch_shapes=[pltpu.VMEM((SUBLANE, D_FF), F32), pltpu.VMEM((ts, D_FF), BF16)],
        compiler_params=_cparams(("parallel", "arbitrary")),
        name="conv_ffn",
    )(x3, g, wup, cw, cb, wd, gf)


def _inproj_columns():
    starts = np.concatenate([[0], np.cumsum(IN_SPLITS)])
    (p_in, a_q, a_k, a_v, a_r, a_lr, d_q, d_k, d_v, d_b, d_a, d_g,
     n_q, n_kc, n_vc, n_ks, n_vs, n_kw, n_vw, n_g) = [np.arange(starts[i], starts[i + 1])
                                                      for i in range(len(IN_SPLITS))]
    n_q = n_q.reshape(NSA_GROUPS, NSA_HPG, NSA_DH).transpose(1, 0, 2).reshape(-1)
    misc = np.full((ZM_W,), N_IN)
    misc[MISC_LR:MISC_LR + GLA_LOWRANK] = a_lr
    misc[MISC_B:MISC_B + GDN_HEADS] = d_b
    misc[MISC_A:MISC_A + GDN_HEADS] = d_a
    misc[MISC_G:MISC_G + 3 * NSA_HEADS] = n_g
    cols = np.concatenate([p_in, a_q, a_k, a_v, a_r, d_q, d_k, d_v, d_g,
                           n_q, n_ks, n_vs, n_kw, n_vw, n_kc, n_vc, misc])
    assert cols.shape[0] == Z_W
    return cols


def _head_expand(offset, n_heads, width):
    e = np.zeros((ZM_W, n_heads * width), np.float32)
    for h in range(n_heads):
        e[offset + h, h * width:(h + 1) * width] = 1.0
    return e


def _nsa_constants(s):
    n_cmp = s // NSA_CMP_STRIDE - NSA_CMP_LEN // NSA_CMP_STRIDE + 1
    n_slc = s // NSA_SEL_LEN
    c_start = np.arange(n_cmp) * NSA_CMP_STRIDE
    s_start = np.arange(n_slc) * NSA_SEL_LEN
    cover = np.zeros((N_CMP_PAD, N_SLC_PAD), np.float32)
    cover[:n_cmp, :n_slc] = ((c_start[:, None] <= s_start[None, :] + NSA_SEL_LEN - 1)
                             & (c_start[:, None] + NSA_CMP_LEN - 1 >= s_start[None, :]))
    gexp = np.zeros((ZM_W, 3, MIX_W), np.float32)
    for g in range(NSA_GROUPS):
        for j in range(NSA_HPG):
            slot = j * NSA_GROUPS + g
            for c in range(3):
                gexp[MISC_G + (g * NSA_HPG + j) * 3 + c, c, slot * NSA_DH:(slot + 1) * NSA_DH] = 1.0
    return jnp.asarray(cover.T, dtype=BF16), jnp.asarray(gexp.reshape(ZM_W, 3 * MIX_W), dtype=BF16)


def _block_diag(blocks):
    n, a, b = blocks.shape
    return jnp.einsum('gh,gab->gahb', jnp.eye(n, dtype=blocks.dtype), blocks).reshape(n * a, n * b)


def kernel(x, mem, g_mix, w_in, pool_w, pool_scale, gla_w_lr, gla_b_lr, gla_g_norm, gdn_conv, gdn_a_log,
           gdn_dt_bias, gdn_g_norm, nsa_pe, nsa_cmp_w1, nsa_cmp_w2, w_branch, w_gate, b_gate, w_out, g_cross,
           g_mem, w_xq, w_mem_kv, w_xo, g_ffn, w_up, ffn_conv, ffn_conv_b, w_down, g_final):
    b, s, d = x.shape
    depth = w_in.shape[0]
    t = b * s
    cols = _inproj_columns()
    covt, gexp = _nsa_constants(s)
    eba = jnp.asarray(np.concatenate([_head_expand(MISC_B, GDN_HEADS, GDN_DH),
                                      _head_expand(MISC_A, GDN_HEADS, GDN_DH)], axis=1), dtype=BF16)
    ones_bd = _block_diag(jnp.ones((GDN_HEADS, GDN_DH, GDN_DH), BF16))
    nsa_rows = np.arange(MIX_W).reshape(NSA_GROUPS, NSA_HPG, NSA_DH).transpose(1, 0, 2).reshape(-1)
    row = lambda v: v.reshape(1, -1).astype(F32)
    misc_a = lambda v: jnp.zeros((1, ZM_W), F32).at[0, MISC_A:MISC_A + GDN_HEADS].set(v)

    x2 = x.reshape(t, d)
    for l in range(depth):
        w_in_r = jnp.concatenate([w_in[l], jnp.zeros((d, 1), F32)], axis=1)[:, cols].astype(BF16)
        wlr = jnp.zeros((ZM_W, GLA_HEADS * GLA_DK), F32).at[MISC_LR:MISC_LR + GLA_LOWRANK].set(gla_w_lr[l])
        wlr_hi = wlr.astype(BF16)
        wlr2 = jnp.stack([wlr_hi, (wlr - wlr_hi.astype(F32)).astype(BF16)])
        o_pool, zn, zc, zm, o_gla, o_gdn = _mixers(
            x2.reshape(b, s, d), row(g_mix[l]), w_in_r, _block_diag(pool_w[l]).astype(BF16), row(pool_scale[l]),
            wlr2, row(gla_b_lr[l]), row(jnp.tile(gla_g_norm[l], GLA_HEADS)),
            gdn_conv[l], eba, misc_a(gdn_a_log[l]), misc_a(gdn_dt_bias[l]),
            row(jnp.tile(gdn_g_norm[l], GDN_HEADS)), ones_bd)

        pe_x = jnp.tile(nsa_pe[l], (1, 1, NSA_GROUPS))
        w1 = nsa_cmp_w1[l].reshape(2, NSA_CMP_LEN, NSA_DH, NSA_DH)
        eye_g = jnp.eye(NSA_GROUPS, dtype=F32)
        w1_bd = jnp.einsum('gh,kpde->kpgdhe', eye_g, w1).reshape(2, NSA_CMP_LEN, LANE, LANE).astype(BF16)
        w2_bd = jnp.einsum('gh,kde->kgdhe', eye_g, nsa_cmp_w2[l]).reshape(2, LANE, LANE).astype(BF16)
        ck, cv = _nsa_compress(zc, pe_x, w1_bd, w2_bd)
        o_nsa = _nsa(zn, zm, ck, cv, covt, gexp)

        wb = jnp.concatenate([w_branch[l, :3], w_branch[l, 3][nsa_rows][None]], axis=0).astype(BF16)
        branches = [o.reshape(t, MIX_W) for o in (o_pool, o_gla, o_gdn, o_nsa)]
        x2 = _combine(x2, row(g_mix[l]), branches, w_gate[l].astype(BF16),
                      b_gate[l].reshape(N_BRANCH, 1, d), wb, w_out[l].astype(BF16))

        x3 = _cross(x2.reshape(b, s, d), row(g_cross[l]), w_xq[l].astype(BF16), mem, row(g_mem[l]),
                    w_mem_kv[l].astype(BF16), w_xo[l].astype(BF16))

        x3 = _ffn(x3, row(g_ffn[l]), w_up[l].astype(BF16), ffn_conv[l], row(ffn_conv_b[l]),
                  w_down[l].astype(BF16), row(g_final), final=(l == depth - 1))
        x2 = x3.reshape(t, d)
    return x2.reshape(b, s, d)
```

```python
import functools

import numpy as np
import jax
import jax.numpy as jnp
from jax import lax
from jax.experimental import pallas as pl
from jax.experimental.pallas import tpu as pltpu

F32 = jnp.float32
BF16 = jnp.bfloat16

D_MODEL = 1024
MIX_W = 256
POOL_WINDOWS = (2, 4, 8, 16)
POOL_GW = 64
GLA_HEADS = 4
GLA_DK = 32
GLA_DV = 64
GLA_LOWRANK = 16
GLA_GATE_NORM = 16.0
CHUNK = 64
GDN_HEADS = 4
GDN_DH = 64
GDN_CONV = 4
NSA_HEADS = 4
NSA_GROUPS = 2
NSA_HPG = 2
NSA_DH = 64
NSA_KV = 128
NSA_CMP_LEN = 32
NSA_CMP_STRIDE = 16
NSA_SEL_LEN = 64
NSA_N_SEL = 16
NSA_WINDOW = 512
X_HEADS = 4
X_DH = 128
D_FF = 2816
FFN_CONV = 3
EPS = 1e-6
N_BRANCH = 4

IN_SPLITS = (MIX_W,
             128, 128, 256, 256, GLA_LOWRANK,
             MIX_W, MIX_W, MIX_W, GDN_HEADS, GDN_HEADS, MIX_W,
             256, NSA_KV, NSA_KV, NSA_KV, NSA_KV, NSA_KV, NSA_KV, 3 * NSA_HEADS)
N_IN = sum(IN_SPLITS)

MISC_LR = 0
MISC_B = 16
MISC_A = 20
MISC_G = 24
LANE = 128
SUBLANE = 8

Z_WIDTHS = (256, 768, 1024, 768, 256, 128)
GDN_QKV_W = 3 * MIX_W
ZP_W, ZG_W, ZD_W, ZN_W, ZC_W, ZM_W = Z_WIDTHS
Z_W = sum(Z_WIDTHS)
NEG_BIG = -1e30
F32_LOWEST = float(np.finfo(np.float32).min)

TM_COMB = 512
TM_CROSS = 1024
CROSS_SUB = 512
TS_FFN = 1024
FF_CHUNK = 256
TQ_NSA = 256
NSA_SEQS = 2
REC_SEQS = 16
REC_LAG = 3
GLA_START = 6
N_SLC_PAD = 128
N_CMP_PAD = 128

VMEM_LIMIT = 56 * 1024 * 1024


def _cparams(sem):
    return pltpu.CompilerParams(dimension_semantics=sem, vmem_limit_bytes=VMEM_LIMIT)


def _rms(x, g):
    return x * lax.rsqrt(jnp.mean(x * x, axis=-1, keepdims=True) + EPS) * g


def _dot(a, b):
    return jnp.dot(a.astype(BF16), b.astype(BF16), preferred_element_type=F32)


def _dot_nt(a, b):
    return lax.dot_general(a.astype(BF16), b.astype(BF16), (((1,), (1,)), ((), ())),
                           preferred_element_type=F32)


def _dot_tn(a, b):
    return lax.dot_general(a.astype(BF16), b.astype(BF16), (((0,), (0,)), ((), ())),
                           preferred_element_type=F32)


def _split2_dot(a, b):
    hi = a.astype(BF16)
    lo = (a - hi.astype(F32)).astype(BF16)
    return jnp.dot(hi, b, preferred_element_type=F32) + jnp.dot(lo, b, preferred_element_type=F32)


def _split3_lhs_dot(a, b):
    hi = a.astype(BF16)
    r1 = a - hi.astype(F32)
    mid = r1.astype(BF16)
    lo = (r1 - mid.astype(F32)).astype(BF16)
    return ((jnp.dot(hi, b, preferred_element_type=F32) + jnp.dot(mid, b, preferred_element_type=F32))
            + jnp.dot(lo, b, preferred_element_type=F32))


def _iota(shape, axis):
    return lax.broadcasted_iota(jnp.int32, shape, axis)


def _block_mask(rows, cols, rb, cb):
    return (_iota((rows, cols), 0) // rb) == (_iota((rows, cols), 1) // cb)


def _shift_rows(x, k):
    t = _iota(x.shape, 0)
    return jnp.where(t >= k, pltpu.roll(x, k, 0), 0.0)


def _cumsum_rows(x):
    k = 1
    while k < x.shape[0]:
        x = x + _shift_rows(x, k)
        k *= 2
    return x


def _softplus(x):
    return jnp.maximum(x, 0.0) + jnp.log1p(jnp.exp(-jnp.abs(x)))


def _log_sigmoid(x):
    return -_softplus(-x)


def _silu(x):
    return x * jax.nn.sigmoid(x)


def _masked_softmax(s, mask):
    s = jnp.where(mask, s, -jnp.inf)
    m = jnp.maximum(jnp.max(s, axis=-1, keepdims=True), F32_LOWEST)
    e = jnp.exp(s - m)
    den = jnp.sum(e, axis=-1, keepdims=True)
    return e / jnp.where(den > 0, den, 1.0)


def _tile4(x):
    return jnp.concatenate([x, x, x, x], axis=0)


POOL_HALO = 16


def _gla_masks():
    return (_block_mask(4 * CHUNK, 128, CHUNK, GLA_DK),
            _block_mask(4 * CHUNK, 256, CHUNK, GLA_DV),
            _block_mask(256, 128, GLA_DV, GLA_DK),
            (_iota((CHUNK, 256), 1) % CHUNK) <= _iota((CHUNK, 256), 0))


def _seqs(f, *lists):
    return [f(*args) for args in zip(*lists)]


def _run_staggered(stage_gens, starts, periods):
    live = list(range(len(stage_gens)))
    tick = 0
    while live:
        for i in list(live):
            if tick >= starts[i] and (tick - starts[i]) % periods[i] == 0:
                try:
                    next(stage_gens[i])
                except StopIteration:
                    live.remove(i)
        tick += 1


def _gla_chunk(masks, zqk, v, r, zm, wlr_ref, blr_ref, gn_ref, ones_ref, st_refs, out):
    mask_k, mask_v, mask_st, causal = masks
    q = [z[:, 0:128] * GLA_DK ** -0.5 for z in zqk]
    k = [z[:, 128:256] for z in zqk]
    pre = _seqs(lambda a: _dot(a, wlr_ref[...]), zm)
    yield
    bc = _seqs(lambda a: _cumsum_rows(_log_sigmoid(a + blr_ref[...]) / GLA_GATE_NORM), pre)
    bl = [a[CHUNK - 1:CHUNK, :] for a in bc]
    q_e = _seqs(lambda a, c: a * jnp.exp(c), q, bc)
    k_e = _seqs(lambda a, c: a * jnp.exp(-c), k, bc)
    k_u = _seqs(lambda a, c, l: a * jnp.exp(l - c), k, bc, bl)
    st = [ref[...] for ref in st_refs]
    yield
    att = _seqs(lambda a, b: jnp.where(causal, _dot_nt(a, jnp.where(mask_k, _tile4(b), 0.0)), 0.0), q_e, k_e)
    inter = _seqs(_dot_nt, q_e, st)
    kv = _seqs(_dot_tn, v, k_u)
    yield
    o = _seqs(lambda a, b, c: _dot(a, _tile4(b.astype(BF16)) * ones_ref[...]) + c, att, v, inter)
    for ref, s_old, l, upd in zip(st_refs, st, bl, kv):
        ref[...] = s_old * jnp.exp(l) + jnp.where(mask_st, upd, 0.0)
    yield
    ms = _seqs(lambda a: _dot(a * a, ones_ref[...]) * (1.0 / GLA_DV), o)
    yield
    out[:] = _seqs(lambda a, m, g: a * lax.rsqrt(m + EPS) * gn_ref[...] * _silu(g), o, ms, r)


GDN_HALO = SUBLANE


def _gdn_masks():
    c4 = 4 * CHUNK
    col = _iota((CHUNK, c4), 1) % CHUNK
    row = _iota((CHUNK, c4), 0)
    return (_block_mask(c4, c4, CHUNK, CHUNK), col <= row, col < row, col == row)


def _gdn_chunk(masks, halo, zqkv, gate, zm, cw_ref, eba_ref, alog_ref, dtb_ref, gn_ref, ones_ref, st_refs, out):
    mask_bd, incl, strict, diag = masks
    cw = cw_ref[...]
    bd = lambda a: _tile4(a.astype(BF16)) * ones_ref[...]

    def conv_silu(h, cur):
        ext = jnp.concatenate([h, cur], axis=0)
        conv = (cw[3:4] * cur + cw[2:3] * pltpu.roll(ext, 1, 0)[GDN_HALO:]
                + cw[1:2] * pltpu.roll(ext, 2, 0)[GDN_HALO:] + cw[0:1] * pltpu.roll(ext, 3, 0)[GDN_HALO:])
        return _silu(conv)

    qkv = _seqs(conv_silu, halo, zqkv)
    yield
    v = [a[:, 512:768] for a in qkv]
    ssq = _seqs(lambda a: _dot(jnp.concatenate([a[:, 0:256] * a[:, 0:256], a[:, 256:512] * a[:, 256:512]],
                                               axis=0), ones_ref[...]), qkv)
    yield
    q = _seqs(lambda a, s: a[:, 0:256] * lax.rsqrt(s[0:CHUNK] + EPS) * GDN_DH ** -0.5, qkv, ssq)
    k = _seqs(lambda a, s: a[:, 256:512] * lax.rsqrt(s[CHUNK:2 * CHUNK] + EPS), qkv, ssq)
    beta = _seqs(lambda a: _dot(jax.nn.sigmoid(a), eba_ref[:, 0:MIX_W]), zm)
    yield
    gc = _seqs(lambda a: _split3_lhs_dot(_cumsum_rows(-jnp.exp(alog_ref[...]) * _softplus(a + dtb_ref[...])),
                                         eba_ref[:, MIX_W:2 * MIX_W]), zm)
    yield
    gl = [a[CHUNK - 1:CHUNK, :] for a in gc]
    g_row = _seqs(lambda a: jnp.sum(jnp.where(diag, a, 0.0), axis=0, keepdims=True), gc)
    yield
    decay =_seqs(lambda a, b: jnp.exp(jnp.where(incl, a - b, -jnp.inf)), gc, g_row)
    kb = _seqs(lambda a, b: a * b, k, beta)
    vb = _seqs(lambda a, b: a * b, v, beta)
    kq = _seqs(lambda a, b, c: _dot_nt(jnp.concatenate([a, b], axis=0), bd(c)), kb, q, k)
    yield
    n_mat =_seqs(lambda s, d: jnp.where(strict, s[0:CHUNK] * d, 0.0), kq, decay)
    a_qk = _seqs(lambda s, d: s[CHUNK:2 * CHUNK] * d, kq, decay)
    m = _seqs(lambda a: -a, n_mat)
    x = _seqs(lambda a: jnp.where(diag, 1.0, 0.0) + a, m)
    m = _seqs(lambda a: _dot(a, bd(a)), m)
    yield
    p = 2
    while p < CHUNK // 2:
        xm = _seqs(lambda a, b: _dot(jnp.concatenate([a, b], axis=0), bd(b)), x, m)
        x = _seqs(lambda a, r: a + r[0:CHUNK], x, xm)
        m = [r[CHUNK:2 * CHUNK] for r in xm]
        p *= 2
        yield
    x = _seqs(lambda a, b: a + _dot(a, bd(b)), x, m)
    yield
    uw = _seqs(lambda a, b, c, g: _dot(a, jnp.concatenate([bd(b), bd(c * jnp.exp(g))], axis=1)),
               x, vb, kb, gc)
    yield
    st = [ref[...] for ref in st_refs]
    ws = _seqs(lambda a, b, g, s: _dot(jnp.concatenate([a[:, 256:512], b * jnp.exp(g)], axis=0), s),
               uw, q, gc, st)
    yield
    v_new = _seqs(lambda a, r: a[:, 0:256] - r[0:CHUNK], uw, ws)
    o = _seqs(lambda r, qk, vn: r[CHUNK:2 * CHUNK] + _dot(qk, bd(vn)), ws, a_qk, v_new)
    upd = _seqs(lambda a, g, l, vn: _dot_tn(a * jnp.exp(l - g), vn), k, gc, gl, v_new)
    yield
    for ref, s_old, l, u in zip(st_refs, st, gl, upd):
        ref[...] = s_old * jnp.exp(l) + jnp.where(mask_bd, u, 0.0)
    ms = _seqs(lambda a: _dot(a * a, ones_ref[...]) * (1.0 / GDN_DH), o)
    yield
    out[:] = _seqs(lambda a, m_, g: a * lax.rsqrt(m_ + EPS) * gn_ref[...] * _silu(g), o, ms, gate)


def _pool_diff(u, tail, pos0):
    ext = jnp.concatenate([tail, u], axis=0)
    s2 = ext + pltpu.roll(ext, 1, 0)
    s4 = s2 + pltpu.roll(s2, 2, 0)
    s8 = s4 + pltpu.roll(s4, 4, 0)
    s16 = s8 + pltpu.roll(s8, 8, 0)
    grp = _iota(u.shape, 1) // POOL_GW
    win = jnp.where(grp == 0, s2[POOL_HALO:], jnp.where(grp == 1, s4[POOL_HALO:],
                                                        jnp.where(grp == 2, s8[POOL_HALO:], s16[POOL_HALO:])))
    width = jnp.where(grp == 0, POOL_WINDOWS[0],
                      jnp.where(grp == 1, POOL_WINDOWS[1],
                                jnp.where(grp == 2, POOL_WINDOWS[2], POOL_WINDOWS[3])))
    cnt = jnp.minimum(pos0 + _iota(u.shape, 0) + 1, width).astype(F32)
    return win / cnt - u


def _mix_kernel(x_ref, g_ref, w_ref, pw_ref, ps_ref, wlr_ref, blr_ref, gng_ref, cw_ref, eba_ref, alog_ref, dtb_ref,
                gnd_ref, ones_ref, op_ref, zn_ref, zc_ref, zm_ref, og_ref, od_ref, ptail_ref, halo_ref, *st_refs):
    nb = x_ref.shape[0]
    chunk = pl.program_id(1)

    @pl.when(chunk == 0)
    def _():
        ptail_ref[...] = jnp.zeros_like(ptail_ref)
        halo_ref[...] = jnp.zeros_like(halo_ref)
        for st_ref in st_refs:
            st_ref[...] = jnp.zeros_like(st_ref)

    h = _rms(x_ref[...].reshape(nb * CHUNK, D_MODEL), g_ref[...]).astype(BF16)
    proj = lambda lo, n: jnp.dot(h, w_ref[:, lo:lo + n], preferred_element_type=F32)
    seqs = lambda a: [a[i * CHUNK:(i + 1) * CHUNK] for i in range(nb)]
    off_g, off_d = ZP_W, ZP_W + ZG_W
    off_n = off_d + ZD_W
    off_c = off_n + ZN_W
    off_m = off_c + ZC_W
    zqkv = seqs(proj(off_d, GDN_QKV_W))
    zm = seqs(proj(off_m, ZM_W))
    halo = [halo_ref[i] for i in range(nb)]
    for i in range(nb):
        halo_ref[i] = zqkv[i][CHUNK - GDN_HALO:, :]
        zm_ref[i] = zm[i]
    halves = [slice(0, nb // 2), slice(nb // 2, nb)] if nb > 1 else [slice(0, nb)]
    gla_qk, gla_v, gla_r = [], [], []
    gate_parts = [[] for _ in halves]

    def other_projections():
        gla_qk.extend(seqs(proj(off_g, 256)))
        yield
        gla_v.extend(seqs(proj(off_g + 256, 256)))
        yield
        u = seqs(proj(0, ZP_W))
        yield
        diff = [_pool_diff(u[i], ptail_ref[i], chunk * CHUNK) for i in range(nb)]
        for i in range(nb):
            ptail_ref[i] = u[i][CHUNK - POOL_HALO:, :]
        yield
        for lo in range(0, ZN_W, 256):
            zn = seqs(proj(off_n + lo, 256))
            for i in range(nb):
                zn_ref[i, :, lo:lo + 256] = zn[i].astype(zn_ref.dtype)
            yield
        zc = seqs(proj(off_c, ZC_W))
        for i in range(nb):
            zc_ref[i] = zc[i]
        yield
        gla_r.extend(seqs(proj(off_g + 512, 256)))
        yield
        gate = seqs(proj(off_d + GDN_QKV_W, MIX_W))
        for hs, part in zip(halves, gate_parts):
            part.extend(gate[hs])
        yield
        o_pool = seqs(_dot(jnp.concatenate(diff, axis=0), pw_ref[...]) * ps_ref[...])
        for i in range(nb):
            op_ref[i] = o_pool[i].astype(op_ref.dtype)

    og = [None] * nb
    od_parts = [[None] * (hs.stop - hs.start) for hs in halves]
    gdn_masks = _gdn_masks()
    gens = [_gdn_chunk(gdn_masks, halo[hs], zqkv[hs], gate_part, zm[hs], cw_ref, eba_ref, alog_ref, dtb_ref,
                       gnd_ref, ones_ref, st_refs[nb:][hs], part)
            for hs, part, gate_part in zip(halves, od_parts, gate_parts)]
    gens.append(_gla_chunk(_gla_masks(), gla_qk, gla_v, gla_r, zm, wlr_ref, blr_ref, gng_ref, ones_ref,
                           st_refs[:nb], og))
    starts = [REC_LAG * i for i in range(len(halves))] + [GLA_START]
    gens.append(other_projections())
    _run_staggered(gens, starts + [0], [1] * len(gens))
    od = [o for part in od_parts for o in part]
    for i in range(nb):
        og_ref[i] = og[i].astype(og_ref.dtype)
        od_ref[i] = od[i].astype(od_ref.dtype)


def _mixers(x3, g, w, pool_w_bd, pool_scale, wlr2, blr, gn_gla, cw, eba, alog, dtb, gn_gdn, ones_bd):
    b, s, _ = x3.shape
    nb = REC_SEQS if b % REC_SEQS == 0 else 1
    const = lambda shape: pl.BlockSpec(shape, lambda i, c: (0,) * len(shape))
    tok = lambda n: pl.BlockSpec((nb, CHUNK, n), lambda i, c: (i, c, 0))
    outs = ((MIX_W, BF16), (ZN_W, BF16), (ZC_W, F32), (ZM_W, F32), (MIX_W, BF16), (MIX_W, BF16))
    return pl.pallas_call(
        _mix_kernel,
        grid=(b // nb, s // CHUNK),
        in_specs=[tok(D_MODEL), const((1, D_MODEL)), const((D_MODEL, Z_W)), const((MIX_W, MIX_W)), const((1, MIX_W)),
                  const((ZM_W, 128)), const((1, 128)), const((1, MIX_W)),
                  const((GDN_CONV, GDN_QKV_W)), const((ZM_W, 2 * MIX_W)), const((1, ZM_W)), const((1, ZM_W)),
                  const((1, MIX_W)), const((MIX_W, MIX_W))],
        out_specs=[tok(n) for n, _ in outs],
        out_shape=[jax.ShapeDtypeStruct((b, s, n), dt) for n, dt in outs],
        scratch_shapes=([pltpu.VMEM((nb, POOL_HALO, MIX_W), F32), pltpu.VMEM((nb, GDN_HALO, GDN_QKV_W), F32)]
                        + [pltpu.VMEM((256, 128), F32)] * nb + [pltpu.VMEM((256, 256), F32)] * nb),
        compiler_params=_cparams(("parallel", "arbitrary")),
        name="inproj_mixers",
    )(x3, g, w, pool_w_bd, pool_scale, wlr2, blr, gn_gla, cw, eba, alog, dtb, gn_gdn, ones_bd)


def _cmp_kernel(kc_ref, vc_ref, pe_ref, w1_ref, w2_ref, ck_ref, cv_ref):
    n_sub = NSA_CMP_LEN // NSA_CMP_STRIDE
    n_chunks = kc_ref.shape[1] // NSA_CMP_STRIDE
    for which, (src_ref, out_ref) in enumerate(((kc_ref, ck_ref), (vc_ref, cv_ref))):
        parts = [jnp.zeros((n_chunks, LANE), F32) for _ in range(n_sub)]
        for i in range(NSA_CMP_STRIDE):
            slab = src_ref[0, pl.ds(i, n_chunks, stride=NSA_CMP_STRIDE), :]
            for sub in range(n_sub):
                p = sub * NSA_CMP_STRIDE + i
                parts[sub] = parts[sub] + _dot(slab + pe_ref[which, p:p + 1, :], w1_ref[which, p])
        pre = parts[0] + pltpu.roll(parts[1], n_chunks - 1, 0)
        out_ref[0] = _dot(jax.nn.gelu(pre), w2_ref[which])


def _nsa_compress(zc, pe_x, w1_bd, w2_bd):
    b, s, _ = zc.shape
    n_chunks = s // NSA_CMP_STRIDE
    return pl.pallas_call(
        _cmp_kernel,
        grid=(b,),
        in_specs=[pl.BlockSpec((1, s, LANE), lambda i: (i, 0, 0)),
                  pl.BlockSpec((1, s, LANE), lambda i: (i, 0, 1)),
                  pl.BlockSpec((2, NSA_CMP_LEN, LANE), lambda i: (0, 0, 0)),
                  pl.BlockSpec((2, NSA_CMP_LEN, LANE, LANE), lambda i: (0, 0, 0, 0)),
                  pl.BlockSpec((2, LANE, LANE), lambda i: (0, 0, 0))],
        out_specs=[pl.BlockSpec((1, n_chunks, LANE), lambda i: (i, 0, 0))] * 2,
        out_shape=[jax.ShapeDtypeStruct((b, n_chunks, LANE), F32)] * 2,
        compiler_params=_cparams(("parallel",)),
        name="nsa_compress",
    )(zc, zc, pe_x, w1_bd, w2_bd)


def _topk_rank(v):
    n, t = v.shape
    blocks = [v[r:r + SUBLANE] for r in range(0, n, SUBLANE)]
    sub = _iota((SUBLANE, t), 0)
    rank = [jnp.zeros((SUBLANE, t), jnp.int32) for _ in blocks]
    for m in range(n):
        other = v[m:m + 1, :]
        mb, ms = divmod(m, SUBLANE)
        for r, blk in enumerate(blocks):
            if r < mb:
                ahead = (other > blk).astype(jnp.int32)
            elif r > mb:
                ahead = (other >= blk).astype(jnp.int32)
            else:
                ahead = jnp.where(sub > ms, (other >= blk).astype(jnp.int32), (other > blk).astype(jnp.int32))
            rank[r] = rank[r] + ahead
    return jnp.concatenate(rank, axis=0)


def _exp_weights(s, m):
    return jnp.exp((s - m).astype(BF16))


def _normalise_aug(o_aug, in_grp):
    den = pltpu.roll(o_aug, NSA_DH, 1)
    return jnp.where(in_grp, o_aug / jnp.where(den > 0, den, 1.0), 0.0)


def _nsa_kernel(q_ref, ksv_ref, kwv_ref, ck_ref, cv_ref, zm_ref, covt_ref, gexp_ref, o_ref,
                kaug_ref, vaug_ref, vwaug_ref, s_ref, mx_ref, acc_ref):
    nb = q_ref.shape[0]
    tq = q_ref.shape[1]
    s_len = ksv_ref.shape[1]
    n_slc = s_len // NSA_SEL_LEN
    qi = pl.program_id(1)
    s0 = pl.multiple_of(qi * tq, tq)
    lane_grp = _iota((1, LANE), 1) // NSA_DH
    chains = [(bi, g) for bi in range(nb) for g in range(NSA_GROUPS)]
    n_chains = range(len(chains))

    @pl.when(qi == 0)
    def _():
        lane = _iota((s_len, LANE), 1)
        key_blk = _iota((s_len, LANE), 0) // NSA_SEL_LEN
        for bi in range(nb):
            k = ksv_ref[bi, :, 0:LANE].astype(F32)
            v = ksv_ref[bi, :, LANE:2 * LANE].astype(F32)
            vw = kwv_ref[bi, :, LANE:2 * LANE].astype(F32)
            for g in range(NSA_GROUPS):
                own = lane // NSA_DH == g
                onehot = jnp.where(lane - (1 - g) * NSA_DH == key_blk, 1.0, 0.0)
                kaug_ref[bi * NSA_GROUPS + g] = jnp.where(own, k, onehot).astype(BF16)
                vaug_ref[bi * NSA_GROUPS + g] = jnp.where(own, v, 1.0).astype(BF16)
                vwaug_ref[bi * NSA_GROUPS + g] = jnp.where(own, vw, 1.0).astype(BF16)

    t_col = s0 + _iota((tq, 1), 0)
    t_col2 = jnp.concatenate([t_col, t_col], axis=0)
    t_row = s0 + _iota((1, tq), 1)
    q = [q_ref[bi].astype(F32) * NSA_DH ** -0.5 for bi in range(nb)]
    ck_hi = [ck_ref[bi].astype(BF16) for bi in range(nb)]
    ck_lo = [(ck_ref[bi] - ck_hi[bi].astype(F32)).astype(BF16) for bi in range(nb)]
    cv = [cv_ref[bi] for bi in range(nb)]
    cmp_end = _iota((1, N_CMP_PAD), 1) * NSA_CMP_STRIDE + (NSA_CMP_LEN - 1)
    cmp_valid = cmp_end <= t_col2
    blk_t = _iota((n_slc, tq), 0)
    cur_t = t_row // NSA_SEL_LEN
    forced_t = (blk_t == 0) | (blk_t == cur_t) | (blk_t == cur_t - 1)
    future_t = blk_t > cur_t
    row_in_tile = jnp.concatenate([_iota((tq, tq), 0)] * NSA_HPG, axis=0)
    col_in_tile = _iota((NSA_HPG * tq, tq), 1)
    diag_ok = col_in_tile <= row_in_tile
    n_wt = NSA_WINDOW // tq + 1
    never = 2 * tq
    win_off = [pl.multiple_of(jnp.maximum(qi - (n_wt - 1 - w), 0) * tq, tq) for w in range(n_wt)]
    win_ok = [col_in_tile > row_in_tile + jnp.where(qi >= n_wt - 1, 0, never)]
    win_ok += [col_in_tile >= jnp.where(qi >= n_wt - 1 - w, 0, never) for w in range(1, n_wt - 1)]
    win_ok += [diag_ok]

    o_cmp = [[jnp.zeros((tq, LANE), F32) for _ in range(NSA_HPG)] for _ in range(nb)]
    o_slc = [[jnp.zeros((tq, LANE), F32) for _ in range(NSA_HPG)] for _ in range(nb)]
    o_win = [[jnp.zeros((tq, LANE), F32) for _ in range(NSA_HPG)] for _ in range(nb)]
    in_grp = [lane_grp == g for _, g in chains]
    q2 = [jnp.concatenate([jnp.where(in_grp[c], q[bi][:, j * LANE:(j + 1) * LANE], 0.0)
                           for j in range(NSA_HPG)], axis=0) for c, (bi, _) in enumerate(chains)]
    qb = [a.astype(BF16) for a in q2]
    p_c = [_masked_softmax(_dot_nt(qb[c], ck_hi[bi]) + _dot_nt(qb[c], ck_lo[bi]), cmp_valid)
           for c, (bi, _) in enumerate(chains)]
    oc = [jnp.where(in_grp[c], _dot(p_c[c], cv[bi]), 0.0) for c, (bi, _) in enumerate(chains)]
    imp_c = [a[0:tq] + a[tq:2 * tq] for a in p_c]
    imp_hi = [a.astype(BF16) for a in imp_c]
    imp_lo = [(a - h.astype(F32)).astype(BF16) for a, h in zip(imp_c, imp_hi)]
    imp_t = [(_dot_nt(covt_ref[...], h) + _dot_nt(covt_ref[...], l_))[0:n_slc]
             for h, l_ in zip(imp_hi, imp_lo)]
    imp_t = [jnp.where(forced_t, jnp.inf, jnp.where(future_t, -jnp.inf, a)) for a in imp_t]
    s_w = [[jnp.where(ok, _dot_nt(qb[c], kwv_ref[bi, pl.ds(off, tq), 0:LANE]), -jnp.inf)
            for ok, off in zip(win_ok, win_off)] for c, (bi, _) in enumerate(chains)]
    qa = []
    for c, (_, g) in enumerate(chains):
        bias_t = jnp.where(_topk_rank(imp_t[c]) < NSA_N_SEL, 0.0, NEG_BIG)
        lo = (1 - g) * NSA_DH
        rows = ([jnp.zeros((lo, tq), F32)] if lo else []) + [bias_t, jnp.zeros((LANE - lo - n_slc, tq), F32)]
        bias = jnp.concatenate(rows, axis=0).T
        qa.append(jnp.where(in_grp[c], q2[c], jnp.concatenate([bias] * NSA_HPG, axis=0)).astype(BF16))
    m_w = []
    for c in n_chains:
        m_c = s_w[c][0]
        for s_t in s_w[c][1:]:
            m_c = jnp.maximum(m_c, s_t)
        m_w.append(jnp.maximum(jnp.max(m_c, axis=-1, keepdims=True), F32_LOWEST))
    ow = [jnp.zeros((NSA_HPG * tq, LANE), F32) for _ in n_chains]
    for w, off in enumerate(win_off):
        for c in n_chains:
            ow[c] = ow[c] + jnp.dot(_exp_weights(s_w[c][w], m_w[c]), vwaug_ref[c, pl.ds(off, tq), :],
                                    preferred_element_type=F32)
    for c, (bi, _) in enumerate(chains):
        ow_c = _normalise_aug(ow[c], in_grp[c])
        for j in range(NSA_HPG):
            rows_j = slice(j * tq, (j + 1) * tq)
            o_cmp[bi][j] = o_cmp[bi][j] + oc[c][rows_j]
            o_win[bi][j] = o_win[bi][j] + ow_c[rows_j]

    half_max = lambda s_t: jnp.maximum(s_t[:, 0:LANE], s_t[:, LANE:2 * LANE])
    mx_ref[...] = jnp.full(mx_ref.shape, NEG_BIG, F32)
    groups = n_chains

    def score_tiles(tiles):
        offs = [pl.multiple_of(kt * tq, tq) for kt in tiles]
        s_new = [[_dot_nt(qa[g], kaug_ref[g, pl.ds(off, tq), :]) for g in groups] for off in offs]
        for kt, s_kt in zip(tiles, s_new):
            for g in groups:
                s_ref[g, kt] = s_kt[g]
        for g in groups:
            m_new = half_max(s_new[0][g])
            for s_kt in s_new[1:]:
                m_new = jnp.maximum(m_new, half_max(s_kt[g]))
            mx_ref[g] = jnp.maximum(mx_ref[g], m_new)

    def score_pair(p, carry):
        score_tiles([2 * p, 2 * p + 1])
        return carry

    lax.fori_loop(0, qi // 2, score_pair, 0)

    @pl.when(qi % 2 == 1)
    def _():
        score_tiles([qi - 1])

    m_s = []
    for g in groups:
        s_d = jnp.where(diag_ok, _dot_nt(qa[g], kaug_ref[g, pl.ds(s0, tq), :]), NEG_BIG)
        s_ref[g, qi] = s_d
        m_g = jnp.max(jnp.maximum(mx_ref[g], half_max(s_d)), axis=-1, keepdims=True)
        m_s.append(jnp.maximum(m_g, F32_LOWEST))
    acc_ref[...] = jnp.zeros(acc_ref.shape, F32)

    def attend_tiles(tiles):
        offs = [pl.multiple_of(kt * tq, tq) for kt in tiles]
        pv = [[jnp.dot(_exp_weights(s_ref[g, kt], m_s[g]), vaug_ref[g, pl.ds(off, tq), :],
                       preferred_element_type=F32) for g in groups] for kt, off in zip(tiles, offs)]
        for g in groups:
            upd = pv[0][g]
            for pv_kt in pv[1:]:
                upd = upd + pv_kt[g]
            acc_ref[g] += upd

    def attend_pair(p, carry):
        attend_tiles([2 * p, 2 * p + 1])
        return carry

    lax.fori_loop(0, (qi + 1) // 2, attend_pair, 0)

    @pl.when(qi % 2 == 0)
    def _():
        attend_tiles([qi])

    for c, (bi, _) in enumerate(chains):
        os_ = _normalise_aug(acc_ref[c], in_grp[c])
        for j in range(NSA_HPG):
            o_slc[bi][j] = o_slc[bi][j] + os_[j * tq:(j + 1) * tq]

    for bi in range(nb):
        gates = _split2_dot(jax.nn.sigmoid(zm_ref[bi]), gexp_ref[...])
        o = (gates[:, 0:MIX_W] * jnp.concatenate(o_cmp[bi], axis=1)
             + gates[:, MIX_W:2 * MIX_W] * jnp.concatenate(o_slc[bi], axis=1)
             + gates[:, 2 * MIX_W:3 * MIX_W] * jnp.concatenate(o_win[bi], axis=1))
        o_ref[bi] = o.astype(o_ref.dtype)


def _nsa(zn, zm, ck, cv, covt, gexp):
    b, s, _ = zn.shape
    tq = TQ_NSA
    nb = NSA_SEQS if b % NSA_SEQS == 0 else 1
    n_chains = nb * NSA_GROUPS
    return pl.pallas_call(
        _nsa_kernel,
        grid=(b // nb, s // tq),
        in_specs=[pl.BlockSpec((nb, tq, 256), lambda i, j: (i, j, 0)),
                  pl.BlockSpec((nb, s, 256), lambda i, j: (i, 0, 1)),
                  pl.BlockSpec((nb, s, 256), lambda i, j: (i, 0, 2)),
                  pl.BlockSpec((nb, N_CMP_PAD, LANE), lambda i, j: (i, 0, 0)),
                  pl.BlockSpec((nb, N_CMP_PAD, LANE), lambda i, j: (i, 0, 0)),
                  pl.BlockSpec((nb, tq, ZM_W), lambda i, j: (i, j, 0)),
                  pl.BlockSpec((N_SLC_PAD, N_CMP_PAD), lambda i, j: (0, 0)),
                  pl.BlockSpec((ZM_W, 3 * MIX_W), lambda i, j: (0, 0))],
        out_specs=pl.BlockSpec((nb, tq, MIX_W), lambda i, j: (i, j, 0)),
        out_shape=jax.ShapeDtypeStruct((b, s, MIX_W), BF16),
        scratch_shapes=[pltpu.VMEM((n_chains, s, LANE), BF16)] * 3
                       + [pltpu.VMEM((n_chains, s // tq, NSA_HPG * tq, tq), F32),
                          pltpu.VMEM((n_chains, NSA_HPG * tq, LANE), F32),
                          pltpu.VMEM((n_chains, NSA_HPG * tq, LANE), F32)],
        compiler_params=_cparams(("parallel", "arbitrary")),
        name="nsa_attn",
    )(zn, zn, zn, ck, cv, zm, covt, gexp)


def _combine_kernel(x_ref, g_ref, op_ref, oa_ref, od_ref, on_ref, wg_ref, bg_ref, wb_ref, wo_ref, o_ref):
    x = x_ref[...]
    h = _rms(x, g_ref[...]).astype(BF16)
    y = jnp.zeros(x.shape, F32)
    for i, br_ref in enumerate((op_ref, oa_ref, od_ref, on_ref)):
        gate = jax.nn.sigmoid(jnp.dot(h, wg_ref[i], preferred_element_type=F32) + bg_ref[i])
        y = y + gate * jnp.dot(br_ref[...], wb_ref[i], preferred_element_type=F32)
    o_ref[...] = x + _dot(y, wo_ref[...])


def _combine(x2, g, branches, wg, bg, wb, wo):
    t = x2.shape[0]
    tm = TM_COMB
    return pl.pallas_call(
        _combine_kernel,
        grid=(t // tm,),
        in_specs=[pl.BlockSpec((tm, D_MODEL), lambda i: (i, 0)),
                  pl.BlockSpec((1, D_MODEL), lambda i: (0, 0))]
                 + [pl.BlockSpec((tm, MIX_W), lambda i: (i, 0))] * N_BRANCH
                 + [pl.BlockSpec((N_BRANCH, D_MODEL, D_MODEL), lambda i: (0, 0, 0)),
                    pl.BlockSpec((N_BRANCH, 1, D_MODEL), lambda i: (0, 0, 0)),
                    pl.BlockSpec((N_BRANCH, MIX_W, D_MODEL), lambda i: (0, 0, 0)),
                    pl.BlockSpec((D_MODEL, D_MODEL), lambda i: (0, 0))],
        out_specs=pl.BlockSpec((tm, D_MODEL), lambda i: (i, 0)),
        out_shape=jax.ShapeDtypeStruct((t, D_MODEL), F32),
        compiler_params=_cparams(("parallel",)),
        name="combine",
    )(x2, g, *branches, wg, bg, wb, wo)


def _cross_kernel(x_ref, g_ref, wq_ref, mem_ref, gm_ref, wkv_ref, wo_ref, o_ref, kv_ref):
    @pl.when(pl.program_id(1) == 0)
    def _():
        kv_ref[...] = _dot(_rms(mem_ref[0], gm_ref[...]), wkv_ref[...]).astype(kv_ref.dtype)

    n_k = X_HEADS * X_DH
    subs = [slice(i * CROSS_SUB, (i + 1) * CROSS_SUB) for i in range(x_ref.shape[1] // CROSS_SUB)]
    x = [x_ref[0, r, :] for r in subs]
    q = [_dot(_rms(a, g_ref[...]), wq_ref[...]) * X_DH ** -0.5 for a in x]
    pairs = [(i, h) for h in range(X_HEADS) for i in range(len(subs))]
    head = lambda h: slice(h * X_DH, (h + 1) * X_DH)
    sc = [_dot_nt(q[i][:, head(h)], kv_ref[:, head(h)]) for i, h in pairs]
    e = [jnp.exp(s - jnp.max(s, axis=-1, keepdims=True)) for s in sc]
    o = [_dot(e_ih, kv_ref[:, n_k + h * X_DH:n_k + (h + 1) * X_DH]) / jnp.sum(e_ih, axis=-1, keepdims=True)
         for e_ih, (i, h) in zip(e, pairs)]
    for i, r in enumerate(subs):
        o_i = jnp.concatenate([o[pairs.index((i, h))] for h in range(X_HEADS)], axis=1)
        o_ref[0, r, :] = x[i] + _dot(o_i, wo_ref[...])


def _cross(x3, g, wq, mem, g_mem, wkv, wo):
    b, s, _ = x3.shape
    tm = TM_CROSS
    m = mem.shape[1]
    n_k = X_HEADS * X_DH
    return pl.pallas_call(
        _cross_kernel,
        grid=(b, s // tm),
        in_specs=[pl.BlockSpec((1, tm, D_MODEL), lambda i, j: (i, j, 0)),
                  pl.BlockSpec((1, D_MODEL), lambda i, j: (0, 0)),
                  pl.BlockSpec((D_MODEL, n_k), lambda i, j: (0, 0)),
                  pl.BlockSpec((1, m, D_MODEL), lambda i, j: (i, 0, 0)),
                  pl.BlockSpec((1, D_MODEL), lambda i, j: (0, 0)),
                  pl.BlockSpec((D_MODEL, 2 * n_k), lambda i, j: (0, 0)),
                  pl.BlockSpec((n_k, D_MODEL), lambda i, j: (0, 0))],
        out_specs=pl.BlockSpec((1, tm, D_MODEL), lambda i, j: (i, j, 0)),
        out_shape=jax.ShapeDtypeStruct((b, s, D_MODEL), F32),
        scratch_shapes=[pltpu.VMEM((m, 2 * n_k), BF16)],
        compiler_params=_cparams(("parallel", "arbitrary")),
        name="cross_attn",
    )(x3, g, wq, mem, g_mem, wkv, wo)


def _ffn_kernel(x_ref, g_ref, wup_ref, cw_ref, cb_ref, wd_ref, gf_ref, o_ref, tail_ref, act_ref, *, final):
    @pl.when(pl.program_id(1) == 0)
    def _():
        tail_ref[...] = jnp.zeros_like(tail_ref)

    ts = x_ref.shape[1]
    x = x_ref[0]
    hn = _rms(x, g_ref[...]).astype(BF16)
    for c in range(D_FF // FF_CHUNK):
        cols = slice(c * FF_CHUNK, (c + 1) * FF_CHUNK)
        gcols = slice(D_FF + c * FF_CHUNK, D_FF + (c + 1) * FF_CHUNK)
        u = jnp.dot(hn, wup_ref[:, cols], preferred_element_type=F32)
        v = jnp.dot(hn, wup_ref[:, gcols], preferred_element_type=F32)
        ext = jnp.concatenate([tail_ref[:, cols], u], axis=0)
        tail_ref[:, cols] = u[ts - SUBLANE:, :]
        cw = cw_ref[:, cols]
        y = (cw[2:3] * u + cw[1:2] * pltpu.roll(ext, 1, 0)[SUBLANE:]
             + cw[0:1] * pltpu.roll(ext, 2, 0)[SUBLANE:] + cb_ref[:, cols])
        act_ref[:, cols] = (jax.nn.gelu(y) * v).astype(BF16)
    out = x + jnp.dot(act_ref[...], wd_ref[...], preferred_element_type=F32)
    if final:
        out = _rms(out, gf_ref[...])
    o_ref[0] = out


def _ffn(x3, g, wup, cw, cb, wd, gf, final):
    b, s, _ = x3.shape
    ts = TS_FFN
    return pl.pallas_call(
        functools.partial(_ffn_kernel, final=final),
        grid=(b, s // ts),
        in_specs=[pl.BlockSpec((1, ts, D_MODEL), lambda i, j: (i, j, 0)),
                  pl.BlockSpec((1, D_MODEL), lambda i, j: (0, 0)),
                  pl.BlockSpec((D_MODEL, 2 * D_FF), lambda i, j: (0, 0), pipeline_mode=pl.Buffered(1)),
                  pl.BlockSpec((FFN_CONV, D_FF), lambda i, j: (0, 0)),
                  pl.BlockSpec((1, D_FF), lambda i, j: (0, 0)),
                  pl.BlockSpec((D_FF, D_MODEL), lambda i, j: (0, 0), pipeline_mode=pl.Buffered(1)),
                  pl.BlockSpec((1, D_MODEL), lambda i, j: (0, 0))],
        out_specs=pl.BlockSpec((1, ts, D_MODEL), lambda i, j: (i, j, 0)),
        out_shape=jax.ShapeDtypeStruct((b, s, D_MODEL), F32),
        scratch_shapes=[pltpu.VMEM((SUBLANE, D_FF), F32), pltpu.VMEM((ts, D_FF), BF16)],
        compiler_params=_cparams(("parallel", "arbitrary")),
        name="conv_ffn",
    )(x3, g, wup, cw, cb, wd, gf)


def _inproj_columns():
    starts = np.concatenate([[0], np.cumsum(IN_SPLITS)])
    (p_in, a_q, a_k, a_v, a_r, a_lr, d_q, d_k, d_v, d_b, d_a, d_g,
     n_q, n_kc, n_vc, n_ks, n_vs, n_kw, n_vw, n_g) = [np.arange(starts[i], starts[i + 1])
                                                      for i in range(len(IN_SPLITS))]
    n_q = n_q.reshape(NSA_GROUPS, NSA_HPG, NSA_DH).transpose(1, 0, 2).reshape(-1)
    misc = np.full((ZM_W,), N_IN)
    misc[MISC_LR:MISC_LR + GLA_LOWRANK] = a_lr
    misc[MISC_B:MISC_B + GDN_HEADS] = d_b
    misc[MISC_A:MISC_A + GDN_HEADS] = d_a
    misc[MISC_G:MISC_G + 3 * NSA_HEADS] = n_g
    cols = np.concatenate([p_in, a_q, a_k, a_v, a_r, d_q, d_k, d_v, d_g,
                           n_q, n_ks, n_vs, n_kw, n_vw, n_kc, n_vc, misc])
    assert cols.shape[0] == Z_W
    return cols


def _head_expand(offset, n_heads, width):
    e = np.zeros((ZM_W, n_heads * width), np.float32)
    for h in range(n_heads):
        e[offset + h, h * width:(h + 1) * width] = 1.0
    return e


def _nsa_constants(s):
    n_cmp = s // NSA_CMP_STRIDE - NSA_CMP_LEN // NSA_CMP_STRIDE + 1
    n_slc = s // NSA_SEL_LEN
    c_start = np.arange(n_cmp) * NSA_CMP_STRIDE
    s_start = np.arange(n_slc) * NSA_SEL_LEN
    cover = np.zeros((N_CMP_PAD, N_SLC_PAD), np.float32)
    cover[:n_cmp, :n_slc] = ((c_start[:, None] <= s_start[None, :] + NSA_SEL_LEN - 1)
                             & (c_start[:, None] + NSA_CMP_LEN - 1 >= s_start[None, :]))
    gexp = np.zeros((ZM_W, 3, MIX_W), np.float32)
    for g in range(NSA_GROUPS):
        for j in range(NSA_HPG):
            slot = j * NSA_GROUPS + g
            for c in range(3):
                gexp[MISC_G + (g * NSA_HPG + j) * 3 + c, c, slot * NSA_DH:(slot + 1) * NSA_DH] = 1.0
    return jnp.asarray(cover.T, dtype=BF16), jnp.asarray(gexp.reshape(ZM_W, 3 * MIX_W), dtype=BF16)


def _block_diag(blocks):
    n, a, b = blocks.shape
    return jnp.einsum('gh,gab->gahb', jnp.eye(n, dtype=blocks.dtype), blocks).reshape(n * a, n * b)


def kernel(x, mem, g_mix, w_in, pool_w, pool_scale, gla_w_lr, gla_b_lr, gla_g_norm, gdn_conv, gdn_a_log,
           gdn_dt_bias, gdn_g_norm, nsa_pe, nsa_cmp_w1, nsa_cmp_w2, w_branch, w_gate, b_gate, w_out, g_cross,
           g_mem, w_xq, w_mem_kv, w_xo, g_ffn, w_up, ffn_conv, ffn_conv_b, w_down, g_final):
    b, s, d = x.shape
    depth = w_in.shape[0]
    t = b * s
    cols = _inproj_columns()
    covt, gexp = _nsa_constants(s)
    eba = jnp.asarray(np.concatenate([_head_expand(MISC_B, GDN_HEADS, GDN_DH),
                                      _head_expand(MISC_A, GDN_HEADS, GDN_DH)], axis=1), dtype=BF16)
    ones_bd = _block_diag(jnp.ones((GDN_HEADS, GDN_DH, GDN_DH), BF16))
    nsa_rows = np.arange(MIX_W).reshape(NSA_GROUPS, NSA_HPG, NSA_DH).transpose(1, 0, 2).reshape(-1)
    row = lambda v: v.reshape(1, -1).astype(F32)
    misc_a = lambda v: jnp.zeros((1, ZM_W), F32).at[0, MISC_A:MISC_A + GDN_HEADS].set(v)

    x2 = x.reshape(t, d)
    for l in range(depth):
        w_in_r = jnp.concatenate([w_in[l], jnp.zeros((d, 1), F32)], axis=1)[:, cols].astype(BF16)
        wlr = jnp.zeros((ZM_W, GLA_HEADS * GLA_DK), F32).at[MISC_LR:MISC_LR + GLA_LOWRANK].set(gla_w_lr[l])
        wlr2 = wlr.astype(BF16)
        o_pool, zn, zc, zm, o_gla, o_gdn = _mixers(
            x2.reshape(b, s, d), row(g_mix[l]), w_in_r, _block_diag(pool_w[l]).astype(BF16), row(pool_scale[l]),
            wlr2, row(gla_b_lr[l]), row(jnp.tile(gla_g_norm[l], GLA_HEADS)),
            gdn_conv[l], eba, misc_a(gdn_a_log[l]), misc_a(gdn_dt_bias[l]),
            row(jnp.tile(gdn_g_norm[l], GDN_HEADS)), ones_bd)

        pe_x = jnp.tile(nsa_pe[l], (1, 1, NSA_GROUPS))
        w1 = nsa_cmp_w1[l].reshape(2, NSA_CMP_LEN, NSA_DH, NSA_DH)
        eye_g = jnp.eye(NSA_GROUPS, dtype=F32)
        w1_bd = jnp.einsum('gh,kpde->kpgdhe', eye_g, w1).reshape(2, NSA_CMP_LEN, LANE, LANE).astype(BF16)
        w2_bd = jnp.einsum('gh,kde->kgdhe', eye_g, nsa_cmp_w2[l]).reshape(2, LANE, LANE).astype(BF16)
        ck, cv = _nsa_compress(zc, pe_x, w1_bd, w2_bd)
        o_nsa = _nsa(zn, zm, ck, cv, covt, gexp)

        wb = jnp.concatenate([w_branch[l, :3], w_branch[l, 3][nsa_rows][None]], axis=0).astype(BF16)
        branches = [o.reshape(t, MIX_W) for o in (o_pool, o_gla, o_gdn, o_nsa)]
        x2 = _combine(x2, row(g_mix[l]), branches, w_gate[l].astype(BF16),
                      b_gate[l].reshape(N_BRANCH, 1, d), wb, w_out[l].astype(BF16))

        x3 = _cross(x2.reshape(b, s, d), row(g_cross[l]), w_xq[l].astype(BF16), mem, row(g_mem[l]),
                    w_mem_kv[l].astype(BF16), w_xo[l].astype(BF16))

        x3 = _ffn(x3, row(g_ffn[l]), w_up[l].astype(BF16), ffn_conv[l], row(ffn_conv_b[l]),
                  w_down[l].astype(BF16), row(g_final), final=(l == depth - 1))
        x2 = x3.reshape(t, d)
    return x2.reshape(b, s, d)
```

```python
import functools

import numpy as np
import jax
import jax.numpy as jnp
from jax import lax
from jax.experimental import pallas as pl
from jax.experimental.pallas import tpu as pltpu

F32 = jnp.float32
BF16 = jnp.bfloat16

D_MODEL = 1024
MIX_W = 256
POOL_WINDOWS = (2, 4, 8, 16)
POOL_GW = 64
GLA_HEADS = 4
GLA_DK = 32
GLA_DV = 64
GLA_LOWRANK = 16
GLA_GATE_NORM = 16.0
CHUNK = 64
GDN_HEADS = 4
GDN_DH = 64
GDN_CONV = 4
NSA_HEADS = 4
NSA_GROUPS = 2
NSA_HPG = 2
NSA_DH = 64
NSA_KV = 128
NSA_CMP_LEN = 32
NSA_CMP_STRIDE = 16
NSA_SEL_LEN = 64
NSA_N_SEL = 16
NSA_WINDOW = 512
X_HEADS = 4
X_DH = 128
D_FF = 2816
FFN_CONV = 3
EPS = 1e-6
N_BRANCH = 4

IN_SPLITS = (MIX_W,
             128, 128, 256, 256, GLA_LOWRANK,
             MIX_W, MIX_W, MIX_W, GDN_HEADS, GDN_HEADS, MIX_W,
             256, NSA_KV, NSA_KV, NSA_KV, NSA_KV, NSA_KV, NSA_KV, 3 * NSA_HEADS)
N_IN = sum(IN_SPLITS)

MISC_LR = 0
MISC_B = 16
MISC_A = 20
MISC_G = 24
LANE = 128
SUBLANE = 8

Z_WIDTHS = (256, 768, 1024, 768, 256, 128)
GDN_QKV_W = 3 * MIX_W
ZP_W, ZG_W, ZD_W, ZN_W, ZC_W, ZM_W = Z_WIDTHS
Z_W = sum(Z_WIDTHS)
NEG_BIG = -1e30
F32_LOWEST = float(np.finfo(np.float32).min)

TM_COMB = 512
TM_CROSS = 1024
CROSS_SUB = 512
TS_FFN = 1024
FF_CHUNK = 256
TQ_NSA = 256
NSA_SEQS = 2
NSA_TILE_UNROLL = 3
REC_SEQS = 16
REC_LAG = 3
GLA_START = 6
N_SLC_PAD = 128
N_CMP_PAD = 128

VMEM_LIMIT = 56 * 1024 * 1024


def _cparams(sem):
    return pltpu.CompilerParams(dimension_semantics=sem, vmem_limit_bytes=VMEM_LIMIT)


def _rms(x, g):
    return x * lax.rsqrt(jnp.mean(x * x, axis=-1, keepdims=True) + EPS) * g


def _dot(a, b):
    return jnp.dot(a.astype(BF16), b.astype(BF16), preferred_element_type=F32)


def _dot_nt(a, b):
    return lax.dot_general(a.astype(BF16), b.astype(BF16), (((1,), (1,)), ((), ())),
                           preferred_element_type=F32)


def _dot_tn(a, b):
    return lax.dot_general(a.astype(BF16), b.astype(BF16), (((0,), (0,)), ((), ())),
                           preferred_element_type=F32)


def _split2_dot(a, b):
    hi = a.astype(BF16)
    lo = (a - hi.astype(F32)).astype(BF16)
    return jnp.dot(hi, b, preferred_element_type=F32) + jnp.dot(lo, b, preferred_element_type=F32)


def _split3_lhs_dot(a, b):
    hi = a.astype(BF16)
    r1 = a - hi.astype(F32)
    mid = r1.astype(BF16)
    lo = (r1 - mid.astype(F32)).astype(BF16)
    return ((jnp.dot(hi, b, preferred_element_type=F32) + jnp.dot(mid, b, preferred_element_type=F32))
            + jnp.dot(lo, b, preferred_element_type=F32))


def _iota(shape, axis):
    return lax.broadcasted_iota(jnp.int32, shape, axis)


def _block_mask(rows, cols, rb, cb):
    return (_iota((rows, cols), 0) // rb) == (_iota((rows, cols), 1) // cb)


def _shift_rows(x, k):
    t = _iota(x.shape, 0)
    return jnp.where(t >= k, pltpu.roll(x, k, 0), 0.0)


def _cumsum_rows(x):
    k = 1
    while k < x.shape[0]:
        x = x + _shift_rows(x, k)
        k *= 2
    return x


def _softplus(x):
    return jnp.maximum(x, 0.0) + jnp.log1p(jnp.exp(-jnp.abs(x)))


def _log_sigmoid(x):
    return -_softplus(-x)


def _silu(x):
    return x * jax.nn.sigmoid(x)


def _masked_softmax(s, mask):
    s = jnp.where(mask, s, -jnp.inf)
    m = jnp.maximum(jnp.max(s, axis=-1, keepdims=True), F32_LOWEST)
    e = jnp.exp(s - m)
    den = jnp.sum(e, axis=-1, keepdims=True)
    return e / jnp.where(den > 0, den, 1.0)


def _tile4(x):
    return jnp.concatenate([x, x, x, x], axis=0)


POOL_HALO = 16


def _gla_masks():
    return (_block_mask(4 * CHUNK, 128, CHUNK, GLA_DK),
            _block_mask(4 * CHUNK, 256, CHUNK, GLA_DV),
            _block_mask(256, 128, GLA_DV, GLA_DK),
            (_iota((CHUNK, 256), 1) % CHUNK) <= _iota((CHUNK, 256), 0))


def _seqs(f, *lists):
    return [f(*args) for args in zip(*lists)]


def _run_staggered(stage_gens, starts, periods):
    live = list(range(len(stage_gens)))
    tick = 0
    while live:
        for i in list(live):
            if tick >= starts[i] and (tick - starts[i]) % periods[i] == 0:
                try:
                    next(stage_gens[i])
                except StopIteration:
                    live.remove(i)
        tick += 1


def _gla_chunk(masks, zqk, v, r, zm, wlr_ref, blr_ref, gn_ref, ones_ref, st_refs, out):
    mask_k, mask_v, mask_st, causal = masks
    q = [z[:, 0:128] * GLA_DK ** -0.5 for z in zqk]
    k = [z[:, 128:256] for z in zqk]
    pre = _seqs(lambda a: _dot(a, wlr_ref[...]), zm)
    yield
    bc = _seqs(lambda a: _cumsum_rows(_log_sigmoid(a + blr_ref[...]) / GLA_GATE_NORM), pre)
    bl = [a[CHUNK - 1:CHUNK, :] for a in bc]
    q_e = _seqs(lambda a, c: a * jnp.exp(c), q, bc)
    k_e = _seqs(lambda a, c: a * jnp.exp(-c), k, bc)
    k_u = _seqs(lambda a, c, l: a * jnp.exp(l - c), k, bc, bl)
    st = [ref[...] for ref in st_refs]
    yield
    att = _seqs(lambda a, b: jnp.where(causal, _dot_nt(a, jnp.where(mask_k, _tile4(b), 0.0)), 0.0), q_e, k_e)
    inter = _seqs(_dot_nt, q_e, st)
    kv = _seqs(_dot_tn, v, k_u)
    yield
    o = _seqs(lambda a, b, c: _dot(a, _tile4(b.astype(BF16)) * ones_ref[...]) + c, att, v, inter)
    for ref, s_old, l, upd in zip(st_refs, st, bl, kv):
        ref[...] = s_old * jnp.exp(l) + jnp.where(mask_st, upd, 0.0)
    yield
    ms = _seqs(lambda a: _dot(a * a, ones_ref[...]) * (1.0 / GLA_DV), o)
    yield
    out[:] = _seqs(lambda a, m, g: a * lax.rsqrt(m + EPS) * gn_ref[...] * _silu(g), o, ms, r)


GDN_HALO = SUBLANE


def _gdn_masks():
    c4 = 4 * CHUNK
    col = _iota((CHUNK, c4), 1) % CHUNK
    row = _iota((CHUNK, c4), 0)
    return (_block_mask(c4, c4, CHUNK, CHUNK), col <= row, col < row, col == row)


def _gdn_chunk(masks, halo, zqkv, gate, zm, cw_ref, eba_ref, alog_ref, dtb_ref, gn_ref, ones_ref, st_refs, out):
    mask_bd, incl, strict, diag = masks
    cw = cw_ref[...]
    bd = lambda a: _tile4(a.astype(BF16)) * ones_ref[...]

    def conv_silu(h, cur):
        ext = jnp.concatenate([h, cur], axis=0)
        conv = (cw[3:4] * cur + cw[2:3] * pltpu.roll(ext, 1, 0)[GDN_HALO:]
                + cw[1:2] * pltpu.roll(ext, 2, 0)[GDN_HALO:] + cw[0:1] * pltpu.roll(ext, 3, 0)[GDN_HALO:])
        return _silu(conv)

    qkv = _seqs(conv_silu, halo, zqkv)
    yield
    v = [a[:, 512:768] for a in qkv]
    ssq = _seqs(lambda a: _dot(jnp.concatenate([a[:, 0:256] * a[:, 0:256], a[:, 256:512] * a[:, 256:512]],
                                               axis=0), ones_ref[...]), qkv)
    yield
    q = _seqs(lambda a, s: a[:, 0:256] * lax.rsqrt(s[0:CHUNK] + EPS) * GDN_DH ** -0.5, qkv, ssq)
    k = _seqs(lambda a, s: a[:, 256:512] * lax.rsqrt(s[CHUNK:2 * CHUNK] + EPS), qkv, ssq)
    beta = _seqs(lambda a: _dot(jax.nn.sigmoid(a), eba_ref[:, 0:MIX_W]), zm)
    yield
    gc = _seqs(lambda a: _split3_lhs_dot(_cumsum_rows(-jnp.exp(alog_ref[...]) * _softplus(a + dtb_ref[...])),
                                         eba_ref[:, MIX_W:2 * MIX_W]), zm)
    yield
    gl = [a[CHUNK - 1:CHUNK, :] for a in gc]
    g_row = _seqs(lambda a: jnp.sum(jnp.where(diag, a, 0.0), axis=0, keepdims=True), gc)
    yield
    decay =_seqs(lambda a, b: jnp.exp(jnp.where(incl, a - b, -jnp.inf)), gc, g_row)
    kb = _seqs(lambda a, b: a * b, k, beta)
    vb = _seqs(lambda a, b: a * b, v, beta)
    kq = _seqs(lambda a, b, c: _dot_nt(jnp.concatenate([a, b], axis=0), bd(c)), kb, q, k)
    yield
    n_mat =_seqs(lambda s, d: jnp.where(strict, s[0:CHUNK] * d, 0.0), kq, decay)
    a_qk = _seqs(lambda s, d: s[CHUNK:2 * CHUNK] * d, kq, decay)
    m = _seqs(lambda a: -a, n_mat)
    x = _seqs(lambda a: jnp.where(diag, 1.0, 0.0) + a, m)
    m = _seqs(lambda a: _dot(a, bd(a)), m)
    yield
    p = 2
    while p < CHUNK // 2:
        xm = _seqs(lambda a, b: _dot(jnp.concatenate([a, b], axis=0), bd(b)), x, m)
        x = _seqs(lambda a, r: a + r[0:CHUNK], x, xm)
        m = [r[CHUNK:2 * CHUNK] for r in xm]
        p *= 2
        yield
    x = _seqs(lambda a, b: a + _dot(a, bd(b)), x, m)
    yield
    uw = _seqs(lambda a, b, c, g: _dot(a, jnp.concatenate([bd(b), bd(c * jnp.exp(g))], axis=1)),
               x, vb, kb, gc)
    yield
    st = [ref[...] for ref in st_refs]
    ws = _seqs(lambda a, b, g, s: _dot(jnp.concatenate([a[:, 256:512], b * jnp.exp(g)], axis=0), s),
               uw, q, gc, st)
    yield
    v_new = _seqs(lambda a, r: a[:, 0:256] - r[0:CHUNK], uw, ws)
    o = _seqs(lambda r, qk, vn: r[CHUNK:2 * CHUNK] + _dot(qk, bd(vn)), ws, a_qk, v_new)
    upd = _seqs(lambda a, g, l, vn: _dot_tn(a * jnp.exp(l - g), vn), k, gc, gl, v_new)
    yield
    for ref, s_old, l, u in zip(st_refs, st, gl, upd):
        ref[...] = s_old * jnp.exp(l) + jnp.where(mask_bd, u, 0.0)
    ms = _seqs(lambda a: _dot(a * a, ones_ref[...]) * (1.0 / GDN_DH), o)
    yield
    out[:] = _seqs(lambda a, m_, g: a * lax.rsqrt(m_ + EPS) * gn_ref[...] * _silu(g), o, ms, gate)


def _pool_diff(u, tail, pos0):
    ext = jnp.concatenate([tail, u], axis=0)
    s2 = ext + pltpu.roll(ext, 1, 0)
    s4 = s2 + pltpu.roll(s2, 2, 0)
    s8 = s4 + pltpu.roll(s4, 4, 0)
    s16 = s8 + pltpu.roll(s8, 8, 0)
    grp = _iota(u.shape, 1) // POOL_GW
    win = jnp.where(grp == 0, s2[POOL_HALO:], jnp.where(grp == 1, s4[POOL_HALO:],
                                                        jnp.where(grp == 2, s8[POOL_HALO:], s16[POOL_HALO:])))
    width = jnp.where(grp == 0, POOL_WINDOWS[0],
                      jnp.where(grp == 1, POOL_WINDOWS[1],
                                jnp.where(grp == 2, POOL_WINDOWS[2], POOL_WINDOWS[3])))
    cnt = jnp.minimum(pos0 + _iota(u.shape, 0) + 1, width).astype(F32)
    return win / cnt - u


def _mix_kernel(x_ref, g_ref, w_ref, pw_ref, ps_ref, wlr_ref, blr_ref, gng_ref, cw_ref, eba_ref, alog_ref, dtb_ref,
                gnd_ref, ones_ref, op_ref, zn_ref, zc_ref, zm_ref, og_ref, od_ref, ptail_ref, halo_ref, *st_refs):
    nb = x_ref.shape[0]
    chunk = pl.program_id(1)

    @pl.when(chunk == 0)
    def _():
        ptail_ref[...] = jnp.zeros_like(ptail_ref)
        halo_ref[...] = jnp.zeros_like(halo_ref)
        for st_ref in st_refs:
            st_ref[...] = jnp.zeros_like(st_ref)

    h = _rms(x_ref[...].reshape(nb * CHUNK, D_MODEL), g_ref[...]).astype(BF16)
    proj = lambda lo, n: jnp.dot(h, w_ref[:, lo:lo + n], preferred_element_type=F32)
    seqs = lambda a: [a[i * CHUNK:(i + 1) * CHUNK] for i in range(nb)]
    off_g, off_d = ZP_W, ZP_W + ZG_W
    off_n = off_d + ZD_W
    off_c = off_n + ZN_W
    off_m = off_c + ZC_W
    zqkv = seqs(proj(off_d, GDN_QKV_W))
    zm = seqs(proj(off_m, ZM_W))
    halo = [halo_ref[i] for i in range(nb)]
    for i in range(nb):
        halo_ref[i] = zqkv[i][CHUNK - GDN_HALO:, :]
        zm_ref[i] = zm[i]
    halves = [slice(0, nb // 2), slice(nb // 2, nb)] if nb > 1 else [slice(0, nb)]
    gla_qk, gla_v, gla_r = [], [], []
    gate_parts = [[] for _ in halves]

    def other_projections():
        gla_qk.extend(seqs(proj(off_g, 256)))
        yield
        gla_v.extend(seqs(proj(off_g + 256, 256)))
        yield
        u = seqs(proj(0, ZP_W))
        yield
        diff = [_pool_diff(u[i], ptail_ref[i], chunk * CHUNK) for i in range(nb)]
        for i in range(nb):
            ptail_ref[i] = u[i][CHUNK - POOL_HALO:, :]
        yield
        for lo in range(0, ZN_W, 256):
            zn = seqs(proj(off_n + lo, 256))
            for i in range(nb):
                zn_ref[i, :, lo:lo + 256] = zn[i].astype(zn_ref.dtype)
            yield
        zc = seqs(proj(off_c, ZC_W))
        for i in range(nb):
            zc_ref[i] = zc[i]
        yield
        gla_r.extend(seqs(proj(off_g + 512, 256)))
        yield
        gate = seqs(proj(off_d + GDN_QKV_W, MIX_W))
        for hs, part in zip(halves, gate_parts):
            part.extend(gate[hs])
        yield
        o_pool = seqs(_dot(jnp.concatenate(diff, axis=0), pw_ref[...]) * ps_ref[...])
        for i in range(nb):
            op_ref[i] = o_pool[i].astype(op_ref.dtype)

    og = [None] * nb
    od_parts = [[None] * (hs.stop - hs.start) for hs in halves]
    gdn_masks = _gdn_masks()
    gens = [_gdn_chunk(gdn_masks, halo[hs], zqkv[hs], gate_part, zm[hs], cw_ref, eba_ref, alog_ref, dtb_ref,
                       gnd_ref, ones_ref, st_refs[nb:][hs], part)
            for hs, part, gate_part in zip(halves, od_parts, gate_parts)]
    gens.append(_gla_chunk(_gla_masks(), gla_qk, gla_v, gla_r, zm, wlr_ref, blr_ref, gng_ref, ones_ref,
                           st_refs[:nb], og))
    starts = [REC_LAG * i for i in range(len(halves))] + [GLA_START]
    gens.append(other_projections())
    _run_staggered(gens, starts + [0], [1] * len(gens))
    od = [o for part in od_parts for o in part]
    for i in range(nb):
        og_ref[i] = og[i].astype(og_ref.dtype)
        od_ref[i] = od[i].astype(od_ref.dtype)


def _mixers(x3, g, w, pool_w_bd, pool_scale, wlr2, blr, gn_gla, cw, eba, alog, dtb, gn_gdn, ones_bd):
    b, s, _ = x3.shape
    nb = REC_SEQS if b % REC_SEQS == 0 else 1
    const = lambda shape: pl.BlockSpec(shape, lambda i, c: (0,) * len(shape))
    tok = lambda n: pl.BlockSpec((nb, CHUNK, n), lambda i, c: (i, c, 0))
    outs = ((MIX_W, BF16), (ZN_W, BF16), (ZC_W, F32), (ZM_W, F32), (MIX_W, BF16), (MIX_W, BF16))
    return pl.pallas_call(
        _mix_kernel,
        grid=(b // nb, s // CHUNK),
        in_specs=[tok(D_MODEL), const((1, D_MODEL)), const((D_MODEL, Z_W)), const((MIX_W, MIX_W)), const((1, MIX_W)),
                  const((ZM_W, 128)), const((1, 128)), const((1, MIX_W)),
                  const((GDN_CONV, GDN_QKV_W)), const((ZM_W, 2 * MIX_W)), const((1, ZM_W)), const((1, ZM_W)),
                  const((1, MIX_W)), const((MIX_W, MIX_W))],
        out_specs=[tok(n) for n, _ in outs],
        out_shape=[jax.ShapeDtypeStruct((b, s, n), dt) for n, dt in outs],
        scratch_shapes=([pltpu.VMEM((nb, POOL_HALO, MIX_W), F32), pltpu.VMEM((nb, GDN_HALO, GDN_QKV_W), F32)]
                        + [pltpu.VMEM((256, 128), F32)] * nb + [pltpu.VMEM((256, 256), F32)] * nb),
        compiler_params=_cparams(("parallel", "arbitrary")),
        name="inproj_mixers",
    )(x3, g, w, pool_w_bd, pool_scale, wlr2, blr, gn_gla, cw, eba, alog, dtb, gn_gdn, ones_bd)


def _cmp_kernel(kc_ref, vc_ref, pe_ref, w1_ref, w2_ref, ck_ref, cv_ref):
    n_sub = NSA_CMP_LEN // NSA_CMP_STRIDE
    n_chunks = kc_ref.shape[1] // NSA_CMP_STRIDE
    for which, (src_ref, out_ref) in enumerate(((kc_ref, ck_ref), (vc_ref, cv_ref))):
        parts = [jnp.zeros((n_chunks, LANE), F32) for _ in range(n_sub)]
        for i in range(NSA_CMP_STRIDE):
            slab = src_ref[0, pl.ds(i, n_chunks, stride=NSA_CMP_STRIDE), :]
            for sub in range(n_sub):
                p = sub * NSA_CMP_STRIDE + i
                parts[sub] = parts[sub] + _dot(slab + pe_ref[which, p:p + 1, :], w1_ref[which, p])
        pre = parts[0] + pltpu.roll(parts[1], n_chunks - 1, 0)
        out_ref[0] = _dot(jax.nn.gelu(pre), w2_ref[which])


def _nsa_compress(zc, pe_x, w1_bd, w2_bd):
    b, s, _ = zc.shape
    n_chunks = s // NSA_CMP_STRIDE
    return pl.pallas_call(
        _cmp_kernel,
        grid=(b,),
        in_specs=[pl.BlockSpec((1, s, LANE), lambda i: (i, 0, 0)),
                  pl.BlockSpec((1, s, LANE), lambda i: (i, 0, 1)),
                  pl.BlockSpec((2, NSA_CMP_LEN, LANE), lambda i: (0, 0, 0)),
                  pl.BlockSpec((2, NSA_CMP_LEN, LANE, LANE), lambda i: (0, 0, 0, 0)),
                  pl.BlockSpec((2, LANE, LANE), lambda i: (0, 0, 0))],
        out_specs=[pl.BlockSpec((1, n_chunks, LANE), lambda i: (i, 0, 0))] * 2,
        out_shape=[jax.ShapeDtypeStruct((b, n_chunks, LANE), F32)] * 2,
        compiler_params=_cparams(("parallel",)),
        name="nsa_compress",
    )(zc, zc, pe_x, w1_bd, w2_bd)


def _topk_rank(v):
    n, t = v.shape
    blocks = [v[r:r + SUBLANE] for r in range(0, n, SUBLANE)]
    sub = _iota((SUBLANE, t), 0)
    rank = [jnp.zeros((SUBLANE, t), jnp.int32) for _ in blocks]
    for m in range(n):
        other = v[m:m + 1, :]
        mb, ms = divmod(m, SUBLANE)
        for r, blk in enumerate(blocks):
            if r < mb:
                ahead = (other > blk).astype(jnp.int32)
            elif r > mb:
                ahead = (other >= blk).astype(jnp.int32)
            else:
                ahead = jnp.where(sub > ms, (other >= blk).astype(jnp.int32), (other > blk).astype(jnp.int32))
            rank[r] = rank[r] + ahead
    return jnp.concatenate(rank, axis=0)


def _exp_weights(s, m):
    return jnp.exp((s - m).astype(BF16))


def _normalise_aug(o_aug, in_grp):
    den = pltpu.roll(o_aug, NSA_DH, 1)
    return jnp.where(in_grp, o_aug / jnp.where(den > 0, den, 1.0), 0.0)


def _nsa_kernel(q_ref, ksv_ref, kwv_ref, ck_ref, cv_ref, zm_ref, covt_ref, gexp_ref, o_ref,
                kaug_ref, vaug_ref, vwaug_ref, s_ref, mx_ref, acc_ref):
    nb = q_ref.shape[0]
    tq = q_ref.shape[1]
    s_len = ksv_ref.shape[1]
    n_slc = s_len // NSA_SEL_LEN
    qi = pl.program_id(1)
    s0 = pl.multiple_of(qi * tq, tq)
    lane_grp = _iota((1, LANE), 1) // NSA_DH
    chains = [(bi, g) for bi in range(nb) for g in range(NSA_GROUPS)]
    n_chains = range(len(chains))

    @pl.when(qi == 0)
    def _():
        lane = _iota((s_len, LANE), 1)
        key_blk = _iota((s_len, LANE), 0) // NSA_SEL_LEN
        for bi in range(nb):
            k = ksv_ref[bi, :, 0:LANE].astype(F32)
            v = ksv_ref[bi, :, LANE:2 * LANE].astype(F32)
            vw = kwv_ref[bi, :, LANE:2 * LANE].astype(F32)
            for g in range(NSA_GROUPS):
                own = lane // NSA_DH == g
                onehot = jnp.where(lane - (1 - g) * NSA_DH == key_blk, 1.0, 0.0)
                kaug_ref[bi * NSA_GROUPS + g] = jnp.where(own, k, onehot).astype(BF16)
                vaug_ref[bi * NSA_GROUPS + g] = jnp.where(own, v, 1.0).astype(BF16)
                vwaug_ref[bi * NSA_GROUPS + g] = jnp.where(own, vw, 1.0).astype(BF16)

    t_col = s0 + _iota((tq, 1), 0)
    t_col2 = jnp.concatenate([t_col, t_col], axis=0)
    t_row = s0 + _iota((1, tq), 1)
    q = [q_ref[bi].astype(F32) * NSA_DH ** -0.5 for bi in range(nb)]
    ck_hi = [ck_ref[bi].astype(BF16) for bi in range(nb)]
    ck_lo = [(ck_ref[bi] - ck_hi[bi].astype(F32)).astype(BF16) for bi in range(nb)]
    cv = [cv_ref[bi] for bi in range(nb)]
    cmp_end = _iota((1, N_CMP_PAD), 1) * NSA_CMP_STRIDE + (NSA_CMP_LEN - 1)
    cmp_valid = cmp_end <= t_col2
    blk_t = _iota((n_slc, tq), 0)
    cur_t = t_row // NSA_SEL_LEN
    forced_t = (blk_t == 0) | (blk_t == cur_t) | (blk_t == cur_t - 1)
    future_t = blk_t > cur_t
    row_in_tile = jnp.concatenate([_iota((tq, tq), 0)] * NSA_HPG, axis=0)
    col_in_tile = _iota((NSA_HPG * tq, tq), 1)
    diag_ok = col_in_tile <= row_in_tile
    n_wt = NSA_WINDOW // tq + 1
    never = 2 * tq
    win_off = [pl.multiple_of(jnp.maximum(qi - (n_wt - 1 - w), 0) * tq, tq) for w in range(n_wt)]
    win_ok = [col_in_tile > row_in_tile + jnp.where(qi >= n_wt - 1, 0, never)]
    win_ok += [col_in_tile >= jnp.where(qi >= n_wt - 1 - w, 0, never) for w in range(1, n_wt - 1)]
    win_ok += [diag_ok]

    o_cmp = [[jnp.zeros((tq, LANE), F32) for _ in range(NSA_HPG)] for _ in range(nb)]
    o_slc = [[jnp.zeros((tq, LANE), F32) for _ in range(NSA_HPG)] for _ in range(nb)]
    o_win = [[jnp.zeros((tq, LANE), F32) for _ in range(NSA_HPG)] for _ in range(nb)]
    in_grp = [lane_grp == g for _, g in chains]
    q2 = [jnp.concatenate([jnp.where(in_grp[c], q[bi][:, j * LANE:(j + 1) * LANE], 0.0)
                           for j in range(NSA_HPG)], axis=0) for c, (bi, _) in enumerate(chains)]
    qb = [a.astype(BF16) for a in q2]
    p_c = [_masked_softmax(_dot_nt(qb[c], ck_hi[bi]) + _dot_nt(qb[c], ck_lo[bi]), cmp_valid)
           for c, (bi, _) in enumerate(chains)]
    oc = [jnp.where(in_grp[c], _dot(p_c[c], cv[bi]), 0.0) for c, (bi, _) in enumerate(chains)]
    imp_c = [a[0:tq] + a[tq:2 * tq] for a in p_c]
    imp_hi = [a.astype(BF16) for a in imp_c]
    imp_lo = [(a - h.astype(F32)).astype(BF16) for a, h in zip(imp_c, imp_hi)]
    imp_t = [(_dot_nt(covt_ref[...], h) + _dot_nt(covt_ref[...], l_))[0:n_slc]
             for h, l_ in zip(imp_hi, imp_lo)]
    imp_t = [jnp.where(forced_t, jnp.inf, jnp.where(future_t, -jnp.inf, a)) for a in imp_t]
    s_w = [[jnp.where(ok, _dot_nt(qb[c], kwv_ref[bi, pl.ds(off, tq), 0:LANE]), -jnp.inf)
            for ok, off in zip(win_ok, win_off)] for c, (bi, _) in enumerate(chains)]
    qa = []
    for c, (_, g) in enumerate(chains):
        bias_t = jnp.where(_topk_rank(imp_t[c]) < NSA_N_SEL, 0.0, NEG_BIG)
        lo = (1 - g) * NSA_DH
        rows = ([jnp.zeros((lo, tq), F32)] if lo else []) + [bias_t, jnp.zeros((LANE - lo - n_slc, tq), F32)]
        bias = jnp.concatenate(rows, axis=0).T
        qa.append(jnp.where(in_grp[c], q2[c], jnp.concatenate([bias] * NSA_HPG, axis=0)).astype(BF16))
    m_w = []
    for c in n_chains:
        m_c = s_w[c][0]
        for s_t in s_w[c][1:]:
            m_c = jnp.maximum(m_c, s_t)
        m_w.append(jnp.maximum(jnp.max(m_c, axis=-1, keepdims=True), F32_LOWEST))
    ow = [jnp.zeros((NSA_HPG * tq, LANE), F32) for _ in n_chains]
    for w, off in enumerate(win_off):
        for c in n_chains:
            ow[c] = ow[c] + jnp.dot(_exp_weights(s_w[c][w], m_w[c]), vwaug_ref[c, pl.ds(off, tq), :],
                                    preferred_element_type=F32)
    for c, (bi, _) in enumerate(chains):
        ow_c = _normalise_aug(ow[c], in_grp[c])
        for j in range(NSA_HPG):
            rows_j = slice(j * tq, (j + 1) * tq)
            o_cmp[bi][j] = o_cmp[bi][j] + oc[c][rows_j]
            o_win[bi][j] = o_win[bi][j] + ow_c[rows_j]

    half_max = lambda s_t: jnp.maximum(s_t[:, 0:LANE], s_t[:, LANE:2 * LANE])
    mx_ref[...] = jnp.full(mx_ref.shape, NEG_BIG, F32)
    groups = n_chains

    def score_tiles(tiles):
        offs = [pl.multiple_of(kt * tq, tq) for kt in tiles]
        s_new = [[_dot_nt(qa[g], kaug_ref[g, pl.ds(off, tq), :]) for g in groups] for off in offs]
        for kt, s_kt in zip(tiles, s_new):
            for g in groups:
                s_ref[g, kt] = s_kt[g]
        for g in groups:
            m_new = half_max(s_new[0][g])
            for s_kt in s_new[1:]:
                m_new = jnp.maximum(m_new, half_max(s_kt[g]))
            mx_ref[g] = jnp.maximum(mx_ref[g], m_new)

    def run_tiles(fn, n_tiles):
        def body(p, carry):
            fn([NSA_TILE_UNROLL * p + k for k in range(NSA_TILE_UNROLL)])
            return carry

        lax.fori_loop(0, n_tiles // NSA_TILE_UNROLL, body, 0)
        left = n_tiles % NSA_TILE_UNROLL
        for r in range(1, NSA_TILE_UNROLL):
            @pl.when(left == r)
            def _():
                fn([n_tiles - r + k for k in range(r)])

    run_tiles(score_tiles, qi)

    m_s = []
    for g in groups:
        s_d = jnp.where(diag_ok, _dot_nt(qa[g], kaug_ref[g, pl.ds(s0, tq), :]), NEG_BIG)
        s_ref[g, qi] = s_d
        m_g = jnp.max(jnp.maximum(mx_ref[g], half_max(s_d)), axis=-1, keepdims=True)
        m_s.append(jnp.maximum(m_g, F32_LOWEST))
    acc_ref[...] = jnp.zeros(acc_ref.shape, F32)

    def attend_tiles(tiles):
        offs = [pl.multiple_of(kt * tq, tq) for kt in tiles]
        pv = [[jnp.dot(_exp_weights(s_ref[g, kt], m_s[g]), vaug_ref[g, pl.ds(off, tq), :],
                       preferred_element_type=F32) for g in groups] for kt, off in zip(tiles, offs)]
        for g in groups:
            upd = pv[0][g]
            for pv_kt in pv[1:]:
                upd = upd + pv_kt[g]
            acc_ref[g] += upd

    run_tiles(attend_tiles, qi + 1)

    for c, (bi, _) in enumerate(chains):
        os_ = _normalise_aug(acc_ref[c], in_grp[c])
        for j in range(NSA_HPG):
            o_slc[bi][j] = o_slc[bi][j] + os_[j * tq:(j + 1) * tq]

    for bi in range(nb):
        gates = _split2_dot(jax.nn.sigmoid(zm_ref[bi]), gexp_ref[...])
        o = (gates[:, 0:MIX_W] * jnp.concatenate(o_cmp[bi], axis=1)
             + gates[:, MIX_W:2 * MIX_W] * jnp.concatenate(o_slc[bi], axis=1)
             + gates[:, 2 * MIX_W:3 * MIX_W] * jnp.concatenate(o_win[bi], axis=1))
        o_ref[bi] = o.astype(o_ref.dtype)


def _nsa(zn, zm, ck, cv, covt, gexp):
    b, s, _ = zn.shape
    tq = TQ_NSA
    nb = NSA_SEQS if b % NSA_SEQS == 0 else 1
    n_chains = nb * NSA_GROUPS
    return pl.pallas_call(
        _nsa_kernel,
        grid=(b // nb, s // tq),
        in_specs=[pl.BlockSpec((nb, tq, 256), lambda i, j: (i, j, 0)),
                  pl.BlockSpec((nb, s, 256), lambda i, j: (i, 0, 1)),
                  pl.BlockSpec((nb, s, 256), lambda i, j: (i, 0, 2)),
                  pl.BlockSpec((nb, N_CMP_PAD, LANE), lambda i, j: (i, 0, 0)),
                  pl.BlockSpec((nb, N_CMP_PAD, LANE), lambda i, j: (i, 0, 0)),
                  pl.BlockSpec((nb, tq, ZM_W), lambda i, j: (i, j, 0)),
                  pl.BlockSpec((N_SLC_PAD, N_CMP_PAD), lambda i, j: (0, 0)),
                  pl.BlockSpec((ZM_W, 3 * MIX_W), lambda i, j: (0, 0))],
        out_specs=pl.BlockSpec((nb, tq, MIX_W), lambda i, j: (i, j, 0)),
        out_shape=jax.ShapeDtypeStruct((b, s, MIX_W), BF16),
        scratch_shapes=[pltpu.VMEM((n_chains, s, LANE), BF16)] * 3
                       + [pltpu.VMEM((n_chains, s // tq, NSA_HPG * tq, tq), F32),
                          pltpu.VMEM((n_chains, NSA_HPG * tq, LANE), F32),
                          pltpu.VMEM((n_chains, NSA_HPG * tq, LANE), F32)],
        compiler_params=_cparams(("parallel", "arbitrary")),
        name="nsa_attn",
    )(zn, zn, zn, ck, cv, zm, covt, gexp)


def _combine_kernel(x_ref, g_ref, op_ref, oa_ref, od_ref, on_ref, wg_ref, bg_ref, wb_ref, wo_ref, o_ref):
    x = x_ref[...]
    h = _rms(x, g_ref[...]).astype(BF16)
    y = jnp.zeros(x.shape, F32)
    for i, br_ref in enumerate((op_ref, oa_ref, od_ref, on_ref)):
        gate = jax.nn.sigmoid(jnp.dot(h, wg_ref[i], preferred_element_type=F32) + bg_ref[i])
        y = y + gate * jnp.dot(br_ref[...], wb_ref[i], preferred_element_type=F32)
    o_ref[...] = x + _dot(y, wo_ref[...])


def _combine(x2, g, branches, wg, bg, wb, wo):
    t = x2.shape[0]
    tm = TM_COMB
    return pl.pallas_call(
        _combine_kernel,
        grid=(t // tm,),
        in_specs=[pl.BlockSpec((tm, D_MODEL), lambda i: (i, 0)),
                  pl.BlockSpec((1, D_MODEL), lambda i: (0, 0))]
                 + [pl.BlockSpec((tm, MIX_W), lambda i: (i, 0))] * N_BRANCH
                 + [pl.BlockSpec((N_BRANCH, D_MODEL, D_MODEL), lambda i: (0, 0, 0)),
                    pl.BlockSpec((N_BRANCH, 1, D_MODEL), lambda i: (0, 0, 0)),
                    pl.BlockSpec((N_BRANCH, MIX_W, D_MODEL), lambda i: (0, 0, 0)),
                    pl.BlockSpec((D_MODEL, D_MODEL), lambda i: (0, 0))],
        out_specs=pl.BlockSpec((tm, D_MODEL), lambda i: (i, 0)),
        out_shape=jax.ShapeDtypeStruct((t, D_MODEL), F32),
        compiler_params=_cparams(("parallel",)),
        name="combine",
    )(x2, g, *branches, wg, bg, wb, wo)


def _cross_kernel(x_ref, g_ref, wq_ref, mem_ref, gm_ref, wkv_ref, wo_ref, o_ref, kv_ref):
    @pl.when(pl.program_id(1) == 0)
    def _():
        kv_ref[...] = _dot(_rms(mem_ref[0], gm_ref[...]), wkv_ref[...]).astype(kv_ref.dtype)

    n_k = X_HEADS * X_DH
    subs = [slice(i * CROSS_SUB, (i + 1) * CROSS_SUB) for i in range(x_ref.shape[1] // CROSS_SUB)]
    x = [x_ref[0, r, :] for r in subs]
    q = [_dot(_rms(a, g_ref[...]), wq_ref[...]) * X_DH ** -0.5 for a in x]
    pairs = [(i, h) for h in range(X_HEADS) for i in range(len(subs))]
    head = lambda h: slice(h * X_DH, (h + 1) * X_DH)
    sc = [_dot_nt(q[i][:, head(h)], kv_ref[:, head(h)]) for i, h in pairs]
    e = [jnp.exp(s - jnp.max(s, axis=-1, keepdims=True)) for s in sc]
    o = [_dot(e_ih, kv_ref[:, n_k + h * X_DH:n_k + (h + 1) * X_DH]) / jnp.sum(e_ih, axis=-1, keepdims=True)
         for e_ih, (i, h) in zip(e, pairs)]
    for i, r in enumerate(subs):
        o_i = jnp.concatenate([o[pairs.index((i, h))] for h in range(X_HEADS)], axis=1)
        o_ref[0, r, :] = x[i] + _dot(o_i, wo_ref[...])


def _cross(x3, g, wq, mem, g_mem, wkv, wo):
    b, s, _ = x3.shape
    tm = TM_CROSS
    m = mem.shape[1]
    n_k = X_HEADS * X_DH
    return pl.pallas_call(
        _cross_kernel,
        grid=(b, s // tm),
        in_specs=[pl.BlockSpec((1, tm, D_MODEL), lambda i, j: (i, j, 0)),
                  pl.BlockSpec((1, D_MODEL), lambda i, j: (0, 0)),
                  pl.BlockSpec((D_MODEL, n_k), lambda i, j: (0, 0)),
                  pl.BlockSpec((1, m, D_MODEL), lambda i, j: (i, 0, 0)),
                  pl.BlockSpec((1, D_MODEL), lambda i, j: (0, 0)),
                  pl.BlockSpec((D_MODEL, 2 * n_k), lambda i, j: (0, 0)),
                  pl.BlockSpec((n_k, D_MODEL), lambda i, j: (0, 0))],
        out_specs=pl.BlockSpec((1, tm, D_MODEL), lambda i, j: (i, j, 0)),
        out_shape=jax.ShapeDtypeStruct((b, s, D_MODEL), F32),
        scratch_shapes=[pltpu.VMEM((m, 2 * n_k), BF16)],
        compiler_params=_cparams(("parallel", "arbitrary")),
        name="cross_attn",
    )(x3, g, wq, mem, g_mem, wkv, wo)


def _ffn_kernel(x_ref, g_ref, wup_ref, cw_ref, cb_ref, wd_ref, gf_ref, o_ref, tail_ref, act_ref, *, final):
    @pl.when(pl.program_id(1) == 0)
    def _():
        tail_ref[...] = jnp.zeros_like(tail_ref)

    ts = x_ref.shape[1]
    x = x_ref[0]
    hn = _rms(x, g_ref[...]).astype(BF16)
    for c in range(D_FF // FF_CHUNK):
        cols = slice(c * FF_CHUNK, (c + 1) * FF_CHUNK)
        gcols = slice(D_FF + c * FF_CHUNK, D_FF + (c + 1) * FF_CHUNK)
        u = jnp.dot(hn, wup_ref[:, cols], preferred_element_type=F32)
        v = jnp.dot(hn, wup_ref[:, gcols], preferred_element_type=F32)
        ext = jnp.concatenate([tail_ref[:, cols], u], axis=0)
        tail_ref[:, cols] = u[ts - SUBLANE:, :]
        cw = cw_ref[:, cols]
        y = (cw[2:3] * u + cw[1:2] * pltpu.roll(ext, 1, 0)[SUBLANE:]
             + cw[0:1] * pltpu.roll(ext, 2, 0)[SUBLANE:] + cb_ref[:, cols])
        act_ref[:, cols] = (jax.nn.gelu(y) * v).astype(BF16)
    out = x + jnp.dot(act_ref[...], wd_ref[...], preferred_element_type=F32)
    if final:
        out = _rms(out, gf_ref[...])
    o_ref[0] = out


def _ffn(x3, g, wup, cw, cb, wd, gf, final):
    b, s, _ = x3.shape
    ts = TS_FFN
    return pl.pallas_call(
        functools.partial(_ffn_kernel, final=final),
        grid=(b, s // ts),
        in_specs=[pl.BlockSpec((1, ts, D_MODEL), lambda i, j: (i, j, 0)),
                  pl.BlockSpec((1, D_MODEL), lambda i, j: (0, 0)),
                  pl.BlockSpec((D_MODEL, 2 * D_FF), lambda i, j: (0, 0), pipeline_mode=pl.Buffered(1)),
                  pl.BlockSpec((FFN_CONV, D_FF), lambda i, j: (0, 0)),
                  pl.BlockSpec((1, D_FF), lambda i, j: (0, 0)),
                  pl.BlockSpec((D_FF, D_MODEL), lambda i, j: (0, 0), pipeline_mode=pl.Buffered(1)),
                  pl.BlockSpec((1, D_MODEL), lambda i, j: (0, 0))],
        out_specs=pl.BlockSpec((1, ts, D_MODEL), lambda i, j: (i, j, 0)),
        out_shape=jax.ShapeDtypeStruct((b, s, D_MODEL), F32),
        scratch_shapes=[pltpu.VMEM((SUBLANE, D_FF), F32), pltpu.VMEM((ts, D_FF), BF16)],
        compiler_params=_cparams(("parallel", "arbitrary")),
        name="conv_ffn",
    )(x3, g, wup, cw, cb, wd, gf)


def _inproj_columns():
    starts = np.concatenate([[0], np.cumsum(IN_SPLITS)])
    (p_in, a_q, a_k, a_v, a_r, a_lr, d_q, d_k, d_v, d_b, d_a, d_g,
     n_q, n_kc, n_vc, n_ks, n_vs, n_kw, n_vw, n_g) = [np.arange(starts[i], starts[i + 1])
                                                      for i in range(len(IN_SPLITS))]
    n_q = n_q.reshape(NSA_GROUPS, NSA_HPG, NSA_DH).transpose(1, 0, 2).reshape(-1)
    misc = np.full((ZM_W,), N_IN)
    misc[MISC_LR:MISC_LR + GLA_LOWRANK] = a_lr
    misc[MISC_B:MISC_B + GDN_HEADS] = d_b
    misc[MISC_A:MISC_A + GDN_HEADS] = d_a
    misc[MISC_G:MISC_G + 3 * NSA_HEADS] = n_g
    cols = np.concatenate([p_in, a_q, a_k, a_v, a_r, d_q, d_k, d_v, d_g,
                           n_q, n_ks, n_vs, n_kw, n_vw, n_kc, n_vc, misc])
    assert cols.shape[0] == Z_W
    return cols


def _head_expand(offset, n_heads, width):
    e = np.zeros((ZM_W, n_heads * width), np.float32)
    for h in range(n_heads):
        e[offset + h, h * width:(h + 1) * width] = 1.0
    return e


def _nsa_constants(s):
    n_cmp = s // NSA_CMP_STRIDE - NSA_CMP_LEN // NSA_CMP_STRIDE + 1
    n_slc = s // NSA_SEL_LEN
    c_start = np.arange(n_cmp) * NSA_CMP_STRIDE
    s_start = np.arange(n_slc) * NSA_SEL_LEN
    cover = np.zeros((N_CMP_PAD, N_SLC_PAD), np.float32)
    cover[:n_cmp, :n_slc] = ((c_start[:, None] <= s_start[None, :] + NSA_SEL_LEN - 1)
                             & (c_start[:, None] + NSA_CMP_LEN - 1 >= s_start[None, :]))
    gexp = np.zeros((ZM_W, 3, MIX_W), np.float32)
    for g in range(NSA_GROUPS):
        for j in range(NSA_HPG):
            slot = j * NSA_GROUPS + g
            for c in range(3):
                gexp[MISC_G + (g * NSA_HPG + j) * 3 + c, c, slot * NSA_DH:(slot + 1) * NSA_DH] = 1.0
    return jnp.asarray(cover.T, dtype=BF16), jnp.asarray(gexp.reshape(ZM_W, 3 * MIX_W), dtype=BF16)


def _block_diag(blocks):
    n, a, b = blocks.shape
    return jnp.einsum('gh,gab->gahb', jnp.eye(n, dtype=blocks.dtype), blocks).reshape(n * a, n * b)


def kernel(x, mem, g_mix, w_in, pool_w, pool_scale, gla_w_lr, gla_b_lr, gla_g_norm, gdn_conv, gdn_a_log,
           gdn_dt_bias, gdn_g_norm, nsa_pe, nsa_cmp_w1, nsa_cmp_w2, w_branch, w_gate, b_gate, w_out, g_cross,
           g_mem, w_xq, w_mem_kv, w_xo, g_ffn, w_up, ffn_conv, ffn_conv_b, w_down, g_final):
    b, s, d = x.shape
    depth = w_in.shape[0]
    t = b * s
    cols = _inproj_columns()
    covt, gexp = _nsa_constants(s)
    eba = jnp.asarray(np.concatenate([_head_expand(MISC_B, GDN_HEADS, GDN_DH),
                                      _head_expand(MISC_A, GDN_HEADS, GDN_DH)], axis=1), dtype=BF16)
    ones_bd = _block_diag(jnp.ones((GDN_HEADS, GDN_DH, GDN_DH), BF16))
    nsa_rows = np.arange(MIX_W).reshape(NSA_GROUPS, NSA_HPG, NSA_DH).transpose(1, 0, 2).reshape(-1)
    row = lambda v: v.reshape(1, -1).astype(F32)
    misc_a = lambda v: jnp.zeros((1, ZM_W), F32).at[0, MISC_A:MISC_A + GDN_HEADS].set(v)

    x2 = x.reshape(t, d)
    for l in range(depth):
        w_in_r = jnp.concatenate([w_in[l], jnp.zeros((d, 1), F32)], axis=1)[:, cols].astype(BF16)
        wlr = jnp.zeros((ZM_W, GLA_HEADS * GLA_DK), F32).at[MISC_LR:MISC_LR + GLA_LOWRANK].set(gla_w_lr[l])
        wlr2 = wlr.astype(BF16)
        o_pool, zn, zc, zm, o_gla, o_gdn = _mixers(
            x2.reshape(b, s, d), row(g_mix[l]), w_in_r, _block_diag(pool_w[l]).astype(BF16), row(pool_scale[l]),
            wlr2, row(gla_b_lr[l]), row(jnp.tile(gla_g_norm[l], GLA_HEADS)),
            gdn_conv[l], eba, misc_a(gdn_a_log[l]), misc_a(gdn_dt_bias[l]),
            row(jnp.tile(gdn_g_norm[l], GDN_HEADS)), ones_bd)

        pe_x = jnp.tile(nsa_pe[l], (1, 1, NSA_GROUPS))
        w1 = nsa_cmp_w1[l].reshape(2, NSA_CMP_LEN, NSA_DH, NSA_DH)
        eye_g = jnp.eye(NSA_GROUPS, dtype=F32)
        w1_bd = jnp.einsum('gh,kpde->kpgdhe', eye_g, w1).reshape(2, NSA_CMP_LEN, LANE, LANE).astype(BF16)
        w2_bd = jnp.einsum('gh,kde->kgdhe', eye_g, nsa_cmp_w2[l]).reshape(2, LANE, LANE).astype(BF16)
        ck, cv = _nsa_compress(zc, pe_x, w1_bd, w2_bd)
        o_nsa = _nsa(zn, zm, ck, cv, covt, gexp)

        wb = jnp.concatenate([w_branch[l, :3], w_branch[l, 3][nsa_rows][None]], axis=0).astype(BF16)
        branches = [o.reshape(t, MIX_W) for o in (o_pool, o_gla, o_gdn, o_nsa)]
        x2 = _combine(x2, row(g_mix[l]), branches, w_gate[l].astype(BF16),
                      b_gate[l].reshape(N_BRANCH, 1, d), wb, w_out[l].astype(BF16))

        x3 = _cross(x2.reshape(b, s, d), row(g_cross[l]), w_xq[l].astype(BF16), mem, row(g_mem[l]),
                    w_mem_kv[l].astype(BF16), w_xo[l].astype(BF16))

        x3 = _ffn(x3, row(g_ffn[l]), w_up[l].astype(BF16), ffn_conv[l], row(ffn_conv_b[l]),
                  w_down[l].astype(BF16), row(g_final), final=(l == depth - 1))
        x2 = x3.reshape(t, d)
    return x2.reshape(b, s, d)
```

```python
import functools

import numpy as np
import jax
import jax.numpy as jnp
from jax import lax
from jax.experimental import pallas as pl
from jax.experimental.pallas import tpu as pltpu

F32 = jnp.float32
BF16 = jnp.bfloat16

D_MODEL = 1024
MIX_W = 256
POOL_WINDOWS = (2, 4, 8, 16)
POOL_GW = 64
GLA_HEADS = 4
GLA_DK = 32
GLA_DV = 64
GLA_LOWRANK = 16
GLA_GATE_NORM = 16.0
CHUNK = 64
GDN_HEADS = 4
GDN_DH = 64
GDN_CONV = 4
NSA_HEADS = 4
NSA_GROUPS = 2
NSA_HPG = 2
NSA_DH = 64
NSA_KV = 128
NSA_CMP_LEN = 32
NSA_CMP_STRIDE = 16
NSA_SEL_LEN = 64
NSA_N_SEL = 16
NSA_WINDOW = 512
X_HEADS = 4
X_DH = 128
D_FF = 2816
FFN_CONV = 3
EPS = 1e-6
N_BRANCH = 4

IN_SPLITS = (MIX_W,
             128, 128, 256, 256, GLA_LOWRANK,
             MIX_W, MIX_W, MIX_W, GDN_HEADS, GDN_HEADS, MIX_W,
             256, NSA_KV, NSA_KV, NSA_KV, NSA_KV, NSA_KV, NSA_KV, 3 * NSA_HEADS)
N_IN = sum(IN_SPLITS)

MISC_LR = 0
MISC_B = 16
MISC_A = 20
MISC_G = 24
LANE = 128
SUBLANE = 8

Z_WIDTHS = (256, 768, 1024, 768, 256, 128)
GDN_QKV_W = 3 * MIX_W
ZP_W, ZG_W, ZD_W, ZN_W, ZC_W, ZM_W = Z_WIDTHS
Z_W = sum(Z_WIDTHS)
NEG_BIG = -1e30
F32_LOWEST = float(np.finfo(np.float32).min)

TM_COMB = 512
TM_CROSS = 1024
CROSS_SUB = 512
TS_FFN = 1024
FF_CHUNK = 256
TQ_NSA = 256
NSA_SEQS = 2
NSA_TILE_UNROLL = 4
REC_SEQS = 16
REC_LAG = 3
GLA_START = 6
N_SLC_PAD = 128
N_CMP_PAD = 128

VMEM_LIMIT = 56 * 1024 * 1024


def _cparams(sem):
    return pltpu.CompilerParams(dimension_semantics=sem, vmem_limit_bytes=VMEM_LIMIT)


def _rms(x, g):
    return x * lax.rsqrt(jnp.mean(x * x, axis=-1, keepdims=True) + EPS) * g


def _dot(a, b):
    return jnp.dot(a.astype(BF16), b.astype(BF16), preferred_element_type=F32)


def _dot_nt(a, b):
    return lax.dot_general(a.astype(BF16), b.astype(BF16), (((1,), (1,)), ((), ())),
                           preferred_element_type=F32)


def _dot_tn(a, b):
    return lax.dot_general(a.astype(BF16), b.astype(BF16), (((0,), (0,)), ((), ())),
                           preferred_element_type=F32)


def _split2_dot(a, b):
    hi = a.astype(BF16)
    lo = (a - hi.astype(F32)).astype(BF16)
    return jnp.dot(hi, b, preferred_element_type=F32) + jnp.dot(lo, b, preferred_element_type=F32)


def _split3_lhs_dot(a, b):
    hi = a.astype(BF16)
    r1 = a - hi.astype(F32)
    mid = r1.astype(BF16)
    lo = (r1 - mid.astype(F32)).astype(BF16)
    return ((jnp.dot(hi, b, preferred_element_type=F32) + jnp.dot(mid, b, preferred_element_type=F32))
            + jnp.dot(lo, b, preferred_element_type=F32))


def _iota(shape, axis):
    return lax.broadcasted_iota(jnp.int32, shape, axis)


def _block_mask(rows, cols, rb, cb):
    return (_iota((rows, cols), 0) // rb) == (_iota((rows, cols), 1) // cb)


def _shift_rows(x, k):
    t = _iota(x.shape, 0)
    return jnp.where(t >= k, pltpu.roll(x, k, 0), 0.0)


def _cumsum_rows(x):
    k = 1
    while k < x.shape[0]:
        x = x + _shift_rows(x, k)
        k *= 2
    return x


def _softplus(x):
    return jnp.maximum(x, 0.0) + jnp.log1p(jnp.exp(-jnp.abs(x)))


def _log_sigmoid(x):
    return -_softplus(-x)


def _silu(x):
    return x * jax.nn.sigmoid(x)


def _masked_softmax(s, mask):
    s = jnp.where(mask, s, -jnp.inf)
    m = jnp.maximum(jnp.max(s, axis=-1, keepdims=True), F32_LOWEST)
    e = jnp.exp(s - m)
    den = jnp.sum(e, axis=-1, keepdims=True)
    return e / jnp.where(den > 0, den, 1.0)


def _tile4(x):
    return jnp.concatenate([x, x, x, x], axis=0)


POOL_HALO = 16


def _gla_masks():
    return (_block_mask(4 * CHUNK, 128, CHUNK, GLA_DK),
            _block_mask(4 * CHUNK, 256, CHUNK, GLA_DV),
            _block_mask(256, 128, GLA_DV, GLA_DK),
            (_iota((CHUNK, 256), 1) % CHUNK) <= _iota((CHUNK, 256), 0))


def _seqs(f, *lists):
    return [f(*args) for args in zip(*lists)]


def _run_staggered(stage_gens, starts, periods):
    live = list(range(len(stage_gens)))
    tick = 0
    while live:
        for i in list(live):
            if tick >= starts[i] and (tick - starts[i]) % periods[i] == 0:
                try:
                    next(stage_gens[i])
                except StopIteration:
                    live.remove(i)
        tick += 1


def _gla_chunk(masks, zqk, v, r, zm, wlr_ref, blr_ref, gn_ref, ones_ref, st_refs, out):
    mask_k, mask_v, mask_st, causal = masks
    q = [z[:, 0:128] * GLA_DK ** -0.5 for z in zqk]
    k = [z[:, 128:256] for z in zqk]
    pre = _seqs(lambda a: _dot(a, wlr_ref[...]), zm)
    yield
    bc = _seqs(lambda a: _cumsum_rows(_log_sigmoid(a + blr_ref[...]) / GLA_GATE_NORM), pre)
    bl = [a[CHUNK - 1:CHUNK, :] for a in bc]
    q_e = _seqs(lambda a, c: a * jnp.exp(c), q, bc)
    k_e = _seqs(lambda a, c: a * jnp.exp(-c), k, bc)
    k_u = _seqs(lambda a, c, l: a * jnp.exp(l - c), k, bc, bl)
    st = [ref[...] for ref in st_refs]
    yield
    att = _seqs(lambda a, b: jnp.where(causal, _dot_nt(a, jnp.where(mask_k, _tile4(b), 0.0)), 0.0), q_e, k_e)
    inter = _seqs(_dot_nt, q_e, st)
    kv = _seqs(_dot_tn, v, k_u)
    yield
    o = _seqs(lambda a, b, c: _dot(a, _tile4(b.astype(BF16)) * ones_ref[...]) + c, att, v, inter)
    for ref, s_old, l, upd in zip(st_refs, st, bl, kv):
        ref[...] = s_old * jnp.exp(l) + jnp.where(mask_st, upd, 0.0)
    yield
    ms = _seqs(lambda a: _dot(a * a, ones_ref[...]) * (1.0 / GLA_DV), o)
    yield
    out[:] = _seqs(lambda a, m, g: a * lax.rsqrt(m + EPS) * gn_ref[...] * _silu(g), o, ms, r)


GDN_HALO = SUBLANE


def _gdn_masks():
    c4 = 4 * CHUNK
    col = _iota((CHUNK, c4), 1) % CHUNK
    row = _iota((CHUNK, c4), 0)
    return (_block_mask(c4, c4, CHUNK, CHUNK), col <= row, col < row, col == row)


def _gdn_chunk(masks, halo, zqkv, gate, zm, cw_ref, eba_ref, alog_ref, dtb_ref, gn_ref, ones_ref, st_refs, out):
    mask_bd, incl, strict, diag = masks
    cw = cw_ref[...]
    bd = lambda a: _tile4(a.astype(BF16)) * ones_ref[...]

    def conv_silu(h, cur):
        ext = jnp.concatenate([h, cur], axis=0)
        conv = (cw[3:4] * cur + cw[2:3] * pltpu.roll(ext, 1, 0)[GDN_HALO:]
                + cw[1:2] * pltpu.roll(ext, 2, 0)[GDN_HALO:] + cw[0:1] * pltpu.roll(ext, 3, 0)[GDN_HALO:])
        return _silu(conv)

    qkv = _seqs(conv_silu, halo, zqkv)
    yield
    v = [a[:, 512:768] for a in qkv]
    ssq = _seqs(lambda a: _dot(jnp.concatenate([a[:, 0:256] * a[:, 0:256], a[:, 256:512] * a[:, 256:512]],
                                               axis=0), ones_ref[...]), qkv)
    yield
    q = _seqs(lambda a, s: a[:, 0:256] * lax.rsqrt(s[0:CHUNK] + EPS) * GDN_DH ** -0.5, qkv, ssq)
    k = _seqs(lambda a, s: a[:, 256:512] * lax.rsqrt(s[CHUNK:2 * CHUNK] + EPS), qkv, ssq)
    beta = _seqs(lambda a: _dot(jax.nn.sigmoid(a), eba_ref[:, 0:MIX_W]), zm)
    yield
    gc = _seqs(lambda a: _split3_lhs_dot(_cumsum_rows(-jnp.exp(alog_ref[...]) * _softplus(a + dtb_ref[...])),
                                         eba_ref[:, MIX_W:2 * MIX_W]), zm)
    yield
    gl = [a[CHUNK - 1:CHUNK, :] for a in gc]
    g_row = _seqs(lambda a: jnp.sum(jnp.where(diag, a, 0.0), axis=0, keepdims=True), gc)
    yield
    decay =_seqs(lambda a, b: jnp.exp(jnp.where(incl, a - b, -jnp.inf)), gc, g_row)
    kb = _seqs(lambda a, b: a * b, k, beta)
    vb = _seqs(lambda a, b: a * b, v, beta)
    kq = _seqs(lambda a, b, c: _dot_nt(jnp.concatenate([a, b], axis=0), bd(c)), kb, q, k)
    yield
    n_mat =_seqs(lambda s, d: jnp.where(strict, s[0:CHUNK] * d, 0.0), kq, decay)
    a_qk = _seqs(lambda s, d: s[CHUNK:2 * CHUNK] * d, kq, decay)
    m = _seqs(lambda a: -a, n_mat)
    x = _seqs(lambda a: jnp.where(diag, 1.0, 0.0) + a, m)
    m = _seqs(lambda a: _dot(a, bd(a)), m)
    yield
    p = 2
    while p < CHUNK // 2:
        xm = _seqs(lambda a, b: _dot(jnp.concatenate([a, b], axis=0), bd(b)), x, m)
        x = _seqs(lambda a, r: a + r[0:CHUNK], x, xm)
        m = [r[CHUNK:2 * CHUNK] for r in xm]
        p *= 2
        yield
    x = _seqs(lambda a, b: a + _dot(a, bd(b)), x, m)
    yield
    uw = _seqs(lambda a, b, c, g: _dot(a, jnp.concatenate([bd(b), bd(c * jnp.exp(g))], axis=1)),
               x, vb, kb, gc)
    yield
    st = [ref[...] for ref in st_refs]
    ws = _seqs(lambda a, b, g, s: _dot(jnp.concatenate([a[:, 256:512], b * jnp.exp(g)], axis=0), s),
               uw, q, gc, st)
    yield
    v_new = _seqs(lambda a, r: a[:, 0:256] - r[0:CHUNK], uw, ws)
    o = _seqs(lambda r, qk, vn: r[CHUNK:2 * CHUNK] + _dot(qk, bd(vn)), ws, a_qk, v_new)
    upd = _seqs(lambda a, g, l, vn: _dot_tn(a * jnp.exp(l - g), vn), k, gc, gl, v_new)
    yield
    for ref, s_old, l, u in zip(st_refs, st, gl, upd):
        ref[...] = s_old * jnp.exp(l) + jnp.where(mask_bd, u, 0.0)
    ms = _seqs(lambda a: _dot(a * a, ones_ref[...]) * (1.0 / GDN_DH), o)
    yield
    out[:] = _seqs(lambda a, m_, g: a * lax.rsqrt(m_ + EPS) * gn_ref[...] * _silu(g), o, ms, gate)


def _pool_diff(u, tail, pos0):
    ext = jnp.concatenate([tail, u], axis=0)
    s2 = ext + pltpu.roll(ext, 1, 0)
    s4 = s2 + pltpu.roll(s2, 2, 0)
    s8 = s4 + pltpu.roll(s4, 4, 0)
    s16 = s8 + pltpu.roll(s8, 8, 0)
    grp = _iota(u.shape, 1) // POOL_GW
    win = jnp.where(grp == 0, s2[POOL_HALO:], jnp.where(grp == 1, s4[POOL_HALO:],
                                                        jnp.where(grp == 2, s8[POOL_HALO:], s16[POOL_HALO:])))
    width = jnp.where(grp == 0, POOL_WINDOWS[0],
                      jnp.where(grp == 1, POOL_WINDOWS[1],
                                jnp.where(grp == 2, POOL_WINDOWS[2], POOL_WINDOWS[3])))
    cnt = jnp.minimum(pos0 + _iota(u.shape, 0) + 1, width).astype(F32)
    return win / cnt - u


def _mix_kernel(x_ref, g_ref, w_ref, pw_ref, ps_ref, wlr_ref, blr_ref, gng_ref, cw_ref, eba_ref, alog_ref, dtb_ref,
                gnd_ref, ones_ref, op_ref, zn_ref, zc_ref, zm_ref, og_ref, od_ref, ptail_ref, halo_ref, *st_refs):
    nb = x_ref.shape[0]
    chunk = pl.program_id(1)

    @pl.when(chunk == 0)
    def _():
        ptail_ref[...] = jnp.zeros_like(ptail_ref)
        halo_ref[...] = jnp.zeros_like(halo_ref)
        for st_ref in st_refs:
            st_ref[...] = jnp.zeros_like(st_ref)

    h = _rms(x_ref[...].reshape(nb * CHUNK, D_MODEL), g_ref[...]).astype(BF16)
    proj = lambda lo, n: jnp.dot(h, w_ref[:, lo:lo + n], preferred_element_type=F32)
    seqs = lambda a: [a[i * CHUNK:(i + 1) * CHUNK] for i in range(nb)]
    off_g, off_d = ZP_W, ZP_W + ZG_W
    off_n = off_d + ZD_W
    off_c = off_n + ZN_W
    off_m = off_c + ZC_W
    zqkv = seqs(proj(off_d, GDN_QKV_W))
    zm = seqs(proj(off_m, ZM_W))
    halo = [halo_ref[i] for i in range(nb)]
    for i in range(nb):
        halo_ref[i] = zqkv[i][CHUNK - GDN_HALO:, :]
        zm_ref[i] = zm[i]
    halves = [slice(0, nb // 2), slice(nb // 2, nb)] if nb > 1 else [slice(0, nb)]
    gla_qk, gla_v, gla_r = [], [], []
    gate_parts = [[] for _ in halves]

    def other_projections():
        gla_qk.extend(seqs(proj(off_g, 256)))
        yield
        gla_v.extend(seqs(proj(off_g + 256, 256)))
        yield
        u = seqs(proj(0, ZP_W))
        yield
        diff = [_pool_diff(u[i], ptail_ref[i], chunk * CHUNK) for i in range(nb)]
        for i in range(nb):
            ptail_ref[i] = u[i][CHUNK - POOL_HALO:, :]
        yield
        for lo in range(0, ZN_W, 256):
            zn = seqs(proj(off_n + lo, 256))
            for i in range(nb):
                zn_ref[i, :, lo:lo + 256] = zn[i].astype(zn_ref.dtype)
            yield
        zc = seqs(proj(off_c, ZC_W))
        for i in range(nb):
            zc_ref[i] = zc[i]
        yield
        gla_r.extend(seqs(proj(off_g + 512, 256)))
        yield
        gate = seqs(proj(off_d + GDN_QKV_W, MIX_W))
        for hs, part in zip(halves, gate_parts):
            part.extend(gate[hs])
        yield
        o_pool = seqs(_dot(jnp.concatenate(diff, axis=0), pw_ref[...]) * ps_ref[...])
        for i in range(nb):
            op_ref[i] = o_pool[i].astype(op_ref.dtype)

    og = [None] * nb
    od_parts = [[None] * (hs.stop - hs.start) for hs in halves]
    gdn_masks = _gdn_masks()
    gens = [_gdn_chunk(gdn_masks, halo[hs], zqkv[hs], gate_part, zm[hs], cw_ref, eba_ref, alog_ref, dtb_ref,
                       gnd_ref, ones_ref, st_refs[nb:][hs], part)
            for hs, part, gate_part in zip(halves, od_parts, gate_parts)]
    gens.append(_gla_chunk(_gla_masks(), gla_qk, gla_v, gla_r, zm, wlr_ref, blr_ref, gng_ref, ones_ref,
                           st_refs[:nb], og))
    starts = [REC_LAG * i for i in range(len(halves))] + [GLA_START]
    gens.append(other_projections())
    _run_staggered(gens, starts + [0], [1] * len(gens))
    od = [o for part in od_parts for o in part]
    for i in range(nb):
        og_ref[i] = og[i].astype(og_ref.dtype)
        od_ref[i] = od[i].astype(od_ref.dtype)


def _mixers(x3, g, w, pool_w_bd, pool_scale, wlr2, blr, gn_gla, cw, eba, alog, dtb, gn_gdn, ones_bd):
    b, s, _ = x3.shape
    nb = REC_SEQS if b % REC_SEQS == 0 else 1
    const = lambda shape: pl.BlockSpec(shape, lambda i, c: (0,) * len(shape))
    tok = lambda n: pl.BlockSpec((nb, CHUNK, n), lambda i, c: (i, c, 0))
    outs = ((MIX_W, BF16), (ZN_W, BF16), (ZC_W, F32), (ZM_W, F32), (MIX_W, BF16), (MIX_W, BF16))
    return pl.pallas_call(
        _mix_kernel,
        grid=(b // nb, s // CHUNK),
        in_specs=[tok(D_MODEL), const((1, D_MODEL)), const((D_MODEL, Z_W)), const((MIX_W, MIX_W)), const((1, MIX_W)),
                  const((ZM_W, 128)), const((1, 128)), const((1, MIX_W)),
                  const((GDN_CONV, GDN_QKV_W)), const((ZM_W, 2 * MIX_W)), const((1, ZM_W)), const((1, ZM_W)),
                  const((1, MIX_W)), const((MIX_W, MIX_W))],
        out_specs=[tok(n) for n, _ in outs],
        out_shape=[jax.ShapeDtypeStruct((b, s, n), dt) for n, dt in outs],
        scratch_shapes=([pltpu.VMEM((nb, POOL_HALO, MIX_W), F32), pltpu.VMEM((nb, GDN_HALO, GDN_QKV_W), F32)]
                        + [pltpu.VMEM((256, 128), F32)] * nb + [pltpu.VMEM((256, 256), F32)] * nb),
        compiler_params=_cparams(("parallel", "arbitrary")),
        name="inproj_mixers",
    )(x3, g, w, pool_w_bd, pool_scale, wlr2, blr, gn_gla, cw, eba, alog, dtb, gn_gdn, ones_bd)


def _cmp_kernel(kc_ref, vc_ref, pe_ref, w1_ref, w2_ref, ck_ref, cv_ref):
    n_sub = NSA_CMP_LEN // NSA_CMP_STRIDE
    n_chunks = kc_ref.shape[1] // NSA_CMP_STRIDE
    for which, (src_ref, out_ref) in enumerate(((kc_ref, ck_ref), (vc_ref, cv_ref))):
        parts = [jnp.zeros((n_chunks, LANE), F32) for _ in range(n_sub)]
        for i in range(NSA_CMP_STRIDE):
            slab = src_ref[0, pl.ds(i, n_chunks, stride=NSA_CMP_STRIDE), :]
            for sub in range(n_sub):
                p = sub * NSA_CMP_STRIDE + i
                parts[sub] = parts[sub] + _dot(slab + pe_ref[which, p:p + 1, :], w1_ref[which, p])
        pre = parts[0] + pltpu.roll(parts[1], n_chunks - 1, 0)
        out_ref[0] = _dot(jax.nn.gelu(pre), w2_ref[which])


def _nsa_compress(zc, pe_x, w1_bd, w2_bd):
    b, s, _ = zc.shape
    n_chunks = s // NSA_CMP_STRIDE
    return pl.pallas_call(
        _cmp_kernel,
        grid=(b,),
        in_specs=[pl.BlockSpec((1, s, LANE), lambda i: (i, 0, 0)),
                  pl.BlockSpec((1, s, LANE), lambda i: (i, 0, 1)),
                  pl.BlockSpec((2, NSA_CMP_LEN, LANE), lambda i: (0, 0, 0)),
                  pl.BlockSpec((2, NSA_CMP_LEN, LANE, LANE), lambda i: (0, 0, 0, 0)),
                  pl.BlockSpec((2, LANE, LANE), lambda i: (0, 0, 0))],
        out_specs=[pl.BlockSpec((1, n_chunks, LANE), lambda i: (i, 0, 0))] * 2,
        out_shape=[jax.ShapeDtypeStruct((b, n_chunks, LANE), F32)] * 2,
        compiler_params=_cparams(("parallel",)),
        name="nsa_compress",
    )(zc, zc, pe_x, w1_bd, w2_bd)


def _topk_rank(v):
    n, t = v.shape
    blocks = [v[r:r + SUBLANE] for r in range(0, n, SUBLANE)]
    sub = _iota((SUBLANE, t), 0)
    rank = [jnp.zeros((SUBLANE, t), jnp.int32) for _ in blocks]
    for m in range(n):
        other = v[m:m + 1, :]
        mb, ms = divmod(m, SUBLANE)
        for r, blk in enumerate(blocks):
            if r < mb:
                ahead = (other > blk).astype(jnp.int32)
            elif r > mb:
                ahead = (other >= blk).astype(jnp.int32)
            else:
                ahead = jnp.where(sub > ms, (other >= blk).astype(jnp.int32), (other > blk).astype(jnp.int32))
            rank[r] = rank[r] + ahead
    return jnp.concatenate(rank, axis=0)


def _exp_weights(s, m):
    return jnp.exp((s - m).astype(BF16))


def _normalise_aug(o_aug, in_grp):
    den = pltpu.roll(o_aug, NSA_DH, 1)
    return jnp.where(in_grp, o_aug / jnp.where(den > 0, den, 1.0), 0.0)


def _nsa_kernel(q_ref, ksv_ref, kwv_ref, ck_ref, cv_ref, zm_ref, covt_ref, gexp_ref, o_ref,
                kaug_ref, vaug_ref, vwaug_ref, s_ref, mx_ref, acc_ref):
    nb = q_ref.shape[0]
    tq = q_ref.shape[1]
    s_len = ksv_ref.shape[1]
    n_slc = s_len // NSA_SEL_LEN
    qi = pl.program_id(1)
    s0 = pl.multiple_of(qi * tq, tq)
    lane_grp = _iota((1, LANE), 1) // NSA_DH
    chains = [(bi, g) for bi in range(nb) for g in range(NSA_GROUPS)]
    n_chains = range(len(chains))

    @pl.when(qi == 0)
    def _():
        lane = _iota((s_len, LANE), 1)
        key_blk = _iota((s_len, LANE), 0) // NSA_SEL_LEN
        for bi in range(nb):
            k = ksv_ref[bi, :, 0:LANE].astype(F32)
            v = ksv_ref[bi, :, LANE:2 * LANE].astype(F32)
            vw = kwv_ref[bi, :, LANE:2 * LANE].astype(F32)
            for g in range(NSA_GROUPS):
                own = lane // NSA_DH == g
                onehot = jnp.where(lane - (1 - g) * NSA_DH == key_blk, 1.0, 0.0)
                kaug_ref[bi * NSA_GROUPS + g] = jnp.where(own, k, onehot).astype(BF16)
                vaug_ref[bi * NSA_GROUPS + g] = jnp.where(own, v, 1.0).astype(BF16)
                vwaug_ref[bi * NSA_GROUPS + g] = jnp.where(own, vw, 1.0).astype(BF16)

    t_col = s0 + _iota((tq, 1), 0)
    t_col2 = jnp.concatenate([t_col, t_col], axis=0)
    t_row = s0 + _iota((1, tq), 1)
    q = [q_ref[bi].astype(F32) * NSA_DH ** -0.5 for bi in range(nb)]
    ck_hi = [ck_ref[bi].astype(BF16) for bi in range(nb)]
    ck_lo = [(ck_ref[bi] - ck_hi[bi].astype(F32)).astype(BF16) for bi in range(nb)]
    cv = [cv_ref[bi] for bi in range(nb)]
    cmp_end = _iota((1, N_CMP_PAD), 1) * NSA_CMP_STRIDE + (NSA_CMP_LEN - 1)
    cmp_valid = cmp_end <= t_col2
    blk_t = _iota((n_slc, tq), 0)
    cur_t = t_row // NSA_SEL_LEN
    forced_t = (blk_t == 0) | (blk_t == cur_t) | (blk_t == cur_t - 1)
    future_t = blk_t > cur_t
    row_in_tile = jnp.concatenate([_iota((tq, tq), 0)] * NSA_HPG, axis=0)
    col_in_tile = _iota((NSA_HPG * tq, tq), 1)
    diag_ok = col_in_tile <= row_in_tile
    n_wt = NSA_WINDOW // tq + 1
    never = 2 * tq
    win_off = [pl.multiple_of(jnp.maximum(qi - (n_wt - 1 - w), 0) * tq, tq) for w in range(n_wt)]
    win_ok = [col_in_tile > row_in_tile + jnp.where(qi >= n_wt - 1, 0, never)]
    win_ok += [col_in_tile >= jnp.where(qi >= n_wt - 1 - w, 0, never) for w in range(1, n_wt - 1)]
    win_ok += [diag_ok]

    o_cmp = [[jnp.zeros((tq, LANE), F32) for _ in range(NSA_HPG)] for _ in range(nb)]
    o_slc = [[jnp.zeros((tq, LANE), F32) for _ in range(NSA_HPG)] for _ in range(nb)]
    o_win = [[jnp.zeros((tq, LANE), F32) for _ in range(NSA_HPG)] for _ in range(nb)]
    in_grp = [lane_grp == g for _, g in chains]
    q2 = [jnp.concatenate([jnp.where(in_grp[c], q[bi][:, j * LANE:(j + 1) * LANE], 0.0)
                           for j in range(NSA_HPG)], axis=0) for c, (bi, _) in enumerate(chains)]
    qb = [a.astype(BF16) for a in q2]
    p_c = [_masked_softmax(_dot_nt(qb[c], ck_hi[bi]) + _dot_nt(qb[c], ck_lo[bi]), cmp_valid)
           for c, (bi, _) in enumerate(chains)]
    oc = [jnp.where(in_grp[c], _dot(p_c[c], cv[bi]), 0.0) for c, (bi, _) in enumerate(chains)]
    imp_c = [a[0:tq] + a[tq:2 * tq] for a in p_c]
    imp_hi = [a.astype(BF16) for a in imp_c]
    imp_lo = [(a - h.astype(F32)).astype(BF16) for a, h in zip(imp_c, imp_hi)]
    imp_t = [(_dot_nt(covt_ref[...], h) + _dot_nt(covt_ref[...], l_))[0:n_slc]
             for h, l_ in zip(imp_hi, imp_lo)]
    imp_t = [jnp.where(forced_t, jnp.inf, jnp.where(future_t, -jnp.inf, a)) for a in imp_t]
    s_w = [[jnp.where(ok, _dot_nt(qb[c], kwv_ref[bi, pl.ds(off, tq), 0:LANE]), -jnp.inf)
            for ok, off in zip(win_ok, win_off)] for c, (bi, _) in enumerate(chains)]
    qa = []
    for c, (_, g) in enumerate(chains):
        bias_t = jnp.where(_topk_rank(imp_t[c]) < NSA_N_SEL, 0.0, NEG_BIG)
        lo = (1 - g) * NSA_DH
        rows = ([jnp.zeros((lo, tq), F32)] if lo else []) + [bias_t, jnp.zeros((LANE - lo - n_slc, tq), F32)]
        bias = jnp.concatenate(rows, axis=0).T
        qa.append(jnp.where(in_grp[c], q2[c], jnp.concatenate([bias] * NSA_HPG, axis=0)).astype(BF16))
    m_w = []
    for c in n_chains:
        m_c = s_w[c][0]
        for s_t in s_w[c][1:]:
            m_c = jnp.maximum(m_c, s_t)
        m_w.append(jnp.maximum(jnp.max(m_c, axis=-1, keepdims=True), F32_LOWEST))
    ow = [jnp.zeros((NSA_HPG * tq, LANE), F32) for _ in n_chains]
    for w, off in enumerate(win_off):
        for c in n_chains:
            ow[c] = ow[c] + jnp.dot(_exp_weights(s_w[c][w], m_w[c]), vwaug_ref[c, pl.ds(off, tq), :],
                                    preferred_element_type=F32)
    for c, (bi, _) in enumerate(chains):
        ow_c = _normalise_aug(ow[c], in_grp[c])
        for j in range(NSA_HPG):
            rows_j = slice(j * tq, (j + 1) * tq)
            o_cmp[bi][j] = o_cmp[bi][j] + oc[c][rows_j]
            o_win[bi][j] = o_win[bi][j] + ow_c[rows_j]

    half_max = lambda s_t: jnp.maximum(s_t[:, 0:LANE], s_t[:, LANE:2 * LANE])
    mx_ref[...] = jnp.full(mx_ref.shape, NEG_BIG, F32)
    groups = n_chains

    def score_tiles(tiles):
        offs = [pl.multiple_of(kt * tq, tq) for kt in tiles]
        s_new = [[_dot_nt(qa[g], kaug_ref[g, pl.ds(off, tq), :]) for g in groups] for off in offs]
        for kt, s_kt in zip(tiles, s_new):
            for g in groups:
                s_ref[g, kt] = s_kt[g]
        for g in groups:
            m_new = half_max(s_new[0][g])
            for s_kt in s_new[1:]:
                m_new = jnp.maximum(m_new, half_max(s_kt[g]))
            mx_ref[g] = jnp.maximum(mx_ref[g], m_new)

    def run_tiles(fn, n_tiles):
        def body(p, carry):
            fn([NSA_TILE_UNROLL * p + k for k in range(NSA_TILE_UNROLL)])
            return carry

        lax.fori_loop(0, n_tiles // NSA_TILE_UNROLL, body, 0)
        left = n_tiles % NSA_TILE_UNROLL
        for r in range(1, NSA_TILE_UNROLL):
            @pl.when(left == r)
            def _():
                fn([n_tiles - r + k for k in range(r)])

    run_tiles(score_tiles, qi)

    m_s = []
    for g in groups:
        s_d = jnp.where(diag_ok, _dot_nt(qa[g], kaug_ref[g, pl.ds(s0, tq), :]), NEG_BIG)
        s_ref[g, qi] = s_d
        m_g = jnp.max(jnp.maximum(mx_ref[g], half_max(s_d)), axis=-1, keepdims=True)
        m_s.append(jnp.maximum(m_g, F32_LOWEST))
    acc_ref[...] = jnp.zeros(acc_ref.shape, F32)

    def attend_tiles(tiles):
        offs = [pl.multiple_of(kt * tq, tq) for kt in tiles]
        pv = [[jnp.dot(_exp_weights(s_ref[g, kt], m_s[g]), vaug_ref[g, pl.ds(off, tq), :],
                       preferred_element_type=F32) for g in groups] for kt, off in zip(tiles, offs)]
        for g in groups:
            upd = pv[0][g]
            for pv_kt in pv[1:]:
                upd = upd + pv_kt[g]
            acc_ref[g] += upd

    run_tiles(attend_tiles, qi + 1)

    for c, (bi, _) in enumerate(chains):
        os_ = _normalise_aug(acc_ref[c], in_grp[c])
        for j in range(NSA_HPG):
            o_slc[bi][j] = o_slc[bi][j] + os_[j * tq:(j + 1) * tq]

    for bi in range(nb):
        gates = _split2_dot(jax.nn.sigmoid(zm_ref[bi]), gexp_ref[...])
        o = (gates[:, 0:MIX_W] * jnp.concatenate(o_cmp[bi], axis=1)
             + gates[:, MIX_W:2 * MIX_W] * jnp.concatenate(o_slc[bi], axis=1)
             + gates[:, 2 * MIX_W:3 * MIX_W] * jnp.concatenate(o_win[bi], axis=1))
        o_ref[bi] = o.astype(o_ref.dtype)


def _nsa(zn, zm, ck, cv, covt, gexp):
    b, s, _ = zn.shape
    tq = TQ_NSA
    nb = NSA_SEQS if b % NSA_SEQS == 0 else 1
    n_chains = nb * NSA_GROUPS
    return pl.pallas_call(
        _nsa_kernel,
        grid=(b // nb, s // tq),
        in_specs=[pl.BlockSpec((nb, tq, 256), lambda i, j: (i, j, 0)),
                  pl.BlockSpec((nb, s, 256), lambda i, j: (i, 0, 1)),
                  pl.BlockSpec((nb, s, 256), lambda i, j: (i, 0, 2)),
                  pl.BlockSpec((nb, N_CMP_PAD, LANE), lambda i, j: (i, 0, 0)),
                  pl.BlockSpec((nb, N_CMP_PAD, LANE), lambda i, j: (i, 0, 0)),
                  pl.BlockSpec((nb, tq, ZM_W), lambda i, j: (i, j, 0)),
                  pl.BlockSpec((N_SLC_PAD, N_CMP_PAD), lambda i, j: (0, 0)),
                  pl.BlockSpec((ZM_W, 3 * MIX_W), lambda i, j: (0, 0))],
        out_specs=pl.BlockSpec((nb, tq, MIX_W), lambda i, j: (i, j, 0)),
        out_shape=jax.ShapeDtypeStruct((b, s, MIX_W), BF16),
        scratch_shapes=[pltpu.VMEM((n_chains, s, LANE), BF16)] * 3
                       + [pltpu.VMEM((n_chains, s // tq, NSA_HPG * tq, tq), F32),
                          pltpu.VMEM((n_chains, NSA_HPG * tq, LANE), F32),
                          pltpu.VMEM((n_chains, NSA_HPG * tq, LANE), F32)],
        compiler_params=_cparams(("parallel", "arbitrary")),
        name="nsa_attn",
    )(zn, zn, zn, ck, cv, zm, covt, gexp)


def _combine_kernel(x_ref, g_ref, op_ref, oa_ref, od_ref, on_ref, wg_ref, bg_ref, wb_ref, wo_ref, o_ref):
    x = x_ref[...]
    h = _rms(x, g_ref[...]).astype(BF16)
    y = jnp.zeros(x.shape, F32)
    for i, br_ref in enumerate((op_ref, oa_ref, od_ref, on_ref)):
        gate = jax.nn.sigmoid(jnp.dot(h, wg_ref[i], preferred_element_type=F32) + bg_ref[i])
        y = y + gate * jnp.dot(br_ref[...], wb_ref[i], preferred_element_type=F32)
    o_ref[...] = x + _dot(y, wo_ref[...])


def _combine(x2, g, branches, wg, bg, wb, wo):
    t = x2.shape[0]
    tm = TM_COMB
    return pl.pallas_call(
        _combine_kernel,
        grid=(t // tm,),
        in_specs=[pl.BlockSpec((tm, D_MODEL), lambda i: (i, 0)),
                  pl.BlockSpec((1, D_MODEL), lambda i: (0, 0))]
                 + [pl.BlockSpec((tm, MIX_W), lambda i: (i, 0))] * N_BRANCH
                 + [pl.BlockSpec((N_BRANCH, D_MODEL, D_MODEL), lambda i: (0, 0, 0)),
                    pl.BlockSpec((N_BRANCH, 1, D_MODEL), lambda i: (0, 0, 0)),
                    pl.BlockSpec((N_BRANCH, MIX_W, D_MODEL), lambda i: (0, 0, 0)),
                    pl.BlockSpec((D_MODEL, D_MODEL), lambda i: (0, 0))],
        out_specs=pl.BlockSpec((tm, D_MODEL), lambda i: (i, 0)),
        out_shape=jax.ShapeDtypeStruct((t, D_MODEL), F32),
        compiler_params=_cparams(("parallel",)),
        name="combine",
    )(x2, g, *branches, wg, bg, wb, wo)


def _cross_kernel(x_ref, g_ref, wq_ref, mem_ref, gm_ref, wkv_ref, wo_ref, o_ref, kv_ref):
    @pl.when(pl.program_id(1) == 0)
    def _():
        kv_ref[...] = _dot(_rms(mem_ref[0], gm_ref[...]), wkv_ref[...]).astype(kv_ref.dtype)

    n_k = X_HEADS * X_DH
    subs = [slice(i * CROSS_SUB, (i + 1) * CROSS_SUB) for i in range(x_ref.shape[1] // CROSS_SUB)]
    x = [x_ref[0, r, :] for r in subs]
    q = [_dot(_rms(a, g_ref[...]), wq_ref[...]) * X_DH ** -0.5 for a in x]
    pairs = [(i, h) for h in range(X_HEADS) for i in range(len(subs))]
    head = lambda h: slice(h * X_DH, (h + 1) * X_DH)
    sc = [_dot_nt(q[i][:, head(h)], kv_ref[:, head(h)]) for i, h in pairs]
    e = [jnp.exp(s - jnp.max(s, axis=-1, keepdims=True)) for s in sc]
    o = [_dot(e_ih, kv_ref[:, n_k + h * X_DH:n_k + (h + 1) * X_DH]) / jnp.sum(e_ih, axis=-1, keepdims=True)
         for e_ih, (i, h) in zip(e, pairs)]
    for i, r in enumerate(subs):
        o_i = jnp.concatenate([o[pairs.index((i, h))] for h in range(X_HEADS)], axis=1)
        o_ref[0, r, :] = x[i] + _dot(o_i, wo_ref[...])


def _cross(x3, g, wq, mem, g_mem, wkv, wo):
    b, s, _ = x3.shape
    tm = TM_CROSS
    m = mem.shape[1]
    n_k = X_HEADS * X_DH
    return pl.pallas_call(
        _cross_kernel,
        grid=(b, s // tm),
        in_specs=[pl.BlockSpec((1, tm, D_MODEL), lambda i, j: (i, j, 0)),
                  pl.BlockSpec((1, D_MODEL), lambda i, j: (0, 0)),
                  pl.BlockSpec((D_MODEL, n_k), lambda i, j: (0, 0)),
                  pl.BlockSpec((1, m, D_MODEL), lambda i, j: (i, 0, 0)),
                  pl.BlockSpec((1, D_MODEL), lambda i, j: (0, 0)),
                  pl.BlockSpec((D_MODEL, 2 * n_k), lambda i, j: (0, 0)),
                  pl.BlockSpec((n_k, D_MODEL), lambda i, j: (0, 0))],
        out_specs=pl.BlockSpec((1, tm, D_MODEL), lambda i, j: (i, j, 0)),
        out_shape=jax.ShapeDtypeStruct((b, s, D_MODEL), F32),
        scratch_shapes=[pltpu.VMEM((m, 2 * n_k), BF16)],
        compiler_params=_cparams(("parallel", "arbitrary")),
        name="cross_attn",
    )(x3, g, wq, mem, g_mem, wkv, wo)


def _ffn_kernel(x_ref, g_ref, wup_ref, cw_ref, cb_ref, wd_ref, gf_ref, o_ref, tail_ref, act_ref, *, final):
    @pl.when(pl.program_id(1) == 0)
    def _():
        tail_ref[...] = jnp.zeros_like(tail_ref)

    ts = x_ref.shape[1]
    x = x_ref[0]
    hn = _rms(x, g_ref[...]).astype(BF16)
    for c in range(D_FF // FF_CHUNK):
        cols = slice(c * FF_CHUNK, (c + 1) * FF_CHUNK)
        gcols = slice(D_FF + c * FF_CHUNK, D_FF + (c + 1) * FF_CHUNK)
        u = jnp.dot(hn, wup_ref[:, cols], preferred_element_type=F32)
        v = jnp.dot(hn, wup_ref[:, gcols], preferred_element_type=F32)
        ext = jnp.concatenate([tail_ref[:, cols], u], axis=0)
        tail_ref[:, cols] = u[ts - SUBLANE:, :]
        cw = cw_ref[:, cols]
        y = (cw[2:3] * u + cw[1:2] * pltpu.roll(ext, 1, 0)[SUBLANE:]
             + cw[0:1] * pltpu.roll(ext, 2, 0)[SUBLANE:] + cb_ref[:, cols])
        act_ref[:, cols] = (jax.nn.gelu(y) * v).astype(BF16)
    out = x + jnp.dot(act_ref[...], wd_ref[...], preferred_element_type=F32)
    if final:
        out = _rms(out, gf_ref[...])
    o_ref[0] = out


def _ffn(x3, g, wup, cw, cb, wd, gf, final):
    b, s, _ = x3.shape
    ts = TS_FFN
    return pl.pallas_call(
        functools.partial(_ffn_kernel, final=final),
        grid=(b, s // ts),
        in_specs=[pl.BlockSpec((1, ts, D_MODEL), lambda i, j: (i, j, 0)),
                  pl.BlockSpec((1, D_MODEL), lambda i, j: (0, 0)),
                  pl.BlockSpec((D_MODEL, 2 * D_FF), lambda i, j: (0, 0), pipeline_mode=pl.Buffered(1)),
                  pl.BlockSpec((FFN_CONV, D_FF), lambda i, j: (0, 0)),
                  pl.BlockSpec((1, D_FF), lambda i, j: (0, 0)),
                  pl.BlockSpec((D_FF, D_MODEL), lambda i, j: (0, 0), pipeline_mode=pl.Buffered(1)),
                  pl.BlockSpec((1, D_MODEL), lambda i, j: (0, 0))],
        out_specs=pl.BlockSpec((1, ts, D_MODEL), lambda i, j: (i, j, 0)),
        out_shape=jax.ShapeDtypeStruct((b, s, D_MODEL), F32),
        scratch_shapes=[pltpu.VMEM((SUBLANE, D_FF), F32), pltpu.VMEM((ts, D_FF), BF16)],
        compiler_params=_cparams(("parallel", "arbitrary")),
        name="conv_ffn",
    )(x3, g, wup, cw, cb, wd, gf)


def _inproj_columns():
    starts = np.concatenate([[0], np.cumsum(IN_SPLITS)])
    (p_in, a_q, a_k, a_v, a_r, a_lr, d_q, d_k, d_v, d_b, d_a, d_g,
     n_q, n_kc, n_vc, n_ks, n_vs, n_kw, n_vw, n_g) = [np.arange(starts[i], starts[i + 1])
                                                      for i in range(len(IN_SPLITS))]
    n_q = n_q.reshape(NSA_GROUPS, NSA_HPG, NSA_DH).transpose(1, 0, 2).reshape(-1)
    misc = np.full((ZM_W,), N_IN)
    misc[MISC_LR:MISC_LR + GLA_LOWRANK] = a_lr
    misc[MISC_B:MISC_B + GDN_HEADS] = d_b
    misc[MISC_A:MISC_A + GDN_HEADS] = d_a
    misc[MISC_G:MISC_G + 3 * NSA_HEADS] = n_g
    cols = np.concatenate([p_in, a_q, a_k, a_v, a_r, d_q, d_k, d_v, d_g,
                           n_q, n_ks, n_vs, n_kw, n_vw, n_kc, n_vc, misc])
    assert cols.shape[0] == Z_W
    return cols


def _head_expand(offset, n_heads, width):
    e = np.zeros((ZM_W, n_heads * width), np.float32)
    for h in range(n_heads):
        e[offset + h, h * width:(h + 1) * width] = 1.0
    return e


def _nsa_constants(s):
    n_cmp = s // NSA_CMP_STRIDE - NSA_CMP_LEN // NSA_CMP_STRIDE + 1
    n_slc = s // NSA_SEL_LEN
    c_start = np.arange(n_cmp) * NSA_CMP_STRIDE
    s_start = np.arange(n_slc) * NSA_SEL_LEN
    cover = np.zeros((N_CMP_PAD, N_SLC_PAD), np.float32)
    cover[:n_cmp, :n_slc] = ((c_start[:, None] <= s_start[None, :] + NSA_SEL_LEN - 1)
                             & (c_start[:, None] + NSA_CMP_LEN - 1 >= s_start[None, :]))
    gexp = np.zeros((ZM_W, 3, MIX_W), np.float32)
    for g in range(NSA_GROUPS):
        for j in range(NSA_HPG):
            slot = j * NSA_GROUPS + g
            for c in range(3):
                gexp[MISC_G + (g * NSA_HPG + j) * 3 + c, c, slot * NSA_DH:(slot + 1) * NSA_DH] = 1.0
    return jnp.asarray(cover.T, dtype=BF16), jnp.asarray(gexp.reshape(ZM_W, 3 * MIX_W), dtype=BF16)


def _block_diag(blocks):
    n, a, b = blocks.shape
    return jnp.einsum('gh,gab->gahb', jnp.eye(n, dtype=blocks.dtype), blocks).reshape(n * a, n * b)


def kernel(x, mem, g_mix, w_in, pool_w, pool_scale, gla_w_lr, gla_b_lr, gla_g_norm, gdn_conv, gdn_a_log,
           gdn_dt_bias, gdn_g_norm, nsa_pe, nsa_cmp_w1, nsa_cmp_w2, w_branch, w_gate, b_gate, w_out, g_cross,
           g_mem, w_xq, w_mem_kv, w_xo, g_ffn, w_up, ffn_conv, ffn_conv_b, w_down, g_final):
    b, s, d = x.shape
    depth = w_in.shape[0]
    t = b * s
    cols = _inproj_columns()
    covt, gexp = _nsa_constants(s)
    eba = jnp.asarray(np.concatenate([_head_expand(MISC_B, GDN_HEADS, GDN_DH),
                                      _head_expand(MISC_A, GDN_HEADS, GDN_DH)], axis=1), dtype=BF16)
    ones_bd = _block_diag(jnp.ones((GDN_HEADS, GDN_DH, GDN_DH), BF16))
    nsa_rows = np.arange(MIX_W).reshape(NSA_GROUPS, NSA_HPG, NSA_DH).transpose(1, 0, 2).reshape(-1)
    row = lambda v: v.reshape(1, -1).astype(F32)
    misc_a = lambda v: jnp.zeros((1, ZM_W), F32).at[0, MISC_A:MISC_A + GDN_HEADS].set(v)

    x2 = x.reshape(t, d)
    for l in range(depth):
        w_in_r = jnp.concatenate([w_in[l], jnp.zeros((d, 1), F32)], axis=1)[:, cols].astype(BF16)
        wlr = jnp.zeros((ZM_W, GLA_HEADS * GLA_DK), F32).at[MISC_LR:MISC_LR + GLA_LOWRANK].set(gla_w_lr[l])
        wlr2 = wlr.astype(BF16)
        o_pool, zn, zc, zm, o_gla, o_gdn = _mixers(
            x2.reshape(b, s, d), row(g_mix[l]), w_in_r, _block_diag(pool_w[l]).astype(BF16), row(pool_scale[l]),
            wlr2, row(gla_b_lr[l]), row(jnp.tile(gla_g_norm[l], GLA_HEADS)),
            gdn_conv[l], eba, misc_a(gdn_a_log[l]), misc_a(gdn_dt_bias[l]),
            row(jnp.tile(gdn_g_norm[l], GDN_HEADS)), ones_bd)

        pe_x = jnp.tile(nsa_pe[l], (1, 1, NSA_GROUPS))
        w1 = nsa_cmp_w1[l].reshape(2, NSA_CMP_LEN, NSA_DH, NSA_DH)
        eye_g = jnp.eye(NSA_GROUPS, dtype=F32)
        w1_bd = jnp.einsum('gh,kpde->kpgdhe', eye_g, w1).reshape(2, NSA_CMP_LEN, LANE, LANE).astype(BF16)
        w2_bd = jnp.einsum('gh,kde->kgdhe', eye_g, nsa_cmp_w2[l]).reshape(2, LANE, LANE).astype(BF16)
        ck, cv = _nsa_compress(zc, pe_x, w1_bd, w2_bd)
        o_nsa = _nsa(zn, zm, ck, cv, covt, gexp)

        wb = jnp.concatenate([w_branch[l, :3], w_branch[l, 3][nsa_rows][None]], axis=0).astype(BF16)
        branches = [o.reshape(t, MIX_W) for o in (o_pool, o_gla, o_gdn, o_nsa)]
        x2 = _combine(x2, row(g_mix[l]), branches, w_gate[l].astype(BF16),
                      b_gate[l].reshape(N_BRANCH, 1, d), wb, w_out[l].astype(BF16))

        x3 = _cross(x2.reshape(b, s, d), row(g_cross[l]), w_xq[l].astype(BF16), mem, row(g_mem[l]),
                    w_mem_kv[l].astype(BF16), w_xo[l].astype(BF16))

        x3 = _ffn(x3, row(g_ffn[l]), w_up[l].astype(BF16), ffn_conv[l], row(ffn_conv_b[l]),
                  w_down[l].astype(BF16), row(g_final), final=(l == depth - 1))
        x2 = x3.reshape(t, d)
    return x2.reshape(b, s, d)
```

```python
import functools

import numpy as np
import jax
import jax.numpy as jnp
from jax import lax
from jax.experimental import pallas as pl
from jax.experimental.pallas import tpu as pltpu

F32 = jnp.float32
BF16 = jnp.bfloat16

D_MODEL = 1024
MIX_W = 256
POOL_WINDOWS = (2, 4, 8, 16)
POOL_GW = 64
GLA_HEADS = 4
GLA_DK = 32
GLA_DV = 64
GLA_LOWRANK = 16
GLA_GATE_NORM = 16.0
CHUNK = 64
GDN_HEADS = 4
GDN_DH = 64
GDN_CONV = 4
NSA_HEADS = 4
NSA_GROUPS = 2
NSA_HPG = 2
NSA_DH = 64
NSA_KV = 128
NSA_CMP_LEN = 32
NSA_CMP_STRIDE = 16
NSA_SEL_LEN = 64
NSA_N_SEL = 16
NSA_WINDOW = 512
X_HEADS = 4
X_DH = 128
D_FF = 2816
FFN_CONV = 3
EPS = 1e-6
N_BRANCH = 4

IN_SPLITS = (MIX_W,
             128, 128, 256, 256, GLA_LOWRANK,
             MIX_W, MIX_W, MIX_W, GDN_HEADS, GDN_HEADS, MIX_W,
             256, NSA_KV, NSA_KV, NSA_KV, NSA_KV, NSA_KV, NSA_KV, 3 * NSA_HEADS)
N_IN = sum(IN_SPLITS)

MISC_LR = 0
MISC_B = 16
MISC_A = 20
MISC_G = 24
LANE = 128
SUBLANE = 8

Z_WIDTHS = (256, 768, 1024, 768, 256, 128)
GDN_QKV_W = 3 * MIX_W
ZP_W, ZG_W, ZD_W, ZN_W, ZC_W, ZM_W = Z_WIDTHS
Z_W = sum(Z_WIDTHS)
NEG_BIG = -1e30
F32_LOWEST = float(np.finfo(np.float32).min)

TM_COMB = 512
TM_CROSS = 1024
CROSS_SUB = 512
TS_FFN = 1024
FF_CHUNK = 256
TQ_NSA = 256
NSA_SEQS = 2
NSA_TILE_UNROLL = 3
REC_SEQS = 16
REC_LAG = 3
GLA_START = 6
N_SLC_PAD = 128
N_CMP_PAD = 128

VMEM_LIMIT = 56 * 1024 * 1024


def _cparams(sem):
    return pltpu.CompilerParams(dimension_semantics=sem, vmem_limit_bytes=VMEM_LIMIT)


def _rms(x, g):
    return x * lax.rsqrt(jnp.mean(x * x, axis=-1, keepdims=True) + EPS) * g


def _dot(a, b):
    return jnp.dot(a.astype(BF16), b.astype(BF16), preferred_element_type=F32)


def _dot_nt(a, b):
    return lax.dot_general(a.astype(BF16), b.astype(BF16), (((1,), (1,)), ((), ())),
                           preferred_element_type=F32)


def _dot_tn(a, b):
    return lax.dot_general(a.astype(BF16), b.astype(BF16), (((0,), (0,)), ((), ())),
                           preferred_element_type=F32)


def _split2_dot(a, b):
    hi = a.astype(BF16)
    lo = (a - hi.astype(F32)).astype(BF16)
    return jnp.dot(hi, b, preferred_element_type=F32) + jnp.dot(lo, b, preferred_element_type=F32)


def _split3_lhs_dot(a, b):
    hi = a.astype(BF16)
    r1 = a - hi.astype(F32)
    mid = r1.astype(BF16)
    lo = (r1 - mid.astype(F32)).astype(BF16)
    return ((jnp.dot(hi, b, preferred_element_type=F32) + jnp.dot(mid, b, preferred_element_type=F32))
            + jnp.dot(lo, b, preferred_element_type=F32))


def _iota(shape, axis):
    return lax.broadcasted_iota(jnp.int32, shape, axis)


def _block_mask(rows, cols, rb, cb):
    return (_iota((rows, cols), 0) // rb) == (_iota((rows, cols), 1) // cb)


def _shift_rows(x, k):
    t = _iota(x.shape, 0)
    return jnp.where(t >= k, pltpu.roll(x, k, 0), 0.0)


def _cumsum_rows(x):
    k = 1
    while k < x.shape[0]:
        x = x + _shift_rows(x, k)
        k *= 2
    return x


def _softplus(x):
    return jnp.maximum(x, 0.0) + jnp.log1p(jnp.exp(-jnp.abs(x)))


def _log_sigmoid(x):
    return -_softplus(-x)


def _silu(x):
    return x * jax.nn.sigmoid(x)


def _masked_softmax(s, mask):
    s = jnp.where(mask, s, -jnp.inf)
    m = jnp.maximum(jnp.max(s, axis=-1, keepdims=True), F32_LOWEST)
    e = jnp.exp(s - m)
    den = jnp.sum(e, axis=-1, keepdims=True)
    return e / jnp.where(den > 0, den, 1.0)


def _tile4(x):
    return jnp.concatenate([x, x, x, x], axis=0)


POOL_HALO = 16


def _gla_masks():
    return (_block_mask(4 * CHUNK, 128, CHUNK, GLA_DK),
            _block_mask(4 * CHUNK, 256, CHUNK, GLA_DV),
            _block_mask(256, 128, GLA_DV, GLA_DK),
            (_iota((CHUNK, 256), 1) % CHUNK) <= _iota((CHUNK, 256), 0))


def _seqs(f, *lists):
    return [f(*args) for args in zip(*lists)]


def _run_staggered(stage_gens, starts, periods):
    live = list(range(len(stage_gens)))
    tick = 0
    while live:
        for i in list(live):
            if tick >= starts[i] and (tick - starts[i]) % periods[i] == 0:
                try:
                    next(stage_gens[i])
                except StopIteration:
                    live.remove(i)
        tick += 1


def _gla_chunk(masks, zqk, v, r, zm, wlr_ref, blr_ref, gn_ref, ones_ref, st_refs, out):
    mask_k, mask_v, mask_st, causal = masks
    q = [z[:, 0:128] * GLA_DK ** -0.5 for z in zqk]
    k = [z[:, 128:256] for z in zqk]
    pre = _seqs(lambda a: _dot(a, wlr_ref[...]), zm)
    yield
    bc = _seqs(lambda a: _cumsum_rows(_log_sigmoid(a + blr_ref[...]) / GLA_GATE_NORM), pre)
    bl = [a[CHUNK - 1:CHUNK, :] for a in bc]
    q_e = _seqs(lambda a, c: a * jnp.exp(c), q, bc)
    k_e = _seqs(lambda a, c: a * jnp.exp(-c), k, bc)
    k_u = _seqs(lambda a, c, l: a * jnp.exp(l - c), k, bc, bl)
    st = [ref[...] for ref in st_refs]
    yield
    att = _seqs(lambda a, b: jnp.where(causal, _dot_nt(a, jnp.where(mask_k, _tile4(b), 0.0)), 0.0), q_e, k_e)
    inter = _seqs(_dot_nt, q_e, st)
    kv = _seqs(_dot_tn, v, k_u)
    yield
    o = _seqs(lambda a, b, c: _dot(a, _tile4(b.astype(BF16)) * ones_ref[...]) + c, att, v, inter)
    for ref, s_old, l, upd in zip(st_refs, st, bl, kv):
        ref[...] = s_old * jnp.exp(l) + jnp.where(mask_st, upd, 0.0)
    yield
    ms = _seqs(lambda a: _dot(a * a, ones_ref[...]) * (1.0 / GLA_DV), o)
    yield
    out[:] = _seqs(lambda a, m, g: a * lax.rsqrt(m + EPS) * gn_ref[...] * _silu(g), o, ms, r)


GDN_HALO = SUBLANE


def _gdn_masks():
    c4 = 4 * CHUNK
    col = _iota((CHUNK, c4), 1) % CHUNK
    row = _iota((CHUNK, c4), 0)
    return (_block_mask(c4, c4, CHUNK, CHUNK), col <= row, col < row, col == row)


def _gdn_chunk(masks, halo, zqkv, gate, zm, cw_ref, eba_ref, alog_ref, dtb_ref, gn_ref, ones_ref, st_refs, out):
    mask_bd, incl, strict, diag = masks
    cw = cw_ref[...]
    bd = lambda a: _tile4(a.astype(BF16)) * ones_ref[...]

    def conv_silu(h, cur):
        ext = jnp.concatenate([h, cur], axis=0)
        conv = (cw[3:4] * cur + cw[2:3] * pltpu.roll(ext, 1, 0)[GDN_HALO:]
                + cw[1:2] * pltpu.roll(ext, 2, 0)[GDN_HALO:] + cw[0:1] * pltpu.roll(ext, 3, 0)[GDN_HALO:])
        return _silu(conv)

    qkv = _seqs(conv_silu, halo, zqkv)
    yield
    v = [a[:, 512:768] for a in qkv]
    ssq = _seqs(lambda a: _dot(jnp.concatenate([a[:, 0:256] * a[:, 0:256], a[:, 256:512] * a[:, 256:512]],
                                               axis=0), ones_ref[...]), qkv)
    yield
    q = _seqs(lambda a, s: a[:, 0:256] * lax.rsqrt(s[0:CHUNK] + EPS) * GDN_DH ** -0.5, qkv, ssq)
    k = _seqs(lambda a, s: a[:, 256:512] * lax.rsqrt(s[CHUNK:2 * CHUNK] + EPS), qkv, ssq)
    beta = _seqs(lambda a: _dot(jax.nn.sigmoid(a), eba_ref[:, 0:MIX_W]), zm)
    yield
    gc = _seqs(lambda a: _split3_lhs_dot(_cumsum_rows(-jnp.exp(alog_ref[...]) * _softplus(a + dtb_ref[...])),
                                         eba_ref[:, MIX_W:2 * MIX_W]), zm)
    yield
    gl = [a[CHUNK - 1:CHUNK, :] for a in gc]
    g_row = _seqs(lambda a: jnp.sum(jnp.where(diag, a, 0.0), axis=0, keepdims=True), gc)
    yield
    decay =_seqs(lambda a, b: jnp.exp(jnp.where(incl, a - b, -jnp.inf)), gc, g_row)
    kb = _seqs(lambda a, b: a * b, k, beta)
    vb = _seqs(lambda a, b: a * b, v, beta)
    kq = _seqs(lambda a, b, c: _dot_nt(jnp.concatenate([a, b], axis=0), bd(c)), kb, q, k)
    yield
    n_mat =_seqs(lambda s, d: jnp.where(strict, s[0:CHUNK] * d, 0.0), kq, decay)
    a_qk = _seqs(lambda s, d: s[CHUNK:2 * CHUNK] * d, kq, decay)
    m = _seqs(lambda a: -a, n_mat)
    x = _seqs(lambda a: jnp.where(diag, 1.0, 0.0) + a, m)
    m = _seqs(lambda a: _dot(a, bd(a)), m)
    yield
    p = 2
    while p < CHUNK // 2:
        xm = _seqs(lambda a, b: _dot(jnp.concatenate([a, b], axis=0), bd(b)), x, m)
        x = _seqs(lambda a, r: a + r[0:CHUNK], x, xm)
        m = [r[CHUNK:2 * CHUNK] for r in xm]
        p *= 2
        yield
    x = _seqs(lambda a, b: a + _dot(a, bd(b)), x, m)
    yield
    uw = _seqs(lambda a, b, c, g: _dot(a, jnp.concatenate([bd(b), bd(c * jnp.exp(g))], axis=1)),
               x, vb, kb, gc)
    yield
    st = [ref[...] for ref in st_refs]
    ws = _seqs(lambda a, b, g, s: _dot(jnp.concatenate([a[:, 256:512], b * jnp.exp(g)], axis=0), s),
               uw, q, gc, st)
    yield
    v_new = _seqs(lambda a, r: a[:, 0:256] - r[0:CHUNK], uw, ws)
    o = _seqs(lambda r, qk, vn: r[CHUNK:2 * CHUNK] + _dot(qk, bd(vn)), ws, a_qk, v_new)
    upd = _seqs(lambda a, g, l, vn: _dot_tn(a * jnp.exp(l - g), vn), k, gc, gl, v_new)
    yield
    for ref, s_old, l, u in zip(st_refs, st, gl, upd):
        ref[...] = s_old * jnp.exp(l) + jnp.where(mask_bd, u, 0.0)
    ms = _seqs(lambda a: _dot(a * a, ones_ref[...]) * (1.0 / GDN_DH), o)
    yield
    out[:] = _seqs(lambda a, m_, g: a * lax.rsqrt(m_ + EPS) * gn_ref[...] * _silu(g), o, ms, gate)


def _pool_diff(u, tail, pos0):
    ext = jnp.concatenate([tail, u], axis=0)
    s2 = ext + pltpu.roll(ext, 1, 0)
    s4 = s2 + pltpu.roll(s2, 2, 0)
    s8 = s4 + pltpu.roll(s4, 4, 0)
    s16 = s8 + pltpu.roll(s8, 8, 0)
    grp = _iota(u.shape, 1) // POOL_GW
    win = jnp.where(grp == 0, s2[POOL_HALO:], jnp.where(grp == 1, s4[POOL_HALO:],
                                                        jnp.where(grp == 2, s8[POOL_HALO:], s16[POOL_HALO:])))
    width = jnp.where(grp == 0, POOL_WINDOWS[0],
                      jnp.where(grp == 1, POOL_WINDOWS[1],
                                jnp.where(grp == 2, POOL_WINDOWS[2], POOL_WINDOWS[3])))
    cnt = jnp.minimum(pos0 + _iota(u.shape, 0) + 1, width).astype(F32)
    return win / cnt - u


def _mix_kernel(x_ref, g_ref, w_ref, pw_ref, ps_ref, wlr_ref, blr_ref, gng_ref, cw_ref, eba_ref, alog_ref, dtb_ref,
                gnd_ref, ones_ref, op_ref, zn_ref, zc_ref, zm_ref, og_ref, od_ref, ptail_ref, halo_ref, *st_refs):
    nb = x_ref.shape[0]
    chunk = pl.program_id(1)

    @pl.when(chunk == 0)
    def _():
        ptail_ref[...] = jnp.zeros_like(ptail_ref)
        halo_ref[...] = jnp.zeros_like(halo_ref)
        for st_ref in st_refs:
            st_ref[...] = jnp.zeros_like(st_ref)

    h = _rms(x_ref[...].reshape(nb * CHUNK, D_MODEL), g_ref[...]).astype(BF16)
    proj = lambda lo, n: jnp.dot(h, w_ref[:, lo:lo + n], preferred_element_type=F32)
    seqs = lambda a: [a[i * CHUNK:(i + 1) * CHUNK] for i in range(nb)]
    off_g, off_d = ZP_W, ZP_W + ZG_W
    off_n = off_d + ZD_W
    off_c = off_n + ZN_W
    off_m = off_c + ZC_W
    zqkv = seqs(proj(off_d, GDN_QKV_W))
    zm = seqs(proj(off_m, ZM_W))
    halo = [halo_ref[i] for i in range(nb)]
    for i in range(nb):
        halo_ref[i] = zqkv[i][CHUNK - GDN_HALO:, :]
        zm_ref[i] = zm[i]
    halves = [slice(0, nb // 2), slice(nb // 2, nb)] if nb > 1 else [slice(0, nb)]
    gla_qk, gla_v, gla_r = [], [], []
    gate_parts = [[] for _ in halves]

    def other_projections():
        gla_qk.extend(seqs(proj(off_g, 256)))
        yield
        gla_v.extend(seqs(proj(off_g + 256, 256)))
        yield
        u = seqs(proj(0, ZP_W))
        yield
        diff = [_pool_diff(u[i], ptail_ref[i], chunk * CHUNK) for i in range(nb)]
        for i in range(nb):
            ptail_ref[i] = u[i][CHUNK - POOL_HALO:, :]
        yield
        for lo in range(0, ZN_W, 256):
            zn = seqs(proj(off_n + lo, 256))
            for i in range(nb):
                zn_ref[i, :, lo:lo + 256] = zn[i].astype(zn_ref.dtype)
            yield
        zc = seqs(proj(off_c, ZC_W))
        for i in range(nb):
            zc_ref[i] = zc[i]
        yield
        gla_r.extend(seqs(proj(off_g + 512, 256)))
        yield
        gate = seqs(proj(off_d + GDN_QKV_W, MIX_W))
        for hs, part in zip(halves, gate_parts):
            part.extend(gate[hs])
        yield
        o_pool = seqs(_dot(jnp.concatenate(diff, axis=0), pw_ref[...]) * ps_ref[...])
        for i in range(nb):
            op_ref[i] = o_pool[i].astype(op_ref.dtype)

    og = [None] * nb
    od_parts = [[None] * (hs.stop - hs.start) for hs in halves]
    gdn_masks = _gdn_masks()
    gens = [_gdn_chunk(gdn_masks, halo[hs], zqkv[hs], gate_part, zm[hs], cw_ref, eba_ref, alog_ref, dtb_ref,
                       gnd_ref, ones_ref, st_refs[nb:][hs], part)
            for hs, part, gate_part in zip(halves, od_parts, gate_parts)]
    gens.append(_gla_chunk(_gla_masks(), gla_qk, gla_v, gla_r, zm, wlr_ref, blr_ref, gng_ref, ones_ref,
                           st_refs[:nb], og))
    starts = [REC_LAG * i for i in range(len(halves))] + [GLA_START]
    gens.append(other_projections())
    _run_staggered(gens, starts + [0], [1] * len(gens))
    od = [o for part in od_parts for o in part]
    for i in range(nb):
        og_ref[i] = og[i].astype(og_ref.dtype)
        od_ref[i] = od[i].astype(od_ref.dtype)


def _mixers(x3, g, w, pool_w_bd, pool_scale, wlr2, blr, gn_gla, cw, eba, alog, dtb, gn_gdn, ones_bd):
    b, s, _ = x3.shape
    nb = REC_SEQS if b % REC_SEQS == 0 else 1
    const = lambda shape: pl.BlockSpec(shape, lambda i, c: (0,) * len(shape))
    tok = lambda n: pl.BlockSpec((nb, CHUNK, n), lambda i, c: (i, c, 0))
    outs = ((MIX_W, BF16), (ZN_W, BF16), (ZC_W, F32), (ZM_W, F32), (MIX_W, BF16), (MIX_W, BF16))
    return pl.pallas_call(
        _mix_kernel,
        grid=(b // nb, s // CHUNK),
        in_specs=[tok(D_MODEL), const((1, D_MODEL)), const((D_MODEL, Z_W)), const((MIX_W, MIX_W)), const((1, MIX_W)),
                  const((ZM_W, 128)), const((1, 128)), const((1, MIX_W)),
                  const((GDN_CONV, GDN_QKV_W)), const((ZM_W, 2 * MIX_W)), const((1, ZM_W)), const((1, ZM_W)),
                  const((1, MIX_W)), const((MIX_W, MIX_W))],
        out_specs=[tok(n) for n, _ in outs],
        out_shape=[jax.ShapeDtypeStruct((b, s, n), dt) for n, dt in outs],
        scratch_shapes=([pltpu.VMEM((nb, POOL_HALO, MIX_W), F32), pltpu.VMEM((nb, GDN_HALO, GDN_QKV_W), F32)]
                        + [pltpu.VMEM((256, 128), F32)] * nb + [pltpu.VMEM((256, 256), F32)] * nb),
        compiler_params=_cparams(("parallel", "arbitrary")),
        name="inproj_mixers",
    )(x3, g, w, pool_w_bd, pool_scale, wlr2, blr, gn_gla, cw, eba, alog, dtb, gn_gdn, ones_bd)


def _cmp_kernel(kc_ref, vc_ref, pe_ref, w1_ref, w2_ref, ck_ref, cv_ref):
    n_sub = NSA_CMP_LEN // NSA_CMP_STRIDE
    n_chunks = kc_ref.shape[1] // NSA_CMP_STRIDE
    for which, (src_ref, out_ref) in enumerate(((kc_ref, ck_ref), (vc_ref, cv_ref))):
        parts = [jnp.zeros((n_chunks, LANE), F32) for _ in range(n_sub)]
        for i in range(NSA_CMP_STRIDE):
            slab = src_ref[0, pl.ds(i, n_chunks, stride=NSA_CMP_STRIDE), :]
            for sub in range(n_sub):
                p = sub * NSA_CMP_STRIDE + i
                parts[sub] = parts[sub] + _dot(slab + pe_ref[which, p:p + 1, :], w1_ref[which, p])
        pre = parts[0] + pltpu.roll(parts[1], n_chunks - 1, 0)
        out_ref[0] = _dot(jax.nn.gelu(pre), w2_ref[which])


def _nsa_compress(zc, pe_x, w1_bd, w2_bd):
    b, s, _ = zc.shape
    n_chunks = s // NSA_CMP_STRIDE
    return pl.pallas_call(
        _cmp_kernel,
        grid=(b,),
        in_specs=[pl.BlockSpec((1, s, LANE), lambda i: (i, 0, 0)),
                  pl.BlockSpec((1, s, LANE), lambda i: (i, 0, 1)),
                  pl.BlockSpec((2, NSA_CMP_LEN, LANE), lambda i: (0, 0, 0)),
                  pl.BlockSpec((2, NSA_CMP_LEN, LANE, LANE), lambda i: (0, 0, 0, 0)),
                  pl.BlockSpec((2, LANE, LANE), lambda i: (0, 0, 0))],
        out_specs=[pl.BlockSpec((1, n_chunks, LANE), lambda i: (i, 0, 0))] * 2,
        out_shape=[jax.ShapeDtypeStruct((b, n_chunks, LANE), F32)] * 2,
        compiler_params=_cparams(("parallel",)),
        name="nsa_compress",
    )(zc, zc, pe_x, w1_bd, w2_bd)


def _topk_rank(v):
    n, t = v.shape
    blocks = [v[r:r + SUBLANE] for r in range(0, n, SUBLANE)]
    sub = _iota((SUBLANE, t), 0)
    rank = [jnp.zeros((SUBLANE, t), jnp.int32) for _ in blocks]
    for m in range(n):
        other = v[m:m + 1, :]
        mb, ms = divmod(m, SUBLANE)
        for r, blk in enumerate(blocks):
            if r < mb:
                ahead = (other > blk).astype(jnp.int32)
            elif r > mb:
                ahead = (other >= blk).astype(jnp.int32)
            else:
                ahead = jnp.where(sub > ms, (other >= blk).astype(jnp.int32), (other > blk).astype(jnp.int32))
            rank[r] = rank[r] + ahead
    return jnp.concatenate(rank, axis=0)


def _exp_weights(s, m):
    return jnp.exp((s - m).astype(BF16))


def _normalise_aug(o_aug, in_grp):
    den = pltpu.roll(o_aug, NSA_DH, 1)
    return jnp.where(in_grp, o_aug / jnp.where(den > 0, den, 1.0), 0.0)


def _nsa_kernel(q_ref, ksv_ref, kwv_ref, ck_ref, cv_ref, zm_ref, covt_ref, gexp_ref, o_ref,
                kaug_ref, vaug_ref, vwaug_ref, s_ref, mx_ref, acc_ref):
    nb = q_ref.shape[0]
    tq = q_ref.shape[1]
    s_len = ksv_ref.shape[1]
    n_slc = s_len // NSA_SEL_LEN
    qi = pl.program_id(1)
    s0 = pl.multiple_of(qi * tq, tq)
    lane_grp = _iota((1, LANE), 1) // NSA_DH
    chains = [(bi, g) for bi in range(nb) for g in range(NSA_GROUPS)]
    n_chains = range(len(chains))

    @pl.when(qi == 0)
    def _():
        lane = _iota((s_len, LANE), 1)
        key_blk = _iota((s_len, LANE), 0) // NSA_SEL_LEN
        for bi in range(nb):
            k = ksv_ref[bi, :, 0:LANE].astype(F32)
            v = ksv_ref[bi, :, LANE:2 * LANE].astype(F32)
            vw = kwv_ref[bi, :, LANE:2 * LANE].astype(F32)
            for g in range(NSA_GROUPS):
                own = lane // NSA_DH == g
                onehot = jnp.where(lane - (1 - g) * NSA_DH == key_blk, 1.0, 0.0)
                kaug_ref[bi * NSA_GROUPS + g] = jnp.where(own, k, onehot).astype(BF16)
                vaug_ref[bi * NSA_GROUPS + g] = jnp.where(own, v, 1.0).astype(BF16)
                vwaug_ref[bi * NSA_GROUPS + g] = jnp.where(own, vw, 1.0).astype(BF16)

    t_col = s0 + _iota((tq, 1), 0)
    t_col2 = jnp.concatenate([t_col, t_col], axis=0)
    t_row = s0 + _iota((1, tq), 1)
    q = [q_ref[bi].astype(F32) * NSA_DH ** -0.5 for bi in range(nb)]
    ck_hi = [ck_ref[bi].astype(BF16) for bi in range(nb)]
    ck_lo = [(ck_ref[bi] - ck_hi[bi].astype(F32)).astype(BF16) for bi in range(nb)]
    cv = [cv_ref[bi] for bi in range(nb)]
    cmp_end = _iota((1, N_CMP_PAD), 1) * NSA_CMP_STRIDE + (NSA_CMP_LEN - 1)
    cmp_valid = cmp_end <= t_col2
    blk_t = _iota((n_slc, tq), 0)
    cur_t = t_row // NSA_SEL_LEN
    forced_t = (blk_t == 0) | (blk_t == cur_t) | (blk_t == cur_t - 1)
    future_t = blk_t > cur_t
    row_in_tile = jnp.concatenate([_iota((tq, tq), 0)] * NSA_HPG, axis=0)
    col_in_tile = _iota((NSA_HPG * tq, tq), 1)
    diag_ok = col_in_tile <= row_in_tile
    n_wt = NSA_WINDOW // tq + 1
    never = 2 * tq
    win_off = [pl.multiple_of(jnp.maximum(qi - (n_wt - 1 - w), 0) * tq, tq) for w in range(n_wt)]
    win_ok = [col_in_tile > row_in_tile + jnp.where(qi >= n_wt - 1, 0, never)]
    win_ok += [col_in_tile >= jnp.where(qi >= n_wt - 1 - w, 0, never) for w in range(1, n_wt - 1)]
    win_ok += [diag_ok]

    o_cmp = [[jnp.zeros((tq, LANE), F32) for _ in range(NSA_HPG)] for _ in range(nb)]
    o_slc = [[jnp.zeros((tq, LANE), F32) for _ in range(NSA_HPG)] for _ in range(nb)]
    o_win = [[jnp.zeros((tq, LANE), F32) for _ in range(NSA_HPG)] for _ in range(nb)]
    in_grp = [lane_grp == g for _, g in chains]
    q2 = [jnp.concatenate([jnp.where(in_grp[c], q[bi][:, j * LANE:(j + 1) * LANE], 0.0)
                           for j in range(NSA_HPG)], axis=0) for c, (bi, _) in enumerate(chains)]
    qb = [a.astype(BF16) for a in q2]
    p_c = [_masked_softmax(_dot_nt(qb[c], ck_hi[bi]) + _dot_nt(qb[c], ck_lo[bi]), cmp_valid)
           for c, (bi, _) in enumerate(chains)]
    oc = [jnp.where(in_grp[c], _dot(p_c[c], cv[bi]), 0.0) for c, (bi, _) in enumerate(chains)]
    imp_c = [a[0:tq] + a[tq:2 * tq] for a in p_c]
    imp_hi = [a.astype(BF16) for a in imp_c]
    imp_lo = [(a - h.astype(F32)).astype(BF16) for a, h in zip(imp_c, imp_hi)]
    imp_t = [(_dot_nt(covt_ref[...], h) + _dot_nt(covt_ref[...], l_))[0:n_slc]
             for h, l_ in zip(imp_hi, imp_lo)]
    imp_t = [jnp.where(forced_t, jnp.inf, jnp.where(future_t, -jnp.inf, a)) for a in imp_t]
    s_w = [[jnp.where(ok, _dot_nt(qb[c], kwv_ref[bi, pl.ds(off, tq), 0:LANE]), -jnp.inf)
            for ok, off in zip(win_ok, win_off)] for c, (bi, _) in enumerate(chains)]
    qa = []
    for c, (_, g) in enumerate(chains):
        bias_t = jnp.where(_topk_rank(imp_t[c]) < NSA_N_SEL, 0.0, NEG_BIG)
        lo = (1 - g) * NSA_DH
        rows = ([jnp.zeros((lo, tq), F32)] if lo else []) + [bias_t, jnp.zeros((LANE - lo - n_slc, tq), F32)]
        bias = jnp.concatenate(rows, axis=0).T
        qa.append(jnp.where(in_grp[c], q2[c], jnp.concatenate([bias] * NSA_HPG, axis=0)).astype(BF16))
    m_w = []
    for c in n_chains:
        m_c = s_w[c][0]
        for s_t in s_w[c][1:]:
            m_c = jnp.maximum(m_c, s_t)
        m_w.append(jnp.maximum(jnp.max(m_c, axis=-1, keepdims=True), F32_LOWEST))
    ow = [jnp.zeros((NSA_HPG * tq, LANE), F32) for _ in n_chains]
    for w, off in enumerate(win_off):
        for c in n_chains:
            ow[c] = ow[c] + jnp.dot(_exp_weights(s_w[c][w], m_w[c]), vwaug_ref[c, pl.ds(off, tq), :],
                                    preferred_element_type=F32)
    for c, (bi, _) in enumerate(chains):
        ow_c = _normalise_aug(ow[c], in_grp[c])
        for j in range(NSA_HPG):
            rows_j = slice(j * tq, (j + 1) * tq)
            o_cmp[bi][j] = o_cmp[bi][j] + oc[c][rows_j]
            o_win[bi][j] = o_win[bi][j] + ow_c[rows_j]

    half_max = lambda s_t: jnp.maximum(s_t[:, 0:LANE], s_t[:, LANE:2 * LANE])
    mx_ref[...] = jnp.full(mx_ref.shape, NEG_BIG, F32)
    groups = n_chains

    def score_tiles(tiles):
        offs = [pl.multiple_of(kt * tq, tq) for kt in tiles]
        s_new = [[_dot_nt(qa[g], kaug_ref[g, pl.ds(off, tq), :]) for g in groups] for off in offs]
        for kt, s_kt in zip(tiles, s_new):
            for g in groups:
                s_ref[g, kt] = s_kt[g]
        for g in groups:
            m_new = half_max(s_new[0][g])
            for s_kt in s_new[1:]:
                m_new = jnp.maximum(m_new, half_max(s_kt[g]))
            mx_ref[g] = jnp.maximum(mx_ref[g], m_new)

    def run_tiles(fn, n_tiles):
        def body(p, carry):
            fn([NSA_TILE_UNROLL * p + k for k in range(NSA_TILE_UNROLL)])
            return carry

        lax.fori_loop(0, n_tiles // NSA_TILE_UNROLL, body, 0)
        left = n_tiles % NSA_TILE_UNROLL
        for r in range(1, NSA_TILE_UNROLL):
            @pl.when(left == r)
            def _():
                fn([n_tiles - r + k for k in range(r)])

    run_tiles(score_tiles, qi)

    m_s = []
    for g in groups:
        s_d = jnp.where(diag_ok, _dot_nt(qa[g], kaug_ref[g, pl.ds(s0, tq), :]), NEG_BIG)
        s_ref[g, qi] = s_d
        m_g = jnp.max(jnp.maximum(mx_ref[g], half_max(s_d)), axis=-1, keepdims=True)
        m_s.append(jnp.maximum(m_g, F32_LOWEST))
    acc_ref[...] = jnp.zeros(acc_ref.shape, F32)

    def attend_tiles(tiles):
        offs = [pl.multiple_of(kt * tq, tq) for kt in tiles]
        pv = [[jnp.dot(_exp_weights(s_ref[g, kt], m_s[g]), vaug_ref[g, pl.ds(off, tq), :],
                       preferred_element_type=F32) for g in groups] for kt, off in zip(tiles, offs)]
        for g in groups:
            upd = pv[0][g]
            for pv_kt in pv[1:]:
                upd = upd + pv_kt[g]
            acc_ref[g] += upd

    run_tiles(attend_tiles, qi + 1)

    for c, (bi, _) in enumerate(chains):
        os_ = _normalise_aug(acc_ref[c], in_grp[c])
        for j in range(NSA_HPG):
            o_slc[bi][j] = o_slc[bi][j] + os_[j * tq:(j + 1) * tq]

    for bi in range(nb):
        gates = _split2_dot(jax.nn.sigmoid(zm_ref[bi]), gexp_ref[...])
        o = (gates[:, 0:MIX_W] * jnp.concatenate(o_cmp[bi], axis=1)
             + gates[:, MIX_W:2 * MIX_W] * jnp.concatenate(o_slc[bi], axis=1)
             + gates[:, 2 * MIX_W:3 * MIX_W] * jnp.concatenate(o_win[bi], axis=1))
        o_ref[bi] = o.astype(o_ref.dtype)


def _nsa(zn, zm, ck, cv, covt, gexp):
    b, s, _ = zn.shape
    tq = TQ_NSA
    nb = NSA_SEQS if b % NSA_SEQS == 0 else 1
    n_chains = nb * NSA_GROUPS
    return pl.pallas_call(
        _nsa_kernel,
        grid=(b // nb, s // tq),
        in_specs=[pl.BlockSpec((nb, tq, 256), lambda i, j: (i, j, 0)),
                  pl.BlockSpec((nb, s, 256), lambda i, j: (i, 0, 1)),
                  pl.BlockSpec((nb, s, 256), lambda i, j: (i, 0, 2)),
                  pl.BlockSpec((nb, N_CMP_PAD, LANE), lambda i, j: (i, 0, 0)),
                  pl.BlockSpec((nb, N_CMP_PAD, LANE), lambda i, j: (i, 0, 0)),
                  pl.BlockSpec((nb, tq, ZM_W), lambda i, j: (i, j, 0)),
                  pl.BlockSpec((N_SLC_PAD, N_CMP_PAD), lambda i, j: (0, 0)),
                  pl.BlockSpec((ZM_W, 3 * MIX_W), lambda i, j: (0, 0))],
        out_specs=pl.BlockSpec((nb, tq, MIX_W), lambda i, j: (i, j, 0)),
        out_shape=jax.ShapeDtypeStruct((b, s, MIX_W), BF16),
        scratch_shapes=[pltpu.VMEM((n_chains, s, LANE), BF16)] * 3
                       + [pltpu.VMEM((n_chains, s // tq, NSA_HPG * tq, tq), F32),
                          pltpu.VMEM((n_chains, NSA_HPG * tq, LANE), F32),
                          pltpu.VMEM((n_chains, NSA_HPG * tq, LANE), F32)],
        compiler_params=_cparams(("parallel", "arbitrary")),
        name="nsa_attn",
    )(zn, zn, zn, ck, cv, zm, covt, gexp)


def _combine_kernel(x_ref, g_ref, op_ref, oa_ref, od_ref, on_ref, wg_ref, bg_ref, wb_ref, wo_ref, o_ref):
    x = x_ref[...]
    h = _rms(x, g_ref[...]).astype(BF16)
    y = jnp.zeros(x.shape, F32)
    for i, br_ref in enumerate((op_ref, oa_ref, od_ref, on_ref)):
        gate = jax.nn.sigmoid(jnp.dot(h, wg_ref[i], preferred_element_type=F32) + bg_ref[i])
        y = y + gate * jnp.dot(br_ref[...], wb_ref[i], preferred_element_type=F32)
    o_ref[...] = x + _dot(y, wo_ref[...])


def _combine(x2, g, branches, wg, bg, wb, wo):
    t = x2.shape[0]
    tm = TM_COMB
    return pl.pallas_call(
        _combine_kernel,
        grid=(t // tm,),
        in_specs=[pl.BlockSpec((tm, D_MODEL), lambda i: (i, 0)),
                  pl.BlockSpec((1, D_MODEL), lambda i: (0, 0))]
                 + [pl.BlockSpec((tm, MIX_W), lambda i: (i, 0))] * N_BRANCH
                 + [pl.BlockSpec((N_BRANCH, D_MODEL, D_MODEL), lambda i: (0, 0, 0)),
                    pl.BlockSpec((N_BRANCH, 1, D_MODEL), lambda i: (0, 0, 0)),
                    pl.BlockSpec((N_BRANCH, MIX_W, D_MODEL), lambda i: (0, 0, 0)),
                    pl.BlockSpec((D_MODEL, D_MODEL), lambda i: (0, 0))],
        out_specs=pl.BlockSpec((tm, D_MODEL), lambda i: (i, 0)),
        out_shape=jax.ShapeDtypeStruct((t, D_MODEL), F32),
        compiler_params=_cparams(("parallel",)),
        name="combine",
    )(x2, g, *branches, wg, bg, wb, wo)


def _cross_kernel(x_ref, g_ref, wq_ref, mem_ref, gm_ref, wkv_ref, wo_ref, o_ref, kv_ref):
    @pl.when(pl.program_id(1) == 0)
    def _():
        kv_ref[...] = _dot(_rms(mem_ref[0], gm_ref[...]), wkv_ref[...]).astype(kv_ref.dtype)

    n_k = X_HEADS * X_DH
    subs = [slice(i * CROSS_SUB, (i + 1) * CROSS_SUB) for i in range(x_ref.shape[1] // CROSS_SUB)]
    x = [x_ref[0, r, :] for r in subs]
    q = [_dot(_rms(a, g_ref[...]), wq_ref[...]) * X_DH ** -0.5 for a in x]
    pairs = [(i, h) for h in range(X_HEADS) for i in range(len(subs))]
    head = lambda h: slice(h * X_DH, (h + 1) * X_DH)
    sc = [_dot_nt(q[i][:, head(h)], kv_ref[:, head(h)]) for i, h in pairs]
    e = [jnp.exp(s - jnp.max(s, axis=-1, keepdims=True)) for s in sc]
    o = [_dot(e_ih, kv_ref[:, n_k + h * X_DH:n_k + (h + 1) * X_DH]) / jnp.sum(e_ih, axis=-1, keepdims=True)
         for e_ih, (i, h) in zip(e, pairs)]
    for i, r in enumerate(subs):
        o_i = jnp.concatenate([o[pairs.index((i, h))] for h in range(X_HEADS)], axis=1)
        o_ref[0, r, :] = x[i] + _dot(o_i, wo_ref[...])


def _cross(x3, g, wq, mem, g_mem, wkv, wo):
    b, s, _ = x3.shape
    tm = TM_CROSS
    m = mem.shape[1]
    n_k = X_HEADS * X_DH
    return pl.pallas_call(
        _cross_kernel,
        grid=(b, s // tm),
        in_specs=[pl.BlockSpec((1, tm, D_MODEL), lambda i, j: (i, j, 0)),
                  pl.BlockSpec((1, D_MODEL), lambda i, j: (0, 0)),
                  pl.BlockSpec((D_MODEL, n_k), lambda i, j: (0, 0)),
                  pl.BlockSpec((1, m, D_MODEL), lambda i, j: (i, 0, 0)),
                  pl.BlockSpec((1, D_MODEL), lambda i, j: (0, 0)),
                  pl.BlockSpec((D_MODEL, 2 * n_k), lambda i, j: (0, 0)),
                  pl.BlockSpec((n_k, D_MODEL), lambda i, j: (0, 0))],
        out_specs=pl.BlockSpec((1, tm, D_MODEL), lambda i, j: (i, j, 0)),
        out_shape=jax.ShapeDtypeStruct((b, s, D_MODEL), F32),
        scratch_shapes=[pltpu.VMEM((m, 2 * n_k), BF16)],
        compiler_params=_cparams(("parallel", "arbitrary")),
        name="cross_attn",
    )(x3, g, wq, mem, g_mem, wkv, wo)


def _ffn_kernel(x_ref, g_ref, wup_ref, cw_ref, cb_ref, wd_ref, gf_ref, o_ref, tail_ref, act_ref, *, final):
    @pl.when(pl.program_id(1) == 0)
    def _():
        tail_ref[...] = jnp.zeros_like(tail_ref)

    ts = x_ref.shape[1]
    x = x_ref[0]
    hn = _rms(x, g_ref[...]).astype(BF16)
    for c in range(D_FF // FF_CHUNK):
        cols = slice(c * FF_CHUNK, (c + 1) * FF_CHUNK)
        gcols = slice(D_FF + c * FF_CHUNK, D_FF + (c + 1) * FF_CHUNK)
        u = jnp.dot(hn, wup_ref[:, cols], preferred_element_type=F32)
        v = jnp.dot(hn, wup_ref[:, gcols], preferred_element_type=F32)
        ext = jnp.concatenate([tail_ref[:, cols], u], axis=0)
        tail_ref[:, cols] = u[ts - SUBLANE:, :]
        cw = cw_ref[:, cols]
        y = (cw[2:3] * u + cw[1:2] * pltpu.roll(ext, 1, 0)[SUBLANE:]
             + cw[0:1] * pltpu.roll(ext, 2, 0)[SUBLANE:] + cb_ref[:, cols])
        act_ref[:, cols] = (jax.nn.gelu(y) * v).astype(BF16)
    out = x + jnp.dot(act_ref[...], wd_ref[...], preferred_element_type=F32)
    if final:
        out = _rms(out, gf_ref[...])
    o_ref[0] = out


def _ffn(x3, g, wup, cw, cb, wd, gf, final):
    b, s, _ = x3.shape
    ts = TS_FFN
    return pl.pallas_call(
        functools.partial(_ffn_kernel, final=final),
        grid=(b, s // ts),
        in_specs=[pl.BlockSpec((1, ts, D_MODEL), lambda i, j: (i, j, 0)),
                  pl.BlockSpec((1, D_MODEL), lambda i, j: (0, 0)),
                  pl.BlockSpec((D_MODEL, 2 * D_FF), lambda i, j: (0, 0), pipeline_mode=pl.Buffered(1)),
                  pl.BlockSpec((FFN_CONV, D_FF), lambda i, j: (0, 0)),
                  pl.BlockSpec((1, D_FF), lambda i, j: (0, 0)),
                  pl.BlockSpec((D_FF, D_MODEL), lambda i, j: (0, 0), pipeline_mode=pl.Buffered(1)),
                  pl.BlockSpec((1, D_MODEL), lambda i, j: (0, 0))],
        out_specs=pl.BlockSpec((1, ts, D_MODEL), lambda i, j: (i, j, 0)),
        out_shape=jax.ShapeDtypeStruct((b, s, D_MODEL), F32),
        scratch_shapes=[pltpu.VMEM((SUBLANE, D_FF), F32), pltpu.VMEM((ts, D_FF), BF16)],
        compiler_params=_cparams(("parallel", "arbitrary")),
        name="conv_ffn",
    )(x3, g, wup, cw, cb, wd, gf)


def _cross_ffn_kernel(x_ref, gc_ref, wq_ref, mem_ref, gm_ref, wkv_ref, wo_ref, gff_ref, wup_ref, cw_ref, cb_ref,
                      wd_ref, gf_ref, o_ref, kv_ref, tail_ref, act_ref, *, final):
    @pl.when(pl.program_id(1) == 0)
    def _():
        kv_ref[...] = _dot(_rms(mem_ref[0], gm_ref[...]), wkv_ref[...]).astype(kv_ref.dtype)
        tail_ref[...] = jnp.zeros_like(tail_ref)

    n_k = X_HEADS * X_DH
    ts = x_ref.shape[1]
    subs = [slice(i * CROSS_SUB, (i + 1) * CROSS_SUB) for i in range(ts // CROSS_SUB)]
    xs = [x_ref[0, r, :] for r in subs]
    q = [_dot(_rms(a, gc_ref[...]), wq_ref[...]) * X_DH ** -0.5 for a in xs]
    pairs = [(i, h) for h in range(X_HEADS) for i in range(len(subs))]
    head = lambda h: slice(h * X_DH, (h + 1) * X_DH)
    sc = [_dot_nt(q[i][:, head(h)], kv_ref[:, head(h)]) for i, h in pairs]
    e = [jnp.exp(s - jnp.max(s, axis=-1, keepdims=True)) for s in sc]
    o = [_dot(e_ih, kv_ref[:, n_k + h * X_DH:n_k + (h + 1) * X_DH]) / jnp.sum(e_ih, axis=-1, keepdims=True)
         for e_ih, (i, h) in zip(e, pairs)]
    x = jnp.concatenate(
        [xs[i] + _dot(jnp.concatenate([o[pairs.index((i, h))] for h in range(X_HEADS)], axis=1), wo_ref[...])
         for i in range(len(subs))], axis=0)
    hn = _rms(x, gff_ref[...]).astype(BF16)
    for c in range(D_FF // FF_CHUNK):
        cols = slice(c * FF_CHUNK, (c + 1) * FF_CHUNK)
        gcols = slice(D_FF + c * FF_CHUNK, D_FF + (c + 1) * FF_CHUNK)
        u = jnp.dot(hn, wup_ref[:, cols], preferred_element_type=F32)
        v = jnp.dot(hn, wup_ref[:, gcols], preferred_element_type=F32)
        ext = jnp.concatenate([tail_ref[:, cols], u], axis=0)
        tail_ref[:, cols] = u[ts - SUBLANE:, :]
        cw = cw_ref[:, cols]
        y = (cw[2:3] * u + cw[1:2] * pltpu.roll(ext, 1, 0)[SUBLANE:]
             + cw[0:1] * pltpu.roll(ext, 2, 0)[SUBLANE:] + cb_ref[:, cols])
        act_ref[:, cols] = (jax.nn.gelu(y) * v).astype(BF16)
    out = x + jnp.dot(act_ref[...], wd_ref[...], preferred_element_type=F32)
    if final:
        out = _rms(out, gf_ref[...])
    o_ref[0] = out


def _cross_ffn(x3, g_cross, wq, mem, g_mem, wkv, wo, g_ffn, wup, cw, cb, wd, gf, final):
    b, s, _ = x3.shape
    ts = TS_FFN
    m = mem.shape[1]
    n_k = X_HEADS * X_DH
    const = lambda shape: pl.BlockSpec(shape, lambda i, j: (0,) * len(shape))
    once = lambda shape: pl.BlockSpec(shape, lambda i, j: (0,) * len(shape), pipeline_mode=pl.Buffered(1))
    return pl.pallas_call(
        functools.partial(_cross_ffn_kernel, final=final),
        grid=(b, s // ts),
        in_specs=[pl.BlockSpec((1, ts, D_MODEL), lambda i, j: (i, j, 0)),
                  const((1, D_MODEL)), once((D_MODEL, n_k)),
                  pl.BlockSpec((1, m, D_MODEL), lambda i, j: (i, 0, 0)),
                  const((1, D_MODEL)), once((D_MODEL, 2 * n_k)), once((n_k, D_MODEL)),
                  const((1, D_MODEL)), once((D_MODEL, 2 * D_FF)), const((FFN_CONV, D_FF)), const((1, D_FF)),
                  once((D_FF, D_MODEL)), const((1, D_MODEL))],
        out_specs=pl.BlockSpec((1, ts, D_MODEL), lambda i, j: (i, j, 0)),
        out_shape=jax.ShapeDtypeStruct((b, s, D_MODEL), F32),
        scratch_shapes=[pltpu.VMEM((m, 2 * n_k), BF16), pltpu.VMEM((SUBLANE, D_FF), F32),
                        pltpu.VMEM((ts, D_FF), BF16)],
        compiler_params=_cparams(("parallel", "arbitrary")),
        name="cross_ffn",
    )(x3, g_cross, wq, mem, g_mem, wkv, wo, g_ffn, wup, cw, cb, wd, gf)


def _inproj_columns():
    starts = np.concatenate([[0], np.cumsum(IN_SPLITS)])
    (p_in, a_q, a_k, a_v, a_r, a_lr, d_q, d_k, d_v, d_b, d_a, d_g,
     n_q, n_kc, n_vc, n_ks, n_vs, n_kw, n_vw, n_g) = [np.arange(starts[i], starts[i + 1])
                                                      for i in range(len(IN_SPLITS))]
    n_q = n_q.reshape(NSA_GROUPS, NSA_HPG, NSA_DH).transpose(1, 0, 2).reshape(-1)
    misc = np.full((ZM_W,), N_IN)
    misc[MISC_LR:MISC_LR + GLA_LOWRANK] = a_lr
    misc[MISC_B:MISC_B + GDN_HEADS] = d_b
    misc[MISC_A:MISC_A + GDN_HEADS] = d_a
    misc[MISC_G:MISC_G + 3 * NSA_HEADS] = n_g
    cols = np.concatenate([p_in, a_q, a_k, a_v, a_r, d_q, d_k, d_v, d_g,
                           n_q, n_ks, n_vs, n_kw, n_vw, n_kc, n_vc, misc])
    assert cols.shape[0] == Z_W
    return cols


def _head_expand(offset, n_heads, width):
    e = np.zeros((ZM_W, n_heads * width), np.float32)
    for h in range(n_heads):
        e[offset + h, h * width:(h + 1) * width] = 1.0
    return e


def _nsa_constants(s):
    n_cmp = s // NSA_CMP_STRIDE - NSA_CMP_LEN // NSA_CMP_STRIDE + 1
    n_slc = s // NSA_SEL_LEN
    c_start = np.arange(n_cmp) * NSA_CMP_STRIDE
    s_start = np.arange(n_slc) * NSA_SEL_LEN
    cover = np.zeros((N_CMP_PAD, N_SLC_PAD), np.float32)
    cover[:n_cmp, :n_slc] = ((c_start[:, None] <= s_start[None, :] + NSA_SEL_LEN - 1)
                             & (c_start[:, None] + NSA_CMP_LEN - 1 >= s_start[None, :]))
    gexp = np.zeros((ZM_W, 3, MIX_W), np.float32)
    for g in range(NSA_GROUPS):
        for j in range(NSA_HPG):
            slot = j * NSA_GROUPS + g
            for c in range(3):
                gexp[MISC_G + (g * NSA_HPG + j) * 3 + c, c, slot * NSA_DH:(slot + 1) * NSA_DH] = 1.0
    return jnp.asarray(cover.T, dtype=BF16), jnp.asarray(gexp.reshape(ZM_W, 3 * MIX_W), dtype=BF16)


def _block_diag(blocks):
    n, a, b = blocks.shape
    return jnp.einsum('gh,gab->gahb', jnp.eye(n, dtype=blocks.dtype), blocks).reshape(n * a, n * b)


def kernel(x, mem, g_mix, w_in, pool_w, pool_scale, gla_w_lr, gla_b_lr, gla_g_norm, gdn_conv, gdn_a_log,
           gdn_dt_bias, gdn_g_norm, nsa_pe, nsa_cmp_w1, nsa_cmp_w2, w_branch, w_gate, b_gate, w_out, g_cross,
           g_mem, w_xq, w_mem_kv, w_xo, g_ffn, w_up, ffn_conv, ffn_conv_b, w_down, g_final):
    b, s, d = x.shape
    depth = w_in.shape[0]
    t = b * s
    cols = _inproj_columns()
    covt, gexp = _nsa_constants(s)
    eba = jnp.asarray(np.concatenate([_head_expand(MISC_B, GDN_HEADS, GDN_DH),
                                      _head_expand(MISC_A, GDN_HEADS, GDN_DH)], axis=1), dtype=BF16)
    ones_bd = _block_diag(jnp.ones((GDN_HEADS, GDN_DH, GDN_DH), BF16))
    nsa_rows = np.arange(MIX_W).reshape(NSA_GROUPS, NSA_HPG, NSA_DH).transpose(1, 0, 2).reshape(-1)
    row = lambda v: v.reshape(1, -1).astype(F32)
    misc_a = lambda v: jnp.zeros((1, ZM_W), F32).at[0, MISC_A:MISC_A + GDN_HEADS].set(v)

    x2 = x.reshape(t, d)
    for l in range(depth):
        w_in_r = jnp.concatenate([w_in[l], jnp.zeros((d, 1), F32)], axis=1)[:, cols].astype(BF16)
        wlr = jnp.zeros((ZM_W, GLA_HEADS * GLA_DK), F32).at[MISC_LR:MISC_LR + GLA_LOWRANK].set(gla_w_lr[l])
        wlr2 = wlr.astype(BF16)
        o_pool, zn, zc, zm, o_gla, o_gdn = _mixers(
            x2.reshape(b, s, d), row(g_mix[l]), w_in_r, _block_diag(pool_w[l]).astype(BF16), row(pool_scale[l]),
            wlr2, row(gla_b_lr[l]), row(jnp.tile(gla_g_norm[l], GLA_HEADS)),
            gdn_conv[l], eba, misc_a(gdn_a_log[l]), misc_a(gdn_dt_bias[l]),
            row(jnp.tile(gdn_g_norm[l], GDN_HEADS)), ones_bd)

        pe_x = jnp.tile(nsa_pe[l], (1, 1, NSA_GROUPS))
        w1 = nsa_cmp_w1[l].reshape(2, NSA_CMP_LEN, NSA_DH, NSA_DH)
        eye_g = jnp.eye(NSA_GROUPS, dtype=F32)
        w1_bd = jnp.einsum('gh,kpde->kpgdhe', eye_g, w1).reshape(2, NSA_CMP_LEN, LANE, LANE).astype(BF16)
        w2_bd = jnp.einsum('gh,kde->kgdhe', eye_g, nsa_cmp_w2[l]).reshape(2, LANE, LANE).astype(BF16)
        ck, cv = _nsa_compress(zc, pe_x, w1_bd, w2_bd)
        o_nsa = _nsa(zn, zm, ck, cv, covt, gexp)

        wb = jnp.concatenate([w_branch[l, :3], w_branch[l, 3][nsa_rows][None]], axis=0).astype(BF16)
        branches = [o.reshape(t, MIX_W) for o in (o_pool, o_gla, o_gdn, o_nsa)]
        x2 = _combine(x2, row(g_mix[l]), branches, w_gate[l].astype(BF16),
                      b_gate[l].reshape(N_BRANCH, 1, d), wb, w_out[l].astype(BF16))

        x3 = _cross_ffn(x2.reshape(b, s, d), row(g_cross[l]), w_xq[l].astype(BF16), mem, row(g_mem[l]),
                        w_mem_kv[l].astype(BF16), w_xo[l].astype(BF16), row(g_ffn[l]), w_up[l].astype(BF16),
                        ffn_conv[l], row(ffn_conv_b[l]), w_down[l].astype(BF16), row(g_final),
                        final=(l == depth - 1))
        x2 = x3.reshape(t, d)
    return x2.reshape(b, s, d)
```
